```python
import math
import jax
import jax.numpy as jnp
from jax import lax
import numpy as np

D_MODEL = 1024
BATCH = 16
SEQ = 256
DEPTH = 2
DEC_BATCH = 8
DEC_SEQ = 1024
PAST_LEN = 512

GRID_W = 64
N_BRANCH = 4
BR_W = D_MODEL // 4
HD = 64
A_HEADS = BR_W // HD
A_QK = HD // 2
B_BLOCKS = 4
B_BLK = BR_W // B_BLOCKS
CONV_W = 4
CONV_LEFT = 1
LRU_C = 8.0
C_HEADS = BR_W // HD
NA_ROWS = 8
NA_COLS = 16
NA_QB = 16
NA_KB = NA_QB + NA_COLS
D_HEADS = BR_W // HD
D_KV = 2
D_GROUP = D_HEADS // D_KV
WIN = 128
QBLK = 128
ROPE_BASE = 10000.0
EPS = 1e-6
NEG = -1e30
IN_SIZES = (N_BRANCH * BR_W, 3 * A_HEADS * HD, BR_W, 3 * C_HEADS * HD, D_HEADS * HD, 2 * D_KV * HD)
IN_W = sum(IN_SIZES)
IN_SPLITS = tuple(int(s) for s in np.cumsum(IN_SIZES)[:-1])

kernel_name = 'hybrid_diffusion_prefix_step'


def rmsnorm(x, g):
    xf = x.astype(jnp.float32)
    y = xf * lax.rsqrt(jnp.mean(xf * xf, axis=-1, keepdims=True) + EPS)
    return (y * g.astype(jnp.float32)).astype(x.dtype)


def rope1d(x, pos):
    half = x.shape[-1] // 2
    inv = ROPE_BASE ** (-jnp.arange(half, dtype=jnp.float32) / half)
    ang = pos.astype(jnp.float32)[:, None] * inv[None, :]
    cos, sin = jnp.cos(ang), jnp.sin(ang)
    xf = x.astype(jnp.float32)
    x1, x2 = xf[..., :half], xf[..., half:]
    return jnp.concatenate([x1 * cos - x2 * sin, x1 * sin + x2 * cos], axis=-1).astype(x.dtype)


def rope2d(x):
    t = jnp.arange(x.shape[-2])
    half = x.shape[-1] // 2
    return jnp.concatenate([rope1d(x[..., :half], t // GRID_W), rope1d(x[..., half:], t % GRID_W)], axis=-1)


def heads(x, n):
    b, l, _ = x.shape
    return x.reshape(b, l, n, -1).transpose(0, 2, 1, 3)


def softmax_with_sink(s, sink):
    sk = jnp.broadcast_to(sink.astype(jnp.float32)[None, :, :, None, None], s.shape[:-1] + (1,))
    return jax.nn.softmax(jnp.concatenate([sk, s], axis=-1), axis=-1)[..., 1:]


def diff_attention(q, k, v, lam):
    b, h, _, lq, d = q.shape
    nb = lq // QBLK
    scale = d ** -0.5

    def block(qb):
        s = jnp.einsum('bhmqd,bhmkd->bhmqk', qb, k).astype(jnp.float32) * scale
        p = jax.nn.softmax(s, axis=-1)
        w = p[:, :, 0] - lam * p[:, :, 1]
        return jnp.einsum('bhqk,bhkd->bhqd', w.astype(v.dtype), v)

    qb = q.reshape(b, h, 2, nb, QBLK, d).transpose(3, 0, 1, 2, 4, 5)
    out = lax.map(block, qb)
    return out.transpose(1, 2, 0, 3, 4).reshape(b, h, lq, v.shape[-1])


def attend_dense(q, k, v, sink):
    b, kv, g, lq, d = q.shape
    nb = lq // QBLK
    scale = d ** -0.5

    def block(qb):
        s = jnp.einsum('bkgqd,bkcd->bkgqc', qb, k).astype(jnp.float32) * scale
        p = jax.nn.softmax(s, axis=-1) if sink is None else softmax_with_sink(s, sink)
        return jnp.einsum('bkgqc,bkcd->bkgqd', p.astype(v.dtype), v)

    qb = q.reshape(b, kv, g, nb, QBLK, d).transpose(3, 0, 1, 2, 4, 5)
    out = lax.map(block, qb)
    return out.transpose(1, 2, 3, 0, 4, 5).reshape(b, kv, g, lq, v.shape[-1])


def swa_latent(q, k, v, k_ctx, v_ctx, sink):
    b, kv, g, L, d = q.shape
    nb = L // QBLK
    span = QBLK + 2 * WIN
    scale = d ** -0.5
    n_ctx = k_ctx.shape[2]
    pad = ((0, 0), (0, 0), (WIN, WIN), (0, 0))
    kp, vp = jnp.pad(k, pad), jnp.pad(v, pad)

    def block(xs):
        qb, j = xs
        start = j * QBLK
        kb = lax.dynamic_slice_in_dim(kp, start, span, axis=2)
        vb = lax.dynamic_slice_in_dim(vp, start, span, axis=2)
        qi = start + jnp.arange(QBLK)
        ki = start - WIN + jnp.arange(span)
        ok = (jnp.abs(qi[:, None] - ki[None, :]) <= WIN) & (ki >= 0)[None, :] & (ki < L)[None, :]
        s_loc = jnp.where(ok, jnp.einsum('bkgqd,bkjd->bkgqj', qb, kb).astype(jnp.float32) * scale, NEG)
        s_ctx = jnp.einsum('bkgqd,bkcd->bkgqc', qb, k_ctx).astype(jnp.float32) * scale
        p = softmax_with_sink(jnp.concatenate([s_ctx, s_loc], axis=-1), sink).astype(v.dtype)
        return (jnp.einsum('bkgqc,bkcd->bkgqd', p[..., :n_ctx], v_ctx)
                + jnp.einsum('bkgqj,bkjd->bkgqd', p[..., n_ctx:], vb))

    qb = q.reshape(b, kv, g, nb, QBLK, d).transpose(3, 0, 1, 2, 4, 5)
    out = lax.map(block, (qb, jnp.arange(nb, dtype=jnp.int32)))
    return out.transpose(1, 2, 3, 0, 4, 5).reshape(b, kv, g, L, d)


def na_latent(q, k, v, k_ctx, v_ctx, rpb):
    b, h, L, d = q.shape
    rows = L // GRID_W
    wr = min(NA_ROWS, rows)
    ncb = GRID_W // NA_QB
    scale = d ** -0.5
    n_ctx = k_ctx.shape[2]
    row_start = np.clip(np.arange(rows) - wr // 2, 0, rows - wr)
    qcols = np.arange(GRID_W).reshape(ncb, NA_QB)
    col_start = np.clip(qcols - NA_COLS // 2, 0, GRID_W - NA_COLS)
    band = (np.clip(np.arange(ncb) * NA_QB - NA_COLS // 2, 0, GRID_W - NA_KB)[:, None]
            + np.arange(NA_KB)[None, :])
    col_ok = (band[:, None, :] >= col_start[:, :, None]) & (band[:, None, :] < col_start[:, :, None] + NA_COLS)
    dcol = np.clip(band[:, None, :] - qcols[:, :, None] + NA_COLS - 1, 0, 2 * NA_COLS - 2)
    kg = k.reshape(b, h, rows, GRID_W, d)
    vg = v.reshape(b, h, rows, GRID_W, d)
    rpbf = rpb.astype(jnp.float32)

    def row_block(xs):
        qr, rs, ri = xs
        kb = lax.dynamic_slice_in_dim(kg, rs, wr, axis=2)[:, :, :, band]
        vb = lax.dynamic_slice_in_dim(vg, rs, wr, axis=2)[:, :, :, band]
        s_loc = jnp.einsum('bhnqd,bhrnkd->bhnqrk', qr, kb).astype(jnp.float32) * scale
        drow = rs + jnp.arange(wr) - ri + NA_ROWS - 1
        bias = rpbf[:, drow[:, None, None, None], dcol[None]].transpose(0, 2, 3, 1, 4)
        s_loc = jnp.where(col_ok[:, :, None, :], s_loc + bias, NEG).reshape(b, h, ncb, NA_QB, wr * NA_KB)
        s_ctx = jnp.einsum('bhnqd,bhcd->bhnqc', qr, k_ctx).astype(jnp.float32) * scale
        p = jax.nn.softmax(jnp.concatenate([s_ctx, s_loc], axis=-1), axis=-1).astype(v.dtype)
        p_loc = p[..., n_ctx:].reshape(b, h, ncb, NA_QB, wr, NA_KB)
        return (jnp.einsum('bhnqc,bhcd->bhnqd', p[..., :n_ctx], v_ctx)
                + jnp.einsum('bhnqrk,bhrnkd->bhnqd', p_loc, vb))

    qr = q.reshape(b, h, rows, ncb, NA_QB, d).transpose(2, 0, 1, 3, 4, 5)
    out = lax.map(row_block, (qr, jnp.asarray(row_start, jnp.int32), jnp.arange(rows, dtype=jnp.int32)))
    return out.transpose(1, 2, 0, 3, 4, 5).reshape(b, h, L, d)


def conv_centred(x, w, bias):
    L = x.shape[1]
    xp = jnp.pad(x, ((0, 0), (CONV_LEFT, CONV_W - 1 - CONV_LEFT), (0, 0)))
    y = bias
    for j in range(CONV_W):
        y = y + xp[:, j:j + L] * w[j]
    return y


def rglru(x, wa, ba, wx, bx, lam, h0, reverse):
    f32 = jnp.float32
    b, L, W = x.shape
    xf = x.astype(f32)
    xb = xf.reshape(b, L, B_BLOCKS, B_BLK)
    r = jax.nn.sigmoid(jnp.einsum('blnj,njk->blnk', xb, wa.astype(f32)).reshape(b, L, W) + ba.astype(f32))
    i = jax.nn.sigmoid(jnp.einsum('blnj,njk->blnk', xb, wx.astype(f32)).reshape(b, L, W) + bx.astype(f32))
    log_a = -LRU_C * r * jax.nn.softplus(-lam.astype(f32))
    a = jnp.exp(log_a)
    u = jnp.sqrt(-jnp.expm1(2.0 * log_a)) * (i * xf)

    def step(hc, au):
        hc = au[0] * hc + au[1]
        return hc, hc

    h_last, hs = lax.scan(step, h0.astype(f32), (a.swapaxes(0, 1), u.swapaxes(0, 1)), reverse=reverse)
    return hs.swapaxes(0, 1), h_last


def merge_out(h, br, gates, w_mg, b_mg, w_bo, w_o):
    g = jax.nn.sigmoid(jnp.einsum('bld,dnm->blnm', h, w_mg) + b_mg)
    proj = jnp.einsum('blnw,nwm->blnm', br * jax.nn.silu(gates), w_bo)
    return jnp.einsum('blnm,blnm->blm', g, proj) @ w_o


def layer(x, cvec, lp, l, cache):
    (norm_g, w_ada, b_ada, w_in, diff_lam, diff_g, conv_w, conv_b, lru_wa, lru_ba, lru_wx, lru_bx,
     lru_lam, na_rpb, swa_sink, w_mg, b_mg, w_bo, w_o) = lp
    f32 = jnp.float32
    b, L, _ = x.shape
    mod = (jax.nn.silu(cvec) @ w_ada + b_ada)[:, None, :]
    shift, scale, gate = jnp.split(mod, 3, axis=-1)
    h = rmsnorm(x, norm_g) * (1.0 + scale) + shift
    g_br, a_qkv, b_x, c_qkv, d_q, d_kv = jnp.split(h @ w_in, IN_SPLITS, axis=-1)

    aq, ak, av = jnp.split(a_qkv, 3, axis=-1)
    aq = aq.reshape(b, L, A_HEADS, 2, A_QK).transpose(0, 2, 3, 1, 4)
    ak = ak.reshape(b, L, A_HEADS, 2, A_QK).transpose(0, 2, 3, 1, 4)
    av = heads(av, A_HEADS)
    lam_init = 0.8 - 0.6 * math.exp(-0.3 * l)
    lf = diff_lam.astype(f32)
    lam = jnp.exp(jnp.sum(lf[0] * lf[1])) - jnp.exp(jnp.sum(lf[2] * lf[3])) + lam_init

    cq, ck, cv = [heads(t, C_HEADS) for t in jnp.split(c_qkv, 3, axis=-1)]
    dq = d_q.reshape(b, L, D_KV, D_GROUP, HD).transpose(0, 2, 3, 1, 4)
    dk, dv = [heads(t, D_KV) for t in jnp.split(d_kv, 2, axis=-1)]
    sink = swa_sink.reshape(D_KV, D_GROUP)
    xc = conv_centred(b_x, conv_w, conv_b)

    if cache is None:
        ya = diff_attention(aq, ak, av, lam)
        h0f = jnp.zeros((b, BR_W), f32)
        h0b = jnp.zeros((b, BR_W), f32)
        yc = attend_dense(cq[:, :, None], ck, cv, None)[:, :, 0]
        yd = attend_dense(dq, dk, dv, sink)
    else:
        ck_a, cv_a, ck_c, cv_c, ck_d, cv_d, st = cache
        ya = diff_attention(rope2d(aq), jnp.concatenate([ck_a, rope2d(ak)], axis=3),
                            jnp.concatenate([cv_a, av], axis=2), lam)
        h0f, h0b = st[:, 0], st[:, 1]
        yc = na_latent(cq, ck, cv, ck_c, cv_c, na_rpb)
        yd = swa_latent(rope2d(dq), rope2d(dk), dv, ck_d, cv_d, sink)

    ya = (rmsnorm(ya, diff_g) * (1.0 - lam_init)).transpose(0, 2, 1, 3).reshape(b, L, BR_W)
    yf, hf = rglru(xc, lru_wa[0], lru_ba[0], lru_wx[0], lru_bx[0], lru_lam[0], h0f, False)
    yb, hb = rglru(xc, lru_wa[1], lru_ba[1], lru_wx[1], lru_bx[1], lru_lam[1], h0b, True)
    yr = (yf + yb).astype(x.dtype)
    yc = yc.transpose(0, 2, 1, 3).reshape(b, L, BR_W)
    yd = yd.transpose(0, 3, 1, 2, 4).reshape(b, L, BR_W)
    br = jnp.stack([ya, yr, yc, yd], axis=2)
    out = merge_out(h, br, g_br.reshape(b, L, N_BRANCH, BR_W), w_mg, b_mg, w_bo, w_o)
    x = x + gate * out
    if cache is None:
        return x, (ak, av, ck, cv, dk, dv, jnp.stack([hf, hb], axis=1).astype(x.dtype))
    return x, None


def setup_inputs(seed: int = 0) -> dict:
    key = jax.random.key(seed)
    ks = list(jax.random.split(key, 32))
    f32 = jnp.float32

    def nrm(i, shape, s):
        return jax.random.normal(ks[i], shape, f32) * s

    a0 = jax.random.uniform(ks[31], (DEPTH, 2, BR_W), f32, 0.9, 0.999)
    return {
        'x_prompt': nrm(0, (BATCH, SEQ, D_MODEL), 1.0),
        'x_sample': nrm(1, (DEC_BATCH, DEC_SEQ, D_MODEL), 1.0),
        'cache_diff_k': nrm(2, (DEC_BATCH, DEPTH, A_HEADS, 2, PAST_LEN, A_QK), 1.0),
        'cache_diff_v': nrm(3, (DEC_BATCH, DEPTH, A_HEADS, PAST_LEN, HD), 1.0),
        'cache_na_k': nrm(4, (DEC_BATCH, DEPTH, C_HEADS, PAST_LEN, HD), 1.0),
        'cache_na_v': nrm(5, (DEC_BATCH, DEPTH, C_HEADS, PAST_LEN, HD), 1.0),
        'cache_swa_k': nrm(6, (DEC_BATCH, DEPTH, D_KV, PAST_LEN, HD), 1.0),
        'cache_swa_v': nrm(7, (DEC_BATCH, DEPTH, D_KV, PAST_LEN, HD), 1.0),
        'state_lru': nrm(8, (DEC_BATCH, DEPTH, 2, BR_W), 0.5),
        'c': nrm(9, (DEC_BATCH, D_MODEL), 1.0),
        'c_ctx': nrm(10, (D_MODEL,), 1.0),
        'norm_g': 1.0 + nrm(11, (DEPTH, D_MODEL), 0.02),
        'w_ada': nrm(12, (DEPTH, D_MODEL, 3 * D_MODEL), 0.5 * D_MODEL ** -0.5),
        'b_ada': nrm(13, (DEPTH, 3 * D_MODEL), 0.02),
        'w_in': nrm(14, (DEPTH, D_MODEL, IN_W), D_MODEL ** -0.5),
        'diff_lambda': nrm(15, (DEPTH, 4, A_QK), 0.1),
        'diff_norm_g': 1.0 + nrm(16, (DEPTH, HD), 0.02),
        'conv_w': nrm(17, (DEPTH, CONV_W, BR_W), CONV_W ** -0.5),
        'conv_b': nrm(18, (DEPTH, BR_W), 0.02),
        'lru_wa': nrm(19, (DEPTH, 2, B_BLOCKS, B_BLK, B_BLK), B_BLK ** -0.5),
        'lru_ba': nrm(20, (DEPTH, 2, BR_W), 0.02),
        'lru_wx': nrm(21, (DEPTH, 2, B_BLOCKS, B_BLK, B_BLK), B_BLK ** -0.5),
        'lru_bx': nrm(22, (DEPTH, 2, BR_W), 0.02),
        'lru_lam': jnp.log(a0) - jnp.log1p(-a0),
        'na_rpb': nrm(23, (DEPTH, C_HEADS, 2 * NA_ROWS - 1, 2 * NA_COLS - 1), 0.1),
        'swa_sink': nrm(24, (DEPTH, D_HEADS), 0.5),
        'w_mg': nrm(25, (DEPTH, D_MODEL, N_BRANCH, D_MODEL), D_MODEL ** -0.5),
        'b_mg': nrm(26, (DEPTH, N_BRANCH, D_MODEL), 0.02),
        'w_bo': nrm(27, (DEPTH, N_BRANCH, BR_W, D_MODEL), BR_W ** -0.5),
        'w_o': nrm(28, (DEPTH, D_MODEL, D_MODEL), D_MODEL ** -0.5),
        'norm_f': 1.0 + nrm(29, (D_MODEL,), 0.02),
    }


def reference(x_prompt, x_sample, cache_diff_k, cache_diff_v, cache_na_k, cache_na_v, cache_swa_k, cache_swa_v,
              state_lru, c, c_ctx, norm_g, w_ada, b_ada, w_in, diff_lambda, diff_norm_g, conv_w, conv_b,
              lru_wa, lru_ba, lru_wx, lru_bx, lru_lam, na_rpb, swa_sink, w_mg, b_mg, w_bo, w_o, norm_f):
    xp = x_prompt
    xs = x_sample
    per_layer = []
    for l in range(DEPTH):
        lp = (norm_g[l], w_ada[l], b_ada[l], w_in[l], diff_lambda[l], diff_norm_g[l], conv_w[l], conv_b[l],
              lru_wa[l], lru_ba[l], lru_wx[l], lru_bx[l], lru_lam[l], na_rpb[l], swa_sink[l],
              w_mg[l], b_mg[l], w_bo[l], w_o[l])
        xp, st = layer(xp, c_ctx[None, :], lp, l, None)
        per_layer.append(st)
        cache_l = (cache_diff_k[:, l], cache_diff_v[:, l], cache_na_k[:, l], cache_na_v[:, l],
                   cache_swa_k[:, l], cache_swa_v[:, l], state_lru[:, l])
        xs, _ = layer(xs, c, lp, l, cache_l)
    y_prompt = rmsnorm(xp, norm_f)
    y_sample = rmsnorm(xs, norm_f)
    stacked = [jnp.stack([per_layer[l][i] for l in range(DEPTH)], axis=1) for i in range(7)]
    new_diff_k, new_diff_v, new_na_k, new_na_v, new_swa_k, new_swa_v, new_state_lru = stacked
    return (y_prompt, y_sample, new_diff_k, new_diff_v, new_na_k, new_na_v, new_swa_k, new_swa_v, new_state_lru)
```

```python
import functools
import math

import numpy as np
import jax
import jax.numpy as jnp
from jax import lax
from jax.experimental import pallas as pl
from jax.experimental.pallas import tpu as pltpu

F32 = jnp.float32
BF16 = jnp.bfloat16

D_MODEL = 1024
BATCH = 16
SEQ = 256
DEPTH = 2
DEC_BATCH = 8
DEC_SEQ = 1024
PAST_LEN = 512
GRID_W = 64
N_BRANCH = 4
BR_W = D_MODEL // 4
HD = 64
A_HEADS = BR_W // HD
A_QK = HD // 2
B_BLOCKS = 4
B_BLK = BR_W // B_BLOCKS
CONV_W = 4
LRU_C = 8.0
C_HEADS = BR_W // HD
NA_ROWS = 8
NA_COLS = 16
D_HEADS = BR_W // HD
D_KV = 2
D_GROUP = D_HEADS // D_KV
WIN = 128
ROPE_BASE = 10000.0
EPS = 1e-6
NEG = -1e30

N_CTX = BATCH * SEQ
N_LAT = DEC_BATCH * DEC_SEQ
N_TOK = N_CTX + N_LAT
GRID_ROWS = DEC_SEQ // GRID_W

TM = 512
N_CTX_TILES = N_CTX // TM
N_TILES = N_TOK // TM
LAT_TILES_PER_SEQ = DEC_SEQ // TM

PROJ_W = 3072
COL_G = 0
COL_AQ = 1024
COL_AK = 1280
COL_AV = 1536
COL_CQ = 1792
COL_CK = 2048
COL_CV = 2304
COL_DQ = 2560
COL_DKV = 2816
COL_BX = 3072
IN_W = 3328
CACHE_W = 1280

NA_HALF_Q = DEC_SEQ // 2
NA_KROWS = 12
NA_KSPAN = NA_KROWS * GRID_W
NA_HALF_KSTART = 4 * GRID_W

SWA_QB = 128
SWA_SPAN = SWA_QB + 2 * WIN

VMEM_LIMIT = 56 * 1024 * 1024


def _cparams(sem):
    return pltpu.CompilerParams(dimension_semantics=sem, vmem_limit_bytes=VMEM_LIMIT)


def _resident(shape, index_map):
    return pl.BlockSpec(shape, index_map, pipeline_mode=pl.Buffered(1))


def _dot(a, b):
    return jnp.dot(a, b, preferred_element_type=F32)


def _dot_nt(a, b):
    return lax.dot_general(a, b, (((1,), (1,)), ((), ())), preferred_element_type=F32)


def _silu(x):
    return x * jax.nn.sigmoid(x)


def _rope_tables(d):
    half = d // 2
    quarter = half // 2
    lane = np.arange(128)
    q = lane % d
    use_col = (q >= half)
    i = (q % half) % quarter
    first = (q % half) < quarter
    inv = (ROPE_BASE ** (-np.arange(quarter, dtype=np.float32) / np.float32(quarter))).astype(np.float32)
    t = np.arange(DEC_SEQ)
    pos = np.where(use_col[None, :], (t % GRID_W)[:, None], (t // GRID_W)[:, None]).astype(np.float32)
    ang = (pos * inv[i][None, :]).astype(np.float32)
    cos, sin = np.cos(ang), np.sin(ang)
    c = np.concatenate([np.ones((TM, 128), np.float32), cos], axis=0)
    s1 = np.concatenate([np.zeros((TM, 128), np.float32), np.where(first[None, :], -sin, 0.0)], axis=0)
    s2 = np.concatenate([np.zeros((TM, 128), np.float32), np.where(first[None, :], 0.0, sin)], axis=0)
    return np.stack([c, s1, s2]).astype(np.float32), quarter


def _rope128(x, tab_ref, shift):
    return (x * tab_ref[0] + pltpu.roll(x, 128 - shift, 1) * tab_ref[1]
            + pltpu.roll(x, shift, 1) * tab_ref[2])


def _rope(x, tab_ref, shift):
    w = x.shape[1]
    return jnp.concatenate([_rope128(x[:, c:c + 128], tab_ref, shift) for c in range(0, w, 128)], axis=1)


def _mod_kernel(c_ref, w_ref, b_ref, o_ref):
    c = c_ref[...]
    o_ref[0] = _dot(_silu(c).astype(BF16), w_ref[0].astype(BF16)) + b_ref[0]


def _modulation(cvecs, w_ada, b_ada):
    nb = 3 * D_MODEL // 768
    return pl.pallas_call(
        _mod_kernel,
        grid=(DEPTH, nb),
        in_specs=[pl.BlockSpec((16, D_MODEL), lambda l, j: (0, 0)),
                  pl.BlockSpec((1, D_MODEL, 768), lambda l, j: (l, 0, j)),
                  pl.BlockSpec((1, 1, 768), lambda l, j: (l, 0, j))],
        out_specs=pl.BlockSpec((1, 16, 768), lambda l, j: (l, 0, j)),
        out_shape=jax.ShapeDtypeStruct((DEPTH, 16, 3 * D_MODEL), F32),
        compiler_params=_cparams(("arbitrary", "arbitrary")),
        name="modulation",
    )(cvecs, w_ada, b_ada.reshape(DEPTH, 1, 3 * D_MODEL))


def _mod_row(i):
    return jnp.where(i < N_CTX_TILES, 0, 1 + (i - N_CTX_TILES) // LAT_TILES_PER_SEQ)


def _rope_blk(i):
    return jnp.where(i < N_CTX_TILES, 0, 1 + (i - N_CTX_TILES) % LAT_TILES_PER_SEQ)


def _load_x(i, xa_ref, xb_ref):
    return jnp.where(i < N_CTX_TILES, xa_ref[...], xb_ref[...])


def _inproj_kernel(xa_ref, xb_ref, mod_ref, g_ref, w_ref, ta_ref, td_ref,
                   h_ref, proj_ref, bx_ref, cache_ref):
    i = pl.program_id(0)
    x = _load_x(i, xa_ref, xb_ref)
    mod = mod_ref[0]
    shift, scale = mod[:, :D_MODEL], mod[:, D_MODEL:2 * D_MODEL]
    xn = x * lax.rsqrt(jnp.mean(x * x, axis=-1, keepdims=True) + EPS) * g_ref[...]
    hb = (xn * (1.0 + scale) + shift).astype(BF16)
    h_ref[...] = hb

    def mm(c0, c1):
        return _dot(hb, w_ref[:, c0:c1])

    for c in range(0, 1024, 512):
        proj_ref[:, COL_G + c:COL_G + c + 512] = _silu(mm(c, c + 512)).astype(BF16)

    aq = mm(COL_AQ, COL_AQ + 256)
    proj_ref[:, COL_AQ:COL_AQ + 256] = _rope(aq, ta_ref, A_QK // 4).astype(BF16)
    ak = mm(COL_AK, COL_AK + 256)
    proj_ref[:, COL_AK:COL_AK + 256] = _rope(ak, ta_ref, A_QK // 4).astype(BF16)
    cache_ref[:, 0:256] = ak
    av = mm(COL_AV, COL_AV + 256)
    proj_ref[:, COL_AV:COL_AV + 256] = av.astype(BF16)
    cache_ref[:, 256:512] = av

    cq = mm(COL_CQ, COL_CQ + 256)
    proj_ref[:, COL_CQ:COL_CQ + 256] = cq.astype(BF16)
    ckv = mm(COL_CK, COL_CK + 512)
    proj_ref[:, COL_CK:COL_CK + 512] = ckv.astype(BF16)
    cache_ref[:, 512:1024] = ckv

    dq = mm(COL_DQ, COL_DQ + 256)
    proj_ref[:, COL_DQ:COL_DQ + 256] = _rope(dq, td_ref, HD // 4).astype(BF16)
    dkv = mm(COL_DKV, COL_DKV + 256)
    dk = dkv[:, :128]
    proj_ref[:, COL_DKV:COL_DKV + 128] = _rope(dk, td_ref, HD // 4).astype(BF16)
    proj_ref[:, COL_DKV + 128:COL_DKV + 256] = dkv[:, 128:].astype(BF16)
    cache_ref[:, 1024:1280] = dkv

    bx_ref[...] = mm(COL_BX, COL_BX + 256)


def _inproj(xa, xb, xb_off, mod3, l, norm_g_l, w_in_l, tab_a, tab_d):
    row = lambda i: (i, 0)
    return pl.pallas_call(
        _inproj_kernel,
        grid=(N_TILES,),
        in_specs=[
            pl.BlockSpec((TM, D_MODEL), lambda i: (jnp.minimum(i, N_CTX_TILES - 1), 0)),
            pl.BlockSpec((TM, D_MODEL), lambda i: (jnp.maximum(i - N_CTX_TILES, 0) + xb_off, 0)),
            pl.BlockSpec((1, 1, 3 * D_MODEL), lambda i: (l * 16 + _mod_row(i), 0, 0)),
            _resident((1, D_MODEL), lambda i: (0, 0)),
            _resident((D_MODEL, IN_W), lambda i: (0, 0)),
            pl.BlockSpec((3, TM, 128), lambda i: (0, _rope_blk(i), 0)),
            pl.BlockSpec((3, TM, 128), lambda i: (0, _rope_blk(i), 0)),
        ],
        out_specs=[
            pl.BlockSpec((TM, D_MODEL), row),
            pl.BlockSpec((TM, PROJ_W), row),
            pl.BlockSpec((TM, BR_W), row),
            pl.BlockSpec((TM, CACHE_W), lambda i: (jnp.minimum(i, N_CTX_TILES), 0)),
        ],
        out_shape=[
            jax.ShapeDtypeStruct((N_TOK, D_MODEL), BF16),
            jax.ShapeDtypeStruct((N_TOK, PROJ_W), BF16),
            jax.ShapeDtypeStruct((N_TOK, BR_W), F32),
            jax.ShapeDtypeStruct((N_CTX + TM, CACHE_W), F32),
        ],
        compiler_params=_cparams(("arbitrary",)),
        name=f"inproj_l{l}",
    )(xa, xb, mod3, norm_g_l, w_in_l, tab_a, tab_d)


def _diff_lambda(lam_ref, lam_init):
    lv = lam_ref[...]
    s1 = jnp.sum(lv[0:1] * lv[1:2], axis=-1, keepdims=True)
    s2 = jnp.sum(lv[2:3] * lv[3:4], axis=-1, keepdims=True)
    return jnp.exp(s1) - jnp.exp(s2) + lam_init


def _diff_norm(o, dg_ref, lam_init):
    y = o * lax.rsqrt(jnp.mean(o * o, axis=-1, keepdims=True) + EPS) * dg_ref[...]
    return y * (1.0 - lam_init)


def _softmax_parts(parts, extra=None):
    m = functools.reduce(jnp.maximum, [jnp.max(s, axis=-1, keepdims=True) for s in parts])
    if extra is not None:
        m = jnp.maximum(m, extra)
    es = [jnp.exp(s - m) for s in parts]
    tot = functools.reduce(lambda a, b: a + b, [jnp.sum(e, axis=-1, keepdims=True) for e in es])
    if extra is not None:
        tot = tot + jnp.exp(extra - m)
    return es, 1.0 / tot


def _ctx_attn_kernel(lam_init, aq_ref, ak_ref, av_ref, cq_ref, ck_ref, cv_ref, dq_ref, dkv_ref,
                     ga_ref, gc_ref, gd_ref, lam_ref, dg_ref, sink_ref, za_ref, zc_ref, zd_ref):
    lam = _diff_lambda(lam_ref, lam_init)
    a_scale = A_QK ** -0.5
    h_scale = HD ** -0.5

    aq, ak, av = aq_ref[...], ak_ref[...], av_ref[...]
    outs = []
    for h in range(A_HEADS):
        c = h * HD
        ws = []
        for m in range(2):
            q = aq[:, c + m * A_QK:c + (m + 1) * A_QK]
            k = ak[:, c + m * A_QK:c + (m + 1) * A_QK]
            (e,), r = _softmax_parts([_dot_nt(q, k) * a_scale])
            ws.append((e, r))
        w = ws[0][0] * ws[0][1] - ws[1][0] * (lam * ws[1][1])
        o = _dot(w.astype(BF16), av[:, c:c + HD])
        outs.append(_diff_norm(o, dg_ref, lam_init))
    za_ref[...] = (jnp.concatenate(outs, axis=1) * ga_ref[...].astype(F32)).astype(BF16)

    cq, ck, cv = cq_ref[...], ck_ref[...], cv_ref[...]
    outs = []
    for h in range(C_HEADS):
        c = h * HD
        (e,), r = _softmax_parts([_dot_nt(cq[:, c:c + HD], ck[:, c:c + HD]) * h_scale])
        outs.append(_dot(e.astype(BF16), cv[:, c:c + HD]) * r)
    zc_ref[...] = (jnp.concatenate(outs, axis=1) * gc_ref[...].astype(F32)).astype(BF16)

    dq, dkv = dq_ref[...], dkv_ref[...]
    sink = sink_ref[...]
    rows = lax.broadcasted_iota(jnp.int32, (D_GROUP * SEQ, 1), 0)
    outs = []
    for kv in range(D_KV):
        q2 = jnp.concatenate([dq[:, (kv * D_GROUP + g) * HD:(kv * D_GROUP + g + 1) * HD]
                              for g in range(D_GROUP)], axis=0)
        k = dkv[:, kv * HD:(kv + 1) * HD]
        v = dkv[:, D_KV * HD + kv * HD:D_KV * HD + (kv + 1) * HD]
        sk = jnp.where(rows < SEQ, sink[:, kv * D_GROUP:kv * D_GROUP + 1],
                       sink[:, kv * D_GROUP + 1:kv * D_GROUP + 2])
        (e,), r = _softmax_parts([_dot_nt(q2, k) * h_scale], extra=sk)
        o = _dot(e.astype(BF16), v) * r
        outs += [o[g * SEQ:(g + 1) * SEQ] for g in range(D_GROUP)]
    zd_ref[...] = (jnp.concatenate(outs, axis=1) * gd_ref[...].astype(F32)).astype(BF16)


def _ctx_attn(proj, l, lam_l, dg_l, sink_l):
    lam_init = 0.8 - 0.6 * math.exp(-0.3 * l)
    colblk = lambda cb: pl.BlockSpec((SEQ, 256), lambda b: (b, cb))
    zspec = pl.BlockSpec((SEQ, BR_W), lambda b: (b, 0))
    zshape = jax.ShapeDtypeStruct((N_TOK, BR_W), BF16)
    return pl.pallas_call(
        functools.partial(_ctx_attn_kernel, lam_init),
        grid=(BATCH,),
        in_specs=[colblk(COL_AQ // 256), colblk(COL_AK // 256), colblk(COL_AV // 256),
                  colblk(COL_CQ // 256), colblk(COL_CK // 256), colblk(COL_CV // 256),
                  colblk(COL_DQ // 256), colblk(COL_DKV // 256),
                  colblk(0), colblk(2), colblk(3),
                  pl.BlockSpec((4, A_QK), lambda b: (0, 0)),
                  pl.BlockSpec((1, HD), lambda b: (0, 0)),
                  pl.BlockSpec((1, D_HEADS), lambda b: (0, 0))],
        out_specs=[zspec, zspec, zspec],
        out_shape=[zshape, zshape, zshape],
        compiler_params=_cparams(("arbitrary",)),
        name=f"ctx_attn_l{l}",
    )(*([proj] * 11), lam_l, dg_l, sink_l)


LA_QB = 256


def _lat_diff_kernel(lam_init, aq_ref, ak_ref, av_ref, ck_ref, cv_ref, ga_ref, lam_ref, dg_ref, zin_ref, za_ref):
    del zin_ref
    lam = _diff_lambda(lam_ref, lam_init)
    a_scale = A_QK ** -0.5
    aq = aq_ref[...]
    outs = []
    for h in range(A_HEADS):
        c = h * HD
        ws = []
        for m in range(2):
            q = aq[:, c + m * A_QK:c + (m + 1) * A_QK]
            kc = ck_ref[0, 0, h, m].astype(BF16)
            kl = ak_ref[:, c + m * A_QK:c + (m + 1) * A_QK]
            es, r = _softmax_parts([_dot_nt(q, kc) * a_scale, _dot_nt(q, kl) * a_scale])
            ws.append((es, r))
        lr = lam * ws[1][1]
        w_ctx = ws[0][0][0] * ws[0][1] - ws[1][0][0] * lr
        w_loc = ws[0][0][1] * ws[0][1] - ws[1][0][1] * lr
        o = (_dot(w_ctx.astype(BF16), cv_ref[0, 0, h].astype(BF16))
             + _dot(w_loc.astype(BF16), av_ref[:, c:c + HD]))
        outs.append(_diff_norm(o, dg_ref, lam_init))
    za_ref[...] = (jnp.concatenate(outs, axis=1) * ga_ref[...].astype(F32)).astype(BF16)


def _lat_diff(proj, cache_k, cache_v, l, lam_l, dg_l, z_a):
    lam_init = 0.8 - 0.6 * math.exp(-0.3 * l)
    nq = DEC_SEQ // LA_QB
    qrow = lambda b, j: N_CTX // LA_QB + b * nq + j
    srow = lambda b, j: N_CTX // DEC_SEQ + b
    return pl.pallas_call(
        functools.partial(_lat_diff_kernel, lam_init),
        grid=(DEC_BATCH, nq),
        in_specs=[pl.BlockSpec((LA_QB, 256), lambda b, j: (qrow(b, j), COL_AQ // 256)),
                  pl.BlockSpec((DEC_SEQ, 256), lambda b, j: (srow(b, j), COL_AK // 256)),
                  pl.BlockSpec((DEC_SEQ, 256), lambda b, j: (srow(b, j), COL_AV // 256)),
                  pl.BlockSpec((1, 1, A_HEADS, 2, PAST_LEN, A_QK), lambda b, j: (b, l, 0, 0, 0, 0)),
                  pl.BlockSpec((1, 1, A_HEADS, PAST_LEN, HD), lambda b, j: (b, l, 0, 0, 0)),
                  pl.BlockSpec((LA_QB, 256), lambda b, j: (qrow(b, j), 0)),
                  pl.BlockSpec((4, A_QK), lambda b, j: (0, 0)),
                  pl.BlockSpec((1, HD), lambda b, j: (0, 0)),
                  pl.BlockSpec(memory_space=pl.ANY)],
        out_specs=pl.BlockSpec((LA_QB, BR_W), lambda b, j: (qrow(b, j), 0)),
        out_shape=jax.ShapeDtypeStruct((N_TOK, BR_W), BF16),
        input_output_aliases={8: 0},
        compiler_params=_cparams(("arbitrary", "arbitrary")),
        name=f"lat_diff_l{l}",
    )(proj, proj, proj, cache_k, cache_v, proj, lam_l, dg_l, z_a)


def _na_bias_kernel(rpb_ref, o_ref):
    l, h = pl.program_id(0), pl.program_id(1)
    base = (l * C_HEADS + h) * ((2 * NA_ROWS - 1) * (2 * NA_COLS - 1))
    qc = lax.broadcasted_iota(jnp.int32, (GRID_W, GRID_W), 0)
    kc = lax.broadcasted_iota(jnp.int32, (GRID_W, GRID_W), 1)
    cs = jnp.clip(qc - NA_COLS // 2, 0, GRID_W - NA_COLS)
    ok = (kc >= cs) & (kc < cs + NA_COLS)
    dcol = kc - qc + (NA_COLS - 1)
    neg = jnp.full((GRID_W, GRID_W), NEG, F32)
    tabs = []
    for d in range(2 * NA_ROWS - 1):
        t = neg
        for dc in range(2 * NA_COLS - 1):
            t = jnp.where(ok & (dcol == dc), rpb_ref[base + d * (2 * NA_COLS - 1) + dc], t)
        tabs.append(t)
    for half in range(2):
        for rq in range(NA_ROWS):
            r = half * NA_ROWS + rq
            rs = min(max(r - NA_ROWS // 2, 0), GRID_ROWS - NA_ROWS)
            blks = []
            for kr in range(NA_KROWS):
                kabs = half * 4 + kr
                blks.append(tabs[kabs - r + NA_ROWS - 1] if rs <= kabs < rs + NA_ROWS else neg)
            o_ref[0, 0, half, rq * GRID_W:(rq + 1) * GRID_W, :] = jnp.concatenate(blks, axis=1)


def _na_bias(na_rpb):
    return pl.pallas_call(
        _na_bias_kernel,
        grid=(DEPTH, C_HEADS),
        in_specs=[pl.BlockSpec(memory_space=pltpu.SMEM)],
        out_specs=pl.BlockSpec((1, 1, 2, NA_HALF_Q, NA_KSPAN), lambda l, h: (l, h, 0, 0, 0)),
        out_shape=jax.ShapeDtypeStruct((DEPTH, C_HEADS, 2, NA_HALF_Q, NA_KSPAN), F32),
        compiler_params=_cparams(("arbitrary", "arbitrary")),
        name="na_bias",
    )(na_rpb.reshape(-1))


def _lat_na_kernel(cq_ref, ck_ref, cv_ref, kc_ref, vc_ref, bias_ref, gc_ref, zin_ref, zc_ref):
    del zin_ref
    half = pl.program_id(0)
    h_scale = HD ** -0.5
    k0 = pl.multiple_of(half * NA_HALF_KSTART, NA_HALF_KSTART)
    cq = cq_ref[...]
    kl_all = ck_ref[pl.ds(k0, NA_KSPAN), :]
    vl_all = cv_ref[pl.ds(k0, NA_KSPAN), :]
    outs = []
    for h in range(C_HEADS):
        c = h * HD
        q = cq[:, c:c + HD]
        s_ctx = _dot_nt(q, kc_ref[0, 0, h].astype(BF16)) * h_scale
        s_loc = _dot_nt(q, kl_all[:, c:c + HD]) * h_scale + bias_ref[0, h, 0]
        (e_ctx, e_loc), r = _softmax_parts([s_ctx, s_loc])
        o = _dot(e_ctx.astype(BF16), vc_ref[0, 0, h].astype(BF16)) + _dot(e_loc.astype(BF16), vl_all[:, c:c + HD])
        outs.append(o * r)
    zc_ref[...] = (jnp.concatenate(outs, axis=1) * gc_ref[...].astype(F32)).astype(BF16)


def _lat_na(proj, cache_k, cache_v, bias, l, z_c):
    qrow = lambda hf, b: N_CTX // NA_HALF_Q + b * 2 + hf
    srow = lambda hf, b: N_CTX // DEC_SEQ + b
    return pl.pallas_call(
        _lat_na_kernel,
        grid=(2, DEC_BATCH),
        in_specs=[pl.BlockSpec((NA_HALF_Q, 256), lambda hf, b: (qrow(hf, b), COL_CQ // 256)),
                  pl.BlockSpec((DEC_SEQ, 256), lambda hf, b: (srow(hf, b), COL_CK // 256)),
                  pl.BlockSpec((DEC_SEQ, 256), lambda hf, b: (srow(hf, b), COL_CV // 256)),
                  pl.BlockSpec((1, 1, C_HEADS, PAST_LEN, HD), lambda hf, b: (b, l, 0, 0, 0)),
                  pl.BlockSpec((1, 1, C_HEADS, PAST_LEN, HD), lambda hf, b: (b, l, 0, 0, 0)),
                  pl.BlockSpec((1, C_HEADS, 1, NA_HALF_Q, NA_KSPAN), lambda hf, b: (l, 0, hf, 0, 0)),
                  pl.BlockSpec((NA_HALF_Q, 256), lambda hf, b: (qrow(hf, b), 2)),
                  pl.BlockSpec(memory_space=pl.ANY)],
        out_specs=pl.BlockSpec((NA_HALF_Q, BR_W), lambda hf, b: (qrow(hf, b), 0)),
        out_shape=jax.ShapeDtypeStruct((N_TOK, BR_W), BF16),
        input_output_aliases={7: 0},
        compiler_params=_cparams(("arbitrary", "arbitrary")),
        name=f"lat_na_l{l}",
    )(proj, proj, proj, cache_k, cache_v, bias, proj, z_c)


def _lat_swa_kernel(dq_ref, dkv_ref, kc_ref, vc_ref, sink_ref, gd_ref, zin_ref, zd_ref):
    del zin_ref
    j = pl.program_id(1)
    h_scale = HD ** -0.5
    start = pl.multiple_of(jnp.clip(j * SWA_QB - WIN, 0, DEC_SEQ - SWA_SPAN), SWA_QB)
    dq = dq_ref[...]
    kvl = dkv_ref[pl.ds(start, SWA_SPAN), :]
    sink = sink_ref[...]
    rows = lax.broadcasted_iota(jnp.int32, (D_GROUP * SWA_QB, 1), 0)
    qi = j * SWA_QB + lax.broadcasted_iota(jnp.int32, (D_GROUP * SWA_QB, SWA_SPAN), 0) % SWA_QB
    ki = start + lax.broadcasted_iota(jnp.int32, (D_GROUP * SWA_QB, SWA_SPAN), 1)
    ok = jnp.abs(qi - ki) <= WIN
    outs = []
    for kv in range(D_KV):
        q2 = jnp.concatenate([dq[:, (kv * D_GROUP + g) * HD:(kv * D_GROUP + g + 1) * HD]
                              for g in range(D_GROUP)], axis=0)
        sk = jnp.where(rows < SWA_QB, sink[:, kv * D_GROUP:kv * D_GROUP + 1],
                       sink[:, kv * D_GROUP + 1:kv * D_GROUP + 2])
        s_ctx = _dot_nt(q2, kc_ref[0, 0, kv].astype(BF16)) * h_scale
        s_loc = jnp.where(ok, _dot_nt(q2, kvl[:, kv * HD:(kv + 1) * HD]) * h_scale, NEG)
        (e_ctx, e_loc), r = _softmax_parts([s_ctx, s_loc], extra=sk)
        o = (_dot(e_ctx.astype(BF16), vc_ref[0, 0, kv].astype(BF16))
             + _dot(e_loc.astype(BF16), kvl[:, D_KV * HD + kv * HD:D_KV * HD + (kv + 1) * HD])) * r
        outs += [o[g * SWA_QB:(g + 1) * SWA_QB] for g in range(D_GROUP)]
    zd_ref[...] = (jnp.concatenate(outs, axis=1) * gd_ref[...].astype(F32)).astype(BF16)


def _lat_swa(proj, cache_k, cache_v, sink_l, l, z_d):
    nq = DEC_SEQ // SWA_QB
    qrow = lambda b, j: N_CTX // SWA_QB + b * nq + j
    srow = lambda b, j: N_CTX // DEC_SEQ + b
    return pl.pallas_call(
        _lat_swa_kernel,
        grid=(DEC_BATCH, nq),
        in_specs=[pl.BlockSpec((SWA_QB, 256), lambda b, j: (qrow(b, j), COL_DQ // 256)),
                  pl.BlockSpec((DEC_SEQ, 256), lambda b, j: (srow(b, j), COL_DKV // 256)),
                  pl.BlockSpec((1, 1, D_KV, PAST_LEN, HD), lambda b, j: (b, l, 0, 0, 0)),
                  pl.BlockSpec((1, 1, D_KV, PAST_LEN, HD), lambda b, j: (b, l, 0, 0, 0)),
                  pl.BlockSpec((1, D_HEADS), lambda b, j: (0, 0)),
                  pl.BlockSpec((SWA_QB, 256), lambda b, j: (qrow(b, j), 3)),
                  pl.BlockSpec(memory_space=pl.ANY)],
        out_specs=pl.BlockSpec((SWA_QB, BR_W), lambda b, j: (qrow(b, j), 0)),
        out_shape=jax.ShapeDtypeStruct((N_TOK, BR_W), BF16),
        input_output_aliases={6: 0},
        compiler_params=_cparams(("arbitrary", "arbitrary")),
        name=f"lat_swa_l{l}",
    )(proj, proj, cache_k, cache_v, sink_l, proj, z_d)


def _lru_kernel(seq, has_state, *refs):
    if has_state:
        bx_ref, cw_ref, cb_ref, wg_ref, bg_ref, lam_ref, gr_ref, st_ref, zin_ref, zr_ref, a_s, u_s, y_s = refs
        del zin_ref
        fin_ref = None
    else:
        bx_ref, cw_ref, cb_ref, wg_ref, bg_ref, lam_ref, gr_ref, zr_ref, fin_ref, a_s, u_s, y_s = refs
    x = bx_ref[...]
    t = lax.broadcasted_iota(jnp.int32, (seq, 1), 0)
    cw = cw_ref[...]
    xm1 = jnp.where(t >= 1, pltpu.roll(x, 1, 0), 0.0)
    xp1 = jnp.where(t < seq - 1, pltpu.roll(x, seq - 1, 0), 0.0)
    xp2 = jnp.where(t < seq - 2, pltpu.roll(x, seq - 2, 0), 0.0)
    xc = cb_ref[...] + xm1 * cw[0:1] + x * cw[1:2] + xp1 * cw[2:3] + xp2 * cw[3:4]

    gates = _dot(xc.astype(BF16), wg_ref[...]) + bg_ref[...]
    lam = lam_ref[...]
    nl = -lam
    softplus = jnp.maximum(nl, 0.0) + jnp.log1p(jnp.exp(-jnp.abs(nl)))
    tin = t % 8
    for d in range(2):
        r = jax.nn.sigmoid(gates[:, (2 * d) * BR_W:(2 * d + 1) * BR_W])
        ig = jax.nn.sigmoid(gates[:, (2 * d + 1) * BR_W:(2 * d + 2) * BR_W])
        log_a = -LRU_C * r * softplus[d:d + 1]
        a = jnp.exp(log_a)
        u = jnp.sqrt(-jnp.tanh(log_a) * (a * a + 1.0)) * (ig * xc)
        for s in (1, 2, 4):
            if d == 0:
                keep = tin >= s
                a_n = jnp.where(keep, pltpu.roll(a, s, 0), 1.0)
                u_n = jnp.where(keep, pltpu.roll(u, s, 0), 0.0)
            else:
                keep = tin < 8 - s
                a_n = jnp.where(keep, pltpu.roll(a, seq - s, 0), 1.0)
                u_n = jnp.where(keep, pltpu.roll(u, seq - s, 0), 0.0)
            u = u + a * u_n
            a = a * a_n
        a_s[d] = a
        u_s[d] = u

    if has_state:
        hf0 = st_ref[0, 0, 0:1, :]
        hb0 = st_ref[0, 0, 1:2, :]
    else:
        hf0 = jnp.zeros((1, BR_W), F32)
        hb0 = jnp.zeros((1, BR_W), F32)

    nt = seq // 8

    def body(k, carry):
        hf, hb = carry
        rf = pl.ds(pl.multiple_of(k * 8, 8), 8)
        rb = pl.ds(pl.multiple_of((nt - 1 - k) * 8, 8), 8)
        yf = u_s[0, rf, :] + a_s[0, rf, :] * hf
        yb = u_s[1, rb, :] + a_s[1, rb, :] * hb
        y_s[0, rf, :] = yf
        y_s[1, rb, :] = yb
        return yf[7:8, :], yb[0:1, :]

    hf, hb = lax.fori_loop(0, nt, body, (hf0, hb0))
    zr_ref[...] = ((y_s[0] + y_s[1]) * gr_ref[...].astype(F32)).astype(BF16)
    if fin_ref is not None:
        fin_ref[0, 0:1, :] = hf
        fin_ref[0, 1:2, :] = hb


def _lru(seq, nseq, row0, bx, proj, cw_l, cb_l, wg_l, bg_l, lam_l, state=None, l=0, z_r=None):
    has_state = state is not None
    rblk = lambda b: (row0 // seq + b, 0)
    in_specs = [pl.BlockSpec((seq, BR_W), rblk),
                pl.BlockSpec((CONV_W, BR_W), lambda b: (0, 0)),
                pl.BlockSpec((1, BR_W), lambda b: (0, 0)),
                pl.BlockSpec((BR_W, 4 * BR_W), lambda b: (0, 0)),
                pl.BlockSpec((1, 4 * BR_W), lambda b: (0, 0)),
                pl.BlockSpec((2, BR_W), lambda b: (0, 0)),
                pl.BlockSpec((seq, 256), lambda b: (row0 // seq + b, 1))]
    args = [bx, cw_l, cb_l, wg_l, bg_l, lam_l, proj]
    zspec = pl.BlockSpec((seq, BR_W), rblk)
    zshape = jax.ShapeDtypeStruct((N_TOK, BR_W), BF16)
    scratch = [pltpu.VMEM((2, seq, BR_W), F32)] * 3
    if has_state:
        in_specs += [pl.BlockSpec((1, 1, 2, BR_W), lambda b: (b, l, 0, 0)), pl.BlockSpec(memory_space=pl.ANY)]
        args += [state, z_r]
        return pl.pallas_call(
            functools.partial(_lru_kernel, seq, True),
            grid=(nseq,), in_specs=in_specs, out_specs=zspec, out_shape=zshape,
            scratch_shapes=scratch, input_output_aliases={8: 0},
            compiler_params=_cparams(("arbitrary",)), name=f"lru_lat_l{l}",
        )(*args)
    return pl.pallas_call(
        functools.partial(_lru_kernel, seq, False),
        grid=(nseq,), in_specs=in_specs,
        out_specs=[zspec, pl.BlockSpec((1, 2, BR_W), lambda b: (b, 0, 0))],
        out_shape=[zshape, jax.ShapeDtypeStruct((nseq, 2, BR_W), F32)],
        scratch_shapes=scratch,
        compiler_params=_cparams(("arbitrary",)), name=f"lru_ctx_l{l}",
    )(*args)


def _merge_kernel(final, xa_ref, xb_ref, mod_ref, h_ref, za_ref, zr_ref, zc_ref, zd_ref,
                  wmg_ref, bmg_ref, wbo_ref, wo_ref, nf_ref, *out_refs):
    i = pl.program_id(0)
    x = _load_x(i, xa_ref, xb_ref)
    gate = mod_ref[0][:, 2 * D_MODEL:]
    h = h_ref[...]
    zs = [za_ref[...], zr_ref[...], zc_ref[...], zd_ref[...]]
    cols = []
    for c in range(0, D_MODEL, 512):
        acc = None
        for n in range(N_BRANCH):
            g = jax.nn.sigmoid(_dot(h, wmg_ref[:, n * D_MODEL + c:n * D_MODEL + c + 512])
                               + bmg_ref[:, n * D_MODEL + c:n * D_MODEL + c + 512])
            term = g * _dot(zs[n], wbo_ref[n, :, c:c + 512])
            acc = term if acc is None else acc + term
        cols.append(acc.astype(BF16))
    merged = jnp.concatenate(cols, axis=1)
    xn = x + gate * _dot(merged, wo_ref[...])
    if not final:
        out_refs[0][...] = xn
    else:
        y = xn * lax.rsqrt(jnp.mean(xn * xn, axis=-1, keepdims=True) + EPS) * nf_ref[...]

        @pl.when(i < N_CTX_TILES)
        def _():
            out_refs[0][...] = y

        @pl.when(i >= N_CTX_TILES)
        def _():
            out_refs[1][...] = y


def _merge(final, xa, xb, xb_off, mod3, l, h, zs, wmg_l, bmg_l, wbo_l, wo_l, norm_f):
    row = lambda i: (i, 0)
    if final:
        out_specs = [pl.BlockSpec((TM, D_MODEL), lambda i: (jnp.minimum(i, N_CTX_TILES - 1), 0)),
                     pl.BlockSpec((TM, D_MODEL), lambda i: (jnp.maximum(i - N_CTX_TILES, 0), 0))]
        out_shape = [jax.ShapeDtypeStruct((N_CTX, D_MODEL), F32), jax.ShapeDtypeStruct((N_LAT, D_MODEL), F32)]
    else:
        out_specs = pl.BlockSpec((TM, D_MODEL), row)
        out_shape = jax.ShapeDtypeStruct((N_TOK, D_MODEL), F32)
    return pl.pallas_call(
        functools.partial(_merge_kernel, final),
        grid=(N_TILES,),
        in_specs=[
            pl.BlockSpec((TM, D_MODEL), lambda i: (jnp.minimum(i, N_CTX_TILES - 1), 0)),
            pl.BlockSpec((TM, D_MODEL), lambda i: (jnp.maximum(i - N_CTX_TILES, 0) + xb_off, 0)),
            pl.BlockSpec((1, 1, 3 * D_MODEL), lambda i: (l * 16 + _mod_row(i), 0, 0)),
            pl.BlockSpec((TM, D_MODEL), row),
            pl.BlockSpec((TM, BR_W), row), pl.BlockSpec((TM, BR_W), row),
            pl.BlockSpec((TM, BR_W), row), pl.BlockSpec((TM, BR_W), row),
            _resident((D_MODEL, N_BRANCH * D_MODEL), lambda i: (0, 0)),
            _resident((1, N_BRANCH * D_MODEL), lambda i: (0, 0)),
            _resident((N_BRANCH, BR_W, D_MODEL), lambda i: (0, 0, 0)),
            _resident((D_MODEL, D_MODEL), lambda i: (0, 0)),
            _resident((1, D_MODEL), lambda i: (0, 0)),
        ],
        out_specs=out_specs,
        out_shape=out_shape,
        compiler_params=_cparams(("arbitrary",)),
        name=f"merge_l{l}",
    )(xa, xb, mod3, h, *zs, wmg_l, bmg_l, wbo_l, wo_l, norm_f)


def _block_diag(w):
    out = jnp.zeros((BR_W, BR_W), w.dtype)
    for n in range(B_BLOCKS):
        out = out.at[n * B_BLK:(n + 1) * B_BLK, n * B_BLK:(n + 1) * B_BLK].set(w[n])
    return out


def _heads_out(x, n):
    return x.reshape(BATCH, SEQ, n, -1).transpose(0, 2, 1, 3)


def kernel(x_prompt, x_sample, cache_diff_k, cache_diff_v, cache_na_k, cache_na_v, cache_swa_k, cache_swa_v,
           state_lru, c, c_ctx, norm_g, w_ada, b_ada, w_in, diff_lambda, diff_norm_g, conv_w, conv_b,
           lru_wa, lru_ba, lru_wx, lru_bx, lru_lam, na_rpb, swa_sink, w_mg, b_mg, w_bo, w_o, norm_f):
    tab_a_np, _ = _rope_tables(A_QK)
    tab_d_np, _ = _rope_tables(HD)
    tab_a, tab_d = jnp.asarray(tab_a_np), jnp.asarray(tab_d_np)

    cvecs = jnp.concatenate([c_ctx[None, :], c, jnp.zeros((16 - 1 - DEC_BATCH, D_MODEL), F32)], axis=0)
    mod3 = _modulation(cvecs, w_ada, b_ada).reshape(DEPTH * 16, 1, 3 * D_MODEL)
    bias = _na_bias(na_rpb)

    xa = x_prompt.reshape(N_CTX, D_MODEL)
    xb = x_sample.reshape(N_LAT, D_MODEL)
    xb_off = 0
    caches, states = [], []
    y_p = y_s = None
    for l in range(DEPTH):
        w = w_in[l]
        w_perm = jnp.concatenate([w[:, :1024], w[:, 1024:1792], w[:, 2048:2816], w[:, 2816:3072], w[:, 3072:3328],
                                  w[:, 1792:2048]], axis=1).astype(BF16)
        h, proj, bx, cache = _inproj(xa, xb, xb_off, mod3, l, norm_g[l][None, :], w_perm, tab_a, tab_d)
        caches.append(cache[:N_CTX])

        lam_l, dg_l, sink_l = diff_lambda[l], diff_norm_g[l][None, :], swa_sink[l][None, :]
        z_a, z_c, z_d = _ctx_attn(proj, l, lam_l, dg_l, sink_l)
        z_a = _lat_diff(proj, cache_diff_k, cache_diff_v, l, lam_l, dg_l, z_a)
        z_c = _lat_na(proj, cache_na_k, cache_na_v, bias, l, z_c)
        z_d = _lat_swa(proj, cache_swa_k, cache_swa_v, sink_l, l, z_d)

        wg = jnp.concatenate([_block_diag(lru_wa[l, 0]), _block_diag(lru_wx[l, 0]),
                              _block_diag(lru_wa[l, 1]), _block_diag(lru_wx[l, 1])], axis=1).astype(BF16)
        bg = jnp.concatenate([lru_ba[l, 0], lru_bx[l, 0], lru_ba[l, 1], lru_bx[l, 1]])[None, :]
        lru_args = (bx, proj, conv_w[l], conv_b[l][None, :], wg, bg, lru_lam[l])
        z_r, fin = _lru(SEQ, BATCH, 0, *lru_args, l=l)
        z_r = _lru(DEC_SEQ, DEC_BATCH, N_CTX, *lru_args, state=state_lru, l=l, z_r=z_r)
        states.append(fin)

        final = l == DEPTH - 1
        out = _merge(final, xa, xb, xb_off, mod3, l, h, (z_a, z_r, z_c, z_d),
                     w_mg[l].reshape(D_MODEL, N_BRANCH * D_MODEL).astype(BF16), b_mg[l].reshape(1, -1),
                     w_bo[l].astype(BF16), w_o[l].astype(BF16), norm_f[None, :])
        if final:
            y_p, y_s = out
        else:
            xa = xb = out
            xb_off = N_CTX_TILES

    def stacked(c0, c1, n):
        return jnp.stack([_heads_out(cc[:, c0:c1], n) for cc in caches], axis=1)

    new_diff_k = jnp.stack([cc[:, 0:256].reshape(BATCH, SEQ, A_HEADS, 2, A_QK).transpose(0, 2, 3, 1, 4)
                            for cc in caches], axis=1)
    return (y_p.reshape(BATCH, SEQ, D_MODEL), y_s.reshape(DEC_BATCH, DEC_SEQ, D_MODEL),
            new_diff_k, stacked(256, 512, A_HEADS), stacked(512, 768, C_HEADS), stacked(768, 1024, C_HEADS),
            stacked(1024, 1152, D_KV), stacked(1152, 1280, D_KV), jnp.stack(states, axis=1))
```

```python
import functools
import math

import numpy as np
import jax
import jax.numpy as jnp
from jax import lax
from jax.experimental import pallas as pl
from jax.experimental.pallas import tpu as pltpu

F32 = jnp.float32
BF16 = jnp.bfloat16

D_MODEL = 1024
BATCH = 16
SEQ = 256
DEPTH = 2
DEC_BATCH = 8
DEC_SEQ = 1024
PAST_LEN = 512
GRID_W = 64
N_BRANCH = 4
BR_W = D_MODEL // 4
HD = 64
A_HEADS = BR_W // HD
A_QK = HD // 2
B_BLOCKS = 4
B_BLK = BR_W // B_BLOCKS
CONV_W = 4
LRU_C = 8.0
C_HEADS = BR_W // HD
NA_ROWS = 8
NA_COLS = 16
D_HEADS = BR_W // HD
D_KV = 2
D_GROUP = D_HEADS // D_KV
WIN = 128
ROPE_BASE = 10000.0
EPS = 1e-6
NEG = -1e30

N_CTX = BATCH * SEQ
N_LAT = DEC_BATCH * DEC_SEQ
N_TOK = N_CTX + N_LAT
GRID_ROWS = DEC_SEQ // GRID_W

TM = 512
N_CTX_TILES = N_CTX // TM
N_TILES = N_TOK // TM
LAT_TILES_PER_SEQ = DEC_SEQ // TM

PROJ_W = 3072
COL_G = 0
COL_AQ = 1024
COL_AK = 1280
COL_AV = 1536
COL_CQ = 1792
COL_CK = 2048
COL_CV = 2304
COL_DQ = 2560
COL_DKV = 2816
COL_BX = 3072
IN_W = 3328
CACHE_W = 1280

NA_HALF_Q = DEC_SEQ // 2
NA_KROWS = 12
NA_KSPAN = NA_KROWS * GRID_W
NA_HALF_KSTART = 4 * GRID_W

SWA_QB = 128
SWA_SPAN = SWA_QB + 2 * WIN

VMEM_LIMIT = 56 * 1024 * 1024


def _cparams(sem):
    return pltpu.CompilerParams(dimension_semantics=sem, vmem_limit_bytes=VMEM_LIMIT)


def _resident(shape, index_map):
    return pl.BlockSpec(shape, index_map, pipeline_mode=pl.Buffered(1))


def _dot(a, b):
    return jnp.dot(a, b, preferred_element_type=F32)


def _dot_nt(a, b):
    return lax.dot_general(a, b, (((1,), (1,)), ((), ())), preferred_element_type=F32)


def _silu(x):
    return x * jax.nn.sigmoid(x)


def _rope_tables(d):
    half = d // 2
    quarter = half // 2
    lane = np.arange(128)
    q = lane % d
    use_col = (q >= half)
    i = (q % half) % quarter
    first = (q % half) < quarter
    inv = (ROPE_BASE ** (-np.arange(quarter, dtype=np.float32) / np.float32(quarter))).astype(np.float32)
    t = np.arange(DEC_SEQ)
    pos = np.where(use_col[None, :], (t % GRID_W)[:, None], (t // GRID_W)[:, None]).astype(np.float32)
    ang = (pos * inv[i][None, :]).astype(np.float32)
    cos, sin = np.cos(ang), np.sin(ang)
    c = np.concatenate([np.ones((TM, 128), np.float32), cos], axis=0)
    s1 = np.concatenate([np.zeros((TM, 128), np.float32), np.where(first[None, :], -sin, 0.0)], axis=0)
    s2 = np.concatenate([np.zeros((TM, 128), np.float32), np.where(first[None, :], 0.0, sin)], axis=0)
    return np.stack([c, s1, s2]).astype(np.float32), quarter


def _rope128(x, tab_ref, shift):
    return (x * tab_ref[0] + pltpu.roll(x, 128 - shift, 1) * tab_ref[1]
            + pltpu.roll(x, shift, 1) * tab_ref[2])


def _rope(x, tab_ref, shift):
    w = x.shape[1]
    return jnp.concatenate([_rope128(x[:, c:c + 128], tab_ref, shift) for c in range(0, w, 128)], axis=1)


def _mod_kernel(c_ref, w_ref, b_ref, o_ref):
    c = c_ref[...]
    o_ref[0] = _dot(_silu(c).astype(BF16), w_ref[0].astype(BF16)) + b_ref[0]


def _modulation(cvecs, w_ada, b_ada):
    nb = 3 * D_MODEL // 768
    return pl.pallas_call(
        _mod_kernel,
        grid=(DEPTH, nb),
        in_specs=[pl.BlockSpec((16, D_MODEL), lambda l, j: (0, 0)),
                  pl.BlockSpec((1, D_MODEL, 768), lambda l, j: (l, 0, j)),
                  pl.BlockSpec((1, 1, 768), lambda l, j: (l, 0, j))],
        out_specs=pl.BlockSpec((1, 16, 768), lambda l, j: (l, 0, j)),
        out_shape=jax.ShapeDtypeStruct((DEPTH, 16, 3 * D_MODEL), F32),
        compiler_params=_cparams(("arbitrary", "arbitrary")),
        name="modulation",
    )(cvecs, w_ada, b_ada.reshape(DEPTH, 1, 3 * D_MODEL))


W_STEPS = 8
W_ROWS = D_MODEL // W_STEPS


def _tile(i):
    return jnp.maximum(i - W_STEPS, 0)


def _wchunk(i):
    return jnp.minimum(i, W_STEPS - 1)


def _mod_row(t):
    return jnp.where(t < N_CTX_TILES, 0, 1 + (t - N_CTX_TILES) // LAT_TILES_PER_SEQ)


def _rope_blk(t):
    return jnp.where(t < N_CTX_TILES, 0, 1 + (t - N_CTX_TILES) % LAT_TILES_PER_SEQ)


def _load_x(t, xa_ref, xb_ref):
    return jnp.where(t < N_CTX_TILES, xa_ref[...], xb_ref[...])


def _cast_rows(i, src_ref, dst_ref):
    r0 = pl.multiple_of(i * W_ROWS, W_ROWS)
    dst_ref[pl.ds(r0, W_ROWS), :] = src_ref[0].astype(BF16)


WI_G, WI_AQ, WI_AK, WI_AV, WI_BX, WI_CQ, WI_CKV, WI_DQ, WI_DKV = 0, 1024, 1280, 1536, 1792, 2048, 2304, 2816, 3072
SEQ_PER_TILE = TM // SEQ


def _store_heads(ref, val, n_heads, width):
    for s in range(SEQ_PER_TILE):
        for h in range(n_heads):
            ref[s, 0, h] = val[s * SEQ:(s + 1) * SEQ, h * width:(h + 1) * width]


def _inproj_kernel(n_alias, xa_ref, xb_ref, mod_ref, g_ref, w_ref, ta_ref, td_ref, *refs):
    h_ref, proj_ref, bx_ref, dk_ref, dv_ref, nk_ref, nv_ref, sk_ref, sv_ref, w_s = refs[n_alias:]
    i = pl.program_id(0)

    @pl.when(i < W_STEPS)
    def _():
        _cast_rows(i, w_ref, w_s)

    @pl.when(i >= W_STEPS)
    def _():
        t = i - W_STEPS
        is_ctx = t < N_CTX_TILES
        x = _load_x(t, xa_ref, xb_ref)
        mod = mod_ref[0]
        shift, scale = mod[:, :D_MODEL], mod[:, D_MODEL:2 * D_MODEL]
        xn = x * lax.rsqrt(jnp.mean(x * x, axis=-1, keepdims=True) + EPS) * g_ref[0]
        hb =(xn * (1.0 + scale) + shift).astype(BF16)
        h_ref[...] = hb

        def mm(c0, width):
            return _dot(hb, w_s[:, c0:c0 + width])

        for c in range(0, 1024, 512):
            proj_ref[:, COL_G + c:COL_G + c + 512] = _silu(mm(WI_G + c, 512)).astype(BF16)

        aq = mm(WI_AQ, 256)
        proj_ref[:, COL_AQ:COL_AQ + 256] = _rope(aq, ta_ref, A_QK // 4).astype(BF16)
        ak = mm(WI_AK, 256)
        proj_ref[:, COL_AK:COL_AK + 256] = _rope(ak, ta_ref, A_QK // 4).astype(BF16)
        av = mm(WI_AV, 256)
        proj_ref[:, COL_AV:COL_AV + 256] = av.astype(BF16)

        @pl.when(is_ctx)
        def _():
            for s in range(SEQ_PER_TILE):
                for h in range(A_HEADS):
                    for m in range(2):
                        c0 = h * HD + m * A_QK
                        dk_ref[s, 0, h, m] = ak[s * SEQ:(s + 1) * SEQ, c0:c0 + A_QK]
            _store_heads(dv_ref, av, A_HEADS, HD)

        cq = mm(WI_CQ, 256)
        proj_ref[:, COL_CQ:COL_CQ + 256] = cq.astype(BF16)
        ckv = mm(WI_CKV, 512)
        proj_ref[:, COL_CK:COL_CK + 512] = ckv.astype(BF16)

        @pl.when(is_ctx)
        def _():
            _store_heads(nk_ref, ckv[:, :256], C_HEADS, HD)
            _store_heads(nv_ref, ckv[:, 256:], C_HEADS, HD)

        dq = mm(WI_DQ, 256)
        proj_ref[:, COL_DQ:COL_DQ + 256] = _rope(dq, td_ref, HD // 4).astype(BF16)
        dkv = mm(WI_DKV, 256)
        proj_ref[:, COL_DKV:COL_DKV + 128] = _rope(dkv[:, :128], td_ref, HD // 4).astype(BF16)
        proj_ref[:, COL_DKV + 128:COL_DKV + 256] = dkv[:, 128:].astype(BF16)

        @pl.when(is_ctx)
        def _():
            _store_heads(sk_ref, dkv[:, :128], D_KV, HD)
            _store_heads(sv_ref, dkv[:, 128:], D_KV, HD)

        bx_ref[...] = mm(WI_BX, 256)


def _cache_shapes():
    hs = lambda n: (BATCH, DEPTH, n, SEQ, HD)
    return [(BATCH, DEPTH, A_HEADS, 2, SEQ, A_QK), hs(A_HEADS), hs(C_HEADS), hs(C_HEADS), hs(D_KV), hs(D_KV)]


def _inproj(xa, xb, xb_off, mod3, l, norm_g, w_in, tab_a, tab_d, caches):
    row = lambda i: (_tile(i), 0)
    ctx_blk = lambda i: jnp.minimum(_tile(i), N_CTX_TILES - 1)

    def cache_spec(shape):
        blk = (SEQ_PER_TILE, 1) + shape[2:]
        nz = len(shape) - 2
        return pl.BlockSpec(blk, lambda i: (ctx_blk(i), l) + (0,) * nz)

    cshapes = _cache_shapes()
    aliases = {} if caches is None else {7 + k: 3 + k for k in range(6)}
    extra_specs = [] if caches is None else [pl.BlockSpec(memory_space=pl.ANY)] * 6
    extra_args = [] if caches is None else list(caches)
    outs = pl.pallas_call(
        functools.partial(_inproj_kernel, len(extra_args)),
        grid=(W_STEPS + N_TILES,),
        in_specs=[
            pl.BlockSpec((TM, D_MODEL), lambda i: (ctx_blk(i), 0)),
            pl.BlockSpec((TM, D_MODEL), lambda i: (jnp.maximum(_tile(i) - N_CTX_TILES, 0) + xb_off, 0)),
            pl.BlockSpec((1, 1, 3 * D_MODEL), lambda i: (l * 16 + _mod_row(_tile(i)), 0, 0)),
            pl.BlockSpec((1, 1, D_MODEL), lambda i: (l, 0, 0)),
            pl.BlockSpec((1, W_ROWS, IN_W), lambda i: (l, _wchunk(i), 0)),
            pl.BlockSpec((3, TM, 128), lambda i: (0, _rope_blk(_tile(i)), 0)),
            pl.BlockSpec((3, TM, 128), lambda i: (0, _rope_blk(_tile(i)), 0)),
        ] + extra_specs,
        out_specs=[
            pl.BlockSpec((TM, D_MODEL), row),
            pl.BlockSpec((TM, PROJ_W), row),
            pl.BlockSpec((TM, BR_W), row),
        ] + [cache_spec(s) for s in cshapes],
        out_shape=[
            jax.ShapeDtypeStruct((N_TOK, D_MODEL), BF16),
            jax.ShapeDtypeStruct((N_TOK, PROJ_W), BF16),
            jax.ShapeDtypeStruct((N_TOK, BR_W), F32),
        ] + [jax.ShapeDtypeStruct(s, F32) for s in cshapes],
        scratch_shapes=[pltpu.VMEM((D_MODEL, IN_W), BF16)],
        input_output_aliases=aliases,
        compiler_params=_cparams(("arbitrary",)),
        name=f"inproj_l{l}",
    )(xa, xb, mod3, norm_g, w_in, tab_a, tab_d, *extra_args)
    return outs[0], outs[1], outs[2], tuple(outs[3:])


def _diff_lambda(lam_ref, lam_init):
    lv = lam_ref[...]
    s1 = jnp.sum(lv[0:1] * lv[1:2], axis=-1, keepdims=True)
    s2 = jnp.sum(lv[2:3] * lv[3:4], axis=-1, keepdims=True)
    return jnp.exp(s1) - jnp.exp(s2) + lam_init


def _diff_norm(o, dg_ref, lam_init):
    y = o * lax.rsqrt(jnp.mean(o * o, axis=-1, keepdims=True) + EPS) * dg_ref[...]
    return y * (1.0 - lam_init)


def _softmax_parts(parts, extra=None):
    m = functools.reduce(jnp.maximum, [jnp.max(s, axis=-1, keepdims=True) for s in parts])
    if extra is not None:
        m = jnp.maximum(m, extra)
    es = [jnp.exp(s - m) for s in parts]
    tot = functools.reduce(lambda a, b: a + b, [jnp.sum(e, axis=-1, keepdims=True) for e in es])
    if extra is not None:
        tot = tot + jnp.exp(extra - m)
    return es, 1.0 / tot


def _ctx_attn_kernel(lam_init, aq_ref, ak_ref, av_ref, cq_ref, ck_ref, cv_ref, dq_ref, dkv_ref,
                     ga_ref, gc_ref, gd_ref, lam_ref, dg_ref, sink_ref, za_ref, zc_ref, zd_ref):
    lam = _diff_lambda(lam_ref, lam_init)
    a_scale = A_QK ** -0.5
    h_scale = HD ** -0.5

    aq, ak, av = aq_ref[...], ak_ref[...], av_ref[...]
    outs = []
    for h in range(A_HEADS):
        c = h * HD
        ws = []
        for m in range(2):
            q = aq[:, c + m * A_QK:c + (m + 1) * A_QK]
            k = ak[:, c + m * A_QK:c + (m + 1) * A_QK]
            (e,), r = _softmax_parts([_dot_nt(q, k) * a_scale])
            ws.append((e, r))
        w = ws[0][0] * ws[0][1] - ws[1][0] * (lam * ws[1][1])
        o = _dot(w.astype(BF16), av[:, c:c + HD])
        outs.append(_diff_norm(o, dg_ref, lam_init))
    za_ref[...] = (jnp.concatenate(outs, axis=1) * ga_ref[...].astype(F32)).astype(BF16)

    cq, ck, cv = cq_ref[...], ck_ref[...], cv_ref[...]
    outs = []
    for h in range(C_HEADS):
        c = h * HD
        (e,), r = _softmax_parts([_dot_nt(cq[:, c:c + HD], ck[:, c:c + HD]) * h_scale])
        outs.append(_dot(e.astype(BF16), cv[:, c:c + HD]) * r)
    zc_ref[...] = (jnp.concatenate(outs, axis=1) * gc_ref[...].astype(F32)).astype(BF16)

    dq, dkv = dq_ref[...], dkv_ref[...]
    sink = sink_ref[...]
    rows = lax.broadcasted_iota(jnp.int32, (D_GROUP * SEQ, 1), 0)
    outs = []
    for kv in range(D_KV):
        q2 = jnp.concatenate([dq[:, (kv * D_GROUP + g) * HD:(kv * D_GROUP + g + 1) * HD]
                              for g in range(D_GROUP)], axis=0)
        k = dkv[:, kv * HD:(kv + 1) * HD]
        v = dkv[:, D_KV * HD + kv * HD:D_KV * HD + (kv + 1) * HD]
        sk = jnp.where(rows < SEQ, sink[:, kv * D_GROUP:kv * D_GROUP + 1],
                       sink[:, kv * D_GROUP + 1:kv * D_GROUP + 2])
        (e,), r = _softmax_parts([_dot_nt(q2, k) * h_scale], extra=sk)
        o = _dot(e.astype(BF16), v) * r
        outs += [o[g * SEQ:(g + 1) * SEQ] for g in range(D_GROUP)]
    zd_ref[...] = (jnp.concatenate(outs, axis=1) * gd_ref[...].astype(F32)).astype(BF16)


def _ctx_attn(proj, l, lam_l, dg_l, sink_l):
    lam_init = 0.8 - 0.6 * math.exp(-0.3 * l)
    colblk = lambda cb: pl.BlockSpec((SEQ, 256), lambda b: (b, cb))
    zspec = pl.BlockSpec((SEQ, BR_W), lambda b: (b, 0))
    zshape = jax.ShapeDtypeStruct((N_TOK, BR_W), BF16)
    return pl.pallas_call(
        functools.partial(_ctx_attn_kernel, lam_init),
        grid=(BATCH,),
        in_specs=[colblk(COL_AQ // 256), colblk(COL_AK // 256), colblk(COL_AV // 256),
                  colblk(COL_CQ // 256), colblk(COL_CK // 256), colblk(COL_CV // 256),
                  colblk(COL_DQ // 256), colblk(COL_DKV // 256),
                  colblk(0), colblk(2), colblk(3),
                  pl.BlockSpec((4, A_QK), lambda b: (0, 0)),
                  pl.BlockSpec((1, HD), lambda b: (0, 0)),
                  pl.BlockSpec((1, D_HEADS), lambda b: (0, 0))],
        out_specs=[zspec, zspec, zspec],
        out_shape=[zshape, zshape, zshape],
        compiler_params=_cparams(("arbitrary",)),
        name=f"ctx_attn_l{l}",
    )(*([proj] * 11), lam_l, dg_l, sink_l)


LA_QB = 256


def _lat_diff_kernel(lam_init, aq_ref, ak_ref, av_ref, ck_ref, cv_ref, ga_ref, lam_ref, dg_ref, zin_ref, za_ref):
    del zin_ref
    lam = _diff_lambda(lam_ref, lam_init)
    a_scale = A_QK ** -0.5
    aq = aq_ref[...]
    outs = []
    for h in range(A_HEADS):
        c = h * HD
        ws = []
        for m in range(2):
            q = aq[:, c + m * A_QK:c + (m + 1) * A_QK]
            kc = ck_ref[0, 0, h, m].astype(BF16)
            kl = ak_ref[:, c + m * A_QK:c + (m + 1) * A_QK]
            es, r = _softmax_parts([_dot_nt(q, kc) * a_scale, _dot_nt(q, kl) * a_scale])
            ws.append((es, r))
        lr = lam * ws[1][1]
        w_ctx = ws[0][0][0] * ws[0][1] - ws[1][0][0] * lr
        w_loc = ws[0][0][1] * ws[0][1] - ws[1][0][1] * lr
        o = (_dot(w_ctx.astype(BF16), cv_ref[0, 0, h].astype(BF16))
             + _dot(w_loc.astype(BF16), av_ref[:, c:c + HD]))
        outs.append(_diff_norm(o, dg_ref, lam_init))
    za_ref[...] = (jnp.concatenate(outs, axis=1) * ga_ref[...].astype(F32)).astype(BF16)


def _lat_diff(proj, cache_k, cache_v, l, lam_l, dg_l, z_a):
    lam_init = 0.8 - 0.6 * math.exp(-0.3 * l)
    nq = DEC_SEQ // LA_QB
    qrow = lambda b, j: N_CTX // LA_QB + b * nq + j
    srow = lambda b, j: N_CTX // DEC_SEQ + b
    return pl.pallas_call(
        functools.partial(_lat_diff_kernel, lam_init),
        grid=(DEC_BATCH, nq),
        in_specs=[pl.BlockSpec((LA_QB, 256), lambda b, j: (qrow(b, j), COL_AQ // 256)),
                  pl.BlockSpec((DEC_SEQ, 256), lambda b, j: (srow(b, j), COL_AK // 256)),
                  pl.BlockSpec((DEC_SEQ, 256), lambda b, j: (srow(b, j), COL_AV // 256)),
                  pl.BlockSpec((1, 1, A_HEADS, 2, PAST_LEN, A_QK), lambda b, j: (b, l, 0, 0, 0, 0)),
                  pl.BlockSpec((1, 1, A_HEADS, PAST_LEN, HD), lambda b, j: (b, l, 0, 0, 0)),
                  pl.BlockSpec((LA_QB, 256), lambda b, j: (qrow(b, j), 0)),
                  pl.BlockSpec((4, A_QK), lambda b, j: (0, 0)),
                  pl.BlockSpec((1, HD), lambda b, j: (0, 0)),
                  pl.BlockSpec(memory_space=pl.ANY)],
        out_specs=pl.BlockSpec((LA_QB, BR_W), lambda b, j: (qrow(b, j), 0)),
        out_shape=jax.ShapeDtypeStruct((N_TOK, BR_W), BF16),
        input_output_aliases={8: 0},
        compiler_params=_cparams(("arbitrary", "arbitrary")),
        name=f"lat_diff_l{l}",
    )(proj, proj, proj, cache_k, cache_v, proj, lam_l, dg_l, z_a)


def _na_bias_kernel(rpb_ref, o_ref):
    l, h = pl.program_id(0), pl.program_id(1)
    base = (l * C_HEADS + h) * ((2 * NA_ROWS - 1) * (2 * NA_COLS - 1))
    qc = lax.broadcasted_iota(jnp.int32, (GRID_W, GRID_W), 0)
    kc = lax.broadcasted_iota(jnp.int32, (GRID_W, GRID_W), 1)
    cs = jnp.clip(qc - NA_COLS // 2, 0, GRID_W - NA_COLS)
    ok = (kc >= cs) & (kc < cs + NA_COLS)
    dcol = kc - qc + (NA_COLS - 1)
    neg = jnp.full((GRID_W, GRID_W), NEG, F32)
    tabs = []
    for d in range(2 * NA_ROWS - 1):
        t = neg
        for dc in range(2 * NA_COLS - 1):
            t = jnp.where(ok & (dcol == dc), rpb_ref[base + d * (2 * NA_COLS - 1) + dc], t)
        tabs.append(t)
    for half in range(2):
        for rq in range(NA_ROWS):
            r = half * NA_ROWS + rq
            rs = min(max(r - NA_ROWS // 2, 0), GRID_ROWS - NA_ROWS)
            blks = []
            for kr in range(NA_KROWS):
                kabs = half * 4 + kr
                blks.append(tabs[kabs - r + NA_ROWS - 1] if rs <= kabs < rs + NA_ROWS else neg)
            o_ref[0, 0, half, rq * GRID_W:(rq + 1) * GRID_W, :] = jnp.concatenate(blks, axis=1)


def _na_bias(na_rpb):
    return pl.pallas_call(
        _na_bias_kernel,
        grid=(DEPTH, C_HEADS),
        in_specs=[pl.BlockSpec(memory_space=pltpu.SMEM)],
        out_specs=pl.BlockSpec((1, 1, 2, NA_HALF_Q, NA_KSPAN), lambda l, h: (l, h, 0, 0, 0)),
        out_shape=jax.ShapeDtypeStruct((DEPTH, C_HEADS, 2, NA_HALF_Q, NA_KSPAN), F32),
        compiler_params=_cparams(("arbitrary", "arbitrary")),
        name="na_bias",
    )(na_rpb.reshape(-1))


def _lat_na_kernel(cq_ref, ck_ref, cv_ref, kc_ref, vc_ref, bias_ref, gc_ref, zin_ref, zc_ref):
    del zin_ref
    half = pl.program_id(0)
    h_scale = HD ** -0.5
    k0 = pl.multiple_of(half * NA_HALF_KSTART, NA_HALF_KSTART)
    cq = cq_ref[...]
    kl_all = ck_ref[pl.ds(k0, NA_KSPAN), :]
    vl_all = cv_ref[pl.ds(k0, NA_KSPAN), :]
    outs = []
    for h in range(C_HEADS):
        c = h * HD
        q = cq[:, c:c + HD]
        s_ctx = _dot_nt(q, kc_ref[0, 0, h].astype(BF16)) * h_scale
        s_loc = _dot_nt(q, kl_all[:, c:c + HD]) * h_scale + bias_ref[0, h, 0]
        (e_ctx, e_loc), r = _softmax_parts([s_ctx, s_loc])
        o = _dot(e_ctx.astype(BF16), vc_ref[0, 0, h].astype(BF16)) + _dot(e_loc.astype(BF16), vl_all[:, c:c + HD])
        outs.append(o * r)
    zc_ref[...] = (jnp.concatenate(outs, axis=1) * gc_ref[...].astype(F32)).astype(BF16)


def _lat_na(proj, cache_k, cache_v, bias, l, z_c):
    qrow = lambda hf, b: N_CTX // NA_HALF_Q + b * 2 + hf
    srow = lambda hf, b: N_CTX // DEC_SEQ + b
    return pl.pallas_call(
        _lat_na_kernel,
        grid=(2, DEC_BATCH),
        in_specs=[pl.BlockSpec((NA_HALF_Q, 256), lambda hf, b: (qrow(hf, b), COL_CQ // 256)),
                  pl.BlockSpec((DEC_SEQ, 256), lambda hf, b: (srow(hf, b), COL_CK // 256)),
                  pl.BlockSpec((DEC_SEQ, 256), lambda hf, b: (srow(hf, b), COL_CV // 256)),
                  pl.BlockSpec((1, 1, C_HEADS, PAST_LEN, HD), lambda hf, b: (b, l, 0, 0, 0)),
                  pl.BlockSpec((1, 1, C_HEADS, PAST_LEN, HD), lambda hf, b: (b, l, 0, 0, 0)),
                  pl.BlockSpec((1, C_HEADS, 1, NA_HALF_Q, NA_KSPAN), lambda hf, b: (l, 0, hf, 0, 0)),
                  pl.BlockSpec((NA_HALF_Q, 256), lambda hf, b: (qrow(hf, b), 2)),
                  pl.BlockSpec(memory_space=pl.ANY)],
        out_specs=pl.BlockSpec((NA_HALF_Q, BR_W), lambda hf, b: (qrow(hf, b), 0)),
        out_shape=jax.ShapeDtypeStruct((N_TOK, BR_W), BF16),
        input_output_aliases={7: 0},
        compiler_params=_cparams(("arbitrary", "arbitrary")),
        name=f"lat_na_l{l}",
    )(proj, proj, proj, cache_k, cache_v, bias, proj, z_c)


def _lat_swa_kernel(dq_ref, dkv_ref, kc_ref, vc_ref, sink_ref, gd_ref, zin_ref, zd_ref):
    del zin_ref
    j = pl.program_id(1)
    h_scale = HD ** -0.5
    start = pl.multiple_of(jnp.clip(j * SWA_QB - WIN, 0, DEC_SEQ - SWA_SPAN), SWA_QB)
    dq = dq_ref[...]
    kvl = dkv_ref[pl.ds(start, SWA_SPAN), :]
    sink = sink_ref[...]
    rows = lax.broadcasted_iota(jnp.int32, (D_GROUP * SWA_QB, 1), 0)
    qi = j * SWA_QB + lax.broadcasted_iota(jnp.int32, (D_GROUP * SWA_QB, SWA_SPAN), 0) % SWA_QB
    ki = start + lax.broadcasted_iota(jnp.int32, (D_GROUP * SWA_QB, SWA_SPAN), 1)
    ok = jnp.abs(qi - ki) <= WIN
    outs = []
    for kv in range(D_KV):
        q2 = jnp.concatenate([dq[:, (kv * D_GROUP + g) * HD:(kv * D_GROUP + g + 1) * HD]
                              for g in range(D_GROUP)], axis=0)
        sk = jnp.where(rows < SWA_QB, sink[:, kv * D_GROUP:kv * D_GROUP + 1],
                       sink[:, kv * D_GROUP + 1:kv * D_GROUP + 2])
        s_ctx = _dot_nt(q2, kc_ref[0, 0, kv].astype(BF16)) * h_scale
        s_loc = jnp.where(ok, _dot_nt(q2, kvl[:, kv * HD:(kv + 1) * HD]) * h_scale, NEG)
        (e_ctx, e_loc), r = _softmax_parts([s_ctx, s_loc], extra=sk)
        o = (_dot(e_ctx.astype(BF16), vc_ref[0, 0, kv].astype(BF16))
             + _dot(e_loc.astype(BF16), kvl[:, D_KV * HD + kv * HD:D_KV * HD + (kv + 1) * HD])) * r
        outs += [o[g * SWA_QB:(g + 1) * SWA_QB] for g in range(D_GROUP)]
    zd_ref[...] = (jnp.concatenate(outs, axis=1) * gd_ref[...].astype(F32)).astype(BF16)


def _lat_swa(proj, cache_k, cache_v, sink_l, l, z_d):
    nq = DEC_SEQ // SWA_QB
    qrow = lambda b, j: N_CTX // SWA_QB + b * nq + j
    srow = lambda b, j: N_CTX // DEC_SEQ + b
    return pl.pallas_call(
        _lat_swa_kernel,
        grid=(DEC_BATCH, nq),
        in_specs=[pl.BlockSpec((SWA_QB, 256), lambda b, j: (qrow(b, j), COL_DQ // 256)),
                  pl.BlockSpec((DEC_SEQ, 256), lambda b, j: (srow(b, j), COL_DKV // 256)),
                  pl.BlockSpec((1, 1, D_KV, PAST_LEN, HD), lambda b, j: (b, l, 0, 0, 0)),
                  pl.BlockSpec((1, 1, D_KV, PAST_LEN, HD), lambda b, j: (b, l, 0, 0, 0)),
                  pl.BlockSpec((1, D_HEADS), lambda b, j: (0, 0)),
                  pl.BlockSpec((SWA_QB, 256), lambda b, j: (qrow(b, j), 3)),
                  pl.BlockSpec(memory_space=pl.ANY)],
        out_specs=pl.BlockSpec((SWA_QB, BR_W), lambda b, j: (qrow(b, j), 0)),
        out_shape=jax.ShapeDtypeStruct((N_TOK, BR_W), BF16),
        input_output_aliases={6: 0},
        compiler_params=_cparams(("arbitrary", "arbitrary")),
        name=f"lat_swa_l{l}",
    )(proj, proj, cache_k, cache_v, sink_l, proj, z_d)


def _lru_kernel(seq, has_state, *refs):
    if has_state:
        bx_ref, cw_ref, cb_ref, wg_ref, bg_ref, lam_ref, gr_ref, st_ref, zin_ref, zr_ref, a_s, u_s, y_s = refs
        del zin_ref
        fin_ref = None
    else:
        bx_ref, cw_ref, cb_ref, wg_ref, bg_ref, lam_ref, gr_ref = refs[:7]
        zr_ref, fin_ref, a_s, u_s, y_s = refs[-5:]
    x = bx_ref[...]
    t = lax.broadcasted_iota(jnp.int32, (seq, 1), 0)
    cw = cw_ref[...]
    xm1 = jnp.where(t >= 1, pltpu.roll(x, 1, 0), 0.0)
    xp1 = jnp.where(t < seq - 1, pltpu.roll(x, seq - 1, 0), 0.0)
    xp2 = jnp.where(t < seq - 2, pltpu.roll(x, seq - 2, 0), 0.0)
    xc = cb_ref[...] + xm1 * cw[0:1] + x * cw[1:2] + xp1 * cw[2:3] + xp2 * cw[3:4]

    gates = _dot(xc.astype(BF16), wg_ref[...]) + bg_ref[...]
    lam = lam_ref[...]
    nl = -lam
    softplus = jnp.maximum(nl, 0.0) + jnp.log1p(jnp.exp(-jnp.abs(nl)))
    tin = t % 8
    for d in range(2):
        r = jax.nn.sigmoid(gates[:, (2 * d) * BR_W:(2 * d + 1) * BR_W])
        ig = jax.nn.sigmoid(gates[:, (2 * d + 1) * BR_W:(2 * d + 2) * BR_W])
        log_a = -LRU_C * r * softplus[d:d + 1]
        a = jnp.exp(log_a)
        u = jnp.sqrt(-jnp.tanh(log_a) * (a * a + 1.0)) * (ig * xc)
        for s in (1, 2, 4):
            if d == 0:
                keep = tin >= s
                a_n = jnp.where(keep, pltpu.roll(a, s, 0), 1.0)
                u_n = jnp.where(keep, pltpu.roll(u, s, 0), 0.0)
            else:
                keep = tin < 8 - s
                a_n = jnp.where(keep, pltpu.roll(a, seq - s, 0), 1.0)
                u_n = jnp.where(keep, pltpu.roll(u, seq - s, 0), 0.0)
            u = u + a * u_n
            a = a * a_n
        a_s[d] = a
        u_s[d] = u

    if has_state:
        hf0 = st_ref[0, 0, 0:1, :]
        hb0 = st_ref[0, 0, 1:2, :]
    else:
        hf0 = jnp.zeros((1, BR_W), F32)
        hb0 = jnp.zeros((1, BR_W), F32)

    nt = seq // 8

    def body(k, carry):
        hf, hb = carry
        rf = pl.ds(pl.multiple_of(k * 8, 8), 8)
        rb = pl.ds(pl.multiple_of((nt - 1 - k) * 8, 8), 8)
        yf = u_s[0, rf, :] + a_s[0, rf, :] * hf
        yb = u_s[1, rb, :] + a_s[1, rb, :] * hb
        y_s[0, rf, :] = yf
        y_s[1, rb, :] = yb
        return yf[7:8, :], yb[0:1, :]

    hf, hb = lax.fori_loop(0, nt, body, (hf0, hb0))
    zr_ref[...] = ((y_s[0] + y_s[1]) * gr_ref[...].astype(F32)).astype(BF16)
    if fin_ref is not None:
        fin_ref[0, 0, 0:1, :] = hf
        fin_ref[0, 0, 1:2, :] = hb


def _lru(seq, nseq, row0, bx, proj, cw_l, cb_l, wg_l, bg_l, lam_l, state=None, l=0, z_r=None, states=None):
    has_state = state is not None
    rblk = lambda b: (row0 // seq + b, 0)
    in_specs = [pl.BlockSpec((seq, BR_W), rblk),
                pl.BlockSpec((CONV_W, BR_W), lambda b: (0, 0)),
                pl.BlockSpec((1, BR_W), lambda b: (0, 0)),
                pl.BlockSpec((BR_W, 4 * BR_W), lambda b: (0, 0)),
                pl.BlockSpec((1, 4 * BR_W), lambda b: (0, 0)),
                pl.BlockSpec((2, BR_W), lambda b: (0, 0)),
                pl.BlockSpec((seq, 256), lambda b: (row0 // seq + b, 1))]
    args = [bx, cw_l, cb_l, wg_l, bg_l, lam_l, proj]
    zspec = pl.BlockSpec((seq, BR_W), rblk)
    zshape = jax.ShapeDtypeStruct((N_TOK, BR_W), BF16)
    scratch = [pltpu.VMEM((2, seq, BR_W), F32)] * 3
    if has_state:
        in_specs += [pl.BlockSpec((1, 1, 2, BR_W), lambda b: (b, l, 0, 0)), pl.BlockSpec(memory_space=pl.ANY)]
        args += [state, z_r]
        return pl.pallas_call(
            functools.partial(_lru_kernel, seq, True),
            grid=(nseq,), in_specs=in_specs, out_specs=zspec, out_shape=zshape,
            scratch_shapes=scratch, input_output_aliases={8: 0},
            compiler_params=_cparams(("arbitrary",)), name=f"lru_lat_l{l}",
        )(*args)
    aliases = {}
    if states is not None:
        in_specs += [pl.BlockSpec(memory_space=pl.ANY)]
        args += [states]
        aliases = {7: 1}
    return pl.pallas_call(
        functools.partial(_lru_kernel, seq, False),
        grid=(nseq,), in_specs=in_specs,
        out_specs=[zspec, pl.BlockSpec((1, 1, 2, BR_W), lambda b: (b, l, 0, 0))],
        out_shape=[zshape, jax.ShapeDtypeStruct((nseq, DEPTH, 2, BR_W), F32)],
        scratch_shapes=scratch, input_output_aliases=aliases,
        compiler_params=_cparams(("arbitrary",)), name=f"lru_ctx_l{l}",
    )(*args)


def _merge_kernel(final, xa_ref, xb_ref, mod_ref, h_ref, za_ref, zr_ref, zc_ref, zd_ref,
                  wmg_ref, bmg_ref, wbo_ref, wo_ref, nf_ref, *refs):
    out_refs, (wmg_s, wbo_s, wo_s) = refs[:-3], refs[-3:]
    i = pl.program_id(0)

    @pl.when(i < W_STEPS)
    def _():
        _cast_rows(i, wmg_ref, wmg_s)
        _cast_rows(i, wbo_ref, wbo_s)
        _cast_rows(i, wo_ref, wo_s)

    @pl.when(i >= W_STEPS)
    def _():
        t = i - W_STEPS
        x = _load_x(t, xa_ref, xb_ref)
        gate = mod_ref[0][:, 2 * D_MODEL:]
        h = h_ref[...]
        zs = [za_ref[...], zr_ref[...], zc_ref[...], zd_ref[...]]
        bmg = bmg_ref[0]
        cols = []
        for c in range(0, D_MODEL, 512):
            acc = None
            for n in range(N_BRANCH):
                g = jax.nn.sigmoid(_dot(h, wmg_s[:, n * D_MODEL + c:n * D_MODEL + c + 512])
                                   + bmg[:, n * D_MODEL + c:n * D_MODEL + c + 512])
                term = g * _dot(zs[n], wbo_s[n * BR_W:(n + 1) * BR_W, c:c + 512])
                acc = term if acc is None else acc + term
            cols.append(acc.astype(BF16))
        merged = jnp.concatenate(cols, axis=1)
        xn = x + gate * _dot(merged, wo_s[...])
        if not final:
            out_refs[0][...] = xn
        else:
            y = xn * lax.rsqrt(jnp.mean(xn * xn, axis=-1, keepdims=True) + EPS) * nf_ref[...]

            @pl.when(t < N_CTX_TILES)
            def _():
                out_refs[0][...] = y

            @pl.when(t >= N_CTX_TILES)
            def _():
                out_refs[1][...] = y


def _merge(final, xa, xb, xb_off, mod3, l, h, zs, w_mg, b_mg, w_bo, w_o, norm_f):
    row = lambda i: (_tile(i), 0)
    ctx_blk = lambda i: jnp.minimum(_tile(i), N_CTX_TILES - 1)
    lat_blk = lambda i: jnp.maximum(_tile(i) - N_CTX_TILES, 0)
    if final:
        out_specs = [pl.BlockSpec((TM, D_MODEL), lambda i: (ctx_blk(i), 0)),
                     pl.BlockSpec((TM, D_MODEL), lambda i: (lat_blk(i), 0))]
        out_shape = [jax.ShapeDtypeStruct((N_CTX, D_MODEL), F32), jax.ShapeDtypeStruct((N_LAT, D_MODEL), F32)]
    else:
        out_specs = pl.BlockSpec((TM, D_MODEL), row)
        out_shape = jax.ShapeDtypeStruct((N_TOK, D_MODEL), F32)
    wchunk = lambda width: pl.BlockSpec((1, W_ROWS, width), lambda i: (l, _wchunk(i), 0))
    return pl.pallas_call(
        functools.partial(_merge_kernel, final),
        grid=(W_STEPS + N_TILES,),
        in_specs=[
            pl.BlockSpec((TM, D_MODEL), lambda i: (ctx_blk(i), 0)),
            pl.BlockSpec((TM, D_MODEL), lambda i: (lat_blk(i) + xb_off, 0)),
            pl.BlockSpec((1, 1, 3 * D_MODEL), lambda i: (l * 16 + _mod_row(_tile(i)), 0, 0)),
            pl.BlockSpec((TM, D_MODEL), row),
            pl.BlockSpec((TM, BR_W), row), pl.BlockSpec((TM, BR_W), row),
            pl.BlockSpec((TM, BR_W), row), pl.BlockSpec((TM, BR_W), row),
            wchunk(N_BRANCH * D_MODEL),
            pl.BlockSpec((1, 1, N_BRANCH * D_MODEL), lambda i: (l, 0, 0)),
            wchunk(D_MODEL),
            wchunk(D_MODEL),
            pl.BlockSpec((1, D_MODEL), lambda i: (0, 0)),
        ],
        out_specs=out_specs,
        out_shape=out_shape,
        scratch_shapes=[pltpu.VMEM((D_MODEL, N_BRANCH * D_MODEL), BF16),
                        pltpu.VMEM((N_BRANCH * BR_W, D_MODEL), BF16),
                        pltpu.VMEM((D_MODEL, D_MODEL), BF16)],
        compiler_params=_cparams(("arbitrary",)),
        name=f"merge_l{l}",
    )(xa, xb, mod3, h, *zs,
      w_mg.reshape(DEPTH, D_MODEL, N_BRANCH * D_MODEL), b_mg.reshape(DEPTH, 1, N_BRANCH * D_MODEL),
      w_bo.reshape(DEPTH, N_BRANCH * BR_W, D_MODEL), w_o, norm_f.reshape(1, D_MODEL))


def _lru_gate_weights(lru_wa, lru_ba, lru_wx, lru_bx):
    w = jnp.stack([lru_wa[:, 0], lru_wx[:, 0], lru_wa[:, 1], lru_wx[:, 1]], axis=1)
    eye = jnp.eye(B_BLOCKS, dtype=w.dtype)
    dense = w[:, :, :, :, None, :] * eye[None, None, :, None, :, None]
    wg = dense.transpose(0, 2, 3, 1, 4, 5).reshape(DEPTH, BR_W, 4 * BR_W).astype(BF16)
    bg = jnp.stack([lru_ba[:, 0], lru_bx[:, 0], lru_ba[:, 1], lru_bx[:, 1]], axis=1).reshape(DEPTH, 1, 4 * BR_W)
    return wg, bg


def kernel(x_prompt, x_sample, cache_diff_k, cache_diff_v, cache_na_k, cache_na_v, cache_swa_k, cache_swa_v,
           state_lru, c, c_ctx, norm_g, w_ada, b_ada, w_in, diff_lambda, diff_norm_g, conv_w, conv_b,
           lru_wa, lru_ba, lru_wx, lru_bx, lru_lam, na_rpb, swa_sink, w_mg, b_mg, w_bo, w_o, norm_f):
    tab_a_np, _ = _rope_tables(A_QK)
    tab_d_np, _ = _rope_tables(HD)
    tab_a, tab_d = jnp.asarray(tab_a_np), jnp.asarray(tab_d_np)

    cvecs = jnp.concatenate([c_ctx[None, :], c, jnp.zeros((16 - 1 - DEC_BATCH, D_MODEL), F32)], axis=0)
    mod3 = _modulation(cvecs, w_ada, b_ada).reshape(DEPTH * 16, 1, 3 * D_MODEL)
    bias = _na_bias(na_rpb)
    wg, bg = _lru_gate_weights(lru_wa, lru_ba, lru_wx, lru_bx)
    norm_g3 = norm_g.reshape(DEPTH, 1, D_MODEL)

    xa = x_prompt.reshape(N_CTX, D_MODEL)
    xb = x_sample.reshape(N_LAT, D_MODEL)
    xb_off = 0
    caches = states = None
    y_p = y_s = None
    for l in range(DEPTH):
        h, proj, bx, caches = _inproj(xa, xb, xb_off, mod3, l, norm_g3, w_in, tab_a, tab_d, caches)

        lam_l, dg_l, sink_l = diff_lambda[l], diff_norm_g[l][None, :], swa_sink[l][None, :]
        z_a, z_c, z_d = _ctx_attn(proj, l, lam_l, dg_l, sink_l)
        z_a = _lat_diff(proj, cache_diff_k, cache_diff_v, l, lam_l, dg_l, z_a)
        z_c = _lat_na(proj, cache_na_k, cache_na_v, bias, l, z_c)
        z_d = _lat_swa(proj, cache_swa_k, cache_swa_v, sink_l, l, z_d)

        lru_args = (bx, proj, conv_w[l], conv_b[l][None, :], wg[l], bg[l], lru_lam[l])
        z_r, states = _lru(SEQ, BATCH, 0, *lru_args, l=l, states=states)
        z_r = _lru(DEC_SEQ, DEC_BATCH, N_CTX, *lru_args, state=state_lru, l=l, z_r=z_r)

        final = l == DEPTH - 1
        out = _merge(final, xa, xb, xb_off, mod3, l, h, (z_a, z_r, z_c, z_d), w_mg, b_mg, w_bo, w_o, norm_f)
        if final:
            y_p, y_s = out
        else:
            xa = xb = out
            xb_off = N_CTX_TILES

    return (y_p.reshape(BATCH, SEQ, D_MODEL), y_s.reshape(DEC_BATCH, DEC_SEQ, D_MODEL), *caches, states)
```

```python
import functools
import math

import numpy as np
import jax
import jax.numpy as jnp
from jax import lax
from jax.experimental import pallas as pl
from jax.experimental.pallas import tpu as pltpu

F32 = jnp.float32
BF16 = jnp.bfloat16

D_MODEL = 1024
BATCH = 16
SEQ = 256
DEPTH = 2
DEC_BATCH = 8
DEC_SEQ = 1024
PAST_LEN = 512
GRID_W = 64
N_BRANCH = 4
BR_W = D_MODEL // 4
HD = 64
A_HEADS = BR_W // HD
A_QK = HD // 2
B_BLOCKS = 4
B_BLK = BR_W // B_BLOCKS
CONV_W = 4
LRU_C = 8.0
C_HEADS = BR_W // HD
NA_ROWS = 8
NA_COLS = 16
D_HEADS = BR_W // HD
D_KV = 2
D_GROUP = D_HEADS // D_KV
WIN = 128
ROPE_BASE = 10000.0
EPS = 1e-6
NEG = -1e30

N_CTX = BATCH * SEQ
N_LAT = DEC_BATCH * DEC_SEQ
N_TOK = N_CTX + N_LAT
GRID_ROWS = DEC_SEQ // GRID_W

TM = 512
N_CTX_TILES = N_CTX // TM
N_TILES = N_TOK // TM
LAT_TILES_PER_SEQ = DEC_SEQ // TM

PROJ_W = 3072
COL_G = 0
COL_AQ = 1024
COL_AK = 1280
COL_AV = 1536
COL_CQ = 1792
COL_CK = 2048
COL_CV = 2304
COL_DQ = 2560
COL_DKV = 2816
COL_BX = 3072
IN_W = 3328
CACHE_W = 1280

NA_HALF_Q = DEC_SEQ // 2
NA_KROWS = 12
NA_KSPAN = NA_KROWS * GRID_W
NA_HALF_KSTART = 4 * GRID_W

SWA_QB = 128
SWA_SPAN = SWA_QB + 2 * WIN

VMEM_LIMIT = 56 * 1024 * 1024


def _cparams(sem):
    return pltpu.CompilerParams(dimension_semantics=sem, vmem_limit_bytes=VMEM_LIMIT)


def _resident(shape, index_map):
    return pl.BlockSpec(shape, index_map, pipeline_mode=pl.Buffered(1))


def _dot(a, b):
    return jnp.dot(a, b, preferred_element_type=F32)


def _dot_nt(a, b):
    return lax.dot_general(a, b, (((1,), (1,)), ((), ())), preferred_element_type=F32)


def _silu(x):
    return x * jax.nn.sigmoid(x)


def _rope_tables(d):
    half = d // 2
    quarter = half // 2
    lane = np.arange(128)
    q = lane % d
    use_col = (q >= half)
    i = (q % half) % quarter
    first = (q % half) < quarter
    inv = (ROPE_BASE ** (-np.arange(quarter, dtype=np.float32) / np.float32(quarter))).astype(np.float32)
    t = np.arange(DEC_SEQ)
    pos = np.where(use_col[None, :], (t % GRID_W)[:, None], (t // GRID_W)[:, None]).astype(np.float32)
    ang = (pos * inv[i][None, :]).astype(np.float32)
    cos, sin = np.cos(ang), np.sin(ang)
    c = np.concatenate([np.ones((TM, 128), np.float32), cos], axis=0)
    s1 = np.concatenate([np.zeros((TM, 128), np.float32), np.where(first[None, :], -sin, 0.0)], axis=0)
    s2 = np.concatenate([np.zeros((TM, 128), np.float32), np.where(first[None, :], 0.0, sin)], axis=0)
    return np.stack([c, s1, s2]).astype(np.float32), quarter


def _rope128(x, tab_ref, shift):
    return (x * tab_ref[0] + pltpu.roll(x, 128 - shift, 1) * tab_ref[1]
            + pltpu.roll(x, shift, 1) * tab_ref[2])


def _rope(x, tab_ref, shift):
    w = x.shape[1]
    return jnp.concatenate([_rope128(x[:, c:c + 128], tab_ref, shift) for c in range(0, w, 128)], axis=1)


def _mod_kernel(c_ref, w_ref, b_ref, o_ref):
    c = c_ref[...]
    o_ref[0] = _dot(_silu(c).astype(BF16), w_ref[0].astype(BF16)) + b_ref[0]


def _modulation(cvecs, w_ada, b_ada):
    nb = 3 * D_MODEL // 768
    return pl.pallas_call(
        _mod_kernel,
        grid=(DEPTH, nb),
        in_specs=[pl.BlockSpec((16, D_MODEL), lambda l, j: (0, 0)),
                  pl.BlockSpec((1, D_MODEL, 768), lambda l, j: (l, 0, j)),
                  pl.BlockSpec((1, 1, 768), lambda l, j: (l, 0, j))],
        out_specs=pl.BlockSpec((1, 16, 768), lambda l, j: (l, 0, j)),
        out_shape=jax.ShapeDtypeStruct((DEPTH, 16, 3 * D_MODEL), F32),
        compiler_params=_cparams(("arbitrary", "arbitrary")),
        name="modulation",
    )(cvecs, w_ada, b_ada.reshape(DEPTH, 1, 3 * D_MODEL))


W_STEPS = 8
W_ROWS = D_MODEL // W_STEPS


def _tile(i):
    return jnp.maximum(i - W_STEPS, 0)


def _wchunk(i):
    return jnp.minimum(i, W_STEPS - 1)


def _mod_row(t):
    return jnp.where(t < N_CTX_TILES, 0, 1 + (t - N_CTX_TILES) // LAT_TILES_PER_SEQ)


def _rope_blk(t):
    return jnp.where(t < N_CTX_TILES, 0, 1 + (t - N_CTX_TILES) % LAT_TILES_PER_SEQ)


def _load_x(t, xa_ref, xb_ref):
    return jnp.where(t < N_CTX_TILES, xa_ref[...], xb_ref[...])


def _cast_rows(i, src_ref, dst_ref):
    r0 = pl.multiple_of(i * W_ROWS, W_ROWS)
    dst_ref[pl.ds(r0, W_ROWS), :] = src_ref[0].astype(BF16)


WI_G, WI_AQ, WI_AK, WI_AV, WI_BX, WI_CQ, WI_CKV, WI_DQ, WI_DKV = 0, 1024, 1280, 1536, 1792, 2048, 2304, 2816, 3072
SEQ_PER_TILE = TM // SEQ


def _store_heads(ref, val, n_heads, width):
    vt = val.T
    for s in range(SEQ_PER_TILE):
        for h in range(n_heads):
            ref[s, 0, h] = vt[h * width:(h + 1) * width, s * SEQ:(s + 1) * SEQ]


def _inproj_kernel(n_alias, xa_ref, xb_ref, mod_ref, g_ref, w_ref, ta_ref, td_ref, *refs):
    h_ref, proj_ref, bx_ref, dk_ref, dv_ref, nk_ref, nv_ref, sk_ref, sv_ref, w_s = refs[n_alias:]
    i = pl.program_id(0)

    @pl.when(i < W_STEPS)
    def _():
        _cast_rows(i, w_ref, w_s)

    @pl.when(i >= W_STEPS)
    def _():
        t = i - W_STEPS
        is_ctx = t < N_CTX_TILES
        x = _load_x(t, xa_ref, xb_ref)
        mod = mod_ref[0]
        shift, scale = mod[:, :D_MODEL], mod[:, D_MODEL:2 * D_MODEL]
        xn = x * lax.rsqrt(jnp.mean(x * x, axis=-1, keepdims=True) + EPS) * g_ref[0]
        hb =(xn * (1.0 + scale) + shift).astype(BF16)
        h_ref[...] = hb

        def mm(c0, width):
            return _dot(hb, w_s[:, c0:c0 + width])

        for c in range(0, 1024, 512):
            proj_ref[:, COL_G + c:COL_G + c + 512] = _silu(mm(WI_G + c, 512)).astype(BF16)

        aq = mm(WI_AQ, 256)
        proj_ref[:, COL_AQ:COL_AQ + 256] = _rope(aq, ta_ref, A_QK // 4).astype(BF16)
        ak = mm(WI_AK, 256)
        proj_ref[:, COL_AK:COL_AK + 256] = _rope(ak, ta_ref, A_QK // 4).astype(BF16)
        av = mm(WI_AV, 256)
        proj_ref[:, COL_AV:COL_AV + 256] = av.astype(BF16)

        @pl.when(is_ctx)
        def _():
            akt = ak.T
            for s in range(SEQ_PER_TILE):
                for h in range(A_HEADS):
                    for m in range(2):
                        c0 = h * HD + m * A_QK
                        dk_ref[s, 0, h, m] = akt[c0:c0 + A_QK, s * SEQ:(s + 1) * SEQ]
            _store_heads(dv_ref, av, A_HEADS, HD)

        cq = mm(WI_CQ, 256)
        proj_ref[:, COL_CQ:COL_CQ + 256] = cq.astype(BF16)
        ckv = mm(WI_CKV, 512)
        proj_ref[:, COL_CK:COL_CK + 512] = ckv.astype(BF16)

        @pl.when(is_ctx)
        def _():
            _store_heads(nk_ref, ckv[:, :256], C_HEADS, HD)
            _store_heads(nv_ref, ckv[:, 256:], C_HEADS, HD)

        dq = mm(WI_DQ, 256)
        proj_ref[:, COL_DQ:COL_DQ + 256] = _rope(dq, td_ref, HD // 4).astype(BF16)
        dkv = mm(WI_DKV, 256)
        proj_ref[:, COL_DKV:COL_DKV + 128] = _rope(dkv[:, :128], td_ref, HD // 4).astype(BF16)
        proj_ref[:, COL_DKV + 128:COL_DKV + 256] = dkv[:, 128:].astype(BF16)

        @pl.when(is_ctx)
        def _():
            _store_heads(sk_ref, dkv[:, :128], D_KV, HD)
            _store_heads(sv_ref, dkv[:, 128:], D_KV, HD)

        bx_ref[...] = mm(WI_BX, 256)


def _cache_shapes():
    hs = lambda n: (BATCH, DEPTH, n, HD, SEQ)
    return [(BATCH, DEPTH, A_HEADS, 2, A_QK, SEQ), hs(A_HEADS), hs(C_HEADS), hs(C_HEADS), hs(D_KV), hs(D_KV)]


def _inproj(xa, xb, xb_off, mod3, l, norm_g, w_in, tab_a, tab_d, caches):
    row = lambda i: (_tile(i), 0)
    ctx_blk = lambda i: jnp.minimum(_tile(i), N_CTX_TILES - 1)

    def cache_spec(shape):
        blk = (SEQ_PER_TILE, 1) + shape[2:]
        nz = len(shape) - 2
        return pl.BlockSpec(blk, lambda i: (ctx_blk(i), l) + (0,) * nz)

    cshapes = _cache_shapes()
    aliases = {} if caches is None else {7 + k: 3 + k for k in range(6)}
    extra_specs = [] if caches is None else [pl.BlockSpec(memory_space=pl.ANY)] * 6
    extra_args = [] if caches is None else list(caches)
    outs = pl.pallas_call(
        functools.partial(_inproj_kernel, len(extra_args)),
        grid=(W_STEPS + N_TILES,),
        in_specs=[
            pl.BlockSpec((TM, D_MODEL), lambda i: (ctx_blk(i), 0)),
            pl.BlockSpec((TM, D_MODEL), lambda i: (jnp.maximum(_tile(i) - N_CTX_TILES, 0) + xb_off, 0)),
            pl.BlockSpec((1, 1, 3 * D_MODEL), lambda i: (l * 16 + _mod_row(_tile(i)), 0, 0)),
            pl.BlockSpec((1, 1, D_MODEL), lambda i: (l, 0, 0)),
            pl.BlockSpec((1, W_ROWS, IN_W), lambda i: (l, _wchunk(i), 0)),
            pl.BlockSpec((3, TM, 128), lambda i: (0, _rope_blk(_tile(i)), 0)),
            pl.BlockSpec((3, TM, 128), lambda i: (0, _rope_blk(_tile(i)), 0)),
        ] + extra_specs,
        out_specs=[
            pl.BlockSpec((TM, D_MODEL), row),
            pl.BlockSpec((TM, PROJ_W), row),
            pl.BlockSpec((TM, BR_W), row),
        ] + [cache_spec(s) for s in cshapes],
        out_shape=[
            jax.ShapeDtypeStruct((N_TOK, D_MODEL), BF16),
            jax.ShapeDtypeStruct((N_TOK, PROJ_W), BF16),
            jax.ShapeDtypeStruct((N_TOK, BR_W), F32),
        ] + [jax.ShapeDtypeStruct(s, F32) for s in cshapes],
        scratch_shapes=[pltpu.VMEM((D_MODEL, IN_W), BF16)],
        input_output_aliases=aliases,
        compiler_params=_cparams(("arbitrary",)),
        name=f"inproj_l{l}",
    )(xa, xb, mod3, norm_g, w_in, tab_a, tab_d, *extra_args)
    return outs[0], outs[1], outs[2], tuple(outs[3:])


def _diff_lambda(lam_ref, lam_init):
    lv = lam_ref[...]
    s1 = jnp.sum(lv[0:1] * lv[1:2], axis=-1, keepdims=True)
    s2 = jnp.sum(lv[2:3] * lv[3:4], axis=-1, keepdims=True)
    return jnp.exp(s1) - jnp.exp(s2) + lam_init


def _diff_norm(o, dg_ref, lam_init):
    y = o * lax.rsqrt(jnp.mean(o * o, axis=-1, keepdims=True) + EPS) * dg_ref[...]
    return y * (1.0 - lam_init)


def _softmax_parts(parts, extra=None):
    m = functools.reduce(jnp.maximum, [jnp.max(s, axis=-1, keepdims=True) for s in parts])
    if extra is not None:
        m = jnp.maximum(m, extra)
    es = [jnp.exp(s - m) for s in parts]
    tot = functools.reduce(lambda a, b: a + b, [jnp.sum(e, axis=-1, keepdims=True) for e in es])
    if extra is not None:
        tot = tot + jnp.exp(extra - m)
    return es, 1.0 / tot


def _ctx_attn_kernel(lam_init, aq_ref, ak_ref, av_ref, cq_ref, ck_ref, cv_ref, dq_ref, dkv_ref,
                     ga_ref, gc_ref, gd_ref, lam_ref, dg_ref, sink_ref, za_ref, zc_ref, zd_ref):
    lam = _diff_lambda(lam_ref, lam_init)
    a_scale = A_QK ** -0.5
    h_scale = HD ** -0.5

    aq, ak, av = aq_ref[...], ak_ref[...], av_ref[...]
    outs = []
    for h in range(A_HEADS):
        c = h * HD
        ws = []
        for m in range(2):
            q = aq[:, c + m * A_QK:c + (m + 1) * A_QK]
            k = ak[:, c + m * A_QK:c + (m + 1) * A_QK]
            (e,), r = _softmax_parts([_dot_nt(q, k) * a_scale])
            ws.append((e, r))
        w = ws[0][0] * ws[0][1] - ws[1][0] * (lam * ws[1][1])
        o = _dot(w.astype(BF16), av[:, c:c + HD])
        outs.append(_diff_norm(o, dg_ref, lam_init))
    za_ref[...] = (jnp.concatenate(outs, axis=1) * ga_ref[...].astype(F32)).astype(BF16)

    cq, ck, cv = cq_ref[...], ck_ref[...], cv_ref[...]
    outs = []
    for h in range(C_HEADS):
        c = h * HD
        (e,), r = _softmax_parts([_dot_nt(cq[:, c:c + HD], ck[:, c:c + HD]) * h_scale])
        outs.append(_dot(e.astype(BF16), cv[:, c:c + HD]) * r)
    zc_ref[...] = (jnp.concatenate(outs, axis=1) * gc_ref[...].astype(F32)).astype(BF16)

    dq, dkv = dq_ref[...], dkv_ref[...]
    sink = sink_ref[...]
    rows = lax.broadcasted_iota(jnp.int32, (D_GROUP * SEQ, 1), 0)
    outs = []
    for kv in range(D_KV):
        q2 = jnp.concatenate([dq[:, (kv * D_GROUP + g) * HD:(kv * D_GROUP + g + 1) * HD]
                              for g in range(D_GROUP)], axis=0)
        k = dkv[:, kv * HD:(kv + 1) * HD]
        v = dkv[:, D_KV * HD + kv * HD:D_KV * HD + (kv + 1) * HD]
        sk = jnp.where(rows < SEQ, sink[:, kv * D_GROUP:kv * D_GROUP + 1],
                       sink[:, kv * D_GROUP + 1:kv * D_GROUP + 2])
        (e,), r = _softmax_parts([_dot_nt(q2, k) * h_scale], extra=sk)
        o = _dot(e.astype(BF16), v) * r
        outs += [o[g * SEQ:(g + 1) * SEQ] for g in range(D_GROUP)]
    zd_ref[...] = (jnp.concatenate(outs, axis=1) * gd_ref[...].astype(F32)).astype(BF16)


def _ctx_attn(proj, l, lam_l, dg_l, sink_l):
    lam_init = 0.8 - 0.6 * math.exp(-0.3 * l)
    colblk = lambda cb: pl.BlockSpec((SEQ, 256), lambda b: (b, cb))
    zspec = pl.BlockSpec((SEQ, BR_W), lambda b: (b, 0))
    zshape = jax.ShapeDtypeStruct((N_TOK, BR_W), BF16)
    return pl.pallas_call(
        functools.partial(_ctx_attn_kernel, lam_init),
        grid=(BATCH,),
        in_specs=[colblk(COL_AQ // 256), colblk(COL_AK // 256), colblk(COL_AV // 256),
                  colblk(COL_CQ // 256), colblk(COL_CK // 256), colblk(COL_CV // 256),
                  colblk(COL_DQ // 256), colblk(COL_DKV // 256),
                  colblk(0), colblk(2), colblk(3),
                  pl.BlockSpec((4, A_QK), lambda b: (0, 0)),
                  pl.BlockSpec((1, HD), lambda b: (0, 0)),
                  pl.BlockSpec((1, D_HEADS), lambda b: (0, 0))],
        out_specs=[zspec, zspec, zspec],
        out_shape=[zshape, zshape, zshape],
        compiler_params=_cparams(("arbitrary",)),
        name=f"ctx_attn_l{l}",
    )(*([proj] * 11), lam_l, dg_l, sink_l)


LA_QB = 256


def _lat_diff_kernel(lam_init, aq_ref, ak_ref, av_ref, ck_ref, cv_ref, ga_ref, lam_ref, dg_ref, zin_ref, za_ref):
    del zin_ref
    lam = _diff_lambda(lam_ref, lam_init)
    a_scale = A_QK ** -0.5
    aq = aq_ref[...]
    outs = []
    for h in range(A_HEADS):
        c = h * HD
        ws = []
        for m in range(2):
            q = aq[:, c + m * A_QK:c + (m + 1) * A_QK]
            kc_t = ck_ref[0, 0, h, m].astype(BF16)
            kl = ak_ref[:, c + m * A_QK:c + (m + 1) * A_QK]
            es, r = _softmax_parts([_dot(q, kc_t) * a_scale, _dot_nt(q, kl) * a_scale])
            ws.append((es, r))
        lr = lam * ws[1][1]
        w_ctx = ws[0][0][0] * ws[0][1] - ws[1][0][0] * lr
        w_loc = ws[0][0][1] * ws[0][1] - ws[1][0][1] * lr
        o = (_dot_nt(w_ctx.astype(BF16), cv_ref[0, 0, h].astype(BF16))
             + _dot(w_loc.astype(BF16), av_ref[:, c:c + HD]))
        outs.append(_diff_norm(o, dg_ref, lam_init))
    za_ref[...] = (jnp.concatenate(outs, axis=1) * ga_ref[...].astype(F32)).astype(BF16)


def _lat_diff(proj, cache_k, cache_v, l, lam_l, dg_l, z_a):
    lam_init = 0.8 - 0.6 * math.exp(-0.3 * l)
    nq = DEC_SEQ // LA_QB
    qrow = lambda b, j: N_CTX // LA_QB + b * nq + j
    srow = lambda b, j: N_CTX // DEC_SEQ + b
    return pl.pallas_call(
        functools.partial(_lat_diff_kernel, lam_init),
        grid=(DEC_BATCH, nq),
        in_specs=[pl.BlockSpec((LA_QB, 256), lambda b, j: (qrow(b, j), COL_AQ // 256)),
                  pl.BlockSpec((DEC_SEQ, 256), lambda b, j: (srow(b, j), COL_AK // 256)),
                  pl.BlockSpec((DEC_SEQ, 256), lambda b, j: (srow(b, j), COL_AV // 256)),
                  pl.BlockSpec((1, 1, A_HEADS, 2, A_QK, PAST_LEN), lambda b, j: (b, l, 0, 0, 0, 0)),
                  pl.BlockSpec((1, 1, A_HEADS, HD, PAST_LEN), lambda b, j: (b, l, 0, 0, 0)),
                  pl.BlockSpec((LA_QB, 256), lambda b, j: (qrow(b, j), 0)),
                  pl.BlockSpec((4, A_QK), lambda b, j: (0, 0)),
                  pl.BlockSpec((1, HD), lambda b, j: (0, 0)),
                  pl.BlockSpec(memory_space=pl.ANY)],
        out_specs=pl.BlockSpec((LA_QB, BR_W), lambda b, j: (qrow(b, j), 0)),
        out_shape=jax.ShapeDtypeStruct((N_TOK, BR_W), BF16),
        input_output_aliases={8: 0},
        compiler_params=_cparams(("arbitrary", "arbitrary")),
        name=f"lat_diff_l{l}",
    )(proj, proj, proj, cache_k, cache_v, proj, lam_l, dg_l, z_a)


def _na_bias_kernel(rpb_ref, o_ref):
    l, h = pl.program_id(0), pl.program_id(1)
    base = (l * C_HEADS + h) * ((2 * NA_ROWS - 1) * (2 * NA_COLS - 1))
    qc = lax.broadcasted_iota(jnp.int32, (GRID_W, GRID_W), 0)
    kc = lax.broadcasted_iota(jnp.int32, (GRID_W, GRID_W), 1)
    cs = jnp.clip(qc - NA_COLS // 2, 0, GRID_W - NA_COLS)
    ok = (kc >= cs) & (kc < cs + NA_COLS)
    dcol = kc - qc + (NA_COLS - 1)
    neg = jnp.full((GRID_W, GRID_W), NEG, F32)
    tabs = []
    for d in range(2 * NA_ROWS - 1):
        t = neg
        for dc in range(2 * NA_COLS - 1):
            t = jnp.where(ok & (dcol == dc), rpb_ref[base + d * (2 * NA_COLS - 1) + dc], t)
        tabs.append(t)
    for half in range(2):
        for rq in range(NA_ROWS):
            r = half * NA_ROWS + rq
            rs = min(max(r - NA_ROWS // 2, 0), GRID_ROWS - NA_ROWS)
            blks = []
            for kr in range(NA_KROWS):
                kabs = half * 4 + kr
                blks.append(tabs[kabs - r + NA_ROWS - 1] if rs <= kabs < rs + NA_ROWS else neg)
            o_ref[0, 0, half, rq * GRID_W:(rq + 1) * GRID_W, :] = jnp.concatenate(blks, axis=1)


def _na_bias(na_rpb):
    return pl.pallas_call(
        _na_bias_kernel,
        grid=(DEPTH, C_HEADS),
        in_specs=[pl.BlockSpec(memory_space=pltpu.SMEM)],
        out_specs=pl.BlockSpec((1, 1, 2, NA_HALF_Q, NA_KSPAN), lambda l, h: (l, h, 0, 0, 0)),
        out_shape=jax.ShapeDtypeStruct((DEPTH, C_HEADS, 2, NA_HALF_Q, NA_KSPAN), F32),
        compiler_params=_cparams(("arbitrary", "arbitrary")),
        name="na_bias",
    )(na_rpb.reshape(-1))


def _lat_na_kernel(cq_ref, ck_ref, cv_ref, kc_ref, vc_ref, bias_ref, gc_ref, zin_ref, zc_ref):
    del zin_ref
    half = pl.program_id(0)
    h_scale = HD ** -0.5
    k0 = pl.multiple_of(half * NA_HALF_KSTART, NA_HALF_KSTART)
    cq = cq_ref[...]
    kl_all = ck_ref[pl.ds(k0, NA_KSPAN), :]
    vl_all = cv_ref[pl.ds(k0, NA_KSPAN), :]
    outs = []
    for h in range(C_HEADS):
        c = h * HD
        q = cq[:, c:c + HD]
        s_ctx = _dot(q, kc_ref[0, 0, h].astype(BF16)) * h_scale
        s_loc = _dot_nt(q, kl_all[:, c:c + HD]) * h_scale + bias_ref[0, h, 0]
        (e_ctx, e_loc), r = _softmax_parts([s_ctx, s_loc])
        o = (_dot_nt(e_ctx.astype(BF16), vc_ref[0, 0, h].astype(BF16))
             + _dot(e_loc.astype(BF16), vl_all[:, c:c + HD]))
        outs.append(o * r)
    zc_ref[...] = (jnp.concatenate(outs, axis=1) * gc_ref[...].astype(F32)).astype(BF16)


def _lat_na(proj, cache_k, cache_v, bias, l, z_c):
    qrow = lambda hf, b: N_CTX // NA_HALF_Q + b * 2 + hf
    srow = lambda hf, b: N_CTX // DEC_SEQ + b
    return pl.pallas_call(
        _lat_na_kernel,
        grid=(2, DEC_BATCH),
        in_specs=[pl.BlockSpec((NA_HALF_Q, 256), lambda hf, b: (qrow(hf, b), COL_CQ // 256)),
                  pl.BlockSpec((DEC_SEQ, 256), lambda hf, b: (srow(hf, b), COL_CK // 256)),
                  pl.BlockSpec((DEC_SEQ, 256), lambda hf, b: (srow(hf, b), COL_CV // 256)),
                  pl.BlockSpec((1, 1, C_HEADS, HD, PAST_LEN), lambda hf, b: (b, l, 0, 0, 0)),
                  pl.BlockSpec((1, 1, C_HEADS, HD, PAST_LEN), lambda hf, b: (b, l, 0, 0, 0)),
                  pl.BlockSpec((1, C_HEADS, 1, NA_HALF_Q, NA_KSPAN), lambda hf, b: (l, 0, hf, 0, 0)),
                  pl.BlockSpec((NA_HALF_Q, 256), lambda hf, b: (qrow(hf, b), 2)),
                  pl.BlockSpec(memory_space=pl.ANY)],
        out_specs=pl.BlockSpec((NA_HALF_Q, BR_W), lambda hf, b: (qrow(hf, b), 0)),
        out_shape=jax.ShapeDtypeStruct((N_TOK, BR_W), BF16),
        input_output_aliases={7: 0},
        compiler_params=_cparams(("arbitrary", "arbitrary")),
        name=f"lat_na_l{l}",
    )(proj, proj, proj, cache_k, cache_v, bias, proj, z_c)


def _lat_swa_kernel(dq_ref, dkv_ref, kc_ref, vc_ref, sink_ref, gd_ref, zin_ref, zd_ref):
    del zin_ref
    j = pl.program_id(1)
    h_scale = HD ** -0.5
    start = pl.multiple_of(jnp.clip(j * SWA_QB - WIN, 0, DEC_SEQ - SWA_SPAN), SWA_QB)
    dq = dq_ref[...]
    kvl = dkv_ref[pl.ds(start, SWA_SPAN), :]
    sink = sink_ref[...]
    rows = lax.broadcasted_iota(jnp.int32, (D_GROUP * SWA_QB, 1), 0)
    qi = j * SWA_QB + lax.broadcasted_iota(jnp.int32, (D_GROUP * SWA_QB, SWA_SPAN), 0) % SWA_QB
    ki = start + lax.broadcasted_iota(jnp.int32, (D_GROUP * SWA_QB, SWA_SPAN), 1)
    ok = jnp.abs(qi - ki) <= WIN
    outs = []
    for kv in range(D_KV):
        q2 = jnp.concatenate([dq[:, (kv * D_GROUP + g) * HD:(kv * D_GROUP + g + 1) * HD]
                              for g in range(D_GROUP)], axis=0)
        sk = jnp.where(rows < SWA_QB, sink[:, kv * D_GROUP:kv * D_GROUP + 1],
                       sink[:, kv * D_GROUP + 1:kv * D_GROUP + 2])
        s_ctx = _dot(q2, kc_ref[0, 0, kv].astype(BF16)) * h_scale
        s_loc = jnp.where(ok, _dot_nt(q2, kvl[:, kv * HD:(kv + 1) * HD]) * h_scale, NEG)
        (e_ctx, e_loc), r = _softmax_parts([s_ctx, s_loc], extra=sk)
        o = (_dot_nt(e_ctx.astype(BF16), vc_ref[0, 0, kv].astype(BF16))
             + _dot(e_loc.astype(BF16), kvl[:, D_KV * HD + kv * HD:D_KV * HD + (kv + 1) * HD])) * r
        outs += [o[g * SWA_QB:(g + 1) * SWA_QB] for g in range(D_GROUP)]
    zd_ref[...] = (jnp.concatenate(outs, axis=1) * gd_ref[...].astype(F32)).astype(BF16)


def _lat_swa(proj, cache_k, cache_v, sink_l, l, z_d):
    nq = DEC_SEQ // SWA_QB
    qrow = lambda b, j: N_CTX // SWA_QB + b * nq + j
    srow = lambda b, j: N_CTX // DEC_SEQ + b
    return pl.pallas_call(
        _lat_swa_kernel,
        grid=(DEC_BATCH, nq),
        in_specs=[pl.BlockSpec((SWA_QB, 256), lambda b, j: (qrow(b, j), COL_DQ // 256)),
                  pl.BlockSpec((DEC_SEQ, 256), lambda b, j: (srow(b, j), COL_DKV // 256)),
                  pl.BlockSpec((1, 1, D_KV, HD, PAST_LEN), lambda b, j: (b, l, 0, 0, 0)),
                  pl.BlockSpec((1, 1, D_KV, HD, PAST_LEN), lambda b, j: (b, l, 0, 0, 0)),
                  pl.BlockSpec((1, D_HEADS), lambda b, j: (0, 0)),
                  pl.BlockSpec((SWA_QB, 256), lambda b, j: (qrow(b, j), 3)),
                  pl.BlockSpec(memory_space=pl.ANY)],
        out_specs=pl.BlockSpec((SWA_QB, BR_W), lambda b, j: (qrow(b, j), 0)),
        out_shape=jax.ShapeDtypeStruct((N_TOK, BR_W), BF16),
        input_output_aliases={6: 0},
        compiler_params=_cparams(("arbitrary", "arbitrary")),
        name=f"lat_swa_l{l}",
    )(proj, proj, cache_k, cache_v, sink_l, proj, z_d)


def _lru_kernel(seq, has_state, *refs):
    if has_state:
        bx_ref, cw_ref, cb_ref, wg_ref, bg_ref, lam_ref, gr_ref, st_ref, zin_ref, zr_ref, a_s, u_s, y_s = refs
        del zin_ref
        fin_ref = None
    else:
        bx_ref, cw_ref, cb_ref, wg_ref, bg_ref, lam_ref, gr_ref = refs[:7]
        zr_ref, fin_ref, a_s, u_s, y_s = refs[-5:]
    x = bx_ref[...]
    t = lax.broadcasted_iota(jnp.int32, (seq, 1), 0)
    cw = cw_ref[...]
    xm1 = jnp.where(t >= 1, pltpu.roll(x, 1, 0), 0.0)
    xp1 = jnp.where(t < seq - 1, pltpu.roll(x, seq - 1, 0), 0.0)
    xp2 = jnp.where(t < seq - 2, pltpu.roll(x, seq - 2, 0), 0.0)
    xc = cb_ref[...] + xm1 * cw[0:1] + x * cw[1:2] + xp1 * cw[2:3] + xp2 * cw[3:4]

    gates = _dot(xc.astype(BF16), wg_ref[...]) + bg_ref[...]
    lam = lam_ref[...]
    nl = -lam
    softplus = jnp.maximum(nl, 0.0) + jnp.log1p(jnp.exp(-jnp.abs(nl)))
    tin = t % 8
    for d in range(2):
        r = jax.nn.sigmoid(gates[:, (2 * d) * BR_W:(2 * d + 1) * BR_W])
        ig = jax.nn.sigmoid(gates[:, (2 * d + 1) * BR_W:(2 * d + 2) * BR_W])
        log_a = -LRU_C * r * softplus[d:d + 1]
        a = jnp.exp(log_a)
        u = jnp.sqrt(-jnp.tanh(log_a) * (a * a + 1.0)) * (ig * xc)
        for s in (1, 2, 4):
            if d == 0:
                keep = tin >= s
                a_n = jnp.where(keep, pltpu.roll(a, s, 0), 1.0)
                u_n = jnp.where(keep, pltpu.roll(u, s, 0), 0.0)
            else:
                keep = tin < 8 - s
                a_n = jnp.where(keep, pltpu.roll(a, seq - s, 0), 1.0)
                u_n = jnp.where(keep, pltpu.roll(u, seq - s, 0), 0.0)
            u = u + a * u_n
            a = a * a_n
        a_s[d] = a
        u_s[d] = u

    if has_state:
        hf0 = st_ref[0, 0, 0:1, :]
        hb0 = st_ref[0, 0, 1:2, :]
    else:
        hf0 = jnp.zeros((1, BR_W), F32)
        hb0 = jnp.zeros((1, BR_W), F32)

    nt = seq // 8

    def body(k, carry):
        hf, hb = carry
        rf = pl.ds(pl.multiple_of(k * 8, 8), 8)
        rb = pl.ds(pl.multiple_of((nt - 1 - k) * 8, 8), 8)
        yf = u_s[0, rf, :] + a_s[0, rf, :] * hf
        yb = u_s[1, rb, :] + a_s[1, rb, :] * hb
        y_s[0, rf, :] = yf
        y_s[1, rb, :] = yb
        return yf[7:8, :], yb[0:1, :]

    hf, hb = lax.fori_loop(0, nt, body, (hf0, hb0))
    zr_ref[...] = ((y_s[0] + y_s[1]) * gr_ref[...].astype(F32)).astype(BF16)
    if fin_ref is not None:
        fin_ref[0, 0, 0:1, :] = hf
        fin_ref[0, 0, 1:2, :] = hb


def _lru(seq, nseq, row0, bx, proj, cw_l, cb_l, wg_l, bg_l, lam_l, state=None, l=0, z_r=None, states=None):
    has_state = state is not None
    rblk = lambda b: (row0 // seq + b, 0)
    in_specs = [pl.BlockSpec((seq, BR_W), rblk),
                pl.BlockSpec((CONV_W, BR_W), lambda b: (0, 0)),
                pl.BlockSpec((1, BR_W), lambda b: (0, 0)),
                pl.BlockSpec((BR_W, 4 * BR_W), lambda b: (0, 0)),
                pl.BlockSpec((1, 4 * BR_W), lambda b: (0, 0)),
                pl.BlockSpec((2, BR_W), lambda b: (0, 0)),
                pl.BlockSpec((seq, 256), lambda b: (row0 // seq + b, 1))]
    args = [bx, cw_l, cb_l, wg_l, bg_l, lam_l, proj]
    zspec = pl.BlockSpec((seq, BR_W), rblk)
    zshape = jax.ShapeDtypeStruct((N_TOK, BR_W), BF16)
    scratch = [pltpu.VMEM((2, seq, BR_W), F32)] * 3
    if has_state:
        in_specs += [pl.BlockSpec((1, 1, 2, BR_W), lambda b: (b, l, 0, 0)), pl.BlockSpec(memory_space=pl.ANY)]
        args += [state, z_r]
        return pl.pallas_call(
            functools.partial(_lru_kernel, seq, True),
            grid=(nseq,), in_specs=in_specs, out_specs=zspec, out_shape=zshape,
            scratch_shapes=scratch, input_output_aliases={8: 0},
            compiler_params=_cparams(("arbitrary",)), name=f"lru_lat_l{l}",
        )(*args)
    aliases = {}
    if states is not None:
        in_specs += [pl.BlockSpec(memory_space=pl.ANY)]
        args += [states]
        aliases = {7: 1}
    return pl.pallas_call(
        functools.partial(_lru_kernel, seq, False),
        grid=(nseq,), in_specs=in_specs,
        out_specs=[zspec, pl.BlockSpec((1, 1, 2, BR_W), lambda b: (b, l, 0, 0))],
        out_shape=[zshape, jax.ShapeDtypeStruct((nseq, DEPTH, 2, BR_W), F32)],
        scratch_shapes=scratch, input_output_aliases=aliases,
        compiler_params=_cparams(("arbitrary",)), name=f"lru_ctx_l{l}",
    )(*args)


def _merge_kernel(final, xa_ref, xb_ref, mod_ref, h_ref, za_ref, zr_ref, zc_ref, zd_ref,
                  wmg_ref, bmg_ref, wbo_ref, wo_ref, nf_ref, *refs):
    out_refs, (wmg_s, wbo_s, wo_s) = refs[:-3], refs[-3:]
    i = pl.program_id(0)

    @pl.when(i < W_STEPS)
    def _():
        r0 = pl.multiple_of(i * W_ROWS, W_ROWS)
        for n in range(N_BRANCH):
            wmg_s[pl.ds(r0, W_ROWS), n * D_MODEL:(n + 1) * D_MODEL] = wmg_ref[0, :, n, :].astype(BF16)
        _cast_rows(i, wbo_ref, wbo_s)
        _cast_rows(i, wo_ref, wo_s)

    @pl.when(i >= W_STEPS)
    def _():
        t = i - W_STEPS
        x = _load_x(t, xa_ref, xb_ref)
        gate = mod_ref[0][:, 2 * D_MODEL:]
        h = h_ref[...]
        zs = [za_ref[...], zr_ref[...], zc_ref[...], zd_ref[...]]
        bmg = bmg_ref[0]
        cols = []
        for c in range(0, D_MODEL, 512):
            acc = None
            for n in range(N_BRANCH):
                g = jax.nn.sigmoid(_dot(h, wmg_s[:, n * D_MODEL + c:n * D_MODEL + c + 512])
                                   + bmg[n:n + 1, c:c + 512])
                term = g * _dot(zs[n], wbo_s[n * BR_W:(n + 1) * BR_W, c:c + 512])
                acc = term if acc is None else acc + term
            cols.append(acc.astype(BF16))
        merged = jnp.concatenate(cols, axis=1)
        xn = x + gate * _dot(merged, wo_s[...])
        if not final:
            out_refs[0][...] = xn
        else:
            y = xn * lax.rsqrt(jnp.mean(xn * xn, axis=-1, keepdims=True) + EPS) * nf_ref[...]

            @pl.when(t < N_CTX_TILES)
            def _():
                out_refs[0][...] = y

            @pl.when(t >= N_CTX_TILES)
            def _():
                out_refs[1][...] = y


def _merge(final, xa, xb, xb_off, mod3, l, h, zs, w_mg, b_mg, w_bo, w_o, norm_f):
    row = lambda i: (_tile(i), 0)
    ctx_blk = lambda i: jnp.minimum(_tile(i), N_CTX_TILES - 1)
    lat_blk = lambda i: jnp.maximum(_tile(i) - N_CTX_TILES, 0)
    if final:
        out_specs = [pl.BlockSpec((TM, D_MODEL), lambda i: (ctx_blk(i), 0)),
                     pl.BlockSpec((TM, D_MODEL), lambda i: (lat_blk(i), 0))]
        out_shape = [jax.ShapeDtypeStruct((N_CTX, D_MODEL), F32), jax.ShapeDtypeStruct((N_LAT, D_MODEL), F32)]
    else:
        out_specs = pl.BlockSpec((TM, D_MODEL), row)
        out_shape = jax.ShapeDtypeStruct((N_TOK, D_MODEL), F32)
    wchunk = lambda width: pl.BlockSpec((1, W_ROWS, width), lambda i: (l, _wchunk(i), 0))
    return pl.pallas_call(
        functools.partial(_merge_kernel, final),
        grid=(W_STEPS + N_TILES,),
        in_specs=[
            pl.BlockSpec((TM, D_MODEL), lambda i: (ctx_blk(i), 0)),
            pl.BlockSpec((TM, D_MODEL), lambda i: (lat_blk(i) + xb_off, 0)),
            pl.BlockSpec((1, 1, 3 * D_MODEL), lambda i: (l * 16 + _mod_row(_tile(i)), 0, 0)),
            pl.BlockSpec((TM, D_MODEL), row),
            pl.BlockSpec((TM, BR_W), row), pl.BlockSpec((TM, BR_W), row),
            pl.BlockSpec((TM, BR_W), row), pl.BlockSpec((TM, BR_W), row),
            pl.BlockSpec((1, W_ROWS, N_BRANCH, D_MODEL), lambda i: (l, _wchunk(i), 0, 0)),
            pl.BlockSpec((1, N_BRANCH, D_MODEL), lambda i: (l, 0, 0)),
            wchunk(D_MODEL),
            wchunk(D_MODEL),
            pl.BlockSpec((1, D_MODEL), lambda i: (0, 0)),
        ],
        out_specs=out_specs,
        out_shape=out_shape,
        scratch_shapes=[pltpu.VMEM((D_MODEL, N_BRANCH * D_MODEL), BF16),
                        pltpu.VMEM((N_BRANCH * BR_W, D_MODEL), BF16),
                        pltpu.VMEM((D_MODEL, D_MODEL), BF16)],
        compiler_params=_cparams(("arbitrary",)),
        name=f"merge_l{l}",
    )(xa, xb, mod3, h, *zs,
      w_mg, b_mg,
      w_bo.reshape(DEPTH, N_BRANCH * BR_W, D_MODEL), w_o, norm_f.reshape(1, D_MODEL))


def _lru_gate_weights(lru_wa, lru_ba, lru_wx, lru_bx):
    w = jnp.stack([lru_wa[:, 0], lru_wx[:, 0], lru_wa[:, 1], lru_wx[:, 1]], axis=1)
    eye = jnp.eye(B_BLOCKS, dtype=w.dtype)
    dense = w[:, :, :, :, None, :] * eye[None, None, :, None, :, None]
    wg = dense.transpose(0, 2, 3, 1, 4, 5).reshape(DEPTH, BR_W, 4 * BR_W).astype(BF16)
    bg = jnp.stack([lru_ba[:, 0], lru_bx[:, 0], lru_ba[:, 1], lru_bx[:, 1]], axis=1).reshape(DEPTH, 1, 4 * BR_W)
    return wg, bg


def kernel(x_prompt, x_sample, cache_diff_k, cache_diff_v, cache_na_k, cache_na_v, cache_swa_k, cache_swa_v,
           state_lru, c, c_ctx, norm_g, w_ada, b_ada, w_in, diff_lambda, diff_norm_g, conv_w, conv_b,
           lru_wa, lru_ba, lru_wx, lru_bx, lru_lam, na_rpb, swa_sink, w_mg, b_mg, w_bo, w_o, norm_f):
    tab_a_np, _ = _rope_tables(A_QK)
    tab_d_np, _ = _rope_tables(HD)
    tab_a, tab_d = jnp.asarray(tab_a_np), jnp.asarray(tab_d_np)

    cvecs = jnp.concatenate([c_ctx[None, :], c, jnp.zeros((16 - 1 - DEC_BATCH, D_MODEL), F32)], axis=0)
    mod3 = _modulation(cvecs, w_ada, b_ada).reshape(DEPTH * 16, 1, 3 * D_MODEL)
    bias = _na_bias(na_rpb)
    wg, bg = _lru_gate_weights(lru_wa, lru_ba, lru_wx, lru_bx)
    norm_g3 = norm_g.reshape(DEPTH, 1, D_MODEL)
    past = [jnp.swapaxes(t, -1, -2) for t in
            (cache_diff_k, cache_diff_v, cache_na_k, cache_na_v, cache_swa_k, cache_swa_v)]

    xa = x_prompt.reshape(N_CTX, D_MODEL)
    xb = x_sample.reshape(N_LAT, D_MODEL)
    xb_off = 0
    caches = states = None
    y_p = y_s = None
    for l in range(DEPTH):
        h, proj, bx, caches = _inproj(xa, xb, xb_off, mod3, l, norm_g3, w_in, tab_a, tab_d, caches)

        lam_l, dg_l, sink_l = diff_lambda[l], diff_norm_g[l][None, :], swa_sink[l][None, :]
        z_a, z_c, z_d = _ctx_attn(proj, l, lam_l, dg_l, sink_l)
        z_a = _lat_diff(proj, past[0], past[1], l, lam_l, dg_l, z_a)
        z_c = _lat_na(proj, past[2], past[3], bias, l, z_c)
        z_d = _lat_swa(proj, past[4], past[5], sink_l, l, z_d)

        lru_args = (bx, proj, conv_w[l], conv_b[l][None, :], wg[l], bg[l], lru_lam[l])
        z_r, states = _lru(SEQ, BATCH, 0, *lru_args, l=l, states=states)
        z_r = _lru(DEC_SEQ, DEC_BATCH, N_CTX, *lru_args, state=state_lru, l=l, z_r=z_r)

        final = l == DEPTH - 1
        out = _merge(final, xa, xb, xb_off, mod3, l, h, (z_a, z_r, z_c, z_d), w_mg, b_mg, w_bo, w_o, norm_f)
        if final:
            y_p, y_s = out
        else:
            xa = xb = out
            xb_off = N_CTX_TILES

    new_caches = [jnp.swapaxes(t, -1, -2) for t in caches]
    return (y_p.reshape(BATCH, SEQ, D_MODEL), y_s.reshape(DEC_BATCH, DEC_SEQ, D_MODEL), *new_caches, states)
```

```python
import functools
import math

import numpy as np
import jax
import jax.numpy as jnp
from jax import lax
from jax.experimental import pallas as pl
from jax.experimental.pallas import tpu as pltpu

F32 = jnp.float32
BF16 = jnp.bfloat16

D_MODEL = 1024
BATCH = 16
SEQ = 256
DEPTH = 2
DEC_BATCH = 8
DEC_SEQ = 1024
PAST_LEN = 512
GRID_W = 64
N_BRANCH = 4
BR_W = D_MODEL // 4
HD = 64
A_HEADS = BR_W // HD
A_QK = HD // 2
B_BLOCKS = 4
B_BLK = BR_W // B_BLOCKS
CONV_W = 4
LRU_C = 8.0
C_HEADS = BR_W // HD
NA_ROWS = 8
NA_COLS = 16
D_HEADS = BR_W // HD
D_KV = 2
D_GROUP = D_HEADS // D_KV
WIN = 128
ROPE_BASE = 10000.0
EPS = 1e-6
NEG = -1e30

N_CTX = BATCH * SEQ
N_LAT = DEC_BATCH * DEC_SEQ
N_TOK = N_CTX + N_LAT
GRID_ROWS = DEC_SEQ // GRID_W

TM = 512
N_CTX_TILES = N_CTX // TM
N_TILES = N_TOK // TM
LAT_TILES_PER_SEQ = DEC_SEQ // TM

PROJ_W = 3072
COL_G = 0
COL_AQ = 1024
COL_AK = 1280
COL_AV = 1536
COL_CQ = 1792
COL_CK = 2048
COL_CV = 2304
COL_DQ = 2560
COL_DKV = 2816
COL_BX = 3072
IN_W = 3328
CACHE_W = 1280

NA_HALF_Q = DEC_SEQ // 2
NA_KROWS = 12
NA_KSPAN = NA_KROWS * GRID_W
NA_HALF_KSTART = 4 * GRID_W

SWA_QB = 128
SWA_SPAN = SWA_QB + 2 * WIN

LOG2E = math.log2(math.e)
QA_SCALE = A_QK ** -0.5 * LOG2E
QH_SCALE = HD ** -0.5 * LOG2E

VMEM_LIMIT = 56 * 1024 * 1024


def _cparams(sem):
    return pltpu.CompilerParams(dimension_semantics=sem, vmem_limit_bytes=VMEM_LIMIT)


def _resident(shape, index_map):
    return pl.BlockSpec(shape, index_map, pipeline_mode=pl.Buffered(1))


def _dot(a, b):
    return jnp.dot(a, b, preferred_element_type=F32)


def _dot_nt(a, b):
    return lax.dot_general(a, b, (((1,), (1,)), ((), ())), preferred_element_type=F32)


def _silu(x):
    return x * jax.nn.sigmoid(x)


def _rope_tables(d):
    half = d // 2
    quarter = half // 2
    lane = np.arange(128)
    q = lane % d
    use_col = (q >= half)
    i = (q % half) % quarter
    first = (q % half) < quarter
    inv = (ROPE_BASE ** (-np.arange(quarter, dtype=np.float32) / np.float32(quarter))).astype(np.float32)
    t = np.arange(DEC_SEQ)
    pos = np.where(use_col[None, :], (t % GRID_W)[:, None], (t // GRID_W)[:, None]).astype(np.float32)
    ang = (pos * inv[i][None, :]).astype(np.float32)
    cos, sin = np.cos(ang), np.sin(ang)
    c = np.concatenate([np.ones((TM, 128), np.float32), cos], axis=0)
    s1 = np.concatenate([np.zeros((TM, 128), np.float32), np.where(first[None, :], -sin, 0.0)], axis=0)
    s2 = np.concatenate([np.zeros((TM, 128), np.float32), np.where(first[None, :], 0.0, sin)], axis=0)
    return np.stack([c, s1, s2]).astype(np.float32), quarter


def _rope128(x, tab_ref, shift):
    return (x * tab_ref[0] + pltpu.roll(x, 128 - shift, 1) * tab_ref[1]
            + pltpu.roll(x, shift, 1) * tab_ref[2])


def _rope(x, tab_ref, shift):
    w = x.shape[1]
    return jnp.concatenate([_rope128(x[:, c:c + 128], tab_ref, shift) for c in range(0, w, 128)], axis=1)


def _mod_kernel(c_ref, w_ref, b_ref, o_ref):
    c = c_ref[...]
    o_ref[0] = _dot(_silu(c).astype(BF16), w_ref[0].astype(BF16)) + b_ref[0]


def _modulation(cvecs, w_ada, b_ada):
    nb = 3 * D_MODEL // 768
    return pl.pallas_call(
        _mod_kernel,
        grid=(DEPTH, nb),
        in_specs=[pl.BlockSpec((16, D_MODEL), lambda l, j: (0, 0)),
                  pl.BlockSpec((1, D_MODEL, 768), lambda l, j: (l, 0, j)),
                  pl.BlockSpec((1, 1, 768), lambda l, j: (l, 0, j))],
        out_specs=pl.BlockSpec((1, 16, 768), lambda l, j: (l, 0, j)),
        out_shape=jax.ShapeDtypeStruct((DEPTH, 16, 3 * D_MODEL), F32),
        compiler_params=_cparams(("arbitrary", "arbitrary")),
        name="modulation",
    )(cvecs, w_ada, b_ada.reshape(DEPTH, 1, 3 * D_MODEL))


W_STEPS = 8
W_ROWS = D_MODEL // W_STEPS


def _tile(i):
    return jnp.maximum(i - W_STEPS, 0)


def _wchunk(i):
    return jnp.minimum(i, W_STEPS - 1)


def _mod_row(t):
    return jnp.where(t < N_CTX_TILES, 0, 1 + (t - N_CTX_TILES) // LAT_TILES_PER_SEQ)


def _rope_blk(t):
    return jnp.where(t < N_CTX_TILES, 0, 1 + (t - N_CTX_TILES) % LAT_TILES_PER_SEQ)


def _load_x(t, xa_ref, xb_ref):
    return jnp.where(t < N_CTX_TILES, xa_ref[...], xb_ref[...])


def _cast_rows(i, src_ref, dst_ref):
    r0 = pl.multiple_of(i * W_ROWS, W_ROWS)
    dst_ref[pl.ds(r0, W_ROWS), :] = src_ref[0].astype(BF16)


WI_G, WI_AQ, WI_AK, WI_AV, WI_BX, WI_CQ, WI_CKV, WI_DQ, WI_DKV = 0, 1024, 1280, 1536, 1792, 2048, 2304, 2816, 3072
SEQ_PER_TILE = TM // SEQ


def _store_heads(ref, val, n_heads, width):
    vt = val.T
    for s in range(SEQ_PER_TILE):
        for h in range(n_heads):
            ref[s, 0, h] = vt[h * width:(h + 1) * width, s * SEQ:(s + 1) * SEQ]


def _inproj_kernel(n_alias, xa_ref, xb_ref, mod_ref, g_ref, w_ref, ta_ref, td_ref, *refs):
    h_ref, proj_ref, bx_ref, dk_ref, dv_ref, nk_ref, nv_ref, sk_ref, sv_ref, w_s = refs[n_alias:]
    i = pl.program_id(0)

    @pl.when(i < W_STEPS)
    def _():
        _cast_rows(i, w_ref, w_s)

    @pl.when(i >= W_STEPS)
    def _():
        t = i - W_STEPS
        is_ctx = t < N_CTX_TILES
        x = _load_x(t, xa_ref, xb_ref)
        mod = mod_ref[0]
        shift, scale = mod[:, :D_MODEL], mod[:, D_MODEL:2 * D_MODEL]
        xn = x * lax.rsqrt(jnp.mean(x * x, axis=-1, keepdims=True) + EPS) * g_ref[0]
        hb =(xn * (1.0 + scale) + shift).astype(BF16)
        h_ref[...] = hb

        def mm(c0, width):
            return _dot(hb, w_s[:, c0:c0 + width])

        for c in range(0, 1024, 512):
            proj_ref[:, COL_G + c:COL_G + c + 512] = _silu(mm(WI_G + c, 512)).astype(BF16)

        aq = mm(WI_AQ, 256) * QA_SCALE
        proj_ref[:, COL_AQ:COL_AQ + 256] = _rope(aq, ta_ref, A_QK // 4).astype(BF16)
        ak = mm(WI_AK, 256)
        proj_ref[:, COL_AK:COL_AK + 256] = _rope(ak, ta_ref, A_QK // 4).astype(BF16)
        av = mm(WI_AV, 256)
        proj_ref[:, COL_AV:COL_AV + 256] = av.astype(BF16)

        cq = mm(WI_CQ, 256) * QH_SCALE
        proj_ref[:, COL_CQ:COL_CQ + 256] = cq.astype(BF16)
        ckv = mm(WI_CKV, 512)
        proj_ref[:, COL_CK:COL_CK + 512] = ckv.astype(BF16)

        dq = mm(WI_DQ, 256) * QH_SCALE
        proj_ref[:, COL_DQ:COL_DQ + 256] = _rope(dq, td_ref, HD // 4).astype(BF16)
        dkv = mm(WI_DKV, 256)
        proj_ref[:, COL_DKV:COL_DKV + 128] = _rope(dkv[:, :128], td_ref, HD // 4).astype(BF16)
        proj_ref[:, COL_DKV + 128:COL_DKV + 256] = dkv[:, 128:].astype(BF16)

        bx_ref[...] = mm(WI_BX, 256)

        @pl.when(is_ctx)
        def _():
            akt = ak.T
            for s in range(SEQ_PER_TILE):
                for h in range(A_HEADS):
                    for m in range(2):
                        c0 = h * HD + m * A_QK
                        dk_ref[s, 0, h, m] = akt[c0:c0 + A_QK, s * SEQ:(s + 1) * SEQ]
            _store_heads(dv_ref, av, A_HEADS, HD)
            _store_heads(nk_ref, ckv[:, :256], C_HEADS, HD)
            _store_heads(nv_ref, ckv[:, 256:], C_HEADS, HD)
            _store_heads(sk_ref, dkv[:, :128], D_KV, HD)
            _store_heads(sv_ref, dkv[:, 128:], D_KV, HD)


def _cache_shapes():
    hs = lambda n: (BATCH, DEPTH, n, HD, SEQ)
    return [(BATCH, DEPTH, A_HEADS, 2, A_QK, SEQ), hs(A_HEADS), hs(C_HEADS), hs(C_HEADS), hs(D_KV), hs(D_KV)]


def _inproj(xa, xb, xb_off, mod3, l, norm_g, w_in, tab_a, tab_d, caches):
    row = lambda i: (_tile(i), 0)
    ctx_blk = lambda i: jnp.minimum(_tile(i), N_CTX_TILES - 1)

    def cache_spec(shape):
        blk = (SEQ_PER_TILE, 1) + shape[2:]
        nz = len(shape) - 2
        return pl.BlockSpec(blk, lambda i: (ctx_blk(i), l) + (0,) * nz)

    cshapes = _cache_shapes()
    aliases = {} if caches is None else {7 + k: 3 + k for k in range(6)}
    extra_specs = [] if caches is None else [pl.BlockSpec(memory_space=pl.ANY)] * 6
    extra_args = [] if caches is None else list(caches)
    outs = pl.pallas_call(
        functools.partial(_inproj_kernel, len(extra_args)),
        grid=(W_STEPS + N_TILES,),
        in_specs=[
            pl.BlockSpec((TM, D_MODEL), lambda i: (ctx_blk(i), 0)),
            pl.BlockSpec((TM, D_MODEL), lambda i: (jnp.maximum(_tile(i) - N_CTX_TILES, 0) + xb_off, 0)),
            pl.BlockSpec((1, 1, 3 * D_MODEL), lambda i: (l * 16 + _mod_row(_tile(i)), 0, 0)),
            pl.BlockSpec((1, 1, D_MODEL), lambda i: (l, 0, 0)),
            pl.BlockSpec((1, W_ROWS, IN_W), lambda i: (l, _wchunk(i), 0)),
            pl.BlockSpec((3, TM, 128), lambda i: (0, _rope_blk(_tile(i)), 0)),
            pl.BlockSpec((3, TM, 128), lambda i: (0, _rope_blk(_tile(i)), 0)),
        ] + extra_specs,
        out_specs=[
            pl.BlockSpec((TM, D_MODEL), row),
            pl.BlockSpec((TM, PROJ_W), row),
            pl.BlockSpec((TM, BR_W), row),
        ] + [cache_spec(s) for s in cshapes],
        out_shape=[
            jax.ShapeDtypeStruct((N_TOK, D_MODEL), BF16),
            jax.ShapeDtypeStruct((N_TOK, PROJ_W), BF16),
            jax.ShapeDtypeStruct((N_TOK, BR_W), F32),
        ] + [jax.ShapeDtypeStruct(s, F32) for s in cshapes],
        scratch_shapes=[pltpu.VMEM((D_MODEL, IN_W), BF16)],
        input_output_aliases=aliases,
        compiler_params=_cparams(("arbitrary",)),
        name=f"inproj_l{l}",
    )(xa, xb, mod3, norm_g, w_in, tab_a, tab_d, *extra_args)
    return outs[0], outs[1], outs[2], tuple(outs[3:])


def _diff_lambda(lam_ref, lam_init):
    lv = lam_ref[...]
    s1 = jnp.sum(lv[0:1] * lv[1:2], axis=-1, keepdims=True)
    s2 = jnp.sum(lv[2:3] * lv[3:4], axis=-1, keepdims=True)
    return jnp.exp(s1) - jnp.exp(s2) + lam_init


def _diff_norm(o, dg_ref, lam_init):
    y = o * lax.rsqrt(jnp.mean(o * o, axis=-1, keepdims=True) + EPS) * dg_ref[...]
    return y * (1.0 - lam_init)


def _exp2_parts(parts, extra=None):
    m = functools.reduce(jnp.maximum, [jnp.max(s, axis=-1, keepdims=True) for s in parts])
    if extra is not None:
        m = jnp.maximum(m, extra)
    return [jnp.exp2(s - m).astype(BF16) for s in parts], m


def _with_ones(v):
    return jnp.concatenate([v, jnp.ones_like(v)], axis=1)


def _with_ones_t(vt):
    return jnp.concatenate([vt, jnp.ones((16, vt.shape[1]), vt.dtype)], axis=0)


def _stack_group(dq, kv, rows):
    return jnp.concatenate([dq[:, (kv * D_GROUP + g) * HD:(kv * D_GROUP + g + 1) * HD]
                            for g in range(D_GROUP)], axis=0)


def _sink_col(sink2, kv, rows):
    r = lax.broadcasted_iota(jnp.int32, (D_GROUP * rows, 1), 0)
    return jnp.where(r < rows, sink2[:, kv * D_GROUP:kv * D_GROUP + 1], sink2[:, kv * D_GROUP + 1:kv * D_GROUP + 2])


def _ctx_attn_kernel(lam_init, aq_ref, ak_ref, av_ref, cq_ref, ck_ref, cv_ref, dq_ref, dkv_ref,
                     ga_ref, gc_ref, gd_ref, lam_ref, dg_ref, sink_ref, za_ref, zc_ref, zd_ref):
    lam = _diff_lambda(lam_ref, lam_init)
    aq, ak, av = aq_ref[...], ak_ref[...], av_ref[...]
    cq, ck, cv = cq_ref[...], ck_ref[...], cv_ref[...]
    dq, dkv = dq_ref[...], dkv_ref[...]
    sink2 = sink_ref[...] * LOG2E

    sa = [_dot_nt(aq[:, c:c + A_QK], ak[:, c:c + A_QK]) for c in range(0, BR_W, A_QK)]
    sc = [_dot_nt(cq[:, c:c + HD], ck[:, c:c + HD]) for c in range(0, BR_W, HD)]
    sd = [_dot_nt(_stack_group(dq, kv, SEQ), dkv[:, kv * HD:(kv + 1) * HD]) for kv in range(D_KV)]
    sks = [_sink_col(sink2, kv, SEQ) for kv in range(D_KV)]

    ea = [_exp2_parts([s])[0][0] for s in sa]
    ec = [_exp2_parts([s])[0][0] for s in sc]
    ed = [_exp2_parts([s], extra=sk) for s, sk in zip(sd, sks)]

    outs = []
    for h in range(A_HEADS):
        va = _with_ones(av[:, h * HD:(h + 1) * HD])
        p0, p1 = _dot(ea[2 * h], va), _dot(ea[2 * h + 1], va)
        o = p0[:, :HD] * (1.0 / p0[:, HD:HD + 1]) - p1[:, :HD] * (lam / p1[:, HD:HD + 1])
        outs.append(_diff_norm(o, dg_ref, lam_init))
    za_ref[...] = (jnp.concatenate(outs, axis=1) * ga_ref[...].astype(F32)).astype(BF16)

    outs = []
    for h in range(C_HEADS):
        p = _dot(ec[h], _with_ones(cv[:, h * HD:(h + 1) * HD]))
        outs.append(p[:, :HD] * (1.0 / p[:, HD:HD + 1]))
    zc_ref[...] = (jnp.concatenate(outs, axis=1) * gc_ref[...].astype(F32)).astype(BF16)

    outs = []
    for kv in range(D_KV):
        (e,), m = ed[kv]
        p = _dot(e, _with_ones(dkv[:, D_KV * HD + kv * HD:D_KV * HD + (kv + 1) * HD]))
        o = p[:, :HD] * (1.0 / (p[:, HD:HD + 1] + jnp.exp2(sks[kv] - m)))
        outs += [o[g * SEQ:(g + 1) * SEQ] for g in range(D_GROUP)]
    zd_ref[...] = (jnp.concatenate(outs, axis=1) * gd_ref[...].astype(F32)).astype(BF16)


def _ctx_attn(proj, l, lam_l, dg_l, sink_l):
    lam_init = 0.8 - 0.6 * math.exp(-0.3 * l)
    colblk = lambda cb: pl.BlockSpec((SEQ, 256), lambda b: (b, cb))
    zspec = pl.BlockSpec((SEQ, BR_W), lambda b: (b, 0))
    zshape = jax.ShapeDtypeStruct((N_TOK, BR_W), BF16)
    return pl.pallas_call(
        functools.partial(_ctx_attn_kernel, lam_init),
        grid=(BATCH,),
        in_specs=[colblk(COL_AQ // 256), colblk(COL_AK // 256), colblk(COL_AV // 256),
                  colblk(COL_CQ // 256), colblk(COL_CK // 256), colblk(COL_CV // 256),
                  colblk(COL_DQ // 256), colblk(COL_DKV // 256),
                  colblk(0), colblk(2), colblk(3),
                  pl.BlockSpec((4, A_QK), lambda b: (0, 0)),
                  pl.BlockSpec((1, HD), lambda b: (0, 0)),
                  pl.BlockSpec((1, D_HEADS), lambda b: (0, 0))],
        out_specs=[zspec, zspec, zspec],
        out_shape=[zshape, zshape, zshape],
        compiler_params=_cparams(("arbitrary",)),
        name=f"ctx_attn_l{l}",
    )(*([proj] * 11), lam_l, dg_l, sink_l)


LA_QB = 256


def _lat_diff_kernel(lam_init, aq_ref, ak_ref, av_ref, ck_ref, cv_ref, ga_ref, lam_ref, dg_ref, zin_ref, za_ref):
    del zin_ref
    lam = _diff_lambda(lam_ref, lam_init)
    aq = aq_ref[...]
    scores = []
    for h in range(A_HEADS):
        for m in range(2):
            c = h * HD + m * A_QK
            q = aq[:, c:c + A_QK]
            scores.append([_dot(q, ck_ref[0, 0, h, m].astype(BF16)),
                           _dot_nt(q, ak_ref[:, c:c + A_QK])])
    exps = [_exp2_parts(p)[0] for p in scores]
    outs = []
    for h in range(A_HEADS):
        va_ctx = _with_ones_t(cv_ref[0, 0, h].astype(BF16))
        va_loc = _with_ones(av_ref[:, h * HD:(h + 1) * HD])
        num, den = [], []
        for m in range(2):
            e_ctx, e_loc = exps[2 * h + m]
            p_ctx, p_loc = _dot_nt(e_ctx, va_ctx), _dot(e_loc, va_loc)
            num.append(p_ctx[:, :HD] + p_loc[:, :HD])
            den.append(p_ctx[:, HD:HD + 1] + p_loc[:, HD:HD + 1])
        o = num[0] * (1.0 / den[0]) - num[1] * (lam / den[1])
        outs.append(_diff_norm(o, dg_ref, lam_init))
    za_ref[...] = (jnp.concatenate(outs, axis=1) * ga_ref[...].astype(F32)).astype(BF16)


def _lat_diff(proj, cache_k, cache_v, l, lam_l, dg_l, z_a):
    lam_init = 0.8 - 0.6 * math.exp(-0.3 * l)
    nq = DEC_SEQ // LA_QB
    qrow = lambda b, j: N_CTX // LA_QB + b * nq + j
    srow = lambda b, j: N_CTX // DEC_SEQ + b
    return pl.pallas_call(
        functools.partial(_lat_diff_kernel, lam_init),
        grid=(DEC_BATCH, nq),
        in_specs=[pl.BlockSpec((LA_QB, 256), lambda b, j: (qrow(b, j), COL_AQ // 256)),
                  pl.BlockSpec((DEC_SEQ, 256), lambda b, j: (srow(b, j), COL_AK // 256)),
                  pl.BlockSpec((DEC_SEQ, 256), lambda b, j: (srow(b, j), COL_AV // 256)),
                  pl.BlockSpec((1, 1, A_HEADS, 2, A_QK, PAST_LEN), lambda b, j: (b, l, 0, 0, 0, 0)),
                  pl.BlockSpec((1, 1, A_HEADS, HD, PAST_LEN), lambda b, j: (b, l, 0, 0, 0)),
                  pl.BlockSpec((LA_QB, 256), lambda b, j: (qrow(b, j), 0)),
                  pl.BlockSpec((4, A_QK), lambda b, j: (0, 0)),
                  pl.BlockSpec((1, HD), lambda b, j: (0, 0)),
                  pl.BlockSpec(memory_space=pl.ANY)],
        out_specs=pl.BlockSpec((LA_QB, BR_W), lambda b, j: (qrow(b, j), 0)),
        out_shape=jax.ShapeDtypeStruct((N_TOK, BR_W), BF16),
        input_output_aliases={8: 0},
        compiler_params=_cparams(("arbitrary", "arbitrary")),
        name=f"lat_diff_l{l}",
    )(proj, proj, proj, cache_k, cache_v, proj, lam_l, dg_l, z_a)


def _na_bias_kernel(rpb_ref, o_ref):
    l, h = pl.program_id(0), pl.program_id(1)
    base = (l * C_HEADS + h) * ((2 * NA_ROWS - 1) * (2 * NA_COLS - 1))
    qc = lax.broadcasted_iota(jnp.int32, (GRID_W, GRID_W), 0)
    kc = lax.broadcasted_iota(jnp.int32, (GRID_W, GRID_W), 1)
    cs = jnp.clip(qc - NA_COLS // 2, 0, GRID_W - NA_COLS)
    ok = (kc >= cs) & (kc < cs + NA_COLS)
    dcol = kc - qc + (NA_COLS - 1)
    neg = jnp.full((GRID_W, GRID_W), NEG, F32)
    tabs = []
    for d in range(2 * NA_ROWS - 1):
        t = neg
        for dc in range(2 * NA_COLS - 1):
            t = jnp.where(ok & (dcol == dc), rpb_ref[base + d * (2 * NA_COLS - 1) + dc] * LOG2E, t)
        tabs.append(t)
    for half in range(2):
        for rq in range(NA_ROWS):
            r = half * NA_ROWS + rq
            rs = min(max(r - NA_ROWS // 2, 0), GRID_ROWS - NA_ROWS)
            blks = []
            for kr in range(NA_KROWS):
                kabs = half * 4 + kr
                blks.append(tabs[kabs - r + NA_ROWS - 1] if rs <= kabs < rs + NA_ROWS else neg)
            o_ref[0, 0, half, rq * GRID_W:(rq + 1) * GRID_W, :] = jnp.concatenate(blks, axis=1)


def _na_bias(na_rpb):
    return pl.pallas_call(
        _na_bias_kernel,
        grid=(DEPTH, C_HEADS),
        in_specs=[pl.BlockSpec(memory_space=pltpu.SMEM)],
        out_specs=pl.BlockSpec((1, 1, 2, NA_HALF_Q, NA_KSPAN), lambda l, h: (l, h, 0, 0, 0)),
        out_shape=jax.ShapeDtypeStruct((DEPTH, C_HEADS, 2, NA_HALF_Q, NA_KSPAN), F32),
        compiler_params=_cparams(("arbitrary", "arbitrary")),
        name="na_bias",
    )(na_rpb.reshape(-1))


def _lat_na_kernel(cq_ref, ck_ref, cv_ref, kc_ref, vc_ref, bias_ref, gc_ref, zin_ref, zc_ref):
    del zin_ref
    half = pl.program_id(0)
    k0 = pl.multiple_of(half * NA_HALF_KSTART, NA_HALF_KSTART)
    cq = cq_ref[...]
    kl_all = ck_ref[pl.ds(k0, NA_KSPAN), :]
    vl_all = cv_ref[pl.ds(k0, NA_KSPAN), :]
    scores = []
    for h in range(C_HEADS):
        q = cq[:, h * HD:(h + 1) * HD]
        scores.append([_dot(q, kc_ref[0, 0, h].astype(BF16)),
                       _dot_nt(q, kl_all[:, h * HD:(h + 1) * HD]) + bias_ref[0, h, 0]])
    exps = [_exp2_parts(p)[0] for p in scores]
    outs = []
    for h in range(C_HEADS):
        e_ctx, e_loc = exps[h]
        p_ctx = _dot_nt(e_ctx, _with_ones_t(vc_ref[0, 0, h].astype(BF16)))
        p_loc = _dot(e_loc, _with_ones(vl_all[:, h * HD:(h + 1) * HD]))
        outs.append((p_ctx[:, :HD] + p_loc[:, :HD]) * (1.0 / (p_ctx[:, HD:HD + 1] + p_loc[:, HD:HD + 1])))
    zc_ref[...] = (jnp.concatenate(outs, axis=1) * gc_ref[...].astype(F32)).astype(BF16)


def _lat_na(proj, cache_k, cache_v, bias, l, z_c):
    qrow = lambda hf, b: N_CTX // NA_HALF_Q + b * 2 + hf
    srow = lambda hf, b: N_CTX // DEC_SEQ + b
    return pl.pallas_call(
        _lat_na_kernel,
        grid=(2, DEC_BATCH),
        in_specs=[pl.BlockSpec((NA_HALF_Q, 256), lambda hf, b: (qrow(hf, b), COL_CQ // 256)),
                  pl.BlockSpec((DEC_SEQ, 256), lambda hf, b: (srow(hf, b), COL_CK // 256)),
                  pl.BlockSpec((DEC_SEQ, 256), lambda hf, b: (srow(hf, b), COL_CV // 256)),
                  pl.BlockSpec((1, 1, C_HEADS, HD, PAST_LEN), lambda hf, b: (b, l, 0, 0, 0)),
                  pl.BlockSpec((1, 1, C_HEADS, HD, PAST_LEN), lambda hf, b: (b, l, 0, 0, 0)),
                  pl.BlockSpec((1, C_HEADS, 1, NA_HALF_Q, NA_KSPAN), lambda hf, b: (l, 0, hf, 0, 0)),
                  pl.BlockSpec((NA_HALF_Q, 256), lambda hf, b: (qrow(hf, b), 2)),
                  pl.BlockSpec(memory_space=pl.ANY)],
        out_specs=pl.BlockSpec((NA_HALF_Q, BR_W), lambda hf, b: (qrow(hf, b), 0)),
        out_shape=jax.ShapeDtypeStruct((N_TOK, BR_W), BF16),
        input_output_aliases={7: 0},
        compiler_params=_cparams(("arbitrary", "arbitrary")),
        name=f"lat_na_l{l}",
    )(proj, proj, proj, cache_k, cache_v, bias, proj, z_c)


def _lat_swa_kernel(dq_ref, dkv_ref, kc_ref, vc_ref, sink_ref, gd_ref, zin_ref, zd_ref):
    del zin_ref
    j = pl.program_id(1)
    start = pl.multiple_of(jnp.clip(j * SWA_QB - WIN, 0, DEC_SEQ - SWA_SPAN), SWA_QB)
    dq = dq_ref[...]
    kvl = dkv_ref[pl.ds(start, SWA_SPAN), :]
    sink2 = sink_ref[...] * LOG2E
    qi = j * SWA_QB + lax.broadcasted_iota(jnp.int32, (D_GROUP * SWA_QB, SWA_SPAN), 0) % SWA_QB
    ki = start + lax.broadcasted_iota(jnp.int32, (D_GROUP * SWA_QB, SWA_SPAN), 1)
    ok = jnp.abs(qi - ki) <= WIN
    scores, sks = [], []
    for kv in range(D_KV):
        q2 = _stack_group(dq, kv, SWA_QB)
        scores.append([_dot(q2, kc_ref[0, 0, kv].astype(BF16)),
                       jnp.where(ok, _dot_nt(q2, kvl[:, kv * HD:(kv + 1) * HD]), NEG)])
        sks.append(_sink_col(sink2, kv, SWA_QB))
    exps = [_exp2_parts(p, extra=sk) for p, sk in zip(scores, sks)]
    outs = []
    for kv in range(D_KV):
        (e_ctx, e_loc), m = exps[kv]
        p_ctx = _dot_nt(e_ctx, _with_ones_t(vc_ref[0, 0, kv].astype(BF16)))
        p_loc = _dot(e_loc, _with_ones(kvl[:, D_KV * HD + kv * HD:D_KV * HD + (kv + 1) * HD]))
        den = p_ctx[:, HD:HD + 1] + p_loc[:, HD:HD + 1] + jnp.exp2(sks[kv] - m)
        o = (p_ctx[:, :HD] + p_loc[:, :HD]) * (1.0 / den)
        outs += [o[g * SWA_QB:(g + 1) * SWA_QB] for g in range(D_GROUP)]
    zd_ref[...] = (jnp.concatenate(outs, axis=1) * gd_ref[...].astype(F32)).astype(BF16)


def _lat_swa(proj, cache_k, cache_v, sink_l, l, z_d):
    nq = DEC_SEQ // SWA_QB
    qrow = lambda b, j: N_CTX // SWA_QB + b * nq + j
    srow = lambda b, j: N_CTX // DEC_SEQ + b
    return pl.pallas_call(
        _lat_swa_kernel,
        grid=(DEC_BATCH, nq),
        in_specs=[pl.BlockSpec((SWA_QB, 256), lambda b, j: (qrow(b, j), COL_DQ // 256)),
                  pl.BlockSpec((DEC_SEQ, 256), lambda b, j: (srow(b, j), COL_DKV // 256)),
                  pl.BlockSpec((1, 1, D_KV, HD, PAST_LEN), lambda b, j: (b, l, 0, 0, 0)),
                  pl.BlockSpec((1, 1, D_KV, HD, PAST_LEN), lambda b, j: (b, l, 0, 0, 0)),
                  pl.BlockSpec((1, D_HEADS), lambda b, j: (0, 0)),
                  pl.BlockSpec((SWA_QB, 256), lambda b, j: (qrow(b, j), 3)),
                  pl.BlockSpec(memory_space=pl.ANY)],
        out_specs=pl.BlockSpec((SWA_QB, BR_W), lambda b, j: (qrow(b, j), 0)),
        out_shape=jax.ShapeDtypeStruct((N_TOK, BR_W), BF16),
        input_output_aliases={6: 0},
        compiler_params=_cparams(("arbitrary", "arbitrary")),
        name=f"lat_swa_l{l}",
    )(proj, proj, cache_k, cache_v, sink_l, proj, z_d)


def _lru_kernel(seq, has_state, *refs):
    if has_state:
        bx_ref, cw_ref, cb_ref, wg_ref, bg_ref, lam_ref, gr_ref, st_ref, zin_ref, zr_ref, a_s, u_s, y_s = refs
        del zin_ref
        fin_ref = None
    else:
        bx_ref, cw_ref, cb_ref, wg_ref, bg_ref, lam_ref, gr_ref = refs[:7]
        zr_ref, fin_ref, a_s, u_s, y_s = refs[-5:]
    x = bx_ref[...]
    t = lax.broadcasted_iota(jnp.int32, (seq, 1), 0)
    cw = cw_ref[...]
    xm1 = jnp.where(t >= 1, pltpu.roll(x, 1, 0), 0.0)
    xp1 = jnp.where(t < seq - 1, pltpu.roll(x, seq - 1, 0), 0.0)
    xp2 = jnp.where(t < seq - 2, pltpu.roll(x, seq - 2, 0), 0.0)
    xc = cb_ref[...] + xm1 * cw[0:1] + x * cw[1:2] + xp1 * cw[2:3] + xp2 * cw[3:4]

    gates = _dot(xc.astype(BF16), wg_ref[...]) + bg_ref[...]
    lam = lam_ref[...]
    nl = -lam
    softplus = jnp.maximum(nl, 0.0) + jnp.log1p(jnp.exp(-jnp.abs(nl)))
    tin = t % 8
    for d in range(2):
        r = jax.nn.sigmoid(gates[:, (2 * d) * BR_W:(2 * d + 1) * BR_W])
        ig = jax.nn.sigmoid(gates[:, (2 * d + 1) * BR_W:(2 * d + 2) * BR_W])
        log_a = -LRU_C * r * softplus[d:d + 1]
        a = jnp.exp(log_a)
        u = jnp.sqrt(-jnp.tanh(log_a) * (a * a + 1.0)) * (ig * xc)
        for s in (1, 2, 4):
            if d == 0:
                keep = tin >= s
                a_n = jnp.where(keep, pltpu.roll(a, s, 0), 1.0)
                u_n = jnp.where(keep, pltpu.roll(u, s, 0), 0.0)
            else:
                keep = tin < 8 - s
                a_n = jnp.where(keep, pltpu.roll(a, seq - s, 0), 1.0)
                u_n = jnp.where(keep, pltpu.roll(u, seq - s, 0), 0.0)
            u = u + a * u_n
            a = a * a_n
        a_s[d] = a
        u_s[d] = u

    if has_state:
        hf0 = st_ref[0, 0, 0:1, :]
        hb0 = st_ref[0, 0, 1:2, :]
    else:
        hf0 = jnp.zeros((1, BR_W), F32)
        hb0 = jnp.zeros((1, BR_W), F32)

    nt = seq // 8

    def body(k, carry):
        hf, hb = carry
        rf = pl.ds(pl.multiple_of(k * 8, 8), 8)
        rb = pl.ds(pl.multiple_of((nt - 1 - k) * 8, 8), 8)
        yf = u_s[0, rf, :] + a_s[0, rf, :] * hf
        yb = u_s[1, rb, :] + a_s[1, rb, :] * hb
        y_s[0, rf, :] = yf
        y_s[1, rb, :] = yb
        return yf[7:8, :], yb[0:1, :]

    hf, hb = lax.fori_loop(0, nt, body, (hf0, hb0))
    zr_ref[...] = ((y_s[0] + y_s[1]) * gr_ref[...].astype(F32)).astype(BF16)
    if fin_ref is not None:
        fin_ref[0, 0, 0:1, :] = hf
        fin_ref[0, 0, 1:2, :] = hb


def _lru(seq, nseq, row0, bx, proj, cw_l, cb_l, wg_l, bg_l, lam_l, state=None, l=0, z_r=None, states=None):
    has_state = state is not None
    rblk = lambda b: (row0 // seq + b, 0)
    in_specs = [pl.BlockSpec((seq, BR_W), rblk),
                pl.BlockSpec((CONV_W, BR_W), lambda b: (0, 0)),
                pl.BlockSpec((1, BR_W), lambda b: (0, 0)),
                pl.BlockSpec((BR_W, 4 * BR_W), lambda b: (0, 0)),
                pl.BlockSpec((1, 4 * BR_W), lambda b: (0, 0)),
                pl.BlockSpec((2, BR_W), lambda b: (0, 0)),
                pl.BlockSpec((seq, 256), lambda b: (row0 // seq + b, 1))]
    args = [bx, cw_l, cb_l, wg_l, bg_l, lam_l, proj]
    zspec = pl.BlockSpec((seq, BR_W), rblk)
    zshape = jax.ShapeDtypeStruct((N_TOK, BR_W), BF16)
    scratch = [pltpu.VMEM((2, seq, BR_W), F32)] * 3
    if has_state:
        in_specs += [pl.BlockSpec((1, 1, 2, BR_W), lambda b: (b, l, 0, 0)), pl.BlockSpec(memory_space=pl.ANY)]
        args += [state, z_r]
        return pl.pallas_call(
            functools.partial(_lru_kernel, seq, True),
            grid=(nseq,), in_specs=in_specs, out_specs=zspec, out_shape=zshape,
            scratch_shapes=scratch, input_output_aliases={8: 0},
            compiler_params=_cparams(("arbitrary",)), name=f"lru_lat_l{l}",
        )(*args)
    aliases = {}
    if states is not None:
        in_specs += [pl.BlockSpec(memory_space=pl.ANY)]
        args += [states]
        aliases = {7: 1}
    return pl.pallas_call(
        functools.partial(_lru_kernel, seq, False),
        grid=(nseq,), in_specs=in_specs,
        out_specs=[zspec, pl.BlockSpec((1, 1, 2, BR_W), lambda b: (b, l, 0, 0))],
        out_shape=[zshape, jax.ShapeDtypeStruct((nseq, DEPTH, 2, BR_W), F32)],
        scratch_shapes=scratch, input_output_aliases=aliases,
        compiler_params=_cparams(("arbitrary",)), name=f"lru_ctx_l{l}",
    )(*args)


def _merge_kernel(final, xa_ref, xb_ref, mod_ref, h_ref, za_ref, zr_ref, zc_ref, zd_ref,
                  wmg_ref, bmg_ref, wbo_ref, wo_ref, nf_ref, *refs):
    out_refs, (wmg_s, wbo_s, wo_s) = refs[:-3], refs[-3:]
    i = pl.program_id(0)

    @pl.when(i < W_STEPS)
    def _():
        r0 = pl.multiple_of(i * W_ROWS, W_ROWS)
        for n in range(N_BRANCH):
            wmg_s[pl.ds(r0, W_ROWS), n * D_MODEL:(n + 1) * D_MODEL] = wmg_ref[0, :, n, :].astype(BF16)
        _cast_rows(i, wbo_ref, wbo_s)
        _cast_rows(i, wo_ref, wo_s)

    @pl.when(i >= W_STEPS)
    def _():
        t = i - W_STEPS
        x = _load_x(t, xa_ref, xb_ref)
        gate = mod_ref[0][:, 2 * D_MODEL:]
        h = h_ref[...]
        zs = [za_ref[...], zr_ref[...], zc_ref[...], zd_ref[...]]
        bmg = bmg_ref[0]
        cols = []
        for c in range(0, D_MODEL, 512):
            acc = None
            for n in range(N_BRANCH):
                g = jax.nn.sigmoid(_dot(h, wmg_s[:, n * D_MODEL + c:n * D_MODEL + c + 512])
                                   + bmg[n:n + 1, c:c + 512])
                term = g * _dot(zs[n], wbo_s[n * BR_W:(n + 1) * BR_W, c:c + 512])
                acc = term if acc is None else acc + term
            cols.append(acc.astype(BF16))
        merged = jnp.concatenate(cols, axis=1)
        xn = x + gate * _dot(merged, wo_s[...])
        if not final:
            out_refs[0][...] = xn
        else:
            y = xn * lax.rsqrt(jnp.mean(xn * xn, axis=-1, keepdims=True) + EPS) * nf_ref[...]

            @pl.when(t < N_CTX_TILES)
            def _():
                out_refs[0][...] = y

            @pl.when(t >= N_CTX_TILES)
            def _():
                out_refs[1][...] = y


def _merge(final, xa, xb, xb_off, mod3, l, h, zs, w_mg, b_mg, w_bo, w_o, norm_f):
    row = lambda i: (_tile(i), 0)
    ctx_blk = lambda i: jnp.minimum(_tile(i), N_CTX_TILES - 1)
    lat_blk = lambda i: jnp.maximum(_tile(i) - N_CTX_TILES, 0)
    if final:
        out_specs = [pl.BlockSpec((TM, D_MODEL), lambda i: (ctx_blk(i), 0)),
                     pl.BlockSpec((TM, D_MODEL), lambda i: (lat_blk(i), 0))]
        out_shape = [jax.ShapeDtypeStruct((N_CTX, D_MODEL), F32), jax.ShapeDtypeStruct((N_LAT, D_MODEL), F32)]
    else:
        out_specs = pl.BlockSpec((TM, D_MODEL), row)
        out_shape = jax.ShapeDtypeStruct((N_TOK, D_MODEL), F32)
    wchunk = lambda width: pl.BlockSpec((1, W_ROWS, width), lambda i: (l, _wchunk(i), 0))
    return pl.pallas_call(
        functools.partial(_merge_kernel, final),
        grid=(W_STEPS + N_TILES,),
        in_specs=[
            pl.BlockSpec((TM, D_MODEL), lambda i: (ctx_blk(i), 0)),
            pl.BlockSpec((TM, D_MODEL), lambda i: (lat_blk(i) + xb_off, 0)),
            pl.BlockSpec((1, 1, 3 * D_MODEL), lambda i: (l * 16 + _mod_row(_tile(i)), 0, 0)),
            pl.BlockSpec((TM, D_MODEL), row),
            pl.BlockSpec((TM, BR_W), row), pl.BlockSpec((TM, BR_W), row),
            pl.BlockSpec((TM, BR_W), row), pl.BlockSpec((TM, BR_W), row),
            pl.BlockSpec((1, W_ROWS, N_BRANCH, D_MODEL), lambda i: (l, _wchunk(i), 0, 0)),
            pl.BlockSpec((1, N_BRANCH, D_MODEL), lambda i: (l, 0, 0)),
            wchunk(D_MODEL),
            wchunk(D_MODEL),
            pl.BlockSpec((1, D_MODEL), lambda i: (0, 0)),
        ],
        out_specs=out_specs,
        out_shape=out_shape,
        scratch_shapes=[pltpu.VMEM((D_MODEL, N_BRANCH * D_MODEL), BF16),
                        pltpu.VMEM((N_BRANCH * BR_W, D_MODEL), BF16),
                        pltpu.VMEM((D_MODEL, D_MODEL), BF16)],
        compiler_params=_cparams(("arbitrary",)),
        name=f"merge_l{l}",
    )(xa, xb, mod3, h, *zs,
      w_mg, b_mg,
      w_bo.reshape(DEPTH, N_BRANCH * BR_W, D_MODEL), w_o, norm_f.reshape(1, D_MODEL))


def _lru_gate_weights(lru_wa, lru_ba, lru_wx, lru_bx):
    w = jnp.stack([lru_wa[:, 0], lru_wx[:, 0], lru_wa[:, 1], lru_wx[:, 1]], axis=1)
    eye = jnp.eye(B_BLOCKS, dtype=w.dtype)
    dense = w[:, :, :, :, None, :] * eye[None, None, :, None, :, None]
    wg = dense.transpose(0, 2, 3, 1, 4, 5).reshape(DEPTH, BR_W, 4 * BR_W).astype(BF16)
    bg = jnp.stack([lru_ba[:, 0], lru_bx[:, 0], lru_ba[:, 1], lru_bx[:, 1]], axis=1).reshape(DEPTH, 1, 4 * BR_W)
    return wg, bg


def kernel(x_prompt, x_sample, cache_diff_k, cache_diff_v, cache_na_k, cache_na_v, cache_swa_k, cache_swa_v,
           state_lru, c, c_ctx, norm_g, w_ada, b_ada, w_in, diff_lambda, diff_norm_g, conv_w, conv_b,
           lru_wa, lru_ba, lru_wx, lru_bx, lru_lam, na_rpb, swa_sink, w_mg, b_mg, w_bo, w_o, norm_f):
    tab_a_np, _ = _rope_tables(A_QK)
    tab_d_np, _ = _rope_tables(HD)
    tab_a, tab_d = jnp.asarray(tab_a_np), jnp.asarray(tab_d_np)

    cvecs = jnp.concatenate([c_ctx[None, :], c, jnp.zeros((16 - 1 - DEC_BATCH, D_MODEL), F32)], axis=0)
    mod3 = _modulation(cvecs, w_ada, b_ada).reshape(DEPTH * 16, 1, 3 * D_MODEL)
    bias = _na_bias(na_rpb)
    wg, bg = _lru_gate_weights(lru_wa, lru_ba, lru_wx, lru_bx)
    norm_g3 = norm_g.reshape(DEPTH, 1, D_MODEL)
    past = [jnp.swapaxes(t, -1, -2) for t in
            (cache_diff_k, cache_diff_v, cache_na_k, cache_na_v, cache_swa_k, cache_swa_v)]

    xa = x_prompt.reshape(N_CTX, D_MODEL)
    xb = x_sample.reshape(N_LAT, D_MODEL)
    xb_off = 0
    caches = states = None
    y_p = y_s = None
    for l in range(DEPTH):
        h, proj, bx, caches = _inproj(xa, xb, xb_off, mod3, l, norm_g3, w_in, tab_a, tab_d, caches)

        lam_l, dg_l, sink_l = diff_lambda[l], diff_norm_g[l][None, :], swa_sink[l][None, :]
        z_a, z_c, z_d = _ctx_attn(proj, l, lam_l, dg_l, sink_l)
        z_a = _lat_diff(proj, past[0], past[1], l, lam_l, dg_l, z_a)
        z_c = _lat_na(proj, past[2], past[3], bias, l, z_c)
        z_d = _lat_swa(proj, past[4], past[5], sink_l, l, z_d)

        lru_args = (bx, proj, conv_w[l], conv_b[l][None, :], wg[l], bg[l], lru_lam[l])
        z_r, states = _lru(SEQ, BATCH, 0, *lru_args, l=l, states=states)
        z_r = _lru(DEC_SEQ, DEC_BATCH, N_CTX, *lru_args, state=state_lru, l=l, z_r=z_r)

        final = l == DEPTH - 1
        out = _merge(final, xa, xb, xb_off, mod3, l, h, (z_a, z_r, z_c, z_d), w_mg, b_mg, w_bo, w_o, norm_f)
        if final:
            y_p, y_s = out
        else:
            xa = xb = out
            xb_off = N_CTX_TILES

    new_caches = [jnp.swapaxes(t, -1, -2) for t in caches]
    return (y_p.reshape(BATCH, SEQ, D_MODEL), y_s.reshape(DEC_BATCH, DEC_SEQ, D_MODEL), *new_caches, states)
```

```python
import functools
import math

import numpy as np
import jax
import jax.numpy as jnp
from jax import lax
from jax.experimental import pallas as pl
from jax.experimental.pallas import tpu as pltpu

F32 = jnp.float32
BF16 = jnp.bfloat16

D_MODEL = 1024
BATCH = 16
SEQ = 256
DEPTH = 2
DEC_BATCH = 8
DEC_SEQ = 1024
PAST_LEN = 512
GRID_W = 64
N_BRANCH = 4
BR_W = D_MODEL // 4
HD = 64
A_HEADS = BR_W // HD
A_QK = HD // 2
B_BLOCKS = 4
B_BLK = BR_W // B_BLOCKS
CONV_W = 4
LRU_C = 8.0
C_HEADS = BR_W // HD
NA_ROWS = 8
NA_COLS = 16
D_HEADS = BR_W // HD
D_KV = 2
D_GROUP = D_HEADS // D_KV
WIN = 128
ROPE_BASE = 10000.0
EPS = 1e-6
NEG = -1e30

N_CTX = BATCH * SEQ
N_LAT = DEC_BATCH * DEC_SEQ
N_TOK = N_CTX + N_LAT
GRID_ROWS = DEC_SEQ // GRID_W

TM = 512
N_CTX_TILES = N_CTX // TM
N_TILES = N_TOK // TM
LAT_TILES_PER_SEQ = DEC_SEQ // TM

PROJ_W = 3072
COL_G = 0
COL_AQ = 1024
COL_AK = 1280
COL_AV = 1536
COL_CQ = 1792
COL_CK = 2048
COL_CV = 2304
COL_DQ = 2560
COL_DKV = 2816
COL_BX = 3072
IN_W = 3328
CACHE_W = 1280

NA_HALF_Q = DEC_SEQ // 2
NA_KROWS = 12
NA_KSPAN = NA_KROWS * GRID_W
NA_HALF_KSTART = 4 * GRID_W

SWA_QB = 128
SWA_SPAN = SWA_QB + 2 * WIN

LOG2E = math.log2(math.e)
QA_SCALE = A_QK ** -0.5 * LOG2E
QH_SCALE = HD ** -0.5 * LOG2E

VMEM_LIMIT = 56 * 1024 * 1024


def _cparams(sem):
    return pltpu.CompilerParams(dimension_semantics=sem, vmem_limit_bytes=VMEM_LIMIT)


def _resident(shape, index_map):
    return pl.BlockSpec(shape, index_map, pipeline_mode=pl.Buffered(1))


def _dot(a, b):
    return jnp.dot(a, b, preferred_element_type=F32)


def _dot_nt(a, b):
    return lax.dot_general(a, b, (((1,), (1,)), ((), ())), preferred_element_type=F32)


def _silu(x):
    return x * jax.nn.sigmoid(x)


def _rope_tables(d):
    half = d // 2
    quarter = half // 2
    lane = np.arange(128)
    q = lane % d
    use_col = (q >= half)
    i = (q % half) % quarter
    first = (q % half) < quarter
    inv = (ROPE_BASE ** (-np.arange(quarter, dtype=np.float32) / np.float32(quarter))).astype(np.float32)
    t = np.arange(DEC_SEQ)
    pos = np.where(use_col[None, :], (t % GRID_W)[:, None], (t // GRID_W)[:, None]).astype(np.float32)
    ang = (pos * inv[i][None, :]).astype(np.float32)
    cos, sin = np.cos(ang), np.sin(ang)
    c = np.concatenate([np.ones((TM, 128), np.float32), cos], axis=0)
    s1 = np.concatenate([np.zeros((TM, 128), np.float32), np.where(first[None, :], -sin, 0.0)], axis=0)
    s2 = np.concatenate([np.zeros((TM, 128), np.float32), np.where(first[None, :], 0.0, sin)], axis=0)
    return np.stack([c, s1, s2]).astype(np.float32), quarter


def _rope128(x, tab_ref, shift):
    return (x * tab_ref[0] + pltpu.roll(x, 128 - shift, 1) * tab_ref[1]
            + pltpu.roll(x, shift, 1) * tab_ref[2])


def _rope(x, tab_ref, shift):
    w = x.shape[1]
    return jnp.concatenate([_rope128(x[:, c:c + 128], tab_ref, shift) for c in range(0, w, 128)], axis=1)


def _mod_kernel(c_ref, w_ref, b_ref, o_ref):
    c = c_ref[...]
    o_ref[0] = _dot(_silu(c).astype(BF16), w_ref[0].astype(BF16)) + b_ref[0]


def _modulation(cvecs, w_ada, b_ada):
    nb = 3 * D_MODEL // 768
    return pl.pallas_call(
        _mod_kernel,
        grid=(DEPTH, nb),
        in_specs=[pl.BlockSpec((16, D_MODEL), lambda l, j: (0, 0)),
                  pl.BlockSpec((1, D_MODEL, 768), lambda l, j: (l, 0, j)),
                  pl.BlockSpec((1, 1, 768), lambda l, j: (l, 0, j))],
        out_specs=pl.BlockSpec((1, 16, 768), lambda l, j: (l, 0, j)),
        out_shape=jax.ShapeDtypeStruct((DEPTH, 16, 3 * D_MODEL), F32),
        compiler_params=_cparams(("arbitrary", "arbitrary")),
        name="modulation",
    )(cvecs, w_ada, b_ada.reshape(DEPTH, 1, 3 * D_MODEL))


W_STEPS = 8
W_ROWS = D_MODEL // W_STEPS


def _tile(i):
    return jnp.maximum(i - W_STEPS, 0)


def _wchunk(i):
    return jnp.minimum(i, W_STEPS - 1)


def _mod_row(t):
    return jnp.where(t < N_CTX_TILES, 0, 1 + (t - N_CTX_TILES) // LAT_TILES_PER_SEQ)


def _rope_blk(t):
    return jnp.where(t < N_CTX_TILES, 0, 1 + (t - N_CTX_TILES) % LAT_TILES_PER_SEQ)


def _load_x(t, xa_ref, xb_ref):
    return jnp.where(t < N_CTX_TILES, xa_ref[...], xb_ref[...])


def _cast_rows(i, src_ref, dst_ref):
    r0 = pl.multiple_of(i * W_ROWS, W_ROWS)
    dst_ref[pl.ds(r0, W_ROWS), :] = src_ref[0].astype(BF16)


WI_G, WI_AQ, WI_AK, WI_AV, WI_BX, WI_CQ, WI_CKV, WI_DQ, WI_DKV = 0, 1024, 1280, 1536, 1792, 2048, 2304, 2816, 3072
SEQ_PER_TILE = TM // SEQ


def _store_heads(ref, val, n_heads, width):
    vt = val.T
    for s in range(SEQ_PER_TILE):
        for h in range(n_heads):
            ref[s, 0, h] = vt[h * width:(h + 1) * width, s * SEQ:(s + 1) * SEQ]


def _inproj_kernel(n_alias, xa_ref, xb_ref, mod_ref, g_ref, w_ref, ta_ref, td_ref, *refs):
    h_ref, proj_ref, bx_ref, dk_ref, dv_ref, nk_ref, nv_ref, sk_ref, sv_ref, w_s = refs[n_alias:]
    i = pl.program_id(0)

    @pl.when(i < W_STEPS)
    def _():
        _cast_rows(i, w_ref, w_s)

    @pl.when(i >= W_STEPS)
    def _():
        t = i - W_STEPS
        is_ctx = t < N_CTX_TILES
        x = _load_x(t, xa_ref, xb_ref)
        mod = mod_ref[0]
        shift, scale = mod[:, :D_MODEL], mod[:, D_MODEL:2 * D_MODEL]
        xn = x * lax.rsqrt(jnp.mean(x * x, axis=-1, keepdims=True) + EPS) * g_ref[0]
        hb =(xn * (1.0 + scale) + shift).astype(BF16)
        h_ref[...] = hb

        def mm(c0, width):
            return _dot(hb, w_s[:, c0:c0 + width])

        for c in range(0, 1024, 512):
            proj_ref[:, COL_G + c:COL_G + c + 512] = _silu(mm(WI_G + c, 512)).astype(BF16)

        aq = mm(WI_AQ, 256) * QA_SCALE
        proj_ref[:, COL_AQ:COL_AQ + 256] = _rope(aq, ta_ref, A_QK // 4).astype(BF16)
        ak = mm(WI_AK, 256)
        proj_ref[:, COL_AK:COL_AK + 256] = _rope(ak, ta_ref, A_QK // 4).astype(BF16)
        av = mm(WI_AV, 256)
        proj_ref[:, COL_AV:COL_AV + 256] = av.astype(BF16)

        cq = mm(WI_CQ, 256) * QH_SCALE
        proj_ref[:, COL_CQ:COL_CQ + 256] = cq.astype(BF16)
        ckv = mm(WI_CKV, 512)
        proj_ref[:, COL_CK:COL_CK + 512] = ckv.astype(BF16)

        dq = mm(WI_DQ, 256) * QH_SCALE
        proj_ref[:, COL_DQ:COL_DQ + 256] = _rope(dq, td_ref, HD // 4).astype(BF16)
        dkv = mm(WI_DKV, 256)
        proj_ref[:, COL_DKV:COL_DKV + 128] = _rope(dkv[:, :128], td_ref, HD // 4).astype(BF16)
        proj_ref[:, COL_DKV + 128:COL_DKV + 256] = dkv[:, 128:].astype(BF16)

        bx_ref[...] = mm(WI_BX, 256)

        @pl.when(is_ctx)
        def _():
            akt = ak.T
            for s in range(SEQ_PER_TILE):
                for h in range(A_HEADS):
                    for m in range(2):
                        c0 = h * HD + m * A_QK
                        dk_ref[s, 0, h, m] = akt[c0:c0 + A_QK, s * SEQ:(s + 1) * SEQ]
            _store_heads(dv_ref, av, A_HEADS, HD)
            _store_heads(nk_ref, ckv[:, :256], C_HEADS, HD)
            _store_heads(nv_ref, ckv[:, 256:], C_HEADS, HD)
            _store_heads(sk_ref, dkv[:, :128], D_KV, HD)
            _store_heads(sv_ref, dkv[:, 128:], D_KV, HD)


def _cache_shapes():
    hs = lambda n: (BATCH, DEPTH, n, HD, SEQ)
    return [(BATCH, DEPTH, A_HEADS, 2, A_QK, SEQ), hs(A_HEADS), hs(C_HEADS), hs(C_HEADS), hs(D_KV), hs(D_KV)]


def _inproj(xa, xb, xb_off, mod3, l, norm_g, w_in, tab_a, tab_d, caches):
    row = lambda i: (_tile(i), 0)
    ctx_blk = lambda i: jnp.minimum(_tile(i), N_CTX_TILES - 1)

    def cache_spec(shape):
        blk = (SEQ_PER_TILE, 1) + shape[2:]
        nz = len(shape) - 2
        return pl.BlockSpec(blk, lambda i: (ctx_blk(i), l) + (0,) * nz)

    cshapes = _cache_shapes()
    aliases = {} if caches is None else {7 + k: 3 + k for k in range(6)}
    extra_specs = [] if caches is None else [pl.BlockSpec(memory_space=pl.ANY)] * 6
    extra_args = [] if caches is None else list(caches)
    outs = pl.pallas_call(
        functools.partial(_inproj_kernel, len(extra_args)),
        grid=(W_STEPS + N_TILES,),
        in_specs=[
            pl.BlockSpec((TM, D_MODEL), lambda i: (ctx_blk(i), 0)),
            pl.BlockSpec((TM, D_MODEL), lambda i: (jnp.maximum(_tile(i) - N_CTX_TILES, 0) + xb_off, 0)),
            pl.BlockSpec((1, 1, 3 * D_MODEL), lambda i: (l * 16 + _mod_row(_tile(i)), 0, 0)),
            pl.BlockSpec((1, 1, D_MODEL), lambda i: (l, 0, 0)),
            pl.BlockSpec((1, W_ROWS, IN_W), lambda i: (l, _wchunk(i), 0)),
            pl.BlockSpec((3, TM, 128), lambda i: (0, _rope_blk(_tile(i)), 0)),
            pl.BlockSpec((3, TM, 128), lambda i: (0, _rope_blk(_tile(i)), 0)),
        ] + extra_specs,
        out_specs=[
            pl.BlockSpec((TM, D_MODEL), row),
            pl.BlockSpec((TM, PROJ_W), row),
            pl.BlockSpec((TM, BR_W), row),
        ] + [cache_spec(s) for s in cshapes],
        out_shape=[
            jax.ShapeDtypeStruct((N_TOK, D_MODEL), BF16),
            jax.ShapeDtypeStruct((N_TOK, PROJ_W), BF16),
            jax.ShapeDtypeStruct((N_TOK, BR_W), F32),
        ] + [jax.ShapeDtypeStruct(s, F32) for s in cshapes],
        scratch_shapes=[pltpu.VMEM((D_MODEL, IN_W), BF16)],
        input_output_aliases=aliases,
        compiler_params=_cparams(("arbitrary",)),
        name=f"inproj_l{l}",
    )(xa, xb, mod3, norm_g, w_in, tab_a, tab_d, *extra_args)
    return outs[0], outs[1], outs[2], tuple(outs[3:])


def _diff_lambda(lam_ref, lam_init):
    lv = lam_ref[...]
    s1 = jnp.sum(lv[0:1] * lv[1:2], axis=-1, keepdims=True)
    s2 = jnp.sum(lv[2:3] * lv[3:4], axis=-1, keepdims=True)
    return jnp.exp(s1) - jnp.exp(s2) + lam_init


def _diff_norm(o, dg_ref, lam_init):
    y = o * lax.rsqrt(jnp.mean(o * o, axis=-1, keepdims=True) + EPS) * dg_ref[...]
    return y * (1.0 - lam_init)


def _exp2_parts(parts, extra=None):
    m = functools.reduce(jnp.maximum, [jnp.max(s, axis=-1, keepdims=True) for s in parts])
    if extra is not None:
        m = jnp.maximum(m, extra)
    return [jnp.exp2(s - m).astype(BF16) for s in parts], m


def _with_ones(v):
    return jnp.concatenate([v, jnp.ones_like(v)], axis=1)


def _with_ones_t(vt):
    return jnp.concatenate([vt, jnp.ones((16, vt.shape[1]), vt.dtype)], axis=0)


def _stack_group(dq, kv, rows):
    return jnp.concatenate([dq[:, (kv * D_GROUP + g) * HD:(kv * D_GROUP + g + 1) * HD]
                            for g in range(D_GROUP)], axis=0)


def _sink_col(sink2, kv, rows):
    r = lax.broadcasted_iota(jnp.int32, (D_GROUP * rows, 1), 0)
    return jnp.where(r < rows, sink2[:, kv * D_GROUP:kv * D_GROUP + 1], sink2[:, kv * D_GROUP + 1:kv * D_GROUP + 2])


CTX_PER_STEP = 2

def _ctx_attn_kernel(lam_init, aq_ref, ak_ref, av_ref, cq_ref, ck_ref, cv_ref, dq_ref, dkv_ref,
                     ga_ref, gc_ref, gd_ref, lam_ref, dg_ref, sink_ref, za_ref, zc_ref, zd_ref):
    lam = _diff_lambda(lam_ref, lam_init)
    sink2 = sink_ref[...] * LOG2E
    sks = [_sink_col(sink2, kv, SEQ) for kv in range(D_KV)]
    seqs = [slice(s * SEQ, (s + 1) * SEQ) for s in range(CTX_PER_STEP)]

    sa, sc, sd = [], [], []
    for rs in seqs:
        aq, ak = aq_ref[rs, :], ak_ref[rs, :]
        cq, ck = cq_ref[rs, :], ck_ref[rs, :]
        dq, dkv = dq_ref[rs, :], dkv_ref[rs, :]
        sa.append([_dot_nt(aq[:, c:c + A_QK], ak[:, c:c + A_QK]) for c in range(0, BR_W, A_QK)])
        sc.append([_dot_nt(cq[:, c:c + HD], ck[:, c:c + HD]) for c in range(0, BR_W, HD)])
        sd.append([_dot_nt(_stack_group(dq, kv, SEQ), dkv[:, kv * HD:(kv + 1) * HD]) for kv in range(D_KV)])

    ea = [[_exp2_parts([s])[0][0] for s in ss] for ss in sa]
    ec = [[_exp2_parts([s])[0][0] for s in ss] for ss in sc]
    ed = [[_exp2_parts([s], extra=sk) for s, sk in zip(ss, sks)] for ss in sd]

    for i, rs in enumerate(seqs):
        av, cv, dkv = av_ref[rs, :], cv_ref[rs, :], dkv_ref[rs, :]
        outs = []
        for h in range(A_HEADS):
            va = _with_ones(av[:, h * HD:(h + 1) * HD])
            p0, p1 = _dot(ea[i][2 * h], va), _dot(ea[i][2 * h + 1], va)
            o = p0[:, :HD] * (1.0 / p0[:, HD:HD + 1]) - p1[:, :HD] * (lam / p1[:, HD:HD + 1])
            outs.append(_diff_norm(o, dg_ref, lam_init))
        za_ref[rs, :] = (jnp.concatenate(outs, axis=1) * ga_ref[rs, :].astype(F32)).astype(BF16)

        outs = []
        for h in range(C_HEADS):
            p = _dot(ec[i][h], _with_ones(cv[:, h * HD:(h + 1) * HD]))
            outs.append(p[:, :HD] * (1.0 / p[:, HD:HD + 1]))
        zc_ref[rs, :] = (jnp.concatenate(outs, axis=1) * gc_ref[rs, :].astype(F32)).astype(BF16)

        outs = []
        for kv in range(D_KV):
            (e,), m = ed[i][kv]
            p = _dot(e, _with_ones(dkv[:, D_KV * HD + kv * HD:D_KV * HD + (kv + 1) * HD]))
            o = p[:, :HD] * (1.0 / (p[:, HD:HD + 1] + jnp.exp2(sks[kv] - m)))
            outs += [o[g * SEQ:(g + 1) * SEQ] for g in range(D_GROUP)]
        zd_ref[rs, :] = (jnp.concatenate(outs, axis=1) * gd_ref[rs, :].astype(F32)).astype(BF16)


def _ctx_attn(proj, l, lam_l, dg_l, sink_l):
    lam_init = 0.8 - 0.6 * math.exp(-0.3 * l)
    colblk = lambda cb: pl.BlockSpec((CTX_PER_STEP * SEQ, 256), lambda b: (b, cb))
    zspec = pl.BlockSpec((CTX_PER_STEP * SEQ, BR_W), lambda b: (b, 0))
    zshape = jax.ShapeDtypeStruct((N_TOK, BR_W), BF16)
    return pl.pallas_call(
        functools.partial(_ctx_attn_kernel, lam_init),
        grid=(BATCH // CTX_PER_STEP,),
        in_specs=[colblk(COL_AQ // 256), colblk(COL_AK // 256), colblk(COL_AV // 256),
                  colblk(COL_CQ // 256), colblk(COL_CK // 256), colblk(COL_CV // 256),
                  colblk(COL_DQ // 256), colblk(COL_DKV // 256),
                  colblk(0), colblk(2), colblk(3),
                  pl.BlockSpec((4, A_QK), lambda b: (0, 0)),
                  pl.BlockSpec((1, HD), lambda b: (0, 0)),
                  pl.BlockSpec((1, D_HEADS), lambda b: (0, 0))],
        out_specs=[zspec, zspec, zspec],
        out_shape=[zshape, zshape, zshape],
        compiler_params=_cparams(("arbitrary",)),
        name=f"ctx_attn_l{l}",
    )(*([proj] * 11), lam_l, dg_l, sink_l)


LA_QB = 256


def _lat_diff_kernel(lam_init, aq_ref, ak_ref, av_ref, ck_ref, cv_ref, ga_ref, lam_ref, dg_ref, zin_ref, za_ref):
    del zin_ref
    lam = _diff_lambda(lam_ref, lam_init)
    aq = aq_ref[...]
    scores = []
    for h in range(A_HEADS):
        for m in range(2):
            c = h * HD + m * A_QK
            q = aq[:, c:c + A_QK]
            scores.append([_dot(q, ck_ref[0, 0, h, m].astype(BF16)),
                           _dot_nt(q, ak_ref[:, c:c + A_QK])])
    exps = [_exp2_parts(p)[0] for p in scores]
    outs = []
    for h in range(A_HEADS):
        va_ctx = _with_ones_t(cv_ref[0, 0, h].astype(BF16))
        va_loc = _with_ones(av_ref[:, h * HD:(h + 1) * HD])
        num, den = [], []
        for m in range(2):
            e_ctx, e_loc = exps[2 * h + m]
            p_ctx, p_loc = _dot_nt(e_ctx, va_ctx), _dot(e_loc, va_loc)
            num.append(p_ctx[:, :HD] + p_loc[:, :HD])
            den.append(p_ctx[:, HD:HD + 1] + p_loc[:, HD:HD + 1])
        o = num[0] * (1.0 / den[0]) - num[1] * (lam / den[1])
        outs.append(_diff_norm(o, dg_ref, lam_init))
    za_ref[...] = (jnp.concatenate(outs, axis=1) * ga_ref[...].astype(F32)).astype(BF16)


def _lat_diff(proj, cache_k, cache_v, l, lam_l, dg_l, z_a):
    lam_init = 0.8 - 0.6 * math.exp(-0.3 * l)
    nq = DEC_SEQ // LA_QB
    qrow = lambda b, j: N_CTX // LA_QB + b * nq + j
    srow = lambda b, j: N_CTX // DEC_SEQ + b
    return pl.pallas_call(
        functools.partial(_lat_diff_kernel, lam_init),
        grid=(DEC_BATCH, nq),
        in_specs=[pl.BlockSpec((LA_QB, 256), lambda b, j: (qrow(b, j), COL_AQ // 256)),
                  pl.BlockSpec((DEC_SEQ, 256), lambda b, j: (srow(b, j), COL_AK // 256)),
                  pl.BlockSpec((DEC_SEQ, 256), lambda b, j: (srow(b, j), COL_AV // 256)),
                  pl.BlockSpec((1, 1, A_HEADS, 2, A_QK, PAST_LEN), lambda b, j: (b, l, 0, 0, 0, 0)),
                  pl.BlockSpec((1, 1, A_HEADS, HD, PAST_LEN), lambda b, j: (b, l, 0, 0, 0)),
                  pl.BlockSpec((LA_QB, 256), lambda b, j: (qrow(b, j), 0)),
                  pl.BlockSpec((4, A_QK), lambda b, j: (0, 0)),
                  pl.BlockSpec((1, HD), lambda b, j: (0, 0)),
                  pl.BlockSpec(memory_space=pl.ANY)],
        out_specs=pl.BlockSpec((LA_QB, BR_W), lambda b, j: (qrow(b, j), 0)),
        out_shape=jax.ShapeDtypeStruct((N_TOK, BR_W), BF16),
        input_output_aliases={8: 0},
        compiler_params=_cparams(("arbitrary", "arbitrary")),
        name=f"lat_diff_l{l}",
    )(proj, proj, proj, cache_k, cache_v, proj, lam_l, dg_l, z_a)


def _na_bias_kernel(rpb_ref, o_ref):
    l, h = pl.program_id(0), pl.program_id(1)
    base = (l * C_HEADS + h) * ((2 * NA_ROWS - 1) * (2 * NA_COLS - 1))
    qc = lax.broadcasted_iota(jnp.int32, (GRID_W, GRID_W), 0)
    kc = lax.broadcasted_iota(jnp.int32, (GRID_W, GRID_W), 1)
    cs = jnp.clip(qc - NA_COLS // 2, 0, GRID_W - NA_COLS)
    ok = (kc >= cs) & (kc < cs + NA_COLS)
    dcol = kc - qc + (NA_COLS - 1)
    neg = jnp.full((GRID_W, GRID_W), NEG, F32)
    tabs = []
    for d in range(2 * NA_ROWS - 1):
        t = neg
        for dc in range(2 * NA_COLS - 1):
            t = jnp.where(ok & (dcol == dc), rpb_ref[base + d * (2 * NA_COLS - 1) + dc] * LOG2E, t)
        tabs.append(t)
    for half in range(2):
        for rq in range(NA_ROWS):
            r = half * NA_ROWS + rq
            rs = min(max(r - NA_ROWS // 2, 0), GRID_ROWS - NA_ROWS)
            blks = []
            for kr in range(NA_KROWS):
                kabs = half * 4 + kr
                blks.append(tabs[kabs - r + NA_ROWS - 1] if rs <= kabs < rs + NA_ROWS else neg)
            o_ref[0, 0, half, rq * GRID_W:(rq + 1) * GRID_W, :] = jnp.concatenate(blks, axis=1)


def _na_bias(na_rpb):
    return pl.pallas_call(
        _na_bias_kernel,
        grid=(DEPTH, C_HEADS),
        in_specs=[pl.BlockSpec(memory_space=pltpu.SMEM)],
        out_specs=pl.BlockSpec((1, 1, 2, NA_HALF_Q, NA_KSPAN), lambda l, h: (l, h, 0, 0, 0)),
        out_shape=jax.ShapeDtypeStruct((DEPTH, C_HEADS, 2, NA_HALF_Q, NA_KSPAN), F32),
        compiler_params=_cparams(("arbitrary", "arbitrary")),
        name="na_bias",
    )(na_rpb.reshape(-1))


def _lat_na_kernel(cq_ref, ck_ref, cv_ref, kc_ref, vc_ref, bias_ref, gc_ref, zin_ref, zc_ref):
    del zin_ref
    half = pl.program_id(0)
    k0 = pl.multiple_of(half * NA_HALF_KSTART, NA_HALF_KSTART)
    cq = cq_ref[...]
    kl_all = ck_ref[pl.ds(k0, NA_KSPAN), :]
    vl_all = cv_ref[pl.ds(k0, NA_KSPAN), :]
    scores = []
    for h in range(C_HEADS):
        q = cq[:, h * HD:(h + 1) * HD]
        scores.append([_dot(q, kc_ref[0, 0, h].astype(BF16)),
                       _dot_nt(q, kl_all[:, h * HD:(h + 1) * HD]) + bias_ref[0, h, 0]])
    exps = [_exp2_parts(p)[0] for p in scores]
    outs = []
    for h in range(C_HEADS):
        e_ctx, e_loc = exps[h]
        p_ctx = _dot_nt(e_ctx, _with_ones_t(vc_ref[0, 0, h].astype(BF16)))
        p_loc = _dot(e_loc, _with_ones(vl_all[:, h * HD:(h + 1) * HD]))
        outs.append((p_ctx[:, :HD] + p_loc[:, :HD]) * (1.0 / (p_ctx[:, HD:HD + 1] + p_loc[:, HD:HD + 1])))
    zc_ref[...] = (jnp.concatenate(outs, axis=1) * gc_ref[...].astype(F32)).astype(BF16)


def _lat_na(proj, cache_k, cache_v, bias, l, z_c):
    qrow = lambda hf, b: N_CTX // NA_HALF_Q + b * 2 + hf
    srow = lambda hf, b: N_CTX // DEC_SEQ + b
    return pl.pallas_call(
        _lat_na_kernel,
        grid=(2, DEC_BATCH),
        in_specs=[pl.BlockSpec((NA_HALF_Q, 256), lambda hf, b: (qrow(hf, b), COL_CQ // 256)),
                  pl.BlockSpec((DEC_SEQ, 256), lambda hf, b: (srow(hf, b), COL_CK // 256)),
                  pl.BlockSpec((DEC_SEQ, 256), lambda hf, b: (srow(hf, b), COL_CV // 256)),
                  pl.BlockSpec((1, 1, C_HEADS, HD, PAST_LEN), lambda hf, b: (b, l, 0, 0, 0)),
                  pl.BlockSpec((1, 1, C_HEADS, HD, PAST_LEN), lambda hf, b: (b, l, 0, 0, 0)),
                  pl.BlockSpec((1, C_HEADS, 1, NA_HALF_Q, NA_KSPAN), lambda hf, b: (l, 0, hf, 0, 0)),
                  pl.BlockSpec((NA_HALF_Q, 256), lambda hf, b: (qrow(hf, b), 2)),
                  pl.BlockSpec(memory_space=pl.ANY)],
        out_specs=pl.BlockSpec((NA_HALF_Q, BR_W), lambda hf, b: (qrow(hf, b), 0)),
        out_shape=jax.ShapeDtypeStruct((N_TOK, BR_W), BF16),
        input_output_aliases={7: 0},
        compiler_params=_cparams(("arbitrary", "arbitrary")),
        name=f"lat_na_l{l}",
    )(proj, proj, proj, cache_k, cache_v, bias, proj, z_c)


def _lat_swa_kernel(dq_ref, dkv_ref, kc_ref, vc_ref, sink_ref, gd_ref, zin_ref, zd_ref):
    del zin_ref
    nq = DEC_SEQ // SWA_QB
    rows = D_GROUP * SWA_QB
    dq, dkv = dq_ref[...], dkv_ref[...]
    sink2 = sink_ref[...] * LOG2E
    starts = [min(max(j * SWA_QB - WIN, 0), DEC_SEQ - SWA_SPAN) for j in range(nq)]
    qi = lax.broadcasted_iota(jnp.int32, (rows, SWA_SPAN), 0) % SWA_QB
    ki = lax.broadcasted_iota(jnp.int32, (rows, SWA_SPAN), 1)
    masks = {off: jnp.abs(qi + off - ki) <= WIN for off in sorted({j * SWA_QB - s for j, s in enumerate(starts)})}

    qs, s_ctx, s_loc, sks = [], [], [], []
    for kv in range(D_KV):
        q_all = jnp.concatenate([dq[j * SWA_QB:(j + 1) * SWA_QB, (kv * D_GROUP + g) * HD:(kv * D_GROUP + g + 1) * HD]
                                 for j in range(nq) for g in range(D_GROUP)], axis=0)
        qs.append(q_all)
        s_ctx.append(_dot(q_all, kc_ref[0, 0, kv].astype(BF16)))
        s_loc.append([jnp.where(masks[j * SWA_QB - starts[j]],
                                _dot_nt(q_all[j * rows:(j + 1) * rows],
                                        dkv[starts[j]:starts[j] + SWA_SPAN, kv * HD:(kv + 1) * HD]), NEG)
                      for j in range(nq)])
        sks.append(_sink_col(sink2, kv, SWA_QB))

    e_ctx, e_loc, mx = [], [], []
    for kv in range(D_KV):
        m = jnp.concatenate([jnp.maximum(jnp.maximum(
            jnp.max(s_ctx[kv][j * rows:(j + 1) * rows], axis=-1, keepdims=True),
            jnp.max(s_loc[kv][j], axis=-1, keepdims=True)), sks[kv]) for j in range(nq)], axis=0)
        mx.append(m)
        e_ctx.append(jnp.exp2(s_ctx[kv] - m).astype(BF16))
        e_loc.append([jnp.exp2(s_loc[kv][j] - m[j * rows:(j + 1) * rows]).astype(BF16) for j in range(nq)])

    cols = [None] * D_HEADS
    for kv in range(D_KV):
        p_ctx = _dot_nt(e_ctx[kv], _with_ones_t(vc_ref[0, 0, kv].astype(BF16)))
        p_loc = jnp.concatenate(
            [_dot(e_loc[kv][j], _with_ones(dkv[starts[j]:starts[j] + SWA_SPAN,
                                                 D_KV * HD + kv * HD:D_KV * HD + (kv + 1) * HD]))
             for j in range(nq)], axis=0)
        sk_all = jnp.concatenate([sks[kv]] * nq, axis=0)
        den = p_ctx[:, HD:HD + 1] + p_loc[:, HD:HD + 1] + jnp.exp2(sk_all - mx[kv])
        o = (p_ctx[:, :HD] + p_loc[:, :HD]) * (1.0 / den)
        for g in range(D_GROUP):
            cols[kv * D_GROUP + g] = jnp.concatenate(
                [o[j * rows + g * SWA_QB:j * rows + (g + 1) * SWA_QB] for j in range(nq)], axis=0)
    zd_ref[...] = (jnp.concatenate(cols, axis=1) * gd_ref[...].astype(F32)).astype(BF16)


def _lat_swa(proj, cache_k, cache_v, sink_l, l, z_d):
    srow = lambda b: (N_CTX // DEC_SEQ + b)
    return pl.pallas_call(
        _lat_swa_kernel,
        grid=(DEC_BATCH,),
        in_specs=[pl.BlockSpec((DEC_SEQ, 256), lambda b: (srow(b), COL_DQ // 256)),
                  pl.BlockSpec((DEC_SEQ, 256), lambda b: (srow(b), COL_DKV // 256)),
                  pl.BlockSpec((1, 1, D_KV, HD, PAST_LEN), lambda b: (b, l, 0, 0, 0)),
                  pl.BlockSpec((1, 1, D_KV, HD, PAST_LEN), lambda b: (b, l, 0, 0, 0)),
                  pl.BlockSpec((1, D_HEADS), lambda b: (0, 0)),
                  pl.BlockSpec((DEC_SEQ, 256), lambda b: (srow(b), 3)),
                  pl.BlockSpec(memory_space=pl.ANY)],
        out_specs=pl.BlockSpec((DEC_SEQ, BR_W), lambda b: (srow(b), 0)),
        out_shape=jax.ShapeDtypeStruct((N_TOK, BR_W), BF16),
        input_output_aliases={6: 0},
        compiler_params=_cparams(("arbitrary",)),
        name=f"lat_swa_l{l}",
    )(proj, proj, cache_k, cache_v, sink_l, proj, z_d)


def _lru_kernel(seq, has_state, *refs):
    if has_state:
        bx_ref, cw_ref, cb_ref, wg_ref, bg_ref, lam_ref, gr_ref, st_ref, zin_ref, zr_ref, a_s, u_s, y_s = refs
        del zin_ref
        fin_ref = None
    else:
        bx_ref, cw_ref, cb_ref, wg_ref, bg_ref, lam_ref, gr_ref = refs[:7]
        zr_ref, fin_ref, a_s, u_s, y_s = refs[-5:]
    x = bx_ref[...]
    cw = cw_ref[...]
    xp_s = y_s.at[0]
    xp_s[0:8, :] = jnp.zeros((8, BR_W), F32)
    xp_s[8:8 + seq, :] = x
    xp_s[8 + seq:16 + seq, :] = jnp.zeros((8, BR_W), F32)
    xm1, xp1, xp2 = xp_s[7:7 + seq, :], xp_s[9:9 + seq, :], xp_s[10:10 + seq, :]
    xc = cb_ref[...] + xm1 * cw[0:1] + x * cw[1:2] + xp1 * cw[2:3] + xp2 * cw[3:4]

    gates = _dot(xc.astype(BF16), wg_ref[...]) + bg_ref[...]
    lam = lam_ref[...]
    nl = -lam
    softplus = jnp.maximum(nl, 0.0) + jnp.log1p(jnp.exp(-jnp.abs(nl)))
    tin = lax.broadcasted_iota(jnp.int32, (1, 8, 1), 1)
    for d in range(2):
        r = jax.nn.sigmoid(gates[:, (2 * d) * BR_W:(2 * d + 1) * BR_W])
        ig = jax.nn.sigmoid(gates[:, (2 * d + 1) * BR_W:(2 * d + 2) * BR_W])
        log_a = -LRU_C * r * softplus[d:d + 1]
        a = jnp.exp(log_a)
        u = jnp.sqrt(-jnp.tanh(log_a) * (a * a + 1.0)) * (ig * xc)
        a = a.reshape(seq // 8, 8, BR_W)
        u = u.reshape(seq // 8, 8, BR_W)
        for s in (1, 2, 4):
            if d == 0:
                keep = tin >= s
                a_n = jnp.where(keep, pltpu.roll(a, s, 1), 1.0)
                u_n = jnp.where(keep, pltpu.roll(u, s, 1), 0.0)
            else:
                keep = tin < 8 - s
                a_n = jnp.where(keep, pltpu.roll(a, 8 - s, 1), 1.0)
                u_n = jnp.where(keep, pltpu.roll(u, 8 - s, 1), 0.0)
            u = u + a * u_n
            a = a * a_n
        a_s[d] = a.reshape(seq, BR_W)
        u_s[d] = u.reshape(seq, BR_W)

    if has_state:
        hf0 = st_ref[0, 0, 0:1, :]
        hb0 = st_ref[0, 0, 1:2, :]
    else:
        hf0 = jnp.zeros((1, BR_W), F32)
        hb0 = jnp.zeros((1, BR_W), F32)

    nt = seq // 8

    def body(k, carry):
        hf, hb = carry
        rf = pl.ds(pl.multiple_of(k * 8, 8), 8)
        rb = pl.ds(pl.multiple_of((nt - 1 - k) * 8, 8), 8)
        yf = u_s[0, rf, :] + a_s[0, rf, :] * hf
        yb = u_s[1, rb, :] + a_s[1, rb, :] * hb
        y_s[0, rf, :] = yf
        y_s[1, rb, :] = yb
        return yf[7:8, :], yb[0:1, :]

    hf, hb = lax.fori_loop(0, nt, body, (hf0, hb0))
    zr_ref[...] = ((y_s[0, 0:seq, :] + y_s[1, 0:seq, :]) * gr_ref[...].astype(F32)).astype(BF16)
    if fin_ref is not None:
        fin_ref[0, 0, 0:1, :] = hf
        fin_ref[0, 0, 1:2, :] = hb


def _lru(seq, nseq, row0, bx, proj, cw_l, cb_l, wg_l, bg_l, lam_l, state=None, l=0, z_r=None, states=None):
    has_state = state is not None
    rblk = lambda b: (row0 // seq + b, 0)
    in_specs = [pl.BlockSpec((seq, BR_W), rblk),
                pl.BlockSpec((CONV_W, BR_W), lambda b: (0, 0)),
                pl.BlockSpec((1, BR_W), lambda b: (0, 0)),
                pl.BlockSpec((BR_W, 4 * BR_W), lambda b: (0, 0)),
                pl.BlockSpec((1, 4 * BR_W), lambda b: (0, 0)),
                pl.BlockSpec((2, BR_W), lambda b: (0, 0)),
                pl.BlockSpec((seq, 256), lambda b: (row0 // seq + b, 1))]
    args = [bx, cw_l, cb_l, wg_l, bg_l, lam_l, proj]
    zspec = pl.BlockSpec((seq, BR_W), rblk)
    zshape = jax.ShapeDtypeStruct((N_TOK, BR_W), BF16)
    scratch = [pltpu.VMEM((2, seq, BR_W), F32)] * 2 + [pltpu.VMEM((2, seq + 16, BR_W), F32)]
    if has_state:
        in_specs += [pl.BlockSpec((1, 1, 2, BR_W), lambda b: (b, l, 0, 0)), pl.BlockSpec(memory_space=pl.ANY)]
        args += [state, z_r]
        return pl.pallas_call(
            functools.partial(_lru_kernel, seq, True),
            grid=(nseq,), in_specs=in_specs, out_specs=zspec, out_shape=zshape,
            scratch_shapes=scratch, input_output_aliases={8: 0},
            compiler_params=_cparams(("arbitrary",)), name=f"lru_lat_l{l}",
        )(*args)
    aliases = {}
    if states is not None:
        in_specs += [pl.BlockSpec(memory_space=pl.ANY)]
        args += [states]
        aliases = {7: 1}
    return pl.pallas_call(
        functools.partial(_lru_kernel, seq, False),
        grid=(nseq,), in_specs=in_specs,
        out_specs=[zspec, pl.BlockSpec((1, 1, 2, BR_W), lambda b: (b, l, 0, 0))],
        out_shape=[zshape, jax.ShapeDtypeStruct((nseq, DEPTH, 2, BR_W), F32)],
        scratch_shapes=scratch, input_output_aliases=aliases,
        compiler_params=_cparams(("arbitrary",)), name=f"lru_ctx_l{l}",
    )(*args)


def _merge_kernel(final, xa_ref, xb_ref, mod_ref, h_ref, za_ref, zr_ref, zc_ref, zd_ref,
                  wmg_ref, bmg_ref, wbo_ref, wo_ref, nf_ref, *refs):
    out_refs, (wmg_s, wbo_s, wo_s) = refs[:-3], refs[-3:]
    i = pl.program_id(0)

    @pl.when(i < W_STEPS)
    def _():
        r0 = pl.multiple_of(i * W_ROWS, W_ROWS)
        for n in range(N_BRANCH):
            wmg_s[pl.ds(r0, W_ROWS), n * D_MODEL:(n + 1) * D_MODEL] = wmg_ref[0, :, n, :].astype(BF16)
        _cast_rows(i, wbo_ref, wbo_s)
        _cast_rows(i, wo_ref, wo_s)

    @pl.when(i >= W_STEPS)
    def _():
        t = i - W_STEPS
        x = _load_x(t, xa_ref, xb_ref)
        gate = mod_ref[0][:, 2 * D_MODEL:]
        h = h_ref[...]
        zs = [za_ref[...], zr_ref[...], zc_ref[...], zd_ref[...]]
        bmg = bmg_ref[0]
        cols = []
        for c in range(0, D_MODEL, 512):
            acc = None
            for n in range(N_BRANCH):
                g = jax.nn.sigmoid(_dot(h, wmg_s[:, n * D_MODEL + c:n * D_MODEL + c + 512])
                                   + bmg[n:n + 1, c:c + 512])
                term = g * _dot(zs[n], wbo_s[n * BR_W:(n + 1) * BR_W, c:c + 512])
                acc = term if acc is None else acc + term
            cols.append(acc.astype(BF16))
        merged = jnp.concatenate(cols, axis=1)
        xn = x + gate * _dot(merged, wo_s[...])
        if not final:
            out_refs[0][...] = xn
        else:
            y = xn * lax.rsqrt(jnp.mean(xn * xn, axis=-1, keepdims=True) + EPS) * nf_ref[...]

            @pl.when(t < N_CTX_TILES)
            def _():
                out_refs[0][...] = y

            @pl.when(t >= N_CTX_TILES)
            def _():
                out_refs[1][...] = y


def _merge(final, xa, xb, xb_off, mod3, l, h, zs, w_mg, b_mg, w_bo, w_o, norm_f):
    row = lambda i: (_tile(i), 0)
    ctx_blk = lambda i: jnp.minimum(_tile(i), N_CTX_TILES - 1)
    lat_blk = lambda i: jnp.maximum(_tile(i) - N_CTX_TILES, 0)
    if final:
        out_specs = [pl.BlockSpec((TM, D_MODEL), lambda i: (ctx_blk(i), 0)),
                     pl.BlockSpec((TM, D_MODEL), lambda i: (lat_blk(i), 0))]
        out_shape = [jax.ShapeDtypeStruct((N_CTX, D_MODEL), F32), jax.ShapeDtypeStruct((N_LAT, D_MODEL), F32)]
    else:
        out_specs = pl.BlockSpec((TM, D_MODEL), row)
        out_shape = jax.ShapeDtypeStruct((N_TOK, D_MODEL), F32)
    wchunk = lambda width: pl.BlockSpec((1, W_ROWS, width), lambda i: (l, _wchunk(i), 0))
    return pl.pallas_call(
        functools.partial(_merge_kernel, final),
        grid=(W_STEPS + N_TILES,),
        in_specs=[
            pl.BlockSpec((TM, D_MODEL), lambda i: (ctx_blk(i), 0)),
            pl.BlockSpec((TM, D_MODEL), lambda i: (lat_blk(i) + xb_off, 0)),
            pl.BlockSpec((1, 1, 3 * D_MODEL), lambda i: (l * 16 + _mod_row(_tile(i)), 0, 0)),
            pl.BlockSpec((TM, D_MODEL), row),
            pl.BlockSpec((TM, BR_W), row), pl.BlockSpec((TM, BR_W), row),
            pl.BlockSpec((TM, BR_W), row), pl.BlockSpec((TM, BR_W), row),
            pl.BlockSpec((1, W_ROWS, N_BRANCH, D_MODEL), lambda i: (l, _wchunk(i), 0, 0)),
            pl.BlockSpec((1, N_BRANCH, D_MODEL), lambda i: (l, 0, 0)),
            wchunk(D_MODEL),
            wchunk(D_MODEL),
            pl.BlockSpec((1, D_MODEL), lambda i: (0, 0)),
        ],
        out_specs=out_specs,
        out_shape=out_shape,
        scratch_shapes=[pltpu.VMEM((D_MODEL, N_BRANCH * D_MODEL), BF16),
                        pltpu.VMEM((N_BRANCH * BR_W, D_MODEL), BF16),
                        pltpu.VMEM((D_MODEL, D_MODEL), BF16)],
        compiler_params=_cparams(("arbitrary",)),
        name=f"merge_l{l}",
    )(xa, xb, mod3, h, *zs,
      w_mg, b_mg,
      w_bo.reshape(DEPTH, N_BRANCH * BR_W, D_MODEL), w_o, norm_f.reshape(1, D_MODEL))


def _lru_gate_weights(lru_wa, lru_ba, lru_wx, lru_bx):
    w = jnp.stack([lru_wa[:, 0], lru_wx[:, 0], lru_wa[:, 1], lru_wx[:, 1]], axis=1)
    eye = jnp.eye(B_BLOCKS, dtype=w.dtype)
    dense = w[:, :, :, :, None, :] * eye[None, None, :, None, :, None]
    wg = dense.transpose(0, 2, 3, 1, 4, 5).reshape(DEPTH, BR_W, 4 * BR_W).astype(BF16)
    bg = jnp.stack([lru_ba[:, 0], lru_bx[:, 0], lru_ba[:, 1], lru_bx[:, 1]], axis=1).reshape(DEPTH, 1, 4 * BR_W)
    return wg, bg


def kernel(x_prompt, x_sample, cache_diff_k, cache_diff_v, cache_na_k, cache_na_v, cache_swa_k, cache_swa_v,
           state_lru, c, c_ctx, norm_g, w_ada, b_ada, w_in, diff_lambda, diff_norm_g, conv_w, conv_b,
           lru_wa, lru_ba, lru_wx, lru_bx, lru_lam, na_rpb, swa_sink, w_mg, b_mg, w_bo, w_o, norm_f):
    tab_a_np, _ = _rope_tables(A_QK)
    tab_d_np, _ = _rope_tables(HD)
    tab_a, tab_d = jnp.asarray(tab_a_np), jnp.asarray(tab_d_np)

    cvecs = jnp.concatenate([c_ctx[None, :], c, jnp.zeros((16 - 1 - DEC_BATCH, D_MODEL), F32)], axis=0)
    mod3 = _modulation(cvecs, w_ada, b_ada).reshape(DEPTH * 16, 1, 3 * D_MODEL)
    bias = _na_bias(na_rpb)
    wg, bg = _lru_gate_weights(lru_wa, lru_ba, lru_wx, lru_bx)
    norm_g3 = norm_g.reshape(DEPTH, 1, D_MODEL)
    past = [jnp.swapaxes(t, -1, -2) for t in
            (cache_diff_k, cache_diff_v, cache_na_k, cache_na_v, cache_swa_k, cache_swa_v)]

    xa = x_prompt.reshape(N_CTX, D_MODEL)
    xb = x_sample.reshape(N_LAT, D_MODEL)
    xb_off = 0
    caches = states = None
    y_p = y_s = None
    for l in range(DEPTH):
        h, proj, bx, caches = _inproj(xa, xb, xb_off, mod3, l, norm_g3, w_in, tab_a, tab_d, caches)

        lam_l, dg_l, sink_l = diff_lambda[l], diff_norm_g[l][None, :], swa_sink[l][None, :]
        z_a, z_c, z_d = _ctx_attn(proj, l, lam_l, dg_l, sink_l)
        z_a = _lat_diff(proj, past[0], past[1], l, lam_l, dg_l, z_a)
        z_c = _lat_na(proj, past[2], past[3], bias, l, z_c)
        z_d = _lat_swa(proj, past[4], past[5], sink_l, l, z_d)

        lru_args = (bx, proj, conv_w[l], conv_b[l][None, :], wg[l], bg[l], lru_lam[l])
        z_r, states = _lru(SEQ, BATCH, 0, *lru_args, l=l, states=states)
        z_r = _lru(DEC_SEQ, DEC_BATCH, N_CTX, *lru_args, state=state_lru, l=l, z_r=z_r)

        final = l == DEPTH - 1
        out = _merge(final, xa, xb, xb_off, mod3, l, h, (z_a, z_r, z_c, z_d), w_mg, b_mg, w_bo, w_o, norm_f)
        if final:
            y_p, y_s = out
        else:
            xa = xb = out
            xb_off = N_CTX_TILES

    new_caches = [jnp.swapaxes(t, -1, -2) for t in caches]
    return (y_p.reshape(BATCH, SEQ, D_MODEL), y_s.reshape(DEC_BATCH, DEC_SEQ, D_MODEL), *new_caches, states)
```

```python
import functools
import math

import numpy as np
import jax
import jax.numpy as jnp
from jax import lax
from jax.experimental import pallas as pl
from jax.experimental.pallas import tpu as pltpu

F32 = jnp.float32
BF16 = jnp.bfloat16

D_MODEL = 1024
BATCH = 16
SEQ = 256
DEPTH = 2
DEC_BATCH = 8
DEC_SEQ = 1024
PAST_LEN = 512
GRID_W = 64
N_BRANCH = 4
BR_W = D_MODEL // 4
HD = 64
A_HEADS = BR_W // HD
A_QK = HD // 2
B_BLOCKS = 4
B_BLK = BR_W // B_BLOCKS
CONV_W = 4
LRU_C = 8.0
C_HEADS = BR_W // HD
NA_ROWS = 8
NA_COLS = 16
D_HEADS = BR_W // HD
D_KV = 2
D_GROUP = D_HEADS // D_KV
WIN = 128
ROPE_BASE = 10000.0
EPS = 1e-6
NEG = -1e30

N_CTX = BATCH * SEQ
N_LAT = DEC_BATCH * DEC_SEQ
N_TOK = N_CTX + N_LAT
GRID_ROWS = DEC_SEQ // GRID_W

TM = 512
N_CTX_TILES = N_CTX // TM
N_TILES = N_TOK // TM
LAT_TILES_PER_SEQ = DEC_SEQ // TM

PROJ_W = 3328
COL_G = 0
COL_AQ = 1024
COL_AK = 1280
COL_AV = 1536
COL_CQ = 1792
COL_CK = 2048
COL_CV = 2304
COL_DQ = 2560
COL_DK = 2816
COL_DV = 3072
IN_W = 3328
WS_W = IN_W + 2 * D_KV * HD

NA_HALF_Q = DEC_SEQ // 2
NA_KROWS = 12
NA_KSPAN = NA_KROWS * GRID_W
NA_HALF_KSTART = 4 * GRID_W

SWA_QB = 128
SWA_SPAN = SWA_QB + 2 * WIN

LOG2E = math.log2(math.e)
QA_SCALE = A_QK ** -0.5 * LOG2E
QH_SCALE = HD ** -0.5 * LOG2E

VMEM_LIMIT = 56 * 1024 * 1024


def _cparams(sem):
    return pltpu.CompilerParams(dimension_semantics=sem, vmem_limit_bytes=VMEM_LIMIT)


def _resident(shape, index_map):
    return pl.BlockSpec(shape, index_map, pipeline_mode=pl.Buffered(1))


def _dot(a, b):
    return jnp.dot(a, b, preferred_element_type=F32)


def _dot_nt(a, b):
    return lax.dot_general(a, b, (((1,), (1,)), ((), ())), preferred_element_type=F32)


def _silu(x):
    return x * jax.nn.sigmoid(x)


def _rope_tables(d):
    half = d // 2
    quarter = half // 2
    lane = np.arange(128)
    q = lane % d
    use_col = (q >= half)
    i = (q % half) % quarter
    first = (q % half) < quarter
    inv = (ROPE_BASE ** (-np.arange(quarter, dtype=np.float32) / np.float32(quarter))).astype(np.float32)
    t = np.arange(DEC_SEQ)
    pos = np.where(use_col[None, :], (t % GRID_W)[:, None], (t // GRID_W)[:, None]).astype(np.float32)
    ang = (pos * inv[i][None, :]).astype(np.float32)
    cos, sin = np.cos(ang), np.sin(ang)
    c = np.concatenate([np.ones((TM, 128), np.float32), cos], axis=0)
    s1 = np.concatenate([np.zeros((TM, 128), np.float32), np.where(first[None, :], -sin, 0.0)], axis=0)
    s2 = np.concatenate([np.zeros((TM, 128), np.float32), np.where(first[None, :], 0.0, sin)], axis=0)
    return np.stack([c, s1, s2]).astype(np.float32), quarter


def _rope128(x, tab_ref, shift):
    return (x * tab_ref[0] + pltpu.roll(x, 128 - shift, 1) * tab_ref[1]
            + pltpu.roll(x, shift, 1) * tab_ref[2])


def _rope(x, tab_ref, shift):
    w = x.shape[1]
    return jnp.concatenate([_rope128(x[:, c:c + 128], tab_ref, shift) for c in range(0, w, 128)], axis=1)


def _mod_kernel(c_ref, w_ref, b_ref, o_ref):
    c = c_ref[...]
    o_ref[0] = _dot(_silu(c).astype(BF16), w_ref[0].astype(BF16)) + b_ref[0]


def _modulation(cvecs, w_ada, b_ada):
    nb = 3 * D_MODEL // 768
    return pl.pallas_call(
        _mod_kernel,
        grid=(DEPTH, nb),
        in_specs=[pl.BlockSpec((16, D_MODEL), lambda l, j: (0, 0)),
                  pl.BlockSpec((1, D_MODEL, 768), lambda l, j: (l, 0, j)),
                  pl.BlockSpec((1, 1, 768), lambda l, j: (l, 0, j))],
        out_specs=pl.BlockSpec((1, 16, 768), lambda l, j: (l, 0, j)),
        out_shape=jax.ShapeDtypeStruct((DEPTH, 16, 3 * D_MODEL), F32),
        compiler_params=_cparams(("arbitrary", "arbitrary")),
        name="modulation",
    )(cvecs, w_ada, b_ada.reshape(DEPTH, 1, 3 * D_MODEL))


W_STEPS = 8
W_ROWS = D_MODEL // W_STEPS


def _tile(i):
    return jnp.maximum(i - W_STEPS, 0)


def _wchunk(i):
    return jnp.minimum(i, W_STEPS - 1)


def _mod_row(t):
    return jnp.where(t < N_CTX_TILES, 0, 1 + (t - N_CTX_TILES) // LAT_TILES_PER_SEQ)


def _rope_blk(t):
    return jnp.where(t < N_CTX_TILES, 0, 1 + (t - N_CTX_TILES) % LAT_TILES_PER_SEQ)


def _load_x(t, xa_ref, xb_ref):
    return jnp.where(t < N_CTX_TILES, xa_ref[...], xb_ref[...])


def _cast_rows(i, src_ref, dst_ref):
    r0 = pl.multiple_of(i * W_ROWS, W_ROWS)
    dst_ref[pl.ds(r0, W_ROWS), :] = src_ref[0].astype(BF16)


WI_G, WI_AQ, WI_AK, WI_AV, WI_BX, WI_CQ, WI_CKV, WI_DQ, WI_DKV = 0, 1024, 1280, 1536, 1792, 2048, 2304, 2816, 3072
SEQ_PER_TILE = TM // SEQ


def _store_heads(ref, val, n_heads, width):
    vt = val.T
    for s in range(SEQ_PER_TILE):
        for h in range(n_heads):
            ref[s, 0, h] = vt[h * width:(h + 1) * width, s * SEQ:(s + 1) * SEQ]


def _inproj_kernel(n_alias, xa_ref, xb_ref, mod_ref, g_ref, w_ref, ta_ref, td_ref, *refs):
    h_ref, proj_ref, bx_ref, dk_ref, dv_ref, nk_ref, nv_ref, sk_ref, sv_ref, w_s = refs[n_alias:]
    i = pl.program_id(0)

    @pl.when(i < W_STEPS)
    def _():
        r0 = pl.multiple_of(i * W_ROWS, W_ROWS)
        wb = w_ref[0].astype(BF16)
        w_s[pl.ds(r0, W_ROWS), 0:WI_DKV] = wb[:, 0:WI_DKV]
        for n, src in enumerate((0, 0, 1, 1, 2, 2, 3, 3)):
            w_s[pl.ds(r0, W_ROWS), WI_DKV + n * HD:WI_DKV + (n + 1) * HD] = \
                wb[:, WI_DKV + src * HD:WI_DKV + (src + 1) * HD]

    @pl.when(i >= W_STEPS)
    def _():
        t = i - W_STEPS
        is_ctx = t < N_CTX_TILES
        x = _load_x(t, xa_ref, xb_ref)
        mod = mod_ref[0]
        shift, scale = mod[:, :D_MODEL], mod[:, D_MODEL:2 * D_MODEL]
        xn = x * lax.rsqrt(jnp.mean(x * x, axis=-1, keepdims=True) + EPS) * g_ref[0]
        hb =(xn * (1.0 + scale) + shift).astype(BF16)
        h_ref[...] = hb

        def mm(c0, width):
            return _dot(hb, w_s[:, c0:c0 + width])

        gs = [mm(WI_G + c, 512) for c in range(0, 1024, 512)]
        aq, ak, av = mm(WI_AQ, 256), mm(WI_AK, 256), mm(WI_AV, 256)
        cq, ckv = mm(WI_CQ, 256), mm(WI_CKV, 512)
        dq, dk4, dv4 = mm(WI_DQ, 256), mm(WI_DKV, 256), mm(WI_DKV + 256, 256)
        bx_ref[...] = mm(WI_BX, 256)

        for n, g in enumerate(gs):
            proj_ref[:, COL_G + n * 512:COL_G + (n + 1) * 512] = _silu(g).astype(BF16)
        proj_ref[:, COL_AQ:COL_AQ + 256] = _rope(aq * QA_SCALE, ta_ref, A_QK // 4).astype(BF16)
        proj_ref[:, COL_AK:COL_AK + 256] = _rope(ak, ta_ref, A_QK // 4).astype(BF16)
        proj_ref[:, COL_AV:COL_AV + 256] = av.astype(BF16)
        proj_ref[:, COL_CQ:COL_CQ + 256] = (cq * QH_SCALE).astype(BF16)
        proj_ref[:, COL_CK:COL_CK + 512] = ckv.astype(BF16)
        proj_ref[:, COL_DQ:COL_DQ + 256] = _rope(dq * QH_SCALE, td_ref, HD // 4).astype(BF16)
        proj_ref[:, COL_DK:COL_DK + 256] = _rope(dk4, td_ref, HD // 4).astype(BF16)
        proj_ref[:, COL_DV:COL_DV + 256] = dv4.astype(BF16)

        @pl.when(is_ctx)
        def _():
            akt = ak.T
            for s in range(SEQ_PER_TILE):
                for h in range(A_HEADS):
                    for m in range(2):
                        c0 = h * HD + m * A_QK
                        dk_ref[s, 0, h, m] = akt[c0:c0 + A_QK, s * SEQ:(s + 1) * SEQ]
            _store_heads(dv_ref, av, A_HEADS, HD)
            _store_heads(nk_ref, ckv[:, :256], C_HEADS, HD)
            _store_heads(nv_ref, ckv[:, 256:], C_HEADS, HD)
            for ref, val in ((sk_ref, dk4), (sv_ref, dv4)):
                vt = val.T
                for s in range(SEQ_PER_TILE):
                    for kv in range(D_KV):
                        ref[s, 0, kv] = vt[kv * D_GROUP * HD:kv * D_GROUP * HD + HD, s * SEQ:(s + 1) * SEQ]


def _cache_shapes():
    hs = lambda n: (BATCH, DEPTH, n, HD, SEQ)
    return [(BATCH, DEPTH, A_HEADS, 2, A_QK, SEQ), hs(A_HEADS), hs(C_HEADS), hs(C_HEADS), hs(D_KV), hs(D_KV)]


def _inproj(xa, xb, xb_off, mod3, l, norm_g, w_in, tab_a, tab_d, caches):
    row = lambda i: (_tile(i), 0)
    ctx_blk = lambda i: jnp.minimum(_tile(i), N_CTX_TILES - 1)

    def cache_spec(shape):
        blk = (SEQ_PER_TILE, 1) + shape[2:]
        nz = len(shape) - 2
        return pl.BlockSpec(blk, lambda i: (ctx_blk(i), l) + (0,) * nz)

    cshapes = _cache_shapes()
    aliases = {} if caches is None else {7 + k: 3 + k for k in range(6)}
    extra_specs = [] if caches is None else [pl.BlockSpec(memory_space=pl.ANY)] * 6
    extra_args = [] if caches is None else list(caches)
    outs = pl.pallas_call(
        functools.partial(_inproj_kernel, len(extra_args)),
        grid=(W_STEPS + N_TILES,),
        in_specs=[
            pl.BlockSpec((TM, D_MODEL), lambda i: (ctx_blk(i), 0)),
            pl.BlockSpec((TM, D_MODEL), lambda i: (jnp.maximum(_tile(i) - N_CTX_TILES, 0) + xb_off, 0)),
            pl.BlockSpec((1, 1, 3 * D_MODEL), lambda i: (l * 16 + _mod_row(_tile(i)), 0, 0)),
            pl.BlockSpec((1, 1, D_MODEL), lambda i: (l, 0, 0)),
            pl.BlockSpec((1, W_ROWS, IN_W), lambda i: (l, _wchunk(i), 0)),
            pl.BlockSpec((3, TM, 128), lambda i: (0, _rope_blk(_tile(i)), 0)),
            pl.BlockSpec((3, TM, 128), lambda i: (0, _rope_blk(_tile(i)), 0)),
        ] + extra_specs,
        out_specs=[
            pl.BlockSpec((TM, D_MODEL), row),
            pl.BlockSpec((TM, PROJ_W), row),
            pl.BlockSpec((TM, BR_W), row),
        ] + [cache_spec(s) for s in cshapes],
        out_shape=[
            jax.ShapeDtypeStruct((N_TOK, D_MODEL), BF16),
            jax.ShapeDtypeStruct((N_TOK, PROJ_W), BF16),
            jax.ShapeDtypeStruct((N_TOK, BR_W), F32),
        ] + [jax.ShapeDtypeStruct(s, F32) for s in cshapes],
        scratch_shapes=[pltpu.VMEM((D_MODEL, WS_W), BF16)],
        input_output_aliases=aliases,
        compiler_params=_cparams(("arbitrary",)),
        name=f"inproj_l{l}",
    )(xa, xb, mod3, norm_g, w_in, tab_a, tab_d, *extra_args)
    return outs[0], outs[1], outs[2], tuple(outs[3:])


def _diff_lambda(lam_ref, lam_init):
    lv = lam_ref[...]
    s1 = jnp.sum(lv[0:1] * lv[1:2], axis=-1, keepdims=True)
    s2 = jnp.sum(lv[2:3] * lv[3:4], axis=-1, keepdims=True)
    return jnp.exp(s1) - jnp.exp(s2) + lam_init


def _diff_norm(o, dg_ref, lam_init):
    y = o * lax.rsqrt(jnp.mean(o * o, axis=-1, keepdims=True) + EPS) * dg_ref[...]
    return y * (1.0 - lam_init)


def _exp2_parts(parts, extra=None):
    m = functools.reduce(jnp.maximum, [jnp.max(s, axis=-1, keepdims=True) for s in parts])
    if extra is not None:
        m = jnp.maximum(m, extra)
    return [jnp.exp2(s - m).astype(BF16) for s in parts], m


def _pipelined(n, score_fn, pv_fn):
    outs = []
    nxt = score_fn(0)
    for c in range(n):
        cur, nxt = nxt, (score_fn(c + 1) if c + 1 < n else None)
        outs.append(pv_fn(c, cur))
    return outs


def _with_ones(v):
    return jnp.concatenate([v, jnp.ones_like(v)], axis=1)


def _with_ones_t(vt):
    return jnp.concatenate([vt, jnp.ones((16, vt.shape[1]), vt.dtype)], axis=0)


def _stack_group(dq, kv, rows):
    return jnp.concatenate([dq[:, (kv * D_GROUP + g) * HD:(kv * D_GROUP + g + 1) * HD]
                            for g in range(D_GROUP)], axis=0)


def _sink_col(sink2, kv, rows):
    r = lax.broadcasted_iota(jnp.int32, (D_GROUP * rows, 1), 0)
    return jnp.where(r < rows, sink2[:, kv * D_GROUP:kv * D_GROUP + 1], sink2[:, kv * D_GROUP + 1:kv * D_GROUP + 2])


CTX_PER_STEP = 2


def _lane_masks(n):
    lane = lax.broadcasted_iota(jnp.int32, (1, BR_W), 1)
    w = BR_W // n
    ms = [jnp.where((lane >= p * w) & (lane < (p + 1) * w), 1.0, 0.0).astype(BF16) for p in range(n)]
    return ms, [1.0 - m for m in ms]


def _swap_halves(p):
    return jnp.concatenate([p[:, BR_W // 2:], p[:, :BR_W // 2]], axis=1)


def _pick_heads(os):
    grp = lax.broadcasted_iota(jnp.int32, os[0].shape, 1) // HD
    acc = os[0]
    for h in range(1, len(os)):
        acc = jnp.where(grp == h, os[h], acc)
    return acc


def _head_rms(o, dg4_ref, lam_init):
    r = lax.broadcasted_iota(jnp.int32, (BR_W, BR_W), 0) // HD
    c = lax.broadcasted_iota(jnp.int32, (BR_W, BR_W), 1) // HD
    ss = _dot((o * o).astype(BF16), jnp.where(r == c, 1.0, 0.0).astype(BF16))
    return o * lax.rsqrt(ss * (1.0 / HD) + EPS) * dg4_ref[...] * (1.0 - lam_init)


def _ctx_attn_kernel(lam_init, aq_ref, ak_ref, av_ref, cq_ref, ck_ref, cv_ref, dq_ref, dk_ref, dv_ref,
                     ga_ref, gc_ref, gd_ref, lam_ref, dg4_ref, sink_ref, za_ref, zc_ref, zd_ref):
    lam = _diff_lambda(lam_ref, lam_init)
    sink2 = sink_ref[...] * LOG2E
    m_qk, _ = _lane_masks(2 * A_HEADS)
    m_hd, o_hd = _lane_masks(C_HEADS)
    seqs = [slice(s * SEQ, (s + 1) * SEQ) for s in range(CTX_PER_STEP)]

    sa, sc, sd = [], [], []
    for rs in seqs:
        aq, cq, dq = aq_ref[rs, :], cq_ref[rs, :], dq_ref[rs, :]
        ak, ck, dk = ak_ref[rs, :], ck_ref[rs, :], dk_ref[rs, :]
        sa.append([_dot_nt(aq * m, ak) for m in m_qk])
        sc.append([_dot_nt(cq * m, ck) for m in m_hd])
        sd.append([_dot_nt(dq * m, dk) for m in m_hd])

    ea = [[_exp2_parts([s])[0][0] for s in ss] for ss in sa]
    ec = [[_exp2_parts([s])[0][0] for s in ss] for ss in sc]
    ed = [[_exp2_parts([s], extra=sink2[:, j:j + 1]) for j, s in enumerate(ss)] for ss in sd]

    for i, rs in enumerate(seqs):
        av, cv, dv = av_ref[rs, :], cv_ref[rs, :], dv_ref[rs, :]
        os = []
        for h in range(A_HEADS):
            w = av * m_hd[h] + o_hd[h]
            p0, p1 = _dot(ea[i][2 * h], w), _dot(ea[i][2 * h + 1], w)
            os.append(p0 * (1.0 / _swap_halves(p0)) - p1 * (lam / _swap_halves(p1)))
        za_ref[rs, :] = (_head_rms(_pick_heads(os), dg4_ref, lam_init) * ga_ref[rs, :].astype(F32)).astype(BF16)

        os = []
        for h in range(C_HEADS):
            p = _dot(ec[i][h], cv * m_hd[h] + o_hd[h])
            os.append(p * (1.0 / _swap_halves(p)))
        zc_ref[rs, :] = (_pick_heads(os) * gc_ref[rs, :].astype(F32)).astype(BF16)

        os = []
        for j in range(D_HEADS):
            (e,), m = ed[i][j]
            p = _dot(e, dv * m_hd[j] + o_hd[j])
            os.append(p * (1.0 / (_swap_halves(p) + jnp.exp2(sink2[:, j:j + 1] - m))))
        zd_ref[rs, :] = (_pick_heads(os) * gd_ref[rs, :].astype(F32)).astype(BF16)


def _ctx_attn(proj, l, lam_l, dg_l, sink_l):
    lam_init = 0.8 - 0.6 * math.exp(-0.3 * l)
    colblk = lambda cb: pl.BlockSpec((CTX_PER_STEP * SEQ, 256), lambda b: (b, cb))
    zspec = pl.BlockSpec((CTX_PER_STEP * SEQ, BR_W), lambda b: (b, 0))
    zshape = jax.ShapeDtypeStruct((N_TOK, BR_W), BF16)
    return pl.pallas_call(
        functools.partial(_ctx_attn_kernel, lam_init),
        grid=(BATCH // CTX_PER_STEP,),
        in_specs=[colblk(COL_AQ // 256), colblk(COL_AK // 256), colblk(COL_AV // 256),
                  colblk(COL_CQ // 256), colblk(COL_CK // 256), colblk(COL_CV // 256),
                  colblk(COL_DQ // 256), colblk(COL_DK // 256), colblk(COL_DV // 256),
                  colblk(0), colblk(2), colblk(3),
                  pl.BlockSpec((4, A_QK), lambda b: (0, 0)),
                  pl.BlockSpec((1, BR_W), lambda b: (0, 0)),
                  pl.BlockSpec((1, D_HEADS), lambda b: (0, 0))],
        out_specs=[zspec, zspec, zspec],
        out_shape=[zshape, zshape, zshape],
        compiler_params=_cparams(("arbitrary",)),
        name=f"ctx_attn_l{l}",
    )(*([proj] * 12), lam_l, jnp.tile(dg_l, (1, A_HEADS)), sink_l)


LA_QB = 256


def _lat_diff_kernel(lam_init, aq_ref, ak_ref, av_ref, ck_ref, cv_ref, ga_ref, lam_ref, dg_ref, zin_ref, za_ref):
    del zin_ref
    lam = _diff_lambda(lam_ref, lam_init)
    aq = aq_ref[...]
    va_ctx = [_with_ones_t(cv_ref[0, 0, h].astype(BF16)) for h in range(A_HEADS)]
    va_loc = [_with_ones(av_ref[:, h * HD:(h + 1) * HD]) for h in range(A_HEADS)]

    def scores(c):
        h, m = divmod(c, 2)
        lo = h * HD + m * A_QK
        q = aq[:, lo:lo + A_QK]
        return [_dot(q, ck_ref[0, 0, h, m].astype(BF16)),
                _dot_nt(q, ak_ref[:, lo:lo + A_QK])]

    def pv(c, s):
        (e_ctx, e_loc), _ = _exp2_parts(s)
        p_ctx, p_loc = _dot_nt(e_ctx, va_ctx[c // 2]), _dot(e_loc, va_loc[c // 2])
        return p_ctx[:, :HD] + p_loc[:, :HD], p_ctx[:, HD:HD + 1] + p_loc[:, HD:HD + 1]

    ss = [scores(c) for c in range(2 * A_HEADS)]
    nd = [pv(c, s) for c, s in enumerate(ss)]
    outs = []
    for h in range(A_HEADS):
        (n0, d0), (n1, d1) = nd[2 * h], nd[2 * h + 1]
        outs.append(_diff_norm(n0 * (1.0 / d0) - n1 * (lam / d1), dg_ref, lam_init))
    za_ref[...] = (jnp.concatenate(outs, axis=1) * ga_ref[...].astype(F32)).astype(BF16)


def _lat_diff(proj, cache_k, cache_v, l, lam_l, dg_l, z_a):
    lam_init = 0.8 - 0.6 * math.exp(-0.3 * l)
    nq = DEC_SEQ // LA_QB
    qrow = lambda b, j: N_CTX // LA_QB + b * nq + j
    srow = lambda b, j: N_CTX // DEC_SEQ + b
    return pl.pallas_call(
        functools.partial(_lat_diff_kernel, lam_init),
        grid=(DEC_BATCH, nq),
        in_specs=[pl.BlockSpec((LA_QB, 256), lambda b, j: (qrow(b, j), COL_AQ // 256)),
                  pl.BlockSpec((DEC_SEQ, 256), lambda b, j: (srow(b, j), COL_AK // 256)),
                  pl.BlockSpec((DEC_SEQ, 256), lambda b, j: (srow(b, j), COL_AV // 256)),
                  pl.BlockSpec((1, 1, A_HEADS, 2, A_QK, PAST_LEN), lambda b, j: (b, l, 0, 0, 0, 0)),
                  pl.BlockSpec((1, 1, A_HEADS, HD, PAST_LEN), lambda b, j: (b, l, 0, 0, 0)),
                  pl.BlockSpec((LA_QB, 256), lambda b, j: (qrow(b, j), 0)),
                  pl.BlockSpec((4, A_QK), lambda b, j: (0, 0)),
                  pl.BlockSpec((1, HD), lambda b, j: (0, 0)),
                  pl.BlockSpec(memory_space=pl.ANY)],
        out_specs=pl.BlockSpec((LA_QB, BR_W), lambda b, j: (qrow(b, j), 0)),
        out_shape=jax.ShapeDtypeStruct((N_TOK, BR_W), BF16),
        input_output_aliases={8: 0},
        compiler_params=_cparams(("arbitrary", "arbitrary")),
        name=f"lat_diff_l{l}",
    )(proj, proj, proj, cache_k, cache_v, proj, lam_l, dg_l, z_a)


def _na_bias_kernel(rpb_ref, o_ref):
    l, h = pl.program_id(0), pl.program_id(1)
    base = (l * C_HEADS + h) * ((2 * NA_ROWS - 1) * (2 * NA_COLS - 1))
    qc = lax.broadcasted_iota(jnp.int32, (GRID_W, GRID_W), 0)
    kc = lax.broadcasted_iota(jnp.int32, (GRID_W, GRID_W), 1)
    cs = jnp.clip(qc - NA_COLS // 2, 0, GRID_W - NA_COLS)
    ok = (kc >= cs) & (kc < cs + NA_COLS)
    dcol = kc - qc + (NA_COLS - 1)
    neg = jnp.full((GRID_W, GRID_W), NEG, F32)
    tabs = []
    for d in range(2 * NA_ROWS - 1):
        t = neg
        for dc in range(2 * NA_COLS - 1):
            t = jnp.where(ok & (dcol == dc), rpb_ref[base + d * (2 * NA_COLS - 1) + dc] * LOG2E, t)
        tabs.append(t)
    for half in range(2):
        for rq in range(NA_ROWS):
            r = half * NA_ROWS + rq
            rs = min(max(r - NA_ROWS // 2, 0), GRID_ROWS - NA_ROWS)
            blks = []
            for kr in range(NA_KROWS):
                kabs = half * 4 + kr
                blks.append(tabs[kabs - r + NA_ROWS - 1] if rs <= kabs < rs + NA_ROWS else neg)
            o_ref[0, 0, half, rq * GRID_W:(rq + 1) * GRID_W, :] = jnp.concatenate(blks, axis=1)


def _na_bias(na_rpb):
    return pl.pallas_call(
        _na_bias_kernel,
        grid=(DEPTH, C_HEADS),
        in_specs=[pl.BlockSpec(memory_space=pltpu.SMEM)],
        out_specs=pl.BlockSpec((1, 1, 2, NA_HALF_Q, NA_KSPAN), lambda l, h: (l, h, 0, 0, 0)),
        out_shape=jax.ShapeDtypeStruct((DEPTH, C_HEADS, 2, NA_HALF_Q, NA_KSPAN), F32),
        compiler_params=_cparams(("arbitrary", "arbitrary")),
        name="na_bias",
    )(na_rpb.reshape(-1))


def _lat_na_kernel(cq_ref, ck_ref, cv_ref, kc_ref, vc_ref, bias_ref, gc_ref, zin_ref, zc_ref):
    del zin_ref
    half = pl.program_id(0)
    k0 = pl.multiple_of(half * NA_HALF_KSTART, NA_HALF_KSTART)
    cq = cq_ref[...]
    kl_all = ck_ref[pl.ds(k0, NA_KSPAN), :]
    vl_all = cv_ref[pl.ds(k0, NA_KSPAN), :]
    def scores(h):
        q = cq[:, h * HD:(h + 1) * HD]
        return [_dot(q, kc_ref[0, 0, h].astype(BF16)),
                _dot_nt(q, kl_all[:, h * HD:(h + 1) * HD]) + bias_ref[0, h, 0]]

    def pv(h, s):
        (e_ctx, e_loc), _ = _exp2_parts(s)
        p_ctx = _dot_nt(e_ctx, _with_ones_t(vc_ref[0, 0, h].astype(BF16)))
        p_loc = _dot(e_loc, _with_ones(vl_all[:, h * HD:(h + 1) * HD]))
        return (p_ctx[:, :HD] + p_loc[:, :HD]) * (1.0 / (p_ctx[:, HD:HD + 1] + p_loc[:, HD:HD + 1]))

    ss = [scores(h) for h in range(C_HEADS)]
    outs = [pv(h, s) for h, s in enumerate(ss)]
    zc_ref[...] = (jnp.concatenate(outs, axis=1) * gc_ref[...].astype(F32)).astype(BF16)


def _lat_na(proj, cache_k, cache_v, bias, l, z_c):
    qrow = lambda hf, b: N_CTX // NA_HALF_Q + b * 2 + hf
    srow = lambda hf, b: N_CTX // DEC_SEQ + b
    return pl.pallas_call(
        _lat_na_kernel,
        grid=(2, DEC_BATCH),
        in_specs=[pl.BlockSpec((NA_HALF_Q, 256), lambda hf, b: (qrow(hf, b), COL_CQ // 256)),
                  pl.BlockSpec((DEC_SEQ, 256), lambda hf, b: (srow(hf, b), COL_CK // 256)),
                  pl.BlockSpec((DEC_SEQ, 256), lambda hf, b: (srow(hf, b), COL_CV // 256)),
                  pl.BlockSpec((1, 1, C_HEADS, HD, PAST_LEN), lambda hf, b: (b, l, 0, 0, 0)),
                  pl.BlockSpec((1, 1, C_HEADS, HD, PAST_LEN), lambda hf, b: (b, l, 0, 0, 0)),
                  pl.BlockSpec((1, C_HEADS, 1, NA_HALF_Q, NA_KSPAN), lambda hf, b: (l, 0, hf, 0, 0)),
                  pl.BlockSpec((NA_HALF_Q, 256), lambda hf, b: (qrow(hf, b), 2)),
                  pl.BlockSpec(memory_space=pl.ANY)],
        out_specs=pl.BlockSpec((NA_HALF_Q, BR_W), lambda hf, b: (qrow(hf, b), 0)),
        out_shape=jax.ShapeDtypeStruct((N_TOK, BR_W), BF16),
        input_output_aliases={7: 0},
        compiler_params=_cparams(("arbitrary", "arbitrary")),
        name=f"lat_na_l{l}",
    )(proj, proj, proj, cache_k, cache_v, bias, proj, z_c)


def _lat_swa_kernel(dq_ref, dk_ref, dv_ref, kc_ref, vc_ref, sink_ref, gd_ref, zin_ref, zd_ref):
    del zin_ref
    nq = DEC_SEQ // SWA_QB
    rows = D_GROUP * SWA_QB
    dq, dk, dv = dq_ref[...], dk_ref[...], dv_ref[...]
    sink2 = sink_ref[...] * LOG2E
    m_hd, _ = _lane_masks(D_HEADS)
    starts = [min(max(j * SWA_QB - WIN, 0), DEC_SEQ - SWA_SPAN) for j in range(nq)]
    qi = lax.broadcasted_iota(jnp.int32, (rows, SWA_SPAN), 0) % SWA_QB
    ki = lax.broadcasted_iota(jnp.int32, (rows, SWA_SPAN), 1)
    masks = {off: jnp.abs(qi + off - ki) <= WIN for off in sorted({j * SWA_QB - s for j, s in enumerate(starts)})}
    kc_t = jnp.concatenate([kc_ref[0, 0, kv] for kv in range(D_KV) for _ in range(D_GROUP)], axis=0).astype(BF16)
    vc_t = jnp.concatenate([vc_ref[0, 0, kv] for kv in range(D_KV) for _ in range(D_GROUP)], axis=0).astype(BF16)
    feat = lax.broadcasted_iota(jnp.int32, (BR_W, 1), 0) // (D_GROUP * HD)

    qm = [dq * m for m in m_hd]
    sks = [_sink_col(sink2, kv, SWA_QB) for kv in range(D_KV)]
    w_loc, w_ctx = [], []
    for kv in range(D_KV):
        pair = m_hd[kv * D_GROUP] + m_hd[kv * D_GROUP + 1]
        w_loc.append(dv * pair + (1.0 - pair))
        w_ctx.append(jnp.where(feat == kv, vc_t, jnp.ones_like(vc_t)))

    def scores(c):
        kv, j = divmod(c, nq)
        q2 = jnp.concatenate([qm[kv * D_GROUP + g][j * SWA_QB:(j + 1) * SWA_QB] for g in range(D_GROUP)], axis=0)
        return [_dot(q2, kc_t),
                jnp.where(masks[j * SWA_QB - starts[j]], _dot_nt(q2, dk[starts[j]:starts[j] + SWA_SPAN]), NEG)]

    def pv(c, s):
        kv, j = divmod(c, nq)
        (e_ctx, e_loc), m = _exp2_parts(s, extra=sks[kv])
        p = _dot_nt(e_ctx, w_ctx[kv]) + _dot(e_loc, w_loc[kv][starts[j]:starts[j] + SWA_SPAN])
        return p * (1.0 / (_swap_halves(p) + jnp.exp2(sks[kv] - m)))

    o = _pipelined(D_KV * nq, scores, pv)
    os = [jnp.concatenate([o[kv * nq + j][g * SWA_QB:(g + 1) * SWA_QB] for j in range(nq)], axis=0)
          for kv in range(D_KV) for g in range(D_GROUP)]
    zd_ref[...] = (_pick_heads(os) * gd_ref[...].astype(F32)).astype(BF16)


def _lat_swa(proj, cache_k, cache_v, sink_l, l, z_d):
    srow = lambda b: (N_CTX // DEC_SEQ + b)
    return pl.pallas_call(
        _lat_swa_kernel,
        grid=(DEC_BATCH,),
        in_specs=[pl.BlockSpec((DEC_SEQ, 256), lambda b: (srow(b), COL_DQ // 256)),
                  pl.BlockSpec((DEC_SEQ, 256), lambda b: (srow(b), COL_DK // 256)),
                  pl.BlockSpec((DEC_SEQ, 256), lambda b: (srow(b), COL_DV // 256)),
                  pl.BlockSpec((1, 1, D_KV, HD, PAST_LEN), lambda b: (b, l, 0, 0, 0)),
                  pl.BlockSpec((1, 1, D_KV, HD, PAST_LEN), lambda b: (b, l, 0, 0, 0)),
                  pl.BlockSpec((1, D_HEADS), lambda b: (0, 0)),
                  pl.BlockSpec((DEC_SEQ, 256), lambda b: (srow(b), 3)),
                  pl.BlockSpec(memory_space=pl.ANY)],
        out_specs=pl.BlockSpec((DEC_SEQ, BR_W), lambda b: (srow(b), 0)),
        out_shape=jax.ShapeDtypeStruct((N_TOK, BR_W), BF16),
        input_output_aliases={7: 0},
        compiler_params=_cparams(("arbitrary",)),
        name=f"lat_swa_l{l}",
    )(proj, proj, proj, cache_k, cache_v, sink_l, proj, z_d)


def _lru_kernel(seq, has_state, *refs):
    if has_state:
        bx_ref, cw_ref, cb_ref, wg_ref, bg_ref, lam_ref, gr_ref, st_ref, zin_ref, zr_ref, a_s, u_s, y_s = refs
        del zin_ref
        fin_ref = None
    else:
        bx_ref, cw_ref, cb_ref, wg_ref, bg_ref, lam_ref, gr_ref = refs[:7]
        zr_ref, fin_ref, a_s, u_s, y_s = refs[-5:]
    x = bx_ref[...]
    cw = cw_ref[...]
    xp_s = y_s.at[0]
    xp_s[0:8, :] = jnp.zeros((8, BR_W), F32)
    xp_s[8:8 + seq, :] = x
    xp_s[8 + seq:16 + seq, :] = jnp.zeros((8, BR_W), F32)
    xm1, xp1, xp2 = xp_s[7:7 + seq, :], xp_s[9:9 + seq, :], xp_s[10:10 + seq, :]
    xc = cb_ref[...] + xm1 * cw[0:1] + x * cw[1:2] + xp1 * cw[2:3] + xp2 * cw[3:4]

    gates = _dot(xc.astype(BF16), wg_ref[...]) + bg_ref[...]
    lam = lam_ref[...]
    nl = -lam
    softplus = jnp.maximum(nl, 0.0) + jnp.log1p(jnp.exp(-jnp.abs(nl)))
    tin = lax.broadcasted_iota(jnp.int32, (1, 8, 1), 1)
    for d in range(2):
        r = jax.nn.sigmoid(gates[:, (2 * d) * BR_W:(2 * d + 1) * BR_W])
        ig = jax.nn.sigmoid(gates[:, (2 * d + 1) * BR_W:(2 * d + 2) * BR_W])
        log_a = -LRU_C * r * softplus[d:d + 1]
        a = jnp.exp(log_a)
        u = jnp.sqrt(-jnp.tanh(log_a) * (a * a + 1.0)) * (ig * xc)
        a = a.reshape(seq // 8, 8, BR_W)
        u = u.reshape(seq // 8, 8, BR_W)
        for s in (1, 2, 4):
            if d == 0:
                keep = tin >= s
                a_n = jnp.where(keep, pltpu.roll(a, s, 1), 1.0)
                u_n = jnp.where(keep, pltpu.roll(u, s, 1), 0.0)
            else:
                keep = tin < 8 - s
                a_n = jnp.where(keep, pltpu.roll(a, 8 - s, 1), 1.0)
                u_n = jnp.where(keep, pltpu.roll(u, 8 - s, 1), 0.0)
            u = u + a * u_n
            a = a * a_n
        a_s[d] = a.reshape(seq, BR_W)
        u_s[d] = u.reshape(seq, BR_W)

    if has_state:
        hf0 = st_ref[0, 0, 0:1, :]
        hb0 = st_ref[0, 0, 1:2, :]
    else:
        hf0 = jnp.zeros((1, BR_W), F32)
        hb0 = jnp.zeros((1, BR_W), F32)

    nt = seq // 8

    def body(k, carry):
        hf, hb = carry
        rf = pl.ds(pl.multiple_of(k * 8, 8), 8)
        rb = pl.ds(pl.multiple_of((nt - 1 - k) * 8, 8), 8)
        yf = u_s[0, rf, :] + a_s[0, rf, :] * hf
        yb = u_s[1, rb, :] + a_s[1, rb, :] * hb
        y_s[0, rf, :] = yf
        y_s[1, rb, :] = yb
        return yf[7:8, :], yb[0:1, :]

    hf, hb = lax.fori_loop(0, nt, body, (hf0, hb0))
    zr_ref[...] = ((y_s[0, 0:seq, :] + y_s[1, 0:seq, :]) * gr_ref[...].astype(F32)).astype(BF16)
    if fin_ref is not None:
        fin_ref[0, 0, 0:1, :] = hf
        fin_ref[0, 0, 1:2, :] = hb


def _lru(seq, nseq, row0, bx, proj, cw_l, cb_l, wg_l, bg_l, lam_l, state=None, l=0, z_r=None, states=None):
    has_state = state is not None
    rblk = lambda b: (row0 // seq + b, 0)
    in_specs = [pl.BlockSpec((seq, BR_W), rblk),
                pl.BlockSpec((CONV_W, BR_W), lambda b: (0, 0)),
                pl.BlockSpec((1, BR_W), lambda b: (0, 0)),
                pl.BlockSpec((BR_W, 4 * BR_W), lambda b: (0, 0)),
                pl.BlockSpec((1, 4 * BR_W), lambda b: (0, 0)),
                pl.BlockSpec((2, BR_W), lambda b: (0, 0)),
                pl.BlockSpec((seq, 256), lambda b: (row0 // seq + b, 1))]
    args = [bx, cw_l, cb_l, wg_l, bg_l, lam_l, proj]
    zspec = pl.BlockSpec((seq, BR_W), rblk)
    zshape = jax.ShapeDtypeStruct((N_TOK, BR_W), BF16)
    scratch = [pltpu.VMEM((2, seq, BR_W), F32)] * 2 + [pltpu.VMEM((2, seq + 16, BR_W), F32)]
    if has_state:
        in_specs += [pl.BlockSpec((1, 1, 2, BR_W), lambda b: (b, l, 0, 0)), pl.BlockSpec(memory_space=pl.ANY)]
        args += [state, z_r]
        return pl.pallas_call(
            functools.partial(_lru_kernel, seq, True),
            grid=(nseq,), in_specs=in_specs, out_specs=zspec, out_shape=zshape,
            scratch_shapes=scratch, input_output_aliases={8: 0},
            compiler_params=_cparams(("arbitrary",)), name=f"lru_lat_l{l}",
        )(*args)
    aliases = {}
    if states is not None:
        in_specs += [pl.BlockSpec(memory_space=pl.ANY)]
        args += [states]
        aliases = {7: 1}
    return pl.pallas_call(
        functools.partial(_lru_kernel, seq, False),
        grid=(nseq,), in_specs=in_specs,
        out_specs=[zspec, pl.BlockSpec((1, 1, 2, BR_W), lambda b: (b, l, 0, 0))],
        out_shape=[zshape, jax.ShapeDtypeStruct((nseq, DEPTH, 2, BR_W), F32)],
        scratch_shapes=scratch, input_output_aliases=aliases,
        compiler_params=_cparams(("arbitrary",)), name=f"lru_ctx_l{l}",
    )(*args)


def _merge_kernel(final, xa_ref, xb_ref, mod_ref, h_ref, za_ref, zr_ref, zc_ref, zd_ref,
                  wmg_ref, bmg_ref, wbo_ref, wo_ref, nf_ref, *refs):
    out_refs, (wmg_s, wbo_s, wo_s) = refs[:-3], refs[-3:]
    i = pl.program_id(0)

    @pl.when(i < W_STEPS)
    def _():
        r0 = pl.multiple_of(i * W_ROWS, W_ROWS)
        for n in range(N_BRANCH):
            wmg_s[pl.ds(r0, W_ROWS), n * D_MODEL:(n + 1) * D_MODEL] = wmg_ref[0, :, n, :].astype(BF16)
        _cast_rows(i, wbo_ref, wbo_s)
        _cast_rows(i, wo_ref, wo_s)

    @pl.when(i >= W_STEPS)
    def _():
        t = i - W_STEPS
        x = _load_x(t, xa_ref, xb_ref)
        gate = mod_ref[0][:, 2 * D_MODEL:]
        h = h_ref[...]
        zs = [za_ref[...], zr_ref[...], zc_ref[...], zd_ref[...]]
        bmg = bmg_ref[0]
        cols = []
        for c in range(0, D_MODEL, 512):
            acc = None
            for n in range(N_BRANCH):
                g = jax.nn.sigmoid(_dot(h, wmg_s[:, n * D_MODEL + c:n * D_MODEL + c + 512])
                                   + bmg[n:n + 1, c:c + 512])
                term = g * _dot(zs[n], wbo_s[n * BR_W:(n + 1) * BR_W, c:c + 512])
                acc = term if acc is None else acc + term
            cols.append(acc.astype(BF16))
        merged = jnp.concatenate(cols, axis=1)
        xn = x + gate * _dot(merged, wo_s[...])
        if not final:
            out_refs[0][...] = xn
        else:
            y = xn * lax.rsqrt(jnp.mean(xn * xn, axis=-1, keepdims=True) + EPS) * nf_ref[...]

            @pl.when(t < N_CTX_TILES)
            def _():
                out_refs[0][...] = y

            @pl.when(t >= N_CTX_TILES)
            def _():
                out_refs[1][...] = y


def _merge(final, xa, xb, xb_off, mod3, l, h, zs, w_mg, b_mg, w_bo, w_o, norm_f):
    row = lambda i: (_tile(i), 0)
    ctx_blk = lambda i: jnp.minimum(_tile(i), N_CTX_TILES - 1)
    lat_blk = lambda i: jnp.maximum(_tile(i) - N_CTX_TILES, 0)
    if final:
        out_specs = [pl.BlockSpec((TM, D_MODEL), lambda i: (ctx_blk(i), 0)),
                     pl.BlockSpec((TM, D_MODEL), lambda i: (lat_blk(i), 0))]
        out_shape = [jax.ShapeDtypeStruct((N_CTX, D_MODEL), F32), jax.ShapeDtypeStruct((N_LAT, D_MODEL), F32)]
    else:
        out_specs = pl.BlockSpec((TM, D_MODEL), row)
        out_shape = jax.ShapeDtypeStruct((N_TOK, D_MODEL), F32)
    wchunk = lambda width: pl.BlockSpec((1, W_ROWS, width), lambda i: (l, _wchunk(i), 0))
    return pl.pallas_call(
        functools.partial(_merge_kernel, final),
        grid=(W_STEPS + N_TILES,),
        in_specs=[
            pl.BlockSpec((TM, D_MODEL), lambda i: (ctx_blk(i), 0)),
            pl.BlockSpec((TM, D_MODEL), lambda i: (lat_blk(i) + xb_off, 0)),
            pl.BlockSpec((1, 1, 3 * D_MODEL), lambda i: (l * 16 + _mod_row(_tile(i)), 0, 0)),
            pl.BlockSpec((TM, D_MODEL), row),
            pl.BlockSpec((TM, BR_W), row), pl.BlockSpec((TM, BR_W), row),
            pl.BlockSpec((TM, BR_W), row), pl.BlockSpec((TM, BR_W), row),
            pl.BlockSpec((1, W_ROWS, N_BRANCH, D_MODEL), lambda i: (l, _wchunk(i), 0, 0)),
            pl.BlockSpec((1, N_BRANCH, D_MODEL), lambda i: (l, 0, 0)),
            wchunk(D_MODEL),
            wchunk(D_MODEL),
            pl.BlockSpec((1, D_MODEL), lambda i: (0, 0)),
        ],
        out_specs=out_specs,
        out_shape=out_shape,
        scratch_shapes=[pltpu.VMEM((D_MODEL, N_BRANCH * D_MODEL), BF16),
                        pltpu.VMEM((N_BRANCH * BR_W, D_MODEL), BF16),
                        pltpu.VMEM((D_MODEL, D_MODEL), BF16)],
        compiler_params=_cparams(("arbitrary",)),
        name=f"merge_l{l}",
    )(xa, xb, mod3, h, *zs,
      w_mg, b_mg,
      w_bo.reshape(DEPTH, N_BRANCH * BR_W, D_MODEL), w_o, norm_f.reshape(1, D_MODEL))


def _lru_gate_weights(lru_wa, lru_ba, lru_wx, lru_bx):
    w = jnp.stack([lru_wa[:, 0], lru_wx[:, 0], lru_wa[:, 1], lru_wx[:, 1]], axis=1)
    eye = jnp.eye(B_BLOCKS, dtype=w.dtype)
    dense = w[:, :, :, :, None, :] * eye[None, None, :, None, :, None]
    wg = dense.transpose(0, 2, 3, 1, 4, 5).reshape(DEPTH, BR_W, 4 * BR_W).astype(BF16)
    bg = jnp.stack([lru_ba[:, 0], lru_bx[:, 0], lru_ba[:, 1], lru_bx[:, 1]], axis=1).reshape(DEPTH, 1, 4 * BR_W)
    return wg, bg


def kernel(x_prompt, x_sample, cache_diff_k, cache_diff_v, cache_na_k, cache_na_v, cache_swa_k, cache_swa_v,
           state_lru, c, c_ctx, norm_g, w_ada, b_ada, w_in, diff_lambda, diff_norm_g, conv_w, conv_b,
           lru_wa, lru_ba, lru_wx, lru_bx, lru_lam, na_rpb, swa_sink, w_mg, b_mg, w_bo, w_o, norm_f):
    tab_a_np, _ = _rope_tables(A_QK)
    tab_d_np, _ = _rope_tables(HD)
    tab_a, tab_d = jnp.asarray(tab_a_np), jnp.asarray(tab_d_np)

    cvecs = jnp.concatenate([c_ctx[None, :], c, jnp.zeros((16 - 1 - DEC_BATCH, D_MODEL), F32)], axis=0)
    mod3 = _modulation(cvecs, w_ada, b_ada).reshape(DEPTH * 16, 1, 3 * D_MODEL)
    bias = _na_bias(na_rpb)
    wg, bg = _lru_gate_weights(lru_wa, lru_ba, lru_wx, lru_bx)
    norm_g3 = norm_g.reshape(DEPTH, 1, D_MODEL)
    past = [jnp.swapaxes(t, -1, -2) for t in
            (cache_diff_k, cache_diff_v, cache_na_k, cache_na_v, cache_swa_k, cache_swa_v)]

    xa = x_prompt.reshape(N_CTX, D_MODEL)
    xb = x_sample.reshape(N_LAT, D_MODEL)
    xb_off = 0
    caches = states = None
    y_p = y_s = None
    for l in range(DEPTH):
        h, proj, bx, caches = _inproj(xa, xb, xb_off, mod3, l, norm_g3, w_in, tab_a, tab_d, caches)

        lam_l, dg_l, sink_l = diff_lambda[l], diff_norm_g[l][None, :], swa_sink[l][None, :]
        z_a, z_c, z_d = _ctx_attn(proj, l, lam_l, dg_l, sink_l)
        z_a = _lat_diff(proj, past[0], past[1], l, lam_l, dg_l, z_a)
        z_c = _lat_na(proj, past[2], past[3], bias, l, z_c)
        z_d = _lat_swa(proj, past[4], past[5], sink_l, l, z_d)

        lru_args = (bx, proj, conv_w[l], conv_b[l][None, :], wg[l], bg[l], lru_lam[l])
        z_r, states = _lru(SEQ, BATCH, 0, *lru_args, l=l, states=states)
        z_r = _lru(DEC_SEQ, DEC_BATCH, N_CTX, *lru_args, state=state_lru, l=l, z_r=z_r)

        final = l == DEPTH - 1
        out = _merge(final, xa, xb, xb_off, mod3, l, h, (z_a, z_r, z_c, z_d), w_mg, b_mg, w_bo, w_o, norm_f)
        if final:
            y_p, y_s = out
        else:
            xa = xb = out
            xb_off = N_CTX_TILES

    new_caches = [jnp.swapaxes(t, -1, -2) for t in caches]
    return (y_p.reshape(BATCH, SEQ, D_MODEL), y_s.reshape(DEC_BATCH, DEC_SEQ, D_MODEL), *new_caches, states)
```

```python
import functools
import math

import numpy as np
import jax
import jax.numpy as jnp
from jax import lax
from jax.experimental import pallas as pl
from jax.experimental.pallas import tpu as pltpu

F32 = jnp.float32
BF16 = jnp.bfloat16

D_MODEL = 1024
BATCH = 16
SEQ = 256
DEPTH = 2
DEC_BATCH = 8
DEC_SEQ = 1024
PAST_LEN = 512
GRID_W = 64
N_BRANCH = 4
BR_W = D_MODEL // 4
HD = 64
A_HEADS = BR_W // HD
A_QK = HD // 2
B_BLOCKS = 4
B_BLK = BR_W // B_BLOCKS
CONV_W = 4
LRU_C = 8.0
C_HEADS = BR_W // HD
NA_ROWS = 8
NA_COLS = 16
D_HEADS = BR_W // HD
D_KV = 2
D_GROUP = D_HEADS // D_KV
WIN = 128
ROPE_BASE = 10000.0
EPS = 1e-6
NEG = -1e30

N_CTX = BATCH * SEQ
N_LAT = DEC_BATCH * DEC_SEQ
N_TOK = N_CTX + N_LAT
GRID_ROWS = DEC_SEQ // GRID_W

TM = 512
N_CTX_TILES = N_CTX // TM
N_TILES = N_TOK // TM
LAT_TILES_PER_SEQ = DEC_SEQ // TM

PROJ_W = 3328
COL_G = 0
COL_AQ = 1024
COL_AK = 1280
COL_AV = 1536
COL_CQ = 1792
COL_CK = 2048
COL_CV = 2304
COL_DQ = 2560
COL_DK = 2816
COL_DV = 3072
IN_W = 3328
WS_W = IN_W + 2 * D_KV * HD

NA_HALF_Q = DEC_SEQ // 2
NA_KROWS = 12
NA_KSPAN = NA_KROWS * GRID_W
NA_HALF_KSTART = 4 * GRID_W

SWA_QB = 128
SWA_SPAN = SWA_QB + 2 * WIN

LOG2E = math.log2(math.e)
QA_SCALE = A_QK ** -0.5 * LOG2E
QH_SCALE = HD ** -0.5 * LOG2E

VMEM_LIMIT = 56 * 1024 * 1024


def _cparams(sem):
    return pltpu.CompilerParams(dimension_semantics=sem, vmem_limit_bytes=VMEM_LIMIT)


def _resident(shape, index_map):
    return pl.BlockSpec(shape, index_map, pipeline_mode=pl.Buffered(1))


def _dot(a, b):
    return jnp.dot(a, b, preferred_element_type=F32)


def _dot_nt(a, b):
    return lax.dot_general(a, b, (((1,), (1,)), ((), ())), preferred_element_type=F32)


def _silu(x):
    return x * jax.nn.sigmoid(x)


def _rope_tables(d):
    half = d // 2
    quarter = half // 2
    lane = np.arange(128)
    q = lane % d
    use_col = (q >= half)
    i = (q % half) % quarter
    first = (q % half) < quarter
    inv = (ROPE_BASE ** (-np.arange(quarter, dtype=np.float32) / np.float32(quarter))).astype(np.float32)
    t = np.arange(DEC_SEQ)
    pos = np.where(use_col[None, :], (t % GRID_W)[:, None], (t // GRID_W)[:, None]).astype(np.float32)
    ang = (pos * inv[i][None, :]).astype(np.float32)
    cos, sin = np.cos(ang), np.sin(ang)
    c = np.concatenate([np.ones((TM, 128), np.float32), cos], axis=0)
    s1 = np.concatenate([np.zeros((TM, 128), np.float32), np.where(first[None, :], -sin, 0.0)], axis=0)
    s2 = np.concatenate([np.zeros((TM, 128), np.float32), np.where(first[None, :], 0.0, sin)], axis=0)
    return np.stack([c, s1, s2]).astype(np.float32), quarter


def _rope128(x, tab_ref, shift):
    return (x * tab_ref[0] + pltpu.roll(x, 128 - shift, 1) * tab_ref[1]
            + pltpu.roll(x, shift, 1) * tab_ref[2])


def _rope(x, tab_ref, shift):
    w = x.shape[1]
    return jnp.concatenate([_rope128(x[:, c:c + 128], tab_ref, shift) for c in range(0, w, 128)], axis=1)


def _mod_kernel(c_ref, w_ref, b_ref, o_ref):
    c = c_ref[...]
    o_ref[0] = _dot(_silu(c).astype(BF16), w_ref[0].astype(BF16)) + b_ref[0]


def _modulation(cvecs, w_ada, b_ada):
    nb = 3 * D_MODEL // 768
    return pl.pallas_call(
        _mod_kernel,
        grid=(DEPTH, nb),
        in_specs=[pl.BlockSpec((16, D_MODEL), lambda l, j: (0, 0)),
                  pl.BlockSpec((1, D_MODEL, 768), lambda l, j: (l, 0, j)),
                  pl.BlockSpec((1, 1, 768), lambda l, j: (l, 0, j))],
        out_specs=pl.BlockSpec((1, 16, 768), lambda l, j: (l, 0, j)),
        out_shape=jax.ShapeDtypeStruct((DEPTH, 16, 3 * D_MODEL), F32),
        compiler_params=_cparams(("arbitrary", "arbitrary")),
        name="modulation",
    )(cvecs, w_ada, b_ada.reshape(DEPTH, 1, 3 * D_MODEL))


W_STEPS = 8
W_ROWS = D_MODEL // W_STEPS


def _tile(i):
    return jnp.maximum(i - W_STEPS, 0)


def _wchunk(i):
    return jnp.minimum(i, W_STEPS - 1)


def _mod_row(t):
    return jnp.where(t < N_CTX_TILES, 0, 1 + (t - N_CTX_TILES) // LAT_TILES_PER_SEQ)


def _rope_blk(t):
    return jnp.where(t < N_CTX_TILES, 0, 1 + (t - N_CTX_TILES) % LAT_TILES_PER_SEQ)


def _load_x(t, xa_ref, xb_ref):
    return jnp.where(t < N_CTX_TILES, xa_ref[...], xb_ref[...])


def _cast_rows(i, src_ref, dst_ref):
    r0 = pl.multiple_of(i * W_ROWS, W_ROWS)
    dst_ref[pl.ds(r0, W_ROWS), :] = src_ref[0].astype(BF16)


WI_G, WI_AQ, WI_AK, WI_AV, WI_BX, WI_CQ, WI_CKV, WI_DQ, WI_DKV = 0, 1024, 1280, 1536, 1792, 2048, 2304, 2816, 3072
SEQ_PER_TILE = TM // SEQ


def _store_heads(ref, val, n_heads, width):
    vt = val.T
    for s in range(SEQ_PER_TILE):
        for h in range(n_heads):
            ref[s, 0, h] = vt[h * width:(h + 1) * width, s * SEQ:(s + 1) * SEQ]


def _inproj_kernel(n_alias, xa_ref, xb_ref, mod_ref, g_ref, w_ref, ta_ref, td_ref, *refs):
    h_ref, proj_ref, bx_ref, dk_ref, dv_ref, nk_ref, nv_ref, sk_ref, sv_ref, w_s = refs[n_alias:]
    i = pl.program_id(0)

    @pl.when(i < W_STEPS)
    def _():
        r0 = pl.multiple_of(i * W_ROWS, W_ROWS)
        wb = w_ref[0].astype(BF16)
        w_s[pl.ds(r0, W_ROWS), 0:WI_DKV] = wb[:, 0:WI_DKV]
        for n, src in enumerate((0, 0, 1, 1, 2, 2, 3, 3)):
            w_s[pl.ds(r0, W_ROWS), WI_DKV + n * HD:WI_DKV + (n + 1) * HD] = \
                wb[:, WI_DKV + src * HD:WI_DKV + (src + 1) * HD]

    @pl.when(i >= W_STEPS)
    def _():
        t = i - W_STEPS
        is_ctx = t < N_CTX_TILES
        x = _load_x(t, xa_ref, xb_ref)
        mod = mod_ref[0]
        shift, scale = mod[:, :D_MODEL], mod[:, D_MODEL:2 * D_MODEL]
        xn = x * lax.rsqrt(jnp.mean(x * x, axis=-1, keepdims=True) + EPS) * g_ref[0]
        hb =(xn * (1.0 + scale) + shift).astype(BF16)
        h_ref[...] = hb

        def mm(c0, width):
            return _dot(hb, w_s[:, c0:c0 + width])

        gs = [mm(WI_G + c, 512) for c in range(0, 1024, 512)]
        aq, ak, av = mm(WI_AQ, 256), mm(WI_AK, 256), mm(WI_AV, 256)
        cq, ckv = mm(WI_CQ, 256), mm(WI_CKV, 512)
        dq, dk4, dv4 = mm(WI_DQ, 256), mm(WI_DKV, 256), mm(WI_DKV + 256, 256)
        bx_ref[...] = mm(WI_BX, 256)

        for n, g in enumerate(gs):
            proj_ref[:, COL_G + n * 512:COL_G + (n + 1) * 512] = _silu(g).astype(BF16)
        proj_ref[:, COL_AQ:COL_AQ + 256] = _rope(aq * QA_SCALE, ta_ref, A_QK // 4).astype(BF16)
        proj_ref[:, COL_AK:COL_AK + 256] = _rope(ak, ta_ref, A_QK // 4).astype(BF16)
        proj_ref[:, COL_AV:COL_AV + 256] = av.astype(BF16)
        proj_ref[:, COL_CQ:COL_CQ + 256] = (cq * QH_SCALE).astype(BF16)
        proj_ref[:, COL_CK:COL_CK + 512] = ckv.astype(BF16)
        proj_ref[:, COL_DQ:COL_DQ + 256] = _rope(dq * QH_SCALE, td_ref, HD // 4).astype(BF16)
        proj_ref[:, COL_DK:COL_DK + 256] = _rope(dk4, td_ref, HD // 4).astype(BF16)
        proj_ref[:, COL_DV:COL_DV + 256] = dv4.astype(BF16)

        @pl.when(is_ctx)
        def _():
            akt = ak.T
            for s in range(SEQ_PER_TILE):
                for h in range(A_HEADS):
                    for m in range(2):
                        c0 = h * HD + m * A_QK
                        dk_ref[s, 0, h, m] = akt[c0:c0 + A_QK, s * SEQ:(s + 1) * SEQ]
            _store_heads(dv_ref, av, A_HEADS, HD)
            _store_heads(nk_ref, ckv[:, :256], C_HEADS, HD)
            _store_heads(nv_ref, ckv[:, 256:], C_HEADS, HD)
            for ref, val in ((sk_ref, dk4), (sv_ref, dv4)):
                vt = val.T
                for s in range(SEQ_PER_TILE):
                    for kv in range(D_KV):
                        ref[s, 0, kv] = vt[kv * D_GROUP * HD:kv * D_GROUP * HD + HD, s * SEQ:(s + 1) * SEQ]


def _cache_shapes():
    hs = lambda n: (BATCH, DEPTH, n, HD, SEQ)
    return [(BATCH, DEPTH, A_HEADS, 2, A_QK, SEQ), hs(A_HEADS), hs(C_HEADS), hs(C_HEADS), hs(D_KV), hs(D_KV)]


def _inproj(xa, xb, xb_off, mod3, l, norm_g, w_in, tab_a, tab_d, caches):
    row = lambda i: (_tile(i), 0)
    ctx_blk = lambda i: jnp.minimum(_tile(i), N_CTX_TILES - 1)

    def cache_spec(shape):
        blk = (SEQ_PER_TILE, 1) + shape[2:]
        nz = len(shape) - 2
        return pl.BlockSpec(blk, lambda i: (ctx_blk(i), l) + (0,) * nz)

    cshapes = _cache_shapes()
    aliases = {} if caches is None else {7 + k: 3 + k for k in range(6)}
    extra_specs = [] if caches is None else [pl.BlockSpec(memory_space=pl.ANY)] * 6
    extra_args = [] if caches is None else list(caches)
    outs = pl.pallas_call(
        functools.partial(_inproj_kernel, len(extra_args)),
        grid=(W_STEPS + N_TILES,),
        in_specs=[
            pl.BlockSpec((TM, D_MODEL), lambda i: (ctx_blk(i), 0)),
            pl.BlockSpec((TM, D_MODEL), lambda i: (jnp.maximum(_tile(i) - N_CTX_TILES, 0) + xb_off, 0)),
            pl.BlockSpec((1, 1, 3 * D_MODEL), lambda i: (l * 16 + _mod_row(_tile(i)), 0, 0)),
            pl.BlockSpec((1, 1, D_MODEL), lambda i: (l, 0, 0)),
            pl.BlockSpec((1, W_ROWS, IN_W), lambda i: (l, _wchunk(i), 0)),
            pl.BlockSpec((3, TM, 128), lambda i: (0, _rope_blk(_tile(i)), 0)),
            pl.BlockSpec((3, TM, 128), lambda i: (0, _rope_blk(_tile(i)), 0)),
        ] + extra_specs,
        out_specs=[
            pl.BlockSpec((TM, D_MODEL), row),
            pl.BlockSpec((TM, PROJ_W), row),
            pl.BlockSpec((TM, BR_W), row),
        ] + [cache_spec(s) for s in cshapes],
        out_shape=[
            jax.ShapeDtypeStruct((N_TOK, D_MODEL), BF16),
            jax.ShapeDtypeStruct((N_TOK, PROJ_W), BF16),
            jax.ShapeDtypeStruct((N_TOK, BR_W), F32),
        ] + [jax.ShapeDtypeStruct(s, F32) for s in cshapes],
        scratch_shapes=[pltpu.VMEM((D_MODEL, WS_W), BF16)],
        input_output_aliases=aliases,
        compiler_params=_cparams(("arbitrary",)),
        name=f"inproj_l{l}",
    )(xa, xb, mod3, norm_g, w_in, tab_a, tab_d, *extra_args)
    return outs[0], outs[1], outs[2], tuple(outs[3:])


def _diff_lambda(lam_ref, lam_init):
    lv = lam_ref[0]
    s1 = jnp.sum(lv[0:1] * lv[1:2], axis=-1, keepdims=True)
    s2 = jnp.sum(lv[2:3] * lv[3:4], axis=-1, keepdims=True)
    return jnp.exp(s1) - jnp.exp(s2) + lam_init


def _diff_norm(o, dg_ref, lam_init):
    y = o * lax.rsqrt(jnp.mean(o * o, axis=-1, keepdims=True) + EPS) * dg_ref[0]
    return y * (1.0 - lam_init)


def _exp2_parts(parts, extra=None):
    m = functools.reduce(jnp.maximum, [jnp.max(s, axis=-1, keepdims=True) for s in parts])
    if extra is not None:
        m = jnp.maximum(m, extra)
    return [jnp.exp2(s - m).astype(BF16) for s in parts], m


def _pipelined(n, score_fn, pv_fn):
    outs = []
    nxt = score_fn(0)
    for c in range(n):
        cur, nxt = nxt, (score_fn(c + 1) if c + 1 < n else None)
        outs.append(pv_fn(c, cur))
    return outs


def _with_ones(v):
    return jnp.concatenate([v, jnp.ones_like(v)], axis=1)


def _with_ones_t(vt):
    return jnp.concatenate([vt, jnp.ones((16, vt.shape[1]), vt.dtype)], axis=0)


def _stack_group(dq, kv, rows):
    return jnp.concatenate([dq[:, (kv * D_GROUP + g) * HD:(kv * D_GROUP + g + 1) * HD]
                            for g in range(D_GROUP)], axis=0)


def _sink_col(sink2, kv, rows):
    r = lax.broadcasted_iota(jnp.int32, (D_GROUP * rows, 1), 0)
    return jnp.where(r < rows, sink2[:, kv * D_GROUP:kv * D_GROUP + 1], sink2[:, kv * D_GROUP + 1:kv * D_GROUP + 2])


CTX_PER_STEP = 2


def _lane_masks(n):
    lane = lax.broadcasted_iota(jnp.int32, (1, BR_W), 1)
    w = BR_W // n
    ms = [jnp.where((lane >= p * w) & (lane < (p + 1) * w), 1.0, 0.0).astype(BF16) for p in range(n)]
    return ms, [1.0 - m for m in ms]


def _swap_halves(p):
    return jnp.concatenate([p[:, BR_W // 2:], p[:, :BR_W // 2]], axis=1)


def _pick_heads(os):
    grp = lax.broadcasted_iota(jnp.int32, os[0].shape, 1) // HD
    acc = os[0]
    for h in range(1, len(os)):
        acc = jnp.where(grp == h, os[h], acc)
    return acc


def _head_rms(o, dg4_ref, lam_init):
    r = lax.broadcasted_iota(jnp.int32, (BR_W, BR_W), 0) // HD
    c = lax.broadcasted_iota(jnp.int32, (BR_W, BR_W), 1) // HD
    ss = _dot((o * o).astype(BF16), jnp.where(r == c, 1.0, 0.0).astype(BF16))
    return o * lax.rsqrt(ss * (1.0 / HD) + EPS) * dg4_ref[0] * (1.0 - lam_init)


def _ctx_attn_kernel(lam_init, aq_ref, ak_ref, av_ref, cq_ref, ck_ref, cv_ref, dq_ref, dk_ref, dv_ref,
                     ga_ref, gc_ref, gd_ref, lam_ref, dg4_ref, sink_ref, za_ref, zc_ref, zd_ref):
    lam = _diff_lambda(lam_ref, lam_init)
    sink2 = sink_ref[0] * LOG2E
    m_qk, _ = _lane_masks(2 * A_HEADS)
    m_hd, o_hd = _lane_masks(C_HEADS)
    seqs = [slice(s * SEQ, (s + 1) * SEQ) for s in range(CTX_PER_STEP)]

    sa, sc, sd = [], [], []
    for rs in seqs:
        aq, cq, dq = aq_ref[rs, :], cq_ref[rs, :], dq_ref[rs, :]
        ak, ck, dk = ak_ref[rs, :], ck_ref[rs, :], dk_ref[rs, :]
        sa.append([_dot_nt(aq * m, ak) for m in m_qk])
        sc.append([_dot_nt(cq * m, ck) for m in m_hd])
        sd.append([_dot_nt(dq * m, dk) for m in m_hd])

    ea = [[_exp2_parts([s])[0][0] for s in ss] for ss in sa]
    ec = [[_exp2_parts([s])[0][0] for s in ss] for ss in sc]
    ed = [[_exp2_parts([s], extra=sink2[:, j:j + 1]) for j, s in enumerate(ss)] for ss in sd]

    for i, rs in enumerate(seqs):
        av, cv, dv = av_ref[rs, :], cv_ref[rs, :], dv_ref[rs, :]
        os = []
        for h in range(A_HEADS):
            w = av * m_hd[h] + o_hd[h]
            p0, p1 = _dot(ea[i][2 * h], w), _dot(ea[i][2 * h + 1], w)
            os.append(p0 * (1.0 / _swap_halves(p0)) - p1 * (lam / _swap_halves(p1)))
        za_ref[rs, :] = (_head_rms(_pick_heads(os), dg4_ref, lam_init) * ga_ref[rs, :].astype(F32)).astype(BF16)

        os = []
        for h in range(C_HEADS):
            p = _dot(ec[i][h], cv * m_hd[h] + o_hd[h])
            os.append(p * (1.0 / _swap_halves(p)))
        zc_ref[rs, :] = (_pick_heads(os) * gc_ref[rs, :].astype(F32)).astype(BF16)

        os = []
        for j in range(D_HEADS):
            (e,), m = ed[i][j]
            p = _dot(e, dv * m_hd[j] + o_hd[j])
            os.append(p * (1.0 / (_swap_halves(p) + jnp.exp2(sink2[:, j:j + 1] - m))))
        zd_ref[rs, :] = (_pick_heads(os) * gd_ref[rs, :].astype(F32)).astype(BF16)


def _ctx_attn(proj, l, diff_lambda, dg4, sink3):
    lam_init = 0.8 - 0.6 * math.exp(-0.3 * l)
    colblk = lambda cb: pl.BlockSpec((CTX_PER_STEP * SEQ, 256), lambda b: (b, cb))
    zspec = pl.BlockSpec((CTX_PER_STEP * SEQ, BR_W), lambda b: (b, 0))
    zshape = jax.ShapeDtypeStruct((N_TOK, BR_W), BF16)
    return pl.pallas_call(
        functools.partial(_ctx_attn_kernel, lam_init),
        grid=(BATCH // CTX_PER_STEP,),
        in_specs=[colblk(COL_AQ // 256), colblk(COL_AK // 256), colblk(COL_AV // 256),
                  colblk(COL_CQ // 256), colblk(COL_CK // 256), colblk(COL_CV // 256),
                  colblk(COL_DQ // 256), colblk(COL_DK // 256), colblk(COL_DV // 256),
                  colblk(0), colblk(2), colblk(3),
                  pl.BlockSpec((1, 4, A_QK), lambda b: (l, 0, 0)),
                  pl.BlockSpec((1, 1, BR_W), lambda b: (l, 0, 0)),
                  pl.BlockSpec((1, 1, D_HEADS), lambda b: (l, 0, 0))],
        out_specs=[zspec, zspec, zspec],
        out_shape=[zshape, zshape, zshape],
        compiler_params=_cparams(("arbitrary",)),
        name=f"ctx_attn_l{l}",
    )(*([proj] * 12), diff_lambda, dg4, sink3)


LA_QB = 256


def _lat_diff_kernel(lam_init, aq_ref, ak_ref, av_ref, ck_ref, cv_ref, ga_ref, lam_ref, dg_ref, zin_ref, za_ref):
    del zin_ref
    lam = _diff_lambda(lam_ref, lam_init)
    aq = aq_ref[...]
    va_ctx = [_with_ones_t(cv_ref[0, 0, h].astype(BF16)) for h in range(A_HEADS)]
    va_loc = [_with_ones(av_ref[:, h * HD:(h + 1) * HD]) for h in range(A_HEADS)]

    def scores(c):
        h, m = divmod(c, 2)
        lo = h * HD + m * A_QK
        q = aq[:, lo:lo + A_QK]
        return [_dot(q, ck_ref[0, 0, h, m].astype(BF16)),
                _dot_nt(q, ak_ref[:, lo:lo + A_QK])]

    def pv(c, s):
        (e_ctx, e_loc), _ = _exp2_parts(s)
        p_ctx, p_loc = _dot_nt(e_ctx, va_ctx[c // 2]), _dot(e_loc, va_loc[c // 2])
        return p_ctx[:, :HD] + p_loc[:, :HD], p_ctx[:, HD:HD + 1] + p_loc[:, HD:HD + 1]

    ss = [scores(c) for c in range(2 * A_HEADS)]
    nd = [pv(c, s) for c, s in enumerate(ss)]
    outs = []
    for h in range(A_HEADS):
        (n0, d0), (n1, d1) = nd[2 * h], nd[2 * h + 1]
        outs.append(_diff_norm(n0 * (1.0 / d0) - n1 * (lam / d1), dg_ref, lam_init))
    za_ref[...] = (jnp.concatenate(outs, axis=1) * ga_ref[...].astype(F32)).astype(BF16)


def _lat_diff(proj, cache_k, cache_v, l, diff_lambda, dg3, z_a):
    lam_init = 0.8 - 0.6 * math.exp(-0.3 * l)
    nq = DEC_SEQ // LA_QB
    qrow = lambda b, j: N_CTX // LA_QB + b * nq + j
    srow = lambda b, j: N_CTX // DEC_SEQ + b
    return pl.pallas_call(
        functools.partial(_lat_diff_kernel, lam_init),
        grid=(DEC_BATCH, nq),
        in_specs=[pl.BlockSpec((LA_QB, 256), lambda b, j: (qrow(b, j), COL_AQ // 256)),
                  pl.BlockSpec((DEC_SEQ, 256), lambda b, j: (srow(b, j), COL_AK // 256)),
                  pl.BlockSpec((DEC_SEQ, 256), lambda b, j: (srow(b, j), COL_AV // 256)),
                  pl.BlockSpec((1, 1, A_HEADS, 2, A_QK, PAST_LEN), lambda b, j: (b, l, 0, 0, 0, 0)),
                  pl.BlockSpec((1, 1, A_HEADS, HD, PAST_LEN), lambda b, j: (b, l, 0, 0, 0)),
                  pl.BlockSpec((LA_QB, 256), lambda b, j: (qrow(b, j), 0)),
                  pl.BlockSpec((1, 4, A_QK), lambda b, j: (l, 0, 0)),
                  pl.BlockSpec((1, 1, HD), lambda b, j: (l, 0, 0)),
                  pl.BlockSpec(memory_space=pl.ANY)],
        out_specs=pl.BlockSpec((LA_QB, BR_W), lambda b, j: (qrow(b, j), 0)),
        out_shape=jax.ShapeDtypeStruct((N_TOK, BR_W), BF16),
        input_output_aliases={8: 0},
        compiler_params=_cparams(("arbitrary", "arbitrary")),
        name=f"lat_diff_l{l}",
    )(proj, proj, proj, cache_k, cache_v, proj, diff_lambda, dg3, z_a)


def _na_bias_kernel(rpb_ref, o_ref):
    l, h = pl.program_id(0), pl.program_id(1)
    base = (l * C_HEADS + h) * ((2 * NA_ROWS - 1) * (2 * NA_COLS - 1))
    qc = lax.broadcasted_iota(jnp.int32, (GRID_W, GRID_W), 0)
    kc = lax.broadcasted_iota(jnp.int32, (GRID_W, GRID_W), 1)
    cs = jnp.clip(qc - NA_COLS // 2, 0, GRID_W - NA_COLS)
    ok = (kc >= cs) & (kc < cs + NA_COLS)
    dcol = kc - qc + (NA_COLS - 1)
    neg = jnp.full((GRID_W, GRID_W), NEG, F32)
    tabs = []
    for d in range(2 * NA_ROWS - 1):
        t = neg
        for dc in range(2 * NA_COLS - 1):
            t = jnp.where(ok & (dcol == dc), rpb_ref[base + d * (2 * NA_COLS - 1) + dc] * LOG2E, t)
        tabs.append(t)
    for half in range(2):
        for rq in range(NA_ROWS):
            r = half * NA_ROWS + rq
            rs = min(max(r - NA_ROWS // 2, 0), GRID_ROWS - NA_ROWS)
            blks = []
            for kr in range(NA_KROWS):
                kabs = half * 4 + kr
                blks.append(tabs[kabs - r + NA_ROWS - 1] if rs <= kabs < rs + NA_ROWS else neg)
            o_ref[0, 0, half, rq * GRID_W:(rq + 1) * GRID_W, :] = jnp.concatenate(blks, axis=1)


def _na_bias(na_rpb):
    return pl.pallas_call(
        _na_bias_kernel,
        grid=(DEPTH, C_HEADS),
        in_specs=[pl.BlockSpec(memory_space=pltpu.SMEM)],
        out_specs=pl.BlockSpec((1, 1, 2, NA_HALF_Q, NA_KSPAN), lambda l, h: (l, h, 0, 0, 0)),
        out_shape=jax.ShapeDtypeStruct((DEPTH, C_HEADS, 2, NA_HALF_Q, NA_KSPAN), F32),
        compiler_params=_cparams(("arbitrary", "arbitrary")),
        name="na_bias",
    )(na_rpb.reshape(-1))


def _lat_na_kernel(cq_ref, ck_ref, cv_ref, kc_ref, vc_ref, bias_ref, gc_ref, zin_ref, zc_ref):
    del zin_ref
    half = pl.program_id(0)
    k0 = pl.multiple_of(half * NA_HALF_KSTART, NA_HALF_KSTART)
    cq = cq_ref[...]
    kl_all = ck_ref[pl.ds(k0, NA_KSPAN), :]
    vl_all = cv_ref[pl.ds(k0, NA_KSPAN), :]
    def scores(h):
        q = cq[:, h * HD:(h + 1) * HD]
        return [_dot(q, kc_ref[0, 0, h].astype(BF16)),
                _dot_nt(q, kl_all[:, h * HD:(h + 1) * HD]) + bias_ref[0, h, 0]]

    def pv(h, s):
        (e_ctx, e_loc), _ = _exp2_parts(s)
        p_ctx = _dot_nt(e_ctx, _with_ones_t(vc_ref[0, 0, h].astype(BF16)))
        p_loc = _dot(e_loc, _with_ones(vl_all[:, h * HD:(h + 1) * HD]))
        return (p_ctx[:, :HD] + p_loc[:, :HD]) * (1.0 / (p_ctx[:, HD:HD + 1] + p_loc[:, HD:HD + 1]))

    ss = [scores(h) for h in range(C_HEADS)]
    outs = [pv(h, s) for h, s in enumerate(ss)]
    zc_ref[...] = (jnp.concatenate(outs, axis=1) * gc_ref[...].astype(F32)).astype(BF16)


def _lat_na(proj, cache_k, cache_v, bias, l, z_c):
    qrow = lambda hf, b: N_CTX // NA_HALF_Q + b * 2 + hf
    srow = lambda hf, b: N_CTX // DEC_SEQ + b
    return pl.pallas_call(
        _lat_na_kernel,
        grid=(2, DEC_BATCH),
        in_specs=[pl.BlockSpec((NA_HALF_Q, 256), lambda hf, b: (qrow(hf, b), COL_CQ // 256)),
                  pl.BlockSpec((DEC_SEQ, 256), lambda hf, b: (srow(hf, b), COL_CK // 256)),
                  pl.BlockSpec((DEC_SEQ, 256), lambda hf, b: (srow(hf, b), COL_CV // 256)),
                  pl.BlockSpec((1, 1, C_HEADS, HD, PAST_LEN), lambda hf, b: (b, l, 0, 0, 0)),
                  pl.BlockSpec((1, 1, C_HEADS, HD, PAST_LEN), lambda hf, b: (b, l, 0, 0, 0)),
                  pl.BlockSpec((1, C_HEADS, 1, NA_HALF_Q, NA_KSPAN), lambda hf, b: (l, 0, hf, 0, 0)),
                  pl.BlockSpec((NA_HALF_Q, 256), lambda hf, b: (qrow(hf, b), 2)),
                  pl.BlockSpec(memory_space=pl.ANY)],
        out_specs=pl.BlockSpec((NA_HALF_Q, BR_W), lambda hf, b: (qrow(hf, b), 0)),
        out_shape=jax.ShapeDtypeStruct((N_TOK, BR_W), BF16),
        input_output_aliases={7: 0},
        compiler_params=_cparams(("arbitrary", "arbitrary")),
        name=f"lat_na_l{l}",
    )(proj, proj, proj, cache_k, cache_v, bias, proj, z_c)


def _lat_swa_kernel(dq_ref, dk_ref, dv_ref, kc_ref, vc_ref, sink_ref, gd_ref, zin_ref, zd_ref):
    del zin_ref
    nq = DEC_SEQ // SWA_QB
    rows = D_GROUP * SWA_QB
    dq, dk, dv = dq_ref[...], dk_ref[...], dv_ref[...]
    sink2 = sink_ref[0] * LOG2E
    m_hd, _ = _lane_masks(D_HEADS)
    starts = [min(max(j * SWA_QB - WIN, 0), DEC_SEQ - SWA_SPAN) for j in range(nq)]
    qi = lax.broadcasted_iota(jnp.int32, (rows, SWA_SPAN), 0) % SWA_QB
    ki = lax.broadcasted_iota(jnp.int32, (rows, SWA_SPAN), 1)
    masks = {off: jnp.abs(qi + off - ki) <= WIN for off in sorted({j * SWA_QB - s for j, s in enumerate(starts)})}
    kc_t = jnp.concatenate([kc_ref[0, 0, kv] for kv in range(D_KV) for _ in range(D_GROUP)], axis=0).astype(BF16)
    vc_t = jnp.concatenate([vc_ref[0, 0, kv] for kv in range(D_KV) for _ in range(D_GROUP)], axis=0).astype(BF16)
    feat = lax.broadcasted_iota(jnp.int32, (BR_W, 1), 0) // (D_GROUP * HD)

    qm = [dq * m for m in m_hd]
    sks = [_sink_col(sink2, kv, SWA_QB) for kv in range(D_KV)]
    w_loc, w_ctx = [], []
    for kv in range(D_KV):
        pair = m_hd[kv * D_GROUP] + m_hd[kv * D_GROUP + 1]
        w_loc.append(dv * pair + (1.0 - pair))
        w_ctx.append(jnp.where(feat == kv, vc_t, jnp.ones_like(vc_t)))

    def scores(c):
        kv, j = divmod(c, nq)
        q2 = jnp.concatenate([qm[kv * D_GROUP + g][j * SWA_QB:(j + 1) * SWA_QB] for g in range(D_GROUP)], axis=0)
        return [_dot(q2, kc_t),
                jnp.where(masks[j * SWA_QB - starts[j]], _dot_nt(q2, dk[starts[j]:starts[j] + SWA_SPAN]), NEG)]

    def pv(c, s):
        kv, j = divmod(c, nq)
        (e_ctx, e_loc), m = _exp2_parts(s, extra=sks[kv])
        p = _dot_nt(e_ctx, w_ctx[kv]) + _dot(e_loc, w_loc[kv][starts[j]:starts[j] + SWA_SPAN])
        return p * (1.0 / (_swap_halves(p) + jnp.exp2(sks[kv] - m)))

    o = _pipelined(D_KV * nq, scores, pv)
    os = [jnp.concatenate([o[kv * nq + j][g * SWA_QB:(g + 1) * SWA_QB] for j in range(nq)], axis=0)
          for kv in range(D_KV) for g in range(D_GROUP)]
    zd_ref[...] = (_pick_heads(os) * gd_ref[...].astype(F32)).astype(BF16)


def _lat_swa(proj, cache_k, cache_v, sink3, l, z_d):
    srow = lambda b: (N_CTX // DEC_SEQ + b)
    return pl.pallas_call(
        _lat_swa_kernel,
        grid=(DEC_BATCH,),
        in_specs=[pl.BlockSpec((DEC_SEQ, 256), lambda b: (srow(b), COL_DQ // 256)),
                  pl.BlockSpec((DEC_SEQ, 256), lambda b: (srow(b), COL_DK // 256)),
                  pl.BlockSpec((DEC_SEQ, 256), lambda b: (srow(b), COL_DV // 256)),
                  pl.BlockSpec((1, 1, D_KV, HD, PAST_LEN), lambda b: (b, l, 0, 0, 0)),
                  pl.BlockSpec((1, 1, D_KV, HD, PAST_LEN), lambda b: (b, l, 0, 0, 0)),
                  pl.BlockSpec((1, 1, D_HEADS), lambda b: (l, 0, 0)),
                  pl.BlockSpec((DEC_SEQ, 256), lambda b: (srow(b), 3)),
                  pl.BlockSpec(memory_space=pl.ANY)],
        out_specs=pl.BlockSpec((DEC_SEQ, BR_W), lambda b: (srow(b), 0)),
        out_shape=jax.ShapeDtypeStruct((N_TOK, BR_W), BF16),
        input_output_aliases={7: 0},
        compiler_params=_cparams(("arbitrary",)),
        name=f"lat_swa_l{l}",
    )(proj, proj, proj, cache_k, cache_v, sink3, proj, z_d)


def _lru_kernel(seq, has_state, *refs):
    if has_state:
        bx_ref, cw_ref, cb_ref, wg_ref, bg_ref, lam_ref, gr_ref, st_ref, zin_ref, zr_ref, a_s, u_s, y_s = refs
        del zin_ref
        fin_ref = None
    else:
        bx_ref, cw_ref, cb_ref, wg_ref, bg_ref, lam_ref, gr_ref = refs[:7]
        zr_ref, fin_ref, a_s, u_s, y_s = refs[-5:]
    x = bx_ref[...]
    cw = cw_ref[0]
    xp_s = y_s.at[0]
    xp_s[0:8, :] = jnp.zeros((8, BR_W), F32)
    xp_s[8:8 + seq, :] = x
    xp_s[8 + seq:16 + seq, :] = jnp.zeros((8, BR_W), F32)
    xm1, xp1, xp2 = xp_s[7:7 + seq, :], xp_s[9:9 + seq, :], xp_s[10:10 + seq, :]
    xc = cb_ref[0] + xm1 * cw[0:1] + x * cw[1:2] + xp1 * cw[2:3] + xp2 * cw[3:4]

    gates = _dot(xc.astype(BF16), wg_ref[0]) + bg_ref[0]
    lam = lam_ref[0]
    nl = -lam
    softplus = jnp.maximum(nl, 0.0) + jnp.log1p(jnp.exp(-jnp.abs(nl)))
    tin = lax.broadcasted_iota(jnp.int32, (1, 8, 1), 1)
    for d in range(2):
        r = jax.nn.sigmoid(gates[:, (2 * d) * BR_W:(2 * d + 1) * BR_W])
        ig = jax.nn.sigmoid(gates[:, (2 * d + 1) * BR_W:(2 * d + 2) * BR_W])
        log_a = -LRU_C * r * softplus[d:d + 1]
        a = jnp.exp(log_a)
        u = jnp.sqrt(-jnp.tanh(log_a) * (a * a + 1.0)) * (ig * xc)
        a = a.reshape(seq // 8, 8, BR_W)
        u = u.reshape(seq // 8, 8, BR_W)
        for s in (1, 2, 4):
            if d == 0:
                keep = tin >= s
                a_n = jnp.where(keep, pltpu.roll(a, s, 1), 1.0)
                u_n = jnp.where(keep, pltpu.roll(u, s, 1), 0.0)
            else:
                keep = tin < 8 - s
                a_n = jnp.where(keep, pltpu.roll(a, 8 - s, 1), 1.0)
                u_n = jnp.where(keep, pltpu.roll(u, 8 - s, 1), 0.0)
            u = u + a * u_n
            a = a * a_n
        a_s[d] = a.reshape(seq, BR_W)
        u_s[d] = u.reshape(seq, BR_W)

    if has_state:
        hf0 = st_ref[0, 0, 0:1, :]
        hb0 = st_ref[0, 0, 1:2, :]
    else:
        hf0 = jnp.zeros((1, BR_W), F32)
        hb0 = jnp.zeros((1, BR_W), F32)

    nt = seq // 8

    def body(k, carry):
        hf, hb = carry
        rf = pl.ds(pl.multiple_of(k * 8, 8), 8)
        rb = pl.ds(pl.multiple_of((nt - 1 - k) * 8, 8), 8)
        yf = u_s[0, rf, :] + a_s[0, rf, :] * hf
        yb = u_s[1, rb, :] + a_s[1, rb, :] * hb
        y_s[0, rf, :] = yf
        y_s[1, rb, :] = yb
        return yf[7:8, :], yb[0:1, :]

    hf, hb = lax.fori_loop(0, nt, body, (hf0, hb0))
    zr_ref[...] = ((y_s[0, 0:seq, :] + y_s[1, 0:seq, :]) * gr_ref[...].astype(F32)).astype(BF16)
    if fin_ref is not None:
        fin_ref[0, 0, 0:1, :] = hf
        fin_ref[0, 0, 1:2, :] = hb


def _lru(seq, nseq, row0, bx, proj, conv_w, conv_b3, wg, bg, lru_lam, state=None, l=0, z_r=None, states=None):
    has_state = state is not None
    rblk = lambda b: (row0 // seq + b, 0)
    in_specs = [pl.BlockSpec((seq, BR_W), rblk),
                pl.BlockSpec((1, CONV_W, BR_W), lambda b: (l, 0, 0)),
                pl.BlockSpec((1, 1, BR_W), lambda b: (l, 0, 0)),
                pl.BlockSpec((1, BR_W, 4 * BR_W), lambda b: (l, 0, 0)),
                pl.BlockSpec((1, 1, 4 * BR_W), lambda b: (l, 0, 0)),
                pl.BlockSpec((1, 2, BR_W), lambda b: (l, 0, 0)),
                pl.BlockSpec((seq, 256), lambda b: (row0 // seq + b, 1))]
    args = [bx, conv_w, conv_b3, wg, bg, lru_lam, proj]
    zspec = pl.BlockSpec((seq, BR_W), rblk)
    zshape = jax.ShapeDtypeStruct((N_TOK, BR_W), BF16)
    scratch = [pltpu.VMEM((2, seq, BR_W), F32)] * 2 + [pltpu.VMEM((2, seq + 16, BR_W), F32)]
    if has_state:
        in_specs += [pl.BlockSpec((1, 1, 2, BR_W), lambda b: (b, l, 0, 0)), pl.BlockSpec(memory_space=pl.ANY)]
        args += [state, z_r]
        return pl.pallas_call(
            functools.partial(_lru_kernel, seq, True),
            grid=(nseq,), in_specs=in_specs, out_specs=zspec, out_shape=zshape,
            scratch_shapes=scratch, input_output_aliases={8: 0},
            compiler_params=_cparams(("arbitrary",)), name=f"lru_lat_l{l}",
        )(*args)
    aliases = {}
    if states is not None:
        in_specs += [pl.BlockSpec(memory_space=pl.ANY)]
        args += [states]
        aliases = {7: 1}
    return pl.pallas_call(
        functools.partial(_lru_kernel, seq, False),
        grid=(nseq,), in_specs=in_specs,
        out_specs=[zspec, pl.BlockSpec((1, 1, 2, BR_W), lambda b: (b, l, 0, 0))],
        out_shape=[zshape, jax.ShapeDtypeStruct((nseq, DEPTH, 2, BR_W), F32)],
        scratch_shapes=scratch, input_output_aliases=aliases,
        compiler_params=_cparams(("arbitrary",)), name=f"lru_ctx_l{l}",
    )(*args)


def _merge_kernel(layer, final, xa_ref, xb_ref, mod_ref, h_ref, za_ref, zr_ref, zc_ref, zd_ref,
                  wmg_hbm, bmg_ref, wbo_ref, wo_ref, nf_ref, *refs):
    out_refs, (wmg_s, wbo_s, wo_s, stage, sem) = refs[:-5], refs[-5:]
    i = pl.program_id(0)

    def wmg_copy(chunk, n):
        slot = chunk % 2
        return pltpu.make_async_copy(wmg_hbm.at[layer, pl.ds(chunk * W_ROWS, W_ROWS), n, :],
                                     stage.at[slot, n], sem.at[slot, n])

    @pl.when(i == 0)
    def _():
        for n in range(N_BRANCH):
            wmg_copy(0, n).start()

    @pl.when(i < W_STEPS - 1)
    def _():
        for n in range(N_BRANCH):
            wmg_copy(i + 1, n).start()

    @pl.when(i < W_STEPS)
    def _():
        r0 = pl.multiple_of(i * W_ROWS, W_ROWS)
        for n in range(N_BRANCH):
            wmg_copy(i, n).wait()
            wmg_s[pl.ds(r0, W_ROWS), n * D_MODEL:(n + 1) * D_MODEL] = stage[i % 2, n].astype(BF16)
        _cast_rows(i, wbo_ref, wbo_s)
        _cast_rows(i, wo_ref, wo_s)

    @pl.when(i >= W_STEPS)
    def _():
        t = i - W_STEPS
        x = _load_x(t, xa_ref, xb_ref)
        gate = mod_ref[0][:, 2 * D_MODEL:]
        h = h_ref[...]
        zs = [za_ref[...], zr_ref[...], zc_ref[...], zd_ref[...]]
        bmg = bmg_ref[0]
        cols = []
        for c in range(0, D_MODEL, 512):
            acc = None
            for n in range(N_BRANCH):
                g = jax.nn.sigmoid(_dot(h, wmg_s[:, n * D_MODEL + c:n * D_MODEL + c + 512])
                                   + bmg[n:n + 1, c:c + 512])
                term = g * _dot(zs[n], wbo_s[n * BR_W:(n + 1) * BR_W, c:c + 512])
                acc = term if acc is None else acc + term
            cols.append(acc.astype(BF16))
        merged = jnp.concatenate(cols, axis=1)
        xn = x + gate * _dot(merged, wo_s[...])
        if not final:
            out_refs[0][...] = xn
        else:
            y = xn * lax.rsqrt(jnp.mean(xn * xn, axis=-1, keepdims=True) + EPS) * nf_ref[...]

            @pl.when(t < N_CTX_TILES)
            def _():
                out_refs[0][...] = y

            @pl.when(t >= N_CTX_TILES)
            def _():
                out_refs[1][...] = y


def _merge(final, xa, xb, xb_off, mod3, l, h, zs, w_mg, b_mg, w_bo, w_o, norm_f):
    row = lambda i: (_tile(i), 0)
    ctx_blk = lambda i: jnp.minimum(_tile(i), N_CTX_TILES - 1)
    lat_blk = lambda i: jnp.maximum(_tile(i) - N_CTX_TILES, 0)
    if final:
        out_specs = [pl.BlockSpec((TM, D_MODEL), lambda i: (ctx_blk(i), 0)),
                     pl.BlockSpec((TM, D_MODEL), lambda i: (lat_blk(i), 0))]
        out_shape = [jax.ShapeDtypeStruct((N_CTX, D_MODEL), F32), jax.ShapeDtypeStruct((N_LAT, D_MODEL), F32)]
    else:
        out_specs = pl.BlockSpec((TM, D_MODEL), row)
        out_shape = jax.ShapeDtypeStruct((N_TOK, D_MODEL), F32)
    wchunk = lambda width: pl.BlockSpec((1, W_ROWS, width), lambda i: (l, _wchunk(i), 0))
    return pl.pallas_call(
        functools.partial(_merge_kernel, l, final),
        grid=(W_STEPS + N_TILES,),
        in_specs=[
            pl.BlockSpec((TM, D_MODEL), lambda i: (ctx_blk(i), 0)),
            pl.BlockSpec((TM, D_MODEL), lambda i: (lat_blk(i) + xb_off, 0)),
            pl.BlockSpec((1, 1, 3 * D_MODEL), lambda i: (l * 16 + _mod_row(_tile(i)), 0, 0)),
            pl.BlockSpec((TM, D_MODEL), row),
            pl.BlockSpec((TM, BR_W), row), pl.BlockSpec((TM, BR_W), row),
            pl.BlockSpec((TM, BR_W), row), pl.BlockSpec((TM, BR_W), row),
            pl.BlockSpec(memory_space=pl.ANY),
            pl.BlockSpec((1, N_BRANCH, D_MODEL), lambda i: (l, 0, 0)),
            wchunk(D_MODEL),
            wchunk(D_MODEL),
            pl.BlockSpec((1, D_MODEL), lambda i: (0, 0)),
        ],
        out_specs=out_specs,
        out_shape=out_shape,
        scratch_shapes=[pltpu.VMEM((D_MODEL, N_BRANCH * D_MODEL), BF16),
                        pltpu.VMEM((N_BRANCH * BR_W, D_MODEL), BF16),
                        pltpu.VMEM((D_MODEL, D_MODEL), BF16),
                        pltpu.VMEM((2, N_BRANCH, W_ROWS, D_MODEL), F32),
                        pltpu.SemaphoreType.DMA((2, N_BRANCH))],
        compiler_params=_cparams(("arbitrary",)),
        name=f"merge_l{l}",
    )(xa, xb, mod3, h, *zs,
      w_mg, b_mg,
      w_bo.reshape(DEPTH, N_BRANCH * BR_W, D_MODEL), w_o, norm_f.reshape(1, D_MODEL))


def _lru_gate_weights(lru_wa, lru_ba, lru_wx, lru_bx):
    w = jnp.stack([lru_wa[:, 0], lru_wx[:, 0], lru_wa[:, 1], lru_wx[:, 1]], axis=1)
    eye = jnp.eye(B_BLOCKS, dtype=w.dtype)
    dense = w[:, :, :, :, None, :] * eye[None, None, :, None, :, None]
    wg = dense.transpose(0, 2, 3, 1, 4, 5).reshape(DEPTH, BR_W, 4 * BR_W).astype(BF16)
    bg = jnp.stack([lru_ba[:, 0], lru_bx[:, 0], lru_ba[:, 1], lru_bx[:, 1]], axis=1).reshape(DEPTH, 1, 4 * BR_W)
    return wg, bg


def kernel(x_prompt, x_sample, cache_diff_k, cache_diff_v, cache_na_k, cache_na_v, cache_swa_k, cache_swa_v,
           state_lru, c, c_ctx, norm_g, w_ada, b_ada, w_in, diff_lambda, diff_norm_g, conv_w, conv_b,
           lru_wa, lru_ba, lru_wx, lru_bx, lru_lam, na_rpb, swa_sink, w_mg, b_mg, w_bo, w_o, norm_f):
    tab_a_np, _ = _rope_tables(A_QK)
    tab_d_np, _ = _rope_tables(HD)
    tab_a, tab_d = jnp.asarray(tab_a_np), jnp.asarray(tab_d_np)

    cvecs = jnp.concatenate([c_ctx[None, :], c, jnp.zeros((16 - 1 - DEC_BATCH, D_MODEL), F32)], axis=0)
    mod3 = _modulation(cvecs, w_ada, b_ada).reshape(DEPTH * 16, 1, 3 * D_MODEL)
    bias = _na_bias(na_rpb)
    wg, bg = _lru_gate_weights(lru_wa, lru_ba, lru_wx, lru_bx)
    norm_g3 = norm_g.reshape(DEPTH, 1, D_MODEL)
    dg3 = diff_norm_g.reshape(DEPTH, 1, HD)
    dg4 = jnp.tile(diff_norm_g, (1, A_HEADS)).reshape(DEPTH, 1, BR_W)
    sink3 = swa_sink.reshape(DEPTH, 1, D_HEADS)
    conv_b3 = conv_b.reshape(DEPTH, 1, BR_W)
    past = [jnp.swapaxes(t, -1, -2) for t in
            (cache_diff_k, cache_diff_v, cache_na_k, cache_na_v, cache_swa_k, cache_swa_v)]

    xa = x_prompt.reshape(N_CTX, D_MODEL)
    xb = x_sample.reshape(N_LAT, D_MODEL)
    xb_off = 0
    caches = states = None
    y_p = y_s = None
    for l in range(DEPTH):
        h, proj, bx, caches = _inproj(xa, xb, xb_off, mod3, l, norm_g3, w_in, tab_a, tab_d, caches)

        z_a, z_c, z_d = _ctx_attn(proj, l, diff_lambda, dg4, sink3)
        z_a = _lat_diff(proj, past[0], past[1], l, diff_lambda, dg3, z_a)
        z_c = _lat_na(proj, past[2], past[3], bias, l, z_c)
        z_d = _lat_swa(proj, past[4], past[5], sink3, l, z_d)

        lru_args = (bx, proj, conv_w, conv_b3, wg, bg, lru_lam)
        z_r, states = _lru(SEQ, BATCH, 0, *lru_args, l=l, states=states)
        z_r = _lru(DEC_SEQ, DEC_BATCH, N_CTX, *lru_args, state=state_lru, l=l, z_r=z_r)

        final = l == DEPTH - 1
        out = _merge(final, xa, xb, xb_off, mod3, l, h, (z_a, z_r, z_c, z_d), w_mg, b_mg, w_bo, w_o, norm_f)
        if final:
            y_p, y_s = out
        else:
            xa = xb = out
            xb_off = N_CTX_TILES

    new_caches = [jnp.swapaxes(t, -1, -2) for t in caches]
    return (y_p.reshape(BATCH, SEQ, D_MODEL), y_s.reshape(DEC_BATCH, DEC_SEQ, D_MODEL), *new_caches, states)
```

```python
import functools
import math

import numpy as np
import jax
import jax.numpy as jnp
from jax import lax
from jax.experimental import pallas as pl
from jax.experimental.pallas import tpu as pltpu

F32 = jnp.float32
BF16 = jnp.bfloat16

D_MODEL = 1024
BATCH = 16
SEQ = 256
DEPTH = 2
DEC_BATCH = 8
DEC_SEQ = 1024
PAST_LEN = 512
GRID_W = 64
N_BRANCH = 4
BR_W = D_MODEL // 4
HD = 64
A_HEADS = BR_W // HD
A_QK = HD // 2
B_BLOCKS = 4
B_BLK = BR_W // B_BLOCKS
CONV_W = 4
LRU_C = 8.0
C_HEADS = BR_W // HD
NA_ROWS = 8
NA_COLS = 16
D_HEADS = BR_W // HD
D_KV = 2
D_GROUP = D_HEADS // D_KV
WIN = 128
ROPE_BASE = 10000.0
EPS = 1e-6
NEG = -1e30

N_CTX = BATCH * SEQ
N_LAT = DEC_BATCH * DEC_SEQ
N_TOK = N_CTX + N_LAT
GRID_ROWS = DEC_SEQ // GRID_W

TM = 512
N_CTX_TILES = N_CTX // TM
N_TILES = N_TOK // TM
LAT_TILES_PER_SEQ = DEC_SEQ // TM

PROJ_W = 3328
COL_G = 0
COL_AQ = 1024
COL_AK = 1280
COL_AV = 1536
COL_CQ = 1792
COL_CK = 2048
COL_CV = 2304
COL_DQ = 2560
COL_DK = 2816
COL_DV = 3072
IN_W = 3328
WS_W = IN_W + 2 * D_KV * HD

NA_HALF_Q = DEC_SEQ // 2
NA_KROWS = 12
NA_KSPAN = NA_KROWS * GRID_W
NA_HALF_KSTART = 4 * GRID_W

SWA_QB = 128
SWA_SPAN = SWA_QB + 2 * WIN

LOG2E = math.log2(math.e)
QA_SCALE = A_QK ** -0.5 * LOG2E
QH_SCALE = HD ** -0.5 * LOG2E

VMEM_LIMIT = 56 * 1024 * 1024


def _cparams(sem):
    return pltpu.CompilerParams(dimension_semantics=sem, vmem_limit_bytes=VMEM_LIMIT)


def _resident(shape, index_map):
    return pl.BlockSpec(shape, index_map, pipeline_mode=pl.Buffered(1))


def _dot(a, b):
    return jnp.dot(a, b, preferred_element_type=F32)


def _dot_nt(a, b):
    return lax.dot_general(a, b, (((1,), (1,)), ((), ())), preferred_element_type=F32)


def _silu(x):
    return x * jax.nn.sigmoid(x)


def _rope_tables(d):
    half = d // 2
    quarter = half // 2
    lane = np.arange(128)
    q = lane % d
    use_col = (q >= half)
    i = (q % half) % quarter
    first = (q % half) < quarter
    inv = (ROPE_BASE ** (-np.arange(quarter, dtype=np.float32) / np.float32(quarter))).astype(np.float32)
    t = np.arange(DEC_SEQ)
    pos = np.where(use_col[None, :], (t % GRID_W)[:, None], (t // GRID_W)[:, None]).astype(np.float32)
    ang = (pos * inv[i][None, :]).astype(np.float32)
    cos, sin = np.cos(ang), np.sin(ang)
    c = np.concatenate([np.ones((TM, 128), np.float32), cos], axis=0)
    s1 = np.concatenate([np.zeros((TM, 128), np.float32), np.where(first[None, :], -sin, 0.0)], axis=0)
    s2 = np.concatenate([np.zeros((TM, 128), np.float32), np.where(first[None, :], 0.0, sin)], axis=0)
    return np.stack([c, s1, s2]).astype(np.float32), quarter


def _rope128(x, tab_ref, shift):
    return (x * tab_ref[0] + pltpu.roll(x, 128 - shift, 1) * tab_ref[1]
            + pltpu.roll(x, shift, 1) * tab_ref[2])


def _rope(x, tab_ref, shift):
    w = x.shape[1]
    return jnp.concatenate([_rope128(x[:, c:c + 128], tab_ref, shift) for c in range(0, w, 128)], axis=1)


def _mod_kernel(c_ref, w_ref, b_ref, o_ref):
    c = c_ref[...]
    o_ref[0] = _dot(_silu(c).astype(BF16), w_ref[0].astype(BF16)) + b_ref[0]


def _modulation(cvecs, w_ada, b_ada):
    nb = 3 * D_MODEL // 768
    return pl.pallas_call(
        _mod_kernel,
        grid=(DEPTH, nb),
        in_specs=[pl.BlockSpec((16, D_MODEL), lambda l, j: (0, 0)),
                  pl.BlockSpec((1, D_MODEL, 768), lambda l, j: (l, 0, j)),
                  pl.BlockSpec((1, 1, 768), lambda l, j: (l, 0, j))],
        out_specs=pl.BlockSpec((1, 16, 768), lambda l, j: (l, 0, j)),
        out_shape=jax.ShapeDtypeStruct((DEPTH, 16, 3 * D_MODEL), F32),
        compiler_params=_cparams(("arbitrary", "arbitrary")),
        name="modulation",
    )(cvecs, w_ada, b_ada.reshape(DEPTH, 1, 3 * D_MODEL))


W_STEPS = 8
W_ROWS = D_MODEL // W_STEPS


def _tile(i):
    return jnp.maximum(i - W_STEPS, 0)


def _wchunk(i):
    return jnp.minimum(i, W_STEPS - 1)


def _mod_row(t):
    return jnp.where(t < N_CTX_TILES, 0, 1 + (t - N_CTX_TILES) // LAT_TILES_PER_SEQ)


def _rope_blk(t):
    return jnp.where(t < N_CTX_TILES, 0, 1 + (t - N_CTX_TILES) % LAT_TILES_PER_SEQ)


def _load_x(t, xa_ref, xb_ref):
    return jnp.where(t < N_CTX_TILES, xa_ref[...], xb_ref[...])


def _cast_rows(i, src_ref, dst_ref):
    r0 = pl.multiple_of(i * W_ROWS, W_ROWS)
    dst_ref[pl.ds(r0, W_ROWS), :] = src_ref[0].astype(BF16)


WI_G, WI_AQ, WI_AK, WI_AV, WI_BX, WI_CQ, WI_CKV, WI_DQ, WI_DKV = 0, 1024, 1280, 1536, 1792, 2048, 2304, 2816, 3072
SEQ_PER_TILE = TM // SEQ


def _store_heads(ref, val, n_heads, width):
    vt = val.T
    for s in range(SEQ_PER_TILE):
        for h in range(n_heads):
            ref[s, 0, h] = vt[h * width:(h + 1) * width, s * SEQ:(s + 1) * SEQ]


def _inproj_kernel(n_alias, xa_ref, xb_ref, mod_ref, g_ref, w_ref, ta_ref, td_ref, *refs):
    h_ref, proj_ref, bx_ref, dk_ref, dv_ref, nk_ref, nv_ref, sk_ref, sv_ref, w_s = refs[n_alias:]
    i = pl.program_id(0)

    @pl.when(i < W_STEPS)
    def _():
        r0 = pl.multiple_of(i * W_ROWS, W_ROWS)
        wb = w_ref[0].astype(BF16)
        w_s[pl.ds(r0, W_ROWS), 0:WI_DKV] = wb[:, 0:WI_DKV]
        for n, src in enumerate((0, 0, 1, 1, 2, 2, 3, 3)):
            w_s[pl.ds(r0, W_ROWS), WI_DKV + n * HD:WI_DKV + (n + 1) * HD] = \
                wb[:, WI_DKV + src * HD:WI_DKV + (src + 1) * HD]

    @pl.when(i >= W_STEPS)
    def _():
        t = i - W_STEPS
        is_ctx = t < N_CTX_TILES
        x = _load_x(t, xa_ref, xb_ref)
        mod = mod_ref[0]
        shift, scale = mod[:, :D_MODEL], mod[:, D_MODEL:2 * D_MODEL]
        xn = x * lax.rsqrt(jnp.mean(x * x, axis=-1, keepdims=True) + EPS) * g_ref[0]
        hb =(xn * (1.0 + scale) + shift).astype(BF16)
        h_ref[...] = hb

        def mm(c0, width):
            return _dot(hb, w_s[:, c0:c0 + width])

        gs = [mm(WI_G + c, 512) for c in range(0, 1024, 512)]
        aq, ak, av = mm(WI_AQ, 256), mm(WI_AK, 256), mm(WI_AV, 256)
        cq, ckv = mm(WI_CQ, 256), mm(WI_CKV, 512)
        dq, dk4, dv4 = mm(WI_DQ, 256), mm(WI_DKV, 256), mm(WI_DKV + 256, 256)
        bx_ref[...] = mm(WI_BX, 256)

        for n, g in enumerate(gs):
            proj_ref[:, COL_G + n * 512:COL_G + (n + 1) * 512] = _silu(g).astype(BF16)
        proj_ref[:, COL_AQ:COL_AQ + 256] = _rope(aq * QA_SCALE, ta_ref, A_QK // 4).astype(BF16)
        proj_ref[:, COL_AK:COL_AK + 256] = _rope(ak, ta_ref, A_QK // 4).astype(BF16)
        proj_ref[:, COL_AV:COL_AV + 256] = av.astype(BF16)
        proj_ref[:, COL_CQ:COL_CQ + 256] = (cq * QH_SCALE).astype(BF16)
        proj_ref[:, COL_CK:COL_CK + 512] = ckv.astype(BF16)
        proj_ref[:, COL_DQ:COL_DQ + 256] = _rope(dq * QH_SCALE, td_ref, HD // 4).astype(BF16)
        proj_ref[:, COL_DK:COL_DK + 256] = _rope(dk4, td_ref, HD // 4).astype(BF16)
        proj_ref[:, COL_DV:COL_DV + 256] = dv4.astype(BF16)

        @pl.when(is_ctx)
        def _():
            akt = ak.T
            for s in range(SEQ_PER_TILE):
                for h in range(A_HEADS):
                    for m in range(2):
                        c0 = h * HD + m * A_QK
                        dk_ref[s, 0, h, m] = akt[c0:c0 + A_QK, s * SEQ:(s + 1) * SEQ]
            _store_heads(dv_ref, av, A_HEADS, HD)
            _store_heads(nk_ref, ckv[:, :256], C_HEADS, HD)
            _store_heads(nv_ref, ckv[:, 256:], C_HEADS, HD)
            for ref, val in ((sk_ref, dk4), (sv_ref, dv4)):
                vt = val.T
                for s in range(SEQ_PER_TILE):
                    for kv in range(D_KV):
                        ref[s, 0, kv] = vt[kv * D_GROUP * HD:kv * D_GROUP * HD + HD, s * SEQ:(s + 1) * SEQ]


def _cache_shapes():
    hs = lambda n: (BATCH, DEPTH, n, HD, SEQ)
    return [(BATCH, DEPTH, A_HEADS, 2, A_QK, SEQ), hs(A_HEADS), hs(C_HEADS), hs(C_HEADS), hs(D_KV), hs(D_KV)]


def _inproj(xa, xb, xb_off, mod3, l, norm_g, w_in, tab_a, tab_d, caches):
    row = lambda i: (_tile(i), 0)
    ctx_blk = lambda i: jnp.minimum(_tile(i), N_CTX_TILES - 1)

    def cache_spec(shape):
        blk = (SEQ_PER_TILE, 1) + shape[2:]
        nz = len(shape) - 2
        return pl.BlockSpec(blk, lambda i: (ctx_blk(i), l) + (0,) * nz)

    cshapes = _cache_shapes()
    aliases = {} if caches is None else {7 + k: 3 + k for k in range(6)}
    extra_specs = [] if caches is None else [pl.BlockSpec(memory_space=pl.ANY)] * 6
    extra_args = [] if caches is None else list(caches)
    outs = pl.pallas_call(
        functools.partial(_inproj_kernel, len(extra_args)),
        grid=(W_STEPS + N_TILES,),
        in_specs=[
            pl.BlockSpec((TM, D_MODEL), lambda i: (ctx_blk(i), 0)),
            pl.BlockSpec((TM, D_MODEL), lambda i: (jnp.maximum(_tile(i) - N_CTX_TILES, 0) + xb_off, 0)),
            pl.BlockSpec((1, 1, 3 * D_MODEL), lambda i: (l * 16 + _mod_row(_tile(i)), 0, 0)),
            pl.BlockSpec((1, 1, D_MODEL), lambda i: (l, 0, 0)),
            pl.BlockSpec((1, W_ROWS, IN_W), lambda i: (l, _wchunk(i), 0)),
            pl.BlockSpec((3, TM, 128), lambda i: (0, _rope_blk(_tile(i)), 0)),
            pl.BlockSpec((3, TM, 128), lambda i: (0, _rope_blk(_tile(i)), 0)),
        ] + extra_specs,
        out_specs=[
            pl.BlockSpec((TM, D_MODEL), row),
            pl.BlockSpec((TM, PROJ_W), row),
            pl.BlockSpec((TM, BR_W), row),
        ] + [cache_spec(s) for s in cshapes],
        out_shape=[
            jax.ShapeDtypeStruct((N_TOK, D_MODEL), BF16),
            jax.ShapeDtypeStruct((N_TOK, PROJ_W), BF16),
            jax.ShapeDtypeStruct((N_TOK, BR_W), F32),
        ] + [jax.ShapeDtypeStruct(s, F32) for s in cshapes],
        scratch_shapes=[pltpu.VMEM((D_MODEL, WS_W), BF16)],
        input_output_aliases=aliases,
        compiler_params=_cparams(("arbitrary",)),
        name=f"inproj_l{l}",
    )(xa, xb, mod3, norm_g, w_in, tab_a, tab_d, *extra_args)
    return outs[0], outs[1], outs[2], tuple(outs[3:])


def _diff_lambda(lam_ref, lam_init):
    lv = lam_ref[0]
    s1 = jnp.sum(lv[0:1] * lv[1:2], axis=-1, keepdims=True)
    s2 = jnp.sum(lv[2:3] * lv[3:4], axis=-1, keepdims=True)
    return jnp.exp(s1) - jnp.exp(s2) + lam_init


def _diff_norm(o, dg_ref, lam_init):
    y = o * lax.rsqrt(jnp.mean(o * o, axis=-1, keepdims=True) + EPS) * dg_ref[0]
    return y * (1.0 - lam_init)


def _exp2_parts(parts, extra=None):
    m = functools.reduce(jnp.maximum, [jnp.max(s, axis=-1, keepdims=True) for s in parts])
    if extra is not None:
        m = jnp.maximum(m, extra)
    return [jnp.exp2(s - m).astype(BF16) for s in parts], m


def _pipelined(n, score_fn, pv_fn):
    outs = []
    nxt = score_fn(0)
    for c in range(n):
        cur, nxt = nxt, (score_fn(c + 1) if c + 1 < n else None)
        outs.append(pv_fn(c, cur))
    return outs


def _with_ones(v):
    return jnp.concatenate([v, jnp.ones_like(v)], axis=1)


def _with_ones_t(vt):
    return jnp.concatenate([vt, jnp.ones((16, vt.shape[1]), vt.dtype)], axis=0)


def _stack_group(dq, kv, rows):
    return jnp.concatenate([dq[:, (kv * D_GROUP + g) * HD:(kv * D_GROUP + g + 1) * HD]
                            for g in range(D_GROUP)], axis=0)


def _sink_col(sink2, kv, rows):
    r = lax.broadcasted_iota(jnp.int32, (D_GROUP * rows, 1), 0)
    return jnp.where(r < rows, sink2[:, kv * D_GROUP:kv * D_GROUP + 1], sink2[:, kv * D_GROUP + 1:kv * D_GROUP + 2])


CTX_PER_STEP = 2


def _lane_masks(n):
    lane = lax.broadcasted_iota(jnp.int32, (1, BR_W), 1)
    w = BR_W // n
    ms = [jnp.where((lane >= p * w) & (lane < (p + 1) * w), 1.0, 0.0).astype(BF16) for p in range(n)]
    return ms, [1.0 - m for m in ms]


def _swap_halves(p):
    return jnp.concatenate([p[:, BR_W // 2:], p[:, :BR_W // 2]], axis=1)


def _pick_heads(os):
    grp = lax.broadcasted_iota(jnp.int32, os[0].shape, 1) // HD
    acc = os[0]
    for h in range(1, len(os)):
        acc = jnp.where(grp == h, os[h], acc)
    return acc


def _head_rms(o, dg4_ref, lam_init):
    r = lax.broadcasted_iota(jnp.int32, (BR_W, BR_W), 0) // HD
    c = lax.broadcasted_iota(jnp.int32, (BR_W, BR_W), 1) // HD
    ss = _dot((o * o).astype(BF16), jnp.where(r == c, 1.0, 0.0).astype(BF16))
    return o * lax.rsqrt(ss * (1.0 / HD) + EPS) * dg4_ref[0] * (1.0 - lam_init)


def _ctx_attn_kernel(n_alias, lam_init, aq_ref, ak_ref, av_ref, cq_ref, ck_ref, cv_ref, dq_ref, dk_ref, dv_ref,
                     ga_ref, gc_ref, gd_ref, lam_ref, dg4_ref, sink_ref,
                     bx_ref, gr_ref, cw_ref, cb_ref, wg_ref, bg_ref, llam_ref, *refs):
    za_ref, zc_ref, zd_ref, zr_ref, fin_ref, xp_s = refs[n_alias:]

    for s in range(CTX_PER_STEP):
        rs = slice(s * SEQ, (s + 1) * SEQ)
        xc = _lru_conv(bx_ref[rs, :], cw_ref[0], cb_ref[0], xp_s)
        pre = _dot(xc.astype(BF16), wg_ref[0]) + bg_ref[0]
        h0 = jnp.zeros((1, BR_W), F32)
        yf, hf = _lru_scan(0, xc, pre[:, :2 * BR_W], llam_ref[0, 0:1, :], h0)
        yb, hb = _lru_scan(1, xc, pre[:, 2 * BR_W:], llam_ref[0, 1:2, :], h0)
        zr_ref[rs, :] = ((yf + yb) * gr_ref[rs, :].astype(F32)).astype(BF16)
        fin_ref[s, 0, 0:1, :] = hf
        fin_ref[s, 0, 1:2, :] = hb

    lam = _diff_lambda(lam_ref, lam_init)
    sink2 = sink_ref[0] * LOG2E
    m_qk, _ = _lane_masks(2 * A_HEADS)
    m_hd, o_hd = _lane_masks(C_HEADS)
    seqs = [slice(s * SEQ, (s + 1) * SEQ) for s in range(CTX_PER_STEP)]

    sa, sc, sd = [], [], []
    for rs in seqs:
        aq, cq, dq = aq_ref[rs, :], cq_ref[rs, :], dq_ref[rs, :]
        ak, ck, dk = ak_ref[rs, :], ck_ref[rs, :], dk_ref[rs, :]
        sa.append([_dot_nt(aq * m, ak) for m in m_qk])
        sc.append([_dot_nt(cq * m, ck) for m in m_hd])
        sd.append([_dot_nt(dq * m, dk) for m in m_hd])

    ea = [[_exp2_parts([s])[0][0] for s in ss] for ss in sa]
    ec = [[_exp2_parts([s])[0][0] for s in ss] for ss in sc]
    ed = [[_exp2_parts([s], extra=sink2[:, j:j + 1]) for j, s in enumerate(ss)] for ss in sd]

    for i, rs in enumerate(seqs):
        av, cv, dv = av_ref[rs, :], cv_ref[rs, :], dv_ref[rs, :]
        os = []
        for h in range(A_HEADS):
            w = av * m_hd[h] + o_hd[h]
            p0, p1 = _dot(ea[i][2 * h], w), _dot(ea[i][2 * h + 1], w)
            os.append(p0 * (1.0 / _swap_halves(p0)) - p1 * (lam / _swap_halves(p1)))
        za_ref[rs, :] = (_head_rms(_pick_heads(os), dg4_ref, lam_init) * ga_ref[rs, :].astype(F32)).astype(BF16)

        os = []
        for h in range(C_HEADS):
            p = _dot(ec[i][h], cv * m_hd[h] + o_hd[h])
            os.append(p * (1.0 / _swap_halves(p)))
        zc_ref[rs, :] = (_pick_heads(os) * gc_ref[rs, :].astype(F32)).astype(BF16)

        os = []
        for j in range(D_HEADS):
            (e,), m = ed[i][j]
            p = _dot(e, dv * m_hd[j] + o_hd[j])
            os.append(p * (1.0 / (_swap_halves(p) + jnp.exp2(sink2[:, j:j + 1] - m))))
        zd_ref[rs, :] = (_pick_heads(os) * gd_ref[rs, :].astype(F32)).astype(BF16)


def _ctx_attn(proj, bx, l, diff_lambda, dg4, sink3, lru_params, states):
    lam_init = 0.8 - 0.6 * math.exp(-0.3 * l)
    colblk = lambda cb: pl.BlockSpec((CTX_PER_STEP * SEQ, 256), lambda b: (b, cb))
    zspec = pl.BlockSpec((CTX_PER_STEP * SEQ, BR_W), lambda b: (b, 0))
    zshape = jax.ShapeDtypeStruct((N_TOK, BR_W), BF16)
    extra = [] if states is None else [states]
    return pl.pallas_call(
        functools.partial(_ctx_attn_kernel, len(extra), lam_init),
        grid=(BATCH // CTX_PER_STEP,),
        in_specs=[colblk(COL_AQ // 256), colblk(COL_AK // 256), colblk(COL_AV // 256),
                  colblk(COL_CQ // 256), colblk(COL_CK // 256), colblk(COL_CV // 256),
                  colblk(COL_DQ // 256), colblk(COL_DK // 256), colblk(COL_DV // 256),
                  colblk(0), colblk(2), colblk(3),
                  pl.BlockSpec((1, 4, A_QK), lambda b: (l, 0, 0)),
                  pl.BlockSpec((1, 1, BR_W), lambda b: (l, 0, 0)),
                  pl.BlockSpec((1, 1, D_HEADS), lambda b: (l, 0, 0)),
                  pl.BlockSpec((CTX_PER_STEP * SEQ, BR_W), lambda b: (b, 0)), colblk(1)]
                 + _lru_specs(l, lambda b: (l, 0, 0)) + [pl.BlockSpec(memory_space=pl.ANY)] * len(extra),
        out_specs=[zspec, zspec, zspec, zspec,
                   pl.BlockSpec((CTX_PER_STEP, 1, 2, BR_W), lambda b: (b, l, 0, 0))],
        out_shape=[zshape, zshape, zshape, jax.ShapeDtypeStruct((N_CTX, BR_W), BF16),
                   jax.ShapeDtypeStruct((BATCH, DEPTH, 2, BR_W), F32)],
        scratch_shapes=[pltpu.VMEM((SEQ + 16, BR_W), F32)],
        input_output_aliases={} if states is None else {22: 4},
        compiler_params=_cparams(("arbitrary",)),
        name=f"ctx_attn_l{l}",
    )(*([proj] * 12), diff_lambda, dg4, sink3, bx, proj, *lru_params, *extra)


LA_QB = 256


def _lat_diff_kernel(lam_init, aq_ref, ak_ref, av_ref, ck_ref, cv_ref, ga_ref, lam_ref, dg_ref, zin_ref, za_ref):
    del zin_ref
    lam = _diff_lambda(lam_ref, lam_init)
    aq = aq_ref[...]
    va_ctx = [_with_ones_t(cv_ref[0, 0, h].astype(BF16)) for h in range(A_HEADS)]
    va_loc = [_with_ones(av_ref[:, h * HD:(h + 1) * HD]) for h in range(A_HEADS)]

    def scores(c):
        h, m = divmod(c, 2)
        lo = h * HD + m * A_QK
        q = aq[:, lo:lo + A_QK]
        return [_dot(q, ck_ref[0, 0, h, m].astype(BF16)),
                _dot_nt(q, ak_ref[:, lo:lo + A_QK])]

    def pv(c, s):
        (e_ctx, e_loc), _ = _exp2_parts(s)
        p_ctx, p_loc = _dot_nt(e_ctx, va_ctx[c // 2]), _dot(e_loc, va_loc[c // 2])
        return p_ctx[:, :HD] + p_loc[:, :HD], p_ctx[:, HD:HD + 1] + p_loc[:, HD:HD + 1]

    ss = [scores(c) for c in range(2 * A_HEADS)]
    nd = [pv(c, s) for c, s in enumerate(ss)]
    outs = []
    for h in range(A_HEADS):
        (n0, d0), (n1, d1) = nd[2 * h], nd[2 * h + 1]
        outs.append(_diff_norm(n0 * (1.0 / d0) - n1 * (lam / d1), dg_ref, lam_init))
    za_ref[...] = (jnp.concatenate(outs, axis=1) * ga_ref[...].astype(F32)).astype(BF16)


def _lat_diff(proj, cache_k, cache_v, l, diff_lambda, dg3, z_a):
    lam_init = 0.8 - 0.6 * math.exp(-0.3 * l)
    nq = DEC_SEQ // LA_QB
    qrow = lambda b, j: N_CTX // LA_QB + b * nq + j
    srow = lambda b, j: N_CTX // DEC_SEQ + b
    return pl.pallas_call(
        functools.partial(_lat_diff_kernel, lam_init),
        grid=(DEC_BATCH, nq),
        in_specs=[pl.BlockSpec((LA_QB, 256), lambda b, j: (qrow(b, j), COL_AQ // 256)),
                  pl.BlockSpec((DEC_SEQ, 256), lambda b, j: (srow(b, j), COL_AK // 256)),
                  pl.BlockSpec((DEC_SEQ, 256), lambda b, j: (srow(b, j), COL_AV // 256)),
                  pl.BlockSpec((1, 1, A_HEADS, 2, A_QK, PAST_LEN), lambda b, j: (b, l, 0, 0, 0, 0)),
                  pl.BlockSpec((1, 1, A_HEADS, HD, PAST_LEN), lambda b, j: (b, l, 0, 0, 0)),
                  pl.BlockSpec((LA_QB, 256), lambda b, j: (qrow(b, j), 0)),
                  pl.BlockSpec((1, 4, A_QK), lambda b, j: (l, 0, 0)),
                  pl.BlockSpec((1, 1, HD), lambda b, j: (l, 0, 0)),
                  pl.BlockSpec(memory_space=pl.ANY)],
        out_specs=pl.BlockSpec((LA_QB, BR_W), lambda b, j: (qrow(b, j), 0)),
        out_shape=jax.ShapeDtypeStruct((N_TOK, BR_W), BF16),
        input_output_aliases={8: 0},
        compiler_params=_cparams(("arbitrary", "arbitrary")),
        name=f"lat_diff_l{l}",
    )(proj, proj, proj, cache_k, cache_v, proj, diff_lambda, dg3, z_a)


def _na_bias_kernel(rpb_ref, o_ref):
    l, h = pl.program_id(0), pl.program_id(1)
    base = (l * C_HEADS + h) * ((2 * NA_ROWS - 1) * (2 * NA_COLS - 1))
    qc = lax.broadcasted_iota(jnp.int32, (GRID_W, GRID_W), 0)
    kc = lax.broadcasted_iota(jnp.int32, (GRID_W, GRID_W), 1)
    cs = jnp.clip(qc - NA_COLS // 2, 0, GRID_W - NA_COLS)
    ok = (kc >= cs) & (kc < cs + NA_COLS)
    dcol = kc - qc + (NA_COLS - 1)
    neg = jnp.full((GRID_W, GRID_W), NEG, F32)
    tabs = []
    for d in range(2 * NA_ROWS - 1):
        t = neg
        for dc in range(2 * NA_COLS - 1):
            t = jnp.where(ok & (dcol == dc), rpb_ref[base + d * (2 * NA_COLS - 1) + dc] * LOG2E, t)
        tabs.append(t)
    for half in range(2):
        for rq in range(NA_ROWS):
            r = half * NA_ROWS + rq
            rs = min(max(r - NA_ROWS // 2, 0), GRID_ROWS - NA_ROWS)
            blks = []
            for kr in range(NA_KROWS):
                kabs = half * 4 + kr
                blks.append(tabs[kabs - r + NA_ROWS - 1] if rs <= kabs < rs + NA_ROWS else neg)
            o_ref[0, 0, half, rq * GRID_W:(rq + 1) * GRID_W, :] = jnp.concatenate(blks, axis=1)


def _na_bias(na_rpb):
    return pl.pallas_call(
        _na_bias_kernel,
        grid=(DEPTH, C_HEADS),
        in_specs=[pl.BlockSpec(memory_space=pltpu.SMEM)],
        out_specs=pl.BlockSpec((1, 1, 2, NA_HALF_Q, NA_KSPAN), lambda l, h: (l, h, 0, 0, 0)),
        out_shape=jax.ShapeDtypeStruct((DEPTH, C_HEADS, 2, NA_HALF_Q, NA_KSPAN), F32),
        compiler_params=_cparams(("arbitrary", "arbitrary")),
        name="na_bias",
    )(na_rpb.reshape(-1))


def _lat_na_kernel(half, cq_ref, ck_ref, cv_ref, kc_ref, vc_ref, bias_ref, gc_ref,
                   bx_ref, cw_ref, cb_ref, wg_ref, bg_ref, llam_ref, st_ref, zin_ref, zc_ref, y_ref, xp_s):
    del zin_ref
    xc = _lru_conv(bx_ref[...], cw_ref[0], cb_ref[0], xp_s)
    pre = _dot(xc.astype(BF16), wg_ref[0, :, half * 2 * BR_W:(half + 1) * 2 * BR_W]) \
        + bg_ref[0, :, half * 2 * BR_W:(half + 1) * 2 * BR_W]
    y_ref[...], _ = _lru_scan(half, xc, pre, llam_ref[0, half:half + 1, :], st_ref[0, 0, half:half + 1, :])

    k0 = half * NA_HALF_KSTART
    cq = cq_ref[...]
    kl_all = ck_ref[k0:k0 + NA_KSPAN, :]
    vl_all = cv_ref[k0:k0 + NA_KSPAN, :]

    def scores(h):
        q = cq[:, h * HD:(h + 1) * HD]
        return [_dot(q, kc_ref[0, 0, h].astype(BF16)),
                _dot_nt(q, kl_all[:, h * HD:(h + 1) * HD]) + bias_ref[0, h, 0]]

    def pv(h, s):
        (e_ctx, e_loc), _ = _exp2_parts(s)
        p_ctx = _dot_nt(e_ctx, _with_ones_t(vc_ref[0, 0, h].astype(BF16)))
        p_loc = _dot(e_loc, _with_ones(vl_all[:, h * HD:(h + 1) * HD]))
        return (p_ctx[:, :HD] + p_loc[:, :HD]) * (1.0 / (p_ctx[:, HD:HD + 1] + p_loc[:, HD:HD + 1]))

    ss = [scores(h) for h in range(C_HEADS)]
    outs = [pv(h, s) for h, s in enumerate(ss)]
    zc_ref[...] = (jnp.concatenate(outs, axis=1) * gc_ref[...].astype(F32)).astype(BF16)


def _lat_na(half, proj, bx, cache_k, cache_v, bias, l, lru_params, state_lru, z_c):
    qrow = lambda b: N_CTX // NA_HALF_Q + b * 2 + half
    srow = lambda b: N_CTX // DEC_SEQ + b
    return pl.pallas_call(
        functools.partial(_lat_na_kernel, half),
        grid=(DEC_BATCH,),
        in_specs=[pl.BlockSpec((NA_HALF_Q, 256), lambda b: (qrow(b), COL_CQ // 256)),
                  pl.BlockSpec((DEC_SEQ, 256), lambda b: (srow(b), COL_CK // 256)),
                  pl.BlockSpec((DEC_SEQ, 256), lambda b: (srow(b), COL_CV // 256)),
                  pl.BlockSpec((1, 1, C_HEADS, HD, PAST_LEN), lambda b: (b, l, 0, 0, 0)),
                  pl.BlockSpec((1, 1, C_HEADS, HD, PAST_LEN), lambda b: (b, l, 0, 0, 0)),
                  pl.BlockSpec((1, C_HEADS, 1, NA_HALF_Q, NA_KSPAN), lambda b: (l, 0, half, 0, 0)),
                  pl.BlockSpec((NA_HALF_Q, 256), lambda b: (qrow(b), 2)),
                  pl.BlockSpec((DEC_SEQ, BR_W), lambda b: (srow(b), 0))]
                 + _lru_specs(l, lambda b: (l, 0, 0))
                 + [pl.BlockSpec((1, 1, 2, BR_W), lambda b: (b, l, 0, 0)), pl.BlockSpec(memory_space=pl.ANY)],
        out_specs=[pl.BlockSpec((NA_HALF_Q, BR_W), lambda b: (qrow(b), 0)),
                   pl.BlockSpec((DEC_SEQ, BR_W), lambda b: (b, 0))],
        out_shape=[jax.ShapeDtypeStruct((N_TOK, BR_W), BF16), jax.ShapeDtypeStruct((N_LAT, BR_W), F32)],
        scratch_shapes=[pltpu.VMEM((DEC_SEQ + 16, BR_W), F32)],
        input_output_aliases={14: 0},
        compiler_params=_cparams(("arbitrary",)),
        name=f"lat_na{half}_l{l}",
    )(proj, proj, proj, cache_k, cache_v, bias, proj, bx, *lru_params, state_lru, z_c)


def _lat_swa_kernel(dq_ref, dk_ref, dv_ref, kc_ref, vc_ref, sink_ref, gd_ref, zin_ref, zd_ref):
    del zin_ref
    nq = DEC_SEQ // SWA_QB
    rows = D_GROUP * SWA_QB
    dq, dk, dv = dq_ref[...], dk_ref[...], dv_ref[...]
    sink2 = sink_ref[0] * LOG2E
    m_hd, _ = _lane_masks(D_HEADS)
    starts = [min(max(j * SWA_QB - WIN, 0), DEC_SEQ - SWA_SPAN) for j in range(nq)]
    qi = lax.broadcasted_iota(jnp.int32, (rows, SWA_SPAN), 0) % SWA_QB
    ki = lax.broadcasted_iota(jnp.int32, (rows, SWA_SPAN), 1)
    masks = {off: jnp.abs(qi + off - ki) <= WIN for off in sorted({j * SWA_QB - s for j, s in enumerate(starts)})}
    kc_t = jnp.concatenate([kc_ref[0, 0, kv] for kv in range(D_KV) for _ in range(D_GROUP)], axis=0).astype(BF16)
    vc_t = jnp.concatenate([vc_ref[0, 0, kv] for kv in range(D_KV) for _ in range(D_GROUP)], axis=0).astype(BF16)
    feat = lax.broadcasted_iota(jnp.int32, (BR_W, 1), 0) // (D_GROUP * HD)

    qm = [dq * m for m in m_hd]
    sks = [_sink_col(sink2, kv, SWA_QB) for kv in range(D_KV)]
    w_loc, w_ctx = [], []
    for kv in range(D_KV):
        pair = m_hd[kv * D_GROUP] + m_hd[kv * D_GROUP + 1]
        w_loc.append(dv * pair + (1.0 - pair))
        w_ctx.append(jnp.where(feat == kv, vc_t, jnp.ones_like(vc_t)))

    def scores(c):
        kv, j = divmod(c, nq)
        q2 = jnp.concatenate([qm[kv * D_GROUP + g][j * SWA_QB:(j + 1) * SWA_QB] for g in range(D_GROUP)], axis=0)
        return [_dot(q2, kc_t),
                jnp.where(masks[j * SWA_QB - starts[j]], _dot_nt(q2, dk[starts[j]:starts[j] + SWA_SPAN]), NEG)]

    def pv(c, s):
        kv, j = divmod(c, nq)
        (e_ctx, e_loc), m = _exp2_parts(s, extra=sks[kv])
        p = _dot_nt(e_ctx, w_ctx[kv]) + _dot(e_loc, w_loc[kv][starts[j]:starts[j] + SWA_SPAN])
        return p * (1.0 / (_swap_halves(p) + jnp.exp2(sks[kv] - m)))

    o = _pipelined(D_KV * nq, scores, pv)
    os = [jnp.concatenate([o[kv * nq + j][g * SWA_QB:(g + 1) * SWA_QB] for j in range(nq)], axis=0)
          for kv in range(D_KV) for g in range(D_GROUP)]
    zd_ref[...] = (_pick_heads(os) * gd_ref[...].astype(F32)).astype(BF16)


def _lat_swa(proj, cache_k, cache_v, sink3, l, z_d):
    srow = lambda b: (N_CTX // DEC_SEQ + b)
    return pl.pallas_call(
        _lat_swa_kernel,
        grid=(DEC_BATCH,),
        in_specs=[pl.BlockSpec((DEC_SEQ, 256), lambda b: (srow(b), COL_DQ // 256)),
                  pl.BlockSpec((DEC_SEQ, 256), lambda b: (srow(b), COL_DK // 256)),
                  pl.BlockSpec((DEC_SEQ, 256), lambda b: (srow(b), COL_DV // 256)),
                  pl.BlockSpec((1, 1, D_KV, HD, PAST_LEN), lambda b: (b, l, 0, 0, 0)),
                  pl.BlockSpec((1, 1, D_KV, HD, PAST_LEN), lambda b: (b, l, 0, 0, 0)),
                  pl.BlockSpec((1, 1, D_HEADS), lambda b: (l, 0, 0)),
                  pl.BlockSpec((DEC_SEQ, 256), lambda b: (srow(b), 3)),
                  pl.BlockSpec(memory_space=pl.ANY)],
        out_specs=pl.BlockSpec((DEC_SEQ, BR_W), lambda b: (srow(b), 0)),
        out_shape=jax.ShapeDtypeStruct((N_TOK, BR_W), BF16),
        input_output_aliases={7: 0},
        compiler_params=_cparams(("arbitrary",)),
        name=f"lat_swa_l{l}",
    )(proj, proj, proj, cache_k, cache_v, sink3, proj, z_d)


def _lru_conv(x, cw, cb, xp_s):
    seq = x.shape[0]
    xp_s[0:8, :] = jnp.zeros((8, BR_W), F32)
    xp_s[8:8 + seq, :] = x
    xp_s[8 + seq:16 + seq, :] = jnp.zeros((8, BR_W), F32)
    xm1, xp1, xp2 = xp_s[7:7 + seq, :], xp_s[9:9 + seq, :], xp_s[10:10 + seq, :]
    return cb + xm1 * cw[0:1] + x * cw[1:2] + xp1 * cw[2:3] + xp2 * cw[3:4]


def _lru_scan(d, xc, pre, lam_row, h0):
    seq = xc.shape[0]
    nl = -lam_row
    softplus = jnp.maximum(nl, 0.0) + jnp.log1p(jnp.exp(-jnp.abs(nl)))
    r = jax.nn.sigmoid(pre[:, :BR_W])
    ig = jax.nn.sigmoid(pre[:, BR_W:])
    log_a = -LRU_C * r * softplus
    a = jnp.exp(log_a)
    u = jnp.sqrt(-jnp.tanh(log_a) * (a * a + 1.0)) * (ig * xc)
    tin = lax.broadcasted_iota(jnp.int32, (1, 8, 1), 1)
    a = a.reshape(seq // 8, 8, BR_W)
    u = u.reshape(seq // 8, 8, BR_W)
    for s in (1, 2, 4):
        if d == 0:
            keep = tin >= s
            a_n = jnp.where(keep, pltpu.roll(a, s, 1), 1.0)
            u_n = jnp.where(keep, pltpu.roll(u, s, 1), 0.0)
        else:
            keep = tin < 8 - s
            a_n = jnp.where(keep, pltpu.roll(a, 8 - s, 1), 1.0)
            u_n = jnp.where(keep, pltpu.roll(u, 8 - s, 1), 0.0)
        u = u + a * u_n
        a = a * a_n
    nt = seq // 8
    ys = [None] * nt
    h = h0
    for k in (range(nt) if d == 0 else range(nt - 1, -1, -1)):
        y = u[k] + a[k] * h
        ys[k] = y
        h = y[7:8, :] if d == 0 else y[0:1, :]
    return jnp.concatenate(ys, axis=0), h


def _lru_specs(l, imap):
    return [pl.BlockSpec((1, CONV_W, BR_W), imap), pl.BlockSpec((1, 1, BR_W), imap),
            pl.BlockSpec((1, BR_W, 4 * BR_W), imap), pl.BlockSpec((1, 1, 4 * BR_W), imap),
            pl.BlockSpec((1, 2, BR_W), imap)]


def _merge_kernel(layer, final, xa_ref, xb_ref, mod_ref, h_ref, za_ref, zrc_ref, yf_ref, yb_ref, gr_ref,
                  zc_ref, zd_ref, wmg_hbm, bmg_ref, wbo_ref, wo_ref, nf_ref, *refs):
    out_refs, (wmg_s, wbo_s, wo_s, stage, sem) = refs[:-5], refs[-5:]
    i = pl.program_id(0)

    def wmg_copy(chunk, n):
        slot = chunk % 2
        return pltpu.make_async_copy(wmg_hbm.at[layer, pl.ds(chunk * W_ROWS, W_ROWS), n, :],
                                     stage.at[slot, n], sem.at[slot, n])

    @pl.when(i == 0)
    def _():
        for n in range(N_BRANCH):
            wmg_copy(0, n).start()

    @pl.when(i < W_STEPS - 1)
    def _():
        for n in range(N_BRANCH):
            wmg_copy(i + 1, n).start()

    @pl.when(i < W_STEPS)
    def _():
        r0 = pl.multiple_of(i * W_ROWS, W_ROWS)
        for n in range(N_BRANCH):
            wmg_copy(i, n).wait()
            wmg_s[pl.ds(r0, W_ROWS), n * D_MODEL:(n + 1) * D_MODEL] = stage[i % 2, n].astype(BF16)
        _cast_rows(i, wbo_ref, wbo_s)
        _cast_rows(i, wo_ref, wo_s)

    @pl.when(i >= W_STEPS)
    def _():
        t = i - W_STEPS
        x = _load_x(t, xa_ref, xb_ref)
        gate = mod_ref[0][:, 2 * D_MODEL:]
        h = h_ref[...]
        z_r = jnp.where(t < N_CTX_TILES, zrc_ref[...],
                        ((yf_ref[...] + yb_ref[...]) * gr_ref[...].astype(F32)).astype(BF16))
        zs = [za_ref[...], z_r, zc_ref[...], zd_ref[...]]
        bmg = bmg_ref[0]
        cols = []
        for c in range(0, D_MODEL, 512):
            acc = None
            for n in range(N_BRANCH):
                g = jax.nn.sigmoid(_dot(h, wmg_s[:, n * D_MODEL + c:n * D_MODEL + c + 512])
                                   + bmg[n:n + 1, c:c + 512])
                term = g * _dot(zs[n], wbo_s[n * BR_W:(n + 1) * BR_W, c:c + 512])
                acc = term if acc is None else acc + term
            cols.append(acc.astype(BF16))
        merged = jnp.concatenate(cols, axis=1)
        xn = x + gate * _dot(merged, wo_s[...])
        if not final:
            out_refs[0][...] = xn
        else:
            y = xn * lax.rsqrt(jnp.mean(xn * xn, axis=-1, keepdims=True) + EPS) * nf_ref[...]

            @pl.when(t < N_CTX_TILES)
            def _():
                out_refs[0][...] = y

            @pl.when(t >= N_CTX_TILES)
            def _():
                out_refs[1][...] = y


def _merge(final, xa, xb, xb_off, mod3, l, h, proj, z_a, zr_ctx, y_f, y_b, z_c, z_d, w_mg, b_mg, w_bo, w_o, norm_f):
    row = lambda i: (_tile(i), 0)
    ctx_blk = lambda i: jnp.minimum(_tile(i), N_CTX_TILES - 1)
    lat_blk = lambda i: jnp.maximum(_tile(i) - N_CTX_TILES, 0)
    if final:
        out_specs = [pl.BlockSpec((TM, D_MODEL), lambda i: (ctx_blk(i), 0)),
                     pl.BlockSpec((TM, D_MODEL), lambda i: (lat_blk(i), 0))]
        out_shape = [jax.ShapeDtypeStruct((N_CTX, D_MODEL), F32), jax.ShapeDtypeStruct((N_LAT, D_MODEL), F32)]
    else:
        out_specs = pl.BlockSpec((TM, D_MODEL), row)
        out_shape = jax.ShapeDtypeStruct((N_TOK, D_MODEL), F32)
    wchunk = lambda width: pl.BlockSpec((1, W_ROWS, width), lambda i: (l, _wchunk(i), 0))
    return pl.pallas_call(
        functools.partial(_merge_kernel, l, final),
        grid=(W_STEPS + N_TILES,),
        in_specs=[
            pl.BlockSpec((TM, D_MODEL), lambda i: (ctx_blk(i), 0)),
            pl.BlockSpec((TM, D_MODEL), lambda i: (lat_blk(i) + xb_off, 0)),
            pl.BlockSpec((1, 1, 3 * D_MODEL), lambda i: (l * 16 + _mod_row(_tile(i)), 0, 0)),
            pl.BlockSpec((TM, D_MODEL), row),
            pl.BlockSpec((TM, BR_W), row),
            pl.BlockSpec((TM, BR_W), lambda i: (ctx_blk(i), 0)),
            pl.BlockSpec((TM, BR_W), lambda i: (lat_blk(i), 0)),
            pl.BlockSpec((TM, BR_W), lambda i: (lat_blk(i), 0)),
            pl.BlockSpec((TM, BR_W), lambda i: (_tile(i), 1)),
            pl.BlockSpec((TM, BR_W), row), pl.BlockSpec((TM, BR_W), row),
            pl.BlockSpec(memory_space=pl.ANY),
            pl.BlockSpec((1, N_BRANCH, D_MODEL), lambda i: (l, 0, 0)),
            wchunk(D_MODEL),
            wchunk(D_MODEL),
            pl.BlockSpec((1, D_MODEL), lambda i: (0, 0)),
        ],
        out_specs=out_specs,
        out_shape=out_shape,
        scratch_shapes=[pltpu.VMEM((D_MODEL, N_BRANCH * D_MODEL), BF16),
                        pltpu.VMEM((N_BRANCH * BR_W, D_MODEL), BF16),
                        pltpu.VMEM((D_MODEL, D_MODEL), BF16),
                        pltpu.VMEM((2, N_BRANCH, W_ROWS, D_MODEL), F32),
                        pltpu.SemaphoreType.DMA((2, N_BRANCH))],
        compiler_params=_cparams(("arbitrary",)),
        name=f"merge_l{l}",
    )(xa, xb, mod3, h, z_a, zr_ctx, y_f, y_b, proj, z_c, z_d,
      w_mg, b_mg,
      w_bo.reshape(DEPTH, N_BRANCH * BR_W, D_MODEL), w_o, norm_f.reshape(1, D_MODEL))


def _lru_gate_weights(lru_wa, lru_ba, lru_wx, lru_bx):
    w = jnp.stack([lru_wa[:, 0], lru_wx[:, 0], lru_wa[:, 1], lru_wx[:, 1]], axis=1)
    eye = jnp.eye(B_BLOCKS, dtype=w.dtype)
    dense = w[:, :, :, :, None, :] * eye[None, None, :, None, :, None]
    wg = dense.transpose(0, 2, 3, 1, 4, 5).reshape(DEPTH, BR_W, 4 * BR_W).astype(BF16)
    bg = jnp.stack([lru_ba[:, 0], lru_bx[:, 0], lru_ba[:, 1], lru_bx[:, 1]], axis=1).reshape(DEPTH, 1, 4 * BR_W)
    return wg, bg


def kernel(x_prompt, x_sample, cache_diff_k, cache_diff_v, cache_na_k, cache_na_v, cache_swa_k, cache_swa_v,
           state_lru, c, c_ctx, norm_g, w_ada, b_ada, w_in, diff_lambda, diff_norm_g, conv_w, conv_b,
           lru_wa, lru_ba, lru_wx, lru_bx, lru_lam, na_rpb, swa_sink, w_mg, b_mg, w_bo, w_o, norm_f):
    tab_a_np, _ = _rope_tables(A_QK)
    tab_d_np, _ = _rope_tables(HD)
    tab_a, tab_d = jnp.asarray(tab_a_np), jnp.asarray(tab_d_np)

    cvecs = jnp.concatenate([c_ctx[None, :], c, jnp.zeros((16 - 1 - DEC_BATCH, D_MODEL), F32)], axis=0)
    mod3 = _modulation(cvecs, w_ada, b_ada).reshape(DEPTH * 16, 1, 3 * D_MODEL)
    bias = _na_bias(na_rpb)
    wg, bg = _lru_gate_weights(lru_wa, lru_ba, lru_wx, lru_bx)
    norm_g3 = norm_g.reshape(DEPTH, 1, D_MODEL)
    dg3 = diff_norm_g.reshape(DEPTH, 1, HD)
    dg4 = jnp.tile(diff_norm_g, (1, A_HEADS)).reshape(DEPTH, 1, BR_W)
    sink3 = swa_sink.reshape(DEPTH, 1, D_HEADS)
    conv_b3 = conv_b.reshape(DEPTH, 1, BR_W)
    past = [jnp.swapaxes(t, -1, -2) for t in
            (cache_diff_k, cache_diff_v, cache_na_k, cache_na_v, cache_swa_k, cache_swa_v)]

    xa = x_prompt.reshape(N_CTX, D_MODEL)
    xb = x_sample.reshape(N_LAT, D_MODEL)
    xb_off = 0
    caches = states = None
    y_p = y_s = None
    for l in range(DEPTH):
        h, proj, bx, caches = _inproj(xa, xb, xb_off, mod3, l, norm_g3, w_in, tab_a, tab_d, caches)

        lru_params = (conv_w, conv_b3, wg, bg, lru_lam)
        z_a, z_c, z_d, zr_ctx, states = _ctx_attn(proj, bx, l, diff_lambda, dg4, sink3, lru_params, states)
        z_a = _lat_diff(proj, past[0], past[1], l, diff_lambda, dg3, z_a)
        z_c, y_f = _lat_na(0, proj, bx, past[2], past[3], bias, l, lru_params, state_lru, z_c)
        z_c, y_b = _lat_na(1, proj, bx, past[2], past[3], bias, l, lru_params, state_lru, z_c)
        z_d = _lat_swa(proj, past[4], past[5], sink3, l, z_d)

        final = l == DEPTH - 1
        out = _merge(final, xa, xb, xb_off, mod3, l, h, proj, z_a, zr_ctx, y_f, y_b, z_c, z_d,
                     w_mg, b_mg, w_bo, w_o, norm_f)
        if final:
            y_p, y_s = out
        else:
            xa = xb = out
            xb_off = N_CTX_TILES

    new_caches = [jnp.swapaxes(t, -1, -2) for t in caches]
    return (y_p.reshape(BATCH, SEQ, D_MODEL), y_s.reshape(DEC_BATCH, DEC_SEQ, D_MODEL), *new_caches, states)
```

```python
import functools
import math

import numpy as np
import jax
import jax.numpy as jnp
from jax import lax
from jax.experimental import pallas as pl
from jax.experimental.pallas import tpu as pltpu

F32 = jnp.float32
BF16 = jnp.bfloat16

D_MODEL = 1024
BATCH = 16
SEQ = 256
DEPTH = 2
DEC_BATCH = 8
DEC_SEQ = 1024
PAST_LEN = 512
GRID_W = 64
N_BRANCH = 4
BR_W = D_MODEL // 4
HD = 64
A_HEADS = BR_W // HD
A_QK = HD // 2
B_BLOCKS = 4
B_BLK = BR_W // B_BLOCKS
CONV_W = 4
LRU_C = 8.0
C_HEADS = BR_W // HD
NA_ROWS = 8
NA_COLS = 16
D_HEADS = BR_W // HD
D_KV = 2
D_GROUP = D_HEADS // D_KV
WIN = 128
ROPE_BASE = 10000.0
EPS = 1e-6
NEG = -1e30

N_CTX = BATCH * SEQ
N_LAT = DEC_BATCH * DEC_SEQ
N_TOK = N_CTX + N_LAT
GRID_ROWS = DEC_SEQ // GRID_W

TM = 512
N_CTX_TILES = N_CTX // TM
N_TILES = N_TOK // TM
LAT_TILES_PER_SEQ = DEC_SEQ // TM

PROJ_W = 3328
COL_G = 0
COL_AQ = 1024
COL_AK = 1280
COL_AV = 1536
COL_CQ = 1792
COL_CK = 2048
COL_CV = 2304
COL_DQ = 2560
COL_DK = 2816
COL_DV = 3072
IN_W = 3328
WS_W = IN_W + 2 * D_KV * HD

NA_HALF_Q = DEC_SEQ // 2
NA_KROWS = 12
NA_KSPAN = NA_KROWS * GRID_W
NA_HALF_KSTART = 4 * GRID_W

SWA_QB = 128
SWA_SPAN = SWA_QB + 2 * WIN

LOG2E = math.log2(math.e)
QA_SCALE = A_QK ** -0.5 * LOG2E
QH_SCALE = HD ** -0.5 * LOG2E

VMEM_LIMIT = 56 * 1024 * 1024


def _cparams(sem):
    return pltpu.CompilerParams(dimension_semantics=sem, vmem_limit_bytes=VMEM_LIMIT)


def _resident(shape, index_map):
    return pl.BlockSpec(shape, index_map, pipeline_mode=pl.Buffered(1))


def _dot(a, b):
    return jnp.dot(a, b, preferred_element_type=F32)


def _dot_nt(a, b):
    return lax.dot_general(a, b, (((1,), (1,)), ((), ())), preferred_element_type=F32)


def _silu(x):
    return x * jax.nn.sigmoid(x)


def _rope_tables(d):
    half = d // 2
    quarter = half // 2
    lane = np.arange(128)
    q = lane % d
    use_col = (q >= half)
    i = (q % half) % quarter
    first = (q % half) < quarter
    inv = (ROPE_BASE ** (-np.arange(quarter, dtype=np.float32) / np.float32(quarter))).astype(np.float32)
    t = np.arange(DEC_SEQ)
    pos = np.where(use_col[None, :], (t % GRID_W)[:, None], (t // GRID_W)[:, None]).astype(np.float32)
    ang = (pos * inv[i][None, :]).astype(np.float32)
    cos, sin = np.cos(ang), np.sin(ang)
    c = np.concatenate([np.ones((TM, 128), np.float32), cos], axis=0)
    s1 = np.concatenate([np.zeros((TM, 128), np.float32), np.where(first[None, :], -sin, 0.0)], axis=0)
    s2 = np.concatenate([np.zeros((TM, 128), np.float32), np.where(first[None, :], 0.0, sin)], axis=0)
    return np.stack([c, s1, s2]).astype(np.float32), quarter


def _rope128(x, tab_ref, shift):
    return (x * tab_ref[0] + pltpu.roll(x, 128 - shift, 1) * tab_ref[1]
            + pltpu.roll(x, shift, 1) * tab_ref[2])


def _rope(x, tab_ref, shift):
    w = x.shape[1]
    return jnp.concatenate([_rope128(x[:, c:c + 128], tab_ref, shift) for c in range(0, w, 128)], axis=1)


def _mod_kernel(c_ref, w_ref, b_ref, o_ref):
    c = c_ref[...]
    o_ref[0] = _dot(_silu(c).astype(BF16), w_ref[0].astype(BF16)) + b_ref[0]


def _modulation(cvecs, w_ada, b_ada):
    nb = 3 * D_MODEL // 768
    return pl.pallas_call(
        _mod_kernel,
        grid=(DEPTH, nb),
        in_specs=[pl.BlockSpec((16, D_MODEL), lambda l, j: (0, 0)),
                  pl.BlockSpec((1, D_MODEL, 768), lambda l, j: (l, 0, j)),
                  pl.BlockSpec((1, 1, 768), lambda l, j: (l, 0, j))],
        out_specs=pl.BlockSpec((1, 16, 768), lambda l, j: (l, 0, j)),
        out_shape=jax.ShapeDtypeStruct((DEPTH, 16, 3 * D_MODEL), F32),
        compiler_params=_cparams(("arbitrary", "arbitrary")),
        name="modulation",
    )(cvecs, w_ada, b_ada.reshape(DEPTH, 1, 3 * D_MODEL))


W_STEPS = 8
W_ROWS = D_MODEL // W_STEPS


def _tile(i):
    return jnp.maximum(i - W_STEPS, 0)


def _wchunk(i):
    return jnp.minimum(i, W_STEPS - 1)


def _mod_row(t):
    return jnp.where(t < N_CTX_TILES, 0, 1 + (t - N_CTX_TILES) // LAT_TILES_PER_SEQ)


def _rope_blk(t):
    return jnp.where(t < N_CTX_TILES, 0, 1 + (t - N_CTX_TILES) % LAT_TILES_PER_SEQ)


def _load_x(t, xa_ref, xb_ref):
    return jnp.where(t < N_CTX_TILES, xa_ref[...], xb_ref[...])


def _cast_rows(i, src_ref, dst_ref):
    r0 = pl.multiple_of(i * W_ROWS, W_ROWS)
    dst_ref[pl.ds(r0, W_ROWS), :] = src_ref[0].astype(BF16)


WI_G, WI_AQ, WI_AK, WI_AV, WI_BX, WI_CQ, WI_CKV, WI_DQ, WI_DKV = 0, 1024, 1280, 1536, 1792, 2048, 2304, 2816, 3072
SEQ_PER_TILE = TM // SEQ


def _store_heads(ref, val, n_heads, width):
    vt = val.T
    for s in range(SEQ_PER_TILE):
        for h in range(n_heads):
            ref[s, 0, h] = vt[h * width:(h + 1) * width, s * SEQ:(s + 1) * SEQ]


def _inproj_kernel(n_alias, xa_ref, xb_ref, mod_ref, g_ref, w_ref, ta_ref, td_ref, *refs):
    h_ref, proj_ref, bx_ref, dk_ref, dv_ref, nk_ref, nv_ref, sk_ref, sv_ref, w_s = refs[n_alias:]
    i = pl.program_id(0)

    @pl.when(i < W_STEPS)
    def _():
        r0 = pl.multiple_of(i * W_ROWS, W_ROWS)
        wb = w_ref[0].astype(BF16)
        w_s[pl.ds(r0, W_ROWS), 0:WI_DKV] = wb[:, 0:WI_DKV]
        for n, src in enumerate((0, 0, 1, 1, 2, 2, 3, 3)):
            w_s[pl.ds(r0, W_ROWS), WI_DKV + n * HD:WI_DKV + (n + 1) * HD] = \
                wb[:, WI_DKV + src * HD:WI_DKV + (src + 1) * HD]

    @pl.when(i >= W_STEPS)
    def _():
        t = i - W_STEPS
        is_ctx = t < N_CTX_TILES
        x = _load_x(t, xa_ref, xb_ref)
        mod = mod_ref[0]
        shift, scale = mod[:, :D_MODEL], mod[:, D_MODEL:2 * D_MODEL]
        xn = x * lax.rsqrt(jnp.mean(x * x, axis=-1, keepdims=True) + EPS) * g_ref[0]
        hb =(xn * (1.0 + scale) + shift).astype(BF16)
        h_ref[...] = hb

        def mm(c0, width):
            return _dot(hb, w_s[:, c0:c0 + width])

        gs = [mm(WI_G + c, 512) for c in range(0, 1024, 512)]
        aq, ak, av = mm(WI_AQ, 256), mm(WI_AK, 256), mm(WI_AV, 256)
        cq, ckv = mm(WI_CQ, 256), mm(WI_CKV, 512)
        dq, dk4, dv4 = mm(WI_DQ, 256), mm(WI_DKV, 256), mm(WI_DKV + 256, 256)
        bx_ref[...] = mm(WI_BX, 256)

        for n, g in enumerate(gs):
            proj_ref[:, COL_G + n * 512:COL_G + (n + 1) * 512] = _silu(g).astype(BF16)
        proj_ref[:, COL_AQ:COL_AQ + 256] = _rope(aq * QA_SCALE, ta_ref, A_QK // 4).astype(BF16)
        proj_ref[:, COL_AK:COL_AK + 256] = _rope(ak, ta_ref, A_QK // 4).astype(BF16)
        proj_ref[:, COL_AV:COL_AV + 256] = av.astype(BF16)
        proj_ref[:, COL_CQ:COL_CQ + 256] = (cq * QH_SCALE).astype(BF16)
        proj_ref[:, COL_CK:COL_CK + 512] = ckv.astype(BF16)
        proj_ref[:, COL_DQ:COL_DQ + 256] = _rope(dq * QH_SCALE, td_ref, HD // 4).astype(BF16)
        proj_ref[:, COL_DK:COL_DK + 256] = _rope(dk4, td_ref, HD // 4).astype(BF16)
        proj_ref[:, COL_DV:COL_DV + 256] = dv4.astype(BF16)

        @pl.when(is_ctx)
        def _():
            akt = ak.T
            for s in range(SEQ_PER_TILE):
                for h in range(A_HEADS):
                    for m in range(2):
                        c0 = h * HD + m * A_QK
                        dk_ref[s, 0, h, m] = akt[c0:c0 + A_QK, s * SEQ:(s + 1) * SEQ]
            _store_heads(dv_ref, av, A_HEADS, HD)
            _store_heads(nk_ref, ckv[:, :256], C_HEADS, HD)
            _store_heads(nv_ref, ckv[:, 256:], C_HEADS, HD)
            for ref, val in ((sk_ref, dk4), (sv_ref, dv4)):
                vt = val.T
                for s in range(SEQ_PER_TILE):
                    for kv in range(D_KV):
                        ref[s, 0, kv] = vt[kv * D_GROUP * HD:kv * D_GROUP * HD + HD, s * SEQ:(s + 1) * SEQ]


def _cache_shapes():
    hs = lambda n: (BATCH, DEPTH, n, HD, SEQ)
    return [(BATCH, DEPTH, A_HEADS, 2, A_QK, SEQ), hs(A_HEADS), hs(C_HEADS), hs(C_HEADS), hs(D_KV), hs(D_KV)]


def _inproj(xa, xb, xb_off, mod3, l, norm_g, w_in, tab_a, tab_d, caches):
    row = lambda i: (_tile(i), 0)
    ctx_blk = lambda i: jnp.minimum(_tile(i), N_CTX_TILES - 1)

    def cache_spec(shape):
        blk = (SEQ_PER_TILE, 1) + shape[2:]
        nz = len(shape) - 2
        return pl.BlockSpec(blk, lambda i: (ctx_blk(i), l) + (0,) * nz)

    cshapes = _cache_shapes()
    aliases = {} if caches is None else {7 + k: 3 + k for k in range(6)}
    extra_specs = [] if caches is None else [pl.BlockSpec(memory_space=pl.ANY)] * 6
    extra_args = [] if caches is None else list(caches)
    outs = pl.pallas_call(
        functools.partial(_inproj_kernel, len(extra_args)),
        grid=(W_STEPS + N_TILES,),
        in_specs=[
            pl.BlockSpec((TM, D_MODEL), lambda i: (ctx_blk(i), 0)),
            pl.BlockSpec((TM, D_MODEL), lambda i: (jnp.maximum(_tile(i) - N_CTX_TILES, 0) + xb_off, 0)),
            pl.BlockSpec((1, 1, 3 * D_MODEL), lambda i: (l * 16 + _mod_row(_tile(i)), 0, 0)),
            pl.BlockSpec((1, 1, D_MODEL), lambda i: (l, 0, 0)),
            pl.BlockSpec((1, W_ROWS, IN_W), lambda i: (l, _wchunk(i), 0)),
            pl.BlockSpec((3, TM, 128), lambda i: (0, _rope_blk(_tile(i)), 0)),
            pl.BlockSpec((3, TM, 128), lambda i: (0, _rope_blk(_tile(i)), 0)),
        ] + extra_specs,
        out_specs=[
            pl.BlockSpec((TM, D_MODEL), row),
            pl.BlockSpec((TM, PROJ_W), row),
            pl.BlockSpec((TM, BR_W), row),
        ] + [cache_spec(s) for s in cshapes],
        out_shape=[
            jax.ShapeDtypeStruct((N_TOK, D_MODEL), BF16),
            jax.ShapeDtypeStruct((N_TOK, PROJ_W), BF16),
            jax.ShapeDtypeStruct((N_TOK, BR_W), F32),
        ] + [jax.ShapeDtypeStruct(s, F32) for s in cshapes],
        scratch_shapes=[pltpu.VMEM((D_MODEL, WS_W), BF16)],
        input_output_aliases=aliases,
        compiler_params=_cparams(("arbitrary",)),
        name=f"inproj_l{l}",
    )(xa, xb, mod3, norm_g, w_in, tab_a, tab_d, *extra_args)
    return outs[0], outs[1], outs[2], tuple(outs[3:])


def _diff_lambda(lam_ref, lam_init):
    lv = lam_ref[0]
    s1 = jnp.sum(lv[0:1] * lv[1:2], axis=-1, keepdims=True)
    s2 = jnp.sum(lv[2:3] * lv[3:4], axis=-1, keepdims=True)
    return jnp.exp(s1) - jnp.exp(s2) + lam_init


def _diff_norm(o, dg_ref, lam_init):
    y = o * lax.rsqrt(jnp.mean(o * o, axis=-1, keepdims=True) + EPS) * dg_ref[0]
    return y * (1.0 - lam_init)


def _exp2_parts(parts, extra=None):
    m = functools.reduce(jnp.maximum, [jnp.max(s, axis=-1, keepdims=True) for s in parts])
    if extra is not None:
        m = jnp.maximum(m, extra)
    return [jnp.exp2(s - m).astype(BF16) for s in parts], m


def _pipelined(n, score_fn, pv_fn):
    outs = []
    nxt = score_fn(0)
    for c in range(n):
        cur, nxt = nxt, (score_fn(c + 1) if c + 1 < n else None)
        outs.append(pv_fn(c, cur))
    return outs


def _with_ones(v):
    return jnp.concatenate([v, jnp.ones_like(v)], axis=1)


def _with_ones_t(vt):
    return jnp.concatenate([vt, jnp.ones((16, vt.shape[1]), vt.dtype)], axis=0)


def _stack_group(dq, kv, rows):
    return jnp.concatenate([dq[:, (kv * D_GROUP + g) * HD:(kv * D_GROUP + g + 1) * HD]
                            for g in range(D_GROUP)], axis=0)


def _sink_col(sink2, kv, rows):
    r = lax.broadcasted_iota(jnp.int32, (D_GROUP * rows, 1), 0)
    return jnp.where(r < rows, sink2[:, kv * D_GROUP:kv * D_GROUP + 1], sink2[:, kv * D_GROUP + 1:kv * D_GROUP + 2])


CTX_PER_STEP = 2


def _lane_masks(n):
    lane = lax.broadcasted_iota(jnp.int32, (1, BR_W), 1)
    w = BR_W // n
    ms = [jnp.where((lane >= p * w) & (lane < (p + 1) * w), 1.0, 0.0).astype(BF16) for p in range(n)]
    return ms, [1.0 - m for m in ms]


def _swap_halves(p):
    return jnp.concatenate([p[:, BR_W // 2:], p[:, :BR_W // 2]], axis=1)


def _pick_heads(os):
    grp = lax.broadcasted_iota(jnp.int32, os[0].shape, 1) // HD
    acc = os[0]
    for h in range(1, len(os)):
        acc = jnp.where(grp == h, os[h], acc)
    return acc


def _head_rms(o, dg4_ref, lam_init):
    r = lax.broadcasted_iota(jnp.int32, (BR_W, BR_W), 0) // HD
    c = lax.broadcasted_iota(jnp.int32, (BR_W, BR_W), 1) // HD
    ss = _dot((o * o).astype(BF16), jnp.where(r == c, 1.0, 0.0).astype(BF16))
    return o * lax.rsqrt(ss * (1.0 / HD) + EPS) * dg4_ref[0] * (1.0 - lam_init)


def _ctx_attn_kernel(n_alias, lam_init, aq_ref, ak_ref, av_ref, cq_ref, ck_ref, cv_ref, dq_ref, dk_ref, dv_ref,
                     ga_ref, gc_ref, gd_ref, lam_ref, dg4_ref, sink_ref,
                     bx_ref, gr_ref, cw_ref, cb_ref, wg_ref, bg_ref, llam_ref, *refs):
    za_ref, zc_ref, zd_ref, zr_ref, fin_ref = refs[n_alias:]
    seqs = [slice(s * SEQ, (s + 1) * SEQ) for s in range(CTX_PER_STEP)]

    for s, rs in enumerate(seqs):
        xc = _lru_conv(bx_ref[rs, :], cw_ref[0], cb_ref[0])
        pre = _dot(xc.astype(BF16), wg_ref[0]) + bg_ref[0]
        h0 = jnp.zeros((1, BR_W), F32)
        yf, hf = _lru_scan(0, xc, pre[:, :2 * BR_W], llam_ref[0, 0:1, :], h0)
        yb, hb = _lru_scan(1, xc, pre[:, 2 * BR_W:], llam_ref[0, 1:2, :], h0)
        zr_ref[rs, :] = ((yf + yb) * gr_ref[rs, :].astype(F32)).astype(BF16)
        fin_ref[s, 0, 0:1, :] = hf
        fin_ref[s, 0, 1:2, :] = hb

    lam = _diff_lambda(lam_ref, lam_init)
    sink2 = sink_ref[0] * LOG2E
    m_qk, _ = _lane_masks(2 * A_HEADS)
    m_hd, o_hd = _lane_masks(C_HEADS)

    sa, sc, sd = [], [], []
    for rs in seqs:
        aq, cq, dq = aq_ref[rs, :], cq_ref[rs, :], dq_ref[rs, :]
        ak, ck, dk = ak_ref[rs, :], ck_ref[rs, :], dk_ref[rs, :]
        sa.append([_dot_nt(aq * m, ak) for m in m_qk])
        sc.append([_dot_nt(cq * m, ck) for m in m_hd])
        sd.append([_dot_nt(dq * m, dk) for m in m_hd])

    ea =[[_exp2_parts([s])[0][0] for s in ss] for ss in sa]
    ec = [[_exp2_parts([s])[0][0] for s in ss] for ss in sc]
    ed = [[_exp2_parts([s], extra=sink2[:, j:j + 1]) for j, s in enumerate(ss)] for ss in sd]

    for i, rs in enumerate(seqs):
        av, cv, dv = av_ref[rs, :], cv_ref[rs, :], dv_ref[rs, :]
        os = []
        for h in range(A_HEADS):
            w = av * m_hd[h] + o_hd[h]
            p0, p1 = _dot(ea[i][2 * h], w), _dot(ea[i][2 * h + 1], w)
            os.append(p0 * (1.0 / _swap_halves(p0)) - p1 * (lam / _swap_halves(p1)))
        za_ref[rs, :] = (_head_rms(_pick_heads(os), dg4_ref, lam_init) * ga_ref[rs, :].astype(F32)).astype(BF16)

        os = []
        for h in range(C_HEADS):
            p = _dot(ec[i][h], cv * m_hd[h] + o_hd[h])
            os.append(p * (1.0 / _swap_halves(p)))
        zc_ref[rs, :] = (_pick_heads(os) * gc_ref[rs, :].astype(F32)).astype(BF16)

        os = []
        for j in range(D_HEADS):
            (e,), m = ed[i][j]
            p = _dot(e, dv * m_hd[j] + o_hd[j])
            os.append(p * (1.0 / (_swap_halves(p) + jnp.exp2(sink2[:, j:j + 1] - m))))
        zd_ref[rs, :] = (_pick_heads(os) * gd_ref[rs, :].astype(F32)).astype(BF16)


def _ctx_attn(proj, bx, l, diff_lambda, dg4, sink3, lru_params, states):
    lam_init = 0.8 - 0.6 * math.exp(-0.3 * l)
    colblk = lambda cb: pl.BlockSpec((CTX_PER_STEP * SEQ, 256), lambda b: (b, cb))
    zspec = pl.BlockSpec((CTX_PER_STEP * SEQ, BR_W), lambda b: (b, 0))
    zshape = jax.ShapeDtypeStruct((N_TOK, BR_W), BF16)
    extra = [] if states is None else [states]
    return pl.pallas_call(
        functools.partial(_ctx_attn_kernel, len(extra), lam_init),
        grid=(BATCH // CTX_PER_STEP,),
        in_specs=[colblk(COL_AQ // 256), colblk(COL_AK // 256), colblk(COL_AV // 256),
                  colblk(COL_CQ // 256), colblk(COL_CK // 256), colblk(COL_CV // 256),
                  colblk(COL_DQ // 256), colblk(COL_DK // 256), colblk(COL_DV // 256),
                  colblk(0), colblk(2), colblk(3),
                  pl.BlockSpec((1, 4, A_QK), lambda b: (l, 0, 0)),
                  pl.BlockSpec((1, 1, BR_W), lambda b: (l, 0, 0)),
                  pl.BlockSpec((1, 1, D_HEADS), lambda b: (l, 0, 0)),
                  pl.BlockSpec((CTX_PER_STEP * SEQ, BR_W), lambda b: (b, 0)), colblk(1)]
                 + _lru_specs(l, lambda b: (l, 0, 0)) + [pl.BlockSpec(memory_space=pl.ANY)] * len(extra),
        out_specs=[zspec, zspec, zspec, zspec,
                   pl.BlockSpec((CTX_PER_STEP, 1, 2, BR_W), lambda b: (b, l, 0, 0))],
        out_shape=[zshape, zshape, zshape, jax.ShapeDtypeStruct((N_CTX, BR_W), BF16),
                   jax.ShapeDtypeStruct((BATCH, DEPTH, 2, BR_W), F32)],
        input_output_aliases={} if states is None else {22: 4},
        compiler_params=_cparams(("arbitrary",)),
        name=f"ctx_attn_l{l}",
    )(*([proj] * 12), diff_lambda, dg4, sink3, bx, proj, *lru_params, *extra)


LA_QB = 256


def _lat_diff_kernel(lam_init, aq_ref, ak_ref, av_ref, ck_ref, cv_ref, ga_ref, lam_ref, dg_ref, zin_ref, za_ref):
    del zin_ref
    lam = _diff_lambda(lam_ref, lam_init)
    aq = aq_ref[...]
    va_ctx = [_with_ones_t(cv_ref[0, 0, h].astype(BF16)) for h in range(A_HEADS)]
    va_loc = [_with_ones(av_ref[:, h * HD:(h + 1) * HD]) for h in range(A_HEADS)]

    def scores(c):
        h, m = divmod(c, 2)
        lo = h * HD + m * A_QK
        q = aq[:, lo:lo + A_QK]
        return [_dot(q, ck_ref[0, 0, h, m].astype(BF16)),
                _dot_nt(q, ak_ref[:, lo:lo + A_QK])]

    def pv(c, s):
        (e_ctx, e_loc), _ = _exp2_parts(s)
        p_ctx, p_loc = _dot_nt(e_ctx, va_ctx[c // 2]), _dot(e_loc, va_loc[c // 2])
        return p_ctx[:, :HD] + p_loc[:, :HD], p_ctx[:, HD:HD + 1] + p_loc[:, HD:HD + 1]

    ss = [scores(c) for c in range(2 * A_HEADS)]
    nd = [pv(c, s) for c, s in enumerate(ss)]
    outs = []
    for h in range(A_HEADS):
        (n0, d0), (n1, d1) = nd[2 * h], nd[2 * h + 1]
        outs.append(_diff_norm(n0 * (1.0 / d0) - n1 * (lam / d1), dg_ref, lam_init))
    za_ref[...] = (jnp.concatenate(outs, axis=1) * ga_ref[...].astype(F32)).astype(BF16)


def _lat_diff(proj, cache_k, cache_v, l, diff_lambda, dg3, z_a):
    lam_init = 0.8 - 0.6 * math.exp(-0.3 * l)
    nq = DEC_SEQ // LA_QB
    qrow = lambda b, j: N_CTX // LA_QB + b * nq + j
    srow = lambda b, j: N_CTX // DEC_SEQ + b
    return pl.pallas_call(
        functools.partial(_lat_diff_kernel, lam_init),
        grid=(DEC_BATCH, nq),
        in_specs=[pl.BlockSpec((LA_QB, 256), lambda b, j: (qrow(b, j), COL_AQ // 256)),
                  pl.BlockSpec((DEC_SEQ, 256), lambda b, j: (srow(b, j), COL_AK // 256)),
                  pl.BlockSpec((DEC_SEQ, 256), lambda b, j: (srow(b, j), COL_AV // 256)),
                  pl.BlockSpec((1, 1, A_HEADS, 2, A_QK, PAST_LEN), lambda b, j: (b, l, 0, 0, 0, 0)),
                  pl.BlockSpec((1, 1, A_HEADS, HD, PAST_LEN), lambda b, j: (b, l, 0, 0, 0)),
                  pl.BlockSpec((LA_QB, 256), lambda b, j: (qrow(b, j), 0)),
                  pl.BlockSpec((1, 4, A_QK), lambda b, j: (l, 0, 0)),
                  pl.BlockSpec((1, 1, HD), lambda b, j: (l, 0, 0)),
                  pl.BlockSpec(memory_space=pl.ANY)],
        out_specs=pl.BlockSpec((LA_QB, BR_W), lambda b, j: (qrow(b, j), 0)),
        out_shape=jax.ShapeDtypeStruct((N_TOK, BR_W), BF16),
        input_output_aliases={8: 0},
        compiler_params=_cparams(("arbitrary", "arbitrary")),
        name=f"lat_diff_l{l}",
    )(proj, proj, proj, cache_k, cache_v, proj, diff_lambda, dg3, z_a)


def _na_bias_kernel(rpb_ref, o_ref):
    l, h = pl.program_id(0), pl.program_id(1)
    base = (l * C_HEADS + h) * ((2 * NA_ROWS - 1) * (2 * NA_COLS - 1))
    qc = lax.broadcasted_iota(jnp.int32, (GRID_W, GRID_W), 0)
    kc = lax.broadcasted_iota(jnp.int32, (GRID_W, GRID_W), 1)
    cs = jnp.clip(qc - NA_COLS // 2, 0, GRID_W - NA_COLS)
    ok = (kc >= cs) & (kc < cs + NA_COLS)
    dcol = kc - qc + (NA_COLS - 1)
    neg = jnp.full((GRID_W, GRID_W), NEG, F32)
    tabs = []
    for d in range(2 * NA_ROWS - 1):
        t = neg
        for dc in range(2 * NA_COLS - 1):
            t = jnp.where(ok & (dcol == dc), rpb_ref[base + d * (2 * NA_COLS - 1) + dc] * LOG2E, t)
        tabs.append(t)
    for half in range(2):
        for rq in range(NA_ROWS):
            r = half * NA_ROWS + rq
            rs = min(max(r - NA_ROWS // 2, 0), GRID_ROWS - NA_ROWS)
            blks = []
            for kr in range(NA_KROWS):
                kabs = half * 4 + kr
                blks.append(tabs[kabs - r + NA_ROWS - 1] if rs <= kabs < rs + NA_ROWS else neg)
            o_ref[0, 0, half, rq * GRID_W:(rq + 1) * GRID_W, :] = jnp.concatenate(blks, axis=1)


def _na_bias(na_rpb):
    return pl.pallas_call(
        _na_bias_kernel,
        grid=(DEPTH, C_HEADS),
        in_specs=[pl.BlockSpec(memory_space=pltpu.SMEM)],
        out_specs=pl.BlockSpec((1, 1, 2, NA_HALF_Q, NA_KSPAN), lambda l, h: (l, h, 0, 0, 0)),
        out_shape=jax.ShapeDtypeStruct((DEPTH, C_HEADS, 2, NA_HALF_Q, NA_KSPAN), F32),
        compiler_params=_cparams(("arbitrary", "arbitrary")),
        name="na_bias",
    )(na_rpb.reshape(-1))


def _lat_na_kernel(half, cq_ref, ck_ref, cv_ref, kc_ref, vc_ref, bias_ref, gc_ref,
                   bx_ref, cw_ref, cb_ref, wg_ref, bg_ref, llam_ref, st_ref, zin_ref, zc_ref, y_ref):
    del zin_ref
    k0 = half * NA_HALF_KSTART
    cq = cq_ref[...]
    kl_all = ck_ref[k0:k0 + NA_KSPAN, :]
    vl_all = cv_ref[k0:k0 + NA_KSPAN, :]

    def scores(h):
        q = cq[:, h * HD:(h + 1) * HD]
        return [_dot(q, kc_ref[0, 0, h].astype(BF16)),
                _dot_nt(q, kl_all[:, h * HD:(h + 1) * HD]) + bias_ref[0, h, 0]]

    def pv(h, s):
        (e_ctx, e_loc), _ = _exp2_parts(s)
        p_ctx = _dot_nt(e_ctx, _with_ones_t(vc_ref[0, 0, h].astype(BF16)))
        p_loc = _dot(e_loc, _with_ones(vl_all[:, h * HD:(h + 1) * HD]))
        return (p_ctx[:, :HD] + p_loc[:, :HD]) * (1.0 / (p_ctx[:, HD:HD + 1] + p_loc[:, HD:HD + 1]))

    xc = _lru_conv(bx_ref[...], cw_ref[0], cb_ref[0])
    pre = _dot(xc.astype(BF16), wg_ref[0, :, half * 2 * BR_W:(half + 1) * 2 * BR_W]) \
        + bg_ref[0, :, half * 2 * BR_W:(half + 1) * 2 * BR_W]
    y_ref[...], _ = _lru_scan(half, xc, pre, llam_ref[0, half:half + 1, :], st_ref[0, 0, half:half + 1, :])

    ss = [scores(h) for h in range(C_HEADS)]
    outs = [pv(h, s) for h, s in enumerate(ss)]
    zc_ref[...] = (jnp.concatenate(outs, axis=1) * gc_ref[...].astype(F32)).astype(BF16)


def _lat_na(half, proj, bx, cache_k, cache_v, bias, l, lru_params, state_lru, z_c):
    qrow = lambda b: N_CTX // NA_HALF_Q + b * 2 + half
    srow = lambda b: N_CTX // DEC_SEQ + b
    return pl.pallas_call(
        functools.partial(_lat_na_kernel, half),
        grid=(DEC_BATCH,),
        in_specs=[pl.BlockSpec((NA_HALF_Q, 256), lambda b: (qrow(b), COL_CQ // 256)),
                  pl.BlockSpec((DEC_SEQ, 256), lambda b: (srow(b), COL_CK // 256)),
                  pl.BlockSpec((DEC_SEQ, 256), lambda b: (srow(b), COL_CV // 256)),
                  pl.BlockSpec((1, 1, C_HEADS, HD, PAST_LEN), lambda b: (b, l, 0, 0, 0)),
                  pl.BlockSpec((1, 1, C_HEADS, HD, PAST_LEN), lambda b: (b, l, 0, 0, 0)),
                  pl.BlockSpec((1, C_HEADS, 1, NA_HALF_Q, NA_KSPAN), lambda b: (l, 0, half, 0, 0)),
                  pl.BlockSpec((NA_HALF_Q, 256), lambda b: (qrow(b), 2)),
                  pl.BlockSpec((DEC_SEQ, BR_W), lambda b: (srow(b), 0))]
                 + _lru_specs(l, lambda b: (l, 0, 0))
                 + [pl.BlockSpec((1, 1, 2, BR_W), lambda b: (b, l, 0, 0)), pl.BlockSpec(memory_space=pl.ANY)],
        out_specs=[pl.BlockSpec((NA_HALF_Q, BR_W), lambda b: (qrow(b), 0)),
                   pl.BlockSpec((DEC_SEQ, BR_W), lambda b: (b, 0))],
        out_shape=[jax.ShapeDtypeStruct((N_TOK, BR_W), BF16), jax.ShapeDtypeStruct((N_LAT, BR_W), F32)],
        input_output_aliases={14: 0},
        compiler_params=_cparams(("arbitrary",)),
        name=f"lat_na{half}_l{l}",
    )(proj, proj, proj, cache_k, cache_v, bias, proj, bx, *lru_params, state_lru, z_c)


def _lat_swa_kernel(dq_ref, dk_ref, dv_ref, kc_ref, vc_ref, sink_ref, gd_ref, zin_ref, zd_ref):
    del zin_ref
    nq = DEC_SEQ // SWA_QB
    rows = D_GROUP * SWA_QB
    dq, dk, dv = dq_ref[...], dk_ref[...], dv_ref[...]
    sink2 = sink_ref[0] * LOG2E
    m_hd, _ = _lane_masks(D_HEADS)
    starts = [min(max(j * SWA_QB - WIN, 0), DEC_SEQ - SWA_SPAN) for j in range(nq)]
    qi = lax.broadcasted_iota(jnp.int32, (rows, SWA_SPAN), 0) % SWA_QB
    ki = lax.broadcasted_iota(jnp.int32, (rows, SWA_SPAN), 1)
    masks = {off: jnp.abs(qi + off - ki) <= WIN for off in sorted({j * SWA_QB - s for j, s in enumerate(starts)})}
    kc_t = jnp.concatenate([kc_ref[0, 0, kv] for kv in range(D_KV) for _ in range(D_GROUP)], axis=0).astype(BF16)
    vc_t = jnp.concatenate([vc_ref[0, 0, kv] for kv in range(D_KV) for _ in range(D_GROUP)], axis=0).astype(BF16)
    feat = lax.broadcasted_iota(jnp.int32, (BR_W, 1), 0) // (D_GROUP * HD)

    qm = [dq * m for m in m_hd]
    sks = [_sink_col(sink2, kv, SWA_QB) for kv in range(D_KV)]
    w_loc, w_ctx = [], []
    for kv in range(D_KV):
        pair = m_hd[kv * D_GROUP] + m_hd[kv * D_GROUP + 1]
        w_loc.append(dv * pair + (1.0 - pair))
        w_ctx.append(jnp.where(feat == kv, vc_t, jnp.ones_like(vc_t)))

    def scores(c):
        kv, j = divmod(c, nq)
        q2 = jnp.concatenate([qm[kv * D_GROUP + g][j * SWA_QB:(j + 1) * SWA_QB] for g in range(D_GROUP)], axis=0)
        return [_dot(q2, kc_t),
                jnp.where(masks[j * SWA_QB - starts[j]], _dot_nt(q2, dk[starts[j]:starts[j] + SWA_SPAN]), NEG)]

    def pv(c, s):
        kv, j = divmod(c, nq)
        (e_ctx, e_loc), m = _exp2_parts(s, extra=sks[kv])
        p = _dot_nt(e_ctx, w_ctx[kv]) + _dot(e_loc, w_loc[kv][starts[j]:starts[j] + SWA_SPAN])
        return p * (1.0 / (_swap_halves(p) + jnp.exp2(sks[kv] - m)))

    o = _pipelined(D_KV * nq, scores, pv)
    os = [jnp.concatenate([o[kv * nq + j][g * SWA_QB:(g + 1) * SWA_QB] for j in range(nq)], axis=0)
          for kv in range(D_KV) for g in range(D_GROUP)]
    zd_ref[...] = (_pick_heads(os) * gd_ref[...].astype(F32)).astype(BF16)


def _lat_swa(proj, cache_k, cache_v, sink3, l, z_d):
    srow = lambda b: (N_CTX // DEC_SEQ + b)
    return pl.pallas_call(
        _lat_swa_kernel,
        grid=(DEC_BATCH,),
        in_specs=[pl.BlockSpec((DEC_SEQ, 256), lambda b: (srow(b), COL_DQ // 256)),
                  pl.BlockSpec((DEC_SEQ, 256), lambda b: (srow(b), COL_DK // 256)),
                  pl.BlockSpec((DEC_SEQ, 256), lambda b: (srow(b), COL_DV // 256)),
                  pl.BlockSpec((1, 1, D_KV, HD, PAST_LEN), lambda b: (b, l, 0, 0, 0)),
                  pl.BlockSpec((1, 1, D_KV, HD, PAST_LEN), lambda b: (b, l, 0, 0, 0)),
                  pl.BlockSpec((1, 1, D_HEADS), lambda b: (l, 0, 0)),
                  pl.BlockSpec((DEC_SEQ, 256), lambda b: (srow(b), 3)),
                  pl.BlockSpec(memory_space=pl.ANY)],
        out_specs=pl.BlockSpec((DEC_SEQ, BR_W), lambda b: (srow(b), 0)),
        out_shape=jax.ShapeDtypeStruct((N_TOK, BR_W), BF16),
        input_output_aliases={7: 0},
        compiler_params=_cparams(("arbitrary",)),
        name=f"lat_swa_l{l}",
    )(proj, proj, proj, cache_k, cache_v, sink3, proj, z_d)


def _lru_conv(x, cw, cb):
    seq = x.shape[0]
    x3 = x.reshape(seq // 8, 8, BR_W)
    tin = lax.broadcasted_iota(jnp.int32, (1, 8, 1), 1)
    zero = jnp.zeros((1, 8, BR_W), F32)

    def at(shift):
        r = pltpu.roll(x3, (-shift) % 8, 1)
        if shift < 0:
            return jnp.where(tin < -shift, jnp.concatenate([zero, r[:-1]], axis=0), r)
        return jnp.where(tin >= 8 - shift, jnp.concatenate([r[1:], zero], axis=0), r)

    xc = cb + at(-1) * cw[0:1] + x3 * cw[1:2] + at(1) * cw[2:3] + at(2) * cw[3:4]
    return xc.reshape(seq, BR_W)


def _lru_scan(d, xc, pre, lam_row, h0):
    seq = xc.shape[0]
    nl = -lam_row
    softplus = jnp.maximum(nl, 0.0) + jnp.log1p(jnp.exp(-jnp.abs(nl)))
    r = jax.nn.sigmoid(pre[:, :BR_W])
    ig = jax.nn.sigmoid(pre[:, BR_W:])
    log_a = -LRU_C * r * softplus
    a = jnp.exp(log_a)
    u = jnp.exp(0.5 * jnp.log(-jnp.tanh(log_a) * (a * a + 1.0))) * (ig * xc)
    tin = lax.broadcasted_iota(jnp.int32, (1, 8, 1), 1)
    a = a.reshape(seq // 8, 8, BR_W)
    u = u.reshape(seq // 8, 8, BR_W)
    for s in (1, 2, 4):
        if d == 0:
            keep = tin >= s
            a_n = jnp.where(keep, pltpu.roll(a, s, 1), 1.0)
            u_n = jnp.where(keep, pltpu.roll(u, s, 1), 0.0)
        else:
            keep = tin < 8 - s
            a_n = jnp.where(keep, pltpu.roll(a, 8 - s, 1), 1.0)
            u_n = jnp.where(keep, pltpu.roll(u, 8 - s, 1), 0.0)
        u = u + a * u_n
        a = a * a_n
    nt = seq // 8
    ys = [None] * nt
    h = h0
    for k in (range(nt) if d == 0 else range(nt - 1, -1, -1)):
        y = u[k] + a[k] * h
        ys[k] = y
        h = y[7:8, :] if d == 0 else y[0:1, :]
    return jnp.concatenate(ys, axis=0), h


def _lru_specs(l, imap):
    return [pl.BlockSpec((1, CONV_W, BR_W), imap), pl.BlockSpec((1, 1, BR_W), imap),
            pl.BlockSpec((1, BR_W, 4 * BR_W), imap), pl.BlockSpec((1, 1, 4 * BR_W), imap),
            pl.BlockSpec((1, 2, BR_W), imap)]


def _merge_kernel(layer, final, xa_ref, xb_ref, mod_ref, h_ref, za_ref, zrc_ref, yf_ref, yb_ref, gr_ref,
                  zc_ref, zd_ref, wmg_hbm, bmg_ref, wbo_ref, wo_ref, nf_ref, *refs):
    out_refs, (wmg_s, wbo_s, wo_s, stage, sem) = refs[:-5], refs[-5:]
    i = pl.program_id(0)

    def wmg_copy(chunk, n):
        slot = chunk % 2
        return pltpu.make_async_copy(wmg_hbm.at[layer, pl.ds(chunk * W_ROWS, W_ROWS), n, :],
                                     stage.at[slot, n], sem.at[slot, n])

    @pl.when(i == 0)
    def _():
        for n in range(N_BRANCH):
            wmg_copy(0, n).start()

    @pl.when(i < W_STEPS - 1)
    def _():
        for n in range(N_BRANCH):
            wmg_copy(i + 1, n).start()

    @pl.when(i < W_STEPS)
    def _():
        r0 = pl.multiple_of(i * W_ROWS, W_ROWS)
        for n in range(N_BRANCH):
            wmg_copy(i, n).wait()
            wmg_s[pl.ds(r0, W_ROWS), n * D_MODEL:(n + 1) * D_MODEL] = stage[i % 2, n].astype(BF16)
        _cast_rows(i, wbo_ref, wbo_s)
        _cast_rows(i, wo_ref, wo_s)

    @pl.when(i >= W_STEPS)
    def _():
        t = i - W_STEPS
        x = _load_x(t, xa_ref, xb_ref)
        gate = mod_ref[0][:, 2 * D_MODEL:]
        h = h_ref[...]
        z_r = jnp.where(t < N_CTX_TILES, zrc_ref[...],
                        ((yf_ref[...] + yb_ref[...]) * gr_ref[...].astype(F32)).astype(BF16))
        zs = [za_ref[...], z_r, zc_ref[...], zd_ref[...]]
        bmg = bmg_ref[0]
        cols = []
        for c in range(0, D_MODEL, 512):
            acc = None
            for n in range(N_BRANCH):
                g = jax.nn.sigmoid(_dot(h, wmg_s[:, n * D_MODEL + c:n * D_MODEL + c + 512])
                                   + bmg[n:n + 1, c:c + 512])
                term = g * _dot(zs[n], wbo_s[n * BR_W:(n + 1) * BR_W, c:c + 512])
                acc = term if acc is None else acc + term
            cols.append(acc.astype(BF16))
        merged = jnp.concatenate(cols, axis=1)
        xn = x + gate * _dot(merged, wo_s[...])
        if not final:
            out_refs[0][...] = xn
        else:
            y = xn * lax.rsqrt(jnp.mean(xn * xn, axis=-1, keepdims=True) + EPS) * nf_ref[...]

            @pl.when(t < N_CTX_TILES)
            def _():
                out_refs[0][...] = y

            @pl.when(t >= N_CTX_TILES)
            def _():
                out_refs[1][...] = y


def _merge(final, xa, xb, xb_off, mod3, l, h, proj, z_a, zr_ctx, y_f, y_b, z_c, z_d, w_mg, b_mg, w_bo, w_o, norm_f):
    row = lambda i: (_tile(i), 0)
    ctx_blk = lambda i: jnp.minimum(_tile(i), N_CTX_TILES - 1)
    lat_blk = lambda i: jnp.maximum(_tile(i) - N_CTX_TILES, 0)
    if final:
        out_specs = [pl.BlockSpec((TM, D_MODEL), lambda i: (ctx_blk(i), 0)),
                     pl.BlockSpec((TM, D_MODEL), lambda i: (lat_blk(i), 0))]
        out_shape = [jax.ShapeDtypeStruct((N_CTX, D_MODEL), F32), jax.ShapeDtypeStruct((N_LAT, D_MODEL), F32)]
    else:
        out_specs = pl.BlockSpec((TM, D_MODEL), row)
        out_shape = jax.ShapeDtypeStruct((N_TOK, D_MODEL), F32)
    wchunk = lambda width: pl.BlockSpec((1, W_ROWS, width), lambda i: (l, _wchunk(i), 0))
    return pl.pallas_call(
        functools.partial(_merge_kernel, l, final),
        grid=(W_STEPS + N_TILES,),
        in_specs=[
            pl.BlockSpec((TM, D_MODEL), lambda i: (ctx_blk(i), 0)),
            pl.BlockSpec((TM, D_MODEL), lambda i: (lat_blk(i) + xb_off, 0)),
            pl.BlockSpec((1, 1, 3 * D_MODEL), lambda i: (l * 16 + _mod_row(_tile(i)), 0, 0)),
            pl.BlockSpec((TM, D_MODEL), row),
            pl.BlockSpec((TM, BR_W), row),
            pl.BlockSpec((TM, BR_W), lambda i: (ctx_blk(i), 0)),
            pl.BlockSpec((TM, BR_W), lambda i: (lat_blk(i), 0)),
            pl.BlockSpec((TM, BR_W), lambda i: (lat_blk(i), 0)),
            pl.BlockSpec((TM, BR_W), lambda i: (_tile(i), 1)),
            pl.BlockSpec((TM, BR_W), row), pl.BlockSpec((TM, BR_W), row),
            pl.BlockSpec(memory_space=pl.ANY),
            pl.BlockSpec((1, N_BRANCH, D_MODEL), lambda i: (l, 0, 0)),
            wchunk(D_MODEL),
            wchunk(D_MODEL),
            pl.BlockSpec((1, D_MODEL), lambda i: (0, 0)),
        ],
        out_specs=out_specs,
        out_shape=out_shape,
        scratch_shapes=[pltpu.VMEM((D_MODEL, N_BRANCH * D_MODEL), BF16),
                        pltpu.VMEM((N_BRANCH * BR_W, D_MODEL), BF16),
                        pltpu.VMEM((D_MODEL, D_MODEL), BF16),
                        pltpu.VMEM((2, N_BRANCH, W_ROWS, D_MODEL), F32),
                        pltpu.SemaphoreType.DMA((2, N_BRANCH))],
        compiler_params=_cparams(("arbitrary",)),
        name=f"merge_l{l}",
    )(xa, xb, mod3, h, z_a, zr_ctx, y_f, y_b, proj, z_c, z_d,
      w_mg, b_mg,
      w_bo.reshape(DEPTH, N_BRANCH * BR_W, D_MODEL), w_o, norm_f.reshape(1, D_MODEL))


def _lru_gate_weights(lru_wa, lru_ba, lru_wx, lru_bx):
    w = jnp.stack([lru_wa[:, 0], lru_wx[:, 0], lru_wa[:, 1], lru_wx[:, 1]], axis=1)
    eye = jnp.eye(B_BLOCKS, dtype=w.dtype)
    dense = w[:, :, :, :, None, :] * eye[None, None, :, None, :, None]
    wg = dense.transpose(0, 2, 3, 1, 4, 5).reshape(DEPTH, BR_W, 4 * BR_W).astype(BF16)
    bg = jnp.stack([lru_ba[:, 0], lru_bx[:, 0], lru_ba[:, 1], lru_bx[:, 1]], axis=1).reshape(DEPTH, 1, 4 * BR_W)
    return wg, bg


def kernel(x_prompt, x_sample, cache_diff_k, cache_diff_v, cache_na_k, cache_na_v, cache_swa_k, cache_swa_v,
           state_lru, c, c_ctx, norm_g, w_ada, b_ada, w_in, diff_lambda, diff_norm_g, conv_w, conv_b,
           lru_wa, lru_ba, lru_wx, lru_bx, lru_lam, na_rpb, swa_sink, w_mg, b_mg, w_bo, w_o, norm_f):
    tab_a_np, _ = _rope_tables(A_QK)
    tab_d_np, _ = _rope_tables(HD)
    tab_a, tab_d = jnp.asarray(tab_a_np), jnp.asarray(tab_d_np)

    cvecs = jnp.concatenate([c_ctx[None, :], c, jnp.zeros((16 - 1 - DEC_BATCH, D_MODEL), F32)], axis=0)
    mod3 = _modulation(cvecs, w_ada, b_ada).reshape(DEPTH * 16, 1, 3 * D_MODEL)
    bias = _na_bias(na_rpb)
    wg, bg = _lru_gate_weights(lru_wa, lru_ba, lru_wx, lru_bx)
    norm_g3 = norm_g.reshape(DEPTH, 1, D_MODEL)
    dg3 = diff_norm_g.reshape(DEPTH, 1, HD)
    dg4 = jnp.tile(diff_norm_g, (1, A_HEADS)).reshape(DEPTH, 1, BR_W)
    sink3 = swa_sink.reshape(DEPTH, 1, D_HEADS)
    conv_b3 = conv_b.reshape(DEPTH, 1, BR_W)
    past = [jnp.swapaxes(t, -1, -2) for t in
            (cache_diff_k, cache_diff_v, cache_na_k, cache_na_v, cache_swa_k, cache_swa_v)]

    xa = x_prompt.reshape(N_CTX, D_MODEL)
    xb = x_sample.reshape(N_LAT, D_MODEL)
    xb_off = 0
    caches = states = None
    y_p = y_s = None
    for l in range(DEPTH):
        h, proj, bx, caches = _inproj(xa, xb, xb_off, mod3, l, norm_g3, w_in, tab_a, tab_d, caches)

        lru_params = (conv_w, conv_b3, wg, bg, lru_lam)
        z_a, z_c, z_d, zr_ctx, states = _ctx_attn(proj, bx, l, diff_lambda, dg4, sink3, lru_params, states)
        z_a = _lat_diff(proj, past[0], past[1], l, diff_lambda, dg3, z_a)
        z_c, y_f = _lat_na(0, proj, bx, past[2], past[3], bias, l, lru_params, state_lru, z_c)
        z_c, y_b = _lat_na(1, proj, bx, past[2], past[3], bias, l, lru_params, state_lru, z_c)
        z_d = _lat_swa(proj, past[4], past[5], sink3, l, z_d)

        final = l == DEPTH - 1
        out = _merge(final, xa, xb, xb_off, mod3, l, h, proj, z_a, zr_ctx, y_f, y_b, z_c, z_d,
                     w_mg, b_mg, w_bo, w_o, norm_f)
        if final:
            y_p, y_s = out
        else:
            xa = xb = out
            xb_off = N_CTX_TILES

    new_caches = [jnp.swapaxes(t, -1, -2) for t in caches]
    return (y_p.reshape(BATCH, SEQ, D_MODEL), y_s.reshape(DEC_BATCH, DEC_SEQ, D_MODEL), *new_caches, states)
```

```python
import functools
import math

import numpy as np
import jax
import jax.numpy as jnp
from jax import lax
from jax.experimental import pallas as pl
from jax.experimental.pallas import tpu as pltpu

F32 = jnp.float32
BF16 = jnp.bfloat16

D_MODEL = 1024
BATCH = 16
SEQ = 256
DEPTH = 2
DEC_BATCH = 8
DEC_SEQ = 1024
PAST_LEN = 512
GRID_W = 64
N_BRANCH = 4
BR_W = D_MODEL // 4
HD = 64
A_HEADS = BR_W // HD
A_QK = HD // 2
B_BLOCKS = 4
B_BLK = BR_W // B_BLOCKS
CONV_W = 4
LRU_C = 8.0
C_HEADS = BR_W // HD
NA_ROWS = 8
NA_COLS = 16
D_HEADS = BR_W // HD
D_KV = 2
D_GROUP = D_HEADS // D_KV
WIN = 128
ROPE_BASE = 10000.0
EPS = 1e-6
NEG = -1e30

N_CTX = BATCH * SEQ
N_LAT = DEC_BATCH * DEC_SEQ
N_TOK = N_CTX + N_LAT
GRID_ROWS = DEC_SEQ // GRID_W

TM = 512
N_CTX_TILES = N_CTX // TM
N_TILES = N_TOK // TM
LAT_TILES_PER_SEQ = DEC_SEQ // TM

PROJ_W = 3328
COL_G = 0
COL_AQ = 1024
COL_AK = 1280
COL_AV = 1536
COL_CQ = 1792
COL_CK = 2048
COL_CV = 2304
COL_DQ = 2560
COL_DK = 2816
COL_DV = 3072
IN_W = 3328
WS_W = IN_W + 2 * D_KV * HD

NA_HALF_Q = DEC_SEQ // 2
NA_KROWS = 12
NA_KSPAN = NA_KROWS * GRID_W
NA_HALF_KSTART = 4 * GRID_W

SWA_QB = 128
SWA_SPAN = SWA_QB + 2 * WIN

LOG2E = math.log2(math.e)
QA_SCALE = A_QK ** -0.5 * LOG2E
QH_SCALE = HD ** -0.5 * LOG2E

VMEM_LIMIT = 56 * 1024 * 1024


def _cparams(sem):
    return pltpu.CompilerParams(dimension_semantics=sem, vmem_limit_bytes=VMEM_LIMIT)


def _resident(shape, index_map):
    return pl.BlockSpec(shape, index_map, pipeline_mode=pl.Buffered(1))


def _dot(a, b):
    return jnp.dot(a, b, preferred_element_type=F32)


def _dot_nt(a, b):
    return lax.dot_general(a, b, (((1,), (1,)), ((), ())), preferred_element_type=F32)


def _silu(x):
    return x * jax.nn.sigmoid(x)


def _rope_tables(d):
    half = d // 2
    quarter = half // 2
    lane = np.arange(128)
    q = lane % d
    use_col = (q >= half)
    i = (q % half) % quarter
    first = (q % half) < quarter
    inv = (ROPE_BASE ** (-np.arange(quarter, dtype=np.float32) / np.float32(quarter))).astype(np.float32)
    t = np.arange(DEC_SEQ)
    pos = np.where(use_col[None, :], (t % GRID_W)[:, None], (t // GRID_W)[:, None]).astype(np.float32)
    ang = (pos * inv[i][None, :]).astype(np.float32)
    cos, sin = np.cos(ang), np.sin(ang)
    c = np.concatenate([np.ones((TM, 128), np.float32), cos], axis=0)
    s1 = np.concatenate([np.zeros((TM, 128), np.float32), np.where(first[None, :], -sin, 0.0)], axis=0)
    s2 = np.concatenate([np.zeros((TM, 128), np.float32), np.where(first[None, :], 0.0, sin)], axis=0)
    return np.stack([c, s1, s2]).astype(np.float32), quarter


def _rope128(x, tab_ref, shift):
    return (x * tab_ref[0] + pltpu.roll(x, 128 - shift, 1) * tab_ref[1]
            + pltpu.roll(x, shift, 1) * tab_ref[2])


def _rope(x, tab_ref, shift):
    w = x.shape[1]
    return jnp.concatenate([_rope128(x[:, c:c + 128], tab_ref, shift) for c in range(0, w, 128)], axis=1)


def _mod_kernel(c_ref, w_ref, b_ref, o_ref):
    c = c_ref[...]
    o_ref[0] = _dot(_silu(c).astype(BF16), w_ref[0].astype(BF16)) + b_ref[0]


def _modulation(cvecs, w_ada, b_ada):
    nb = 3 * D_MODEL // 768
    return pl.pallas_call(
        _mod_kernel,
        grid=(DEPTH, nb),
        in_specs=[pl.BlockSpec((16, D_MODEL), lambda l, j: (0, 0)),
                  pl.BlockSpec((1, D_MODEL, 768), lambda l, j: (l, 0, j)),
                  pl.BlockSpec((1, 1, 768), lambda l, j: (l, 0, j))],
        out_specs=pl.BlockSpec((1, 16, 768), lambda l, j: (l, 0, j)),
        out_shape=jax.ShapeDtypeStruct((DEPTH, 16, 3 * D_MODEL), F32),
        compiler_params=_cparams(("arbitrary", "arbitrary")),
        name="modulation",
    )(cvecs, w_ada, b_ada.reshape(DEPTH, 1, 3 * D_MODEL))


W_STEPS = 8
W_ROWS = D_MODEL // W_STEPS


def _tile(i):
    return jnp.maximum(i - W_STEPS, 0)


def _wchunk(i):
    return jnp.minimum(i, W_STEPS - 1)


def _mod_row(t):
    return jnp.where(t < N_CTX_TILES, 0, 1 + (t - N_CTX_TILES) // LAT_TILES_PER_SEQ)


def _rope_blk(t):
    return jnp.where(t < N_CTX_TILES, 0, 1 + (t - N_CTX_TILES) % LAT_TILES_PER_SEQ)


def _load_x(t, xa_ref, xb_ref):
    return jnp.where(t < N_CTX_TILES, xa_ref[...], xb_ref[...])


def _cast_rows(i, src_ref, dst_ref):
    r0 = pl.multiple_of(i * W_ROWS, W_ROWS)
    dst_ref[pl.ds(r0, W_ROWS), :] = src_ref[0].astype(BF16)


WI_G, WI_AQ, WI_AK, WI_AV, WI_BX, WI_CQ, WI_CKV, WI_DQ, WI_DKV = 0, 1024, 1280, 1536, 1792, 2048, 2304, 2816, 3072
SEQ_PER_TILE = TM // SEQ


def _store_heads(ref, val, n_heads, width):
    vt = val.T
    for s in range(SEQ_PER_TILE):
        for h in range(n_heads):
            ref[s, 0, h] = vt[h * width:(h + 1) * width, s * SEQ:(s + 1) * SEQ]


def _inproj_kernel(n_alias, xa_ref, xb_ref, mod_ref, g_ref, w_ref, ta_ref, td_ref, *refs):
    h_ref, proj_ref, bx_ref, dk_ref, dv_ref, nk_ref, nv_ref, sk_ref, sv_ref, w_s = refs[n_alias:]
    i = pl.program_id(0)

    @pl.when(i < W_STEPS)
    def _():
        r0 = pl.multiple_of(i * W_ROWS, W_ROWS)
        wb = w_ref[0].astype(BF16)
        w_s[pl.ds(r0, W_ROWS), 0:WI_DKV] = wb[:, 0:WI_DKV]
        for n, src in enumerate((0, 0, 1, 1, 2, 2, 3, 3)):
            w_s[pl.ds(r0, W_ROWS), WI_DKV + n * HD:WI_DKV + (n + 1) * HD] = \
                wb[:, WI_DKV + src * HD:WI_DKV + (src + 1) * HD]

    @pl.when(i >= W_STEPS)
    def _():
        t = i - W_STEPS
        is_ctx = t < N_CTX_TILES
        x = _load_x(t, xa_ref, xb_ref)
        mod = mod_ref[0]
        shift, scale = mod[:, :D_MODEL], mod[:, D_MODEL:2 * D_MODEL]
        xn = x * lax.rsqrt(jnp.mean(x * x, axis=-1, keepdims=True) + EPS) * g_ref[0]
        hb =(xn * (1.0 + scale) + shift).astype(BF16)
        h_ref[...] = hb

        def mm(c0, width):
            return _dot(hb, w_s[:, c0:c0 + width])

        gs = [mm(WI_G + c, 512) for c in range(0, 1024, 512)]
        aq, ak, av = mm(WI_AQ, 256), mm(WI_AK, 256), mm(WI_AV, 256)
        cq, ckv = mm(WI_CQ, 256), mm(WI_CKV, 512)
        dq, dk4, dv4 = mm(WI_DQ, 256), mm(WI_DKV, 256), mm(WI_DKV + 256, 256)
        bx_ref[...] = mm(WI_BX, 256)

        for n, g in enumerate(gs):
            proj_ref[:, COL_G + n * 512:COL_G + (n + 1) * 512] = _silu(g).astype(BF16)
        proj_ref[:, COL_AQ:COL_AQ + 256] = _rope(aq * QA_SCALE, ta_ref, A_QK // 4).astype(BF16)
        proj_ref[:, COL_AK:COL_AK + 256] = _rope(ak, ta_ref, A_QK // 4).astype(BF16)
        proj_ref[:, COL_AV:COL_AV + 256] = av.astype(BF16)
        proj_ref[:, COL_CQ:COL_CQ + 256] = (cq * QH_SCALE).astype(BF16)
        proj_ref[:, COL_CK:COL_CK + 512] = ckv.astype(BF16)
        proj_ref[:, COL_DQ:COL_DQ + 256] = _rope(dq * QH_SCALE, td_ref, HD // 4).astype(BF16)
        proj_ref[:, COL_DK:COL_DK + 256] = _rope(dk4, td_ref, HD // 4).astype(BF16)
        proj_ref[:, COL_DV:COL_DV + 256] = dv4.astype(BF16)

        @pl.when(is_ctx)
        def _():
            akt = ak.T
            for s in range(SEQ_PER_TILE):
                for h in range(A_HEADS):
                    for m in range(2):
                        c0 = h * HD + m * A_QK
                        dk_ref[s, 0, h, m] = akt[c0:c0 + A_QK, s * SEQ:(s + 1) * SEQ]
            _store_heads(dv_ref, av, A_HEADS, HD)
            _store_heads(nk_ref, ckv[:, :256], C_HEADS, HD)
            _store_heads(nv_ref, ckv[:, 256:], C_HEADS, HD)
            for ref, val in ((sk_ref, dk4), (sv_ref, dv4)):
                vt = val.T
                for s in range(SEQ_PER_TILE):
                    for kv in range(D_KV):
                        ref[s, 0, kv] = vt[kv * D_GROUP * HD:kv * D_GROUP * HD + HD, s * SEQ:(s + 1) * SEQ]


def _cache_shapes():
    hs = lambda n: (BATCH, DEPTH, n, HD, SEQ)
    return [(BATCH, DEPTH, A_HEADS, 2, A_QK, SEQ), hs(A_HEADS), hs(C_HEADS), hs(C_HEADS), hs(D_KV), hs(D_KV)]


def _inproj(xa, xb, xb_off, mod3, l, norm_g, w_in, tab_a, tab_d, caches):
    row = lambda i: (_tile(i), 0)
    ctx_blk = lambda i: jnp.minimum(_tile(i), N_CTX_TILES - 1)

    def cache_spec(shape):
        blk = (SEQ_PER_TILE, 1) + shape[2:]
        nz = len(shape) - 2
        return pl.BlockSpec(blk, lambda i: (ctx_blk(i), l) + (0,) * nz)

    cshapes = _cache_shapes()
    aliases = {} if caches is None else {7 + k: 3 + k for k in range(6)}
    extra_specs = [] if caches is None else [pl.BlockSpec(memory_space=pl.ANY)] * 6
    extra_args = [] if caches is None else list(caches)
    outs = pl.pallas_call(
        functools.partial(_inproj_kernel, len(extra_args)),
        grid=(W_STEPS + N_TILES,),
        in_specs=[
            pl.BlockSpec((TM, D_MODEL), lambda i: (ctx_blk(i), 0)),
            pl.BlockSpec((TM, D_MODEL), lambda i: (jnp.maximum(_tile(i) - N_CTX_TILES, 0) + xb_off, 0)),
            pl.BlockSpec((1, 1, 3 * D_MODEL), lambda i: (l * 16 + _mod_row(_tile(i)), 0, 0)),
            pl.BlockSpec((1, 1, D_MODEL), lambda i: (l, 0, 0)),
            pl.BlockSpec((1, W_ROWS, IN_W), lambda i: (l, _wchunk(i), 0)),
            pl.BlockSpec((3, TM, 128), lambda i: (0, _rope_blk(_tile(i)), 0)),
            pl.BlockSpec((3, TM, 128), lambda i: (0, _rope_blk(_tile(i)), 0)),
        ] + extra_specs,
        out_specs=[
            pl.BlockSpec((TM, D_MODEL), row),
            pl.BlockSpec((TM, PROJ_W), row),
            pl.BlockSpec((TM, BR_W), row),
        ] + [cache_spec(s) for s in cshapes],
        out_shape=[
            jax.ShapeDtypeStruct((N_TOK, D_MODEL), BF16),
            jax.ShapeDtypeStruct((N_TOK, PROJ_W), BF16),
            jax.ShapeDtypeStruct((N_TOK, BR_W), F32),
        ] + [jax.ShapeDtypeStruct(s, F32) for s in cshapes],
        scratch_shapes=[pltpu.VMEM((D_MODEL, WS_W), BF16)],
        input_output_aliases=aliases,
        compiler_params=_cparams(("arbitrary",)),
        name=f"inproj_l{l}",
    )(xa, xb, mod3, norm_g, w_in, tab_a, tab_d, *extra_args)
    return outs[0], outs[1], outs[2], tuple(outs[3:])


def _diff_lambda(lam_ref, lam_init):
    lv = lam_ref[0]
    s1 = jnp.sum(lv[0:1] * lv[1:2], axis=-1, keepdims=True)
    s2 = jnp.sum(lv[2:3] * lv[3:4], axis=-1, keepdims=True)
    return jnp.exp(s1) - jnp.exp(s2) + lam_init


def _diff_norm(o, dg_ref, lam_init):
    y = o * lax.rsqrt(jnp.mean(o * o, axis=-1, keepdims=True) + EPS) * dg_ref[0]
    return y * (1.0 - lam_init)


def _exp2_parts(parts, extra=None):
    m = functools.reduce(jnp.maximum, [jnp.max(s, axis=-1, keepdims=True) for s in parts])
    if extra is not None:
        m = jnp.maximum(m, extra)
    return [jnp.exp2(s - m).astype(BF16) for s in parts], m


def _pipelined(n, score_fn, pv_fn):
    outs = []
    nxt = score_fn(0)
    for c in range(n):
        cur, nxt = nxt, (score_fn(c + 1) if c + 1 < n else None)
        outs.append(pv_fn(c, cur))
    return outs


def _with_ones(v):
    return jnp.concatenate([v, jnp.ones_like(v)], axis=1)


def _with_ones_t(vt):
    return jnp.concatenate([vt, jnp.ones((16, vt.shape[1]), vt.dtype)], axis=0)


def _stack_group(dq, kv, rows):
    return jnp.concatenate([dq[:, (kv * D_GROUP + g) * HD:(kv * D_GROUP + g + 1) * HD]
                            for g in range(D_GROUP)], axis=0)


def _sink_col(sink2, kv, rows):
    r = lax.broadcasted_iota(jnp.int32, (D_GROUP * rows, 1), 0)
    return jnp.where(r < rows, sink2[:, kv * D_GROUP:kv * D_GROUP + 1], sink2[:, kv * D_GROUP + 1:kv * D_GROUP + 2])


CTX_PER_STEP = 2


def _lane_masks(n):
    lane = lax.broadcasted_iota(jnp.int32, (1, BR_W), 1)
    w = BR_W // n
    ms = [jnp.where((lane >= p * w) & (lane < (p + 1) * w), 1.0, 0.0).astype(BF16) for p in range(n)]
    return ms, [1.0 - m for m in ms]


def _swap_halves(p):
    return jnp.concatenate([p[:, BR_W // 2:], p[:, :BR_W // 2]], axis=1)


def _pick_heads(os):
    grp = lax.broadcasted_iota(jnp.int32, os[0].shape, 1) // HD
    acc = os[0]
    for h in range(1, len(os)):
        acc = jnp.where(grp == h, os[h], acc)
    return acc


def _head_rms(o, dg4_ref, lam_init):
    r = lax.broadcasted_iota(jnp.int32, (BR_W, BR_W), 0) // HD
    c = lax.broadcasted_iota(jnp.int32, (BR_W, BR_W), 1) // HD
    ss = _dot((o * o).astype(BF16), jnp.where(r == c, 1.0, 0.0).astype(BF16))
    return o * lax.rsqrt(ss * (1.0 / HD) + EPS) * dg4_ref[0] * (1.0 - lam_init)


def _ctx_attn_kernel(n_alias, lam_init, aq_ref, ak_ref, av_ref, cq_ref, ck_ref, cv_ref, dq_ref, dk_ref, dv_ref,
                     ga_ref, gc_ref, gd_ref, lam_ref, dg4_ref, sink_ref,
                     bx_ref, gr_ref, cw_ref, cb_ref, wg_ref, bg_ref, llam_ref, *refs):
    za_ref, zc_ref, zd_ref, zr_ref, fin_ref = refs[n_alias:]
    seqs = [slice(s * SEQ, (s + 1) * SEQ) for s in range(CTX_PER_STEP)]

    for s, rs in enumerate(seqs):
        xc = _lru_conv(bx_ref[rs, :], cw_ref[0], cb_ref[0])
        pre = _dot(xc.astype(BF16), wg_ref[0]) + bg_ref[0]
        h0 = jnp.zeros((1, BR_W), F32)
        yf, hf = _lru_scan(0, xc, pre[:, :2 * BR_W], llam_ref[0, 0:1, :], h0)
        yb, hb = _lru_scan(1, xc, pre[:, 2 * BR_W:], llam_ref[0, 1:2, :], h0)
        zr_ref[rs, :] = ((yf + yb) * gr_ref[rs, :].astype(F32)).astype(BF16)
        fin_ref[s, 0, 0:1, :] = hf
        fin_ref[s, 0, 1:2, :] = hb

    lam = _diff_lambda(lam_ref, lam_init)
    sink2 = sink_ref[0] * LOG2E
    m_qk, _ = _lane_masks(2 * A_HEADS)
    m_hd, o_hd = _lane_masks(C_HEADS)

    sa, sc, sd = [], [], []
    for rs in seqs:
        aq, cq, dq = aq_ref[rs, :], cq_ref[rs, :], dq_ref[rs, :]
        ak, ck, dk = ak_ref[rs, :], ck_ref[rs, :], dk_ref[rs, :]
        sa.append([_dot_nt(aq * m, ak) for m in m_qk])
        sc.append([_dot_nt(cq * m, ck) for m in m_hd])
        sd.append([_dot_nt(dq * m, dk) for m in m_hd])

    ea =[[_exp2_parts([s])[0][0] for s in ss] for ss in sa]
    ec = [[_exp2_parts([s])[0][0] for s in ss] for ss in sc]
    ed = [[_exp2_parts([s], extra=sink2[:, j:j + 1]) for j, s in enumerate(ss)] for ss in sd]

    for i, rs in enumerate(seqs):
        av, cv, dv = av_ref[rs, :], cv_ref[rs, :], dv_ref[rs, :]
        os = []
        for h in range(A_HEADS):
            w = av * m_hd[h] + o_hd[h]
            p0, p1 = _dot(ea[i][2 * h], w), _dot(ea[i][2 * h + 1], w)
            os.append(p0 * (1.0 / _swap_halves(p0)) - p1 * (lam / _swap_halves(p1)))
        za_ref[rs, :] = (_head_rms(_pick_heads(os), dg4_ref, lam_init) * ga_ref[rs, :].astype(F32)).astype(BF16)

        os = []
        for h in range(C_HEADS):
            p = _dot(ec[i][h], cv * m_hd[h] + o_hd[h])
            os.append(p * (1.0 / _swap_halves(p)))
        zc_ref[rs, :] = (_pick_heads(os) * gc_ref[rs, :].astype(F32)).astype(BF16)

        os = []
        for j in range(D_HEADS):
            (e,), m = ed[i][j]
            p = _dot(e, dv * m_hd[j] + o_hd[j])
            os.append(p * (1.0 / (_swap_halves(p) + jnp.exp2(sink2[:, j:j + 1] - m))))
        zd_ref[rs, :] = (_pick_heads(os) * gd_ref[rs, :].astype(F32)).astype(BF16)


def _ctx_attn(proj, bx, l, diff_lambda, dg4, sink3, lru_params, states):
    lam_init = 0.8 - 0.6 * math.exp(-0.3 * l)
    colblk = lambda cb: pl.BlockSpec((CTX_PER_STEP * SEQ, BR_W), lambda b: (b, cb))
    zspec = pl.BlockSpec((CTX_PER_STEP * SEQ, BR_W), lambda b: (b, 0))
    zshape = jax.ShapeDtypeStruct((N_CTX, BR_W), BF16)
    extra = [] if states is None else [states]
    return pl.pallas_call(
        functools.partial(_ctx_attn_kernel, len(extra), lam_init),
        grid=(BATCH // CTX_PER_STEP,),
        in_specs=[colblk(COL_AQ // 256), colblk(COL_AK // 256), colblk(COL_AV // 256),
                  colblk(COL_CQ // 256), colblk(COL_CK // 256), colblk(COL_CV // 256),
                  colblk(COL_DQ // 256), colblk(COL_DK // 256), colblk(COL_DV // 256),
                  colblk(0), colblk(2), colblk(3),
                  pl.BlockSpec((1, 4, A_QK), lambda b: (l, 0, 0)),
                  pl.BlockSpec((1, 1, BR_W), lambda b: (l, 0, 0)),
                  pl.BlockSpec((1, 1, D_HEADS), lambda b: (l, 0, 0)),
                  pl.BlockSpec((CTX_PER_STEP * SEQ, BR_W), lambda b: (b, 0)), colblk(1)]
                 + _lru_specs(l, lambda b: (l, 0, 0)) + [pl.BlockSpec(memory_space=pl.ANY)] * len(extra),
        out_specs=[zspec, zspec, zspec, zspec,
                   pl.BlockSpec((CTX_PER_STEP, 1, 2, BR_W), lambda b: (b, l, 0, 0))],
        out_shape=[zshape, zshape, zshape, zshape, jax.ShapeDtypeStruct((BATCH, DEPTH, 2, BR_W), F32)],
        input_output_aliases={} if states is None else {22: 4},
        compiler_params=_cparams(("arbitrary",)),
        name=f"ctx_attn_l{l}",
    )(*([proj] * 12), diff_lambda, dg4, sink3, bx, proj, *lru_params, *extra)


LA_QB = 256


def _lat_diff_kernel(lam_init, aq_ref, ak_ref, av_ref, ck_ref, cv_ref, ga_ref, lam_ref, dg_ref, za_ref):
    lam = _diff_lambda(lam_ref, lam_init)
    aq = aq_ref[...]
    va_ctx = [_with_ones_t(cv_ref[0, 0, h].astype(BF16)) for h in range(A_HEADS)]
    va_loc = [_with_ones(av_ref[:, h * HD:(h + 1) * HD]) for h in range(A_HEADS)]

    def scores(c):
        h, m = divmod(c, 2)
        lo = h * HD + m * A_QK
        q = aq[:, lo:lo + A_QK]
        return [_dot(q, ck_ref[0, 0, h, m].astype(BF16)),
                _dot_nt(q, ak_ref[:, lo:lo + A_QK])]

    def pv(c, s):
        (e_ctx, e_loc), _ = _exp2_parts(s)
        p_ctx, p_loc = _dot_nt(e_ctx, va_ctx[c // 2]), _dot(e_loc, va_loc[c // 2])
        return p_ctx[:, :HD] + p_loc[:, :HD], p_ctx[:, HD:HD + 1] + p_loc[:, HD:HD + 1]

    ss = [scores(c) for c in range(2 * A_HEADS)]
    nd = [pv(c, s) for c, s in enumerate(ss)]
    outs = []
    for h in range(A_HEADS):
        (n0, d0), (n1, d1) = nd[2 * h], nd[2 * h + 1]
        outs.append(_diff_norm(n0 * (1.0 / d0) - n1 * (lam / d1), dg_ref, lam_init))
    za_ref[...] = (jnp.concatenate(outs, axis=1) * ga_ref[...].astype(F32)).astype(BF16)


def _lat_diff(proj, cache_k, cache_v, l, diff_lambda, dg3):
    lam_init = 0.8 - 0.6 * math.exp(-0.3 * l)
    nq = DEC_SEQ // LA_QB
    qrow = lambda b, j: N_CTX // LA_QB + b * nq + j
    srow = lambda b, j: N_CTX // DEC_SEQ + b
    return pl.pallas_call(
        functools.partial(_lat_diff_kernel, lam_init),
        grid=(DEC_BATCH, nq),
        in_specs=[pl.BlockSpec((LA_QB, BR_W), lambda b, j: (qrow(b, j), COL_AQ // BR_W)),
                  pl.BlockSpec((DEC_SEQ, BR_W), lambda b, j: (srow(b, j), COL_AK // BR_W)),
                  pl.BlockSpec((DEC_SEQ, BR_W), lambda b, j: (srow(b, j), COL_AV // BR_W)),
                  pl.BlockSpec((1, 1, A_HEADS, 2, A_QK, PAST_LEN), lambda b, j: (b, l, 0, 0, 0, 0)),
                  pl.BlockSpec((1, 1, A_HEADS, HD, PAST_LEN), lambda b, j: (b, l, 0, 0, 0)),
                  pl.BlockSpec((LA_QB, BR_W), lambda b, j: (qrow(b, j), 0)),
                  pl.BlockSpec((1, 4, A_QK), lambda b, j: (l, 0, 0)),
                  pl.BlockSpec((1, 1, HD), lambda b, j: (l, 0, 0))],
        out_specs=pl.BlockSpec((LA_QB, BR_W), lambda b, j: (b * nq + j, 0)),
        out_shape=jax.ShapeDtypeStruct((N_LAT, BR_W), BF16),
        compiler_params=_cparams(("arbitrary", "arbitrary")),
        name=f"lat_diff_l{l}",
    )(proj, proj, proj, cache_k, cache_v, proj, diff_lambda, dg3)


def _na_bias_kernel(rpb_ref, o_ref):
    l, h = pl.program_id(0), pl.program_id(1)
    base = (l * C_HEADS + h) * ((2 * NA_ROWS - 1) * (2 * NA_COLS - 1))
    qc = lax.broadcasted_iota(jnp.int32, (GRID_W, GRID_W), 0)
    kc = lax.broadcasted_iota(jnp.int32, (GRID_W, GRID_W), 1)
    cs = jnp.clip(qc - NA_COLS // 2, 0, GRID_W - NA_COLS)
    ok = (kc >= cs) & (kc < cs + NA_COLS)
    dcol = kc - qc + (NA_COLS - 1)
    neg = jnp.full((GRID_W, GRID_W), NEG, F32)
    tabs = []
    for d in range(2 * NA_ROWS - 1):
        t = neg
        for dc in range(2 * NA_COLS - 1):
            t = jnp.where(ok & (dcol == dc), rpb_ref[base + d * (2 * NA_COLS - 1) + dc] * LOG2E, t)
        tabs.append(t)
    for half in range(2):
        for rq in range(NA_ROWS):
            r = half * NA_ROWS + rq
            rs = min(max(r - NA_ROWS // 2, 0), GRID_ROWS - NA_ROWS)
            blks = []
            for kr in range(NA_KROWS):
                kabs = half * 4 + kr
                blks.append(tabs[kabs - r + NA_ROWS - 1] if rs <= kabs < rs + NA_ROWS else neg)
            o_ref[0, 0, half, rq * GRID_W:(rq + 1) * GRID_W, :] = jnp.concatenate(blks, axis=1)


def _na_bias(na_rpb):
    return pl.pallas_call(
        _na_bias_kernel,
        grid=(DEPTH, C_HEADS),
        in_specs=[pl.BlockSpec(memory_space=pltpu.SMEM)],
        out_specs=pl.BlockSpec((1, 1, 2, NA_HALF_Q, NA_KSPAN), lambda l, h: (l, h, 0, 0, 0)),
        out_shape=jax.ShapeDtypeStruct((DEPTH, C_HEADS, 2, NA_HALF_Q, NA_KSPAN), F32),
        compiler_params=_cparams(("arbitrary", "arbitrary")),
        name="na_bias",
    )(na_rpb.reshape(-1))


def _lat_na_kernel(half, cq_ref, ck_ref, cv_ref, kc_ref, vc_ref, bias_ref, gc_ref,
                   bx_ref, cw_ref, cb_ref, wg_ref, bg_ref, llam_ref, st_ref, zc_ref, y_ref):
    k0 = half * NA_HALF_KSTART
    cq = cq_ref[...]
    kl_all = ck_ref[k0:k0 + NA_KSPAN, :]
    vl_all = cv_ref[k0:k0 + NA_KSPAN, :]

    def scores(h):
        q = cq[:, h * HD:(h + 1) * HD]
        return [_dot(q, kc_ref[0, 0, h].astype(BF16)),
                _dot_nt(q, kl_all[:, h * HD:(h + 1) * HD]) + bias_ref[0, h, 0]]

    def pv(h, s):
        (e_ctx, e_loc), _ = _exp2_parts(s)
        p_ctx = _dot_nt(e_ctx, _with_ones_t(vc_ref[0, 0, h].astype(BF16)))
        p_loc = _dot(e_loc, _with_ones(vl_all[:, h * HD:(h + 1) * HD]))
        return (p_ctx[:, :HD] + p_loc[:, :HD]) * (1.0 / (p_ctx[:, HD:HD + 1] + p_loc[:, HD:HD + 1]))

    xc = _lru_conv(bx_ref[...], cw_ref[0], cb_ref[0])
    pre = _dot(xc.astype(BF16), wg_ref[0, :, half * 2 * BR_W:(half + 1) * 2 * BR_W]) \
        + bg_ref[0, :, half * 2 * BR_W:(half + 1) * 2 * BR_W]
    y_ref[...], _ = _lru_scan(half, xc, pre, llam_ref[0, half:half + 1, :], st_ref[0, 0, half:half + 1, :])

    ss = [scores(h) for h in range(C_HEADS)]
    outs = [pv(h, s) for h, s in enumerate(ss)]
    zc_ref[...] = (jnp.concatenate(outs, axis=1) * gc_ref[...].astype(F32)).astype(BF16)


def _lat_na(half, proj, bx, cache_k, cache_v, bias, l, lru_params, state_lru):
    qrow = lambda b: N_CTX // NA_HALF_Q + b * 2 + half
    srow = lambda b: N_CTX // DEC_SEQ + b
    return pl.pallas_call(
        functools.partial(_lat_na_kernel, half),
        grid=(DEC_BATCH,),
        in_specs=[pl.BlockSpec((NA_HALF_Q, BR_W), lambda b: (qrow(b), COL_CQ // BR_W)),
                  pl.BlockSpec((DEC_SEQ, BR_W), lambda b: (srow(b), COL_CK // BR_W)),
                  pl.BlockSpec((DEC_SEQ, BR_W), lambda b: (srow(b), COL_CV // BR_W)),
                  pl.BlockSpec((1, 1, C_HEADS, HD, PAST_LEN), lambda b: (b, l, 0, 0, 0)),
                  pl.BlockSpec((1, 1, C_HEADS, HD, PAST_LEN), lambda b: (b, l, 0, 0, 0)),
                  pl.BlockSpec((1, C_HEADS, 1, NA_HALF_Q, NA_KSPAN), lambda b: (l, 0, half, 0, 0)),
                  pl.BlockSpec((NA_HALF_Q, BR_W), lambda b: (qrow(b), 2)),
                  pl.BlockSpec((DEC_SEQ, BR_W), lambda b: (srow(b), 0))]
                 + _lru_specs(l, lambda b: (l, 0, 0))
                 + [pl.BlockSpec((1, 1, 2, BR_W), lambda b: (b, l, 0, 0))],
        out_specs=[pl.BlockSpec((NA_HALF_Q, BR_W), lambda b: (b, 0)),
                   pl.BlockSpec((DEC_SEQ, BR_W), lambda b: (b, 0))],
        out_shape=[jax.ShapeDtypeStruct((N_LAT // 2, BR_W), BF16), jax.ShapeDtypeStruct((N_LAT, BR_W), F32)],
        compiler_params=_cparams(("arbitrary",)),
        name=f"lat_na{half}_l{l}",
    )(proj, proj, proj, cache_k, cache_v, bias, proj, bx, *lru_params, state_lru)


def _lat_swa_kernel(dq_ref, dk_ref, dv_ref, kc_ref, vc_ref, sink_ref, gd_ref, zd_ref):
    nq = DEC_SEQ // SWA_QB
    rows = D_GROUP * SWA_QB
    dq, dk, dv = dq_ref[...], dk_ref[...], dv_ref[...]
    sink2 = sink_ref[0] * LOG2E
    m_hd, _ = _lane_masks(D_HEADS)
    starts = [min(max(j * SWA_QB - WIN, 0), DEC_SEQ - SWA_SPAN) for j in range(nq)]
    qi = lax.broadcasted_iota(jnp.int32, (rows, SWA_SPAN), 0) % SWA_QB
    ki = lax.broadcasted_iota(jnp.int32, (rows, SWA_SPAN), 1)
    masks = {off: jnp.abs(qi + off - ki) <= WIN for off in sorted({j * SWA_QB - s for j, s in enumerate(starts)})}
    kc_t = jnp.concatenate([kc_ref[0, 0, kv] for kv in range(D_KV) for _ in range(D_GROUP)], axis=0).astype(BF16)
    vc_t = jnp.concatenate([vc_ref[0, 0, kv] for kv in range(D_KV) for _ in range(D_GROUP)], axis=0).astype(BF16)
    feat = lax.broadcasted_iota(jnp.int32, (BR_W, 1), 0) // (D_GROUP * HD)

    qm = [dq * m for m in m_hd]
    sks = [_sink_col(sink2, kv, SWA_QB) for kv in range(D_KV)]
    w_loc, w_ctx = [], []
    for kv in range(D_KV):
        pair = m_hd[kv * D_GROUP] + m_hd[kv * D_GROUP + 1]
        w_loc.append(dv * pair + (1.0 - pair))
        w_ctx.append(jnp.where(feat == kv, vc_t, jnp.ones_like(vc_t)))

    def scores(c):
        kv, j = divmod(c, nq)
        q2 = jnp.concatenate([qm[kv * D_GROUP + g][j * SWA_QB:(j + 1) * SWA_QB] for g in range(D_GROUP)], axis=0)
        return [_dot(q2, kc_t),
                jnp.where(masks[j * SWA_QB - starts[j]], _dot_nt(q2, dk[starts[j]:starts[j] + SWA_SPAN]), NEG)]

    def pv(c, s):
        kv, j = divmod(c, nq)
        (e_ctx, e_loc), m = _exp2_parts(s, extra=sks[kv])
        p = _dot_nt(e_ctx, w_ctx[kv]) + _dot(e_loc, w_loc[kv][starts[j]:starts[j] + SWA_SPAN])
        return p * (1.0 / (_swap_halves(p) + jnp.exp2(sks[kv] - m)))

    o = _pipelined(D_KV * nq, scores, pv)
    os = [jnp.concatenate([o[kv * nq + j][g * SWA_QB:(g + 1) * SWA_QB] for j in range(nq)], axis=0)
          for kv in range(D_KV) for g in range(D_GROUP)]
    zd_ref[...] = (_pick_heads(os) * gd_ref[...].astype(F32)).astype(BF16)


def _lat_swa(proj, cache_k, cache_v, sink3, l):
    srow = lambda b: (N_CTX // DEC_SEQ + b)
    return pl.pallas_call(
        _lat_swa_kernel,
        grid=(DEC_BATCH,),
        in_specs=[pl.BlockSpec((DEC_SEQ, BR_W), lambda b: (srow(b), COL_DQ // BR_W)),
                  pl.BlockSpec((DEC_SEQ, BR_W), lambda b: (srow(b), COL_DK // BR_W)),
                  pl.BlockSpec((DEC_SEQ, BR_W), lambda b: (srow(b), COL_DV // BR_W)),
                  pl.BlockSpec((1, 1, D_KV, HD, PAST_LEN), lambda b: (b, l, 0, 0, 0)),
                  pl.BlockSpec((1, 1, D_KV, HD, PAST_LEN), lambda b: (b, l, 0, 0, 0)),
                  pl.BlockSpec((1, 1, D_HEADS), lambda b: (l, 0, 0)),
                  pl.BlockSpec((DEC_SEQ, BR_W), lambda b: (srow(b), 3))],
        out_specs=pl.BlockSpec((DEC_SEQ, BR_W), lambda b: (b, 0)),
        out_shape=jax.ShapeDtypeStruct((N_LAT, BR_W), BF16),
        compiler_params=_cparams(("arbitrary",)),
        name=f"lat_swa_l{l}",
    )(proj, proj, proj, cache_k, cache_v, sink3, proj)


def _lru_conv(x, cw, cb):
    seq = x.shape[0]
    x3 = x.reshape(seq // 8, 8, BR_W)
    tin = lax.broadcasted_iota(jnp.int32, (1, 8, 1), 1)
    zero = jnp.zeros((1, 8, BR_W), F32)

    def at(shift):
        r = pltpu.roll(x3, (-shift) % 8, 1)
        if shift < 0:
            return jnp.where(tin < -shift, jnp.concatenate([zero, r[:-1]], axis=0), r)
        return jnp.where(tin >= 8 - shift, jnp.concatenate([r[1:], zero], axis=0), r)

    xc = cb + at(-1) * cw[0:1] + x3 * cw[1:2] + at(1) * cw[2:3] + at(2) * cw[3:4]
    return xc.reshape(seq, BR_W)


def _lru_scan(d, xc, pre, lam_row, h0):
    seq = xc.shape[0]
    nl = -lam_row
    softplus = jnp.maximum(nl, 0.0) + jnp.log1p(jnp.exp(-jnp.abs(nl)))
    r = jax.nn.sigmoid(pre[:, :BR_W])
    ig = jax.nn.sigmoid(pre[:, BR_W:])
    log_a = -LRU_C * r * softplus
    a = jnp.exp(log_a)
    u = jnp.exp(0.5 * jnp.log(-jnp.tanh(log_a) * (a * a + 1.0))) * (ig * xc)
    tin = lax.broadcasted_iota(jnp.int32, (1, 8, 1), 1)
    a = a.reshape(seq // 8, 8, BR_W)
    u = u.reshape(seq // 8, 8, BR_W)
    for s in (1, 2, 4):
        if d == 0:
            keep = tin >= s
            a_n = jnp.where(keep, pltpu.roll(a, s, 1), 1.0)
            u_n = jnp.where(keep, pltpu.roll(u, s, 1), 0.0)
        else:
            keep = tin < 8 - s
            a_n = jnp.where(keep, pltpu.roll(a, 8 - s, 1), 1.0)
            u_n = jnp.where(keep, pltpu.roll(u, 8 - s, 1), 0.0)
        u = u + a * u_n
        a = a * a_n
    nt = seq // 8
    ys = [None] * nt
    h = h0
    for k in (range(nt) if d == 0 else range(nt - 1, -1, -1)):
        y = u[k] + a[k] * h
        ys[k] = y
        h = y[7:8, :] if d == 0 else y[0:1, :]
    return jnp.concatenate(ys, axis=0), h


def _lru_specs(l, imap):
    return [pl.BlockSpec((1, CONV_W, BR_W), imap), pl.BlockSpec((1, 1, BR_W), imap),
            pl.BlockSpec((1, BR_W, 4 * BR_W), imap), pl.BlockSpec((1, 1, 4 * BR_W), imap),
            pl.BlockSpec((1, 2, BR_W), imap)]


def _merge_kernel(layer, final, xa_ref, xb_ref, mod_ref, h_ref, zac_ref, zal_ref, zrc_ref, yf_ref, yb_ref, gr_ref,
                  zcc_ref, zc0_ref, zc1_ref, zdc_ref, zdl_ref, wmg_hbm, bmg_ref, wbo_ref, wo_ref, nf_ref, *refs):
    out_refs, (wmg_s, wbo_s, wo_s, stage, sem) = refs[:-5], refs[-5:]
    i = pl.program_id(0)

    def wmg_copy(chunk, n):
        slot = chunk % 2
        return pltpu.make_async_copy(wmg_hbm.at[layer, pl.ds(chunk * W_ROWS, W_ROWS), n, :],
                                     stage.at[slot, n], sem.at[slot, n])

    @pl.when(i == 0)
    def _():
        for n in range(N_BRANCH):
            wmg_copy(0, n).start()

    @pl.when(i < W_STEPS - 1)
    def _():
        for n in range(N_BRANCH):
            wmg_copy(i + 1, n).start()

    @pl.when(i < W_STEPS)
    def _():
        r0 = pl.multiple_of(i * W_ROWS, W_ROWS)
        for n in range(N_BRANCH):
            wmg_copy(i, n).wait()
            wmg_s[pl.ds(r0, W_ROWS), n * D_MODEL:(n + 1) * D_MODEL] = stage[i % 2, n].astype(BF16)
        _cast_rows(i, wbo_ref, wbo_s)
        _cast_rows(i, wo_ref, wo_s)

    @pl.when(i >= W_STEPS)
    def _():
        t = i - W_STEPS
        x = _load_x(t, xa_ref, xb_ref)
        gate = mod_ref[0][:, 2 * D_MODEL:]
        h = h_ref[...]
        is_ctx = t < N_CTX_TILES
        second_half = (t - N_CTX_TILES) % LAT_TILES_PER_SEQ == 1
        z_r = ((yf_ref[...] + yb_ref[...]) * gr_ref[...].astype(F32)).astype(BF16)
        zs = [jnp.where(is_ctx, zac_ref[...], zal_ref[...]),
              jnp.where(is_ctx, zrc_ref[...], z_r),
              jnp.where(is_ctx, zcc_ref[...], jnp.where(second_half, zc1_ref[...], zc0_ref[...])),
              jnp.where(is_ctx, zdc_ref[...], zdl_ref[...])]
        bmg = bmg_ref[0]
        cols = []
        for c in range(0, D_MODEL, 512):
            acc = None
            for n in range(N_BRANCH):
                g = jax.nn.sigmoid(_dot(h, wmg_s[:, n * D_MODEL + c:n * D_MODEL + c + 512])
                                   + bmg[n:n + 1, c:c + 512])
                term = g * _dot(zs[n], wbo_s[n * BR_W:(n + 1) * BR_W, c:c + 512])
                acc = term if acc is None else acc + term
            cols.append(acc.astype(BF16))
        merged = jnp.concatenate(cols, axis=1)
        xn = x + gate * _dot(merged, wo_s[...])
        if not final:
            out_refs[0][...] = xn
        else:
            y = xn * lax.rsqrt(jnp.mean(xn * xn, axis=-1, keepdims=True) + EPS) * nf_ref[...]

            @pl.when(t < N_CTX_TILES)
            def _():
                out_refs[0][...] = y

            @pl.when(t >= N_CTX_TILES)
            def _():
                out_refs[1][...] = y


def _merge(final, xa, xb, xb_off, mod3, l, h, proj, z_ctx, za_lat, y_f, y_b, zc_lat, zd_lat,
           w_mg, b_mg, w_bo, w_o, norm_f):
    assert TM == NA_HALF_Q
    row = lambda i: (_tile(i), 0)
    ctx_blk = lambda i: jnp.minimum(_tile(i), N_CTX_TILES - 1)
    lat_blk = lambda i: jnp.maximum(_tile(i) - N_CTX_TILES, 0)
    ctx_spec = pl.BlockSpec((TM, BR_W), lambda i: (ctx_blk(i), 0))
    lat_spec = pl.BlockSpec((TM, BR_W), lambda i: (lat_blk(i), 0))
    half_spec = pl.BlockSpec((TM, BR_W), lambda i: (lat_blk(i) // LAT_TILES_PER_SEQ, 0))
    if final:
        out_specs = [pl.BlockSpec((TM, D_MODEL), lambda i: (ctx_blk(i), 0)),
                     pl.BlockSpec((TM, D_MODEL), lambda i: (lat_blk(i), 0))]
        out_shape = [jax.ShapeDtypeStruct((N_CTX, D_MODEL), F32), jax.ShapeDtypeStruct((N_LAT, D_MODEL), F32)]
    else:
        out_specs = pl.BlockSpec((TM, D_MODEL), row)
        out_shape = jax.ShapeDtypeStruct((N_TOK, D_MODEL), F32)
    wchunk = lambda width: pl.BlockSpec((1, W_ROWS, width), lambda i: (l, _wchunk(i), 0))
    return pl.pallas_call(
        functools.partial(_merge_kernel, l, final),
        grid=(W_STEPS + N_TILES,),
        in_specs=[
            pl.BlockSpec((TM, D_MODEL), lambda i: (ctx_blk(i), 0)),
            pl.BlockSpec((TM, D_MODEL), lambda i: (lat_blk(i) + xb_off, 0)),
            pl.BlockSpec((1, 1, 3 * D_MODEL), lambda i: (l * 16 + _mod_row(_tile(i)), 0, 0)),
            pl.BlockSpec((TM, D_MODEL), row),
            ctx_spec, lat_spec,
            ctx_spec, lat_spec, lat_spec,
            pl.BlockSpec((TM, BR_W), lambda i: (_tile(i), 1)),
            ctx_spec, half_spec, half_spec,
            ctx_spec, lat_spec,
            pl.BlockSpec(memory_space=pl.ANY),
            pl.BlockSpec((1, N_BRANCH, D_MODEL), lambda i: (l, 0, 0)),
            wchunk(D_MODEL),
            wchunk(D_MODEL),
            pl.BlockSpec((1, D_MODEL), lambda i: (0, 0)),
        ],
        out_specs=out_specs,
        out_shape=out_shape,
        scratch_shapes=[pltpu.VMEM((D_MODEL, N_BRANCH * D_MODEL), BF16),
                        pltpu.VMEM((N_BRANCH * BR_W, D_MODEL), BF16),
                        pltpu.VMEM((D_MODEL, D_MODEL), BF16),
                        pltpu.VMEM((2, N_BRANCH, W_ROWS, D_MODEL), F32),
                        pltpu.SemaphoreType.DMA((2, N_BRANCH))],
        compiler_params=_cparams(("arbitrary",)),
        name=f"merge_l{l}",
    )(xa, xb, mod3, h, z_ctx[0], za_lat, z_ctx[3], y_f, y_b, proj, z_ctx[1], zc_lat[0], zc_lat[1], z_ctx[2], zd_lat,
      w_mg, b_mg,
      w_bo.reshape(DEPTH, N_BRANCH * BR_W, D_MODEL), w_o, norm_f.reshape(1, D_MODEL))


def _lru_gate_weights(lru_wa, lru_ba, lru_wx, lru_bx):
    w = jnp.stack([lru_wa[:, 0], lru_wx[:, 0], lru_wa[:, 1], lru_wx[:, 1]], axis=1)
    eye = jnp.eye(B_BLOCKS, dtype=w.dtype)
    dense = w[:, :, :, :, None, :] * eye[None, None, :, None, :, None]
    wg = dense.transpose(0, 2, 3, 1, 4, 5).reshape(DEPTH, BR_W, 4 * BR_W).astype(BF16)
    bg = jnp.stack([lru_ba[:, 0], lru_bx[:, 0], lru_ba[:, 1], lru_bx[:, 1]], axis=1).reshape(DEPTH, 1, 4 * BR_W)
    return wg, bg


def kernel(x_prompt, x_sample, cache_diff_k, cache_diff_v, cache_na_k, cache_na_v, cache_swa_k, cache_swa_v,
           state_lru, c, c_ctx, norm_g, w_ada, b_ada, w_in, diff_lambda, diff_norm_g, conv_w, conv_b,
           lru_wa, lru_ba, lru_wx, lru_bx, lru_lam, na_rpb, swa_sink, w_mg, b_mg, w_bo, w_o, norm_f):
    tab_a_np, _ = _rope_tables(A_QK)
    tab_d_np, _ = _rope_tables(HD)
    tab_a, tab_d = jnp.asarray(tab_a_np), jnp.asarray(tab_d_np)

    cvecs = jnp.concatenate([c_ctx[None, :], c, jnp.zeros((16 - 1 - DEC_BATCH, D_MODEL), F32)], axis=0)
    mod3 = _modulation(cvecs, w_ada, b_ada).reshape(DEPTH * 16, 1, 3 * D_MODEL)
    bias = _na_bias(na_rpb)
    wg, bg = _lru_gate_weights(lru_wa, lru_ba, lru_wx, lru_bx)
    norm_g3 = norm_g.reshape(DEPTH, 1, D_MODEL)
    dg3 = diff_norm_g.reshape(DEPTH, 1, HD)
    dg4 = jnp.tile(diff_norm_g, (1, A_HEADS)).reshape(DEPTH, 1, BR_W)
    sink3 = swa_sink.reshape(DEPTH, 1, D_HEADS)
    conv_b3 = conv_b.reshape(DEPTH, 1, BR_W)
    past = [jnp.swapaxes(t, -1, -2) for t in
            (cache_diff_k, cache_diff_v, cache_na_k, cache_na_v, cache_swa_k, cache_swa_v)]

    xa = x_prompt.reshape(N_CTX, D_MODEL)
    xb = x_sample.reshape(N_LAT, D_MODEL)
    xb_off = 0
    caches = states = None
    y_p = y_s = None
    for l in range(DEPTH):
        h, proj, bx, caches = _inproj(xa, xb, xb_off, mod3, l, norm_g3, w_in, tab_a, tab_d, caches)

        lru_params = (conv_w, conv_b3, wg, bg, lru_lam)
        *z_ctx, states = _ctx_attn(proj, bx, l, diff_lambda, dg4, sink3, lru_params, states)
        za_lat = _lat_diff(proj, past[0], past[1], l, diff_lambda, dg3)
        zc0, y_f = _lat_na(0, proj, bx, past[2], past[3], bias, l, lru_params, state_lru)
        zc1, y_b = _lat_na(1, proj, bx, past[2], past[3], bias, l, lru_params, state_lru)
        zd_lat = _lat_swa(proj, past[4], past[5], sink3, l)

        final = l == DEPTH - 1
        out = _merge(final, xa, xb, xb_off, mod3, l, h, proj, z_ctx, za_lat, y_f, y_b, (zc0, zc1), zd_lat,
                     w_mg, b_mg, w_bo, w_o, norm_f)
        if final:
            y_p, y_s = out
        else:
            xa = xb = out
            xb_off = N_CTX_TILES

    new_caches = [jnp.swapaxes(t, -1, -2) for t in caches]
    return (y_p.reshape(BATCH, SEQ, D_MODEL), y_s.reshape(DEC_BATCH, DEC_SEQ, D_MODEL), *new_caches, states)
```

```python
import functools
import math

import numpy as np
import jax
import jax.numpy as jnp
from jax import lax
from jax.experimental import pallas as pl
from jax.experimental.pallas import tpu as pltpu

F32 = jnp.float32
BF16 = jnp.bfloat16

D_MODEL = 1024
BATCH = 16
SEQ = 256
DEPTH = 2
DEC_BATCH = 8
DEC_SEQ = 1024
PAST_LEN = 512
GRID_W = 64
N_BRANCH = 4
BR_W = D_MODEL // 4
HD = 64
A_HEADS = BR_W // HD
A_QK = HD // 2
B_BLOCKS = 4
B_BLK = BR_W // B_BLOCKS
CONV_W = 4
LRU_C = 8.0
C_HEADS = BR_W // HD
NA_ROWS = 8
NA_COLS = 16
D_HEADS = BR_W // HD
D_KV = 2
D_GROUP = D_HEADS // D_KV
WIN = 128
ROPE_BASE = 10000.0
EPS = 1e-6
NEG = -1e30

N_CTX = BATCH * SEQ
N_LAT = DEC_BATCH * DEC_SEQ
N_TOK = N_CTX + N_LAT
GRID_ROWS = DEC_SEQ // GRID_W

TM = 512
N_CTX_TILES = N_CTX // TM
N_TILES = N_TOK // TM
LAT_TILES_PER_SEQ = DEC_SEQ // TM

PROJ_W = 3328
COL_G = 0
COL_AQ = 1024
COL_AK = 1280
COL_AV = 1536
COL_CQ = 1792
COL_CK = 2048
COL_CV = 2304
COL_DQ = 2560
COL_DK = 2816
COL_DV = 3072
IN_W = 3328
WS_W = IN_W + 2 * D_KV * HD

NA_HALF_Q = DEC_SEQ // 2
NA_KROWS = 12
NA_KSPAN = NA_KROWS * GRID_W
NA_HALF_KSTART = 4 * GRID_W

SWA_QB = 128
SWA_SPAN = SWA_QB + 2 * WIN

LOG2E = math.log2(math.e)
QA_SCALE = A_QK ** -0.5 * LOG2E
QH_SCALE = HD ** -0.5 * LOG2E

VMEM_LIMIT = 56 * 1024 * 1024


def _cparams(sem):
    return pltpu.CompilerParams(dimension_semantics=sem, vmem_limit_bytes=VMEM_LIMIT)


def _dot(a, b):
    return jnp.dot(a, b, preferred_element_type=F32)


def _dot_nt(a, b):
    return lax.dot_general(a, b, (((1,), (1,)), ((), ())), preferred_element_type=F32)


def _silu(x):
    return x * jax.nn.sigmoid(x)


def _rope_tables(d):
    half = d // 2
    quarter = half // 2
    lane = np.arange(128)
    q = lane % d
    use_col = (q >= half)
    i = (q % half) % quarter
    first = (q % half) < quarter
    inv = (ROPE_BASE ** (-np.arange(quarter, dtype=np.float32) / np.float32(quarter))).astype(np.float32)
    t = np.arange(DEC_SEQ)
    pos = np.where(use_col[None, :], (t % GRID_W)[:, None], (t // GRID_W)[:, None]).astype(np.float32)
    ang = (pos * inv[i][None, :]).astype(np.float32)
    cos, sin = np.cos(ang), np.sin(ang)
    c = np.concatenate([np.ones((TM, 128), np.float32), cos], axis=0)
    s1 = np.concatenate([np.zeros((TM, 128), np.float32), np.where(first[None, :], -sin, 0.0)], axis=0)
    s2 = np.concatenate([np.zeros((TM, 128), np.float32), np.where(first[None, :], 0.0, sin)], axis=0)
    return np.stack([c, s1, s2]).astype(np.float32), quarter


def _rope128(x, tab_ref, shift):
    return (x * tab_ref[0] + pltpu.roll(x, 128 - shift, 1) * tab_ref[1]
            + pltpu.roll(x, shift, 1) * tab_ref[2])


def _rope(x, tab_ref, shift):
    w = x.shape[1]
    return jnp.concatenate([_rope128(x[:, c:c + 128], tab_ref, shift) for c in range(0, w, 128)], axis=1)


def _mod_kernel(c_ref, w_ref, b_ref, o_ref):
    c = c_ref[...]
    o_ref[0] = _dot(_silu(c).astype(BF16), w_ref[0].astype(BF16)) + b_ref[0]


def _modulation(cvecs, w_ada, b_ada):
    nb = 3 * D_MODEL // 768
    return pl.pallas_call(
        _mod_kernel,
        grid=(DEPTH, nb),
        in_specs=[pl.BlockSpec((16, D_MODEL), lambda l, j: (0, 0)),
                  pl.BlockSpec((1, D_MODEL, 768), lambda l, j: (l, 0, j)),
                  pl.BlockSpec((1, 1, 768), lambda l, j: (l, 0, j))],
        out_specs=pl.BlockSpec((1, 16, 768), lambda l, j: (l, 0, j)),
        out_shape=jax.ShapeDtypeStruct((DEPTH, 16, 3 * D_MODEL), F32),
        compiler_params=_cparams(("arbitrary", "arbitrary")),
        name="modulation",
    )(cvecs, w_ada, b_ada.reshape(DEPTH, 1, 3 * D_MODEL))


W_STEPS = 8
W_ROWS = D_MODEL // W_STEPS


def _tile(i):
    return jnp.maximum(i - W_STEPS, 0)


def _wchunk(i):
    return jnp.minimum(i, W_STEPS - 1)


def _mod_row(t):
    return jnp.where(t < N_CTX_TILES, 0, 1 + (t - N_CTX_TILES) // LAT_TILES_PER_SEQ)


def _rope_blk(t):
    return jnp.where(t < N_CTX_TILES, 0, 1 + (t - N_CTX_TILES) % LAT_TILES_PER_SEQ)


def _load_x(t, xa_ref, xb_ref):
    return jnp.where(t < N_CTX_TILES, xa_ref[...], xb_ref[...])


def _cast_rows(i, src_ref, dst_ref):
    r0 = pl.multiple_of(i * W_ROWS, W_ROWS)
    dst_ref[pl.ds(r0, W_ROWS), :] = src_ref[0].astype(BF16)


WI_G, WI_AQ, WI_AK, WI_AV, WI_BX, WI_CQ, WI_CKV, WI_DQ, WI_DKV = 0, 1024, 1280, 1536, 1792, 2048, 2304, 2816, 3072
SEQ_PER_TILE = TM // SEQ


def _store_heads(ref, val, n_heads, width):
    vt = val.T
    for s in range(SEQ_PER_TILE):
        for h in range(n_heads):
            ref[s, 0, h] = vt[h * width:(h + 1) * width, s * SEQ:(s + 1) * SEQ]


def _inproj_kernel(n_alias, xa_ref, xb_ref, mod_ref, g_ref, w_ref, ta_ref, td_ref, *refs):
    h_ref, proj_ref, bx_ref, dk_ref, dv_ref, nk_ref, nv_ref, sk_ref, sv_ref, w_s = refs[n_alias:]
    i = pl.program_id(0)

    @pl.when(i < W_STEPS)
    def _():
        r0 = pl.multiple_of(i * W_ROWS, W_ROWS)
        wb = w_ref[0].astype(BF16)
        w_s[pl.ds(r0, W_ROWS), 0:WI_DKV] = wb[:, 0:WI_DKV]
        for n, src in enumerate((0, 0, 1, 1, 2, 2, 3, 3)):
            w_s[pl.ds(r0, W_ROWS), WI_DKV + n * HD:WI_DKV + (n + 1) * HD] = \
                wb[:, WI_DKV + src * HD:WI_DKV + (src + 1) * HD]

    @pl.when(i >= W_STEPS)
    def _():
        t = i - W_STEPS
        is_ctx = t < N_CTX_TILES
        x = _load_x(t, xa_ref, xb_ref)
        mod = mod_ref[0]
        shift, scale = mod[:, :D_MODEL], mod[:, D_MODEL:2 * D_MODEL]
        xn = x * lax.rsqrt(jnp.mean(x * x, axis=-1, keepdims=True) + EPS) * g_ref[0]
        hb =(xn * (1.0 + scale) + shift).astype(BF16)
        h_ref[...] = hb

        def mm(c0, width):
            return _dot(hb, w_s[:, c0:c0 + width])

        w2 = 2 * BR_W
        gs = [mm(WI_G + c, w2) for c in range(0, N_BRANCH * BR_W, w2)]
        aq, ak, av = mm(WI_AQ, BR_W), mm(WI_AK, BR_W), mm(WI_AV, BR_W)
        cq, ckv = mm(WI_CQ, BR_W), mm(WI_CKV, w2)
        dq, dk4, dv4 = mm(WI_DQ, BR_W), mm(WI_DKV, BR_W), mm(WI_DKV + BR_W, BR_W)
        bx_ref[...] = mm(WI_BX, BR_W)

        for n, g in enumerate(gs):
            proj_ref[:, COL_G + n * w2:COL_G + (n + 1) * w2] = _silu(g).astype(BF16)
        proj_ref[:, COL_AQ:COL_AQ + BR_W] = _rope(aq * QA_SCALE, ta_ref, A_QK // 4).astype(BF16)
        proj_ref[:, COL_AK:COL_AK + BR_W] = _rope(ak, ta_ref, A_QK // 4).astype(BF16)
        proj_ref[:, COL_AV:COL_AV + BR_W] = av.astype(BF16)
        proj_ref[:, COL_CQ:COL_CQ + BR_W] = (cq * QH_SCALE).astype(BF16)
        proj_ref[:, COL_CK:COL_CK + w2] = ckv.astype(BF16)
        proj_ref[:, COL_DQ:COL_DQ + BR_W] = _rope(dq * QH_SCALE, td_ref, HD // 4).astype(BF16)
        proj_ref[:, COL_DK:COL_DK + BR_W] = _rope(dk4, td_ref, HD // 4).astype(BF16)
        proj_ref[:, COL_DV:COL_DV + BR_W] = dv4.astype(BF16)

        @pl.when(is_ctx)
        def _():
            akt = ak.T
            for s in range(SEQ_PER_TILE):
                for h in range(A_HEADS):
                    for m in range(2):
                        c0 = h * HD + m * A_QK
                        dk_ref[s, 0, h, m] = akt[c0:c0 + A_QK, s * SEQ:(s + 1) * SEQ]
            _store_heads(dv_ref, av, A_HEADS, HD)
            _store_heads(nk_ref, ckv[:, :BR_W], C_HEADS, HD)
            _store_heads(nv_ref, ckv[:, BR_W:], C_HEADS, HD)
            for ref, val in ((sk_ref, dk4), (sv_ref, dv4)):
                vt = val.T
                for s in range(SEQ_PER_TILE):
                    for kv in range(D_KV):
                        ref[s, 0, kv] = vt[kv * D_GROUP * HD:kv * D_GROUP * HD + HD, s * SEQ:(s + 1) * SEQ]


def _cache_shapes():
    hs = lambda n: (BATCH, DEPTH, n, HD, SEQ)
    return [(BATCH, DEPTH, A_HEADS, 2, A_QK, SEQ), hs(A_HEADS), hs(C_HEADS), hs(C_HEADS), hs(D_KV), hs(D_KV)]


def _inproj(xa, xb, xb_off, mod3, l, norm_g, w_in, tab_a, tab_d, caches):
    row = lambda i: (_tile(i), 0)
    ctx_blk = lambda i: jnp.minimum(_tile(i), N_CTX_TILES - 1)

    def cache_spec(shape):
        blk = (SEQ_PER_TILE, 1) + shape[2:]
        nz = len(shape) - 2
        return pl.BlockSpec(blk, lambda i: (ctx_blk(i), l) + (0,) * nz)

    cshapes = _cache_shapes()
    aliases = {} if caches is None else {7 + k: 3 + k for k in range(6)}
    extra_specs = [] if caches is None else [pl.BlockSpec(memory_space=pl.ANY)] * 6
    extra_args = [] if caches is None else list(caches)
    outs = pl.pallas_call(
        functools.partial(_inproj_kernel, len(extra_args)),
        grid=(W_STEPS + N_TILES,),
        in_specs=[
            pl.BlockSpec((TM, D_MODEL), lambda i: (ctx_blk(i), 0)),
            pl.BlockSpec((TM, D_MODEL), lambda i: (jnp.maximum(_tile(i) - N_CTX_TILES, 0) + xb_off, 0)),
            pl.BlockSpec((1, 1, 3 * D_MODEL), lambda i: (l * 16 + _mod_row(_tile(i)), 0, 0)),
            pl.BlockSpec((1, 1, D_MODEL), lambda i: (l, 0, 0)),
            pl.BlockSpec((1, W_ROWS, IN_W), lambda i: (l, _wchunk(i), 0)),
            pl.BlockSpec((3, TM, 128), lambda i: (0, _rope_blk(_tile(i)), 0)),
            pl.BlockSpec((3, TM, 128), lambda i: (0, _rope_blk(_tile(i)), 0)),
        ] + extra_specs,
        out_specs=[
            pl.BlockSpec((TM, D_MODEL), row),
            pl.BlockSpec((TM, PROJ_W), row),
            pl.BlockSpec((TM, BR_W), row),
        ] + [cache_spec(s) for s in cshapes],
        out_shape=[
            jax.ShapeDtypeStruct((N_TOK, D_MODEL), BF16),
            jax.ShapeDtypeStruct((N_TOK, PROJ_W), BF16),
            jax.ShapeDtypeStruct((N_TOK, BR_W), F32),
        ] + [jax.ShapeDtypeStruct(s, F32) for s in cshapes],
        scratch_shapes=[pltpu.VMEM((D_MODEL, WS_W), BF16)],
        input_output_aliases=aliases,
        compiler_params=_cparams(("arbitrary",)),
        name=f"inproj_l{l}",
    )(xa, xb, mod3, norm_g, w_in, tab_a, tab_d, *extra_args)
    return outs[0], outs[1], outs[2], tuple(outs[3:])


def _diff_lambda(lam_ref, lam_init):
    lv = lam_ref[0]
    s1 = jnp.sum(lv[0:1] * lv[1:2], axis=-1, keepdims=True)
    s2 = jnp.sum(lv[2:3] * lv[3:4], axis=-1, keepdims=True)
    return jnp.exp(s1) - jnp.exp(s2) + lam_init


def _diff_norm(o, dg_ref, lam_init):
    y = o * lax.rsqrt(jnp.mean(o * o, axis=-1, keepdims=True) + EPS) * dg_ref[0]
    return y * (1.0 - lam_init)


def _exp2_parts(parts, extra=None):
    m = functools.reduce(jnp.maximum, [jnp.max(s, axis=-1, keepdims=True) for s in parts])
    if extra is not None:
        m = jnp.maximum(m, extra)
    return [jnp.exp2(s - m).astype(BF16) for s in parts], m


def _pipelined(n, score_fn, pv_fn):
    outs = []
    nxt = score_fn(0)
    for c in range(n):
        cur, nxt = nxt, (score_fn(c + 1) if c + 1 < n else None)
        outs.append(pv_fn(c, cur))
    return outs


def _with_ones(v):
    return jnp.concatenate([v, jnp.ones_like(v)], axis=1)


def _with_ones_t(vt):
    return jnp.concatenate([vt, jnp.ones((16, vt.shape[1]), vt.dtype)], axis=0)


def _sink_col(sink2, kv, rows):
    r = lax.broadcasted_iota(jnp.int32, (D_GROUP * rows, 1), 0)
    return jnp.where(r < rows, sink2[:, kv * D_GROUP:kv * D_GROUP + 1], sink2[:, kv * D_GROUP + 1:kv * D_GROUP + 2])


CTX_PER_STEP = 2


def _lane_masks(n):
    lane = lax.broadcasted_iota(jnp.int32, (1, BR_W), 1)
    w = BR_W // n
    ms = [jnp.where((lane >= p * w) & (lane < (p + 1) * w), 1.0, 0.0).astype(BF16) for p in range(n)]
    return ms, [1.0 - m for m in ms]


def _swap_halves(p):
    return jnp.concatenate([p[:, BR_W // 2:], p[:, :BR_W // 2]], axis=1)


def _pick_heads(os):
    grp = lax.broadcasted_iota(jnp.int32, os[0].shape, 1) // HD
    acc = os[0]
    for h in range(1, len(os)):
        acc = jnp.where(grp == h, os[h], acc)
    return acc


def _head_rms(o, dg4_ref, lam_init):
    r = lax.broadcasted_iota(jnp.int32, (BR_W, BR_W), 0) // HD
    c = lax.broadcasted_iota(jnp.int32, (BR_W, BR_W), 1) // HD
    ss = _dot((o * o).astype(BF16), jnp.where(r == c, 1.0, 0.0).astype(BF16))
    return o * lax.rsqrt(ss * (1.0 / HD) + EPS) * dg4_ref[0] * (1.0 - lam_init)


def _ctx_attn_kernel(n_alias, lam_init, aq_ref, ak_ref, av_ref, cq_ref, ck_ref, cv_ref, dq_ref, dk_ref, dv_ref,
                     ga_ref, gc_ref, gd_ref, lam_ref, dg4_ref, sink_ref,
                     bx_ref, gr_ref, cw_ref, cb_ref, wg_ref, bg_ref, llam_ref, *refs):
    z_ref, fin_ref = refs[n_alias:]
    seqs = [slice(s * SEQ, (s + 1) * SEQ) for s in range(CTX_PER_STEP)]

    for s, rs in enumerate(seqs):
        xc = _lru_conv(bx_ref[rs, :], cw_ref[0], cb_ref[0])
        pre = _dot(xc.astype(BF16), wg_ref[0]) + bg_ref[0]
        h0 = jnp.zeros((1, BR_W), F32)
        yf, hf = _lru_scan(0, xc, pre[:, :2 * BR_W], llam_ref[0, 0:1, :], h0)
        yb, hb = _lru_scan(1, xc, pre[:, 2 * BR_W:], llam_ref[0, 1:2, :], h0)
        z_ref[rs, BR_W:2 * BR_W] = ((yf + yb) * gr_ref[rs, :].astype(F32)).astype(BF16)
        fin_ref[s, 0, 0:1, :] = hf
        fin_ref[s, 0, 1:2, :] = hb

    lam = _diff_lambda(lam_ref, lam_init)
    sink2 = sink_ref[0] * LOG2E
    m_qk, _ = _lane_masks(2 * A_HEADS)
    m_hd, o_hd = _lane_masks(C_HEADS)

    sa, sc, sd = [], [], []
    for rs in seqs:
        aq, cq, dq = aq_ref[rs, :], cq_ref[rs, :], dq_ref[rs, :]
        ak, ck, dk = ak_ref[rs, :], ck_ref[rs, :], dk_ref[rs, :]
        sa.append([_dot_nt(aq * m, ak) for m in m_qk])
        sc.append([_dot_nt(cq * m, ck) for m in m_hd])
        sd.append([_dot_nt(dq * m, dk) for m in m_hd])

    ea =[[_exp2_parts([s])[0][0] for s in ss] for ss in sa]
    ec = [[_exp2_parts([s])[0][0] for s in ss] for ss in sc]
    ed = [[_exp2_parts([s], extra=sink2[:, j:j + 1]) for j, s in enumerate(ss)] for ss in sd]

    for i, rs in enumerate(seqs):
        av, cv, dv = av_ref[rs, :], cv_ref[rs, :], dv_ref[rs, :]
        os = []
        for h in range(A_HEADS):
            w = av * m_hd[h] + o_hd[h]
            p0, p1 = _dot(ea[i][2 * h], w), _dot(ea[i][2 * h + 1], w)
            os.append(p0 * (1.0 / _swap_halves(p0)) - p1 * (lam / _swap_halves(p1)))
        z_ref[rs, 0:BR_W] = (_head_rms(_pick_heads(os), dg4_ref, lam_init) * ga_ref[rs, :].astype(F32)).astype(BF16)

        os = []
        for h in range(C_HEADS):
            p = _dot(ec[i][h], cv * m_hd[h] + o_hd[h])
            os.append(p * (1.0 / _swap_halves(p)))
        z_ref[rs, 2 * BR_W:3 * BR_W] = (_pick_heads(os) * gc_ref[rs, :].astype(F32)).astype(BF16)

        os = []
        for j in range(D_HEADS):
            (e,), m = ed[i][j]
            p = _dot(e, dv * m_hd[j] + o_hd[j])
            os.append(p * (1.0 / (_swap_halves(p) + jnp.exp2(sink2[:, j:j + 1] - m))))
        z_ref[rs, 3 * BR_W:4 * BR_W] = (_pick_heads(os) * gd_ref[rs, :].astype(F32)).astype(BF16)


def _ctx_attn(proj, bx, l, diff_lambda, dg4, sink3, lru_params, states):
    lam_init = 0.8 - 0.6 * math.exp(-0.3 * l)
    colblk = lambda cb: pl.BlockSpec((CTX_PER_STEP * SEQ, BR_W), lambda b: (b, cb))
    extra = [] if states is None else [states]
    return pl.pallas_call(
        functools.partial(_ctx_attn_kernel, len(extra), lam_init),
        grid=(BATCH // CTX_PER_STEP,),
        in_specs=[colblk(COL_AQ // BR_W), colblk(COL_AK // BR_W), colblk(COL_AV // BR_W),
                  colblk(COL_CQ // BR_W), colblk(COL_CK // BR_W), colblk(COL_CV // BR_W),
                  colblk(COL_DQ // BR_W), colblk(COL_DK // BR_W), colblk(COL_DV // BR_W),
                  colblk(0), colblk(2), colblk(3),
                  pl.BlockSpec((1, 4, A_QK), lambda b: (l, 0, 0)),
                  pl.BlockSpec((1, 1, BR_W), lambda b: (l, 0, 0)),
                  pl.BlockSpec((1, 1, D_HEADS), lambda b: (l, 0, 0)),
                  pl.BlockSpec((CTX_PER_STEP * SEQ, BR_W), lambda b: (b, 0)), colblk(1)]
                 + _lru_specs(l, lambda b: (l, 0, 0)) + [pl.BlockSpec(memory_space=pl.ANY)] * len(extra),
        out_specs=[pl.BlockSpec((CTX_PER_STEP * SEQ, N_BRANCH * BR_W), lambda b: (b, 0)),
                   pl.BlockSpec((CTX_PER_STEP, 1, 2, BR_W), lambda b: (b, l, 0, 0))],
        out_shape=[jax.ShapeDtypeStruct((N_CTX, N_BRANCH * BR_W), BF16),
                   jax.ShapeDtypeStruct((BATCH, DEPTH, 2, BR_W), F32)],
        input_output_aliases={} if states is None else {22: 1},
        compiler_params=_cparams(("arbitrary",)),
        name=f"ctx_attn_l{l}",
    )(*([proj] * 12), diff_lambda, dg4, sink3, bx, proj, *lru_params, *extra)


LA_QB = 256


def _lat_diff_kernel(lam_init, aq_ref, ak_ref, av_ref, ck_ref, cv_ref, ga_ref, lam_ref, dg_ref, za_ref):
    lam = _diff_lambda(lam_ref, lam_init)
    aq = aq_ref[...]
    va_ctx = [_with_ones_t(cv_ref[0, 0, h].astype(BF16)) for h in range(A_HEADS)]
    va_loc = [_with_ones(av_ref[:, h * HD:(h + 1) * HD]) for h in range(A_HEADS)]

    def scores(c):
        h, m = divmod(c, 2)
        lo = h * HD + m * A_QK
        q = aq[:, lo:lo + A_QK]
        return [_dot(q, ck_ref[0, 0, h, m].astype(BF16)),
                _dot_nt(q, ak_ref[:, lo:lo + A_QK])]

    def pv(c, s):
        (e_ctx, e_loc), _ = _exp2_parts(s)
        p_ctx, p_loc = _dot_nt(e_ctx, va_ctx[c // 2]), _dot(e_loc, va_loc[c // 2])
        return p_ctx[:, :HD] + p_loc[:, :HD], p_ctx[:, HD:HD + 1] + p_loc[:, HD:HD + 1]

    ss = [scores(c) for c in range(2 * A_HEADS)]
    nd = [pv(c, s) for c, s in enumerate(ss)]
    outs = []
    for h in range(A_HEADS):
        (n0, d0), (n1, d1) = nd[2 * h], nd[2 * h + 1]
        outs.append(_diff_norm(n0 * (1.0 / d0) - n1 * (lam / d1), dg_ref, lam_init))
    za_ref[...] = (jnp.concatenate(outs, axis=1) * ga_ref[...].astype(F32)).astype(BF16)


def _lat_diff(proj, cache_k, cache_v, l, diff_lambda, dg3):
    lam_init = 0.8 - 0.6 * math.exp(-0.3 * l)
    nq = DEC_SEQ // LA_QB
    qrow = lambda b, j: N_CTX // LA_QB + b * nq + j
    srow = lambda b, j: N_CTX // DEC_SEQ + b
    return pl.pallas_call(
        functools.partial(_lat_diff_kernel, lam_init),
        grid=(DEC_BATCH, nq),
        in_specs=[pl.BlockSpec((LA_QB, BR_W), lambda b, j: (qrow(b, j), COL_AQ // BR_W)),
                  pl.BlockSpec((DEC_SEQ, BR_W), lambda b, j: (srow(b, j), COL_AK // BR_W)),
                  pl.BlockSpec((DEC_SEQ, BR_W), lambda b, j: (srow(b, j), COL_AV // BR_W)),
                  pl.BlockSpec((1, 1, A_HEADS, 2, A_QK, PAST_LEN), lambda b, j: (b, l, 0, 0, 0, 0)),
                  pl.BlockSpec((1, 1, A_HEADS, HD, PAST_LEN), lambda b, j: (b, l, 0, 0, 0)),
                  pl.BlockSpec((LA_QB, BR_W), lambda b, j: (qrow(b, j), 0)),
                  pl.BlockSpec((1, 4, A_QK), lambda b, j: (l, 0, 0)),
                  pl.BlockSpec((1, 1, HD), lambda b, j: (l, 0, 0))],
        out_specs=pl.BlockSpec((LA_QB, BR_W), lambda b, j: (b * nq + j, 0)),
        out_shape=jax.ShapeDtypeStruct((N_LAT, BR_W), BF16),
        compiler_params=_cparams(("arbitrary", "arbitrary")),
        name=f"lat_diff_l{l}",
    )(proj, proj, proj, cache_k, cache_v, proj, diff_lambda, dg3)


def _na_bias_kernel(rpb_ref, o_ref):
    l, h = pl.program_id(0), pl.program_id(1)
    base = (l * C_HEADS + h) * ((2 * NA_ROWS - 1) * (2 * NA_COLS - 1))
    qc = lax.broadcasted_iota(jnp.int32, (GRID_W, GRID_W), 0)
    kc = lax.broadcasted_iota(jnp.int32, (GRID_W, GRID_W), 1)
    cs = jnp.clip(qc - NA_COLS // 2, 0, GRID_W - NA_COLS)
    ok = (kc >= cs) & (kc < cs + NA_COLS)
    dcol = kc - qc + (NA_COLS - 1)
    neg = jnp.full((GRID_W, GRID_W), NEG, F32)
    tabs = []
    for d in range(2 * NA_ROWS - 1):
        t = neg
        for dc in range(2 * NA_COLS - 1):
            t = jnp.where(ok & (dcol == dc), rpb_ref[base + d * (2 * NA_COLS - 1) + dc] * LOG2E, t)
        tabs.append(t)
    for half in range(2):
        for rq in range(NA_ROWS):
            r = half * NA_ROWS + rq
            rs = min(max(r - NA_ROWS // 2, 0), GRID_ROWS - NA_ROWS)
            blks = []
            for kr in range(NA_KROWS):
                kabs = half * 4 + kr
                blks.append(tabs[kabs - r + NA_ROWS - 1] if rs <= kabs < rs + NA_ROWS else neg)
            o_ref[0, 0, half, rq * GRID_W:(rq + 1) * GRID_W, :] = jnp.concatenate(blks, axis=1)


def _na_bias(na_rpb):
    return pl.pallas_call(
        _na_bias_kernel,
        grid=(DEPTH, C_HEADS),
        in_specs=[pl.BlockSpec(memory_space=pltpu.SMEM)],
        out_specs=pl.BlockSpec((1, 1, 2, NA_HALF_Q, NA_KSPAN), lambda l, h: (l, h, 0, 0, 0)),
        out_shape=jax.ShapeDtypeStruct((DEPTH, C_HEADS, 2, NA_HALF_Q, NA_KSPAN), F32),
        compiler_params=_cparams(("arbitrary", "arbitrary")),
        name="na_bias",
    )(na_rpb.reshape(-1))


def _lat_na_kernel(half, cq_ref, ck_ref, cv_ref, kc_ref, vc_ref, bias_ref, gc_ref,
                   bx_ref, cw_ref, cb_ref, wg_ref, bg_ref, llam_ref, st_ref, zc_ref, y_ref):
    k0 = half * NA_HALF_KSTART
    cq = cq_ref[...]
    kl_all = ck_ref[k0:k0 + NA_KSPAN, :]
    vl_all = cv_ref[k0:k0 + NA_KSPAN, :]

    def scores(h):
        q = cq[:, h * HD:(h + 1) * HD]
        return [_dot(q, kc_ref[0, 0, h].astype(BF16)),
                _dot_nt(q, kl_all[:, h * HD:(h + 1) * HD]) + bias_ref[0, h, 0]]

    def pv(h, s):
        (e_ctx, e_loc), _ = _exp2_parts(s)
        p_ctx = _dot_nt(e_ctx, _with_ones_t(vc_ref[0, 0, h].astype(BF16)))
        p_loc = _dot(e_loc, _with_ones(vl_all[:, h * HD:(h + 1) * HD]))
        return (p_ctx[:, :HD] + p_loc[:, :HD]) * (1.0 / (p_ctx[:, HD:HD + 1] + p_loc[:, HD:HD + 1]))

    xc = _lru_conv(bx_ref[...], cw_ref[0], cb_ref[0])
    pre = _dot(xc.astype(BF16), wg_ref[0, :, half * 2 * BR_W:(half + 1) * 2 * BR_W]) \
        + bg_ref[0, :, half * 2 * BR_W:(half + 1) * 2 * BR_W]
    y_ref[...], _ = _lru_scan(half, xc, pre, llam_ref[0, half:half + 1, :], st_ref[0, 0, half:half + 1, :])

    ss = [scores(h) for h in range(C_HEADS)]
    outs = [pv(h, s) for h, s in enumerate(ss)]
    zc_ref[...] = (jnp.concatenate(outs, axis=1) * gc_ref[...].astype(F32)).astype(BF16)


def _lat_na(half, proj, bx, cache_k, cache_v, bias, l, lru_params, state_lru):
    qrow = lambda b: N_CTX // NA_HALF_Q + b * 2 + half
    srow = lambda b: N_CTX // DEC_SEQ + b
    return pl.pallas_call(
        functools.partial(_lat_na_kernel, half),
        grid=(DEC_BATCH,),
        in_specs=[pl.BlockSpec((NA_HALF_Q, BR_W), lambda b: (qrow(b), COL_CQ // BR_W)),
                  pl.BlockSpec((DEC_SEQ, BR_W), lambda b: (srow(b), COL_CK // BR_W)),
                  pl.BlockSpec((DEC_SEQ, BR_W), lambda b: (srow(b), COL_CV // BR_W)),
                  pl.BlockSpec((1, 1, C_HEADS, HD, PAST_LEN), lambda b: (b, l, 0, 0, 0)),
                  pl.BlockSpec((1, 1, C_HEADS, HD, PAST_LEN), lambda b: (b, l, 0, 0, 0)),
                  pl.BlockSpec((1, C_HEADS, 1, NA_HALF_Q, NA_KSPAN), lambda b: (l, 0, half, 0, 0)),
                  pl.BlockSpec((NA_HALF_Q, BR_W), lambda b: (qrow(b), 2)),
                  pl.BlockSpec((DEC_SEQ, BR_W), lambda b: (srow(b), 0))]
                 + _lru_specs(l, lambda b: (l, 0, 0))
                 + [pl.BlockSpec((1, 1, 2, BR_W), lambda b: (b, l, 0, 0))],
        out_specs=[pl.BlockSpec((NA_HALF_Q, BR_W), lambda b: (b, 0)),
                   pl.BlockSpec((DEC_SEQ, BR_W), lambda b: (b, 0))],
        out_shape=[jax.ShapeDtypeStruct((N_LAT // 2, BR_W), BF16), jax.ShapeDtypeStruct((N_LAT, BR_W), F32)],
        compiler_params=_cparams(("arbitrary",)),
        name=f"lat_na{half}_l{l}",
    )(proj, proj, proj, cache_k, cache_v, bias, proj, bx, *lru_params, state_lru)


def _lat_swa_kernel(dq_ref, dk_ref, dv_ref, kc_ref, vc_ref, sink_ref, gd_ref, zd_ref):
    nq = DEC_SEQ // SWA_QB
    rows = D_GROUP * SWA_QB
    dq, dk, dv = dq_ref[...], dk_ref[...], dv_ref[...]
    sink2 = sink_ref[0] * LOG2E
    m_hd, _ = _lane_masks(D_HEADS)
    starts = [min(max(j * SWA_QB - WIN, 0), DEC_SEQ - SWA_SPAN) for j in range(nq)]
    qi = lax.broadcasted_iota(jnp.int32, (rows, SWA_SPAN), 0) % SWA_QB
    ki = lax.broadcasted_iota(jnp.int32, (rows, SWA_SPAN), 1)
    masks = {off: jnp.abs(qi + off - ki) <= WIN for off in sorted({j * SWA_QB - s for j, s in enumerate(starts)})}
    kc_t = jnp.concatenate([kc_ref[0, 0, kv] for kv in range(D_KV) for _ in range(D_GROUP)], axis=0).astype(BF16)
    vc_t = jnp.concatenate([vc_ref[0, 0, kv] for kv in range(D_KV) for _ in range(D_GROUP)], axis=0).astype(BF16)
    feat = lax.broadcasted_iota(jnp.int32, (BR_W, 1), 0) // (D_GROUP * HD)

    qm = [dq * m for m in m_hd]
    sks = [_sink_col(sink2, kv, SWA_QB) for kv in range(D_KV)]
    w_loc, w_ctx = [], []
    for kv in range(D_KV):
        pair = m_hd[kv * D_GROUP] + m_hd[kv * D_GROUP + 1]
        w_loc.append(dv * pair + (1.0 - pair))
        w_ctx.append(jnp.where(feat == kv, vc_t, jnp.ones_like(vc_t)))

    def scores(c):
        kv, j = divmod(c, nq)
        q2 = jnp.concatenate([qm[kv * D_GROUP + g][j * SWA_QB:(j + 1) * SWA_QB] for g in range(D_GROUP)], axis=0)
        return [_dot(q2, kc_t),
                jnp.where(masks[j * SWA_QB - starts[j]], _dot_nt(q2, dk[starts[j]:starts[j] + SWA_SPAN]), NEG)]

    def pv(c, s):
        kv, j = divmod(c, nq)
        (e_ctx, e_loc), m = _exp2_parts(s, extra=sks[kv])
        p = _dot_nt(e_ctx, w_ctx[kv]) + _dot(e_loc, w_loc[kv][starts[j]:starts[j] + SWA_SPAN])
        return p * (1.0 / (_swap_halves(p) + jnp.exp2(sks[kv] - m)))

    o = _pipelined(D_KV * nq, scores, pv)
    os = [jnp.concatenate([o[kv * nq + j][g * SWA_QB:(g + 1) * SWA_QB] for j in range(nq)], axis=0)
          for kv in range(D_KV) for g in range(D_GROUP)]
    zd_ref[...] = (_pick_heads(os) * gd_ref[...].astype(F32)).astype(BF16)


def _lat_swa(proj, cache_k, cache_v, sink3, l):
    srow = lambda b: (N_CTX // DEC_SEQ + b)
    return pl.pallas_call(
        _lat_swa_kernel,
        grid=(DEC_BATCH,),
        in_specs=[pl.BlockSpec((DEC_SEQ, BR_W), lambda b: (srow(b), COL_DQ // BR_W)),
                  pl.BlockSpec((DEC_SEQ, BR_W), lambda b: (srow(b), COL_DK // BR_W)),
                  pl.BlockSpec((DEC_SEQ, BR_W), lambda b: (srow(b), COL_DV // BR_W)),
                  pl.BlockSpec((1, 1, D_KV, HD, PAST_LEN), lambda b: (b, l, 0, 0, 0)),
                  pl.BlockSpec((1, 1, D_KV, HD, PAST_LEN), lambda b: (b, l, 0, 0, 0)),
                  pl.BlockSpec((1, 1, D_HEADS), lambda b: (l, 0, 0)),
                  pl.BlockSpec((DEC_SEQ, BR_W), lambda b: (srow(b), 3))],
        out_specs=pl.BlockSpec((DEC_SEQ, BR_W), lambda b: (b, 0)),
        out_shape=jax.ShapeDtypeStruct((N_LAT, BR_W), BF16),
        compiler_params=_cparams(("arbitrary",)),
        name=f"lat_swa_l{l}",
    )(proj, proj, proj, cache_k, cache_v, sink3, proj)


def _lru_conv(x, cw, cb):
    seq = x.shape[0]
    x3 = x.reshape(seq // 8, 8, BR_W)
    tin = lax.broadcasted_iota(jnp.int32, (1, 8, 1), 1)
    zero = jnp.zeros((1, 8, BR_W), F32)

    def at(shift):
        r = pltpu.roll(x3, (-shift) % 8, 1)
        if shift < 0:
            return jnp.where(tin < -shift, jnp.concatenate([zero, r[:-1]], axis=0), r)
        return jnp.where(tin >= 8 - shift, jnp.concatenate([r[1:], zero], axis=0), r)

    xc = cb + at(-1) * cw[0:1] + x3 * cw[1:2] + at(1) * cw[2:3] + at(2) * cw[3:4]
    return xc.reshape(seq, BR_W)


def _lru_scan(d, xc, pre, lam_row, h0):
    seq = xc.shape[0]
    nl = -lam_row
    softplus = jnp.maximum(nl, 0.0) + jnp.log1p(jnp.exp(-jnp.abs(nl)))
    r = jax.nn.sigmoid(pre[:, :BR_W])
    ig = jax.nn.sigmoid(pre[:, BR_W:])
    log_a = -LRU_C * r * softplus
    a = jnp.exp(log_a)
    u = jnp.exp(0.5 * jnp.log(-jnp.tanh(log_a) * (a * a + 1.0))) * (ig * xc)
    tin = lax.broadcasted_iota(jnp.int32, (1, 8, 1), 1)
    a = a.reshape(seq // 8, 8, BR_W)
    u = u.reshape(seq // 8, 8, BR_W)
    for s in (1, 2, 4):
        if d == 0:
            keep = tin >= s
            a_n = jnp.where(keep, pltpu.roll(a, s, 1), 1.0)
            u_n = jnp.where(keep, pltpu.roll(u, s, 1), 0.0)
        else:
            keep = tin < 8 - s
            a_n = jnp.where(keep, pltpu.roll(a, 8 - s, 1), 1.0)
            u_n = jnp.where(keep, pltpu.roll(u, 8 - s, 1), 0.0)
        u = u + a * u_n
        a = a * a_n
    nt = seq // 8
    ys = [None] * nt
    h = h0
    for k in (range(nt) if d == 0 else range(nt - 1, -1, -1)):
        y = u[k] + a[k] * h
        ys[k] = y
        h = y[7:8, :] if d == 0 else y[0:1, :]
    return jnp.concatenate(ys, axis=0), h


def _lru_specs(l, imap):
    return [pl.BlockSpec((1, CONV_W, BR_W), imap), pl.BlockSpec((1, 1, BR_W), imap),
            pl.BlockSpec((1, BR_W, 4 * BR_W), imap), pl.BlockSpec((1, 1, 4 * BR_W), imap),
            pl.BlockSpec((1, 2, BR_W), imap)]


def _merge_kernel(layer, final, xa_ref, xb_ref, mod_ref, h_ref, zctx_ref, zal_ref, yf_ref, yb_ref, gr_ref,
                  zc0_ref, zc1_ref, zdl_ref, wmg_hbm, bmg_ref, wbo_ref, wo_ref, nf_ref, *refs):
    out_refs, (wmg_s, wbo_s, wo_s, stage, sem) = refs[:-5], refs[-5:]
    i = pl.program_id(0)

    def wmg_copy(chunk, n):
        slot = chunk % 2
        return pltpu.make_async_copy(wmg_hbm.at[layer, pl.ds(chunk * W_ROWS, W_ROWS), n, :],
                                     stage.at[slot, n], sem.at[slot, n])

    @pl.when(i == 0)
    def _():
        for n in range(N_BRANCH):
            wmg_copy(0, n).start()

    @pl.when(i < W_STEPS - 1)
    def _():
        for n in range(N_BRANCH):
            wmg_copy(i + 1, n).start()

    @pl.when(i < W_STEPS)
    def _():
        r0 = pl.multiple_of(i * W_ROWS, W_ROWS)
        for n in range(N_BRANCH):
            wmg_copy(i, n).wait()
            wmg_s[pl.ds(r0, W_ROWS), n * D_MODEL:(n + 1) * D_MODEL] = stage[i % 2, n].astype(BF16)
        _cast_rows(i, wbo_ref, wbo_s)
        _cast_rows(i, wo_ref, wo_s)

    @pl.when(i >= W_STEPS)
    def _():
        t = i - W_STEPS
        x = _load_x(t, xa_ref, xb_ref)
        gate = mod_ref[0][:, 2 * D_MODEL:]
        h = h_ref[...]
        is_ctx = t < N_CTX_TILES
        second_half = (t - N_CTX_TILES) % LAT_TILES_PER_SEQ == 1
        z_lat = [zal_ref[...],
                 ((yf_ref[...] + yb_ref[...]) * gr_ref[...].astype(F32)).astype(BF16),
                 jnp.where(second_half, zc1_ref[...], zc0_ref[...]),
                 zdl_ref[...]]
        zs = [jnp.where(is_ctx, zctx_ref[:, n * BR_W:(n + 1) * BR_W], z_lat[n]) for n in range(N_BRANCH)]
        bmg = bmg_ref[0]
        cols = []
        for c in range(0, D_MODEL, 512):
            acc = None
            for n in range(N_BRANCH):
                g = jax.nn.sigmoid(_dot(h, wmg_s[:, n * D_MODEL + c:n * D_MODEL + c + 512])
                                   + bmg[n:n + 1, c:c + 512])
                term = g * _dot(zs[n], wbo_s[n * BR_W:(n + 1) * BR_W, c:c + 512])
                acc = term if acc is None else acc + term
            cols.append(acc.astype(BF16))
        merged = jnp.concatenate(cols, axis=1)
        xn = x + gate * _dot(merged, wo_s[...])
        if not final:
            out_refs[0][...] = xn
        else:
            y = xn * lax.rsqrt(jnp.mean(xn * xn, axis=-1, keepdims=True) + EPS) * nf_ref[...]

            @pl.when(t < N_CTX_TILES)
            def _():
                out_refs[0][...] = y

            @pl.when(t >= N_CTX_TILES)
            def _():
                out_refs[1][...] = y


def _merge(final, xa, xb, xb_off, mod3, l, h, proj, z_ctx, za_lat, y_f, y_b, zc_lat, zd_lat,
           w_mg, b_mg, w_bo, w_o, norm_f):
    assert TM == NA_HALF_Q
    row = lambda i: (_tile(i), 0)
    ctx_blk = lambda i: jnp.minimum(_tile(i), N_CTX_TILES - 1)
    lat_blk = lambda i: jnp.maximum(_tile(i) - N_CTX_TILES, 0)
    lat_spec = pl.BlockSpec((TM, BR_W), lambda i: (lat_blk(i), 0))
    half_spec = pl.BlockSpec((TM, BR_W), lambda i: (lat_blk(i) // LAT_TILES_PER_SEQ, 0))
    if final:
        out_specs = [pl.BlockSpec((TM, D_MODEL), lambda i: (ctx_blk(i), 0)),
                     pl.BlockSpec((TM, D_MODEL), lambda i: (lat_blk(i), 0))]
        out_shape = [jax.ShapeDtypeStruct((N_CTX, D_MODEL), F32), jax.ShapeDtypeStruct((N_LAT, D_MODEL), F32)]
    else:
        out_specs = pl.BlockSpec((TM, D_MODEL), row)
        out_shape = jax.ShapeDtypeStruct((N_TOK, D_MODEL), F32)
    wchunk = lambda width: pl.BlockSpec((1, W_ROWS, width), lambda i: (l, _wchunk(i), 0))
    return pl.pallas_call(
        functools.partial(_merge_kernel, l, final),
        grid=(W_STEPS + N_TILES,),
        in_specs=[
            pl.BlockSpec((TM, D_MODEL), lambda i: (ctx_blk(i), 0)),
            pl.BlockSpec((TM, D_MODEL), lambda i: (lat_blk(i) + xb_off, 0)),
            pl.BlockSpec((1, 1, 3 * D_MODEL), lambda i: (l * 16 + _mod_row(_tile(i)), 0, 0)),
            pl.BlockSpec((TM, D_MODEL), row),
            pl.BlockSpec((TM, N_BRANCH * BR_W), lambda i: (ctx_blk(i), 0)),
            lat_spec,
            lat_spec, lat_spec,
            pl.BlockSpec((TM, BR_W), lambda i: (_tile(i), 1)),
            half_spec, half_spec,
            lat_spec,
            pl.BlockSpec(memory_space=pl.ANY),
            pl.BlockSpec((1, N_BRANCH, D_MODEL), lambda i: (l, 0, 0)),
            wchunk(D_MODEL),
            wchunk(D_MODEL),
            pl.BlockSpec((1, D_MODEL), lambda i: (0, 0)),
        ],
        out_specs=out_specs,
        out_shape=out_shape,
        scratch_shapes=[pltpu.VMEM((D_MODEL, N_BRANCH * D_MODEL), BF16),
                        pltpu.VMEM((N_BRANCH * BR_W, D_MODEL), BF16),
                        pltpu.VMEM((D_MODEL, D_MODEL), BF16),
                        pltpu.VMEM((2, N_BRANCH, W_ROWS, D_MODEL), F32),
                        pltpu.SemaphoreType.DMA((2, N_BRANCH))],
        compiler_params=_cparams(("arbitrary",)),
        name=f"merge_l{l}",
    )(xa, xb, mod3, h, z_ctx, za_lat, y_f, y_b, proj, zc_lat[0], zc_lat[1], zd_lat,
      w_mg, b_mg,
      w_bo.reshape(DEPTH, N_BRANCH * BR_W, D_MODEL), w_o, norm_f.reshape(1, D_MODEL))


def _lru_gate_weights(lru_wa, lru_ba, lru_wx, lru_bx):
    w = jnp.stack([lru_wa[:, 0], lru_wx[:, 0], lru_wa[:, 1], lru_wx[:, 1]], axis=1)
    eye = jnp.eye(B_BLOCKS, dtype=w.dtype)
    dense = w[:, :, :, :, None, :] * eye[None, None, :, None, :, None]
    wg = dense.transpose(0, 2, 3, 1, 4, 5).reshape(DEPTH, BR_W, 4 * BR_W).astype(BF16)
    bg = jnp.stack([lru_ba[:, 0], lru_bx[:, 0], lru_ba[:, 1], lru_bx[:, 1]], axis=1).reshape(DEPTH, 1, 4 * BR_W)
    return wg, bg


def kernel(x_prompt, x_sample, cache_diff_k, cache_diff_v, cache_na_k, cache_na_v, cache_swa_k, cache_swa_v,
           state_lru, c, c_ctx, norm_g, w_ada, b_ada, w_in, diff_lambda, diff_norm_g, conv_w, conv_b,
           lru_wa, lru_ba, lru_wx, lru_bx, lru_lam, na_rpb, swa_sink, w_mg, b_mg, w_bo, w_o, norm_f):
    tab_a_np, _ = _rope_tables(A_QK)
    tab_d_np, _ = _rope_tables(HD)
    tab_a, tab_d = jnp.asarray(tab_a_np), jnp.asarray(tab_d_np)

    cvecs = jnp.concatenate([c_ctx[None, :], c, jnp.zeros((16 - 1 - DEC_BATCH, D_MODEL), F32)], axis=0)
    mod3 = _modulation(cvecs, w_ada, b_ada).reshape(DEPTH * 16, 1, 3 * D_MODEL)
    bias = _na_bias(na_rpb)
    wg, bg = _lru_gate_weights(lru_wa, lru_ba, lru_wx, lru_bx)
    norm_g3 = norm_g.reshape(DEPTH, 1, D_MODEL)
    dg3 = diff_norm_g.reshape(DEPTH, 1, HD)
    dg4 = jnp.tile(diff_norm_g, (1, A_HEADS)).reshape(DEPTH, 1, BR_W)
    sink3 = swa_sink.reshape(DEPTH, 1, D_HEADS)
    conv_b3 = conv_b.reshape(DEPTH, 1, BR_W)
    past = [jnp.swapaxes(t, -1, -2) for t in
            (cache_diff_k, cache_diff_v, cache_na_k, cache_na_v, cache_swa_k, cache_swa_v)]

    xa = x_prompt.reshape(N_CTX, D_MODEL)
    xb = x_sample.reshape(N_LAT, D_MODEL)
    xb_off = 0
    caches = states = None
    y_p = y_s = None
    for l in range(DEPTH):
        h, proj, bx, caches = _inproj(xa, xb, xb_off, mod3, l, norm_g3, w_in, tab_a, tab_d, caches)

        lru_params = (conv_w, conv_b3, wg, bg, lru_lam)
        z_ctx, states = _ctx_attn(proj, bx, l, diff_lambda, dg4, sink3, lru_params, states)
        za_lat = _lat_diff(proj, past[0], past[1], l, diff_lambda, dg3)
        zc0, y_f = _lat_na(0, proj, bx, past[2], past[3], bias, l, lru_params, state_lru)
        zc1, y_b = _lat_na(1, proj, bx, past[2], past[3], bias, l, lru_params, state_lru)
        zd_lat = _lat_swa(proj, past[4], past[5], sink3, l)

        final = l == DEPTH - 1
        out = _merge(final, xa, xb, xb_off, mod3, l, h, proj, z_ctx, za_lat, y_f, y_b, (zc0, zc1), zd_lat,
                     w_mg, b_mg, w_bo, w_o, norm_f)
        if final:
            y_p, y_s = out
        else:
            xa = xb = out
            xb_off = N_CTX_TILES

    new_caches = [jnp.swapaxes(t, -1, -2) for t in caches]
    return (y_p.reshape(BATCH, SEQ, D_MODEL), y_s.reshape(DEC_BATCH, DEC_SEQ, D_MODEL), *new_caches, states)
```

```python
import functools
import math

import numpy as np
import jax
import jax.numpy as jnp
from jax import lax
from jax.experimental import pallas as pl
from jax.experimental.pallas import tpu as pltpu

F32 = jnp.float32
BF16 = jnp.bfloat16

D_MODEL = 1024
BATCH = 16
SEQ = 256
DEPTH = 2
DEC_BATCH = 8
DEC_SEQ = 1024
PAST_LEN = 512
GRID_W = 64
N_BRANCH = 4
BR_W = D_MODEL // 4
HD = 64
A_HEADS = BR_W // HD
A_QK = HD // 2
B_BLOCKS = 4
B_BLK = BR_W // B_BLOCKS
CONV_W = 4
LRU_C = 8.0
C_HEADS = BR_W // HD
NA_ROWS = 8
NA_COLS = 16
D_HEADS = BR_W // HD
D_KV = 2
D_GROUP = D_HEADS // D_KV
WIN = 128
ROPE_BASE = 10000.0
EPS = 1e-6
NEG = -1e30

N_CTX = BATCH * SEQ
N_LAT = DEC_BATCH * DEC_SEQ
N_TOK = N_CTX + N_LAT
GRID_ROWS = DEC_SEQ // GRID_W

TM = 512
N_CTX_TILES = N_CTX // TM
N_TILES = N_TOK // TM
LAT_TILES_PER_SEQ = DEC_SEQ // TM

PROJ_W = 3328
COL_G = 0
COL_AQ = 1024
COL_AK = 1280
COL_AV = 1536
COL_CQ = 1792
COL_CK = 2048
COL_CV = 2304
COL_DQ = 2560
COL_DK = 2816
COL_DV = 3072
IN_W = 3328
WS_W = IN_W + 2 * D_KV * HD

NA_HALF_Q = DEC_SEQ // 2
NA_KROWS = 12
NA_KSPAN = NA_KROWS * GRID_W
NA_HALF_KSTART = 4 * GRID_W

SWA_QB = 128
SWA_SPAN = SWA_QB + 2 * WIN

LOG2E = math.log2(math.e)
QA_SCALE = A_QK ** -0.5 * LOG2E
QH_SCALE = HD ** -0.5 * LOG2E

VMEM_LIMIT = 56 * 1024 * 1024


def _cparams(sem):
    return pltpu.CompilerParams(dimension_semantics=sem, vmem_limit_bytes=VMEM_LIMIT)


def _dot(a, b):
    return jnp.dot(a, b, preferred_element_type=F32)


def _dot_nt(a, b):
    return lax.dot_general(a, b, (((1,), (1,)), ((), ())), preferred_element_type=F32)


def _silu(x):
    return x * jax.nn.sigmoid(x)


def _rope_tables(d):
    half = d // 2
    quarter = half // 2
    lane = np.arange(128)
    q = lane % d
    use_col = (q >= half)
    i = (q % half) % quarter
    first = (q % half) < quarter
    inv = (ROPE_BASE ** (-np.arange(quarter, dtype=np.float32) / np.float32(quarter))).astype(np.float32)
    t = np.arange(DEC_SEQ)
    pos = np.where(use_col[None, :], (t % GRID_W)[:, None], (t // GRID_W)[:, None]).astype(np.float32)
    ang = (pos * inv[i][None, :]).astype(np.float32)
    cos, sin = np.cos(ang), np.sin(ang)
    c = np.concatenate([np.ones((TM, 128), np.float32), cos], axis=0)
    s1 = np.concatenate([np.zeros((TM, 128), np.float32), np.where(first[None, :], -sin, 0.0)], axis=0)
    s2 = np.concatenate([np.zeros((TM, 128), np.float32), np.where(first[None, :], 0.0, sin)], axis=0)
    return np.stack([c, s1, s2]).astype(np.float32), quarter


def _rope128(x, tab_ref, shift):
    return (x * tab_ref[0] + pltpu.roll(x, 128 - shift, 1) * tab_ref[1]
            + pltpu.roll(x, shift, 1) * tab_ref[2])


def _rope(x, tab_ref, shift):
    w = x.shape[1]
    return jnp.concatenate([_rope128(x[:, c:c + 128], tab_ref, shift) for c in range(0, w, 128)], axis=1)


def _mod_kernel(c_ref, w_ref, b_ref, o_ref):
    c = c_ref[...]
    o_ref[0] = _dot(_silu(c).astype(BF16), w_ref[0].astype(BF16)) + b_ref[0]


def _modulation(cvecs, w_ada, b_ada):
    nb = 3 * D_MODEL // 768
    return pl.pallas_call(
        _mod_kernel,
        grid=(DEPTH, nb),
        in_specs=[pl.BlockSpec((16, D_MODEL), lambda l, j: (0, 0)),
                  pl.BlockSpec((1, D_MODEL, 768), lambda l, j: (l, 0, j)),
                  pl.BlockSpec((1, 1, 768), lambda l, j: (l, 0, j))],
        out_specs=pl.BlockSpec((1, 16, 768), lambda l, j: (l, 0, j)),
        out_shape=jax.ShapeDtypeStruct((DEPTH, 16, 3 * D_MODEL), F32),
        compiler_params=_cparams(("arbitrary", "arbitrary")),
        name="modulation",
    )(cvecs, w_ada, b_ada.reshape(DEPTH, 1, 3 * D_MODEL))


W_STEPS = 8
W_ROWS = D_MODEL // W_STEPS


def _tile(i):
    return jnp.maximum(i - W_STEPS, 0)


def _wchunk(i):
    return jnp.minimum(i, W_STEPS - 1)


def _mod_row(t):
    return jnp.where(t < N_CTX_TILES, 0, 1 + (t - N_CTX_TILES) // LAT_TILES_PER_SEQ)


def _rope_blk(t):
    return jnp.where(t < N_CTX_TILES, 0, 1 + (t - N_CTX_TILES) % LAT_TILES_PER_SEQ)


def _load_x(t, xa_ref, xb_ref):
    return jnp.where(t < N_CTX_TILES, xa_ref[...], xb_ref[...])


def _cast_rows(i, src_ref, dst_ref):
    r0 = pl.multiple_of(i * W_ROWS, W_ROWS)
    dst_ref[pl.ds(r0, W_ROWS), :] = src_ref[0].astype(BF16)


WI_G, WI_AQ, WI_AK, WI_AV, WI_BX, WI_CQ, WI_CKV, WI_DQ, WI_DKV = 0, 1024, 1280, 1536, 1792, 2048, 2304, 2816, 3072
SEQ_PER_TILE = TM // SEQ


def _store_heads(ref, val, n_heads, width):
    vt = val.T
    for s in range(SEQ_PER_TILE):
        for h in range(n_heads):
            ref[s, 0, h] = vt[h * width:(h + 1) * width, s * SEQ:(s + 1) * SEQ]


def _inproj_kernel(n_alias, xa_ref, xb_ref, mod_ref, g_ref, w_ref, ta_ref, td_ref, *refs):
    h_ref, proj_ref, bx_ref, dk_ref, dv_ref, nk_ref, nv_ref, sk_ref, sv_ref, w_s = refs[n_alias:]
    i = pl.program_id(0)

    @pl.when(i < W_STEPS)
    def _():
        r0 = pl.multiple_of(i * W_ROWS, W_ROWS)
        wb = w_ref[0].astype(BF16)
        w_s[pl.ds(r0, W_ROWS), 0:WI_DKV] = wb[:, 0:WI_DKV]
        for n, src in enumerate((0, 0, 1, 1, 2, 2, 3, 3)):
            w_s[pl.ds(r0, W_ROWS), WI_DKV + n * HD:WI_DKV + (n + 1) * HD] = \
                wb[:, WI_DKV + src * HD:WI_DKV + (src + 1) * HD]

    @pl.when(i >= W_STEPS)
    def _():
        t = i - W_STEPS
        is_ctx = t < N_CTX_TILES
        x = _load_x(t, xa_ref, xb_ref)
        mod = mod_ref[0]
        shift, scale = mod[:, :D_MODEL], mod[:, D_MODEL:2 * D_MODEL]
        xn = x * lax.rsqrt(jnp.mean(x * x, axis=-1, keepdims=True) + EPS) * g_ref[0]
        hb =(xn * (1.0 + scale) + shift).astype(BF16)
        h_ref[...] = hb

        def mm(c0, width):
            return _dot(hb, w_s[:, c0:c0 + width])

        w2 = 2 * BR_W
        gs = [mm(WI_G + c, w2) for c in range(0, N_BRANCH * BR_W, w2)]
        aq, ak, av = mm(WI_AQ, BR_W), mm(WI_AK, BR_W), mm(WI_AV, BR_W)
        cq, ckv = mm(WI_CQ, BR_W), mm(WI_CKV, w2)
        dq, dk4, dv4 = mm(WI_DQ, BR_W), mm(WI_DKV, BR_W), mm(WI_DKV + BR_W, BR_W)
        bx_ref[...] = mm(WI_BX, BR_W)

        for n, g in enumerate(gs):
            proj_ref[:, COL_G + n * w2:COL_G + (n + 1) * w2] = _silu(g).astype(BF16)
        proj_ref[:, COL_AQ:COL_AQ + BR_W] = _rope(aq * QA_SCALE, ta_ref, A_QK // 4).astype(BF16)
        proj_ref[:, COL_AK:COL_AK + BR_W] = _rope(ak, ta_ref, A_QK // 4).astype(BF16)
        proj_ref[:, COL_AV:COL_AV + BR_W] = av.astype(BF16)
        proj_ref[:, COL_CQ:COL_CQ + BR_W] = (cq * QH_SCALE).astype(BF16)
        proj_ref[:, COL_CK:COL_CK + w2] = ckv.astype(BF16)
        proj_ref[:, COL_DQ:COL_DQ + BR_W] = _rope(dq * QH_SCALE, td_ref, HD // 4).astype(BF16)
        proj_ref[:, COL_DK:COL_DK + BR_W] = _rope(dk4, td_ref, HD // 4).astype(BF16)
        proj_ref[:, COL_DV:COL_DV + BR_W] = dv4.astype(BF16)

        @pl.when(is_ctx)
        def _():
            akt = ak.T
            for s in range(SEQ_PER_TILE):
                for h in range(A_HEADS):
                    for m in range(2):
                        c0 = h * HD + m * A_QK
                        dk_ref[s, 0, h, m] = akt[c0:c0 + A_QK, s * SEQ:(s + 1) * SEQ]
            _store_heads(dv_ref, av, A_HEADS, HD)
            _store_heads(nk_ref, ckv[:, :BR_W], C_HEADS, HD)
            _store_heads(nv_ref, ckv[:, BR_W:], C_HEADS, HD)
            for ref, val in ((sk_ref, dk4), (sv_ref, dv4)):
                vt = val.T
                for s in range(SEQ_PER_TILE):
                    for kv in range(D_KV):
                        ref[s, 0, kv] = vt[kv * D_GROUP * HD:kv * D_GROUP * HD + HD, s * SEQ:(s + 1) * SEQ]


def _cache_shapes():
    hs = lambda n: (BATCH, DEPTH, n, HD, SEQ)
    return [(BATCH, DEPTH, A_HEADS, 2, A_QK, SEQ), hs(A_HEADS), hs(C_HEADS), hs(C_HEADS), hs(D_KV), hs(D_KV)]


def _inproj(xa, xb, xb_off, mod3, l, norm_g, w_in, tab_a, tab_d, caches):
    row = lambda i: (_tile(i), 0)
    ctx_blk = lambda i: jnp.minimum(_tile(i), N_CTX_TILES - 1)

    def cache_spec(shape):
        blk = (SEQ_PER_TILE, 1) + shape[2:]
        nz = len(shape) - 2
        return pl.BlockSpec(blk, lambda i: (ctx_blk(i), l) + (0,) * nz)

    cshapes = _cache_shapes()
    aliases = {} if caches is None else {7 + k: 3 + k for k in range(6)}
    extra_specs = [] if caches is None else [pl.BlockSpec(memory_space=pl.ANY)] * 6
    extra_args = [] if caches is None else list(caches)
    outs = pl.pallas_call(
        functools.partial(_inproj_kernel, len(extra_args)),
        grid=(W_STEPS + N_TILES,),
        in_specs=[
            pl.BlockSpec((TM, D_MODEL), lambda i: (ctx_blk(i), 0)),
            pl.BlockSpec((TM, D_MODEL), lambda i: (jnp.maximum(_tile(i) - N_CTX_TILES, 0) + xb_off, 0)),
            pl.BlockSpec((1, 1, 3 * D_MODEL), lambda i: (l * 16 + _mod_row(_tile(i)), 0, 0)),
            pl.BlockSpec((1, 1, D_MODEL), lambda i: (l, 0, 0)),
            pl.BlockSpec((1, W_ROWS, IN_W), lambda i: (l, _wchunk(i), 0)),
            pl.BlockSpec((3, TM, 128), lambda i: (0, _rope_blk(_tile(i)), 0)),
            pl.BlockSpec((3, TM, 128), lambda i: (0, _rope_blk(_tile(i)), 0)),
        ] + extra_specs,
        out_specs=[
            pl.BlockSpec((TM, D_MODEL), row),
            pl.BlockSpec((TM, PROJ_W), row),
            pl.BlockSpec((TM, BR_W), row),
        ] + [cache_spec(s) for s in cshapes],
        out_shape=[
            jax.ShapeDtypeStruct((N_TOK, D_MODEL), BF16),
            jax.ShapeDtypeStruct((N_TOK, PROJ_W), BF16),
            jax.ShapeDtypeStruct((N_TOK, BR_W), F32),
        ] + [jax.ShapeDtypeStruct(s, F32) for s in cshapes],
        scratch_shapes=[pltpu.VMEM((D_MODEL, WS_W), BF16)],
        input_output_aliases=aliases,
        compiler_params=_cparams(("arbitrary",)),
        name=f"inproj_l{l}",
    )(xa, xb, mod3, norm_g, w_in, tab_a, tab_d, *extra_args)
    return outs[0], outs[1], outs[2], tuple(outs[3:])


def _diff_lambda(lam_ref, lam_init):
    lv = lam_ref[0]
    s1 = jnp.sum(lv[0:1] * lv[1:2], axis=-1, keepdims=True)
    s2 = jnp.sum(lv[2:3] * lv[3:4], axis=-1, keepdims=True)
    return jnp.exp(s1) - jnp.exp(s2) + lam_init


def _diff_norm(o, dg_ref, lam_init):
    y = o * lax.rsqrt(jnp.mean(o * o, axis=-1, keepdims=True) + EPS) * dg_ref[0]
    return y * (1.0 - lam_init)


def _exp2_parts(parts, extra=None):
    m = functools.reduce(jnp.maximum, [jnp.max(s, axis=-1, keepdims=True) for s in parts])
    if extra is not None:
        m = jnp.maximum(m, extra)
    return [jnp.exp2(s - m).astype(BF16) for s in parts], m


def _pipelined(n, score_fn, pv_fn):
    outs = []
    nxt = score_fn(0)
    for c in range(n):
        cur, nxt = nxt, (score_fn(c + 1) if c + 1 < n else None)
        outs.append(pv_fn(c, cur))
    return outs


def _with_ones(v):
    return jnp.concatenate([v, jnp.ones_like(v)], axis=1)


def _with_ones_t(vt):
    return jnp.concatenate([vt, jnp.ones((16, vt.shape[1]), vt.dtype)], axis=0)


def _sink_col(sink2, kv, rows):
    r = lax.broadcasted_iota(jnp.int32, (D_GROUP * rows, 1), 0)
    return jnp.where(r < rows, sink2[:, kv * D_GROUP:kv * D_GROUP + 1], sink2[:, kv * D_GROUP + 1:kv * D_GROUP + 2])


CTX_PER_STEP = 2


def _lane_masks(n):
    lane = lax.broadcasted_iota(jnp.int32, (1, BR_W), 1)
    w = BR_W // n
    ms = [jnp.where((lane >= p * w) & (lane < (p + 1) * w), 1.0, 0.0).astype(BF16) for p in range(n)]
    return ms, [1.0 - m for m in ms]


def _swap_halves(p):
    return jnp.concatenate([p[:, BR_W // 2:], p[:, :BR_W // 2]], axis=1)


def _pick_heads(os):
    grp = lax.broadcasted_iota(jnp.int32, os[0].shape, 1) // HD
    acc = os[0]
    for h in range(1, len(os)):
        acc = jnp.where(grp == h, os[h], acc)
    return acc


def _head_rms(o, dg4_ref, lam_init):
    r = lax.broadcasted_iota(jnp.int32, (BR_W, BR_W), 0) // HD
    c = lax.broadcasted_iota(jnp.int32, (BR_W, BR_W), 1) // HD
    ss = _dot((o * o).astype(BF16), jnp.where(r == c, 1.0, 0.0).astype(BF16))
    return o * lax.rsqrt(ss * (1.0 / HD) + EPS) * dg4_ref[0] * (1.0 - lam_init)


def _ctx_attn_kernel(n_alias, lam_init, aq_ref, ak_ref, av_ref, cq_ref, ck_ref, cv_ref, dq_ref, dk_ref, dv_ref,
                     ga_ref, gc_ref, gd_ref, lam_ref, dg4_ref, sink_ref,
                     bx_ref, gr_ref, cw_ref, cb_ref, wg_ref, bg_ref, llam_ref, *refs):
    z_ref, fin_ref = refs[n_alias:]
    seqs = [slice(s * SEQ, (s + 1) * SEQ) for s in range(CTX_PER_STEP)]

    for s, rs in enumerate(seqs):
        xc = _lru_conv(bx_ref[rs, :], cw_ref[0], cb_ref[0])
        pre = _dot(xc.astype(BF16), wg_ref[0]) + bg_ref[0]
        h0 = jnp.zeros((1, BR_W), F32)
        yf, hf = _lru_scan(0, xc, pre[:, :2 * BR_W], llam_ref[0, 0:1, :], h0)
        yb, hb = _lru_scan(1, xc, pre[:, 2 * BR_W:], llam_ref[0, 1:2, :], h0)
        z_ref[rs, BR_W:2 * BR_W] = ((yf + yb) * gr_ref[rs, :].astype(F32)).astype(BF16)
        fin_ref[s, 0, 0:1, :] = hf
        fin_ref[s, 0, 1:2, :] = hb

    lam = _diff_lambda(lam_ref, lam_init)
    sink2 = sink_ref[0] * LOG2E
    m_qk, _ = _lane_masks(2 * A_HEADS)
    m_hd, o_hd = _lane_masks(C_HEADS)

    sa, sc, sd = [], [], []
    for rs in seqs:
        aq, cq, dq = aq_ref[rs, :], cq_ref[rs, :], dq_ref[rs, :]
        ak, ck, dk = ak_ref[rs, :], ck_ref[rs, :], dk_ref[rs, :]
        sa.append([_dot_nt(aq * m, ak) for m in m_qk])
        sc.append([_dot_nt(cq * m, ck) for m in m_hd])
        sd.append([_dot_nt(dq * m, dk) for m in m_hd])

    ea =[[_exp2_parts([s])[0][0] for s in ss] for ss in sa]
    ec = [[_exp2_parts([s])[0][0] for s in ss] for ss in sc]
    ed = [[_exp2_parts([s], extra=sink2[:, j:j + 1]) for j, s in enumerate(ss)] for ss in sd]

    for i, rs in enumerate(seqs):
        av, cv, dv = av_ref[rs, :], cv_ref[rs, :], dv_ref[rs, :]
        os = []
        for h in range(A_HEADS):
            w = av * m_hd[h] + o_hd[h]
            p0, p1 = _dot(ea[i][2 * h], w), _dot(ea[i][2 * h + 1], w)
            os.append(p0 * (1.0 / _swap_halves(p0)) - p1 * (lam / _swap_halves(p1)))
        z_ref[rs, 0:BR_W] = (_head_rms(_pick_heads(os), dg4_ref, lam_init) * ga_ref[rs, :].astype(F32)).astype(BF16)

        os = []
        for h in range(C_HEADS):
            p = _dot(ec[i][h], cv * m_hd[h] + o_hd[h])
            os.append(p * (1.0 / _swap_halves(p)))
        z_ref[rs, 2 * BR_W:3 * BR_W] = (_pick_heads(os) * gc_ref[rs, :].astype(F32)).astype(BF16)

        os = []
        for j in range(D_HEADS):
            (e,), m = ed[i][j]
            p = _dot(e, dv * m_hd[j] + o_hd[j])
            os.append(p * (1.0 / (_swap_halves(p) + jnp.exp2(sink2[:, j:j + 1] - m))))
        z_ref[rs, 3 * BR_W:4 * BR_W] = (_pick_heads(os) * gd_ref[rs, :].astype(F32)).astype(BF16)


def _ctx_attn(proj, bx, l, diff_lambda, dg4, sink3, lru_params, states):
    lam_init = 0.8 - 0.6 * math.exp(-0.3 * l)
    colblk = lambda cb: pl.BlockSpec((CTX_PER_STEP * SEQ, BR_W), lambda b: (b, cb))
    extra = [] if states is None else [states]
    return pl.pallas_call(
        functools.partial(_ctx_attn_kernel, len(extra), lam_init),
        grid=(BATCH // CTX_PER_STEP,),
        in_specs=[colblk(COL_AQ // BR_W), colblk(COL_AK // BR_W), colblk(COL_AV // BR_W),
                  colblk(COL_CQ // BR_W), colblk(COL_CK // BR_W), colblk(COL_CV // BR_W),
                  colblk(COL_DQ // BR_W), colblk(COL_DK // BR_W), colblk(COL_DV // BR_W),
                  colblk(0), colblk(2), colblk(3),
                  pl.BlockSpec((1, 4, A_QK), lambda b: (l, 0, 0)),
                  pl.BlockSpec((1, 1, BR_W), lambda b: (l, 0, 0)),
                  pl.BlockSpec((1, 1, D_HEADS), lambda b: (l, 0, 0)),
                  pl.BlockSpec((CTX_PER_STEP * SEQ, BR_W), lambda b: (b, 0)), colblk(1)]
                 + _lru_specs(l, lambda b: (l, 0, 0)) + [pl.BlockSpec(memory_space=pl.ANY)] * len(extra),
        out_specs=[pl.BlockSpec((CTX_PER_STEP * SEQ, N_BRANCH * BR_W), lambda b: (b, 0)),
                   pl.BlockSpec((CTX_PER_STEP, 1, 2, BR_W), lambda b: (b, l, 0, 0))],
        out_shape=[jax.ShapeDtypeStruct((N_CTX, N_BRANCH * BR_W), BF16),
                   jax.ShapeDtypeStruct((BATCH, DEPTH, 2, BR_W), F32)],
        input_output_aliases={} if states is None else {22: 1},
        compiler_params=_cparams(("arbitrary",)),
        name=f"ctx_attn_l{l}",
    )(*([proj] * 12), diff_lambda, dg4, sink3, bx, proj, *lru_params, *extra)


LA_QB = 256


def _lat_diff_kernel(lam_init, aq_ref, ak_ref, av_ref, ck_ref, cv_ref, ga_ref, lam_ref, dg_ref, za_ref):
    lam = _diff_lambda(lam_ref, lam_init)
    aq = aq_ref[...]
    va_ctx = [_with_ones_t(cv_ref[0, 0, h].astype(BF16)) for h in range(A_HEADS)]
    va_loc = [_with_ones(av_ref[:, h * HD:(h + 1) * HD]) for h in range(A_HEADS)]

    def scores(c):
        h, m = divmod(c, 2)
        lo = h * HD + m * A_QK
        q = aq[:, lo:lo + A_QK]
        return [_dot(q, ck_ref[0, 0, h, m].astype(BF16)),
                _dot_nt(q, ak_ref[:, lo:lo + A_QK])]

    def pv(c, s):
        (e_ctx, e_loc), _ = _exp2_parts(s)
        p_ctx, p_loc = _dot_nt(e_ctx, va_ctx[c // 2]), _dot(e_loc, va_loc[c // 2])
        return p_ctx[:, :HD] + p_loc[:, :HD], p_ctx[:, HD:HD + 1] + p_loc[:, HD:HD + 1]

    ss = [scores(c) for c in range(2 * A_HEADS)]
    nd = [pv(c, s) for c, s in enumerate(ss)]
    outs = []
    for h in range(A_HEADS):
        (n0, d0), (n1, d1) = nd[2 * h], nd[2 * h + 1]
        outs.append(_diff_norm(n0 * (1.0 / d0) - n1 * (lam / d1), dg_ref, lam_init))
    za_ref[...] = (jnp.concatenate(outs, axis=1) * ga_ref[...].astype(F32)).astype(BF16)


def _lat_diff(proj, cache_k, cache_v, l, diff_lambda, dg3):
    lam_init = 0.8 - 0.6 * math.exp(-0.3 * l)
    nq = DEC_SEQ // LA_QB
    qrow = lambda b, j: N_CTX // LA_QB + b * nq + j
    srow = lambda b, j: N_CTX // DEC_SEQ + b
    return pl.pallas_call(
        functools.partial(_lat_diff_kernel, lam_init),
        grid=(DEC_BATCH, nq),
        in_specs=[pl.BlockSpec((LA_QB, BR_W), lambda b, j: (qrow(b, j), COL_AQ // BR_W)),
                  pl.BlockSpec((DEC_SEQ, BR_W), lambda b, j: (srow(b, j), COL_AK // BR_W)),
                  pl.BlockSpec((DEC_SEQ, BR_W), lambda b, j: (srow(b, j), COL_AV // BR_W)),
                  pl.BlockSpec((1, 1, A_HEADS, 2, A_QK, PAST_LEN), lambda b, j: (b, l, 0, 0, 0, 0)),
                  pl.BlockSpec((1, 1, A_HEADS, HD, PAST_LEN), lambda b, j: (b, l, 0, 0, 0)),
                  pl.BlockSpec((LA_QB, BR_W), lambda b, j: (qrow(b, j), 0)),
                  pl.BlockSpec((1, 4, A_QK), lambda b, j: (l, 0, 0)),
                  pl.BlockSpec((1, 1, HD), lambda b, j: (l, 0, 0))],
        out_specs=pl.BlockSpec((LA_QB, BR_W), lambda b, j: (b * nq + j, 0)),
        out_shape=jax.ShapeDtypeStruct((N_LAT, BR_W), BF16),
        compiler_params=_cparams(("arbitrary", "arbitrary")),
        name=f"lat_diff_l{l}",
    )(proj, proj, proj, cache_k, cache_v, proj, diff_lambda, dg3)


def _na_bias_kernel(rpb_ref, o_ref):
    l, h = pl.program_id(0), pl.program_id(1)
    base = (l * C_HEADS + h) * ((2 * NA_ROWS - 1) * (2 * NA_COLS - 1))
    qc = lax.broadcasted_iota(jnp.int32, (GRID_W, GRID_W), 0)
    kc = lax.broadcasted_iota(jnp.int32, (GRID_W, GRID_W), 1)
    cs = jnp.clip(qc - NA_COLS // 2, 0, GRID_W - NA_COLS)
    ok = (kc >= cs) & (kc < cs + NA_COLS)
    dcol = kc - qc + (NA_COLS - 1)
    neg = jnp.full((GRID_W, GRID_W), NEG, F32)
    tabs = []
    for d in range(2 * NA_ROWS - 1):
        t = neg
        for dc in range(2 * NA_COLS - 1):
            t = jnp.where(ok & (dcol == dc), rpb_ref[base + d * (2 * NA_COLS - 1) + dc] * LOG2E, t)
        tabs.append(t)
    for half in range(2):
        for rq in range(NA_ROWS):
            r = half * NA_ROWS + rq
            rs = min(max(r - NA_ROWS // 2, 0), GRID_ROWS - NA_ROWS)
            blks = []
            for kr in range(NA_KROWS):
                kabs = half * 4 + kr
                blks.append(tabs[kabs - r + NA_ROWS - 1] if rs <= kabs < rs + NA_ROWS else neg)
            o_ref[0, 0, half, rq * GRID_W:(rq + 1) * GRID_W, :] = jnp.concatenate(blks, axis=1)


def _na_bias(na_rpb):
    return pl.pallas_call(
        _na_bias_kernel,
        grid=(DEPTH, C_HEADS),
        in_specs=[pl.BlockSpec(memory_space=pltpu.SMEM)],
        out_specs=pl.BlockSpec((1, 1, 2, NA_HALF_Q, NA_KSPAN), lambda l, h: (l, h, 0, 0, 0)),
        out_shape=jax.ShapeDtypeStruct((DEPTH, C_HEADS, 2, NA_HALF_Q, NA_KSPAN), F32),
        compiler_params=_cparams(("arbitrary", "arbitrary")),
        name="na_bias",
    )(na_rpb.reshape(-1))


def _lat_na_kernel(half, cq_ref, ck_ref, cv_ref, kc_ref, vc_ref, bias_ref, gc_ref,
                   bx_ref, gr_ref, cw_ref, cb_ref, wg_ref, bg_ref, llam_ref, st_ref, zc_ref, y_ref):
    k0 = half * NA_HALF_KSTART
    cq = cq_ref[...]
    kl_all = ck_ref[k0:k0 + NA_KSPAN, :]
    vl_all = cv_ref[k0:k0 + NA_KSPAN, :]

    def scores(h):
        q = cq[:, h * HD:(h + 1) * HD]
        return [_dot(q, kc_ref[0, 0, h].astype(BF16)),
                _dot_nt(q, kl_all[:, h * HD:(h + 1) * HD]) + bias_ref[0, h, 0]]

    def pv(h, s):
        (e_ctx, e_loc), _ = _exp2_parts(s)
        p_ctx = _dot_nt(e_ctx, _with_ones_t(vc_ref[0, 0, h].astype(BF16)))
        p_loc = _dot(e_loc, _with_ones(vl_all[:, h * HD:(h + 1) * HD]))
        return (p_ctx[:, :HD] + p_loc[:, :HD]) * (1.0 / (p_ctx[:, HD:HD + 1] + p_loc[:, HD:HD + 1]))

    xc = _lru_conv(bx_ref[...], cw_ref[0], cb_ref[0])
    pre = _dot(xc.astype(BF16), wg_ref[0, :, half * 2 * BR_W:(half + 1) * 2 * BR_W]) \
        + bg_ref[0, :, half * 2 * BR_W:(half + 1) * 2 * BR_W]
    y, _ = _lru_scan(half, xc, pre, llam_ref[0, half:half + 1, :], st_ref[0, 0, half:half + 1, :])
    y_ref[...] = y * gr_ref[...].astype(F32)

    ss = [scores(h) for h in range(C_HEADS)]
    outs = [pv(h, s) for h, s in enumerate(ss)]
    zc_ref[...] = (jnp.concatenate(outs, axis=1) * gc_ref[...].astype(F32)).astype(BF16)


def _lat_na(half, proj, bx, cache_k, cache_v, bias, l, lru_params, state_lru):
    qrow = lambda b: N_CTX // NA_HALF_Q + b * 2 + half
    srow = lambda b: N_CTX // DEC_SEQ + b
    return pl.pallas_call(
        functools.partial(_lat_na_kernel, half),
        grid=(DEC_BATCH,),
        in_specs=[pl.BlockSpec((NA_HALF_Q, BR_W), lambda b: (qrow(b), COL_CQ // BR_W)),
                  pl.BlockSpec((DEC_SEQ, BR_W), lambda b: (srow(b), COL_CK // BR_W)),
                  pl.BlockSpec((DEC_SEQ, BR_W), lambda b: (srow(b), COL_CV // BR_W)),
                  pl.BlockSpec((1, 1, C_HEADS, HD, PAST_LEN), lambda b: (b, l, 0, 0, 0)),
                  pl.BlockSpec((1, 1, C_HEADS, HD, PAST_LEN), lambda b: (b, l, 0, 0, 0)),
                  pl.BlockSpec((1, C_HEADS, 1, NA_HALF_Q, NA_KSPAN), lambda b: (l, 0, half, 0, 0)),
                  pl.BlockSpec((NA_HALF_Q, BR_W), lambda b: (qrow(b), 2)),
                  pl.BlockSpec((DEC_SEQ, BR_W), lambda b: (srow(b), 0)),
                  pl.BlockSpec((DEC_SEQ, BR_W), lambda b: (srow(b), 1))]
                 + _lru_specs(l, lambda b: (l, 0, 0))
                 + [pl.BlockSpec((1, 1, 2, BR_W), lambda b: (b, l, 0, 0))],
        out_specs=[pl.BlockSpec((NA_HALF_Q, BR_W), lambda b: (b, 0)),
                   pl.BlockSpec((DEC_SEQ, BR_W), lambda b: (b, 0))],
        out_shape=[jax.ShapeDtypeStruct((N_LAT // 2, BR_W), BF16), jax.ShapeDtypeStruct((N_LAT, BR_W), F32)],
        compiler_params=_cparams(("arbitrary",)),
        name=f"lat_na{half}_l{l}",
    )(proj, proj, proj, cache_k, cache_v, bias, proj, bx, proj, *lru_params, state_lru)


def _lat_swa_kernel(dq_ref, dk_ref, dv_ref, kc_ref, vc_ref, sink_ref, gd_ref, zd_ref):
    nq = DEC_SEQ // SWA_QB
    rows = D_GROUP * SWA_QB
    dq, dk, dv = dq_ref[...], dk_ref[...], dv_ref[...]
    sink2 = sink_ref[0] * LOG2E
    m_hd, _ = _lane_masks(D_HEADS)
    starts = [min(max(j * SWA_QB - WIN, 0), DEC_SEQ - SWA_SPAN) for j in range(nq)]
    qi = lax.broadcasted_iota(jnp.int32, (rows, SWA_SPAN), 0) % SWA_QB
    ki = lax.broadcasted_iota(jnp.int32, (rows, SWA_SPAN), 1)
    masks = {off: jnp.abs(qi + off - ki) <= WIN for off in sorted({j * SWA_QB - s for j, s in enumerate(starts)})}
    kc_t = jnp.concatenate([kc_ref[0, 0, kv] for kv in range(D_KV) for _ in range(D_GROUP)], axis=0).astype(BF16)
    vc_t = jnp.concatenate([vc_ref[0, 0, kv] for kv in range(D_KV) for _ in range(D_GROUP)], axis=0).astype(BF16)
    feat = lax.broadcasted_iota(jnp.int32, (BR_W, 1), 0) // (D_GROUP * HD)

    qm = [dq * m for m in m_hd]
    sks = [_sink_col(sink2, kv, SWA_QB) for kv in range(D_KV)]
    w_loc, w_ctx = [], []
    for kv in range(D_KV):
        pair = m_hd[kv * D_GROUP] + m_hd[kv * D_GROUP + 1]
        w_loc.append(dv * pair + (1.0 - pair))
        w_ctx.append(jnp.where(feat == kv, vc_t, jnp.ones_like(vc_t)))

    def scores(c):
        kv, j = divmod(c, nq)
        q2 = jnp.concatenate([qm[kv * D_GROUP + g][j * SWA_QB:(j + 1) * SWA_QB] for g in range(D_GROUP)], axis=0)
        return [_dot(q2, kc_t),
                jnp.where(masks[j * SWA_QB - starts[j]], _dot_nt(q2, dk[starts[j]:starts[j] + SWA_SPAN]), NEG)]

    def pv(c, s):
        kv, j = divmod(c, nq)
        (e_ctx, e_loc), m = _exp2_parts(s, extra=sks[kv])
        p = _dot_nt(e_ctx, w_ctx[kv]) + _dot(e_loc, w_loc[kv][starts[j]:starts[j] + SWA_SPAN])
        return p * (1.0 / (_swap_halves(p) + jnp.exp2(sks[kv] - m)))

    o = _pipelined(D_KV * nq, scores, pv)
    os = [jnp.concatenate([o[kv * nq + j][g * SWA_QB:(g + 1) * SWA_QB] for j in range(nq)], axis=0)
          for kv in range(D_KV) for g in range(D_GROUP)]
    zd_ref[...] = (_pick_heads(os) * gd_ref[...].astype(F32)).astype(BF16)


def _lat_swa(proj, cache_k, cache_v, sink3, l):
    srow = lambda b: (N_CTX // DEC_SEQ + b)
    return pl.pallas_call(
        _lat_swa_kernel,
        grid=(DEC_BATCH,),
        in_specs=[pl.BlockSpec((DEC_SEQ, BR_W), lambda b: (srow(b), COL_DQ // BR_W)),
                  pl.BlockSpec((DEC_SEQ, BR_W), lambda b: (srow(b), COL_DK // BR_W)),
                  pl.BlockSpec((DEC_SEQ, BR_W), lambda b: (srow(b), COL_DV // BR_W)),
                  pl.BlockSpec((1, 1, D_KV, HD, PAST_LEN), lambda b: (b, l, 0, 0, 0)),
                  pl.BlockSpec((1, 1, D_KV, HD, PAST_LEN), lambda b: (b, l, 0, 0, 0)),
                  pl.BlockSpec((1, 1, D_HEADS), lambda b: (l, 0, 0)),
                  pl.BlockSpec((DEC_SEQ, BR_W), lambda b: (srow(b), 3))],
        out_specs=pl.BlockSpec((DEC_SEQ, BR_W), lambda b: (b, 0)),
        out_shape=jax.ShapeDtypeStruct((N_LAT, BR_W), BF16),
        compiler_params=_cparams(("arbitrary",)),
        name=f"lat_swa_l{l}",
    )(proj, proj, proj, cache_k, cache_v, sink3, proj)


def _lru_conv(x, cw, cb):
    seq = x.shape[0]
    x3 = x.reshape(seq // 8, 8, BR_W)
    tin = lax.broadcasted_iota(jnp.int32, (1, 8, 1), 1)
    zero = jnp.zeros((1, 8, BR_W), F32)

    def at(shift):
        r = pltpu.roll(x3, (-shift) % 8, 1)
        if shift < 0:
            return jnp.where(tin < -shift, jnp.concatenate([zero, r[:-1]], axis=0), r)
        return jnp.where(tin >= 8 - shift, jnp.concatenate([r[1:], zero], axis=0), r)

    xc = cb + at(-1) * cw[0:1] + x3 * cw[1:2] + at(1) * cw[2:3] + at(2) * cw[3:4]
    return xc.reshape(seq, BR_W)


def _lru_scan(d, xc, pre, lam_row, h0):
    seq = xc.shape[0]
    nl = -lam_row
    softplus = jnp.maximum(nl, 0.0) + jnp.log1p(jnp.exp(-jnp.abs(nl)))
    r = jax.nn.sigmoid(pre[:, :BR_W])
    ig = jax.nn.sigmoid(pre[:, BR_W:])
    log_a = -LRU_C * r * softplus
    a = jnp.exp(log_a)
    u = jnp.exp(0.5 * jnp.log(-jnp.tanh(log_a) * (a * a + 1.0))) * (ig * xc)
    tin = lax.broadcasted_iota(jnp.int32, (1, 8, 1), 1)
    a = a.reshape(seq // 8, 8, BR_W)
    u = u.reshape(seq // 8, 8, BR_W)
    for s in (1, 2, 4):
        if d == 0:
            keep = tin >= s
            a_n = jnp.where(keep, pltpu.roll(a, s, 1), 1.0)
            u_n = jnp.where(keep, pltpu.roll(u, s, 1), 0.0)
        else:
            keep = tin < 8 - s
            a_n = jnp.where(keep, pltpu.roll(a, 8 - s, 1), 1.0)
            u_n = jnp.where(keep, pltpu.roll(u, 8 - s, 1), 0.0)
        u = u + a * u_n
        a = a * a_n
    nt = seq // 8
    ys = [None] * nt
    h = h0
    for k in (range(nt) if d == 0 else range(nt - 1, -1, -1)):
        y = u[k] + a[k] * h
        ys[k] = y
        h = y[7:8, :] if d == 0 else y[0:1, :]
    return jnp.concatenate(ys, axis=0), h


def _lru_specs(l, imap):
    return [pl.BlockSpec((1, CONV_W, BR_W), imap), pl.BlockSpec((1, 1, BR_W), imap),
            pl.BlockSpec((1, BR_W, 4 * BR_W), imap), pl.BlockSpec((1, 1, 4 * BR_W), imap),
            pl.BlockSpec((1, 2, BR_W), imap)]


def _merge_kernel(layer, final, xa_ref, xb_ref, mod_ref, h_ref, zctx_ref, zal_ref, yf_ref, yb_ref,
                  zc0_ref, zc1_ref, zdl_ref, wmg_hbm, bmg_ref, wbo_ref, wo_ref, nf_ref, *refs):
    out_refs, (wmg_s, wbo_s, wo_s, stage, sem) = refs[:-5], refs[-5:]
    i = pl.program_id(0)

    def wmg_copy(chunk, n):
        slot = chunk % 2
        return pltpu.make_async_copy(wmg_hbm.at[layer, pl.ds(chunk * W_ROWS, W_ROWS), n, :],
                                     stage.at[slot, n], sem.at[slot, n])

    @pl.when(i == 0)
    def _():
        for n in range(N_BRANCH):
            wmg_copy(0, n).start()

    @pl.when(i < W_STEPS - 1)
    def _():
        for n in range(N_BRANCH):
            wmg_copy(i + 1, n).start()

    @pl.when(i < W_STEPS)
    def _():
        r0 = pl.multiple_of(i * W_ROWS, W_ROWS)
        for n in range(N_BRANCH):
            wmg_copy(i, n).wait()
            wmg_s[pl.ds(r0, W_ROWS), n * D_MODEL:(n + 1) * D_MODEL] = stage[i % 2, n].astype(BF16)
        _cast_rows(i, wbo_ref, wbo_s)
        _cast_rows(i, wo_ref, wo_s)

    @pl.when(i >= W_STEPS)
    def _():
        t = i - W_STEPS
        x = _load_x(t, xa_ref, xb_ref)
        gate = mod_ref[0][:, 2 * D_MODEL:]
        h = h_ref[...]
        is_ctx = t < N_CTX_TILES
        second_half = (t - N_CTX_TILES) % LAT_TILES_PER_SEQ == 1
        z_lat = [zal_ref[...],
                 (yf_ref[...] + yb_ref[...]).astype(BF16),
                 jnp.where(second_half, zc1_ref[...], zc0_ref[...]),
                 zdl_ref[...]]
        zs = [jnp.where(is_ctx, zctx_ref[:, n * BR_W:(n + 1) * BR_W], z_lat[n]) for n in range(N_BRANCH)]
        bmg = bmg_ref[0]
        cols = []
        for c in range(0, D_MODEL, 512):
            acc = None
            for n in range(N_BRANCH):
                g = jax.nn.sigmoid(_dot(h, wmg_s[:, n * D_MODEL + c:n * D_MODEL + c + 512])
                                   + bmg[n:n + 1, c:c + 512])
                term = g * _dot(zs[n], wbo_s[n * BR_W:(n + 1) * BR_W, c:c + 512])
                acc = term if acc is None else acc + term
            cols.append(acc.astype(BF16))
        merged = jnp.concatenate(cols, axis=1)
        xn = x + gate * _dot(merged, wo_s[...])
        if not final:
            out_refs[0][...] = xn
        else:
            y = xn * lax.rsqrt(jnp.mean(xn * xn, axis=-1, keepdims=True) + EPS) * nf_ref[...]

            @pl.when(t < N_CTX_TILES)
            def _():
                out_refs[0][...] = y

            @pl.when(t >= N_CTX_TILES)
            def _():
                out_refs[1][...] = y


def _merge(final, xa, xb, xb_off, mod3, l, h, z_ctx, za_lat, y_f, y_b, zc_lat, zd_lat,
           w_mg, b_mg, w_bo, w_o, norm_f):
    assert TM == NA_HALF_Q
    row = lambda i: (_tile(i), 0)
    ctx_blk = lambda i: jnp.minimum(_tile(i), N_CTX_TILES - 1)
    lat_blk = lambda i: jnp.maximum(_tile(i) - N_CTX_TILES, 0)
    lat_spec = pl.BlockSpec((TM, BR_W), lambda i: (lat_blk(i), 0))
    half_spec = pl.BlockSpec((TM, BR_W), lambda i: (lat_blk(i) // LAT_TILES_PER_SEQ, 0))
    if final:
        out_specs = [pl.BlockSpec((TM, D_MODEL), lambda i: (ctx_blk(i), 0)),
                     pl.BlockSpec((TM, D_MODEL), lambda i: (lat_blk(i), 0))]
        out_shape = [jax.ShapeDtypeStruct((N_CTX, D_MODEL), F32), jax.ShapeDtypeStruct((N_LAT, D_MODEL), F32)]
    else:
        out_specs = pl.BlockSpec((TM, D_MODEL), row)
        out_shape = jax.ShapeDtypeStruct((N_TOK, D_MODEL), F32)
    wchunk = lambda width: pl.BlockSpec((1, W_ROWS, width), lambda i: (l, _wchunk(i), 0))
    return pl.pallas_call(
        functools.partial(_merge_kernel, l, final),
        grid=(W_STEPS + N_TILES,),
        in_specs=[
            pl.BlockSpec((TM, D_MODEL), lambda i: (ctx_blk(i), 0)),
            pl.BlockSpec((TM, D_MODEL), lambda i: (lat_blk(i) + xb_off, 0)),
            pl.BlockSpec((1, 1, 3 * D_MODEL), lambda i: (l * 16 + _mod_row(_tile(i)), 0, 0)),
            pl.BlockSpec((TM, D_MODEL), row),
            pl.BlockSpec((TM, N_BRANCH * BR_W), lambda i: (ctx_blk(i), 0)),
            lat_spec,
            lat_spec, lat_spec,
            half_spec, half_spec,
            lat_spec,
            pl.BlockSpec(memory_space=pl.ANY),
            pl.BlockSpec((1, N_BRANCH, D_MODEL), lambda i: (l, 0, 0)),
            wchunk(D_MODEL),
            wchunk(D_MODEL),
            pl.BlockSpec((1, D_MODEL), lambda i: (0, 0)),
        ],
        out_specs=out_specs,
        out_shape=out_shape,
        scratch_shapes=[pltpu.VMEM((D_MODEL, N_BRANCH * D_MODEL), BF16),
                        pltpu.VMEM((N_BRANCH * BR_W, D_MODEL), BF16),
                        pltpu.VMEM((D_MODEL, D_MODEL), BF16),
                        pltpu.VMEM((2, N_BRANCH, W_ROWS, D_MODEL), F32),
                        pltpu.SemaphoreType.DMA((2, N_BRANCH))],
        compiler_params=_cparams(("arbitrary",)),
        name=f"merge_l{l}",
    )(xa, xb, mod3, h, z_ctx, za_lat, y_f, y_b, zc_lat[0], zc_lat[1], zd_lat,
      w_mg, b_mg,
      w_bo.reshape(DEPTH, N_BRANCH * BR_W, D_MODEL), w_o, norm_f.reshape(1, D_MODEL))


def _lru_gate_weights(lru_wa, lru_ba, lru_wx, lru_bx):
    w = jnp.stack([lru_wa[:, 0], lru_wx[:, 0], lru_wa[:, 1], lru_wx[:, 1]], axis=1)
    eye = jnp.eye(B_BLOCKS, dtype=w.dtype)
    dense = w[:, :, :, :, None, :] * eye[None, None, :, None, :, None]
    wg = dense.transpose(0, 2, 3, 1, 4, 5).reshape(DEPTH, BR_W, 4 * BR_W).astype(BF16)
    bg = jnp.stack([lru_ba[:, 0], lru_bx[:, 0], lru_ba[:, 1], lru_bx[:, 1]], axis=1).reshape(DEPTH, 1, 4 * BR_W)
    return wg, bg


def kernel(x_prompt, x_sample, cache_diff_k, cache_diff_v, cache_na_k, cache_na_v, cache_swa_k, cache_swa_v,
           state_lru, c, c_ctx, norm_g, w_ada, b_ada, w_in, diff_lambda, diff_norm_g, conv_w, conv_b,
           lru_wa, lru_ba, lru_wx, lru_bx, lru_lam, na_rpb, swa_sink, w_mg, b_mg, w_bo, w_o, norm_f):
    tab_a_np, _ = _rope_tables(A_QK)
    tab_d_np, _ = _rope_tables(HD)
    tab_a, tab_d = jnp.asarray(tab_a_np), jnp.asarray(tab_d_np)

    cvecs = jnp.concatenate([c_ctx[None, :], c, jnp.zeros((16 - 1 - DEC_BATCH, D_MODEL), F32)], axis=0)
    mod3 = _modulation(cvecs, w_ada, b_ada).reshape(DEPTH * 16, 1, 3 * D_MODEL)
    bias = _na_bias(na_rpb)
    wg, bg = _lru_gate_weights(lru_wa, lru_ba, lru_wx, lru_bx)
    norm_g3 = norm_g.reshape(DEPTH, 1, D_MODEL)
    dg3 = diff_norm_g.reshape(DEPTH, 1, HD)
    dg4 = jnp.tile(diff_norm_g, (1, A_HEADS)).reshape(DEPTH, 1, BR_W)
    sink3 = swa_sink.reshape(DEPTH, 1, D_HEADS)
    conv_b3 = conv_b.reshape(DEPTH, 1, BR_W)
    past = [jnp.swapaxes(t, -1, -2) for t in
            (cache_diff_k, cache_diff_v, cache_na_k, cache_na_v, cache_swa_k, cache_swa_v)]

    xa = x_prompt.reshape(N_CTX, D_MODEL)
    xb = x_sample.reshape(N_LAT, D_MODEL)
    xb_off = 0
    caches = states = None
    y_p = y_s = None
    for l in range(DEPTH):
        h, proj, bx, caches = _inproj(xa, xb, xb_off, mod3, l, norm_g3, w_in, tab_a, tab_d, caches)

        lru_params = (conv_w, conv_b3, wg, bg, lru_lam)
        z_ctx, states = _ctx_attn(proj, bx, l, diff_lambda, dg4, sink3, lru_params, states)
        za_lat = _lat_diff(proj, past[0], past[1], l, diff_lambda, dg3)
        zc0, y_f = _lat_na(0, proj, bx, past[2], past[3], bias, l, lru_params, state_lru)
        zc1, y_b = _lat_na(1, proj, bx, past[2], past[3], bias, l, lru_params, state_lru)
        zd_lat = _lat_swa(proj, past[4], past[5], sink3, l)

        final = l == DEPTH - 1
        out = _merge(final, xa, xb, xb_off, mod3, l, h, z_ctx, za_lat, y_f, y_b, (zc0, zc1), zd_lat,
                     w_mg, b_mg, w_bo, w_o, norm_f)
        if final:
            y_p, y_s = out
        else:
            xa = xb = out
            xb_off = N_CTX_TILES

    new_caches = [jnp.swapaxes(t, -1, -2) for t in caches]
    return (y_p.reshape(BATCH, SEQ, D_MODEL), y_s.reshape(DEC_BATCH, DEC_SEQ, D_MODEL), *new_caches, states)
```

```python
import functools
import math

import numpy as np
import jax
import jax.numpy as jnp
from jax import lax
from jax.experimental import pallas as pl
from jax.experimental.pallas import tpu as pltpu

F32 = jnp.float32
BF16 = jnp.bfloat16

D_MODEL = 1024
BATCH = 16
SEQ = 256
DEPTH = 2
DEC_BATCH = 8
DEC_SEQ = 1024
PAST_LEN = 512
GRID_W = 64
N_BRANCH = 4
BR_W = D_MODEL // 4
HD = 64
A_HEADS = BR_W // HD
A_QK = HD // 2
B_BLOCKS = 4
B_BLK = BR_W // B_BLOCKS
CONV_W = 4
LRU_C = 8.0
C_HEADS = BR_W // HD
NA_ROWS = 8
NA_COLS = 16
D_HEADS = BR_W // HD
D_KV = 2
D_GROUP = D_HEADS // D_KV
WIN = 128
ROPE_BASE = 10000.0
EPS = 1e-6
NEG = -1e30

N_CTX = BATCH * SEQ
N_LAT = DEC_BATCH * DEC_SEQ
N_TOK = N_CTX + N_LAT
GRID_ROWS = DEC_SEQ // GRID_W

TM = 512
N_CTX_TILES = N_CTX // TM
N_TILES = N_TOK // TM
LAT_TILES_PER_SEQ = DEC_SEQ // TM

PROJ_W = 3328
COL_G = 0
COL_AQ = 1024
COL_AK = 1280
COL_AV = 1536
COL_CQ = 1792
COL_CK = 2048
COL_CV = 2304
COL_DQ = 2560
COL_DK = 2816
COL_DV = 3072
IN_W = 3328
WS_W = IN_W + 2 * D_KV * HD

NA_HALF_Q = DEC_SEQ // 2
NA_KROWS = 12
NA_KSPAN = NA_KROWS * GRID_W
NA_HALF_KSTART = 4 * GRID_W

SWA_QB = 128
SWA_SPAN = SWA_QB + 2 * WIN

LOG2E = math.log2(math.e)
QA_SCALE = A_QK ** -0.5 * LOG2E
QH_SCALE = HD ** -0.5 * LOG2E

VMEM_LIMIT = 56 * 1024 * 1024


def _cparams(sem):
    return pltpu.CompilerParams(dimension_semantics=sem, vmem_limit_bytes=VMEM_LIMIT)


def _dot(a, b):
    return jnp.dot(a, b, preferred_element_type=F32)


def _dot_nt(a, b):
    return lax.dot_general(a, b, (((1,), (1,)), ((), ())), preferred_element_type=F32)


def _silu(x):
    return x * jax.nn.sigmoid(x)


def _rope_tables(d):
    half = d // 2
    quarter = half // 2
    lane = np.arange(128)
    q = lane % d
    use_col = (q >= half)
    i = (q % half) % quarter
    first = (q % half) < quarter
    inv = (ROPE_BASE ** (-np.arange(quarter, dtype=np.float32) / np.float32(quarter))).astype(np.float32)
    t = np.arange(DEC_SEQ)
    pos = np.where(use_col[None, :], (t % GRID_W)[:, None], (t // GRID_W)[:, None]).astype(np.float32)
    ang = (pos * inv[i][None, :]).astype(np.float32)
    cos, sin = np.cos(ang), np.sin(ang)
    c = np.concatenate([np.ones((TM, 128), np.float32), cos], axis=0)
    s1 = np.concatenate([np.zeros((TM, 128), np.float32), np.where(first[None, :], -sin, 0.0)], axis=0)
    s2 = np.concatenate([np.zeros((TM, 128), np.float32), np.where(first[None, :], 0.0, sin)], axis=0)
    return np.stack([c, s1, s2]).astype(np.float32), quarter


def _rope128(x, tab_ref, shift):
    return (x * tab_ref[0] + pltpu.roll(x, 128 - shift, 1) * tab_ref[1]
            + pltpu.roll(x, shift, 1) * tab_ref[2])


def _rope(x, tab_ref, shift):
    w = x.shape[1]
    return jnp.concatenate([_rope128(x[:, c:c + 128], tab_ref, shift) for c in range(0, w, 128)], axis=1)


def _mod_kernel(c_ref, w_ref, b_ref, o_ref):
    c = c_ref[...]
    o_ref[0] = _dot(_silu(c).astype(BF16), w_ref[0].astype(BF16)) + b_ref[0]


def _modulation(cvecs, w_ada, b_ada):
    nb = 3 * D_MODEL // 768
    return pl.pallas_call(
        _mod_kernel,
        grid=(DEPTH, nb),
        in_specs=[pl.BlockSpec((16, D_MODEL), lambda l, j: (0, 0)),
                  pl.BlockSpec((1, D_MODEL, 768), lambda l, j: (l, 0, j)),
                  pl.BlockSpec((1, 1, 768), lambda l, j: (l, 0, j))],
        out_specs=pl.BlockSpec((1, 16, 768), lambda l, j: (l, 0, j)),
        out_shape=jax.ShapeDtypeStruct((DEPTH, 16, 3 * D_MODEL), F32),
        compiler_params=_cparams(("arbitrary", "arbitrary")),
        name="modulation",
    )(cvecs, w_ada, b_ada.reshape(DEPTH, 1, 3 * D_MODEL))


W_STEPS = 8
W_ROWS = D_MODEL // W_STEPS


def _tile(i):
    return jnp.maximum(i - W_STEPS, 0)


def _wchunk(i):
    return jnp.minimum(i, W_STEPS - 1)


def _mod_row(t):
    return jnp.where(t < N_CTX_TILES, 0, 1 + (t - N_CTX_TILES) // LAT_TILES_PER_SEQ)


def _rope_blk(t):
    return jnp.where(t < N_CTX_TILES, 0, 1 + (t - N_CTX_TILES) % LAT_TILES_PER_SEQ)


def _load_x(t, xa_ref, xb_ref):
    return jnp.where(t < N_CTX_TILES, xa_ref[...], xb_ref[...])


def _cast_rows(i, src_ref, dst_ref):
    r0 = pl.multiple_of(i * W_ROWS, W_ROWS)
    dst_ref[pl.ds(r0, W_ROWS), :] = src_ref[0].astype(BF16)


WI_G, WI_AQ, WI_AK, WI_AV, WI_BX, WI_CQ, WI_CKV, WI_DQ, WI_DKV = 0, 1024, 1280, 1536, 1792, 2048, 2304, 2816, 3072
SEQ_PER_TILE = TM // SEQ


def _store_heads(ref, val, n_heads, width):
    vt = val.T
    for s in range(SEQ_PER_TILE):
        for h in range(n_heads):
            ref[s, 0, h] = vt[h * width:(h + 1) * width, s * SEQ:(s + 1) * SEQ]


def _inproj_kernel(n_alias, xa_ref, xb_ref, mod_ref, g_ref, w_ref, ta_ref, td_ref, *refs):
    h_ref, proj_ref, bx_ref, dk_ref, dv_ref, nk_ref, nv_ref, sk_ref, sv_ref, w_s = refs[n_alias:]
    i = pl.program_id(0)

    @pl.when(i < W_STEPS)
    def _():
        r0 = pl.multiple_of(i * W_ROWS, W_ROWS)
        wb = w_ref[0].astype(BF16)
        w_s[pl.ds(r0, W_ROWS), 0:WI_DKV] = wb[:, 0:WI_DKV]
        for n, src in enumerate((0, 0, 1, 1, 2, 2, 3, 3)):
            w_s[pl.ds(r0, W_ROWS), WI_DKV + n * HD:WI_DKV + (n + 1) * HD] = \
                wb[:, WI_DKV + src * HD:WI_DKV + (src + 1) * HD]

    @pl.when(i >= W_STEPS)
    def _():
        t = i - W_STEPS
        is_ctx = t < N_CTX_TILES
        x = _load_x(t, xa_ref, xb_ref)
        mod = mod_ref[0]
        shift, scale = mod[:, :D_MODEL], mod[:, D_MODEL:2 * D_MODEL]
        xn = x * lax.rsqrt(jnp.mean(x * x, axis=-1, keepdims=True) + EPS) * g_ref[0]
        hb = (xn * (1.0 + scale) + shift).astype(BF16)
        h_ref[...] = hb

        def mm(c0, width):
            return _dot(hb, w_s[:, c0:c0 + width])

        w2 = 2 * BR_W
        gs = [mm(WI_G + c, w2) for c in range(0, N_BRANCH * BR_W, w2)]
        aq, ak, av = mm(WI_AQ, BR_W), mm(WI_AK, BR_W), mm(WI_AV, BR_W)
        cq, ckv = mm(WI_CQ, BR_W), mm(WI_CKV, w2)
        dq, dk4, dv4 = mm(WI_DQ, BR_W), mm(WI_DKV, BR_W), mm(WI_DKV + BR_W, BR_W)
        bx_ref[...] = mm(WI_BX, BR_W)

        for n, g in enumerate(gs):
            proj_ref[:, COL_G + n * w2:COL_G + (n + 1) * w2] = _silu(g).astype(BF16)
        proj_ref[:, COL_AQ:COL_AQ + BR_W] = _rope(aq * QA_SCALE, ta_ref, A_QK // 4).astype(BF16)
        proj_ref[:, COL_AK:COL_AK + BR_W] = _rope(ak, ta_ref, A_QK // 4).astype(BF16)
        proj_ref[:, COL_AV:COL_AV + BR_W] = av.astype(BF16)
        proj_ref[:, COL_CQ:COL_CQ + BR_W] = (cq * QH_SCALE).astype(BF16)
        proj_ref[:, COL_CK:COL_CK + w2] = ckv.astype(BF16)
        proj_ref[:, COL_DQ:COL_DQ + BR_W] = _rope(dq * QH_SCALE, td_ref, HD // 4).astype(BF16)
        proj_ref[:, COL_DK:COL_DK + BR_W] = _rope(dk4, td_ref, HD // 4).astype(BF16)
        proj_ref[:, COL_DV:COL_DV + BR_W] = dv4.astype(BF16)

        @pl.when(is_ctx)
        def _():
            akt = ak.T
            for s in range(SEQ_PER_TILE):
                for h in range(A_HEADS):
                    for m in range(2):
                        c0 = h * HD + m * A_QK
                        dk_ref[s, 0, h, m] = akt[c0:c0 + A_QK, s * SEQ:(s + 1) * SEQ]
            _store_heads(dv_ref, av, A_HEADS, HD)
            _store_heads(nk_ref, ckv[:, :BR_W], C_HEADS, HD)
            _store_heads(nv_ref, ckv[:, BR_W:], C_HEADS, HD)
            for ref, val in ((sk_ref, dk4), (sv_ref, dv4)):
                vt = val.T
                for s in range(SEQ_PER_TILE):
                    for kv in range(D_KV):
                        ref[s, 0, kv] = vt[kv * D_GROUP * HD:kv * D_GROUP * HD + HD, s * SEQ:(s + 1) * SEQ]


def _cache_shapes():
    hs = lambda n: (BATCH, DEPTH, n, HD, SEQ)
    return [(BATCH, DEPTH, A_HEADS, 2, A_QK, SEQ), hs(A_HEADS), hs(C_HEADS), hs(C_HEADS), hs(D_KV), hs(D_KV)]


def _inproj(xa, xb, xb_off, mod3, l, norm_g, w_in, tab_a, tab_d, caches):
    row = lambda i: (_tile(i), 0)
    ctx_blk = lambda i: jnp.minimum(_tile(i), N_CTX_TILES - 1)

    def cache_spec(shape):
        blk = (SEQ_PER_TILE, 1) + shape[2:]
        nz = len(shape) - 2
        return pl.BlockSpec(blk, lambda i: (ctx_blk(i), l) + (0,) * nz)

    cshapes = _cache_shapes()
    aliases = {} if caches is None else {7 + k: 3 + k for k in range(6)}
    extra_specs = [] if caches is None else [pl.BlockSpec(memory_space=pl.ANY)] * 6
    extra_args = [] if caches is None else list(caches)
    outs = pl.pallas_call(
        functools.partial(_inproj_kernel, len(extra_args)),
        grid=(W_STEPS + N_TILES,),
        in_specs=[
            pl.BlockSpec((TM, D_MODEL), lambda i: (ctx_blk(i), 0)),
            pl.BlockSpec((TM, D_MODEL), lambda i: (jnp.maximum(_tile(i) - N_CTX_TILES, 0) + xb_off, 0)),
            pl.BlockSpec((1, 1, 3 * D_MODEL), lambda i: (l * 16 + _mod_row(_tile(i)), 0, 0)),
            pl.BlockSpec((1, 1, D_MODEL), lambda i: (l, 0, 0)),
            pl.BlockSpec((1, W_ROWS, IN_W), lambda i: (l, _wchunk(i), 0)),
            pl.BlockSpec((3, TM, 128), lambda i: (0, _rope_blk(_tile(i)), 0)),
            pl.BlockSpec((3, TM, 128), lambda i: (0, _rope_blk(_tile(i)), 0)),
        ] + extra_specs,
        out_specs=[
            pl.BlockSpec((TM, D_MODEL), row),
            pl.BlockSpec((TM, PROJ_W), row),
            pl.BlockSpec((TM, BR_W), row),
        ] + [cache_spec(s) for s in cshapes],
        out_shape=[
            jax.ShapeDtypeStruct((N_TOK, D_MODEL), BF16),
            jax.ShapeDtypeStruct((N_TOK, PROJ_W), BF16),
            jax.ShapeDtypeStruct((N_TOK, BR_W), F32),
        ] + [jax.ShapeDtypeStruct(s, F32) for s in cshapes],
        scratch_shapes=[pltpu.VMEM((D_MODEL, WS_W), BF16)],
        input_output_aliases=aliases,
        compiler_params=_cparams(("arbitrary",)),
        name=f"inproj_l{l}",
    )(xa, xb, mod3, norm_g, w_in, tab_a, tab_d, *extra_args)
    return outs[0], outs[1], outs[2], tuple(outs[3:])


def _diff_lambda(lam_ref, lam_init):
    lv = lam_ref[0]
    s1 = jnp.sum(lv[0:1] * lv[1:2], axis=-1, keepdims=True)
    s2 = jnp.sum(lv[2:3] * lv[3:4], axis=-1, keepdims=True)
    return jnp.exp(s1) - jnp.exp(s2) + lam_init


def _diff_norm(o, dg_ref, lam_init):
    y = o * lax.rsqrt(jnp.mean(o * o, axis=-1, keepdims=True) + EPS) * dg_ref[0]
    return y * (1.0 - lam_init)


def _exp2_parts(parts, extra=None):
    m = functools.reduce(jnp.maximum, [jnp.max(s, axis=-1, keepdims=True) for s in parts])
    if extra is not None:
        m = jnp.maximum(m, extra)
    return [jnp.exp2(s - m).astype(BF16) for s in parts], m


def _pipelined(n, score_fn, pv_fn):
    outs = []
    nxt = score_fn(0)
    for c in range(n):
        cur, nxt = nxt, (score_fn(c + 1) if c + 1 < n else None)
        outs.append(pv_fn(c, cur))
    return outs


def _with_ones(v):
    return jnp.concatenate([v, jnp.ones_like(v)], axis=1)


def _with_ones_t(vt):
    return jnp.concatenate([vt, jnp.ones((16, vt.shape[1]), vt.dtype)], axis=0)


def _sink_col(sink2, kv, rows):
    r = lax.broadcasted_iota(jnp.int32, (D_GROUP * rows, 1), 0)
    return jnp.where(r < rows, sink2[:, kv * D_GROUP:kv * D_GROUP + 1], sink2[:, kv * D_GROUP + 1:kv * D_GROUP + 2])


CTX_PER_STEP = 2


def _lane_masks(n):
    lane = lax.broadcasted_iota(jnp.int32, (1, BR_W), 1)
    w = BR_W // n
    ms = [jnp.where((lane >= p * w) & (lane < (p + 1) * w), 1.0, 0.0).astype(BF16) for p in range(n)]
    return ms, [1.0 - m for m in ms]


def _swap_halves(p):
    return jnp.concatenate([p[:, BR_W // 2:], p[:, :BR_W // 2]], axis=1)


def _pick_heads(os):
    grp = lax.broadcasted_iota(jnp.int32, os[0].shape, 1) // HD
    acc = os[0]
    for h in range(1, len(os)):
        acc = jnp.where(grp == h, os[h], acc)
    return acc


def _head_rms(o, dg4_ref, lam_init):
    r = lax.broadcasted_iota(jnp.int32, (BR_W, BR_W), 0) // HD
    c = lax.broadcasted_iota(jnp.int32, (BR_W, BR_W), 1) // HD
    ss = _dot((o * o).astype(BF16), jnp.where(r == c, 1.0, 0.0).astype(BF16))
    return o * lax.rsqrt(ss * (1.0 / HD) + EPS) * dg4_ref[0] * (1.0 - lam_init)


def _ctx_attn_kernel(n_alias, lam_init, aq_ref, ak_ref, av_ref, cq_ref, ck_ref, cv_ref, dq_ref, dk_ref, dv_ref,
                     ga_ref, gc_ref, gd_ref, lam_ref, dg4_ref, sink_ref,
                     bx_ref, gr_ref, cw_ref, cb_ref, wa_ref, ba_ref, wx_ref, bxg_ref, llam_ref, *refs):
    z_ref, fin_ref = refs[n_alias:]
    seqs = [slice(s * SEQ, (s + 1) * SEQ) for s in range(CTX_PER_STEP)]

    xcs = [_lru_conv(bx_ref[rs, :], cw_ref[0], cb_ref[0]) for rs in seqs]
    xc_all = jnp.concatenate(xcs, axis=0)
    pres = [_lru_gates(d, xc_all, wa_ref, ba_ref, wx_ref, bxg_ref) for d in range(2)]
    for s, rs in enumerate(seqs):
        h0 = jnp.zeros((1, BR_W), F32)
        yf, hf = _lru_scan(0, xcs[s], [p[rs] for p in pres[0]], llam_ref[0, 0:1, :], h0)
        yb, hb = _lru_scan(1, xcs[s], [p[rs] for p in pres[1]], llam_ref[0, 1:2, :], h0)
        z_ref[rs, BR_W:2 * BR_W] = ((yf + yb) * gr_ref[rs, :].astype(F32)).astype(BF16)
        fin_ref[s, 0, 0:1, :] = hf
        fin_ref[s, 0, 1:2, :] = hb

    lam = _diff_lambda(lam_ref, lam_init)
    sink2 = sink_ref[0] * LOG2E
    m_qk, _ = _lane_masks(2 * A_HEADS)
    m_hd, o_hd = _lane_masks(C_HEADS)

    sa, sc, sd = [], [], []
    for rs in seqs:
        aq, cq, dq = aq_ref[rs, :], cq_ref[rs, :], dq_ref[rs, :]
        ak, ck, dk = ak_ref[rs, :], ck_ref[rs, :], dk_ref[rs, :]
        sa.append([_dot_nt(aq * m, ak) for m in m_qk])
        sc.append([_dot_nt(cq * m, ck) for m in m_hd])
        sd.append([_dot_nt(dq * m, dk) for m in m_hd])

    ea =[[_exp2_parts([s])[0][0] for s in ss] for ss in sa]
    ec = [[_exp2_parts([s])[0][0] for s in ss] for ss in sc]
    ed = [[_exp2_parts([s], extra=sink2[:, j:j + 1]) for j, s in enumerate(ss)] for ss in sd]

    for i, rs in enumerate(seqs):
        av, cv, dv = av_ref[rs, :], cv_ref[rs, :], dv_ref[rs, :]
        os = []
        for h in range(A_HEADS):
            w = av * m_hd[h] + o_hd[h]
            p0, p1 = _dot(ea[i][2 * h], w), _dot(ea[i][2 * h + 1], w)
            os.append(p0 * (1.0 / _swap_halves(p0)) - p1 * (lam / _swap_halves(p1)))
        z_ref[rs, 0:BR_W] = (_head_rms(_pick_heads(os), dg4_ref, lam_init) * ga_ref[rs, :].astype(F32)).astype(BF16)

        os = []
        for h in range(C_HEADS):
            p = _dot(ec[i][h], cv * m_hd[h] + o_hd[h])
            os.append(p * (1.0 / _swap_halves(p)))
        z_ref[rs, 2 * BR_W:3 * BR_W] = (_pick_heads(os) * gc_ref[rs, :].astype(F32)).astype(BF16)

        os = []
        for j in range(D_HEADS):
            (e,), m = ed[i][j]
            p = _dot(e, dv * m_hd[j] + o_hd[j])
            os.append(p * (1.0 / (_swap_halves(p) + jnp.exp2(sink2[:, j:j + 1] - m))))
        z_ref[rs, 3 * BR_W:4 * BR_W] = (_pick_heads(os) * gd_ref[rs, :].astype(F32)).astype(BF16)


def _ctx_attn(proj, bx, l, diff_lambda, dg4, sink3, lru_params, states):
    lam_init = 0.8 - 0.6 * math.exp(-0.3 * l)
    colblk = lambda cb: pl.BlockSpec((CTX_PER_STEP * SEQ, BR_W), lambda b: (b, cb))
    extra = [] if states is None else [states]
    return pl.pallas_call(
        functools.partial(_ctx_attn_kernel, len(extra), lam_init),
        grid=(BATCH // CTX_PER_STEP,),
        in_specs=[colblk(COL_AQ // BR_W), colblk(COL_AK // BR_W), colblk(COL_AV // BR_W),
                  colblk(COL_CQ // BR_W), colblk(COL_CK // BR_W), colblk(COL_CV // BR_W),
                  colblk(COL_DQ // BR_W), colblk(COL_DK // BR_W), colblk(COL_DV // BR_W),
                  colblk(0), colblk(2), colblk(3),
                  pl.BlockSpec((1, 4, A_QK), lambda b: (l, 0, 0)),
                  pl.BlockSpec((1, 1, BR_W), lambda b: (l, 0, 0)),
                  pl.BlockSpec((1, 1, D_HEADS), lambda b: (l, 0, 0)),
                  pl.BlockSpec((CTX_PER_STEP * SEQ, BR_W), lambda b: (b, 0)), colblk(1)]
                 + _lru_specs(l, lambda b: (l, 0, 0)) + [pl.BlockSpec(memory_space=pl.ANY)] * len(extra),
        out_specs=[pl.BlockSpec((CTX_PER_STEP * SEQ, N_BRANCH * BR_W), lambda b: (b, 0)),
                   pl.BlockSpec((CTX_PER_STEP, 1, 2, BR_W), lambda b: (b, l, 0, 0))],
        out_shape=[jax.ShapeDtypeStruct((N_CTX, N_BRANCH * BR_W), BF16),
                   jax.ShapeDtypeStruct((BATCH, DEPTH, 2, BR_W), F32)],
        input_output_aliases={} if states is None else {24: 1},
        compiler_params=_cparams(("arbitrary",)),
        name=f"ctx_attn_l{l}",
    )(*([proj] * 12), diff_lambda, dg4, sink3, bx, proj, *lru_params, *extra)


LA_QB = 256


def _lat_diff_kernel(lam_init, aq_ref, ak_ref, av_ref, ck_ref, cv_ref, ga_ref, lam_ref, dg_ref, za_ref):
    lam = _diff_lambda(lam_ref, lam_init)
    aq = aq_ref[...]
    va_ctx = [_with_ones_t(cv_ref[0, 0, h].astype(BF16)) for h in range(A_HEADS)]
    va_loc = [_with_ones(av_ref[:, h * HD:(h + 1) * HD]) for h in range(A_HEADS)]

    def scores(c):
        h, m = divmod(c, 2)
        lo = h * HD + m * A_QK
        q = aq[:, lo:lo + A_QK]
        return [_dot(q, ck_ref[0, 0, h, m].astype(BF16)),
                _dot_nt(q, ak_ref[:, lo:lo + A_QK])]

    def pv(c, s):
        (e_ctx, e_loc), _ = _exp2_parts(s)
        p_ctx, p_loc = _dot_nt(e_ctx, va_ctx[c // 2]), _dot(e_loc, va_loc[c // 2])
        return p_ctx[:, :HD] + p_loc[:, :HD], p_ctx[:, HD:HD + 1] + p_loc[:, HD:HD + 1]

    ss = [scores(c) for c in range(2 * A_HEADS)]
    nd = [pv(c, s) for c, s in enumerate(ss)]
    outs = []
    for h in range(A_HEADS):
        (n0, d0), (n1, d1) = nd[2 * h], nd[2 * h + 1]
        outs.append(_diff_norm(n0 * (1.0 / d0) - n1 * (lam / d1), dg_ref, lam_init))
    za_ref[...] = (jnp.concatenate(outs, axis=1) * ga_ref[...].astype(F32)).astype(BF16)


def _lat_diff(proj, cache_k, cache_v, l, diff_lambda, dg3):
    lam_init = 0.8 - 0.6 * math.exp(-0.3 * l)
    nq = DEC_SEQ // LA_QB
    qrow = lambda b, j: N_CTX // LA_QB + b * nq + j
    srow = lambda b, j: N_CTX // DEC_SEQ + b
    return pl.pallas_call(
        functools.partial(_lat_diff_kernel, lam_init),
        grid=(DEC_BATCH, nq),
        in_specs=[pl.BlockSpec((LA_QB, BR_W), lambda b, j: (qrow(b, j), COL_AQ // BR_W)),
                  pl.BlockSpec((DEC_SEQ, BR_W), lambda b, j: (srow(b, j), COL_AK // BR_W)),
                  pl.BlockSpec((DEC_SEQ, BR_W), lambda b, j: (srow(b, j), COL_AV // BR_W)),
                  pl.BlockSpec((1, 1, A_HEADS, 2, A_QK, PAST_LEN), lambda b, j: (b, l, 0, 0, 0, 0)),
                  pl.BlockSpec((1, 1, A_HEADS, HD, PAST_LEN), lambda b, j: (b, l, 0, 0, 0)),
                  pl.BlockSpec((LA_QB, BR_W), lambda b, j: (qrow(b, j), 0)),
                  pl.BlockSpec((1, 4, A_QK), lambda b, j: (l, 0, 0)),
                  pl.BlockSpec((1, 1, HD), lambda b, j: (l, 0, 0))],
        out_specs=pl.BlockSpec((LA_QB, BR_W), lambda b, j: (b * nq + j, 0)),
        out_shape=jax.ShapeDtypeStruct((N_LAT, BR_W), BF16),
        compiler_params=_cparams(("arbitrary", "arbitrary")),
        name=f"lat_diff_l{l}",
    )(proj, proj, proj, cache_k, cache_v, proj, diff_lambda, dg3)


def _na_bias_kernel(rpb_ref, o_ref):
    l, h = pl.program_id(0), pl.program_id(1)
    base = (l * C_HEADS + h) * ((2 * NA_ROWS - 1) * (2 * NA_COLS - 1))
    qc = lax.broadcasted_iota(jnp.int32, (GRID_W, GRID_W), 0)
    kc = lax.broadcasted_iota(jnp.int32, (GRID_W, GRID_W), 1)
    cs = jnp.clip(qc - NA_COLS // 2, 0, GRID_W - NA_COLS)
    ok = (kc >= cs) & (kc < cs + NA_COLS)
    dcol = kc - qc + (NA_COLS - 1)
    neg = jnp.full((GRID_W, GRID_W), NEG, F32)
    tabs = []
    for d in range(2 * NA_ROWS - 1):
        t = neg
        for dc in range(2 * NA_COLS - 1):
            t = jnp.where(ok & (dcol == dc), rpb_ref[base + d * (2 * NA_COLS - 1) + dc] * LOG2E, t)
        tabs.append(t)
    for half in range(2):
        for rq in range(NA_ROWS):
            r = half * NA_ROWS + rq
            rs = min(max(r - NA_ROWS // 2, 0), GRID_ROWS - NA_ROWS)
            blks = []
            for kr in range(NA_KROWS):
                kabs = half * 4 + kr
                blks.append(tabs[kabs - r + NA_ROWS - 1] if rs <= kabs < rs + NA_ROWS else neg)
            o_ref[0, 0, half, rq * GRID_W:(rq + 1) * GRID_W, :] = jnp.concatenate(blks, axis=1)


def _na_bias(na_rpb):
    return pl.pallas_call(
        _na_bias_kernel,
        grid=(DEPTH, C_HEADS),
        in_specs=[pl.BlockSpec(memory_space=pltpu.SMEM)],
        out_specs=pl.BlockSpec((1, 1, 2, NA_HALF_Q, NA_KSPAN), lambda l, h: (l, h, 0, 0, 0)),
        out_shape=jax.ShapeDtypeStruct((DEPTH, C_HEADS, 2, NA_HALF_Q, NA_KSPAN), F32),
        compiler_params=_cparams(("arbitrary", "arbitrary")),
        name="na_bias",
    )(na_rpb.reshape(-1))


def _lat_na_kernel(half, cq_ref, ck_ref, cv_ref, kc_ref, vc_ref, bias_ref, gc_ref,
                   bx_ref, gr_ref, cw_ref, cb_ref, wa_ref, ba_ref, wx_ref, bxg_ref, llam_ref, st_ref, zc_ref, y_ref):
    k0 = half * NA_HALF_KSTART
    cq = cq_ref[...]
    kl_all = ck_ref[k0:k0 + NA_KSPAN, :]
    vl_all = cv_ref[k0:k0 + NA_KSPAN, :]

    def scores(h):
        q = cq[:, h * HD:(h + 1) * HD]
        return [_dot(q, kc_ref[0, 0, h].astype(BF16)),
                _dot_nt(q, kl_all[:, h * HD:(h + 1) * HD]) + bias_ref[0, h, 0]]

    def pv(h, s):
        (e_ctx, e_loc), _ = _exp2_parts(s)
        p_ctx = _dot_nt(e_ctx, _with_ones_t(vc_ref[0, 0, h].astype(BF16)))
        p_loc = _dot(e_loc, _with_ones(vl_all[:, h * HD:(h + 1) * HD]))
        return (p_ctx[:, :HD] + p_loc[:, :HD]) * (1.0 / (p_ctx[:, HD:HD + 1] + p_loc[:, HD:HD + 1]))

    xc = _lru_conv(bx_ref[...], cw_ref[0], cb_ref[0])
    pre = _lru_gates(half, xc, wa_ref, ba_ref, wx_ref, bxg_ref)
    y, _ = _lru_scan(half, xc, pre, llam_ref[0, half:half + 1, :], st_ref[0, 0, half:half + 1, :])
    y_ref[...] = y * gr_ref[...].astype(F32)

    ss = [scores(h) for h in range(C_HEADS)]
    outs = [pv(h, s) for h, s in enumerate(ss)]
    zc_ref[...] = (jnp.concatenate(outs, axis=1) * gc_ref[...].astype(F32)).astype(BF16)


def _lat_na(half, proj, bx, cache_k, cache_v, bias, l, lru_params, state_lru):
    qrow = lambda b: N_CTX // NA_HALF_Q + b * 2 + half
    srow = lambda b: N_CTX // DEC_SEQ + b
    return pl.pallas_call(
        functools.partial(_lat_na_kernel, half),
        grid=(DEC_BATCH,),
        in_specs=[pl.BlockSpec((NA_HALF_Q, BR_W), lambda b: (qrow(b), COL_CQ // BR_W)),
                  pl.BlockSpec((DEC_SEQ, BR_W), lambda b: (srow(b), COL_CK // BR_W)),
                  pl.BlockSpec((DEC_SEQ, BR_W), lambda b: (srow(b), COL_CV // BR_W)),
                  pl.BlockSpec((1, 1, C_HEADS, HD, PAST_LEN), lambda b: (b, l, 0, 0, 0)),
                  pl.BlockSpec((1, 1, C_HEADS, HD, PAST_LEN), lambda b: (b, l, 0, 0, 0)),
                  pl.BlockSpec((1, C_HEADS, 1, NA_HALF_Q, NA_KSPAN), lambda b: (l, 0, half, 0, 0)),
                  pl.BlockSpec((NA_HALF_Q, BR_W), lambda b: (qrow(b), 2)),
                  pl.BlockSpec((DEC_SEQ, BR_W), lambda b: (srow(b), 0)),
                  pl.BlockSpec((DEC_SEQ, BR_W), lambda b: (srow(b), 1))]
                 + _lru_specs(l, lambda b: (l, 0, 0))
                 + [pl.BlockSpec((1, 1, 2, BR_W), lambda b: (b, l, 0, 0))],
        out_specs=[pl.BlockSpec((NA_HALF_Q, BR_W), lambda b: (b, 0)),
                   pl.BlockSpec((DEC_SEQ, BR_W), lambda b: (b, 0))],
        out_shape=[jax.ShapeDtypeStruct((N_LAT // 2, BR_W), BF16), jax.ShapeDtypeStruct((N_LAT, BR_W), F32)],
        compiler_params=_cparams(("arbitrary",)),
        name=f"lat_na{half}_l{l}",
    )(proj, proj, proj, cache_k, cache_v, bias, proj, bx, proj, *lru_params, state_lru)


def _lat_swa_kernel(dq_ref, dk_ref, dv_ref, kc_ref, vc_ref, sink_ref, gd_ref, zd_ref):
    nq = DEC_SEQ // SWA_QB
    rows = D_GROUP * SWA_QB
    dq, dk, dv = dq_ref[...], dk_ref[...], dv_ref[...]
    sink2 = sink_ref[0] * LOG2E
    m_hd, _ = _lane_masks(D_HEADS)
    starts = [min(max(j * SWA_QB - WIN, 0), DEC_SEQ - SWA_SPAN) for j in range(nq)]
    qi = lax.broadcasted_iota(jnp.int32, (rows, SWA_SPAN), 0) % SWA_QB
    ki = lax.broadcasted_iota(jnp.int32, (rows, SWA_SPAN), 1)
    masks = {off: jnp.abs(qi + off - ki) <= WIN for off in sorted({j * SWA_QB - s for j, s in enumerate(starts)})}
    kc_t = jnp.concatenate([kc_ref[0, 0, kv] for kv in range(D_KV) for _ in range(D_GROUP)], axis=0).astype(BF16)
    vc_t = jnp.concatenate([vc_ref[0, 0, kv] for kv in range(D_KV) for _ in range(D_GROUP)], axis=0).astype(BF16)
    feat = lax.broadcasted_iota(jnp.int32, (BR_W, 1), 0) // (D_GROUP * HD)

    qm = [dq * m for m in m_hd]
    sks = [_sink_col(sink2, kv, SWA_QB) for kv in range(D_KV)]
    w_loc, w_ctx = [], []
    for kv in range(D_KV):
        pair = m_hd[kv * D_GROUP] + m_hd[kv * D_GROUP + 1]
        w_loc.append(dv * pair + (1.0 - pair))
        w_ctx.append(jnp.where(feat == kv, vc_t, jnp.ones_like(vc_t)))

    def scores(c):
        kv, j = divmod(c, nq)
        q2 = jnp.concatenate([qm[kv * D_GROUP + g][j * SWA_QB:(j + 1) * SWA_QB] for g in range(D_GROUP)], axis=0)
        return [_dot(q2, kc_t),
                jnp.where(masks[j * SWA_QB - starts[j]], _dot_nt(q2, dk[starts[j]:starts[j] + SWA_SPAN]), NEG)]

    def pv(c, s):
        kv, j = divmod(c, nq)
        (e_ctx, e_loc), m = _exp2_parts(s, extra=sks[kv])
        p = _dot_nt(e_ctx, w_ctx[kv]) + _dot(e_loc, w_loc[kv][starts[j]:starts[j] + SWA_SPAN])
        return p * (1.0 / (_swap_halves(p) + jnp.exp2(sks[kv] - m)))

    o = _pipelined(D_KV * nq, scores, pv)
    os = [jnp.concatenate([o[kv * nq + j][g * SWA_QB:(g + 1) * SWA_QB] for j in range(nq)], axis=0)
          for kv in range(D_KV) for g in range(D_GROUP)]
    zd_ref[...] = (_pick_heads(os) * gd_ref[...].astype(F32)).astype(BF16)


def _lat_swa(proj, cache_k, cache_v, sink3, l):
    srow = lambda b: (N_CTX // DEC_SEQ + b)
    return pl.pallas_call(
        _lat_swa_kernel,
        grid=(DEC_BATCH,),
        in_specs=[pl.BlockSpec((DEC_SEQ, BR_W), lambda b: (srow(b), COL_DQ // BR_W)),
                  pl.BlockSpec((DEC_SEQ, BR_W), lambda b: (srow(b), COL_DK // BR_W)),
                  pl.BlockSpec((DEC_SEQ, BR_W), lambda b: (srow(b), COL_DV // BR_W)),
                  pl.BlockSpec((1, 1, D_KV, HD, PAST_LEN), lambda b: (b, l, 0, 0, 0)),
                  pl.BlockSpec((1, 1, D_KV, HD, PAST_LEN), lambda b: (b, l, 0, 0, 0)),
                  pl.BlockSpec((1, 1, D_HEADS), lambda b: (l, 0, 0)),
                  pl.BlockSpec((DEC_SEQ, BR_W), lambda b: (srow(b), 3))],
        out_specs=pl.BlockSpec((DEC_SEQ, BR_W), lambda b: (b, 0)),
        out_shape=jax.ShapeDtypeStruct((N_LAT, BR_W), BF16),
        compiler_params=_cparams(("arbitrary",)),
        name=f"lat_swa_l{l}",
    )(proj, proj, proj, cache_k, cache_v, sink3, proj)


def _lru_conv(x, cw, cb):
    seq = x.shape[0]
    x3 = x.reshape(seq // 8, 8, BR_W)
    tin = lax.broadcasted_iota(jnp.int32, (1, 8, 1), 1)
    zero = jnp.zeros((1, 8, BR_W), F32)

    def at(shift):
        r = pltpu.roll(x3, (-shift) % 8, 1)
        if shift < 0:
            return jnp.where(tin < -shift, jnp.concatenate([zero, r[:-1]], axis=0), r)
        return jnp.where(tin >= 8 - shift, jnp.concatenate([r[1:], zero], axis=0), r)

    xc = cb + at(-1) * cw[0:1] + x3 * cw[1:2] + at(1) * cw[2:3] + at(2) * cw[3:4]
    return xc.reshape(seq, BR_W)


def _gate_dense(w_ref, d):
    rows = []
    for n in range(B_BLOCKS):
        pieces = [jnp.zeros((B_BLK, n * B_BLK), BF16)] if n else []
        pieces.append(w_ref[0, d, n].astype(BF16))
        if n < B_BLOCKS - 1:
            pieces.append(jnp.zeros((B_BLK, (B_BLOCKS - 1 - n) * B_BLK), BF16))
        rows.append(jnp.concatenate(pieces, axis=1))
    return jnp.concatenate(rows, axis=0)


def _lru_gates(d, xc, wa_ref, ba_ref, wx_ref, bxg_ref):
    xb = xc.astype(BF16)
    return (_dot(xb, _gate_dense(wa_ref, d)) + ba_ref[0, d:d + 1, :],
            _dot(xb, _gate_dense(wx_ref, d)) + bxg_ref[0, d:d + 1, :])


def _lru_scan(d, xc, pre, lam_row, h0):
    seq = xc.shape[0]
    nl = -lam_row
    softplus = jnp.maximum(nl, 0.0) + jnp.log1p(jnp.exp(-jnp.abs(nl)))
    r = jax.nn.sigmoid(pre[0])
    ig = jax.nn.sigmoid(pre[1])
    log_a = -LRU_C * r * softplus
    a = jnp.exp(log_a)
    u = jnp.exp(0.5 * jnp.log(-jnp.tanh(log_a) * (a * a + 1.0))) * (ig * xc)
    tin = lax.broadcasted_iota(jnp.int32, (1, 8, 1), 1)
    a = a.reshape(seq // 8, 8, BR_W)
    u = u.reshape(seq // 8, 8, BR_W)
    for s in (1, 2, 4):
        if d == 0:
            keep = tin >= s
            a_n = jnp.where(keep, pltpu.roll(a, s, 1), 1.0)
            u_n = jnp.where(keep, pltpu.roll(u, s, 1), 0.0)
        else:
            keep = tin < 8 - s
            a_n = jnp.where(keep, pltpu.roll(a, 8 - s, 1), 1.0)
            u_n = jnp.where(keep, pltpu.roll(u, 8 - s, 1), 0.0)
        u = u + a * u_n
        a = a * a_n
    nt = seq // 8
    ys = [None] * nt
    h = h0
    for k in (range(nt) if d == 0 else range(nt - 1, -1, -1)):
        y = u[k] + a[k] * h
        ys[k] = y
        h = y[7:8, :] if d == 0 else y[0:1, :]
    return jnp.concatenate(ys, axis=0), h


def _lru_specs(l, imap):
    wspec = pl.BlockSpec((1, 2, B_BLOCKS, B_BLK, B_BLK), lambda *g: imap(*g) + (0, 0))
    vspec = pl.BlockSpec((1, 2, BR_W), imap)
    return [pl.BlockSpec((1, CONV_W, BR_W), imap), pl.BlockSpec((1, 1, BR_W), imap),
            wspec, vspec, wspec, vspec, vspec]


def _merge_kernel(layer, final, xa_ref, xb_ref, mod_ref, h_ref, zctx_ref, zal_ref, yf_ref, yb_ref,
                  zc0_ref, zc1_ref, zdl_ref, wmg_hbm, bmg_ref, wbo_ref, wo_ref, nf_ref, *refs):
    out_refs, (wmg_s, wbo_s, wo_s, stage, sem) = refs[:-5], refs[-5:]
    i = pl.program_id(0)

    def wmg_copy(chunk, n):
        slot = chunk % 2
        return pltpu.make_async_copy(wmg_hbm.at[layer, pl.ds(chunk * W_ROWS, W_ROWS), n, :],
                                     stage.at[slot, n], sem.at[slot, n])

    @pl.when(i == 0)
    def _():
        for n in range(N_BRANCH):
            wmg_copy(0, n).start()

    @pl.when(i < W_STEPS - 1)
    def _():
        for n in range(N_BRANCH):
            wmg_copy(i + 1, n).start()

    @pl.when(i < W_STEPS)
    def _():
        r0 = pl.multiple_of(i * W_ROWS, W_ROWS)
        for n in range(N_BRANCH):
            wmg_copy(i, n).wait()
            wmg_s[pl.ds(r0, W_ROWS), n * D_MODEL:(n + 1) * D_MODEL] = stage[i % 2, n].astype(BF16)
        _cast_rows(i, wbo_ref, wbo_s)
        _cast_rows(i, wo_ref, wo_s)

    @pl.when(i >= W_STEPS)
    def _():
        t = i - W_STEPS
        x = _load_x(t, xa_ref, xb_ref)
        gate = mod_ref[0][:, 2 * D_MODEL:]
        h = h_ref[...]
        is_ctx = t < N_CTX_TILES
        second_half = (t - N_CTX_TILES) % LAT_TILES_PER_SEQ == 1
        z_lat = [zal_ref[...],
                 (yf_ref[...] + yb_ref[...]).astype(BF16),
                 jnp.where(second_half, zc1_ref[...], zc0_ref[...]),
                 zdl_ref[...]]
        zs = [jnp.where(is_ctx, zctx_ref[:, n * BR_W:(n + 1) * BR_W], z_lat[n]) for n in range(N_BRANCH)]
        bmg = bmg_ref[0]
        cols = []
        for c in range(0, D_MODEL, 512):
            acc = None
            for n in range(N_BRANCH):
                g = jax.nn.sigmoid(_dot(h, wmg_s[:, n * D_MODEL + c:n * D_MODEL + c + 512])
                                   + bmg[n:n + 1, c:c + 512])
                term = g * _dot(zs[n], wbo_s[n * BR_W:(n + 1) * BR_W, c:c + 512])
                acc = term if acc is None else acc + term
            cols.append(acc.astype(BF16))
        merged = jnp.concatenate(cols, axis=1)
        xn = x + gate * _dot(merged, wo_s[...])
        if not final:
            out_refs[0][...] = xn
        else:
            y = xn * lax.rsqrt(jnp.mean(xn * xn, axis=-1, keepdims=True) + EPS) * nf_ref[...]

            @pl.when(t < N_CTX_TILES)
            def _():
                out_refs[0][...] = y

            @pl.when(t >= N_CTX_TILES)
            def _():
                out_refs[1][...] = y


def _merge(final, xa, xb, xb_off, mod3, l, h, z_ctx, za_lat, y_f, y_b, zc_lat, zd_lat,
           w_mg, b_mg, w_bo, w_o, norm_f):
    assert TM == NA_HALF_Q
    row = lambda i: (_tile(i), 0)
    ctx_blk = lambda i: jnp.minimum(_tile(i), N_CTX_TILES - 1)
    lat_blk = lambda i: jnp.maximum(_tile(i) - N_CTX_TILES, 0)
    lat_spec = pl.BlockSpec((TM, BR_W), lambda i: (lat_blk(i), 0))
    half_spec = pl.BlockSpec((TM, BR_W), lambda i: (lat_blk(i) // LAT_TILES_PER_SEQ, 0))
    if final:
        out_specs = [pl.BlockSpec((TM, D_MODEL), lambda i: (ctx_blk(i), 0)),
                     pl.BlockSpec((TM, D_MODEL), lambda i: (lat_blk(i), 0))]
        out_shape = [jax.ShapeDtypeStruct((N_CTX, D_MODEL), F32), jax.ShapeDtypeStruct((N_LAT, D_MODEL), F32)]
    else:
        out_specs = pl.BlockSpec((TM, D_MODEL), row)
        out_shape = jax.ShapeDtypeStruct((N_TOK, D_MODEL), F32)
    wchunk = lambda width: pl.BlockSpec((1, W_ROWS, width), lambda i: (l, _wchunk(i), 0))
    return pl.pallas_call(
        functools.partial(_merge_kernel, l, final),
        grid=(W_STEPS + N_TILES,),
        in_specs=[
            pl.BlockSpec((TM, D_MODEL), lambda i: (ctx_blk(i), 0)),
            pl.BlockSpec((TM, D_MODEL), lambda i: (lat_blk(i) + xb_off, 0)),
            pl.BlockSpec((1, 1, 3 * D_MODEL), lambda i: (l * 16 + _mod_row(_tile(i)), 0, 0)),
            pl.BlockSpec((TM, D_MODEL), row),
            pl.BlockSpec((TM, N_BRANCH * BR_W), lambda i: (ctx_blk(i), 0)),
            lat_spec,
            lat_spec, lat_spec,
            half_spec, half_spec,
            lat_spec,
            pl.BlockSpec(memory_space=pl.ANY),
            pl.BlockSpec((1, N_BRANCH, D_MODEL), lambda i: (l, 0, 0)),
            wchunk(D_MODEL),
            wchunk(D_MODEL),
            pl.BlockSpec((1, D_MODEL), lambda i: (0, 0)),
        ],
        out_specs=out_specs,
        out_shape=out_shape,
        scratch_shapes=[pltpu.VMEM((D_MODEL, N_BRANCH * D_MODEL), BF16),
                        pltpu.VMEM((N_BRANCH * BR_W, D_MODEL), BF16),
                        pltpu.VMEM((D_MODEL, D_MODEL), BF16),
                        pltpu.VMEM((2, N_BRANCH, W_ROWS, D_MODEL), F32),
                        pltpu.SemaphoreType.DMA((2, N_BRANCH))],
        compiler_params=_cparams(("arbitrary",)),
        name=f"merge_l{l}",
    )(xa, xb, mod3, h, z_ctx, za_lat, y_f, y_b, zc_lat[0], zc_lat[1], zd_lat,
      w_mg, b_mg,
      w_bo.reshape(DEPTH, N_BRANCH * BR_W, D_MODEL), w_o, norm_f.reshape(1, D_MODEL))


def kernel(x_prompt, x_sample, cache_diff_k, cache_diff_v, cache_na_k, cache_na_v, cache_swa_k, cache_swa_v,
           state_lru, c, c_ctx, norm_g, w_ada, b_ada, w_in, diff_lambda, diff_norm_g, conv_w, conv_b,
           lru_wa, lru_ba, lru_wx, lru_bx, lru_lam, na_rpb, swa_sink, w_mg, b_mg, w_bo, w_o, norm_f):
    tab_a_np, _ = _rope_tables(A_QK)
    tab_d_np, _ = _rope_tables(HD)
    tab_a, tab_d = jnp.asarray(tab_a_np), jnp.asarray(tab_d_np)

    cvecs = jnp.concatenate([c_ctx[None, :], c, jnp.zeros((16 - 1 - DEC_BATCH, D_MODEL), F32)], axis=0)
    mod3 = _modulation(cvecs, w_ada, b_ada).reshape(DEPTH * 16, 1, 3 * D_MODEL)
    bias = _na_bias(na_rpb)
    norm_g3 = norm_g.reshape(DEPTH, 1, D_MODEL)
    dg3 = diff_norm_g.reshape(DEPTH, 1, HD)
    dg4 = jnp.tile(diff_norm_g, (1, A_HEADS)).reshape(DEPTH, 1, BR_W)
    sink3 = swa_sink.reshape(DEPTH, 1, D_HEADS)
    conv_b3 = conv_b.reshape(DEPTH, 1, BR_W)
    past = [jnp.swapaxes(t, -1, -2) for t in
            (cache_diff_k, cache_diff_v, cache_na_k, cache_na_v, cache_swa_k, cache_swa_v)]

    xa = x_prompt.reshape(N_CTX, D_MODEL)
    xb = x_sample.reshape(N_LAT, D_MODEL)
    xb_off = 0
    caches = states = None
    y_p = y_s = None
    for l in range(DEPTH):
        h, proj, bx, caches = _inproj(xa, xb, xb_off, mod3, l, norm_g3, w_in, tab_a, tab_d, caches)

        lru_params = (conv_w, conv_b3, lru_wa, lru_ba, lru_wx, lru_bx, lru_lam)
        z_ctx, states = _ctx_attn(proj, bx, l, diff_lambda, dg4, sink3, lru_params, states)
        za_lat = _lat_diff(proj, past[0], past[1], l, diff_lambda, dg3)
        zc0, y_f = _lat_na(0, proj, bx, past[2], past[3], bias, l, lru_params, state_lru)
        zc1, y_b = _lat_na(1, proj, bx, past[2], past[3], bias, l, lru_params, state_lru)
        zd_lat = _lat_swa(proj, past[4], past[5], sink3, l)

        final = l == DEPTH - 1
        out = _merge(final, xa, xb, xb_off, mod3, l, h, z_ctx, za_lat, y_f, y_b, (zc0, zc1), zd_lat,
                     w_mg, b_mg, w_bo, w_o, norm_f)
        if final:
            y_p, y_s = out
        else:
            xa = xb = out
            xb_off = N_CTX_TILES

    new_caches = [jnp.swapaxes(t, -1, -2) for t in caches]
    return (y_p.reshape(BATCH, SEQ, D_MODEL), y_s.reshape(DEC_BATCH, DEC_SEQ, D_MODEL), *new_caches, states)
```

```python
import functools
import math

import numpy as np
import jax
import jax.numpy as jnp
from jax import lax
from jax.experimental import pallas as pl
from jax.experimental.pallas import tpu as pltpu

F32 = jnp.float32
BF16 = jnp.bfloat16

D_MODEL = 1024
BATCH = 16
SEQ = 256
DEPTH = 2
DEC_BATCH = 8
DEC_SEQ = 1024
PAST_LEN = 512
GRID_W = 64
N_BRANCH = 4
BR_W = D_MODEL // 4
HD = 64
A_HEADS = BR_W // HD
A_QK = HD // 2
B_BLOCKS = 4
B_BLK = BR_W // B_BLOCKS
CONV_W = 4
LRU_C = 8.0
C_HEADS = BR_W // HD
NA_ROWS = 8
NA_COLS = 16
D_HEADS = BR_W // HD
D_KV = 2
D_GROUP = D_HEADS // D_KV
WIN = 128
ROPE_BASE = 10000.0
EPS = 1e-6
NEG = -1e30

LANES = 128
SUBLANES = 8
V7X_VMEM_BYTES = 64 * 1024 * 1024

N_CTX = BATCH * SEQ
N_LAT = DEC_BATCH * DEC_SEQ
N_TOK = N_CTX + N_LAT
GRID_ROWS = DEC_SEQ // GRID_W

TM = 512
N_CTX_TILES = N_CTX // TM
N_TILES = N_TOK // TM
LAT_TILES_PER_SEQ = DEC_SEQ // TM

COL_G = 0
GATE_A, GATE_R, GATE_C, GATE_D = (COL_G // BR_W + k for k in range(N_BRANCH))
COL_AQ = COL_G + N_BRANCH * BR_W
COL_AK = COL_AQ + BR_W
COL_AV = COL_AK + BR_W
COL_CQ = COL_AV + BR_W
COL_CK = COL_CQ + BR_W
COL_CV = COL_CK + BR_W
COL_DQ = COL_CV + BR_W
COL_DK = COL_DQ + BR_W
COL_DV = COL_DK + BR_W
PROJ_W = COL_DV + BR_W
IN_W = (N_BRANCH + 3 + 1 + 3 + 1) * BR_W + 2 * D_KV * HD
WS_W = IN_W + 2 * D_KV * HD

assert GRID_ROWS == 2 * NA_ROWS
NA_HALF_Q = DEC_SEQ // 2
NA_KROWS = NA_ROWS + NA_ROWS // 2
NA_KSPAN = NA_KROWS * GRID_W
NA_HALF_KSTART = (GRID_ROWS - NA_KROWS) * GRID_W

SWA_QB = 256
SWA_SPAN = SWA_QB + 2 * WIN

LOG2E = math.log2(math.e)
QA_SCALE = A_QK ** -0.5 * LOG2E
QH_SCALE = HD ** -0.5 * LOG2E

VMEM_LIMIT = V7X_VMEM_BYTES - 8 * 1024 * 1024

MOD_ROWS = -(-(1 + DEC_BATCH) // SUBLANES) * SUBLANES
MOD_COLS = 6 * LANES


def _cparams(sem):
    return pltpu.CompilerParams(dimension_semantics=sem, vmem_limit_bytes=VMEM_LIMIT)


def _dot(a, b):
    return jnp.dot(a, b, preferred_element_type=F32)


def _dot_nt(a, b):
    return lax.dot_general(a, b, (((1,), (1,)), ((), ())), preferred_element_type=F32)


def _silu(x):
    return x * jax.nn.sigmoid(x)


def _rope_tables(d):
    half = d // 2
    quarter = half // 2
    lane = np.arange(LANES)
    q = lane % d
    use_col = (q >= half)
    i = (q % half) % quarter
    first = (q % half) < quarter
    inv = (ROPE_BASE ** (-np.arange(quarter, dtype=np.float32) / np.float32(quarter))).astype(np.float32)
    t = np.arange(DEC_SEQ)
    pos = np.where(use_col[None, :], (t % GRID_W)[:, None], (t // GRID_W)[:, None]).astype(np.float32)
    ang = (pos * inv[i][None, :]).astype(np.float32)
    cos, sin = np.cos(ang), np.sin(ang)
    c = np.concatenate([np.ones((TM, LANES), np.float32), cos], axis=0)
    s1 = np.concatenate([np.zeros((TM, LANES), np.float32), np.where(first[None, :], -sin, 0.0)], axis=0)
    s2 = np.concatenate([np.zeros((TM, LANES), np.float32), np.where(first[None, :], 0.0, sin)], axis=0)
    return np.stack([c, s1, s2]).astype(np.float32)


def _rope_lanes(x, tab_ref, shift):
    return (x * tab_ref[0] + pltpu.roll(x, LANES - shift, 1) * tab_ref[1]
            + pltpu.roll(x, shift, 1) * tab_ref[2])


def _rope(x, tab_ref, shift):
    w = x.shape[1]
    return jnp.concatenate([_rope_lanes(x[:, c:c + LANES], tab_ref, shift) for c in range(0, w, LANES)], axis=1)


def _mod_kernel(c_ref, w_ref, b_ref, o_ref):
    c = c_ref[...]
    o_ref[0] = _dot(_silu(c).astype(BF16), w_ref[0].astype(BF16)) + b_ref[0]


def _modulation(cvecs, w_ada, b_ada):
    nb = 3 * D_MODEL // MOD_COLS
    return pl.pallas_call(
        _mod_kernel,
        grid=(DEPTH, nb),
        in_specs=[pl.BlockSpec((MOD_ROWS, D_MODEL), lambda l, j: (0, 0)),
                  pl.BlockSpec((1, D_MODEL, MOD_COLS), lambda l, j: (l, 0, j)),
                  pl.BlockSpec((1, 1, MOD_COLS), lambda l, j: (l, 0, j))],
        out_specs=pl.BlockSpec((1, MOD_ROWS, MOD_COLS), lambda l, j: (l, 0, j)),
        out_shape=jax.ShapeDtypeStruct((DEPTH, MOD_ROWS, 3 * D_MODEL), F32),
        compiler_params=_cparams(("arbitrary", "arbitrary")),
        name="modulation",
    )(cvecs, w_ada, b_ada.reshape(DEPTH, 1, 3 * D_MODEL))


W_STEPS = 8
W_ROWS = D_MODEL // W_STEPS


def _tile(i):
    return jnp.maximum(i - W_STEPS, 0)


def _wchunk(i):
    return jnp.minimum(i, W_STEPS - 1)


def _mod_row(t):
    return jnp.where(t < N_CTX_TILES, 0, 1 + (t - N_CTX_TILES) // LAT_TILES_PER_SEQ)


def _rope_blk(t):
    return jnp.where(t < N_CTX_TILES, 0, 1 + (t - N_CTX_TILES) % LAT_TILES_PER_SEQ)


def _load_x(t, xa_ref, xb_ref):
    return jnp.where(t < N_CTX_TILES, xa_ref[...], xb_ref[...])


def _cast_rows(i, src_ref, dst_ref):
    r0 = pl.multiple_of(i * W_ROWS, W_ROWS)
    dst_ref[pl.ds(r0, W_ROWS), :] = src_ref[0].astype(BF16)


WI_G = 0
WI_AQ = WI_G + N_BRANCH * BR_W
WI_AK, WI_AV, WI_BX, WI_CQ, WI_CKV = (WI_AQ + k * BR_W for k in range(1, 6))
WI_DQ = WI_CKV + 2 * BR_W
WI_DKV = WI_DQ + BR_W
SEQ_PER_TILE = TM // SEQ


def _store_heads(ref, val, n_heads, width):
    vt = val.T
    for s in range(SEQ_PER_TILE):
        for h in range(n_heads):
            ref[s, 0, h] = vt[h * width:(h + 1) * width, s * SEQ:(s + 1) * SEQ]


def _inproj_kernel(n_alias, xa_ref, xb_ref, mod_ref, g_ref, w_ref, ta_ref, td_ref, *refs):
    h_ref, proj_ref, bx_ref, dk_ref, dv_ref, nk_ref, nv_ref, sk_ref, sv_ref, w_s = refs[n_alias:]
    i = pl.program_id(0)

    @pl.when(i < W_STEPS)
    def _():
        r0 = pl.multiple_of(i * W_ROWS, W_ROWS)
        wb = w_ref[0].astype(BF16)
        w_s[pl.ds(r0, W_ROWS), 0:WI_DKV] = wb[:, 0:WI_DKV]
        for n, src in enumerate(s for s in range(2 * D_KV) for _ in range(D_GROUP)):
            w_s[pl.ds(r0, W_ROWS), WI_DKV + n * HD:WI_DKV + (n + 1) * HD] = \
                wb[:, WI_DKV + src * HD:WI_DKV + (src + 1) * HD]

    @pl.when(i >= W_STEPS)
    def _():
        t = i - W_STEPS
        is_ctx = t < N_CTX_TILES
        x = _load_x(t, xa_ref, xb_ref)
        mod = mod_ref[0]
        shift, scale = mod[:, :D_MODEL], mod[:, D_MODEL:2 * D_MODEL]
        xn = x * lax.rsqrt(jnp.mean(x * x, axis=-1, keepdims=True) + EPS) * g_ref[0]
        hb = (xn * (1.0 + scale) + shift).astype(BF16)
        h_ref[...] = hb

        def mm(c0, width):
            return _dot(hb, w_s[:, c0:c0 + width])

        w2 = 2 * BR_W
        gs = [mm(WI_G + c, w2) for c in range(0, N_BRANCH * BR_W, w2)]
        aq, ak, av = mm(WI_AQ, BR_W), mm(WI_AK, BR_W), mm(WI_AV, BR_W)
        cq, ckv = mm(WI_CQ, BR_W), mm(WI_CKV, w2)
        dq, dk4, dv4 = mm(WI_DQ, BR_W), mm(WI_DKV, BR_W), mm(WI_DKV + BR_W, BR_W)
        bx_ref[...] = mm(WI_BX, BR_W)

        for n, g in enumerate(gs):
            proj_ref[:, COL_G + n * w2:COL_G + (n + 1) * w2] = _silu(g).astype(BF16)
        proj_ref[:, COL_AQ:COL_AQ + BR_W] = _rope(aq * QA_SCALE, ta_ref, A_QK // 4).astype(BF16)
        proj_ref[:, COL_AK:COL_AK + BR_W] = _rope(ak, ta_ref, A_QK // 4).astype(BF16)
        proj_ref[:, COL_AV:COL_AV + BR_W] = av.astype(BF16)
        proj_ref[:, COL_CQ:COL_CQ + BR_W] = (cq * QH_SCALE).astype(BF16)
        proj_ref[:, COL_CK:COL_CK + w2] = ckv.astype(BF16)
        proj_ref[:, COL_DQ:COL_DQ + BR_W] = _rope(dq * QH_SCALE, td_ref, HD // 4).astype(BF16)
        proj_ref[:, COL_DK:COL_DK + BR_W] = _rope(dk4, td_ref, HD // 4).astype(BF16)
        proj_ref[:, COL_DV:COL_DV + BR_W] = dv4.astype(BF16)

        @pl.when(is_ctx)
        def _():
            akt = ak.T
            for s in range(SEQ_PER_TILE):
                for h in range(A_HEADS):
                    for m in range(2):
                        c0 = h * HD + m * A_QK
                        dk_ref[s, 0, h, m] = akt[c0:c0 + A_QK, s * SEQ:(s + 1) * SEQ]
            _store_heads(dv_ref, av, A_HEADS, HD)
            _store_heads(nk_ref, ckv[:, :BR_W], C_HEADS, HD)
            _store_heads(nv_ref, ckv[:, BR_W:], C_HEADS, HD)
            for ref, val in ((sk_ref, dk4), (sv_ref, dv4)):
                vt = val.T
                for s in range(SEQ_PER_TILE):
                    for kv in range(D_KV):
                        ref[s, 0, kv] = vt[kv * D_GROUP * HD:kv * D_GROUP * HD + HD, s * SEQ:(s + 1) * SEQ]


def _cache_shapes():
    hs = lambda n: (BATCH, DEPTH, n, HD, SEQ)
    return [(BATCH, DEPTH, A_HEADS, 2, A_QK, SEQ), hs(A_HEADS), hs(C_HEADS), hs(C_HEADS), hs(D_KV), hs(D_KV)]


def _inproj(xa, xb, xb_off, mod3, l, norm_g, w_in, tab_a, tab_d, caches):
    row = lambda i: (_tile(i), 0)
    ctx_blk = lambda i: jnp.minimum(_tile(i), N_CTX_TILES - 1)

    def cache_spec(shape):
        blk = (SEQ_PER_TILE, 1) + shape[2:]
        nz = len(shape) - 2
        return pl.BlockSpec(blk, lambda i: (ctx_blk(i), l) + (0,) * nz)

    cshapes = _cache_shapes()
    aliases = {} if caches is None else {7 + k: 3 + k for k in range(6)}
    extra_specs = [] if caches is None else [pl.BlockSpec(memory_space=pl.ANY)] * 6
    extra_args = [] if caches is None else list(caches)
    outs = pl.pallas_call(
        functools.partial(_inproj_kernel, len(extra_args)),
        grid=(W_STEPS + N_TILES,),
        in_specs=[
            pl.BlockSpec((TM, D_MODEL), lambda i: (ctx_blk(i), 0)),
            pl.BlockSpec((TM, D_MODEL), lambda i: (jnp.maximum(_tile(i) - N_CTX_TILES, 0) + xb_off, 0)),
            pl.BlockSpec((1, 1, 3 * D_MODEL), lambda i: (l * MOD_ROWS + _mod_row(_tile(i)), 0, 0)),
            pl.BlockSpec((1, 1, D_MODEL), lambda i: (l, 0, 0)),
            pl.BlockSpec((1, W_ROWS, IN_W), lambda i: (l, _wchunk(i), 0)),
            pl.BlockSpec((3, TM, LANES), lambda i: (0, _rope_blk(_tile(i)), 0)),
            pl.BlockSpec((3, TM, LANES), lambda i: (0, _rope_blk(_tile(i)), 0)),
        ] + extra_specs,
        out_specs=[
            pl.BlockSpec((TM, D_MODEL), row),
            pl.BlockSpec((TM, PROJ_W), row),
            pl.BlockSpec((TM, BR_W), row),
        ] + [cache_spec(s) for s in cshapes],
        out_shape=[
            jax.ShapeDtypeStruct((N_TOK, D_MODEL), BF16),
            jax.ShapeDtypeStruct((N_TOK, PROJ_W), BF16),
            jax.ShapeDtypeStruct((N_TOK, BR_W), F32),
        ] + [jax.ShapeDtypeStruct(s, F32) for s in cshapes],
        scratch_shapes=[pltpu.VMEM((D_MODEL, WS_W), BF16)],
        input_output_aliases=aliases,
        compiler_params=_cparams(("arbitrary",)),
        name=f"inproj_l{l}",
    )(xa, xb, mod3, norm_g, w_in, tab_a, tab_d, *extra_args)
    return outs[0], outs[1], outs[2], tuple(outs[3:])


def _diff_lambda(lam_ref, lam_init):
    lv = lam_ref[0]
    s1 = jnp.sum(lv[0:1] * lv[1:2], axis=-1, keepdims=True)
    s2 = jnp.sum(lv[2:3] * lv[3:4], axis=-1, keepdims=True)
    return jnp.exp(s1) - jnp.exp(s2) + lam_init


def _diff_norm(o, dg_ref, lam_init):
    y = o * lax.rsqrt(jnp.mean(o * o, axis=-1, keepdims=True) + EPS) * dg_ref[0]
    return y * (1.0 - lam_init)


def _exp2_parts(parts, extra=None):
    m = functools.reduce(jnp.maximum, [jnp.max(s, axis=-1, keepdims=True) for s in parts])
    if extra is not None:
        m = jnp.maximum(m, extra)
    return [jnp.exp2(s - m).astype(BF16) for s in parts], m


def _pipelined(n, score_fn, pv_fn):
    outs = []
    nxt = score_fn(0)
    for c in range(n):
        cur, nxt = nxt, (score_fn(c + 1) if c + 1 < n else None)
        outs.append(pv_fn(c, cur))
    return outs


def _with_ones(v):
    return jnp.concatenate([v, jnp.ones_like(v)], axis=1)


def _with_ones_t(vt):
    return jnp.concatenate([vt, jnp.ones((2 * SUBLANES, vt.shape[1]), vt.dtype)], axis=0)


def _sink_col(sink2, kv, rows):
    r = lax.broadcasted_iota(jnp.int32, (D_GROUP * rows, 1), 0)
    return jnp.where(r < rows, sink2[:, kv * D_GROUP:kv * D_GROUP + 1], sink2[:, kv * D_GROUP + 1:kv * D_GROUP + 2])


CTX_PER_STEP = 2


def _lane_groups(n, rows):
    grp = lax.broadcasted_iota(jnp.int32, (rows, BR_W), 1) // (BR_W // n)
    return [grp == p for p in range(n)]


def _only(x, m):
    return jnp.where(m, x, jnp.zeros_like(x))


def _ones_outside(v, m):
    return jnp.where(m, v, jnp.ones_like(v))


def _swap_halves(p):
    return jnp.concatenate([p[:, BR_W // 2:], p[:, :BR_W // 2]], axis=1)


def _pick_heads(os):
    grp = lax.broadcasted_iota(jnp.int32, os[0].shape, 1) // HD
    acc = os[0]
    for h in range(1, len(os)):
        acc = jnp.where(grp == h, os[h], acc)
    return acc


def _head_rms(o, dg4_ref, lam_init):
    r = lax.broadcasted_iota(jnp.int32, (BR_W, BR_W), 0) // HD
    c = lax.broadcasted_iota(jnp.int32, (BR_W, BR_W), 1) // HD
    ss = _dot((o * o).astype(BF16), jnp.where(r == c, 1.0, 0.0).astype(BF16))
    return o * lax.rsqrt(ss * (1.0 / HD) + EPS) * dg4_ref[0] * (1.0 - lam_init)


def _ctx_attn_kernel(n_alias, lam_init, aq_ref, ak_ref, av_ref, cq_ref, ck_ref, cv_ref, dq_ref, dk_ref, dv_ref,
                     ga_ref, gc_ref, gd_ref, lam_ref, dg4_ref, sink_ref,
                     bx_ref, gr_ref, cw_ref, cb_ref, wa_ref, ba_ref, wx_ref, bxg_ref, llam_ref, *refs):
    z_ref, fin_ref = refs[n_alias:]
    seqs = [slice(s * SEQ, (s + 1) * SEQ) for s in range(CTX_PER_STEP)]

    xcs = [_lru_conv(bx_ref[rs, :], cw_ref[0], cb_ref[0]) for rs in seqs]
    xc_all = jnp.concatenate(xcs, axis=0)
    pres = [_lru_gates(d, xc_all, wa_ref, ba_ref, wx_ref, bxg_ref) for d in range(2)]
    for s, rs in enumerate(seqs):
        h0 = jnp.zeros((1, BR_W), F32)
        yf, hf = _lru_scan(0, xcs[s], [p[rs] for p in pres[0]], llam_ref[0, 0:1, :], h0)
        yb, hb = _lru_scan(1, xcs[s], [p[rs] for p in pres[1]], llam_ref[0, 1:2, :], h0)
        z_ref[rs, BR_W:2 * BR_W] = ((yf + yb) * gr_ref[rs, :].astype(F32)).astype(BF16)
        fin_ref[s, 0, 0:1, :] = hf
        fin_ref[s, 0, 1:2, :] = hb

    lam = _diff_lambda(lam_ref, lam_init)
    sink2 = sink_ref[0] * LOG2E
    m_qk = _lane_groups(2 * A_HEADS, SEQ)
    m_hd = _lane_groups(C_HEADS, SEQ)

    sa, sc, sd = [], [], []
    for rs in seqs:
        aq, cq, dq = aq_ref[rs, :], cq_ref[rs, :], dq_ref[rs, :]
        ak, ck, dk = ak_ref[rs, :], ck_ref[rs, :], dk_ref[rs, :]
        sa.append([_dot_nt(_only(aq, m), ak) for m in m_qk])
        sc.append([_dot_nt(_only(cq, m), ck) for m in m_hd])
        sd.append([_dot_nt(_only(dq, m), dk) for m in m_hd])

    ea =[[_exp2_parts([s])[0][0] for s in ss] for ss in sa]
    ec = [[_exp2_parts([s])[0][0] for s in ss] for ss in sc]
    ed = [[_exp2_parts([s], extra=sink2[:, j:j + 1]) for j, s in enumerate(ss)] for ss in sd]

    for i, rs in enumerate(seqs):
        av, cv, dv = av_ref[rs, :], cv_ref[rs, :], dv_ref[rs, :]
        os = []
        for h in range(A_HEADS):
            w = _ones_outside(av, m_hd[h])
            p0, p1 = _dot(ea[i][2 * h], w), _dot(ea[i][2 * h + 1], w)
            os.append(p0 * (1.0 / _swap_halves(p0)) - p1 * (lam / _swap_halves(p1)))
        z_ref[rs, 0:BR_W] = (_head_rms(_pick_heads(os), dg4_ref, lam_init) * ga_ref[rs, :].astype(F32)).astype(BF16)

        os = []
        for h in range(C_HEADS):
            p = _dot(ec[i][h], _ones_outside(cv, m_hd[h]))
            os.append(p * (1.0 / _swap_halves(p)))
        z_ref[rs, 2 * BR_W:3 * BR_W] = (_pick_heads(os) * gc_ref[rs, :].astype(F32)).astype(BF16)

        os = []
        for j in range(D_HEADS):
            (e,), m = ed[i][j]
            p = _dot(e, _ones_outside(dv, m_hd[j]))
            os.append(p * (1.0 / (_swap_halves(p) + jnp.exp2(sink2[:, j:j + 1] - m))))
        z_ref[rs, 3 * BR_W:4 * BR_W] = (_pick_heads(os) * gd_ref[rs, :].astype(F32)).astype(BF16)


def _ctx_attn(proj, bx, l, diff_lambda, dg4, sink3, lru_params, states):
    lam_init = 0.8 - 0.6 * math.exp(-0.3 * l)
    colblk = lambda cb: pl.BlockSpec((CTX_PER_STEP * SEQ, BR_W), lambda b: (b, cb))
    extra = [] if states is None else [states]
    return pl.pallas_call(
        functools.partial(_ctx_attn_kernel, len(extra), lam_init),
        grid=(BATCH // CTX_PER_STEP,),
        in_specs=[colblk(COL_AQ // BR_W), colblk(COL_AK // BR_W), colblk(COL_AV // BR_W),
                  colblk(COL_CQ // BR_W), colblk(COL_CK // BR_W), colblk(COL_CV // BR_W),
                  colblk(COL_DQ // BR_W), colblk(COL_DK // BR_W), colblk(COL_DV // BR_W),
                  colblk(GATE_A), colblk(GATE_C), colblk(GATE_D),
                  pl.BlockSpec((1, 4, A_QK), lambda b: (l, 0, 0)),
                  pl.BlockSpec((1, 1, BR_W), lambda b: (l, 0, 0)),
                  pl.BlockSpec((1, 1, D_HEADS), lambda b: (l, 0, 0)),
                  pl.BlockSpec((CTX_PER_STEP * SEQ, BR_W), lambda b: (b, 0)), colblk(GATE_R)]
                 + _lru_specs(l, lambda b: (l, 0, 0)) + [pl.BlockSpec(memory_space=pl.ANY)] * len(extra),
        out_specs=[pl.BlockSpec((CTX_PER_STEP * SEQ, N_BRANCH * BR_W), lambda b: (b, 0)),
                   pl.BlockSpec((CTX_PER_STEP, 1, 2, BR_W), lambda b: (b, l, 0, 0))],
        out_shape=[jax.ShapeDtypeStruct((N_CTX, N_BRANCH * BR_W), BF16),
                   jax.ShapeDtypeStruct((BATCH, DEPTH, 2, BR_W), F32)],
        input_output_aliases={} if states is None else {24: 1},
        compiler_params=_cparams(("arbitrary",)),
        name=f"ctx_attn_l{l}",
    )(*([proj] * 12), diff_lambda, dg4, sink3, bx, proj, *lru_params, *extra)


LA_QB = 256


def _lat_diff_kernel(lam_init, aq_ref, ak_ref, av_ref, ck_ref, cv_ref, ga_ref, lam_ref, dg_ref, za_ref):
    lam = _diff_lambda(lam_ref, lam_init)
    aq = aq_ref[...]
    va_ctx = [_with_ones_t(cv_ref[0, 0, h].astype(BF16)) for h in range(A_HEADS)]
    va_loc = [_with_ones(av_ref[:, h * HD:(h + 1) * HD]) for h in range(A_HEADS)]

    def scores(c):
        h, m = divmod(c, 2)
        lo = h * HD + m * A_QK
        q = aq[:, lo:lo + A_QK]
        return [_dot(q, ck_ref[0, 0, h, m].astype(BF16)),
                _dot_nt(q, ak_ref[:, lo:lo + A_QK])]

    def pv(c, s):
        (e_ctx, e_loc), _ = _exp2_parts(s)
        p_ctx, p_loc = _dot_nt(e_ctx, va_ctx[c // 2]), _dot(e_loc, va_loc[c // 2])
        return p_ctx[:, :HD] + p_loc[:, :HD], p_ctx[:, HD:HD + 1] + p_loc[:, HD:HD + 1]

    ss = [scores(c) for c in range(2 * A_HEADS)]
    nd = [pv(c, s) for c, s in enumerate(ss)]
    outs = []
    for h in range(A_HEADS):
        (n0, d0), (n1, d1) = nd[2 * h], nd[2 * h + 1]
        outs.append(_diff_norm(n0 * (1.0 / d0) - n1 * (lam / d1), dg_ref, lam_init))
    za_ref[...] = (jnp.concatenate(outs, axis=1) * ga_ref[...].astype(F32)).astype(BF16)


def _lat_diff(proj, cache_k, cache_v, l, diff_lambda, dg3):
    lam_init = 0.8 - 0.6 * math.exp(-0.3 * l)
    nq = DEC_SEQ // LA_QB
    qrow = lambda b, j: N_CTX // LA_QB + b * nq + j
    srow = lambda b, j: N_CTX // DEC_SEQ + b
    return pl.pallas_call(
        functools.partial(_lat_diff_kernel, lam_init),
        grid=(DEC_BATCH, nq),
        in_specs=[pl.BlockSpec((LA_QB, BR_W), lambda b, j: (qrow(b, j), COL_AQ // BR_W)),
                  pl.BlockSpec((DEC_SEQ, BR_W), lambda b, j: (srow(b, j), COL_AK // BR_W)),
                  pl.BlockSpec((DEC_SEQ, BR_W), lambda b, j: (srow(b, j), COL_AV // BR_W)),
                  pl.BlockSpec((1, 1, A_HEADS, 2, A_QK, PAST_LEN), lambda b, j: (b, l, 0, 0, 0, 0)),
                  pl.BlockSpec((1, 1, A_HEADS, HD, PAST_LEN), lambda b, j: (b, l, 0, 0, 0)),
                  pl.BlockSpec((LA_QB, BR_W), lambda b, j: (qrow(b, j), 0)),
                  pl.BlockSpec((1, 4, A_QK), lambda b, j: (l, 0, 0)),
                  pl.BlockSpec((1, 1, HD), lambda b, j: (l, 0, 0))],
        out_specs=pl.BlockSpec((LA_QB, BR_W), lambda b, j: (b * nq + j, 0)),
        out_shape=jax.ShapeDtypeStruct((N_LAT, BR_W), BF16),
        compiler_params=_cparams(("arbitrary", "arbitrary")),
        name=f"lat_diff_l{l}",
    )(proj, proj, proj, cache_k, cache_v, proj, diff_lambda, dg3)


def _na_bias_kernel(rpb_ref, o_ref):
    l, h = pl.program_id(0), pl.program_id(1)
    base = (l * C_HEADS + h) * ((2 * NA_ROWS - 1) * (2 * NA_COLS - 1))
    qc = lax.broadcasted_iota(jnp.int32, (GRID_W, GRID_W), 0)
    kc = lax.broadcasted_iota(jnp.int32, (GRID_W, GRID_W), 1)
    cs = jnp.clip(qc - NA_COLS // 2, 0, GRID_W - NA_COLS)
    ok = (kc >= cs) & (kc < cs + NA_COLS)
    dcol = kc - qc + (NA_COLS - 1)
    neg = jnp.full((GRID_W, GRID_W), NEG, F32)
    tabs = []
    for d in range(2 * NA_ROWS - 1):
        t = neg
        for dc in range(2 * NA_COLS - 1):
            t = jnp.where(ok & (dcol == dc), rpb_ref[base + d * (2 * NA_COLS - 1) + dc] * LOG2E, t)
        tabs.append(t)
    for half in range(2):
        for rq in range(NA_ROWS):
            r = half * NA_ROWS + rq
            rs = min(max(r - NA_ROWS // 2, 0), GRID_ROWS - NA_ROWS)
            blks = []
            for kr in range(NA_KROWS):
                kabs = half * (NA_HALF_KSTART // GRID_W) + kr
                blks.append(tabs[kabs - r + NA_ROWS - 1] if rs <= kabs < rs + NA_ROWS else neg)
            o_ref[0, 0, half, rq * GRID_W:(rq + 1) * GRID_W, :] = jnp.concatenate(blks, axis=1)


def _na_bias(na_rpb):
    return pl.pallas_call(
        _na_bias_kernel,
        grid=(DEPTH, C_HEADS),
        in_specs=[pl.BlockSpec(memory_space=pltpu.SMEM)],
        out_specs=pl.BlockSpec((1, 1, 2, NA_HALF_Q, NA_KSPAN), lambda l, h: (l, h, 0, 0, 0)),
        out_shape=jax.ShapeDtypeStruct((DEPTH, C_HEADS, 2, NA_HALF_Q, NA_KSPAN), F32),
        compiler_params=_cparams(("arbitrary", "arbitrary")),
        name="na_bias",
    )(na_rpb.reshape(-1))


def _lat_na_kernel(half, cq_ref, ck_ref, cv_ref, kc_ref, vc_ref, bias_ref, gc_ref,
                   bx_ref, gr_ref, cw_ref, cb_ref, wa_ref, ba_ref, wx_ref, bxg_ref, llam_ref, st_ref, zc_ref, y_ref):
    k0 = half * NA_HALF_KSTART
    cq = cq_ref[...]
    kl_all = ck_ref[k0:k0 + NA_KSPAN, :]
    vl_all = cv_ref[k0:k0 + NA_KSPAN, :]

    def scores(h):
        q = cq[:, h * HD:(h + 1) * HD]
        return [_dot(q, kc_ref[0, 0, h].astype(BF16)),
                _dot_nt(q, kl_all[:, h * HD:(h + 1) * HD]) + bias_ref[0, h, 0]]

    def pv(h, s):
        (e_ctx, e_loc), _ = _exp2_parts(s)
        p_ctx = _dot_nt(e_ctx, _with_ones_t(vc_ref[0, 0, h].astype(BF16)))
        p_loc = _dot(e_loc, _with_ones(vl_all[:, h * HD:(h + 1) * HD]))
        return (p_ctx[:, :HD] + p_loc[:, :HD]) * (1.0 / (p_ctx[:, HD:HD + 1] + p_loc[:, HD:HD + 1]))

    xc = _lru_conv(bx_ref[...], cw_ref[0], cb_ref[0])
    pre = _lru_gates(half, xc, wa_ref, ba_ref, wx_ref, bxg_ref)
    y, _ = _lru_scan(half, xc, pre, llam_ref[0, half:half + 1, :], st_ref[0, 0, half:half + 1, :])
    y_ref[...] = y * gr_ref[...].astype(F32)

    ss = [scores(h) for h in range(C_HEADS)]
    outs = [pv(h, s) for h, s in enumerate(ss)]
    zc_ref[...] = (jnp.concatenate(outs, axis=1) * gc_ref[...].astype(F32)).astype(BF16)


def _lat_na(half, proj, bx, cache_k, cache_v, bias, l, lru_params, state_lru):
    qrow = lambda b: N_CTX // NA_HALF_Q + b * 2 + half
    srow = lambda b: N_CTX // DEC_SEQ + b
    return pl.pallas_call(
        functools.partial(_lat_na_kernel, half),
        grid=(DEC_BATCH,),
        in_specs=[pl.BlockSpec((NA_HALF_Q, BR_W), lambda b: (qrow(b), COL_CQ // BR_W)),
                  pl.BlockSpec((DEC_SEQ, BR_W), lambda b: (srow(b), COL_CK // BR_W)),
                  pl.BlockSpec((DEC_SEQ, BR_W), lambda b: (srow(b), COL_CV // BR_W)),
                  pl.BlockSpec((1, 1, C_HEADS, HD, PAST_LEN), lambda b: (b, l, 0, 0, 0)),
                  pl.BlockSpec((1, 1, C_HEADS, HD, PAST_LEN), lambda b: (b, l, 0, 0, 0)),
                  pl.BlockSpec((1, C_HEADS, 1, NA_HALF_Q, NA_KSPAN), lambda b: (l, 0, half, 0, 0)),
                  pl.BlockSpec((NA_HALF_Q, BR_W), lambda b: (qrow(b), GATE_C)),
                  pl.BlockSpec((DEC_SEQ, BR_W), lambda b: (srow(b), 0)),
                  pl.BlockSpec((DEC_SEQ, BR_W), lambda b: (srow(b), GATE_R))]
                 + _lru_specs(l, lambda b: (l, 0, 0))
                 + [pl.BlockSpec((1, 1, 2, BR_W), lambda b: (b, l, 0, 0))],
        out_specs=[pl.BlockSpec((NA_HALF_Q, BR_W), lambda b: (b, 0)),
                   pl.BlockSpec((DEC_SEQ, BR_W), lambda b: (b, 0))],
        out_shape=[jax.ShapeDtypeStruct((N_LAT // 2, BR_W), BF16), jax.ShapeDtypeStruct((N_LAT, BR_W), F32)],
        compiler_params=_cparams(("arbitrary",)),
        name=f"lat_na{half}_l{l}",
    )(proj, proj, proj, cache_k, cache_v, bias, proj, bx, proj, *lru_params, state_lru)


def _lat_swa_kernel(dq_ref, dk_ref, dv_ref, kc_ref, vc_ref, sink_ref, gd_ref, zd_ref):
    nq = DEC_SEQ // SWA_QB
    rows = D_GROUP * SWA_QB
    dq, dk, dv = dq_ref[...], dk_ref[...], dv_ref[...]
    sink2 = sink_ref[0] * LOG2E
    m_hd = _lane_groups(D_HEADS, DEC_SEQ)
    starts = [min(max(j * SWA_QB - WIN, 0), DEC_SEQ - SWA_SPAN) for j in range(nq)]
    qi = lax.broadcasted_iota(jnp.int32, (rows, SWA_SPAN), 0) % SWA_QB
    ki = lax.broadcasted_iota(jnp.int32, (rows, SWA_SPAN), 1)
    masks = {off: jnp.abs(qi + off - ki) <= WIN for off in sorted({j * SWA_QB - s for j, s in enumerate(starts)})}
    kc_t = jnp.concatenate([kc_ref[0, 0, kv] for kv in range(D_KV) for _ in range(D_GROUP)], axis=0).astype(BF16)
    vc_t = jnp.concatenate([vc_ref[0, 0, kv] for kv in range(D_KV) for _ in range(D_GROUP)], axis=0).astype(BF16)
    feat = lax.broadcasted_iota(jnp.int32, (BR_W, 1), 0) // (D_GROUP * HD)

    qm = [_only(dq, m) for m in m_hd]
    sks = [_sink_col(sink2, kv, SWA_QB) for kv in range(D_KV)]
    w_loc, w_ctx = [], []
    for kv in range(D_KV):
        w_loc.append(_ones_outside(dv, m_hd[kv * D_GROUP] | m_hd[kv * D_GROUP + 1]))
        w_ctx.append(jnp.where(feat == kv, vc_t, jnp.ones_like(vc_t)))

    def scores(c):
        kv, j = divmod(c, nq)
        q2 = jnp.concatenate([qm[kv * D_GROUP + g][j * SWA_QB:(j + 1) * SWA_QB] for g in range(D_GROUP)], axis=0)
        return [_dot(q2, kc_t),
                jnp.where(masks[j * SWA_QB - starts[j]], _dot_nt(q2, dk[starts[j]:starts[j] + SWA_SPAN]), NEG)]

    def pv(c, s):
        kv, j = divmod(c, nq)
        (e_ctx, e_loc), m = _exp2_parts(s, extra=sks[kv])
        p = _dot_nt(e_ctx, w_ctx[kv]) + _dot(e_loc, w_loc[kv][starts[j]:starts[j] + SWA_SPAN])
        return p * (1.0 / (_swap_halves(p) + jnp.exp2(sks[kv] - m)))

    o = _pipelined(D_KV * nq, scores, pv)
    os = [jnp.concatenate([o[kv * nq + j][g * SWA_QB:(g + 1) * SWA_QB] for j in range(nq)], axis=0)
          for kv in range(D_KV) for g in range(D_GROUP)]
    zd_ref[...] = (_pick_heads(os) * gd_ref[...].astype(F32)).astype(BF16)


def _lat_swa(proj, cache_k, cache_v, sink3, l):
    srow = lambda b: (N_CTX // DEC_SEQ + b)
    return pl.pallas_call(
        _lat_swa_kernel,
        grid=(DEC_BATCH,),
        in_specs=[pl.BlockSpec((DEC_SEQ, BR_W), lambda b: (srow(b), COL_DQ // BR_W)),
                  pl.BlockSpec((DEC_SEQ, BR_W), lambda b: (srow(b), COL_DK // BR_W)),
                  pl.BlockSpec((DEC_SEQ, BR_W), lambda b: (srow(b), COL_DV // BR_W)),
                  pl.BlockSpec((1, 1, D_KV, HD, PAST_LEN), lambda b: (b, l, 0, 0, 0)),
                  pl.BlockSpec((1, 1, D_KV, HD, PAST_LEN), lambda b: (b, l, 0, 0, 0)),
                  pl.BlockSpec((1, 1, D_HEADS), lambda b: (l, 0, 0)),
                  pl.BlockSpec((DEC_SEQ, BR_W), lambda b: (srow(b), GATE_D))],
        out_specs=pl.BlockSpec((DEC_SEQ, BR_W), lambda b: (b, 0)),
        out_shape=jax.ShapeDtypeStruct((N_LAT, BR_W), BF16),
        compiler_params=_cparams(("arbitrary",)),
        name=f"lat_swa_l{l}",
    )(proj, proj, proj, cache_k, cache_v, sink3, proj)


def _lru_conv(x, cw, cb):
    seq = x.shape[0]
    x3 = x.reshape(seq // SUBLANES, SUBLANES, BR_W)
    tin = lax.broadcasted_iota(jnp.int32, (1, SUBLANES, 1), 1)
    zero = jnp.zeros((1, SUBLANES, BR_W), F32)

    def at(shift):
        r = pltpu.roll(x3, (-shift) % SUBLANES, 1)
        if shift < 0:
            return jnp.where(tin < -shift, jnp.concatenate([zero, r[:-1]], axis=0), r)
        return jnp.where(tin >= SUBLANES - shift, jnp.concatenate([r[1:], zero], axis=0), r)

    xc = cb + at(-1) * cw[0:1] + x3 * cw[1:2] + at(1) * cw[2:3] + at(2) * cw[3:4]
    return xc.reshape(seq, BR_W)


def _gate_dense(w_ref, d):
    rows = []
    for n in range(B_BLOCKS):
        pieces = [jnp.zeros((B_BLK, n * B_BLK), BF16)] if n else []
        pieces.append(w_ref[0, d, n].astype(BF16))
        if n < B_BLOCKS - 1:
            pieces.append(jnp.zeros((B_BLK, (B_BLOCKS - 1 - n) * B_BLK), BF16))
        rows.append(jnp.concatenate(pieces, axis=1))
    return jnp.concatenate(rows, axis=0)


def _lru_gates(d, xc, wa_ref, ba_ref, wx_ref, bxg_ref):
    xb = xc.astype(BF16)
    return (_dot(xb, _gate_dense(wa_ref, d)) + ba_ref[0, d:d + 1, :],
            _dot(xb, _gate_dense(wx_ref, d)) + bxg_ref[0, d:d + 1, :])


def _lru_scan(d, xc, pre, lam_row, h0):
    seq = xc.shape[0]
    nl = -lam_row
    softplus = jnp.maximum(nl, 0.0) + jnp.log1p(jnp.exp(-jnp.abs(nl)))
    r = jax.nn.sigmoid(pre[0])
    ig = jax.nn.sigmoid(pre[1])
    log_a = -LRU_C * r * softplus
    a = jnp.exp(log_a)
    u = jnp.exp(0.5 * jnp.log(-jnp.tanh(log_a) * (a * a + 1.0))) * (ig * xc)
    nt = seq // SUBLANES
    tin = lax.broadcasted_iota(jnp.int32, (1, SUBLANES, 1), 1)
    a = a.reshape(nt, SUBLANES, BR_W)
    u = u.reshape(nt, SUBLANES, BR_W)
    s = 1
    while s < SUBLANES:
        if d == 0:
            keep = tin >= s
            a_n = jnp.where(keep, pltpu.roll(a, s, 1), 1.0)
            u_n = jnp.where(keep, pltpu.roll(u, s, 1), 0.0)
        else:
            keep = tin < SUBLANES - s
            a_n = jnp.where(keep, pltpu.roll(a, SUBLANES - s, 1), 1.0)
            u_n = jnp.where(keep, pltpu.roll(u, SUBLANES - s, 1), 0.0)
        u = u + a * u_n
        a = a * a_n
        s *= 2
    ys = [None] * nt
    h = h0
    for k in (range(nt) if d == 0 else range(nt - 1, -1, -1)):
        y = u[k] + a[k] * h
        ys[k] = y
        h = y[SUBLANES - 1:, :] if d == 0 else y[:1, :]
    return jnp.concatenate(ys, axis=0), h


def _lru_specs(l, imap):
    wspec = pl.BlockSpec((1, 2, B_BLOCKS, B_BLK, B_BLK), lambda *g: imap(*g) + (0, 0))
    vspec = pl.BlockSpec((1, 2, BR_W), imap)
    return [pl.BlockSpec((1, CONV_W, BR_W), imap), pl.BlockSpec((1, 1, BR_W), imap),
            wspec, vspec, wspec, vspec, vspec]


MERGE_COLS = 512

def _merge_kernel(layer, final, xa_ref, xb_ref, mod_ref, h_ref, zctx_ref, zal_ref, yf_ref, yb_ref,
                  zc0_ref, zc1_ref, zdl_ref, wmg_hbm, bmg_ref, wbo_ref, wo_ref, nf_ref, *refs):
    out_refs, (wmg_s, wbo_s, wo_s, stage, sem) = refs[:-5], refs[-5:]
    i = pl.program_id(0)

    def wmg_copy(chunk, n):
        slot = chunk % 2
        return pltpu.make_async_copy(wmg_hbm.at[layer, pl.ds(chunk * W_ROWS, W_ROWS), n, :],
                                     stage.at[slot, n], sem.at[slot, n])

    @pl.when(i == 0)
    def _():
        for n in range(N_BRANCH):
            wmg_copy(0, n).start()

    @pl.when(i < W_STEPS - 1)
    def _():
        for n in range(N_BRANCH):
            wmg_copy(i + 1, n).start()

    @pl.when(i < W_STEPS)
    def _():
        r0 = pl.multiple_of(i * W_ROWS, W_ROWS)
        for n in range(N_BRANCH):
            wmg_copy(i, n).wait()
            wmg_s[pl.ds(r0, W_ROWS), n * D_MODEL:(n + 1) * D_MODEL] = stage[i % 2, n].astype(BF16)
        _cast_rows(i, wbo_ref, wbo_s)
        _cast_rows(i, wo_ref, wo_s)

    @pl.when(i >= W_STEPS)
    def _():
        t = i - W_STEPS
        x = _load_x(t, xa_ref, xb_ref)
        gate = mod_ref[0][:, 2 * D_MODEL:]
        h = h_ref[...]
        is_ctx = t < N_CTX_TILES
        second_half = (t - N_CTX_TILES) % LAT_TILES_PER_SEQ == 1
        z_lat = [zal_ref[...],
                 (yf_ref[...] + yb_ref[...]).astype(BF16),
                 jnp.where(second_half, zc1_ref[...], zc0_ref[...]),
                 zdl_ref[...]]
        zs = [jnp.where(is_ctx, zctx_ref[:, n * BR_W:(n + 1) * BR_W], z_lat[n]) for n in range(N_BRANCH)]
        bmg = bmg_ref[0]
        cols = []
        for c in range(0, D_MODEL, MERGE_COLS):
            acc = None
            for n in range(N_BRANCH):
                g = jax.nn.sigmoid(_dot(h, wmg_s[:, n * D_MODEL + c:n * D_MODEL + c + MERGE_COLS])
                                   + bmg[n:n + 1, c:c + MERGE_COLS])
                term = g * _dot(zs[n], wbo_s[n * BR_W:(n + 1) * BR_W, c:c + MERGE_COLS])
                acc = term if acc is None else acc + term
            cols.append(acc.astype(BF16))
        merged = jnp.concatenate(cols, axis=1)
        xn = x + gate * _dot(merged, wo_s[...])
        if not final:
            out_refs[0][...] = xn
        else:
            y = xn * lax.rsqrt(jnp.mean(xn * xn, axis=-1, keepdims=True) + EPS) * nf_ref[...]

            @pl.when(t < N_CTX_TILES)
            def _():
                out_refs[0][...] = y

            @pl.when(t >= N_CTX_TILES)
            def _():
                out_refs[1][...] = y


def _merge(final, xa, xb, xb_off, mod3, l, h, z_ctx, za_lat, y_f, y_b, zc_lat, zd_lat,
           w_mg, b_mg, w_bo, w_o, norm_f):
    assert TM == NA_HALF_Q
    row = lambda i: (_tile(i), 0)
    ctx_blk = lambda i: jnp.minimum(_tile(i), N_CTX_TILES - 1)
    lat_blk = lambda i: jnp.maximum(_tile(i) - N_CTX_TILES, 0)
    lat_spec = pl.BlockSpec((TM, BR_W), lambda i: (lat_blk(i), 0))
    half_spec = pl.BlockSpec((TM, BR_W), lambda i: (lat_blk(i) // LAT_TILES_PER_SEQ, 0))
    if final:
        out_specs = [pl.BlockSpec((TM, D_MODEL), lambda i: (ctx_blk(i), 0)),
                     pl.BlockSpec((TM, D_MODEL), lambda i: (lat_blk(i), 0))]
        out_shape = [jax.ShapeDtypeStruct((N_CTX, D_MODEL), F32), jax.ShapeDtypeStruct((N_LAT, D_MODEL), F32)]
    else:
        out_specs = pl.BlockSpec((TM, D_MODEL), row)
        out_shape = jax.ShapeDtypeStruct((N_TOK, D_MODEL), F32)
    wchunk = lambda width: pl.BlockSpec((1, W_ROWS, width), lambda i: (l, _wchunk(i), 0))
    return pl.pallas_call(
        functools.partial(_merge_kernel, l, final),
        grid=(W_STEPS + N_TILES,),
        in_specs=[
            pl.BlockSpec((TM, D_MODEL), lambda i: (ctx_blk(i), 0)),
            pl.BlockSpec((TM, D_MODEL), lambda i: (lat_blk(i) + xb_off, 0)),
            pl.BlockSpec((1, 1, 3 * D_MODEL), lambda i: (l * MOD_ROWS + _mod_row(_tile(i)), 0, 0)),
            pl.BlockSpec((TM, D_MODEL), row),
            pl.BlockSpec((TM, N_BRANCH * BR_W), lambda i: (ctx_blk(i), 0)),
            lat_spec,
            lat_spec, lat_spec,
            half_spec, half_spec,
            lat_spec,
            pl.BlockSpec(memory_space=pl.ANY),
            pl.BlockSpec((1, N_BRANCH, D_MODEL), lambda i: (l, 0, 0)),
            wchunk(D_MODEL),
            wchunk(D_MODEL),
            pl.BlockSpec((1, D_MODEL), lambda i: (0, 0)),
        ],
        out_specs=out_specs,
        out_shape=out_shape,
        scratch_shapes=[pltpu.VMEM((D_MODEL, N_BRANCH * D_MODEL), BF16),
                        pltpu.VMEM((N_BRANCH * BR_W, D_MODEL), BF16),
                        pltpu.VMEM((D_MODEL, D_MODEL), BF16),
                        pltpu.VMEM((2, N_BRANCH, W_ROWS, D_MODEL), F32),
                        pltpu.SemaphoreType.DMA((2, N_BRANCH))],
        compiler_params=_cparams(("arbitrary",)),
        name=f"merge_l{l}",
    )(xa, xb, mod3, h, z_ctx, za_lat, y_f, y_b, zc_lat[0], zc_lat[1], zd_lat,
      w_mg, b_mg,
      w_bo.reshape(DEPTH, N_BRANCH * BR_W, D_MODEL), w_o, norm_f.reshape(1, D_MODEL))


def kernel(x_prompt, x_sample, cache_diff_k, cache_diff_v, cache_na_k, cache_na_v, cache_swa_k, cache_swa_v,
           state_lru, c, c_ctx, norm_g, w_ada, b_ada, w_in, diff_lambda, diff_norm_g, conv_w, conv_b,
           lru_wa, lru_ba, lru_wx, lru_bx, lru_lam, na_rpb, swa_sink, w_mg, b_mg, w_bo, w_o, norm_f):
    tab_a, tab_d = jnp.asarray(_rope_tables(A_QK)), jnp.asarray(_rope_tables(HD))

    cvecs = jnp.concatenate([c_ctx[None, :], c, jnp.zeros((MOD_ROWS - 1 - DEC_BATCH, D_MODEL), F32)], axis=0)
    mod3 = _modulation(cvecs, w_ada, b_ada).reshape(DEPTH * MOD_ROWS, 1, 3 * D_MODEL)
    bias = _na_bias(na_rpb)
    norm_g3 = norm_g.reshape(DEPTH, 1, D_MODEL)
    dg3 = diff_norm_g.reshape(DEPTH, 1, HD)
    dg4 = jnp.tile(diff_norm_g, (1, A_HEADS)).reshape(DEPTH, 1, BR_W)
    sink3 = swa_sink.reshape(DEPTH, 1, D_HEADS)
    conv_b3 = conv_b.reshape(DEPTH, 1, BR_W)
    past = [jnp.swapaxes(t, -1, -2) for t in
            (cache_diff_k, cache_diff_v, cache_na_k, cache_na_v, cache_swa_k, cache_swa_v)]

    xa = x_prompt.reshape(N_CTX, D_MODEL)
    xb = x_sample.reshape(N_LAT, D_MODEL)
    xb_off = 0
    caches = states = None
    y_p = y_s = None
    for l in range(DEPTH):
        h, proj, bx, caches = _inproj(xa, xb, xb_off, mod3, l, norm_g3, w_in, tab_a, tab_d, caches)

        lru_params = (conv_w, conv_b3, lru_wa, lru_ba, lru_wx, lru_bx, lru_lam)
        z_ctx, states = _ctx_attn(proj, bx, l, diff_lambda, dg4, sink3, lru_params, states)
        za_lat = _lat_diff(proj, past[0], past[1], l, diff_lambda, dg3)
        zc0, y_f = _lat_na(0, proj, bx, past[2], past[3], bias, l, lru_params, state_lru)
        zc1, y_b = _lat_na(1, proj, bx, past[2], past[3], bias, l, lru_params, state_lru)
        zd_lat = _lat_swa(proj, past[4], past[5], sink3, l)

        final = l == DEPTH - 1
        out = _merge(final, xa, xb, xb_off, mod3, l, h, z_ctx, za_lat, y_f, y_b, (zc0, zc1), zd_lat,
                     w_mg, b_mg, w_bo, w_o, norm_f)
        if final:
            y_p, y_s = out
        else:
            xa = xb = out
            xb_off = N_CTX_TILES

    new_caches = [jnp.swapaxes(t, -1, -2) for t in caches]
    return (y_p.reshape(BATCH, SEQ, D_MODEL), y_s.reshape(DEC_BATCH, DEC_SEQ, D_MODEL), *new_caches, states)
```

```python
import functools
import math

import numpy as np
import jax
import jax.numpy as jnp
from jax import lax
from jax.experimental import pallas as pl
from jax.experimental.pallas import tpu as pltpu

F32 = jnp.float32
BF16 = jnp.bfloat16

D_MODEL = 1024
BATCH = 16
SEQ = 256
DEPTH = 2
DEC_BATCH = 8
DEC_SEQ = 1024
PAST_LEN = 512
GRID_W = 64
N_BRANCH = 4
BR_W = D_MODEL // 4
HD = 64
A_HEADS = BR_W // HD
A_QK = HD // 2
B_BLOCKS = 4
B_BLK = BR_W // B_BLOCKS
CONV_W = 4
LRU_C = 8.0
C_HEADS = BR_W // HD
NA_ROWS = 8
NA_COLS = 16
D_HEADS = BR_W // HD
D_KV = 2
D_GROUP = D_HEADS // D_KV
WIN = 128
ROPE_BASE = 10000.0
EPS = 1e-6
NEG = -1e30

LANES = 128
SUBLANES = 8
V7X_VMEM_BYTES = 64 * 1024 * 1024

N_CTX = BATCH * SEQ
N_LAT = DEC_BATCH * DEC_SEQ
N_TOK = N_CTX + N_LAT
GRID_ROWS = DEC_SEQ // GRID_W

TM = 512
N_CTX_TILES = N_CTX // TM
N_TILES = N_TOK // TM
LAT_TILES_PER_SEQ = DEC_SEQ // TM

COL_G = 0
GATE_A, GATE_R, GATE_C, GATE_D = (COL_G // BR_W + k for k in range(N_BRANCH))
COL_AQ = COL_G + N_BRANCH * BR_W
COL_AK = COL_AQ + BR_W
COL_AV = COL_AK + BR_W
COL_CQ = COL_AV + BR_W
COL_CK = COL_CQ + BR_W
COL_CV = COL_CK + BR_W
COL_DQ = COL_CV + BR_W
COL_DK = COL_DQ + BR_W
COL_DV = COL_DK + BR_W
PROJ_W = COL_DV + BR_W
IN_W = (N_BRANCH + 3 + 1 + 3 + 1) * BR_W + 2 * D_KV * HD

assert GRID_ROWS == 2 * NA_ROWS
NA_HALF_Q = DEC_SEQ // 2
NA_KROWS = NA_ROWS + NA_ROWS // 2
NA_KSPAN = NA_KROWS * GRID_W
NA_HALF_KSTART = (GRID_ROWS - NA_KROWS) * GRID_W

SWA_QB = 256
SWA_SPAN = SWA_QB + 2 * WIN

LOG2E = math.log2(math.e)
QA_SCALE = A_QK ** -0.5 * LOG2E
QH_SCALE = HD ** -0.5 * LOG2E

VMEM_LIMIT = V7X_VMEM_BYTES - 8 * 1024 * 1024

MOD_ROWS = -(-(1 + DEC_BATCH) // SUBLANES) * SUBLANES
MOD_COLS = 6 * LANES


def _cparams(sem):
    return pltpu.CompilerParams(dimension_semantics=sem, vmem_limit_bytes=VMEM_LIMIT)


def _dot(a, b):
    return jnp.dot(a, b, preferred_element_type=F32)


def _dot_nt(a, b):
    return lax.dot_general(a, b, (((1,), (1,)), ((), ())), preferred_element_type=F32)


def _silu(x):
    return x * jax.nn.sigmoid(x)


def _rope_tables(d):
    half = d // 2
    quarter = half // 2
    lane = np.arange(LANES)
    q = lane % d
    use_col = (q >= half)
    i = (q % half) % quarter
    first = (q % half) < quarter
    inv = (ROPE_BASE ** (-np.arange(quarter, dtype=np.float32) / np.float32(quarter))).astype(np.float32)
    t = np.arange(DEC_SEQ)
    pos = np.where(use_col[None, :], (t % GRID_W)[:, None], (t // GRID_W)[:, None]).astype(np.float32)
    ang = (pos * inv[i][None, :]).astype(np.float32)
    cos, sin = np.cos(ang), np.sin(ang)
    c = np.concatenate([np.ones((TM, LANES), np.float32), cos], axis=0)
    s1 = np.concatenate([np.zeros((TM, LANES), np.float32), np.where(first[None, :], -sin, 0.0)], axis=0)
    s2 = np.concatenate([np.zeros((TM, LANES), np.float32), np.where(first[None, :], 0.0, sin)], axis=0)
    return np.stack([c, s1, s2]).astype(np.float32)


def _rope_lanes(x, tab_ref, shift):
    return (x * tab_ref[0] + pltpu.roll(x, LANES - shift, 1) * tab_ref[1]
            + pltpu.roll(x, shift, 1) * tab_ref[2])


def _rope(x, tab_ref, shift):
    w = x.shape[1]
    return jnp.concatenate([_rope_lanes(x[:, c:c + LANES], tab_ref, shift) for c in range(0, w, LANES)], axis=1)


def _mod_kernel(c_ref, w_ref, b_ref, o_ref):
    c = c_ref[...]
    o_ref[0] = _dot(_silu(c).astype(BF16), w_ref[0].astype(BF16)) + b_ref[0]


def _modulation(cvecs, w_ada, b_ada):
    nb = 3 * D_MODEL // MOD_COLS
    return pl.pallas_call(
        _mod_kernel,
        grid=(DEPTH, nb),
        in_specs=[pl.BlockSpec((MOD_ROWS, D_MODEL), lambda l, j: (0, 0)),
                  pl.BlockSpec((1, D_MODEL, MOD_COLS), lambda l, j: (l, 0, j)),
                  pl.BlockSpec((1, 1, MOD_COLS), lambda l, j: (l, 0, j))],
        out_specs=pl.BlockSpec((1, MOD_ROWS, MOD_COLS), lambda l, j: (l, 0, j)),
        out_shape=jax.ShapeDtypeStruct((DEPTH, MOD_ROWS, 3 * D_MODEL), F32),
        compiler_params=_cparams(("arbitrary", "arbitrary")),
        name="modulation",
    )(cvecs, w_ada, b_ada.reshape(DEPTH, 1, 3 * D_MODEL))


W_STEPS = 8
W_ROWS = D_MODEL // W_STEPS


def _tile(i):
    return jnp.maximum(i - W_STEPS, 0)


def _wchunk(i):
    return jnp.minimum(i, W_STEPS - 1)


def _mod_row(t):
    return jnp.where(t < N_CTX_TILES, 0, 1 + (t - N_CTX_TILES) // LAT_TILES_PER_SEQ)


def _rope_blk(t):
    return jnp.where(t < N_CTX_TILES, 0, 1 + (t - N_CTX_TILES) % LAT_TILES_PER_SEQ)


def _load_x(t, xa_ref, xb_ref):
    return jnp.where(t < N_CTX_TILES, xa_ref[...], xb_ref[...])


def _cast_rows(i, src_ref, dst_ref):
    r0 = pl.multiple_of(i * W_ROWS, W_ROWS)
    dst_ref[pl.ds(r0, W_ROWS), :] = src_ref[0].astype(BF16)


WI_G = 0
WI_AQ = WI_G + N_BRANCH * BR_W
WI_AK, WI_AV, WI_BX, WI_CQ, WI_CKV = (WI_AQ + k * BR_W for k in range(1, 6))
WI_DQ = WI_CKV + 2 * BR_W
WI_DKV = WI_DQ + BR_W
SEQ_PER_TILE = TM // SEQ
IN_ROWS = 256


def _store_heads(ref, val, n_heads, width):
    vt = val.T
    for s in range(SEQ_PER_TILE):
        for h in range(n_heads):
            ref[s, 0, h] = vt[h * width:(h + 1) * width, s * SEQ:(s + 1) * SEQ]


def _per_query_head(x):
    return jnp.concatenate([x[:, kv * HD:(kv + 1) * HD] for kv in range(D_KV) for _ in range(D_GROUP)], axis=1)


def _inproj_kernel(n_alias, xa_ref, xb_ref, mod_ref, g_ref, w_ref, ta_ref, td_ref, *refs):
    h_ref, proj_ref, bx_ref, dk_ref, dv_ref, nk_ref, nv_ref, sk_ref, sv_ref, w_s = refs[n_alias:]
    i = pl.program_id(0)

    @pl.when(i < W_STEPS)
    def _():
        _cast_rows(i, w_ref, w_s)

    @pl.when(i >= W_STEPS)
    def _():
        t = i - W_STEPS
        is_ctx = t < N_CTX_TILES
        mod = mod_ref[0]
        shift, scale = mod[:, :D_MODEL], mod[:, D_MODEL:2 * D_MODEL]
        w2 = 2 * BR_W
        kept = []
        for r in range(0, TM, IN_ROWS):
            rs = pl.ds(r, IN_ROWS)
            ta, td = ta_ref.at[:, rs, :], td_ref.at[:, rs, :]
            x = jnp.where(is_ctx, xa_ref[rs, :], xb_ref[rs, :])
            xn = x * lax.rsqrt(jnp.mean(x * x, axis=-1, keepdims=True) + EPS) * g_ref[0]
            hb = (xn * (1.0 + scale) + shift).astype(BF16)
            h_ref[rs, :] = hb

            def mm(c0, width):
                return _dot(hb, w_s[:, c0:c0 + width])

            gs = [mm(WI_G + c, w2) for c in range(0, N_BRANCH * BR_W, w2)]
            aq, ak, av = mm(WI_AQ, BR_W), mm(WI_AK, BR_W), mm(WI_AV, BR_W)
            cq, ckv = mm(WI_CQ, BR_W), mm(WI_CKV, w2)
            dq, dkv = mm(WI_DQ, BR_W), mm(WI_DKV, 2 * D_KV * HD)
            dk, dv = _rope(dkv[:, :D_KV * HD], td, HD // 4), dkv[:, D_KV * HD:]
            bx_ref[rs, :] = mm(WI_BX, BR_W)

            for n, g in enumerate(gs):
                proj_ref[rs, COL_G + n * w2:COL_G + (n + 1) * w2] = _silu(g).astype(BF16)
            proj_ref[rs, COL_AQ:COL_AQ + BR_W] = _rope(aq * QA_SCALE, ta, A_QK // 4).astype(BF16)
            proj_ref[rs, COL_AK:COL_AK + BR_W] = _rope(ak, ta, A_QK // 4).astype(BF16)
            proj_ref[rs, COL_AV:COL_AV + BR_W] = av.astype(BF16)
            proj_ref[rs, COL_CQ:COL_CQ + BR_W] = (cq * QH_SCALE).astype(BF16)
            proj_ref[rs, COL_CK:COL_CK + w2] = ckv.astype(BF16)
            proj_ref[rs, COL_DQ:COL_DQ + BR_W] = _rope(dq * QH_SCALE, td, HD // 4).astype(BF16)
            proj_ref[rs, COL_DK:COL_DK + BR_W] = _per_query_head(dk.astype(BF16))
            proj_ref[rs, COL_DV:COL_DV + BR_W] = _per_query_head(dv.astype(BF16))
            kept.append((ak, av, ckv, dk, dv))

        @pl.when(is_ctx)
        def _():
            ak, av, ckv, dk, dv = (jnp.concatenate(v, axis=0) for v in zip(*kept))
            akt = ak.T
            for s in range(SEQ_PER_TILE):
                for h in range(A_HEADS):
                    for m in range(2):
                        c0 = h * HD + m * A_QK
                        dk_ref[s, 0, h, m] = akt[c0:c0 + A_QK, s * SEQ:(s + 1) * SEQ]
            _store_heads(dv_ref, av, A_HEADS, HD)
            _store_heads(nk_ref, ckv[:, :BR_W], C_HEADS, HD)
            _store_heads(nv_ref, ckv[:, BR_W:], C_HEADS, HD)
            _store_heads(sk_ref, dk, D_KV, HD)
            _store_heads(sv_ref, dv, D_KV, HD)


def _cache_shapes():
    hs = lambda n: (BATCH, DEPTH, n, HD, SEQ)
    return [(BATCH, DEPTH, A_HEADS, 2, A_QK, SEQ), hs(A_HEADS), hs(C_HEADS), hs(C_HEADS), hs(D_KV), hs(D_KV)]


def _inproj(xa, xb, xb_off, mod3, l, norm_g, w_in, tab_a, tab_d, caches):
    row = lambda i: (_tile(i), 0)
    ctx_blk = lambda i: jnp.minimum(_tile(i), N_CTX_TILES - 1)

    def cache_spec(shape):
        blk = (SEQ_PER_TILE, 1) + shape[2:]
        nz = len(shape) - 2
        return pl.BlockSpec(blk, lambda i: (ctx_blk(i), l) + (0,) * nz)

    cshapes = _cache_shapes()
    aliases = {} if caches is None else {7 + k: 3 + k for k in range(6)}
    extra_specs = [] if caches is None else [pl.BlockSpec(memory_space=pl.ANY)] * 6
    extra_args = [] if caches is None else list(caches)
    outs = pl.pallas_call(
        functools.partial(_inproj_kernel, len(extra_args)),
        grid=(W_STEPS + N_TILES,),
        in_specs=[
            pl.BlockSpec((TM, D_MODEL), lambda i: (ctx_blk(i), 0)),
            pl.BlockSpec((TM, D_MODEL), lambda i: (jnp.maximum(_tile(i) - N_CTX_TILES, 0) + xb_off, 0)),
            pl.BlockSpec((1, 1, 3 * D_MODEL), lambda i: (l * MOD_ROWS + _mod_row(_tile(i)), 0, 0)),
            pl.BlockSpec((1, 1, D_MODEL), lambda i: (l, 0, 0)),
            pl.BlockSpec((1, W_ROWS, IN_W), lambda i: (l, _wchunk(i), 0)),
            pl.BlockSpec((3, TM, LANES), lambda i: (0, _rope_blk(_tile(i)), 0)),
            pl.BlockSpec((3, TM, LANES), lambda i: (0, _rope_blk(_tile(i)), 0)),
        ] + extra_specs,
        out_specs=[
            pl.BlockSpec((TM, D_MODEL), row),
            pl.BlockSpec((TM, PROJ_W), row),
            pl.BlockSpec((TM, BR_W), row),
        ] + [cache_spec(s) for s in cshapes],
        out_shape=[
            jax.ShapeDtypeStruct((N_TOK, D_MODEL), BF16),
            jax.ShapeDtypeStruct((N_TOK, PROJ_W), BF16),
            jax.ShapeDtypeStruct((N_TOK, BR_W), F32),
        ] + [jax.ShapeDtypeStruct(s, F32) for s in cshapes],
        scratch_shapes=[pltpu.VMEM((D_MODEL, IN_W), BF16)],
        input_output_aliases=aliases,
        compiler_params=_cparams(("arbitrary",)),
        name=f"inproj_l{l}",
    )(xa, xb, mod3, norm_g, w_in, tab_a, tab_d, *extra_args)
    return outs[0], outs[1], outs[2], tuple(outs[3:])


def _diff_lambda(lam_ref, lam_init):
    lv = lam_ref[0]
    s1 = jnp.sum(lv[0:1] * lv[1:2], axis=-1, keepdims=True)
    s2 = jnp.sum(lv[2:3] * lv[3:4], axis=-1, keepdims=True)
    return jnp.exp(s1) - jnp.exp(s2) + lam_init


def _diff_norm(o, dg_ref, lam_init):
    y = o * lax.rsqrt(jnp.mean(o * o, axis=-1, keepdims=True) + EPS) * dg_ref[0]
    return y * (1.0 - lam_init)


def _exp2_parts(parts, extra=None):
    m = functools.reduce(jnp.maximum, [jnp.max(s, axis=-1, keepdims=True) for s in parts])
    if extra is not None:
        m = jnp.maximum(m, extra)
    return [jnp.exp2(s - m).astype(BF16) for s in parts], m


def _pipelined(n, score_fn, pv_fn):
    outs = []
    nxt = score_fn(0)
    for c in range(n):
        cur, nxt = nxt, (score_fn(c + 1) if c + 1 < n else None)
        outs.append(pv_fn(c, cur))
    return outs


def _with_ones(v):
    return jnp.concatenate([v, jnp.ones_like(v)], axis=1)


def _with_ones_t(vt):
    return jnp.concatenate([vt, jnp.ones((2 * SUBLANES, vt.shape[1]), vt.dtype)], axis=0)


def _sink_col(sink2, kv, rows):
    r = lax.broadcasted_iota(jnp.int32, (D_GROUP * rows, 1), 0)
    return jnp.where(r < rows, sink2[:, kv * D_GROUP:kv * D_GROUP + 1], sink2[:, kv * D_GROUP + 1:kv * D_GROUP + 2])


CTX_PER_STEP = 2


def _lane_groups(n, rows):
    grp = lax.broadcasted_iota(jnp.int32, (rows, BR_W), 1) // (BR_W // n)
    return [grp == p for p in range(n)]


def _only(x, m):
    return jnp.where(m, x, jnp.zeros_like(x))


def _ones_outside(v, m):
    return jnp.where(m, v, jnp.ones_like(v))


def _swap_halves(p):
    return jnp.concatenate([p[:, BR_W // 2:], p[:, :BR_W // 2]], axis=1)


def _pick_heads(os):
    grp = lax.broadcasted_iota(jnp.int32, os[0].shape, 1) // HD
    acc = os[0]
    for h in range(1, len(os)):
        acc = jnp.where(grp == h, os[h], acc)
    return acc


def _head_rms(o, dg4_ref, lam_init):
    r = lax.broadcasted_iota(jnp.int32, (BR_W, BR_W), 0) // HD
    c = lax.broadcasted_iota(jnp.int32, (BR_W, BR_W), 1) // HD
    ss = _dot((o * o).astype(BF16), jnp.where(r == c, 1.0, 0.0).astype(BF16))
    return o * lax.rsqrt(ss * (1.0 / HD) + EPS) * dg4_ref[0] * (1.0 - lam_init)


def _ctx_attn_kernel(n_alias, lam_init, aq_ref, ak_ref, av_ref, cq_ref, ck_ref, cv_ref, dq_ref, dk_ref, dv_ref,
                     ga_ref, gc_ref, gd_ref, lam_ref, dg4_ref, sink_ref,
                     bx_ref, gr_ref, cw_ref, cb_ref, wa_ref, ba_ref, wx_ref, bxg_ref, llam_ref, *refs):
    z_ref, fin_ref = refs[n_alias:]
    seqs = [slice(s * SEQ, (s + 1) * SEQ) for s in range(CTX_PER_STEP)]

    xcs = [_lru_conv(bx_ref[rs, :], cw_ref[0], cb_ref[0]) for rs in seqs]
    xc_all = jnp.concatenate(xcs, axis=0)
    pres = [_lru_gates(d, xc_all, wa_ref, ba_ref, wx_ref, bxg_ref) for d in range(2)]
    for s, rs in enumerate(seqs):
        h0 = jnp.zeros((1, BR_W), F32)
        yf, hf = _lru_scan(0, xcs[s], [p[rs] for p in pres[0]], llam_ref[0, 0:1, :], h0)
        yb, hb = _lru_scan(1, xcs[s], [p[rs] for p in pres[1]], llam_ref[0, 1:2, :], h0)
        z_ref[rs, BR_W:2 * BR_W] = ((yf + yb) * gr_ref[rs, :].astype(F32)).astype(BF16)
        fin_ref[s, 0, 0:1, :] = hf
        fin_ref[s, 0, 1:2, :] = hb

    lam = _diff_lambda(lam_ref, lam_init)
    sink2 = sink_ref[0] * LOG2E
    m_qk = _lane_groups(2 * A_HEADS, SEQ)
    m_hd = _lane_groups(C_HEADS, SEQ)

    sa, sc, sd = [], [], []
    for rs in seqs:
        aq, cq, dq = aq_ref[rs, :], cq_ref[rs, :], dq_ref[rs, :]
        ak, ck, dk = ak_ref[rs, :], ck_ref[rs, :], dk_ref[rs, :]
        sa.append([_dot_nt(_only(aq, m), ak) for m in m_qk])
        sc.append([_dot_nt(_only(cq, m), ck) for m in m_hd])
        sd.append([_dot_nt(_only(dq, m), dk) for m in m_hd])

    ea =[[_exp2_parts([s])[0][0] for s in ss] for ss in sa]
    ec = [[_exp2_parts([s])[0][0] for s in ss] for ss in sc]
    ed = [[_exp2_parts([s], extra=sink2[:, j:j + 1]) for j, s in enumerate(ss)] for ss in sd]

    for i, rs in enumerate(seqs):
        av, cv, dv = av_ref[rs, :], cv_ref[rs, :], dv_ref[rs, :]
        os = []
        for h in range(A_HEADS):
            w = _ones_outside(av, m_hd[h])
            p0, p1 = _dot(ea[i][2 * h], w), _dot(ea[i][2 * h + 1], w)
            os.append(p0 * (1.0 / _swap_halves(p0)) - p1 * (lam / _swap_halves(p1)))
        z_ref[rs, 0:BR_W] = (_head_rms(_pick_heads(os), dg4_ref, lam_init) * ga_ref[rs, :].astype(F32)).astype(BF16)

        os = []
        for h in range(C_HEADS):
            p = _dot(ec[i][h], _ones_outside(cv, m_hd[h]))
            os.append(p * (1.0 / _swap_halves(p)))
        z_ref[rs, 2 * BR_W:3 * BR_W] = (_pick_heads(os) * gc_ref[rs, :].astype(F32)).astype(BF16)

        os = []
        for j in range(D_HEADS):
            (e,), m = ed[i][j]
            p = _dot(e, _ones_outside(dv, m_hd[j]))
            os.append(p * (1.0 / (_swap_halves(p) + jnp.exp2(sink2[:, j:j + 1] - m))))
        z_ref[rs, 3 * BR_W:4 * BR_W] = (_pick_heads(os) * gd_ref[rs, :].astype(F32)).astype(BF16)


def _ctx_attn(proj, bx, l, diff_lambda, dg4, sink3, lru_params, states):
    lam_init = 0.8 - 0.6 * math.exp(-0.3 * l)
    colblk = lambda cb: pl.BlockSpec((CTX_PER_STEP * SEQ, BR_W), lambda b: (b, cb))
    extra = [] if states is None else [states]
    return pl.pallas_call(
        functools.partial(_ctx_attn_kernel, len(extra), lam_init),
        grid=(BATCH // CTX_PER_STEP,),
        in_specs=[colblk(COL_AQ // BR_W), colblk(COL_AK // BR_W), colblk(COL_AV // BR_W),
                  colblk(COL_CQ // BR_W), colblk(COL_CK // BR_W), colblk(COL_CV // BR_W),
                  colblk(COL_DQ // BR_W), colblk(COL_DK // BR_W), colblk(COL_DV // BR_W),
                  colblk(GATE_A), colblk(GATE_C), colblk(GATE_D),
                  pl.BlockSpec((1, 4, A_QK), lambda b: (l, 0, 0)),
                  pl.BlockSpec((1, 1, BR_W), lambda b: (l, 0, 0)),
                  pl.BlockSpec((1, 1, D_HEADS), lambda b: (l, 0, 0)),
                  pl.BlockSpec((CTX_PER_STEP * SEQ, BR_W), lambda b: (b, 0)), colblk(GATE_R)]
                 + _lru_specs(l, lambda b: (l, 0, 0)) + [pl.BlockSpec(memory_space=pl.ANY)] * len(extra),
        out_specs=[pl.BlockSpec((CTX_PER_STEP * SEQ, N_BRANCH * BR_W), lambda b: (b, 0)),
                   pl.BlockSpec((CTX_PER_STEP, 1, 2, BR_W), lambda b: (b, l, 0, 0))],
        out_shape=[jax.ShapeDtypeStruct((N_CTX, N_BRANCH * BR_W), BF16),
                   jax.ShapeDtypeStruct((BATCH, DEPTH, 2, BR_W), F32)],
        input_output_aliases={} if states is None else {24: 1},
        compiler_params=_cparams(("arbitrary",)),
        name=f"ctx_attn_l{l}",
    )(*([proj] * 12), diff_lambda, dg4, sink3, bx, proj, *lru_params, *extra)


LA_QB = 256


def _lat_diff_kernel(lam_init, aq_ref, ak_ref, av_ref, ck_ref, cv_ref, ga_ref, lam_ref, dg_ref, za_ref):
    lam = _diff_lambda(lam_ref, lam_init)
    aq = aq_ref[...]
    va_ctx = [_with_ones_t(cv_ref[0, 0, h].astype(BF16)) for h in range(A_HEADS)]
    va_loc = [_with_ones(av_ref[:, h * HD:(h + 1) * HD]) for h in range(A_HEADS)]

    def scores(c):
        h, m = divmod(c, 2)
        lo = h * HD + m * A_QK
        q = aq[:, lo:lo + A_QK]
        return [_dot(q, ck_ref[0, 0, h, m].astype(BF16)),
                _dot_nt(q, ak_ref[:, lo:lo + A_QK])]

    def pv(c, s):
        (e_ctx, e_loc), _ = _exp2_parts(s)
        p_ctx, p_loc = _dot_nt(e_ctx, va_ctx[c // 2]), _dot(e_loc, va_loc[c // 2])
        return p_ctx[:, :HD] + p_loc[:, :HD], p_ctx[:, HD:HD + 1] + p_loc[:, HD:HD + 1]

    ss = [scores(c) for c in range(2 * A_HEADS)]
    nd = [pv(c, s) for c, s in enumerate(ss)]
    outs = []
    for h in range(A_HEADS):
        (n0, d0), (n1, d1) = nd[2 * h], nd[2 * h + 1]
        outs.append(_diff_norm(n0 * (1.0 / d0) - n1 * (lam / d1), dg_ref, lam_init))
    za_ref[...] = (jnp.concatenate(outs, axis=1) * ga_ref[...].astype(F32)).astype(BF16)


def _lat_diff(proj, cache_k, cache_v, l, diff_lambda, dg3):
    lam_init = 0.8 - 0.6 * math.exp(-0.3 * l)
    nq = DEC_SEQ // LA_QB
    qrow = lambda b, j: N_CTX // LA_QB + b * nq + j
    srow = lambda b, j: N_CTX // DEC_SEQ + b
    return pl.pallas_call(
        functools.partial(_lat_diff_kernel, lam_init),
        grid=(DEC_BATCH, nq),
        in_specs=[pl.BlockSpec((LA_QB, BR_W), lambda b, j: (qrow(b, j), COL_AQ // BR_W)),
                  pl.BlockSpec((DEC_SEQ, BR_W), lambda b, j: (srow(b, j), COL_AK // BR_W)),
                  pl.BlockSpec((DEC_SEQ, BR_W), lambda b, j: (srow(b, j), COL_AV // BR_W)),
                  pl.BlockSpec((1, 1, A_HEADS, 2, A_QK, PAST_LEN), lambda b, j: (b, l, 0, 0, 0, 0)),
                  pl.BlockSpec((1, 1, A_HEADS, HD, PAST_LEN), lambda b, j: (b, l, 0, 0, 0)),
                  pl.BlockSpec((LA_QB, BR_W), lambda b, j: (qrow(b, j), 0)),
                  pl.BlockSpec((1, 4, A_QK), lambda b, j: (l, 0, 0)),
                  pl.BlockSpec((1, 1, HD), lambda b, j: (l, 0, 0))],
        out_specs=pl.BlockSpec((LA_QB, BR_W), lambda b, j: (b * nq + j, 0)),
        out_shape=jax.ShapeDtypeStruct((N_LAT, BR_W), BF16),
        compiler_params=_cparams(("arbitrary", "arbitrary")),
        name=f"lat_diff_l{l}",
    )(proj, proj, proj, cache_k, cache_v, proj, diff_lambda, dg3)


def _na_bias_kernel(rpb_ref, o_ref):
    l, h = pl.program_id(0), pl.program_id(1)
    base = (l * C_HEADS + h) * ((2 * NA_ROWS - 1) * (2 * NA_COLS - 1))
    qc = lax.broadcasted_iota(jnp.int32, (GRID_W, GRID_W), 0)
    kc = lax.broadcasted_iota(jnp.int32, (GRID_W, GRID_W), 1)
    cs = jnp.clip(qc - NA_COLS // 2, 0, GRID_W - NA_COLS)
    ok = (kc >= cs) & (kc < cs + NA_COLS)
    dcol = kc - qc + (NA_COLS - 1)
    neg = jnp.full((GRID_W, GRID_W), NEG, F32)
    tabs = []
    for d in range(2 * NA_ROWS - 1):
        t = neg
        for dc in range(2 * NA_COLS - 1):
            t = jnp.where(ok & (dcol == dc), rpb_ref[base + d * (2 * NA_COLS - 1) + dc] * LOG2E, t)
        tabs.append(t)
    for half in range(2):
        for rq in range(NA_ROWS):
            r = half * NA_ROWS + rq
            rs = min(max(r - NA_ROWS // 2, 0), GRID_ROWS - NA_ROWS)
            blks = []
            for kr in range(NA_KROWS):
                kabs = half * (NA_HALF_KSTART // GRID_W) + kr
                blks.append(tabs[kabs - r + NA_ROWS - 1] if rs <= kabs < rs + NA_ROWS else neg)
            o_ref[0, 0, half, rq * GRID_W:(rq + 1) * GRID_W, :] = jnp.concatenate(blks, axis=1)


def _na_bias(na_rpb):
    return pl.pallas_call(
        _na_bias_kernel,
        grid=(DEPTH, C_HEADS),
        in_specs=[pl.BlockSpec(memory_space=pltpu.SMEM)],
        out_specs=pl.BlockSpec((1, 1, 2, NA_HALF_Q, NA_KSPAN), lambda l, h: (l, h, 0, 0, 0)),
        out_shape=jax.ShapeDtypeStruct((DEPTH, C_HEADS, 2, NA_HALF_Q, NA_KSPAN), F32),
        compiler_params=_cparams(("arbitrary", "arbitrary")),
        name="na_bias",
    )(na_rpb.reshape(-1))


def _lat_na_kernel(half, cq_ref, ck_ref, cv_ref, kc_ref, vc_ref, bias_ref, gc_ref,
                   bx_ref, gr_ref, cw_ref, cb_ref, wa_ref, ba_ref, wx_ref, bxg_ref, llam_ref, st_ref, zc_ref, y_ref):
    k0 = half * NA_HALF_KSTART
    cq = cq_ref[...]
    kl_all = ck_ref[k0:k0 + NA_KSPAN, :]
    vl_all = cv_ref[k0:k0 + NA_KSPAN, :]

    def scores(h):
        q = cq[:, h * HD:(h + 1) * HD]
        return [_dot(q, kc_ref[0, 0, h].astype(BF16)),
                _dot_nt(q, kl_all[:, h * HD:(h + 1) * HD]) + bias_ref[0, h, 0]]

    def pv(h, s):
        (e_ctx, e_loc), _ = _exp2_parts(s)
        p_ctx = _dot_nt(e_ctx, _with_ones_t(vc_ref[0, 0, h].astype(BF16)))
        p_loc = _dot(e_loc, _with_ones(vl_all[:, h * HD:(h + 1) * HD]))
        return (p_ctx[:, :HD] + p_loc[:, :HD]) * (1.0 / (p_ctx[:, HD:HD + 1] + p_loc[:, HD:HD + 1]))

    xc = _lru_conv(bx_ref[...], cw_ref[0], cb_ref[0])
    pre = _lru_gates(half, xc, wa_ref, ba_ref, wx_ref, bxg_ref)
    y, _ = _lru_scan(half, xc, pre, llam_ref[0, half:half + 1, :], st_ref[0, 0, half:half + 1, :])
    y_ref[...] = y * gr_ref[...].astype(F32)

    ss = [scores(h) for h in range(C_HEADS)]
    outs = [pv(h, s) for h, s in enumerate(ss)]
    zc_ref[...] = (jnp.concatenate(outs, axis=1) * gc_ref[...].astype(F32)).astype(BF16)


def _lat_na(half, proj, bx, cache_k, cache_v, bias, l, lru_params, state_lru):
    qrow = lambda b: N_CTX // NA_HALF_Q + b * 2 + half
    srow = lambda b: N_CTX // DEC_SEQ + b
    return pl.pallas_call(
        functools.partial(_lat_na_kernel, half),
        grid=(DEC_BATCH,),
        in_specs=[pl.BlockSpec((NA_HALF_Q, BR_W), lambda b: (qrow(b), COL_CQ // BR_W)),
                  pl.BlockSpec((DEC_SEQ, BR_W), lambda b: (srow(b), COL_CK // BR_W)),
                  pl.BlockSpec((DEC_SEQ, BR_W), lambda b: (srow(b), COL_CV // BR_W)),
                  pl.BlockSpec((1, 1, C_HEADS, HD, PAST_LEN), lambda b: (b, l, 0, 0, 0)),
                  pl.BlockSpec((1, 1, C_HEADS, HD, PAST_LEN), lambda b: (b, l, 0, 0, 0)),
                  pl.BlockSpec((1, C_HEADS, 1, NA_HALF_Q, NA_KSPAN), lambda b: (l, 0, half, 0, 0)),
                  pl.BlockSpec((NA_HALF_Q, BR_W), lambda b: (qrow(b), GATE_C)),
                  pl.BlockSpec((DEC_SEQ, BR_W), lambda b: (srow(b), 0)),
                  pl.BlockSpec((DEC_SEQ, BR_W), lambda b: (srow(b), GATE_R))]
                 + _lru_specs(l, lambda b: (l, 0, 0))
                 + [pl.BlockSpec((1, 1, 2, BR_W), lambda b: (b, l, 0, 0))],
        out_specs=[pl.BlockSpec((NA_HALF_Q, BR_W), lambda b: (b, 0)),
                   pl.BlockSpec((DEC_SEQ, BR_W), lambda b: (b, 0))],
        out_shape=[jax.ShapeDtypeStruct((N_LAT // 2, BR_W), BF16), jax.ShapeDtypeStruct((N_LAT, BR_W), F32)],
        compiler_params=_cparams(("arbitrary",)),
        name=f"lat_na{half}_l{l}",
    )(proj, proj, proj, cache_k, cache_v, bias, proj, bx, proj, *lru_params, state_lru)


def _lat_swa_kernel(dq_ref, dk_ref, dv_ref, kc_ref, vc_ref, sink_ref, gd_ref, zd_ref):
    nq = DEC_SEQ // SWA_QB
    rows = D_GROUP * SWA_QB
    dq, dk, dv = dq_ref[...], dk_ref[...], dv_ref[...]
    sink2 = sink_ref[0] * LOG2E
    m_hd = _lane_groups(D_HEADS, DEC_SEQ)
    starts = [min(max(j * SWA_QB - WIN, 0), DEC_SEQ - SWA_SPAN) for j in range(nq)]
    qi = lax.broadcasted_iota(jnp.int32, (rows, SWA_SPAN), 0) % SWA_QB
    ki = lax.broadcasted_iota(jnp.int32, (rows, SWA_SPAN), 1)
    masks = {off: jnp.abs(qi + off - ki) <= WIN for off in sorted({j * SWA_QB - s for j, s in enumerate(starts)})}
    kc_t = jnp.concatenate([kc_ref[0, 0, kv] for kv in range(D_KV) for _ in range(D_GROUP)], axis=0).astype(BF16)
    vc_t = jnp.concatenate([vc_ref[0, 0, kv] for kv in range(D_KV) for _ in range(D_GROUP)], axis=0).astype(BF16)
    feat = lax.broadcasted_iota(jnp.int32, (BR_W, 1), 0) // (D_GROUP * HD)

    qm = [_only(dq, m) for m in m_hd]
    sks = [_sink_col(sink2, kv, SWA_QB) for kv in range(D_KV)]
    w_loc, w_ctx = [], []
    for kv in range(D_KV):
        w_loc.append(_ones_outside(dv, m_hd[kv * D_GROUP] | m_hd[kv * D_GROUP + 1]))
        w_ctx.append(jnp.where(feat == kv, vc_t, jnp.ones_like(vc_t)))

    def scores(c):
        kv, j = divmod(c, nq)
        q2 = jnp.concatenate([qm[kv * D_GROUP + g][j * SWA_QB:(j + 1) * SWA_QB] for g in range(D_GROUP)], axis=0)
        return [_dot(q2, kc_t),
                jnp.where(masks[j * SWA_QB - starts[j]], _dot_nt(q2, dk[starts[j]:starts[j] + SWA_SPAN]), NEG)]

    def pv(c, s):
        kv, j = divmod(c, nq)
        (e_ctx, e_loc), m = _exp2_parts(s, extra=sks[kv])
        p = _dot_nt(e_ctx, w_ctx[kv]) + _dot(e_loc, w_loc[kv][starts[j]:starts[j] + SWA_SPAN])
        return p * (1.0 / (_swap_halves(p) + jnp.exp2(sks[kv] - m)))

    o = _pipelined(D_KV * nq, scores, pv)
    os = [jnp.concatenate([o[kv * nq + j][g * SWA_QB:(g + 1) * SWA_QB] for j in range(nq)], axis=0)
          for kv in range(D_KV) for g in range(D_GROUP)]
    zd_ref[...] = (_pick_heads(os) * gd_ref[...].astype(F32)).astype(BF16)


def _lat_swa(proj, cache_k, cache_v, sink3, l):
    srow = lambda b: (N_CTX // DEC_SEQ + b)
    return pl.pallas_call(
        _lat_swa_kernel,
        grid=(DEC_BATCH,),
        in_specs=[pl.BlockSpec((DEC_SEQ, BR_W), lambda b: (srow(b), COL_DQ // BR_W)),
                  pl.BlockSpec((DEC_SEQ, BR_W), lambda b: (srow(b), COL_DK // BR_W)),
                  pl.BlockSpec((DEC_SEQ, BR_W), lambda b: (srow(b), COL_DV // BR_W)),
                  pl.BlockSpec((1, 1, D_KV, HD, PAST_LEN), lambda b: (b, l, 0, 0, 0)),
                  pl.BlockSpec((1, 1, D_KV, HD, PAST_LEN), lambda b: (b, l, 0, 0, 0)),
                  pl.BlockSpec((1, 1, D_HEADS), lambda b: (l, 0, 0)),
                  pl.BlockSpec((DEC_SEQ, BR_W), lambda b: (srow(b), GATE_D))],
        out_specs=pl.BlockSpec((DEC_SEQ, BR_W), lambda b: (b, 0)),
        out_shape=jax.ShapeDtypeStruct((N_LAT, BR_W), BF16),
        compiler_params=_cparams(("arbitrary",)),
        name=f"lat_swa_l{l}",
    )(proj, proj, proj, cache_k, cache_v, sink3, proj)


def _lru_conv(x, cw, cb):
    seq = x.shape[0]
    x3 = x.reshape(seq // SUBLANES, SUBLANES, BR_W)
    tin = lax.broadcasted_iota(jnp.int32, (1, SUBLANES, 1), 1)
    zero = jnp.zeros((1, SUBLANES, BR_W), F32)

    def at(shift):
        r = pltpu.roll(x3, (-shift) % SUBLANES, 1)
        if shift < 0:
            return jnp.where(tin < -shift, jnp.concatenate([zero, r[:-1]], axis=0), r)
        return jnp.where(tin >= SUBLANES - shift, jnp.concatenate([r[1:], zero], axis=0), r)

    xc = cb + at(-1) * cw[0:1] + x3 * cw[1:2] + at(1) * cw[2:3] + at(2) * cw[3:4]
    return xc.reshape(seq, BR_W)


def _gate_dense(w_ref, d):
    rows = []
    for n in range(B_BLOCKS):
        pieces = [jnp.zeros((B_BLK, n * B_BLK), BF16)] if n else []
        pieces.append(w_ref[0, d, n].astype(BF16))
        if n < B_BLOCKS - 1:
            pieces.append(jnp.zeros((B_BLK, (B_BLOCKS - 1 - n) * B_BLK), BF16))
        rows.append(jnp.concatenate(pieces, axis=1))
    return jnp.concatenate(rows, axis=0)


def _lru_gates(d, xc, wa_ref, ba_ref, wx_ref, bxg_ref):
    xb = xc.astype(BF16)
    return (_dot(xb, _gate_dense(wa_ref, d)) + ba_ref[0, d:d + 1, :],
            _dot(xb, _gate_dense(wx_ref, d)) + bxg_ref[0, d:d + 1, :])


def _lru_scan(d, xc, pre, lam_row, h0):
    seq = xc.shape[0]
    nl = -lam_row
    softplus = jnp.maximum(nl, 0.0) + jnp.log1p(jnp.exp(-jnp.abs(nl)))
    r = jax.nn.sigmoid(pre[0])
    ig = jax.nn.sigmoid(pre[1])
    log_a = -LRU_C * r * softplus
    a = jnp.exp(log_a)
    u = jnp.exp(0.5 * jnp.log(-jnp.tanh(log_a) * (a * a + 1.0))) * (ig * xc)
    nt = seq // SUBLANES
    tin = lax.broadcasted_iota(jnp.int32, (1, SUBLANES, 1), 1)
    a = a.reshape(nt, SUBLANES, BR_W)
    u = u.reshape(nt, SUBLANES, BR_W)
    s = 1
    while s < SUBLANES:
        if d == 0:
            keep = tin >= s
            a_n = jnp.where(keep, pltpu.roll(a, s, 1), 1.0)
            u_n = jnp.where(keep, pltpu.roll(u, s, 1), 0.0)
        else:
            keep = tin < SUBLANES - s
            a_n = jnp.where(keep, pltpu.roll(a, SUBLANES - s, 1), 1.0)
            u_n = jnp.where(keep, pltpu.roll(u, SUBLANES - s, 1), 0.0)
        u = u + a * u_n
        a = a * a_n
        s *= 2
    ys = [None] * nt
    h = h0
    for k in (range(nt) if d == 0 else range(nt - 1, -1, -1)):
        y = u[k] + a[k] * h
        ys[k] = y
        h = y[SUBLANES - 1:, :] if d == 0 else y[:1, :]
    return jnp.concatenate(ys, axis=0), h


def _lru_specs(l, imap):
    wspec = pl.BlockSpec((1, 2, B_BLOCKS, B_BLK, B_BLK), lambda *g: imap(*g) + (0, 0))
    vspec = pl.BlockSpec((1, 2, BR_W), imap)
    return [pl.BlockSpec((1, CONV_W, BR_W), imap), pl.BlockSpec((1, 1, BR_W), imap),
            wspec, vspec, wspec, vspec, vspec]


MERGE_COLS = 512

def _merge_kernel(layer, final, xa_ref, xb_ref, mod_ref, h_ref, zctx_ref, zal_ref, yf_ref, yb_ref,
                  zc0_ref, zc1_ref, zdl_ref, wmg_hbm, bmg_ref, wbo_ref, wo_ref, nf_ref, *refs):
    out_refs, (wmg_s, wbo_s, wo_s, stage, sem) = refs[:-5], refs[-5:]
    i = pl.program_id(0)

    def wmg_copy(chunk, n):
        slot = chunk % 2
        return pltpu.make_async_copy(wmg_hbm.at[layer, pl.ds(chunk * W_ROWS, W_ROWS), n, :],
                                     stage.at[slot, n], sem.at[slot, n])

    @pl.when(i == 0)
    def _():
        for n in range(N_BRANCH):
            wmg_copy(0, n).start()

    @pl.when(i < W_STEPS - 1)
    def _():
        for n in range(N_BRANCH):
            wmg_copy(i + 1, n).start()

    @pl.when(i < W_STEPS)
    def _():
        r0 = pl.multiple_of(i * W_ROWS, W_ROWS)
        for n in range(N_BRANCH):
            wmg_copy(i, n).wait()
            wmg_s[pl.ds(r0, W_ROWS), n * D_MODEL:(n + 1) * D_MODEL] = stage[i % 2, n].astype(BF16)
        _cast_rows(i, wbo_ref, wbo_s)
        _cast_rows(i, wo_ref, wo_s)

    @pl.when(i >= W_STEPS)
    def _():
        t = i - W_STEPS
        x = _load_x(t, xa_ref, xb_ref)
        gate = mod_ref[0][:, 2 * D_MODEL:]
        h = h_ref[...]
        is_ctx = t < N_CTX_TILES
        second_half = (t - N_CTX_TILES) % LAT_TILES_PER_SEQ == 1
        z_lat = [zal_ref[...],
                 (yf_ref[...] + yb_ref[...]).astype(BF16),
                 jnp.where(second_half, zc1_ref[...], zc0_ref[...]),
                 zdl_ref[...]]
        zs = [jnp.where(is_ctx, zctx_ref[:, n * BR_W:(n + 1) * BR_W], z_lat[n]) for n in range(N_BRANCH)]
        bmg = bmg_ref[0]
        cols = []
        for c in range(0, D_MODEL, MERGE_COLS):
            acc = None
            for n in range(N_BRANCH):
                g = jax.nn.sigmoid(_dot(h, wmg_s[:, n * D_MODEL + c:n * D_MODEL + c + MERGE_COLS])
                                   + bmg[n:n + 1, c:c + MERGE_COLS])
                term = g * _dot(zs[n], wbo_s[n * BR_W:(n + 1) * BR_W, c:c + MERGE_COLS])
                acc = term if acc is None else acc + term
            cols.append(acc.astype(BF16))
        merged = jnp.concatenate(cols, axis=1)
        xn = x + gate * _dot(merged, wo_s[...])
        if not final:
            out_refs[0][...] = xn
        else:
            y = xn * lax.rsqrt(jnp.mean(xn * xn, axis=-1, keepdims=True) + EPS) * nf_ref[...]

            @pl.when(t < N_CTX_TILES)
            def _():
                out_refs[0][...] = y

            @pl.when(t >= N_CTX_TILES)
            def _():
                out_refs[1][...] = y


def _merge(final, xa, xb, xb_off, mod3, l, h, z_ctx, za_lat, y_f, y_b, zc_lat, zd_lat,
           w_mg, b_mg, w_bo, w_o, norm_f):
    assert TM == NA_HALF_Q
    row = lambda i: (_tile(i), 0)
    ctx_blk = lambda i: jnp.minimum(_tile(i), N_CTX_TILES - 1)
    lat_blk = lambda i: jnp.maximum(_tile(i) - N_CTX_TILES, 0)
    lat_spec = pl.BlockSpec((TM, BR_W), lambda i: (lat_blk(i), 0))
    half_spec = pl.BlockSpec((TM, BR_W), lambda i: (lat_blk(i) // LAT_TILES_PER_SEQ, 0))
    if final:
        out_specs = [pl.BlockSpec((TM, D_MODEL), lambda i: (ctx_blk(i), 0)),
                     pl.BlockSpec((TM, D_MODEL), lambda i: (lat_blk(i), 0))]
        out_shape = [jax.ShapeDtypeStruct((N_CTX, D_MODEL), F32), jax.ShapeDtypeStruct((N_LAT, D_MODEL), F32)]
    else:
        out_specs = pl.BlockSpec((TM, D_MODEL), row)
        out_shape = jax.ShapeDtypeStruct((N_TOK, D_MODEL), F32)
    wchunk = lambda width: pl.BlockSpec((1, W_ROWS, width), lambda i: (l, _wchunk(i), 0))
    return pl.pallas_call(
        functools.partial(_merge_kernel, l, final),
        grid=(W_STEPS + N_TILES,),
        in_specs=[
            pl.BlockSpec((TM, D_MODEL), lambda i: (ctx_blk(i), 0)),
            pl.BlockSpec((TM, D_MODEL), lambda i: (lat_blk(i) + xb_off, 0)),
            pl.BlockSpec((1, 1, 3 * D_MODEL), lambda i: (l * MOD_ROWS + _mod_row(_tile(i)), 0, 0)),
            pl.BlockSpec((TM, D_MODEL), row),
            pl.BlockSpec((TM, N_BRANCH * BR_W), lambda i: (ctx_blk(i), 0)),
            lat_spec,
            lat_spec, lat_spec,
            half_spec, half_spec,
            lat_spec,
            pl.BlockSpec(memory_space=pl.ANY),
            pl.BlockSpec((1, N_BRANCH, D_MODEL), lambda i: (l, 0, 0)),
            wchunk(D_MODEL),
            wchunk(D_MODEL),
            pl.BlockSpec((1, D_MODEL), lambda i: (0, 0)),
        ],
        out_specs=out_specs,
        out_shape=out_shape,
        scratch_shapes=[pltpu.VMEM((D_MODEL, N_BRANCH * D_MODEL), BF16),
                        pltpu.VMEM((N_BRANCH * BR_W, D_MODEL), BF16),
                        pltpu.VMEM((D_MODEL, D_MODEL), BF16),
                        pltpu.VMEM((2, N_BRANCH, W_ROWS, D_MODEL), F32),
                        pltpu.SemaphoreType.DMA((2, N_BRANCH))],
        compiler_params=_cparams(("arbitrary",)),
        name=f"merge_l{l}",
    )(xa, xb, mod3, h, z_ctx, za_lat, y_f, y_b, zc_lat[0], zc_lat[1], zd_lat,
      w_mg, b_mg,
      w_bo.reshape(DEPTH, N_BRANCH * BR_W, D_MODEL), w_o, norm_f.reshape(1, D_MODEL))


def kernel(x_prompt, x_sample, cache_diff_k, cache_diff_v, cache_na_k, cache_na_v, cache_swa_k, cache_swa_v,
           state_lru, c, c_ctx, norm_g, w_ada, b_ada, w_in, diff_lambda, diff_norm_g, conv_w, conv_b,
           lru_wa, lru_ba, lru_wx, lru_bx, lru_lam, na_rpb, swa_sink, w_mg, b_mg, w_bo, w_o, norm_f):
    tab_a, tab_d = jnp.asarray(_rope_tables(A_QK)), jnp.asarray(_rope_tables(HD))

    cvecs = jnp.concatenate([c_ctx[None, :], c, jnp.zeros((MOD_ROWS - 1 - DEC_BATCH, D_MODEL), F32)], axis=0)
    mod3 = _modulation(cvecs, w_ada, b_ada).reshape(DEPTH * MOD_ROWS, 1, 3 * D_MODEL)
    bias = _na_bias(na_rpb)
    norm_g3 = norm_g.reshape(DEPTH, 1, D_MODEL)
    dg3 = diff_norm_g.reshape(DEPTH, 1, HD)
    dg4 = jnp.tile(diff_norm_g, (1, A_HEADS)).reshape(DEPTH, 1, BR_W)
    sink3 = swa_sink.reshape(DEPTH, 1, D_HEADS)
    conv_b3 = conv_b.reshape(DEPTH, 1, BR_W)
    past = [jnp.swapaxes(t, -1, -2) for t in
            (cache_diff_k, cache_diff_v, cache_na_k, cache_na_v, cache_swa_k, cache_swa_v)]

    xa = x_prompt.reshape(N_CTX, D_MODEL)
    xb = x_sample.reshape(N_LAT, D_MODEL)
    xb_off = 0
    caches = states = None
    y_p = y_s = None
    for l in range(DEPTH):
        h, proj, bx, caches = _inproj(xa, xb, xb_off, mod3, l, norm_g3, w_in, tab_a, tab_d, caches)

        lru_params = (conv_w, conv_b3, lru_wa, lru_ba, lru_wx, lru_bx, lru_lam)
        z_ctx, states = _ctx_attn(proj, bx, l, diff_lambda, dg4, sink3, lru_params, states)
        za_lat = _lat_diff(proj, past[0], past[1], l, diff_lambda, dg3)
        zc0, y_f = _lat_na(0, proj, bx, past[2], past[3], bias, l, lru_params, state_lru)
        zc1, y_b = _lat_na(1, proj, bx, past[2], past[3], bias, l, lru_params, state_lru)
        zd_lat = _lat_swa(proj, past[4], past[5], sink3, l)

        final = l == DEPTH - 1
        out = _merge(final, xa, xb, xb_off, mod3, l, h, z_ctx, za_lat, y_f, y_b, (zc0, zc1), zd_lat,
                     w_mg, b_mg, w_bo, w_o, norm_f)
        if final:
            y_p, y_s = out
        else:
            xa = xb = out
            xb_off = N_CTX_TILES

    new_caches = [jnp.swapaxes(t, -1, -2) for t in caches]
    return (y_p.reshape(BATCH, SEQ, D_MODEL), y_s.reshape(DEC_BATCH, DEC_SEQ, D_MODEL), *new_caches, states)
```

```python
import functools
import math

import numpy as np
import jax
import jax.numpy as jnp
from jax import lax
from jax.experimental import pallas as pl
from jax.experimental.pallas import tpu as pltpu

F32 = jnp.float32
BF16 = jnp.bfloat16

D_MODEL = 1024
BATCH = 16
SEQ = 256
DEPTH = 2
DEC_BATCH = 8
DEC_SEQ = 1024
PAST_LEN = 512
GRID_W = 64
N_BRANCH = 4
BR_W = D_MODEL // 4
HD = 64
A_HEADS = BR_W // HD
A_QK = HD // 2
B_BLOCKS = 4
B_BLK = BR_W // B_BLOCKS
CONV_W = 4
LRU_C = 8.0
C_HEADS = BR_W // HD
NA_ROWS = 8
NA_COLS = 16
D_HEADS = BR_W // HD
D_KV = 2
D_GROUP = D_HEADS // D_KV
WIN = 128
ROPE_BASE = 10000.0
EPS = 1e-6
NEG = -1e30

LANES = 128
SUBLANES = 8
V7X_VMEM_BYTES = 64 * 1024 * 1024

N_CTX = BATCH * SEQ
N_LAT = DEC_BATCH * DEC_SEQ
N_TOK = N_CTX + N_LAT
GRID_ROWS = DEC_SEQ // GRID_W

TM = 512
N_CTX_TILES = N_CTX // TM
N_TILES = N_TOK // TM
LAT_TILES_PER_SEQ = DEC_SEQ // TM

COL_G = 0
GATE_A, GATE_R, GATE_C, GATE_D = (COL_G // BR_W + k for k in range(N_BRANCH))
COL_AQ = COL_G + N_BRANCH * BR_W
COL_AK = COL_AQ + BR_W
COL_AV = COL_AK + BR_W
COL_CQ = COL_AV + BR_W
COL_CK = COL_CQ + BR_W
COL_CV = COL_CK + BR_W
COL_DQ = COL_CV + BR_W
COL_DK = COL_DQ + BR_W
COL_DV = COL_DK + BR_W
PROJ_W = COL_DV + BR_W
IN_W = (N_BRANCH + 3 + 1 + 3 + 1) * BR_W + 2 * D_KV * HD

assert GRID_ROWS == 2 * NA_ROWS
NA_HALF_Q = DEC_SEQ // 2
NA_KROWS = NA_ROWS + NA_ROWS // 2
NA_KSPAN = NA_KROWS * GRID_W
NA_HALF_KSTART = (GRID_ROWS - NA_KROWS) * GRID_W

SWA_QB = 256
SWA_SPAN = SWA_QB + 2 * WIN

LOG2E = math.log2(math.e)
QA_SCALE = A_QK ** -0.5 * LOG2E
QH_SCALE = HD ** -0.5 * LOG2E

VMEM_LIMIT = V7X_VMEM_BYTES - 8 * 1024 * 1024

MOD_ROWS = -(-(1 + DEC_BATCH) // SUBLANES) * SUBLANES
MOD_COLS = 6 * LANES


def _cparams(sem):
    return pltpu.CompilerParams(dimension_semantics=sem, vmem_limit_bytes=VMEM_LIMIT)


def _dot(a, b):
    return jnp.dot(a, b, preferred_element_type=F32)


def _dot_nt(a, b):
    return lax.dot_general(a, b, (((1,), (1,)), ((), ())), preferred_element_type=F32)


def _silu(x):
    return x * jax.nn.sigmoid(x)


def _rope_tables(d):
    half = d // 2
    quarter = half // 2
    lane = np.arange(LANES)
    q = lane % d
    use_col = (q >= half)
    i = (q % half) % quarter
    first = (q % half) < quarter
    inv = (ROPE_BASE ** (-np.arange(quarter, dtype=np.float32) / np.float32(quarter))).astype(np.float32)
    t = np.arange(DEC_SEQ)
    pos = np.where(use_col[None, :], (t % GRID_W)[:, None], (t // GRID_W)[:, None]).astype(np.float32)
    ang = (pos * inv[i][None, :]).astype(np.float32)
    cos, sin = np.cos(ang), np.sin(ang)
    c = np.concatenate([np.ones((TM, LANES), np.float32), cos], axis=0)
    s1 = np.concatenate([np.zeros((TM, LANES), np.float32), np.where(first[None, :], -sin, 0.0)], axis=0)
    s2 = np.concatenate([np.zeros((TM, LANES), np.float32), np.where(first[None, :], 0.0, sin)], axis=0)
    return np.stack([c, s1, s2]).astype(np.float32)


def _rope_lanes(x, tab_ref, shift):
    return (x * tab_ref[0] + pltpu.roll(x, LANES - shift, 1) * tab_ref[1]
            + pltpu.roll(x, shift, 1) * tab_ref[2])


def _rope(x, tab_ref, shift):
    w = x.shape[1]
    return jnp.concatenate([_rope_lanes(x[:, c:c + LANES], tab_ref, shift) for c in range(0, w, LANES)], axis=1)


def _mod_kernel(c_ref, w_ref, b_ref, o_ref):
    c = c_ref[...]
    o_ref[0] = _dot(_silu(c).astype(BF16), w_ref[0].astype(BF16)) + b_ref[0]


def _modulation(cvecs, w_ada, b_ada):
    nb = 3 * D_MODEL // MOD_COLS
    return pl.pallas_call(
        _mod_kernel,
        grid=(DEPTH, nb),
        in_specs=[pl.BlockSpec((MOD_ROWS, D_MODEL), lambda l, j: (0, 0)),
                  pl.BlockSpec((1, D_MODEL, MOD_COLS), lambda l, j: (l, 0, j)),
                  pl.BlockSpec((1, 1, MOD_COLS), lambda l, j: (l, 0, j))],
        out_specs=pl.BlockSpec((1, MOD_ROWS, MOD_COLS), lambda l, j: (l, 0, j)),
        out_shape=jax.ShapeDtypeStruct((DEPTH, MOD_ROWS, 3 * D_MODEL), F32),
        compiler_params=_cparams(("arbitrary", "arbitrary")),
        name="modulation",
    )(cvecs, w_ada, b_ada.reshape(DEPTH, 1, 3 * D_MODEL))


W_STEPS = 4
W_ROWS = D_MODEL // W_STEPS


def _tile(i):
    return jnp.maximum(i - W_STEPS, 0)


def _wchunk(i):
    return jnp.minimum(i, W_STEPS - 1)


def _mod_row(t):
    return jnp.where(t < N_CTX_TILES, 0, 1 + (t - N_CTX_TILES) // LAT_TILES_PER_SEQ)


def _rope_blk(t):
    return jnp.where(t < N_CTX_TILES, 0, 1 + (t - N_CTX_TILES) % LAT_TILES_PER_SEQ)


def _load_x(t, xa_ref, xb_ref):
    return jnp.where(t < N_CTX_TILES, xa_ref[...], xb_ref[...])


def _cast_rows(i, src_ref, dst_ref):
    r0 = pl.multiple_of(i * W_ROWS, W_ROWS)
    dst_ref[pl.ds(r0, W_ROWS), :] = src_ref[0].astype(BF16)


WI_G = 0
WI_AQ = WI_G + N_BRANCH * BR_W
WI_AK, WI_AV, WI_BX, WI_CQ, WI_CKV = (WI_AQ + k * BR_W for k in range(1, 6))
WI_DQ = WI_CKV + 2 * BR_W
WI_DKV = WI_DQ + BR_W
SEQ_PER_TILE = TM // SEQ
IN_ROWS = 256


def _store_heads(ref, val, n_heads, width):
    vt = val.T
    for s in range(SEQ_PER_TILE):
        for h in range(n_heads):
            ref[s, 0, h] = vt[h * width:(h + 1) * width, s * SEQ:(s + 1) * SEQ]


def _per_query_head(x):
    return jnp.concatenate([x[:, kv * HD:(kv + 1) * HD] for kv in range(D_KV) for _ in range(D_GROUP)], axis=1)


def _inproj_kernel(n_alias, xa_ref, xb_ref, mod_ref, g_ref, w_ref, ta_ref, td_ref, *refs):
    h_ref, proj_ref, bx_ref, dk_ref, dv_ref, nk_ref, nv_ref, sk_ref, sv_ref, w_s = refs[n_alias:]
    i = pl.program_id(0)

    @pl.when(i < W_STEPS)
    def _():
        _cast_rows(i, w_ref, w_s)

    @pl.when(i >= W_STEPS)
    def _():
        t = i - W_STEPS
        is_ctx = t < N_CTX_TILES
        mod = mod_ref[0]
        shift, scale = mod[:, :D_MODEL], mod[:, D_MODEL:2 * D_MODEL]
        w2 = 2 * BR_W
        kept = []
        for r in range(0, TM, IN_ROWS):
            rs = pl.ds(r, IN_ROWS)
            ta, td = ta_ref.at[:, rs, :], td_ref.at[:, rs, :]
            x = jnp.where(is_ctx, xa_ref[rs, :], xb_ref[rs, :])
            xn = x * lax.rsqrt(jnp.mean(x * x, axis=-1, keepdims=True) + EPS) * g_ref[0]
            hb = (xn * (1.0 + scale) + shift).astype(BF16)
            h_ref[rs, :] = hb

            def mm(c0, width):
                return _dot(hb, w_s[:, c0:c0 + width])

            gs = [mm(WI_G + c, w2) for c in range(0, N_BRANCH * BR_W, w2)]
            aq, ak, av = mm(WI_AQ, BR_W), mm(WI_AK, BR_W), mm(WI_AV, BR_W)
            cq, ckv = mm(WI_CQ, BR_W), mm(WI_CKV, w2)
            dq, dkv = mm(WI_DQ, BR_W), mm(WI_DKV, 2 * D_KV * HD)
            dk, dv = _rope(dkv[:, :D_KV * HD], td, HD // 4), dkv[:, D_KV * HD:]
            bx_ref[rs, :] = mm(WI_BX, BR_W)

            for n, g in enumerate(gs):
                proj_ref[rs, COL_G + n * w2:COL_G + (n + 1) * w2] = _silu(g).astype(BF16)
            proj_ref[rs, COL_AQ:COL_AQ + BR_W] = _rope(aq * QA_SCALE, ta, A_QK // 4).astype(BF16)
            proj_ref[rs, COL_AK:COL_AK + BR_W] = _rope(ak, ta, A_QK // 4).astype(BF16)
            proj_ref[rs, COL_AV:COL_AV + BR_W] = av.astype(BF16)
            proj_ref[rs, COL_CQ:COL_CQ + BR_W] = (cq * QH_SCALE).astype(BF16)
            proj_ref[rs, COL_CK:COL_CK + w2] = ckv.astype(BF16)
            proj_ref[rs, COL_DQ:COL_DQ + BR_W] = _rope(dq * QH_SCALE, td, HD // 4).astype(BF16)
            proj_ref[rs, COL_DK:COL_DK + BR_W] = _per_query_head(dk.astype(BF16))
            proj_ref[rs, COL_DV:COL_DV + BR_W] = _per_query_head(dv.astype(BF16))
            kept.append((ak, av, ckv, dk, dv))

        @pl.when(is_ctx)
        def _():
            ak, av, ckv, dk, dv = (jnp.concatenate(v, axis=0) for v in zip(*kept))
            akt = ak.T
            for s in range(SEQ_PER_TILE):
                for h in range(A_HEADS):
                    for m in range(2):
                        c0 = h * HD + m * A_QK
                        dk_ref[s, 0, h, m] = akt[c0:c0 + A_QK, s * SEQ:(s + 1) * SEQ]
            _store_heads(dv_ref, av, A_HEADS, HD)
            _store_heads(nk_ref, ckv[:, :BR_W], C_HEADS, HD)
            _store_heads(nv_ref, ckv[:, BR_W:], C_HEADS, HD)
            _store_heads(sk_ref, dk, D_KV, HD)
            _store_heads(sv_ref, dv, D_KV, HD)


def _cache_shapes():
    hs = lambda n: (BATCH, DEPTH, n, HD, SEQ)
    return [(BATCH, DEPTH, A_HEADS, 2, A_QK, SEQ), hs(A_HEADS), hs(C_HEADS), hs(C_HEADS), hs(D_KV), hs(D_KV)]


def _inproj(xa, xb, xb_off, mod3, l, norm_g, w_in, tab_a, tab_d, caches):
    row = lambda i: (_tile(i), 0)
    ctx_blk = lambda i: jnp.minimum(_tile(i), N_CTX_TILES - 1)

    def cache_spec(shape):
        blk = (SEQ_PER_TILE, 1) + shape[2:]
        nz = len(shape) - 2
        return pl.BlockSpec(blk, lambda i: (ctx_blk(i), l) + (0,) * nz)

    cshapes = _cache_shapes()
    aliases = {} if caches is None else {7 + k: 3 + k for k in range(6)}
    extra_specs = [] if caches is None else [pl.BlockSpec(memory_space=pl.ANY)] * 6
    extra_args = [] if caches is None else list(caches)
    outs = pl.pallas_call(
        functools.partial(_inproj_kernel, len(extra_args)),
        grid=(W_STEPS + N_TILES,),
        in_specs=[
            pl.BlockSpec((TM, D_MODEL), lambda i: (ctx_blk(i), 0)),
            pl.BlockSpec((TM, D_MODEL), lambda i: (jnp.maximum(_tile(i) - N_CTX_TILES, 0) + xb_off, 0)),
            pl.BlockSpec((1, 1, 3 * D_MODEL), lambda i: (l * MOD_ROWS + _mod_row(_tile(i)), 0, 0)),
            pl.BlockSpec((1, 1, D_MODEL), lambda i: (l, 0, 0)),
            pl.BlockSpec((1, W_ROWS, IN_W), lambda i: (l, _wchunk(i), 0)),
            pl.BlockSpec((3, TM, LANES), lambda i: (0, _rope_blk(_tile(i)), 0)),
            pl.BlockSpec((3, TM, LANES), lambda i: (0, _rope_blk(_tile(i)), 0)),
        ] + extra_specs,
        out_specs=[
            pl.BlockSpec((TM, D_MODEL), row),
            pl.BlockSpec((TM, PROJ_W), row),
            pl.BlockSpec((TM, BR_W), row),
        ] + [cache_spec(s) for s in cshapes],
        out_shape=[
            jax.ShapeDtypeStruct((N_TOK, D_MODEL), BF16),
            jax.ShapeDtypeStruct((N_TOK, PROJ_W), BF16),
            jax.ShapeDtypeStruct((N_TOK, BR_W), F32),
        ] + [jax.ShapeDtypeStruct(s, F32) for s in cshapes],
        scratch_shapes=[pltpu.VMEM((D_MODEL, IN_W), BF16)],
        input_output_aliases=aliases,
        compiler_params=_cparams(("arbitrary",)),
        name=f"inproj_l{l}",
    )(xa, xb, mod3, norm_g, w_in, tab_a, tab_d, *extra_args)
    return outs[0], outs[1], outs[2], tuple(outs[3:])


def _diff_lambda(lam_ref, lam_init):
    lv = lam_ref[0]
    s1 = jnp.sum(lv[0:1] * lv[1:2], axis=-1, keepdims=True)
    s2 = jnp.sum(lv[2:3] * lv[3:4], axis=-1, keepdims=True)
    return jnp.exp(s1) - jnp.exp(s2) + lam_init


def _diff_norm(o, dg_ref, lam_init):
    y = o * lax.rsqrt(jnp.mean(o * o, axis=-1, keepdims=True) + EPS) * dg_ref[0]
    return y * (1.0 - lam_init)


def _exp2_parts(parts, extra=None):
    m = functools.reduce(jnp.maximum, [jnp.max(s, axis=-1, keepdims=True) for s in parts])
    if extra is not None:
        m = jnp.maximum(m, extra)
    return [jnp.exp2(s - m).astype(BF16) for s in parts], m


def _pipelined(n, score_fn, pv_fn):
    outs = []
    nxt = score_fn(0)
    for c in range(n):
        cur, nxt = nxt, (score_fn(c + 1) if c + 1 < n else None)
        outs.append(pv_fn(c, cur))
    return outs


def _with_ones(v):
    return jnp.concatenate([v, jnp.ones_like(v)], axis=1)


def _with_ones_t(vt):
    return jnp.concatenate([vt, jnp.ones((2 * SUBLANES, vt.shape[1]), vt.dtype)], axis=0)


def _sink_col(sink2, kv, rows):
    r = lax.broadcasted_iota(jnp.int32, (D_GROUP * rows, 1), 0)
    return jnp.where(r < rows, sink2[:, kv * D_GROUP:kv * D_GROUP + 1], sink2[:, kv * D_GROUP + 1:kv * D_GROUP + 2])


CTX_PER_STEP = 2


def _lane_groups(n, rows):
    grp = lax.broadcasted_iota(jnp.int32, (rows, BR_W), 1) // (BR_W // n)
    return [grp == p for p in range(n)]


def _only(x, m):
    return jnp.where(m, x, jnp.zeros_like(x))


def _ones_outside(v, m):
    return jnp.where(m, v, jnp.ones_like(v))


def _swap_halves(p):
    return jnp.concatenate([p[:, BR_W // 2:], p[:, :BR_W // 2]], axis=1)


def _pick_heads(os):
    grp = lax.broadcasted_iota(jnp.int32, os[0].shape, 1) // HD
    acc = os[0]
    for h in range(1, len(os)):
        acc = jnp.where(grp == h, os[h], acc)
    return acc


def _head_rms(o, dg4_ref, lam_init):
    r = lax.broadcasted_iota(jnp.int32, (BR_W, BR_W), 0) // HD
    c = lax.broadcasted_iota(jnp.int32, (BR_W, BR_W), 1) // HD
    ss = _dot((o * o).astype(BF16), jnp.where(r == c, 1.0, 0.0).astype(BF16))
    return o * lax.rsqrt(ss * (1.0 / HD) + EPS) * dg4_ref[0] * (1.0 - lam_init)


def _ctx_attn_kernel(n_alias, lam_init, aq_ref, ak_ref, av_ref, cq_ref, ck_ref, cv_ref, dq_ref, dk_ref, dv_ref,
                     ga_ref, gc_ref, gd_ref, lam_ref, dg4_ref, sink_ref,
                     bx_ref, gr_ref, cw_ref, cb_ref, wa_ref, ba_ref, wx_ref, bxg_ref, llam_ref, *refs):
    z_ref, fin_ref = refs[n_alias:]
    seqs = [slice(s * SEQ, (s + 1) * SEQ) for s in range(CTX_PER_STEP)]

    xcs = [_lru_conv(bx_ref[rs, :], cw_ref[0], cb_ref[0]) for rs in seqs]
    xc_all = jnp.concatenate(xcs, axis=0)
    pres = [_lru_gates(d, xc_all, wa_ref, ba_ref, wx_ref, bxg_ref) for d in range(2)]
    for s, rs in enumerate(seqs):
        h0 = jnp.zeros((1, BR_W), F32)
        yf, hf = _lru_scan(0, xcs[s], [p[rs] for p in pres[0]], llam_ref[0, 0:1, :], h0)
        yb, hb = _lru_scan(1, xcs[s], [p[rs] for p in pres[1]], llam_ref[0, 1:2, :], h0)
        z_ref[rs, BR_W:2 * BR_W] = ((yf + yb) * gr_ref[rs, :].astype(F32)).astype(BF16)
        fin_ref[s, 0, 0:1, :] = hf
        fin_ref[s, 0, 1:2, :] = hb

    lam = _diff_lambda(lam_ref, lam_init)
    sink2 = sink_ref[0] * LOG2E
    m_qk = _lane_groups(2 * A_HEADS, SEQ)
    m_hd = _lane_groups(C_HEADS, SEQ)

    sa, sc, sd = [], [], []
    for rs in seqs:
        aq, cq, dq = aq_ref[rs, :], cq_ref[rs, :], dq_ref[rs, :]
        ak, ck, dk = ak_ref[rs, :], ck_ref[rs, :], dk_ref[rs, :]
        sa.append([_dot_nt(_only(aq, m), ak) for m in m_qk])
        sc.append([_dot_nt(_only(cq, m), ck) for m in m_hd])
        sd.append([_dot_nt(_only(dq, m), dk) for m in m_hd])

    ea =[[_exp2_parts([s])[0][0] for s in ss] for ss in sa]
    ec = [[_exp2_parts([s])[0][0] for s in ss] for ss in sc]
    ed = [[_exp2_parts([s], extra=sink2[:, j:j + 1]) for j, s in enumerate(ss)] for ss in sd]

    for i, rs in enumerate(seqs):
        av, cv, dv = av_ref[rs, :], cv_ref[rs, :], dv_ref[rs, :]
        os = []
        for h in range(A_HEADS):
            w = _ones_outside(av, m_hd[h])
            p0, p1 = _dot(ea[i][2 * h], w), _dot(ea[i][2 * h + 1], w)
            os.append(p0 * (1.0 / _swap_halves(p0)) - p1 * (lam / _swap_halves(p1)))
        z_ref[rs, 0:BR_W] = (_head_rms(_pick_heads(os), dg4_ref, lam_init) * ga_ref[rs, :].astype(F32)).astype(BF16)

        os = []
        for h in range(C_HEADS):
            p = _dot(ec[i][h], _ones_outside(cv, m_hd[h]))
            os.append(p * (1.0 / _swap_halves(p)))
        z_ref[rs, 2 * BR_W:3 * BR_W] = (_pick_heads(os) * gc_ref[rs, :].astype(F32)).astype(BF16)

        os = []
        for j in range(D_HEADS):
            (e,), m = ed[i][j]
            p = _dot(e, _ones_outside(dv, m_hd[j]))
            os.append(p * (1.0 / (_swap_halves(p) + jnp.exp2(sink2[:, j:j + 1] - m))))
        z_ref[rs, 3 * BR_W:4 * BR_W] = (_pick_heads(os) * gd_ref[rs, :].astype(F32)).astype(BF16)


def _ctx_attn(proj, bx, l, diff_lambda, dg4, sink3, lru_params, states):
    lam_init = 0.8 - 0.6 * math.exp(-0.3 * l)
    colblk = lambda cb: pl.BlockSpec((CTX_PER_STEP * SEQ, BR_W), lambda b: (b, cb))
    extra = [] if states is None else [states]
    return pl.pallas_call(
        functools.partial(_ctx_attn_kernel, len(extra), lam_init),
        grid=(BATCH // CTX_PER_STEP,),
        in_specs=[colblk(COL_AQ // BR_W), colblk(COL_AK // BR_W), colblk(COL_AV // BR_W),
                  colblk(COL_CQ // BR_W), colblk(COL_CK // BR_W), colblk(COL_CV // BR_W),
                  colblk(COL_DQ // BR_W), colblk(COL_DK // BR_W), colblk(COL_DV // BR_W),
                  colblk(GATE_A), colblk(GATE_C), colblk(GATE_D),
                  pl.BlockSpec((1, 4, A_QK), lambda b: (l, 0, 0)),
                  pl.BlockSpec((1, 1, BR_W), lambda b: (l, 0, 0)),
                  pl.BlockSpec((1, 1, D_HEADS), lambda b: (l, 0, 0)),
                  pl.BlockSpec((CTX_PER_STEP * SEQ, BR_W), lambda b: (b, 0)), colblk(GATE_R)]
                 + _lru_specs(l, lambda b: (l, 0, 0)) + [pl.BlockSpec(memory_space=pl.ANY)] * len(extra),
        out_specs=[pl.BlockSpec((CTX_PER_STEP * SEQ, N_BRANCH * BR_W), lambda b: (b, 0)),
                   pl.BlockSpec((CTX_PER_STEP, 1, 2, BR_W), lambda b: (b, l, 0, 0))],
        out_shape=[jax.ShapeDtypeStruct((N_CTX, N_BRANCH * BR_W), BF16),
                   jax.ShapeDtypeStruct((BATCH, DEPTH, 2, BR_W), F32)],
        input_output_aliases={} if states is None else {24: 1},
        compiler_params=_cparams(("arbitrary",)),
        name=f"ctx_attn_l{l}",
    )(*([proj] * 12), diff_lambda, dg4, sink3, bx, proj, *lru_params, *extra)


LA_QB = 256


def _lat_diff_kernel(lam_init, aq_ref, ak_ref, av_ref, ck_ref, cv_ref, ga_ref, lam_ref, dg_ref, za_ref):
    lam = _diff_lambda(lam_ref, lam_init)
    aq = aq_ref[...]
    va_ctx = [_with_ones_t(cv_ref[0, 0, h].astype(BF16)) for h in range(A_HEADS)]
    va_loc = [_with_ones(av_ref[:, h * HD:(h + 1) * HD]) for h in range(A_HEADS)]

    def scores(c):
        h, m = divmod(c, 2)
        lo = h * HD + m * A_QK
        q = aq[:, lo:lo + A_QK]
        return [_dot(q, ck_ref[0, 0, h, m].astype(BF16)),
                _dot_nt(q, ak_ref[:, lo:lo + A_QK])]

    def pv(c, s):
        (e_ctx, e_loc), _ = _exp2_parts(s)
        p_ctx, p_loc = _dot_nt(e_ctx, va_ctx[c // 2]), _dot(e_loc, va_loc[c // 2])
        return p_ctx[:, :HD] + p_loc[:, :HD], p_ctx[:, HD:HD + 1] + p_loc[:, HD:HD + 1]

    ss = [scores(c) for c in range(2 * A_HEADS)]
    nd = [pv(c, s) for c, s in enumerate(ss)]
    outs = []
    for h in range(A_HEADS):
        (n0, d0), (n1, d1) = nd[2 * h], nd[2 * h + 1]
        outs.append(_diff_norm(n0 * (1.0 / d0) - n1 * (lam / d1), dg_ref, lam_init))
    za_ref[...] = (jnp.concatenate(outs, axis=1) * ga_ref[...].astype(F32)).astype(BF16)


def _lat_diff(proj, cache_k, cache_v, l, diff_lambda, dg3):
    lam_init = 0.8 - 0.6 * math.exp(-0.3 * l)
    nq = DEC_SEQ // LA_QB
    qrow = lambda b, j: N_CTX // LA_QB + b * nq + j
    srow = lambda b, j: N_CTX // DEC_SEQ + b
    return pl.pallas_call(
        functools.partial(_lat_diff_kernel, lam_init),
        grid=(DEC_BATCH, nq),
        in_specs=[pl.BlockSpec((LA_QB, BR_W), lambda b, j: (qrow(b, j), COL_AQ // BR_W)),
                  pl.BlockSpec((DEC_SEQ, BR_W), lambda b, j: (srow(b, j), COL_AK // BR_W)),
                  pl.BlockSpec((DEC_SEQ, BR_W), lambda b, j: (srow(b, j), COL_AV // BR_W)),
                  pl.BlockSpec((1, 1, A_HEADS, 2, A_QK, PAST_LEN), lambda b, j: (b, l, 0, 0, 0, 0)),
                  pl.BlockSpec((1, 1, A_HEADS, HD, PAST_LEN), lambda b, j: (b, l, 0, 0, 0)),
                  pl.BlockSpec((LA_QB, BR_W), lambda b, j: (qrow(b, j), 0)),
                  pl.BlockSpec((1, 4, A_QK), lambda b, j: (l, 0, 0)),
                  pl.BlockSpec((1, 1, HD), lambda b, j: (l, 0, 0))],
        out_specs=pl.BlockSpec((LA_QB, BR_W), lambda b, j: (b * nq + j, 0)),
        out_shape=jax.ShapeDtypeStruct((N_LAT, BR_W), BF16),
        compiler_params=_cparams(("arbitrary", "arbitrary")),
        name=f"lat_diff_l{l}",
    )(proj, proj, proj, cache_k, cache_v, proj, diff_lambda, dg3)


def _na_bias_kernel(rpb_ref, o_ref):
    l, h = pl.program_id(0), pl.program_id(1)
    base = (l * C_HEADS + h) * ((2 * NA_ROWS - 1) * (2 * NA_COLS - 1))
    qc = lax.broadcasted_iota(jnp.int32, (GRID_W, GRID_W), 0)
    kc = lax.broadcasted_iota(jnp.int32, (GRID_W, GRID_W), 1)
    cs = jnp.clip(qc - NA_COLS // 2, 0, GRID_W - NA_COLS)
    ok = (kc >= cs) & (kc < cs + NA_COLS)
    dcol = kc - qc + (NA_COLS - 1)
    neg = jnp.full((GRID_W, GRID_W), NEG, F32)
    tabs = []
    for d in range(2 * NA_ROWS - 1):
        t = neg
        for dc in range(2 * NA_COLS - 1):
            t = jnp.where(ok & (dcol == dc), rpb_ref[base + d * (2 * NA_COLS - 1) + dc] * LOG2E, t)
        tabs.append(t)
    for half in range(2):
        for rq in range(NA_ROWS):
            r = half * NA_ROWS + rq
            rs = min(max(r - NA_ROWS // 2, 0), GRID_ROWS - NA_ROWS)
            blks = []
            for kr in range(NA_KROWS):
                kabs = half * (NA_HALF_KSTART // GRID_W) + kr
                blks.append(tabs[kabs - r + NA_ROWS - 1] if rs <= kabs < rs + NA_ROWS else neg)
            o_ref[0, 0, half, rq * GRID_W:(rq + 1) * GRID_W, :] = jnp.concatenate(blks, axis=1)


def _na_bias(na_rpb):
    return pl.pallas_call(
        _na_bias_kernel,
        grid=(DEPTH, C_HEADS),
        in_specs=[pl.BlockSpec(memory_space=pltpu.SMEM)],
        out_specs=pl.BlockSpec((1, 1, 2, NA_HALF_Q, NA_KSPAN), lambda l, h: (l, h, 0, 0, 0)),
        out_shape=jax.ShapeDtypeStruct((DEPTH, C_HEADS, 2, NA_HALF_Q, NA_KSPAN), F32),
        compiler_params=_cparams(("arbitrary", "arbitrary")),
        name="na_bias",
    )(na_rpb.reshape(-1))


def _lat_na_kernel(half, cq_ref, ck_ref, cv_ref, kc_ref, vc_ref, bias_ref, gc_ref,
                   bx_ref, gr_ref, cw_ref, cb_ref, wa_ref, ba_ref, wx_ref, bxg_ref, llam_ref, st_ref, zc_ref, y_ref):
    k0 = half * NA_HALF_KSTART
    cq = cq_ref[...]
    kl_all = ck_ref[k0:k0 + NA_KSPAN, :]
    vl_all = cv_ref[k0:k0 + NA_KSPAN, :]

    def scores(h):
        q = cq[:, h * HD:(h + 1) * HD]
        return [_dot(q, kc_ref[0, 0, h].astype(BF16)),
                _dot_nt(q, kl_all[:, h * HD:(h + 1) * HD]) + bias_ref[0, h, 0]]

    def pv(h, s):
        (e_ctx, e_loc), _ = _exp2_parts(s)
        p_ctx = _dot_nt(e_ctx, _with_ones_t(vc_ref[0, 0, h].astype(BF16)))
        p_loc = _dot(e_loc, _with_ones(vl_all[:, h * HD:(h + 1) * HD]))
        return (p_ctx[:, :HD] + p_loc[:, :HD]) * (1.0 / (p_ctx[:, HD:HD + 1] + p_loc[:, HD:HD + 1]))

    xc = _lru_conv(bx_ref[...], cw_ref[0], cb_ref[0])
    pre = _lru_gates(half, xc, wa_ref, ba_ref, wx_ref, bxg_ref)
    y, _ = _lru_scan(half, xc, pre, llam_ref[0, half:half + 1, :], st_ref[0, 0, half:half + 1, :])
    y_ref[...] = y * gr_ref[...].astype(F32)

    ss = [scores(h) for h in range(C_HEADS)]
    outs = [pv(h, s) for h, s in enumerate(ss)]
    zc_ref[...] = (jnp.concatenate(outs, axis=1) * gc_ref[...].astype(F32)).astype(BF16)


def _lat_na(half, proj, bx, cache_k, cache_v, bias, l, lru_params, state_lru):
    qrow = lambda b: N_CTX // NA_HALF_Q + b * 2 + half
    srow = lambda b: N_CTX // DEC_SEQ + b
    return pl.pallas_call(
        functools.partial(_lat_na_kernel, half),
        grid=(DEC_BATCH,),
        in_specs=[pl.BlockSpec((NA_HALF_Q, BR_W), lambda b: (qrow(b), COL_CQ // BR_W)),
                  pl.BlockSpec((DEC_SEQ, BR_W), lambda b: (srow(b), COL_CK // BR_W)),
                  pl.BlockSpec((DEC_SEQ, BR_W), lambda b: (srow(b), COL_CV // BR_W)),
                  pl.BlockSpec((1, 1, C_HEADS, HD, PAST_LEN), lambda b: (b, l, 0, 0, 0)),
                  pl.BlockSpec((1, 1, C_HEADS, HD, PAST_LEN), lambda b: (b, l, 0, 0, 0)),
                  pl.BlockSpec((1, C_HEADS, 1, NA_HALF_Q, NA_KSPAN), lambda b: (l, 0, half, 0, 0)),
                  pl.BlockSpec((NA_HALF_Q, BR_W), lambda b: (qrow(b), GATE_C)),
                  pl.BlockSpec((DEC_SEQ, BR_W), lambda b: (srow(b), 0)),
                  pl.BlockSpec((DEC_SEQ, BR_W), lambda b: (srow(b), GATE_R))]
                 + _lru_specs(l, lambda b: (l, 0, 0))
                 + [pl.BlockSpec((1, 1, 2, BR_W), lambda b: (b, l, 0, 0))],
        out_specs=[pl.BlockSpec((NA_HALF_Q, BR_W), lambda b: (b, 0)),
                   pl.BlockSpec((DEC_SEQ, BR_W), lambda b: (b, 0))],
        out_shape=[jax.ShapeDtypeStruct((N_LAT // 2, BR_W), BF16), jax.ShapeDtypeStruct((N_LAT, BR_W), F32)],
        compiler_params=_cparams(("arbitrary",)),
        name=f"lat_na{half}_l{l}",
    )(proj, proj, proj, cache_k, cache_v, bias, proj, bx, proj, *lru_params, state_lru)


def _lat_swa_kernel(dq_ref, dk_ref, dv_ref, kc_ref, vc_ref, sink_ref, gd_ref, zd_ref):
    nq = DEC_SEQ // SWA_QB
    rows = D_GROUP * SWA_QB
    dq, dk, dv = dq_ref[...], dk_ref[...], dv_ref[...]
    sink2 = sink_ref[0] * LOG2E
    m_hd = _lane_groups(D_HEADS, DEC_SEQ)
    starts = [min(max(j * SWA_QB - WIN, 0), DEC_SEQ - SWA_SPAN) for j in range(nq)]
    qi = lax.broadcasted_iota(jnp.int32, (rows, SWA_SPAN), 0) % SWA_QB
    ki = lax.broadcasted_iota(jnp.int32, (rows, SWA_SPAN), 1)
    masks = {off: jnp.abs(qi + off - ki) <= WIN for off in sorted({j * SWA_QB - s for j, s in enumerate(starts)})}
    kc_t = jnp.concatenate([kc_ref[0, 0, kv] for kv in range(D_KV) for _ in range(D_GROUP)], axis=0).astype(BF16)
    vc_t = jnp.concatenate([vc_ref[0, 0, kv] for kv in range(D_KV) for _ in range(D_GROUP)], axis=0).astype(BF16)
    feat = lax.broadcasted_iota(jnp.int32, (BR_W, 1), 0) // (D_GROUP * HD)

    qm = [_only(dq, m) for m in m_hd]
    sks = [_sink_col(sink2, kv, SWA_QB) for kv in range(D_KV)]
    w_loc, w_ctx = [], []
    for kv in range(D_KV):
        w_loc.append(_ones_outside(dv, m_hd[kv * D_GROUP] | m_hd[kv * D_GROUP + 1]))
        w_ctx.append(jnp.where(feat == kv, vc_t, jnp.ones_like(vc_t)))

    def scores(c):
        kv, j = divmod(c, nq)
        q2 = jnp.concatenate([qm[kv * D_GROUP + g][j * SWA_QB:(j + 1) * SWA_QB] for g in range(D_GROUP)], axis=0)
        return [_dot(q2, kc_t),
                jnp.where(masks[j * SWA_QB - starts[j]], _dot_nt(q2, dk[starts[j]:starts[j] + SWA_SPAN]), NEG)]

    def pv(c, s):
        kv, j = divmod(c, nq)
        (e_ctx, e_loc), m = _exp2_parts(s, extra=sks[kv])
        p = _dot_nt(e_ctx, w_ctx[kv]) + _dot(e_loc, w_loc[kv][starts[j]:starts[j] + SWA_SPAN])
        return p * (1.0 / (_swap_halves(p) + jnp.exp2(sks[kv] - m)))

    o = _pipelined(D_KV * nq, scores, pv)
    os = [jnp.concatenate([o[kv * nq + j][g * SWA_QB:(g + 1) * SWA_QB] for j in range(nq)], axis=0)
          for kv in range(D_KV) for g in range(D_GROUP)]
    zd_ref[...] = (_pick_heads(os) * gd_ref[...].astype(F32)).astype(BF16)


def _lat_swa(proj, cache_k, cache_v, sink3, l):
    srow = lambda b: (N_CTX // DEC_SEQ + b)
    return pl.pallas_call(
        _lat_swa_kernel,
        grid=(DEC_BATCH,),
        in_specs=[pl.BlockSpec((DEC_SEQ, BR_W), lambda b: (srow(b), COL_DQ // BR_W)),
                  pl.BlockSpec((DEC_SEQ, BR_W), lambda b: (srow(b), COL_DK // BR_W)),
                  pl.BlockSpec((DEC_SEQ, BR_W), lambda b: (srow(b), COL_DV // BR_W)),
                  pl.BlockSpec((1, 1, D_KV, HD, PAST_LEN), lambda b: (b, l, 0, 0, 0)),
                  pl.BlockSpec((1, 1, D_KV, HD, PAST_LEN), lambda b: (b, l, 0, 0, 0)),
                  pl.BlockSpec((1, 1, D_HEADS), lambda b: (l, 0, 0)),
                  pl.BlockSpec((DEC_SEQ, BR_W), lambda b: (srow(b), GATE_D))],
        out_specs=pl.BlockSpec((DEC_SEQ, BR_W), lambda b: (b, 0)),
        out_shape=jax.ShapeDtypeStruct((N_LAT, BR_W), BF16),
        compiler_params=_cparams(("arbitrary",)),
        name=f"lat_swa_l{l}",
    )(proj, proj, proj, cache_k, cache_v, sink3, proj)


def _lru_conv(x, cw, cb):
    seq = x.shape[0]
    x3 = x.reshape(seq // SUBLANES, SUBLANES, BR_W)
    tin = lax.broadcasted_iota(jnp.int32, (1, SUBLANES, 1), 1)
    zero = jnp.zeros((1, SUBLANES, BR_W), F32)

    def at(shift):
        r = pltpu.roll(x3, (-shift) % SUBLANES, 1)
        if shift < 0:
            return jnp.where(tin < -shift, jnp.concatenate([zero, r[:-1]], axis=0), r)
        return jnp.where(tin >= SUBLANES - shift, jnp.concatenate([r[1:], zero], axis=0), r)

    xc = cb + at(-1) * cw[0:1] + x3 * cw[1:2] + at(1) * cw[2:3] + at(2) * cw[3:4]
    return xc.reshape(seq, BR_W)


def _gate_dense(w_ref, d):
    rows = []
    for n in range(B_BLOCKS):
        pieces = [jnp.zeros((B_BLK, n * B_BLK), BF16)] if n else []
        pieces.append(w_ref[0, d, n].astype(BF16))
        if n < B_BLOCKS - 1:
            pieces.append(jnp.zeros((B_BLK, (B_BLOCKS - 1 - n) * B_BLK), BF16))
        rows.append(jnp.concatenate(pieces, axis=1))
    return jnp.concatenate(rows, axis=0)


def _lru_gates(d, xc, wa_ref, ba_ref, wx_ref, bxg_ref):
    xb = xc.astype(BF16)
    return (_dot(xb, _gate_dense(wa_ref, d)) + ba_ref[0, d:d + 1, :],
            _dot(xb, _gate_dense(wx_ref, d)) + bxg_ref[0, d:d + 1, :])


def _lru_scan(d, xc, pre, lam_row, h0):
    seq = xc.shape[0]
    nl = -lam_row
    softplus = jnp.maximum(nl, 0.0) + jnp.log1p(jnp.exp(-jnp.abs(nl)))
    r = jax.nn.sigmoid(pre[0])
    ig = jax.nn.sigmoid(pre[1])
    log_a = -LRU_C * r * softplus
    a = jnp.exp(log_a)
    u = jnp.exp(0.5 * jnp.log(-jnp.tanh(log_a) * (a * a + 1.0))) * (ig * xc)
    nt = seq // SUBLANES
    tin = lax.broadcasted_iota(jnp.int32, (1, SUBLANES, 1), 1)
    a = a.reshape(nt, SUBLANES, BR_W)
    u = u.reshape(nt, SUBLANES, BR_W)
    s = 1
    while s < SUBLANES:
        if d == 0:
            keep = tin >= s
            a_n = jnp.where(keep, pltpu.roll(a, s, 1), 1.0)
            u_n = jnp.where(keep, pltpu.roll(u, s, 1), 0.0)
        else:
            keep = tin < SUBLANES - s
            a_n = jnp.where(keep, pltpu.roll(a, SUBLANES - s, 1), 1.0)
            u_n = jnp.where(keep, pltpu.roll(u, SUBLANES - s, 1), 0.0)
        u = u + a * u_n
        a = a * a_n
        s *= 2
    ys = [None] * nt
    h = h0
    for k in (range(nt) if d == 0 else range(nt - 1, -1, -1)):
        y = u[k] + a[k] * h
        ys[k] = y
        h = y[SUBLANES - 1:, :] if d == 0 else y[:1, :]
    return jnp.concatenate(ys, axis=0), h


def _lru_specs(l, imap):
    wspec = pl.BlockSpec((1, 2, B_BLOCKS, B_BLK, B_BLK), lambda *g: imap(*g) + (0, 0))
    vspec = pl.BlockSpec((1, 2, BR_W), imap)
    return [pl.BlockSpec((1, CONV_W, BR_W), imap), pl.BlockSpec((1, 1, BR_W), imap),
            wspec, vspec, wspec, vspec, vspec]


MERGE_COLS = 512

def _merge_kernel(layer, final, xa_ref, xb_ref, mod_ref, h_ref, zctx_ref, zal_ref, yf_ref, yb_ref,
                  zc0_ref, zc1_ref, zdl_ref, wmg_hbm, bmg_ref, wbo_ref, wo_ref, nf_ref, *refs):
    out_refs, (wmg_s, wbo_s, wo_s, stage, sem) = refs[:-5], refs[-5:]
    i = pl.program_id(0)

    def wmg_copy(chunk, n):
        slot = chunk % 2
        return pltpu.make_async_copy(wmg_hbm.at[layer, pl.ds(chunk * W_ROWS, W_ROWS), n, :],
                                     stage.at[slot, n], sem.at[slot, n])

    @pl.when(i == 0)
    def _():
        for n in range(N_BRANCH):
            wmg_copy(0, n).start()

    @pl.when(i < W_STEPS - 1)
    def _():
        for n in range(N_BRANCH):
            wmg_copy(i + 1, n).start()

    @pl.when(i < W_STEPS)
    def _():
        r0 = pl.multiple_of(i * W_ROWS, W_ROWS)
        for n in range(N_BRANCH):
            wmg_copy(i, n).wait()
            wmg_s[pl.ds(r0, W_ROWS), n * D_MODEL:(n + 1) * D_MODEL] = stage[i % 2, n].astype(BF16)
        _cast_rows(i, wbo_ref, wbo_s)
        _cast_rows(i, wo_ref, wo_s)

    @pl.when(i >= W_STEPS)
    def _():
        t = i - W_STEPS
        x = _load_x(t, xa_ref, xb_ref)
        gate = mod_ref[0][:, 2 * D_MODEL:]
        h = h_ref[...]
        is_ctx = t < N_CTX_TILES
        second_half = (t - N_CTX_TILES) % LAT_TILES_PER_SEQ == 1
        z_lat = [zal_ref[...],
                 (yf_ref[...] + yb_ref[...]).astype(BF16),
                 jnp.where(second_half, zc1_ref[...], zc0_ref[...]),
                 zdl_ref[...]]
        zs = [jnp.where(is_ctx, zctx_ref[:, n * BR_W:(n + 1) * BR_W], z_lat[n]) for n in range(N_BRANCH)]
        bmg = bmg_ref[0]
        cols = []
        for c in range(0, D_MODEL, MERGE_COLS):
            acc = None
            for n in range(N_BRANCH):
                g = jax.nn.sigmoid(_dot(h, wmg_s[:, n * D_MODEL + c:n * D_MODEL + c + MERGE_COLS])
                                   + bmg[n:n + 1, c:c + MERGE_COLS])
                term = g * _dot(zs[n], wbo_s[n * BR_W:(n + 1) * BR_W, c:c + MERGE_COLS])
                acc = term if acc is None else acc + term
            cols.append(acc.astype(BF16))
        merged = jnp.concatenate(cols, axis=1)
        xn = x + gate * _dot(merged, wo_s[...])
        if not final:
            out_refs[0][...] = xn
        else:
            y = xn * lax.rsqrt(jnp.mean(xn * xn, axis=-1, keepdims=True) + EPS) * nf_ref[...]

            @pl.when(t < N_CTX_TILES)
            def _():
                out_refs[0][...] = y

            @pl.when(t >= N_CTX_TILES)
            def _():
                out_refs[1][...] = y


def _merge(final, xa, xb, xb_off, mod3, l, h, z_ctx, za_lat, y_f, y_b, zc_lat, zd_lat,
           w_mg, b_mg, w_bo, w_o, norm_f):
    assert TM == NA_HALF_Q
    row = lambda i: (_tile(i), 0)
    ctx_blk = lambda i: jnp.minimum(_tile(i), N_CTX_TILES - 1)
    lat_blk = lambda i: jnp.maximum(_tile(i) - N_CTX_TILES, 0)
    lat_spec = pl.BlockSpec((TM, BR_W), lambda i: (lat_blk(i), 0))
    half_spec = pl.BlockSpec((TM, BR_W), lambda i: (lat_blk(i) // LAT_TILES_PER_SEQ, 0))
    if final:
        out_specs = [pl.BlockSpec((TM, D_MODEL), lambda i: (ctx_blk(i), 0)),
                     pl.BlockSpec((TM, D_MODEL), lambda i: (lat_blk(i), 0))]
        out_shape = [jax.ShapeDtypeStruct((N_CTX, D_MODEL), F32), jax.ShapeDtypeStruct((N_LAT, D_MODEL), F32)]
    else:
        out_specs = pl.BlockSpec((TM, D_MODEL), row)
        out_shape = jax.ShapeDtypeStruct((N_TOK, D_MODEL), F32)
    wchunk = lambda width: pl.BlockSpec((1, W_ROWS, width), lambda i: (l, _wchunk(i), 0))
    return pl.pallas_call(
        functools.partial(_merge_kernel, l, final),
        grid=(W_STEPS + N_TILES,),
        in_specs=[
            pl.BlockSpec((TM, D_MODEL), lambda i: (ctx_blk(i), 0)),
            pl.BlockSpec((TM, D_MODEL), lambda i: (lat_blk(i) + xb_off, 0)),
            pl.BlockSpec((1, 1, 3 * D_MODEL), lambda i: (l * MOD_ROWS + _mod_row(_tile(i)), 0, 0)),
            pl.BlockSpec((TM, D_MODEL), row),
            pl.BlockSpec((TM, N_BRANCH * BR_W), lambda i: (ctx_blk(i), 0)),
            lat_spec,
            lat_spec, lat_spec,
            half_spec, half_spec,
            lat_spec,
            pl.BlockSpec(memory_space=pl.ANY),
            pl.BlockSpec((1, N_BRANCH, D_MODEL), lambda i: (l, 0, 0)),
            wchunk(D_MODEL),
            wchunk(D_MODEL),
            pl.BlockSpec((1, D_MODEL), lambda i: (0, 0)),
        ],
        out_specs=out_specs,
        out_shape=out_shape,
        scratch_shapes=[pltpu.VMEM((D_MODEL, N_BRANCH * D_MODEL), BF16),
                        pltpu.VMEM((N_BRANCH * BR_W, D_MODEL), BF16),
                        pltpu.VMEM((D_MODEL, D_MODEL), BF16),
                        pltpu.VMEM((2, N_BRANCH, W_ROWS, D_MODEL), F32),
                        pltpu.SemaphoreType.DMA((2, N_BRANCH))],
        compiler_params=_cparams(("arbitrary",)),
        name=f"merge_l{l}",
    )(xa, xb, mod3, h, z_ctx, za_lat, y_f, y_b, zc_lat[0], zc_lat[1], zd_lat,
      w_mg, b_mg,
      w_bo.reshape(DEPTH, N_BRANCH * BR_W, D_MODEL), w_o, norm_f.reshape(1, D_MODEL))


def kernel(x_prompt, x_sample, cache_diff_k, cache_diff_v, cache_na_k, cache_na_v, cache_swa_k, cache_swa_v,
           state_lru, c, c_ctx, norm_g, w_ada, b_ada, w_in, diff_lambda, diff_norm_g, conv_w, conv_b,
           lru_wa, lru_ba, lru_wx, lru_bx, lru_lam, na_rpb, swa_sink, w_mg, b_mg, w_bo, w_o, norm_f):
    tab_a, tab_d = jnp.asarray(_rope_tables(A_QK)), jnp.asarray(_rope_tables(HD))

    cvecs = jnp.concatenate([c_ctx[None, :], c, jnp.zeros((MOD_ROWS - 1 - DEC_BATCH, D_MODEL), F32)], axis=0)
    mod3 = _modulation(cvecs, w_ada, b_ada).reshape(DEPTH * MOD_ROWS, 1, 3 * D_MODEL)
    bias = _na_bias(na_rpb)
    norm_g3 = norm_g.reshape(DEPTH, 1, D_MODEL)
    dg3 = diff_norm_g.reshape(DEPTH, 1, HD)
    dg4 = jnp.tile(diff_norm_g, (1, A_HEADS)).reshape(DEPTH, 1, BR_W)
    sink3 = swa_sink.reshape(DEPTH, 1, D_HEADS)
    conv_b3 = conv_b.reshape(DEPTH, 1, BR_W)
    past = [jnp.swapaxes(t, -1, -2) for t in
            (cache_diff_k, cache_diff_v, cache_na_k, cache_na_v, cache_swa_k, cache_swa_v)]

    xa = x_prompt.reshape(N_CTX, D_MODEL)
    xb = x_sample.reshape(N_LAT, D_MODEL)
    xb_off = 0
    caches = states = None
    y_p = y_s = None
    for l in range(DEPTH):
        h, proj, bx, caches = _inproj(xa, xb, xb_off, mod3, l, norm_g3, w_in, tab_a, tab_d, caches)

        lru_params = (conv_w, conv_b3, lru_wa, lru_ba, lru_wx, lru_bx, lru_lam)
        z_ctx, states = _ctx_attn(proj, bx, l, diff_lambda, dg4, sink3, lru_params, states)
        za_lat = _lat_diff(proj, past[0], past[1], l, diff_lambda, dg3)
        zc0, y_f = _lat_na(0, proj, bx, past[2], past[3], bias, l, lru_params, state_lru)
        zc1, y_b = _lat_na(1, proj, bx, past[2], past[3], bias, l, lru_params, state_lru)
        zd_lat = _lat_swa(proj, past[4], past[5], sink3, l)

        final = l == DEPTH - 1
        out = _merge(final, xa, xb, xb_off, mod3, l, h, z_ctx, za_lat, y_f, y_b, (zc0, zc1), zd_lat,
                     w_mg, b_mg, w_bo, w_o, norm_f)
        if final:
            y_p, y_s = out
        else:
            xa = xb = out
            xb_off = N_CTX_TILES

    new_caches = [jnp.swapaxes(t, -1, -2) for t in caches]
    return (y_p.reshape(BATCH, SEQ, D_MODEL), y_s.reshape(DEC_BATCH, DEC_SEQ, D_MODEL), *new_caches, states)
```

```python
import functools
import math

import numpy as np
import jax
import jax.numpy as jnp
from jax import lax
from jax.experimental import pallas as pl
from jax.experimental.pallas import tpu as pltpu

F32 = jnp.float32
BF16 = jnp.bfloat16

D_MODEL = 1024
BATCH = 16
SEQ = 256
DEPTH = 2
DEC_BATCH = 8
DEC_SEQ = 1024
PAST_LEN = 512
GRID_W = 64
N_BRANCH = 4
BR_W = D_MODEL // 4
HD = 64
A_HEADS = BR_W // HD
A_QK = HD // 2
B_BLOCKS = 4
B_BLK = BR_W // B_BLOCKS
CONV_W = 4
LRU_C = 8.0
C_HEADS = BR_W // HD
NA_ROWS = 8
NA_COLS = 16
D_HEADS = BR_W // HD
D_KV = 2
D_GROUP = D_HEADS // D_KV
WIN = 128
ROPE_BASE = 10000.0
EPS = 1e-6
NEG = -1e30

LANES = 128
SUBLANES = 8
V7X_VMEM_BYTES = 64 * 1024 * 1024

N_CTX = BATCH * SEQ
N_LAT = DEC_BATCH * DEC_SEQ
N_TOK = N_CTX + N_LAT
GRID_ROWS = DEC_SEQ // GRID_W

TM = 512
N_CTX_TILES = N_CTX // TM
N_TILES = N_TOK // TM
LAT_TILES_PER_SEQ = DEC_SEQ // TM

COL_G = 0
GATE_A, GATE_R, GATE_C, GATE_D = (COL_G // BR_W + k for k in range(N_BRANCH))
COL_AQ = COL_G + N_BRANCH * BR_W
COL_AK = COL_AQ + BR_W
COL_AV = COL_AK + BR_W
COL_CQ = COL_AV + BR_W
COL_CK = COL_CQ + BR_W
COL_CV = COL_CK + BR_W
COL_DQ = COL_CV + BR_W
COL_DK = COL_DQ + BR_W
COL_DV = COL_DK + BR_W
PROJ_W = COL_DV + BR_W
IN_W = (N_BRANCH + 3 + 1 + 3 + 1) * BR_W + 2 * D_KV * HD

assert GRID_ROWS == 2 * NA_ROWS
NA_HALF_Q = DEC_SEQ // 2
NA_KROWS = NA_ROWS + NA_ROWS // 2
NA_KSPAN = NA_KROWS * GRID_W
NA_HALF_KSTART = (GRID_ROWS - NA_KROWS) * GRID_W

SWA_QB = 256
SWA_SPAN = SWA_QB + 2 * WIN

LOG2E = math.log2(math.e)
QA_SCALE = A_QK ** -0.5 * LOG2E
QH_SCALE = HD ** -0.5 * LOG2E

VMEM_LIMIT = V7X_VMEM_BYTES - 8 * 1024 * 1024

MOD_ROWS = -(-(1 + DEC_BATCH) // SUBLANES) * SUBLANES
MOD_COLS = 6 * LANES


def _cparams(sem):
    return pltpu.CompilerParams(dimension_semantics=sem, vmem_limit_bytes=VMEM_LIMIT)


def _dot(a, b):
    return jnp.dot(a, b, preferred_element_type=F32)


def _dot_nt(a, b):
    return lax.dot_general(a, b, (((1,), (1,)), ((), ())), preferred_element_type=F32)


def _silu(x):
    return x * jax.nn.sigmoid(x)


def _rope_tables(d):
    half = d // 2
    quarter = half // 2
    lane = np.arange(LANES)
    q = lane % d
    use_col = (q >= half)
    i = (q % half) % quarter
    first = (q % half) < quarter
    inv = (ROPE_BASE ** (-np.arange(quarter, dtype=np.float32) / np.float32(quarter))).astype(np.float32)
    t = np.arange(DEC_SEQ)
    pos = np.where(use_col[None, :], (t % GRID_W)[:, None], (t // GRID_W)[:, None]).astype(np.float32)
    ang = (pos * inv[i][None, :]).astype(np.float32)
    cos, sin = np.cos(ang), np.sin(ang)
    c = np.concatenate([np.ones((TM, LANES), np.float32), cos], axis=0)
    s1 = np.concatenate([np.zeros((TM, LANES), np.float32), np.where(first[None, :], -sin, 0.0)], axis=0)
    s2 = np.concatenate([np.zeros((TM, LANES), np.float32), np.where(first[None, :], 0.0, sin)], axis=0)
    return np.stack([c, s1, s2]).astype(np.float32)


def _rope_lanes(x, tab_ref, shift):
    return (x * tab_ref[0] + pltpu.roll(x, LANES - shift, 1) * tab_ref[1]
            + pltpu.roll(x, shift, 1) * tab_ref[2])


def _rope(x, tab_ref, shift):
    w = x.shape[1]
    return jnp.concatenate([_rope_lanes(x[:, c:c + LANES], tab_ref, shift) for c in range(0, w, LANES)], axis=1)


def _mod_kernel(c_ref, w_ref, b_ref, o_ref):
    c = c_ref[...]
    o_ref[0] = _dot(_silu(c).astype(BF16), w_ref[0].astype(BF16)) + b_ref[0]


def _modulation(cvecs, w_ada, b_ada):
    nb = 3 * D_MODEL // MOD_COLS
    return pl.pallas_call(
        _mod_kernel,
        grid=(DEPTH, nb),
        in_specs=[pl.BlockSpec((MOD_ROWS, D_MODEL), lambda l, j: (0, 0)),
                  pl.BlockSpec((1, D_MODEL, MOD_COLS), lambda l, j: (l, 0, j)),
                  pl.BlockSpec((1, 1, MOD_COLS), lambda l, j: (l, 0, j))],
        out_specs=pl.BlockSpec((1, MOD_ROWS, MOD_COLS), lambda l, j: (l, 0, j)),
        out_shape=jax.ShapeDtypeStruct((DEPTH, MOD_ROWS, 3 * D_MODEL), F32),
        compiler_params=_cparams(("arbitrary", "arbitrary")),
        name="modulation",
    )(cvecs, w_ada, b_ada.reshape(DEPTH, 1, 3 * D_MODEL))


W_STEPS = 4
W_ROWS = D_MODEL // W_STEPS


def _tile(i):
    return jnp.maximum(i - W_STEPS, 0)


def _wchunk(i):
    return jnp.minimum(i, W_STEPS - 1)


def _mod_row(t):
    return jnp.where(t < N_CTX_TILES, 0, 1 + (t - N_CTX_TILES) // LAT_TILES_PER_SEQ)


def _rope_blk(t):
    return jnp.where(t < N_CTX_TILES, 0, 1 + (t - N_CTX_TILES) % LAT_TILES_PER_SEQ)


def _cast_rows(i, src_ref, dst_ref):
    r0 = pl.multiple_of(i * W_ROWS, W_ROWS)
    dst_ref[pl.ds(r0, W_ROWS), :] = src_ref[0].astype(BF16)


WI_G = 0
WI_AQ = WI_G + N_BRANCH * BR_W
WI_AK, WI_AV, WI_BX, WI_CQ, WI_CKV = (WI_AQ + k * BR_W for k in range(1, 6))
WI_DQ = WI_CKV + 2 * BR_W
WI_DKV = WI_DQ + BR_W
SEQ_PER_TILE = TM // SEQ
IN_ROWS = 256


def _store_heads(ref, val, n_heads, width):
    vt = val.T
    for s in range(SEQ_PER_TILE):
        for h in range(n_heads):
            ref[s, 0, h] = vt[h * width:(h + 1) * width, s * SEQ:(s + 1) * SEQ]


def _per_query_head(x):
    return jnp.concatenate([x[:, kv * HD:(kv + 1) * HD] for kv in range(D_KV) for _ in range(D_GROUP)], axis=1)


def _inproj_kernel(n_alias, xa_ref, xb_ref, mod_ref, g_ref, w_ref, ta_ref, td_ref, *refs):
    h_ref, proj_ref, bx_ref, dk_ref, dv_ref, nk_ref, nv_ref, sk_ref, sv_ref, w_s = refs[n_alias:]
    i = pl.program_id(0)

    @pl.when(i < W_STEPS)
    def _():
        _cast_rows(i, w_ref, w_s)

    @pl.when(i >= W_STEPS)
    def _():
        t = i - W_STEPS
        is_ctx = t < N_CTX_TILES
        mod = mod_ref[0]
        shift, scale = mod[:, :D_MODEL], mod[:, D_MODEL:2 * D_MODEL]
        w2 = 2 * BR_W
        kept = []
        for r in range(0, TM, IN_ROWS):
            rs = pl.ds(r, IN_ROWS)
            ta, td = ta_ref.at[:, rs, :], td_ref.at[:, rs, :]
            x = jnp.where(is_ctx, xa_ref[rs, :], xb_ref[rs, :])
            xn = x * lax.rsqrt(jnp.mean(x * x, axis=-1, keepdims=True) + EPS) * g_ref[0]
            hb = (xn * (1.0 + scale) + shift).astype(BF16)
            h_ref[rs, :] = hb

            def mm(c0, width):
                return _dot(hb, w_s[:, c0:c0 + width])

            gs = [mm(WI_G + c, w2) for c in range(0, N_BRANCH * BR_W, w2)]
            aq, ak, av = mm(WI_AQ, BR_W), mm(WI_AK, BR_W), mm(WI_AV, BR_W)
            cq, ckv = mm(WI_CQ, BR_W), mm(WI_CKV, w2)
            dq, dkv = mm(WI_DQ, BR_W), mm(WI_DKV, 2 * D_KV * HD)
            dk, dv = _rope(dkv[:, :D_KV * HD], td, HD // 4), dkv[:, D_KV * HD:]
            bx_ref[rs, :] = mm(WI_BX, BR_W)

            for n, g in enumerate(gs):
                proj_ref[rs, COL_G + n * w2:COL_G + (n + 1) * w2] = _silu(g).astype(BF16)
            proj_ref[rs, COL_AQ:COL_AQ + BR_W] = _rope(aq * QA_SCALE, ta, A_QK // 4).astype(BF16)
            proj_ref[rs, COL_AK:COL_AK + BR_W] = _rope(ak, ta, A_QK // 4).astype(BF16)
            proj_ref[rs, COL_AV:COL_AV + BR_W] = av.astype(BF16)
            proj_ref[rs, COL_CQ:COL_CQ + BR_W] = (cq * QH_SCALE).astype(BF16)
            proj_ref[rs, COL_CK:COL_CK + w2] = ckv.astype(BF16)
            proj_ref[rs, COL_DQ:COL_DQ + BR_W] = _rope(dq * QH_SCALE, td, HD // 4).astype(BF16)
            proj_ref[rs, COL_DK:COL_DK + BR_W] = _per_query_head(dk.astype(BF16))
            proj_ref[rs, COL_DV:COL_DV + BR_W] = _per_query_head(dv.astype(BF16))
            kept.append((ak, av, ckv, dk, dv))

        @pl.when(is_ctx)
        def _():
            ak, av, ckv, dk, dv = (jnp.concatenate(v, axis=0) for v in zip(*kept))
            akt = ak.T
            for s in range(SEQ_PER_TILE):
                for h in range(A_HEADS):
                    for m in range(2):
                        c0 = h * HD + m * A_QK
                        dk_ref[s, 0, h, m] = akt[c0:c0 + A_QK, s * SEQ:(s + 1) * SEQ]
            _store_heads(dv_ref, av, A_HEADS, HD)
            _store_heads(nk_ref, ckv[:, :BR_W], C_HEADS, HD)
            _store_heads(nv_ref, ckv[:, BR_W:], C_HEADS, HD)
            _store_heads(sk_ref, dk, D_KV, HD)
            _store_heads(sv_ref, dv, D_KV, HD)


def _cache_shapes():
    hs = lambda n: (BATCH, DEPTH, n, HD, SEQ)
    return [(BATCH, DEPTH, A_HEADS, 2, A_QK, SEQ), hs(A_HEADS), hs(C_HEADS), hs(C_HEADS), hs(D_KV), hs(D_KV)]


def _inproj(xa, xb, xb_off, mod3, l, norm_g, w_in, tab_a, tab_d, caches):
    row = lambda i: (_tile(i), 0)
    ctx_blk = lambda i: jnp.minimum(_tile(i), N_CTX_TILES - 1)

    def cache_spec(shape):
        blk = (SEQ_PER_TILE, 1) + shape[2:]
        nz = len(shape) - 2
        return pl.BlockSpec(blk, lambda i: (ctx_blk(i), l) + (0,) * nz)

    cshapes = _cache_shapes()
    aliases = {} if caches is None else {7 + k: 3 + k for k in range(6)}
    extra_specs = [] if caches is None else [pl.BlockSpec(memory_space=pl.ANY)] * 6
    extra_args = [] if caches is None else list(caches)
    outs = pl.pallas_call(
        functools.partial(_inproj_kernel, len(extra_args)),
        grid=(W_STEPS + N_TILES,),
        in_specs=[
            pl.BlockSpec((TM, D_MODEL), lambda i: (ctx_blk(i), 0)),
            pl.BlockSpec((TM, D_MODEL), lambda i: (jnp.maximum(_tile(i) - N_CTX_TILES, 0) + xb_off, 0)),
            pl.BlockSpec((1, 1, 3 * D_MODEL), lambda i: (l * MOD_ROWS + _mod_row(_tile(i)), 0, 0)),
            pl.BlockSpec((1, 1, D_MODEL), lambda i: (l, 0, 0)),
            pl.BlockSpec((1, W_ROWS, IN_W), lambda i: (l, _wchunk(i), 0)),
            pl.BlockSpec((3, TM, LANES), lambda i: (0, _rope_blk(_tile(i)), 0)),
            pl.BlockSpec((3, TM, LANES), lambda i: (0, _rope_blk(_tile(i)), 0)),
        ] + extra_specs,
        out_specs=[
            pl.BlockSpec((TM, D_MODEL), row),
            pl.BlockSpec((TM, PROJ_W), row),
            pl.BlockSpec((TM, BR_W), row),
        ] + [cache_spec(s) for s in cshapes],
        out_shape=[
            jax.ShapeDtypeStruct((N_TOK, D_MODEL), BF16),
            jax.ShapeDtypeStruct((N_TOK, PROJ_W), BF16),
            jax.ShapeDtypeStruct((N_TOK, BR_W), F32),
        ] + [jax.ShapeDtypeStruct(s, F32) for s in cshapes],
        scratch_shapes=[pltpu.VMEM((D_MODEL, IN_W), BF16)],
        input_output_aliases=aliases,
        compiler_params=_cparams(("arbitrary",)),
        name=f"inproj_l{l}",
    )(xa, xb, mod3, norm_g, w_in, tab_a, tab_d, *extra_args)
    return outs[0], outs[1], outs[2], tuple(outs[3:])


def _diff_lambda(lam_ref, lam_init):
    lv = lam_ref[0]
    s1 = jnp.sum(lv[0:1] * lv[1:2], axis=-1, keepdims=True)
    s2 = jnp.sum(lv[2:3] * lv[3:4], axis=-1, keepdims=True)
    return jnp.exp(s1) - jnp.exp(s2) + lam_init


def _diff_norm(o, dg_ref, lam_init):
    y = o * lax.rsqrt(jnp.mean(o * o, axis=-1, keepdims=True) + EPS) * dg_ref[0]
    return y * (1.0 - lam_init)


def _exp2_parts(parts, extra=None):
    m = functools.reduce(jnp.maximum, [jnp.max(s, axis=-1, keepdims=True) for s in parts])
    if extra is not None:
        m = jnp.maximum(m, extra)
    return [jnp.exp2(s - m).astype(BF16) for s in parts], m


def _pipelined(n, score_fn, pv_fn):
    outs = []
    nxt = score_fn(0)
    for c in range(n):
        cur, nxt = nxt, (score_fn(c + 1) if c + 1 < n else None)
        outs.append(pv_fn(c, cur))
    return outs


def _with_ones(v):
    return jnp.concatenate([v, jnp.ones_like(v)], axis=1)


def _with_ones_t(vt):
    return jnp.concatenate([vt, jnp.ones((2 * SUBLANES, vt.shape[1]), vt.dtype)], axis=0)


def _sink_col(sink2, kv, rows):
    r = lax.broadcasted_iota(jnp.int32, (D_GROUP * rows, 1), 0)
    return jnp.where(r < rows, sink2[:, kv * D_GROUP:kv * D_GROUP + 1], sink2[:, kv * D_GROUP + 1:kv * D_GROUP + 2])


CTX_PER_STEP = 2


def _lane_groups(n, rows):
    grp = lax.broadcasted_iota(jnp.int32, (rows, BR_W), 1) // (BR_W // n)
    return [grp == p for p in range(n)]


def _only(x, m):
    return jnp.where(m, x, jnp.zeros_like(x))


def _ones_outside(v, m):
    return jnp.where(m, v, jnp.ones_like(v))


def _swap_halves(p):
    return jnp.concatenate([p[:, BR_W // 2:], p[:, :BR_W // 2]], axis=1)


def _pick_heads(os):
    grp = lax.broadcasted_iota(jnp.int32, os[0].shape, 1) // HD
    acc = os[0]
    for h in range(1, len(os)):
        acc = jnp.where(grp == h, os[h], acc)
    return acc


def _head_rms(o, dg4_ref, lam_init):
    r = lax.broadcasted_iota(jnp.int32, (BR_W, BR_W), 0) // HD
    c = lax.broadcasted_iota(jnp.int32, (BR_W, BR_W), 1) // HD
    ss = _dot((o * o).astype(BF16), jnp.where(r == c, 1.0, 0.0).astype(BF16))
    return o * lax.rsqrt(ss * (1.0 / HD) + EPS) * dg4_ref[0] * (1.0 - lam_init)


def _ctx_attn_kernel(n_alias, lam_init, aq_ref, ak_ref, av_ref, cq_ref, ck_ref, cv_ref, dq_ref, dk_ref, dv_ref,
                     ga_ref, gc_ref, gd_ref, lam_ref, dg4_ref, sink_ref,
                     bx_ref, gr_ref, cw_ref, cb_ref, wa_ref, ba_ref, wx_ref, bxg_ref, llam_ref, *refs):
    z_ref, fin_ref = refs[n_alias:]
    seqs = [slice(s * SEQ, (s + 1) * SEQ) for s in range(CTX_PER_STEP)]

    xcs = [_lru_conv(bx_ref[rs, :], cw_ref[0], cb_ref[0]) for rs in seqs]
    xc_all = jnp.concatenate(xcs, axis=0)
    pres = [_lru_gates(d, xc_all, wa_ref, ba_ref, wx_ref, bxg_ref) for d in range(2)]
    for s, rs in enumerate(seqs):
        h0 = jnp.zeros((1, BR_W), F32)
        yf, hf = _lru_scan(0, xcs[s], [p[rs] for p in pres[0]], llam_ref[0, 0:1, :], h0)
        yb, hb = _lru_scan(1, xcs[s], [p[rs] for p in pres[1]], llam_ref[0, 1:2, :], h0)
        z_ref[rs, BR_W:2 * BR_W] = ((yf + yb) * gr_ref[rs, :].astype(F32)).astype(BF16)
        fin_ref[s, 0, 0:1, :] = hf
        fin_ref[s, 0, 1:2, :] = hb

    lam = _diff_lambda(lam_ref, lam_init)
    sink2 = sink_ref[0] * LOG2E
    m_qk = _lane_groups(2 * A_HEADS, SEQ)
    m_hd = _lane_groups(C_HEADS, SEQ)

    sa, sc, sd = [], [], []
    for rs in seqs:
        aq, cq, dq = aq_ref[rs, :], cq_ref[rs, :], dq_ref[rs, :]
        ak, ck, dk = ak_ref[rs, :], ck_ref[rs, :], dk_ref[rs, :]
        sa.append([_dot_nt(_only(aq, m), ak) for m in m_qk])
        sc.append([_dot_nt(_only(cq, m), ck) for m in m_hd])
        sd.append([_dot_nt(_only(dq, m), dk) for m in m_hd])

    ea =[[_exp2_parts([s])[0][0] for s in ss] for ss in sa]
    ec = [[_exp2_parts([s])[0][0] for s in ss] for ss in sc]
    ed = [[_exp2_parts([s], extra=sink2[:, j:j + 1]) for j, s in enumerate(ss)] for ss in sd]

    for i, rs in enumerate(seqs):
        av, cv, dv = av_ref[rs, :], cv_ref[rs, :], dv_ref[rs, :]
        os = []
        for h in range(A_HEADS):
            w = _ones_outside(av, m_hd[h])
            p0, p1 = _dot(ea[i][2 * h], w), _dot(ea[i][2 * h + 1], w)
            os.append(p0 * (1.0 / _swap_halves(p0)) - p1 * (lam / _swap_halves(p1)))
        z_ref[rs, 0:BR_W] = (_head_rms(_pick_heads(os), dg4_ref, lam_init) * ga_ref[rs, :].astype(F32)).astype(BF16)

        os = []
        for h in range(C_HEADS):
            p = _dot(ec[i][h], _ones_outside(cv, m_hd[h]))
            os.append(p * (1.0 / _swap_halves(p)))
        z_ref[rs, 2 * BR_W:3 * BR_W] = (_pick_heads(os) * gc_ref[rs, :].astype(F32)).astype(BF16)

        os = []
        for j in range(D_HEADS):
            (e,), m = ed[i][j]
            p = _dot(e, _ones_outside(dv, m_hd[j]))
            os.append(p * (1.0 / (_swap_halves(p) + jnp.exp2(sink2[:, j:j + 1] - m))))
        z_ref[rs, 3 * BR_W:4 * BR_W] = (_pick_heads(os) * gd_ref[rs, :].astype(F32)).astype(BF16)


def _ctx_attn(proj, bx, l, diff_lambda, dg4, sink3, lru_params, states):
    lam_init = 0.8 - 0.6 * math.exp(-0.3 * l)
    colblk = lambda cb: pl.BlockSpec((CTX_PER_STEP * SEQ, BR_W), lambda b: (b, cb))
    extra = [] if states is None else [states]
    return pl.pallas_call(
        functools.partial(_ctx_attn_kernel, len(extra), lam_init),
        grid=(BATCH // CTX_PER_STEP,),
        in_specs=[colblk(COL_AQ // BR_W), colblk(COL_AK // BR_W), colblk(COL_AV // BR_W),
                  colblk(COL_CQ // BR_W), colblk(COL_CK // BR_W), colblk(COL_CV // BR_W),
                  colblk(COL_DQ // BR_W), colblk(COL_DK // BR_W), colblk(COL_DV // BR_W),
                  colblk(GATE_A), colblk(GATE_C), colblk(GATE_D),
                  pl.BlockSpec((1, 4, A_QK), lambda b: (l, 0, 0)),
                  pl.BlockSpec((1, 1, BR_W), lambda b: (l, 0, 0)),
                  pl.BlockSpec((1, 1, D_HEADS), lambda b: (l, 0, 0)),
                  pl.BlockSpec((CTX_PER_STEP * SEQ, BR_W), lambda b: (b, 0)), colblk(GATE_R)]
                 + _lru_specs(l, lambda b: (l, 0, 0)) + [pl.BlockSpec(memory_space=pl.ANY)] * len(extra),
        out_specs=[pl.BlockSpec((CTX_PER_STEP * SEQ, N_BRANCH * BR_W), lambda b: (b, 0)),
                   pl.BlockSpec((CTX_PER_STEP, 1, 2, BR_W), lambda b: (b, l, 0, 0))],
        out_shape=[jax.ShapeDtypeStruct((N_CTX, N_BRANCH * BR_W), BF16),
                   jax.ShapeDtypeStruct((BATCH, DEPTH, 2, BR_W), F32)],
        input_output_aliases={} if states is None else {24: 1},
        compiler_params=_cparams(("arbitrary",)),
        name=f"ctx_attn_l{l}",
    )(*([proj] * 12), diff_lambda, dg4, sink3, bx, proj, *lru_params, *extra)


LA_QB = 256


def _lat_diff_kernel(lam_init, aq_ref, ak_ref, av_ref, ck_ref, cv_ref, ga_ref, lam_ref, dg_ref, za_ref):
    lam = _diff_lambda(lam_ref, lam_init)
    aq = aq_ref[...]
    va_ctx = [_with_ones_t(cv_ref[0, 0, h].astype(BF16)) for h in range(A_HEADS)]
    va_loc = [_with_ones(av_ref[:, h * HD:(h + 1) * HD]) for h in range(A_HEADS)]

    def scores(c):
        h, m = divmod(c, 2)
        lo = h * HD + m * A_QK
        q = aq[:, lo:lo + A_QK]
        return [_dot(q, ck_ref[0, 0, h, m].astype(BF16)),
                _dot_nt(q, ak_ref[:, lo:lo + A_QK])]

    def pv(h, s0, s1):
        (c0, l0), _ = _exp2_parts(s0)
        (c1, l1), _ = _exp2_parts(s1)
        p = (_dot_nt(jnp.concatenate([c0, c1], axis=0), va_ctx[h])[:, :HD + 1]
             + _dot(jnp.concatenate([l0, l1], axis=0), va_loc[h])[:, :HD + 1])
        return p[:LA_QB], p[LA_QB:]

    ss = [scores(c) for c in range(2 * A_HEADS)]
    outs = []
    for h in range(A_HEADS):
        p0, p1 = pv(h, ss[2 * h], ss[2 * h + 1])
        o = p0[:, :HD] * (1.0 / p0[:, HD:HD + 1]) - p1[:, :HD] * (lam / p1[:, HD:HD + 1])
        outs.append(_diff_norm(o, dg_ref, lam_init))
    za_ref[...] = (jnp.concatenate(outs, axis=1) * ga_ref[...].astype(F32)).astype(BF16)


def _lat_diff(proj, cache_k, cache_v, l, diff_lambda, dg3):
    lam_init = 0.8 - 0.6 * math.exp(-0.3 * l)
    nq = DEC_SEQ // LA_QB
    qrow = lambda b, j: N_CTX // LA_QB + b * nq + j
    srow = lambda b, j: N_CTX // DEC_SEQ + b
    return pl.pallas_call(
        functools.partial(_lat_diff_kernel, lam_init),
        grid=(DEC_BATCH, nq),
        in_specs=[pl.BlockSpec((LA_QB, BR_W), lambda b, j: (qrow(b, j), COL_AQ // BR_W)),
                  pl.BlockSpec((DEC_SEQ, BR_W), lambda b, j: (srow(b, j), COL_AK // BR_W)),
                  pl.BlockSpec((DEC_SEQ, BR_W), lambda b, j: (srow(b, j), COL_AV // BR_W)),
                  pl.BlockSpec((1, 1, A_HEADS, 2, A_QK, PAST_LEN), lambda b, j: (b, l, 0, 0, 0, 0)),
                  pl.BlockSpec((1, 1, A_HEADS, HD, PAST_LEN), lambda b, j: (b, l, 0, 0, 0)),
                  pl.BlockSpec((LA_QB, BR_W), lambda b, j: (qrow(b, j), 0)),
                  pl.BlockSpec((1, 4, A_QK), lambda b, j: (l, 0, 0)),
                  pl.BlockSpec((1, 1, HD), lambda b, j: (l, 0, 0))],
        out_specs=pl.BlockSpec((LA_QB, BR_W), lambda b, j: (b * nq + j, 0)),
        out_shape=jax.ShapeDtypeStruct((N_LAT, BR_W), BF16),
        compiler_params=_cparams(("arbitrary", "arbitrary")),
        name=f"lat_diff_l{l}",
    )(proj, proj, proj, cache_k, cache_v, proj, diff_lambda, dg3)


def _na_bias_kernel(rpb_ref, o_ref):
    l, h = pl.program_id(0), pl.program_id(1)
    base = (l * C_HEADS + h) * ((2 * NA_ROWS - 1) * (2 * NA_COLS - 1))
    qc = lax.broadcasted_iota(jnp.int32, (GRID_W, GRID_W), 0)
    kc = lax.broadcasted_iota(jnp.int32, (GRID_W, GRID_W), 1)
    cs = jnp.clip(qc - NA_COLS // 2, 0, GRID_W - NA_COLS)
    ok = (kc >= cs) & (kc < cs + NA_COLS)
    dcol = kc - qc + (NA_COLS - 1)
    neg = jnp.full((GRID_W, GRID_W), NEG, F32)
    tabs = []
    for d in range(2 * NA_ROWS - 1):
        t = neg
        for dc in range(2 * NA_COLS - 1):
            t = jnp.where(ok & (dcol == dc), rpb_ref[base + d * (2 * NA_COLS - 1) + dc] * LOG2E, t)
        tabs.append(t)
    for half in range(2):
        for rq in range(NA_ROWS):
            r = half * NA_ROWS + rq
            rs = min(max(r - NA_ROWS // 2, 0), GRID_ROWS - NA_ROWS)
            blks = []
            for kr in range(NA_KROWS):
                kabs = half * (NA_HALF_KSTART // GRID_W) + kr
                blks.append(tabs[kabs - r + NA_ROWS - 1] if rs <= kabs < rs + NA_ROWS else neg)
            o_ref[0, 0, half, rq * GRID_W:(rq + 1) * GRID_W, :] = jnp.concatenate(blks, axis=1)


def _na_bias(na_rpb):
    return pl.pallas_call(
        _na_bias_kernel,
        grid=(DEPTH, C_HEADS),
        in_specs=[pl.BlockSpec(memory_space=pltpu.SMEM)],
        out_specs=pl.BlockSpec((1, 1, 2, NA_HALF_Q, NA_KSPAN), lambda l, h: (l, h, 0, 0, 0)),
        out_shape=jax.ShapeDtypeStruct((DEPTH, C_HEADS, 2, NA_HALF_Q, NA_KSPAN), F32),
        compiler_params=_cparams(("arbitrary", "arbitrary")),
        name="na_bias",
    )(na_rpb.reshape(-1))


def _lat_na_kernel(half, cq_ref, ck_ref, cv_ref, kc_ref, vc_ref, bias_ref, gc_ref,
                   bx_ref, gr_ref, cw_ref, cb_ref, wa_ref, ba_ref, wx_ref, bxg_ref, llam_ref, st_ref, zc_ref, y_ref):
    k0 = half * NA_HALF_KSTART
    cq = cq_ref[...]
    kl_all = ck_ref[k0:k0 + NA_KSPAN, :]
    vl_all = cv_ref[k0:k0 + NA_KSPAN, :]

    def scores(h):
        q = cq[:, h * HD:(h + 1) * HD]
        return [_dot(q, kc_ref[0, 0, h].astype(BF16)),
                _dot_nt(q, kl_all[:, h * HD:(h + 1) * HD]) + bias_ref[0, h, 0]]

    def pv(h, s):
        (e_ctx, e_loc), _ = _exp2_parts(s)
        p_ctx = _dot_nt(e_ctx, _with_ones_t(vc_ref[0, 0, h].astype(BF16)))
        p_loc = _dot(e_loc, _with_ones(vl_all[:, h * HD:(h + 1) * HD]))
        return (p_ctx[:, :HD] + p_loc[:, :HD]) * (1.0 / (p_ctx[:, HD:HD + 1] + p_loc[:, HD:HD + 1]))

    xc = _lru_conv(bx_ref[...], cw_ref[0], cb_ref[0])
    pre = _lru_gates(half, xc, wa_ref, ba_ref, wx_ref, bxg_ref)
    y, _ = _lru_scan(half, xc, pre, llam_ref[0, half:half + 1, :], st_ref[0, 0, half:half + 1, :])
    y_ref[...] = y * gr_ref[...].astype(F32)

    ss = [scores(h) for h in range(C_HEADS)]
    outs = [pv(h, s) for h, s in enumerate(ss)]
    zc_ref[...] = (jnp.concatenate(outs, axis=1) * gc_ref[...].astype(F32)).astype(BF16)


def _lat_na(half, proj, bx, cache_k, cache_v, bias, l, lru_params, state_lru):
    qrow = lambda b: N_CTX // NA_HALF_Q + b * 2 + half
    srow = lambda b: N_CTX // DEC_SEQ + b
    return pl.pallas_call(
        functools.partial(_lat_na_kernel, half),
        grid=(DEC_BATCH,),
        in_specs=[pl.BlockSpec((NA_HALF_Q, BR_W), lambda b: (qrow(b), COL_CQ // BR_W)),
                  pl.BlockSpec((DEC_SEQ, BR_W), lambda b: (srow(b), COL_CK // BR_W)),
                  pl.BlockSpec((DEC_SEQ, BR_W), lambda b: (srow(b), COL_CV // BR_W)),
                  pl.BlockSpec((1, 1, C_HEADS, HD, PAST_LEN), lambda b: (b, l, 0, 0, 0)),
                  pl.BlockSpec((1, 1, C_HEADS, HD, PAST_LEN), lambda b: (b, l, 0, 0, 0)),
                  pl.BlockSpec((1, C_HEADS, 1, NA_HALF_Q, NA_KSPAN), lambda b: (l, 0, half, 0, 0)),
                  pl.BlockSpec((NA_HALF_Q, BR_W), lambda b: (qrow(b), GATE_C)),
                  pl.BlockSpec((DEC_SEQ, BR_W), lambda b: (srow(b), 0)),
                  pl.BlockSpec((DEC_SEQ, BR_W), lambda b: (srow(b), GATE_R))]
                 + _lru_specs(l, lambda b: (l, 0, 0))
                 + [pl.BlockSpec((1, 1, 2, BR_W), lambda b: (b, l, 0, 0))],
        out_specs=[pl.BlockSpec((NA_HALF_Q, BR_W), lambda b: (b, 0)),
                   pl.BlockSpec((DEC_SEQ, BR_W), lambda b: (b, 0))],
        out_shape=[jax.ShapeDtypeStruct((N_LAT // 2, BR_W), BF16), jax.ShapeDtypeStruct((N_LAT, BR_W), F32)],
        compiler_params=_cparams(("arbitrary",)),
        name=f"lat_na{half}_l{l}",
    )(proj, proj, proj, cache_k, cache_v, bias, proj, bx, proj, *lru_params, state_lru)


def _lat_swa_kernel(dq_ref, dk_ref, dv_ref, kc_ref, vc_ref, sink_ref, gd_ref, zd_ref):
    nq = DEC_SEQ // SWA_QB
    rows = D_GROUP * SWA_QB
    dq, dk, dv = dq_ref[...], dk_ref[...], dv_ref[...]
    sink2 = sink_ref[0] * LOG2E
    m_hd = _lane_groups(D_HEADS, DEC_SEQ)
    starts = [min(max(j * SWA_QB - WIN, 0), DEC_SEQ - SWA_SPAN) for j in range(nq)]
    qi = lax.broadcasted_iota(jnp.int32, (rows, SWA_SPAN), 0) % SWA_QB
    ki = lax.broadcasted_iota(jnp.int32, (rows, SWA_SPAN), 1)
    masks = {off: jnp.abs(qi + off - ki) <= WIN for off in sorted({j * SWA_QB - s for j, s in enumerate(starts)})}
    kc_t = jnp.concatenate([kc_ref[0, 0, kv] for kv in range(D_KV) for _ in range(D_GROUP)], axis=0).astype(BF16)
    vc_t = jnp.concatenate([vc_ref[0, 0, kv] for kv in range(D_KV) for _ in range(D_GROUP)], axis=0).astype(BF16)
    feat = lax.broadcasted_iota(jnp.int32, (BR_W, 1), 0) // (D_GROUP * HD)

    qm = [_only(dq, m) for m in m_hd]
    sks = [_sink_col(sink2, kv, SWA_QB) for kv in range(D_KV)]
    w_loc, w_ctx = [], []
    for kv in range(D_KV):
        w_loc.append(_ones_outside(dv, m_hd[kv * D_GROUP] | m_hd[kv * D_GROUP + 1]))
        w_ctx.append(jnp.where(feat == kv, vc_t, jnp.ones_like(vc_t)))

    def scores(c):
        kv, j = divmod(c, nq)
        q2 = jnp.concatenate([qm[kv * D_GROUP + g][j * SWA_QB:(j + 1) * SWA_QB] for g in range(D_GROUP)], axis=0)
        return [_dot(q2, kc_t),
                jnp.where(masks[j * SWA_QB - starts[j]], _dot_nt(q2, dk[starts[j]:starts[j] + SWA_SPAN]), NEG)]

    def pv(c, s):
        kv, j = divmod(c, nq)
        (e_ctx, e_loc), m = _exp2_parts(s, extra=sks[kv])
        p = _dot_nt(e_ctx, w_ctx[kv]) + _dot(e_loc, w_loc[kv][starts[j]:starts[j] + SWA_SPAN])
        return p * (1.0 / (_swap_halves(p) + jnp.exp2(sks[kv] - m)))

    o = _pipelined(D_KV * nq, scores, pv)
    os = [jnp.concatenate([o[kv * nq + j][g * SWA_QB:(g + 1) * SWA_QB] for j in range(nq)], axis=0)
          for kv in range(D_KV) for g in range(D_GROUP)]
    zd_ref[...] = (_pick_heads(os) * gd_ref[...].astype(F32)).astype(BF16)


def _lat_swa(proj, cache_k, cache_v, sink3, l):
    srow = lambda b: (N_CTX // DEC_SEQ + b)
    return pl.pallas_call(
        _lat_swa_kernel,
        grid=(DEC_BATCH,),
        in_specs=[pl.BlockSpec((DEC_SEQ, BR_W), lambda b: (srow(b), COL_DQ // BR_W)),
                  pl.BlockSpec((DEC_SEQ, BR_W), lambda b: (srow(b), COL_DK // BR_W)),
                  pl.BlockSpec((DEC_SEQ, BR_W), lambda b: (srow(b), COL_DV // BR_W)),
                  pl.BlockSpec((1, 1, D_KV, HD, PAST_LEN), lambda b: (b, l, 0, 0, 0)),
                  pl.BlockSpec((1, 1, D_KV, HD, PAST_LEN), lambda b: (b, l, 0, 0, 0)),
                  pl.BlockSpec((1, 1, D_HEADS), lambda b: (l, 0, 0)),
                  pl.BlockSpec((DEC_SEQ, BR_W), lambda b: (srow(b), GATE_D))],
        out_specs=pl.BlockSpec((DEC_SEQ, BR_W), lambda b: (b, 0)),
        out_shape=jax.ShapeDtypeStruct((N_LAT, BR_W), BF16),
        compiler_params=_cparams(("arbitrary",)),
        name=f"lat_swa_l{l}",
    )(proj, proj, proj, cache_k, cache_v, sink3, proj)


def _lru_conv(x, cw, cb):
    seq = x.shape[0]
    x3 = x.reshape(seq // SUBLANES, SUBLANES, BR_W)
    tin = lax.broadcasted_iota(jnp.int32, (1, SUBLANES, 1), 1)
    zero = jnp.zeros((1, SUBLANES, BR_W), F32)

    def at(shift):
        r = pltpu.roll(x3, (-shift) % SUBLANES, 1)
        if shift < 0:
            return jnp.where(tin < -shift, jnp.concatenate([zero, r[:-1]], axis=0), r)
        return jnp.where(tin >= SUBLANES - shift, jnp.concatenate([r[1:], zero], axis=0), r)

    xc = cb + at(-1) * cw[0:1] + x3 * cw[1:2] + at(1) * cw[2:3] + at(2) * cw[3:4]
    return xc.reshape(seq, BR_W)


def _gate_dense(w_ref, d):
    rows = []
    for n in range(B_BLOCKS):
        pieces = [jnp.zeros((B_BLK, n * B_BLK), BF16)] if n else []
        pieces.append(w_ref[0, d, n].astype(BF16))
        if n < B_BLOCKS - 1:
            pieces.append(jnp.zeros((B_BLK, (B_BLOCKS - 1 - n) * B_BLK), BF16))
        rows.append(jnp.concatenate(pieces, axis=1))
    return jnp.concatenate(rows, axis=0)


def _lru_gates(d, xc, wa_ref, ba_ref, wx_ref, bxg_ref):
    xb = xc.astype(BF16)
    return (_dot(xb, _gate_dense(wa_ref, d)) + ba_ref[0, d:d + 1, :],
            _dot(xb, _gate_dense(wx_ref, d)) + bxg_ref[0, d:d + 1, :])


def _lru_scan(d, xc, pre, lam_row, h0):
    seq = xc.shape[0]
    nl = -lam_row
    softplus = jnp.maximum(nl, 0.0) + jnp.log1p(jnp.exp(-jnp.abs(nl)))
    r = jax.nn.sigmoid(pre[0])
    ig = jax.nn.sigmoid(pre[1])
    log_a = -LRU_C * r * softplus
    a = jnp.exp(log_a)
    u = jnp.exp(0.5 * jnp.log(-jnp.tanh(log_a) * (a * a + 1.0))) * (ig * xc)
    nt = seq // SUBLANES
    tin = lax.broadcasted_iota(jnp.int32, (1, SUBLANES, 1), 1)
    a = a.reshape(nt, SUBLANES, BR_W)
    u = u.reshape(nt, SUBLANES, BR_W)
    s = 1
    while s < SUBLANES:
        if d == 0:
            keep = tin >= s
            a_n = jnp.where(keep, pltpu.roll(a, s, 1), 1.0)
            u_n = jnp.where(keep, pltpu.roll(u, s, 1), 0.0)
        else:
            keep = tin < SUBLANES - s
            a_n = jnp.where(keep, pltpu.roll(a, SUBLANES - s, 1), 1.0)
            u_n = jnp.where(keep, pltpu.roll(u, SUBLANES - s, 1), 0.0)
        u = u + a * u_n
        a = a * a_n
        s *= 2
    ys = [None] * nt
    h = h0
    for k in (range(nt) if d == 0 else range(nt - 1, -1, -1)):
        y = u[k] + a[k] * h
        ys[k] = y
        h = y[SUBLANES - 1:, :] if d == 0 else y[:1, :]
    return jnp.concatenate(ys, axis=0), h


def _lru_specs(l, imap):
    wspec = pl.BlockSpec((1, 2, B_BLOCKS, B_BLK, B_BLK), lambda *g: imap(*g) + (0, 0))
    vspec = pl.BlockSpec((1, 2, BR_W), imap)
    return [pl.BlockSpec((1, CONV_W, BR_W), imap), pl.BlockSpec((1, 1, BR_W), imap),
            wspec, vspec, wspec, vspec, vspec]


MERGE_COLS = 512

def _merge_kernel(layer, final, xa_ref, xb_ref, mod_ref, h_ref, zctx_ref, zal_ref, yf_ref, yb_ref,
                  zc0_ref, zc1_ref, zdl_ref, wmg_hbm, bmg_ref, wbo_ref, wo_ref, nf_ref, *refs):
    out_refs, (wmg_s, wbo_s, wo_s, stage, sem) = refs[:-5], refs[-5:]
    i = pl.program_id(0)

    def wmg_copy(chunk, n):
        slot = chunk % 2
        return pltpu.make_async_copy(wmg_hbm.at[layer, pl.ds(chunk * W_ROWS, W_ROWS), n, :],
                                     stage.at[slot, n], sem.at[slot, n])

    @pl.when(i == 0)
    def _():
        for n in range(N_BRANCH):
            wmg_copy(0, n).start()

    @pl.when(i < W_STEPS - 1)
    def _():
        for n in range(N_BRANCH):
            wmg_copy(i + 1, n).start()

    @pl.when(i < W_STEPS)
    def _():
        r0 = pl.multiple_of(i * W_ROWS, W_ROWS)
        for n in range(N_BRANCH):
            wmg_copy(i, n).wait()
            wmg_s[pl.ds(r0, W_ROWS), n * D_MODEL:(n + 1) * D_MODEL] = stage[i % 2, n].astype(BF16)
        _cast_rows(i, wbo_ref, wbo_s)
        _cast_rows(i, wo_ref, wo_s)

    @pl.when(i >= W_STEPS)
    def _():
        t = i - W_STEPS
        is_ctx = t < N_CTX_TILES
        x = jnp.where(is_ctx, xa_ref[...], xb_ref[...])
        gate = mod_ref[0][:, 2 * D_MODEL:]
        h = h_ref[...]
        second_half = (t - N_CTX_TILES) % LAT_TILES_PER_SEQ == 1
        z_lat = [zal_ref[...],
                 (yf_ref[...] + yb_ref[...]).astype(BF16),
                 jnp.where(second_half, zc1_ref[...], zc0_ref[...]),
                 zdl_ref[...]]
        zs = [jnp.where(is_ctx, zctx_ref[:, n * BR_W:(n + 1) * BR_W], z_lat[n]) for n in range(N_BRANCH)]
        bmg = bmg_ref[0]
        cols = []
        for c in range(0, D_MODEL, MERGE_COLS):
            acc = None
            for n in range(N_BRANCH):
                g = jax.nn.sigmoid(_dot(h, wmg_s[:, n * D_MODEL + c:n * D_MODEL + c + MERGE_COLS])
                                   + bmg[n:n + 1, c:c + MERGE_COLS])
                term = g * _dot(zs[n], wbo_s[n * BR_W:(n + 1) * BR_W, c:c + MERGE_COLS])
                acc = term if acc is None else acc + term
            cols.append(acc.astype(BF16))
        merged = jnp.concatenate(cols, axis=1)
        xn = x + gate * _dot(merged, wo_s[...])
        if not final:
            out_refs[0][...] = xn
        else:
            y = xn * lax.rsqrt(jnp.mean(xn * xn, axis=-1, keepdims=True) + EPS) * nf_ref[...]

            @pl.when(t < N_CTX_TILES)
            def _():
                out_refs[0][...] = y

            @pl.when(t >= N_CTX_TILES)
            def _():
                out_refs[1][...] = y


def _merge(final, xa, xb, xb_off, mod3, l, h, z_ctx, za_lat, y_f, y_b, zc_lat, zd_lat,
           w_mg, b_mg, w_bo, w_o, norm_f):
    assert TM == NA_HALF_Q
    row = lambda i: (_tile(i), 0)
    ctx_blk = lambda i: jnp.minimum(_tile(i), N_CTX_TILES - 1)
    lat_blk = lambda i: jnp.maximum(_tile(i) - N_CTX_TILES, 0)
    lat_spec = pl.BlockSpec((TM, BR_W), lambda i: (lat_blk(i), 0))
    half_spec = pl.BlockSpec((TM, BR_W), lambda i: (lat_blk(i) // LAT_TILES_PER_SEQ, 0))
    if final:
        out_specs = [pl.BlockSpec((TM, D_MODEL), lambda i: (ctx_blk(i), 0)),
                     pl.BlockSpec((TM, D_MODEL), lambda i: (lat_blk(i), 0))]
        out_shape = [jax.ShapeDtypeStruct((N_CTX, D_MODEL), F32), jax.ShapeDtypeStruct((N_LAT, D_MODEL), F32)]
    else:
        out_specs = pl.BlockSpec((TM, D_MODEL), row)
        out_shape = jax.ShapeDtypeStruct((N_TOK, D_MODEL), F32)
    wchunk = lambda width: pl.BlockSpec((1, W_ROWS, width), lambda i: (l, _wchunk(i), 0))
    return pl.pallas_call(
        functools.partial(_merge_kernel, l, final),
        grid=(W_STEPS + N_TILES,),
        in_specs=[
            pl.BlockSpec((TM, D_MODEL), lambda i: (ctx_blk(i), 0)),
            pl.BlockSpec((TM, D_MODEL), lambda i: (lat_blk(i) + xb_off, 0)),
            pl.BlockSpec((1, 1, 3 * D_MODEL), lambda i: (l * MOD_ROWS + _mod_row(_tile(i)), 0, 0)),
            pl.BlockSpec((TM, D_MODEL), row),
            pl.BlockSpec((TM, N_BRANCH * BR_W), lambda i: (ctx_blk(i), 0)),
            lat_spec,
            lat_spec, lat_spec,
            half_spec, half_spec,
            lat_spec,
            pl.BlockSpec(memory_space=pl.ANY),
            pl.BlockSpec((1, N_BRANCH, D_MODEL), lambda i: (l, 0, 0)),
            wchunk(D_MODEL),
            wchunk(D_MODEL),
            pl.BlockSpec((1, D_MODEL), lambda i: (0, 0)),
        ],
        out_specs=out_specs,
        out_shape=out_shape,
        scratch_shapes=[pltpu.VMEM((D_MODEL, N_BRANCH * D_MODEL), BF16),
                        pltpu.VMEM((N_BRANCH * BR_W, D_MODEL), BF16),
                        pltpu.VMEM((D_MODEL, D_MODEL), BF16),
                        pltpu.VMEM((2, N_BRANCH, W_ROWS, D_MODEL), F32),
                        pltpu.SemaphoreType.DMA((2, N_BRANCH))],
        compiler_params=_cparams(("arbitrary",)),
        name=f"merge_l{l}",
    )(xa, xb, mod3, h, z_ctx, za_lat, y_f, y_b, zc_lat[0], zc_lat[1], zd_lat,
      w_mg, b_mg,
      w_bo.reshape(DEPTH, N_BRANCH * BR_W, D_MODEL), w_o, norm_f.reshape(1, D_MODEL))


def kernel(x_prompt, x_sample, cache_diff_k, cache_diff_v, cache_na_k, cache_na_v, cache_swa_k, cache_swa_v,
           state_lru, c, c_ctx, norm_g, w_ada, b_ada, w_in, diff_lambda, diff_norm_g, conv_w, conv_b,
           lru_wa, lru_ba, lru_wx, lru_bx, lru_lam, na_rpb, swa_sink, w_mg, b_mg, w_bo, w_o, norm_f):
    tab_a, tab_d = jnp.asarray(_rope_tables(A_QK)), jnp.asarray(_rope_tables(HD))

    cvecs = jnp.concatenate([c_ctx[None, :], c, jnp.zeros((MOD_ROWS - 1 - DEC_BATCH, D_MODEL), F32)], axis=0)
    mod3 = _modulation(cvecs, w_ada, b_ada).reshape(DEPTH * MOD_ROWS, 1, 3 * D_MODEL)
    bias = _na_bias(na_rpb)
    norm_g3 = norm_g.reshape(DEPTH, 1, D_MODEL)
    dg3 = diff_norm_g.reshape(DEPTH, 1, HD)
    dg4 = jnp.tile(diff_norm_g, (1, A_HEADS)).reshape(DEPTH, 1, BR_W)
    sink3 = swa_sink.reshape(DEPTH, 1, D_HEADS)
    conv_b3 = conv_b.reshape(DEPTH, 1, BR_W)
    past = [jnp.swapaxes(t, -1, -2) for t in
            (cache_diff_k, cache_diff_v, cache_na_k, cache_na_v, cache_swa_k, cache_swa_v)]

    xa = x_prompt.reshape(N_CTX, D_MODEL)
    xb = x_sample.reshape(N_LAT, D_MODEL)
    xb_off = 0
    caches = states = None
    y_p = y_s = None
    for l in range(DEPTH):
        h, proj, bx, caches = _inproj(xa, xb, xb_off, mod3, l, norm_g3, w_in, tab_a, tab_d, caches)

        lru_params = (conv_w, conv_b3, lru_wa, lru_ba, lru_wx, lru_bx, lru_lam)
        z_ctx, states = _ctx_attn(proj, bx, l, diff_lambda, dg4, sink3, lru_params, states)
        za_lat = _lat_diff(proj, past[0], past[1], l, diff_lambda, dg3)
        zc0, y_f = _lat_na(0, proj, bx, past[2], past[3], bias, l, lru_params, state_lru)
        zc1, y_b = _lat_na(1, proj, bx, past[2], past[3], bias, l, lru_params, state_lru)
        zd_lat = _lat_swa(proj, past[4], past[5], sink3, l)

        final = l == DEPTH - 1
        out = _merge(final, xa, xb, xb_off, mod3, l, h, z_ctx, za_lat, y_f, y_b, (zc0, zc1), zd_lat,
                     w_mg, b_mg, w_bo, w_o, norm_f)
        if final:
            y_p, y_s = out
        else:
            xa = xb = out
            xb_off = N_CTX_TILES

    new_caches = [jnp.swapaxes(t, -1, -2) for t in caches]
    return (y_p.reshape(BATCH, SEQ, D_MODEL), y_s.reshape(DEC_BATCH, DEC_SEQ, D_MODEL), *new_caches, states)
```

```python
import functools
import math

import numpy as np
import jax
import jax.numpy as jnp
from jax import lax
from jax.experimental import pallas as pl
from jax.experimental.pallas import tpu as pltpu

F32 = jnp.float32
BF16 = jnp.bfloat16

D_MODEL = 1024
BATCH = 16
SEQ = 256
DEPTH = 2
DEC_BATCH = 8
DEC_SEQ = 1024
PAST_LEN = 512
GRID_W = 64
N_BRANCH = 4
BR_W = D_MODEL // 4
HD = 64
A_HEADS = BR_W // HD
A_QK = HD // 2
B_BLOCKS = 4
B_BLK = BR_W // B_BLOCKS
CONV_W = 4
LRU_C = 8.0
C_HEADS = BR_W // HD
NA_ROWS = 8
NA_COLS = 16
D_HEADS = BR_W // HD
D_KV = 2
D_GROUP = D_HEADS // D_KV
WIN = 128
ROPE_BASE = 10000.0
EPS = 1e-6
NEG = -1e30

LANES = 128
SUBLANES = 8
V7X_VMEM_BYTES = 64 * 1024 * 1024

N_CTX = BATCH * SEQ
N_LAT = DEC_BATCH * DEC_SEQ
N_TOK = N_CTX + N_LAT
GRID_ROWS = DEC_SEQ // GRID_W

TM = 512
N_CTX_TILES = N_CTX // TM
N_TILES = N_TOK // TM
LAT_TILES_PER_SEQ = DEC_SEQ // TM

COL_G = 0
GATE_A, GATE_R, GATE_C, GATE_D = (COL_G // BR_W + k for k in range(N_BRANCH))
COL_AQ = COL_G + N_BRANCH * BR_W
COL_AK = COL_AQ + BR_W
COL_AV = COL_AK + BR_W
COL_CQ = COL_AV + BR_W
COL_CK = COL_CQ + BR_W
COL_CV = COL_CK + BR_W
COL_DQ = COL_CV + BR_W
COL_DK = COL_DQ + BR_W
COL_DV = COL_DK + BR_W
PROJ_W = COL_DV + BR_W
IN_W = (N_BRANCH + 3 + 1 + 3 + 1) * BR_W + 2 * D_KV * HD

assert GRID_ROWS == 2 * NA_ROWS
NA_HALF_Q = DEC_SEQ // 2
NA_KROWS = NA_ROWS + NA_ROWS // 2
NA_KSPAN = NA_KROWS * GRID_W
NA_HALF_KSTART = (GRID_ROWS - NA_KROWS) * GRID_W

SWA_QB = 256
SWA_SPAN = SWA_QB + 2 * WIN

LOG2E = math.log2(math.e)
QA_SCALE = A_QK ** -0.5 * LOG2E
QH_SCALE = HD ** -0.5 * LOG2E

VMEM_LIMIT = V7X_VMEM_BYTES - 8 * 1024 * 1024

MOD_ROWS = -(-(1 + DEC_BATCH) // SUBLANES) * SUBLANES
MOD_COLS = 6 * LANES


def _cparams(sem):
    return pltpu.CompilerParams(dimension_semantics=sem, vmem_limit_bytes=VMEM_LIMIT)


def _dot(a, b):
    return jnp.dot(a, b, preferred_element_type=F32)


def _dot_nt(a, b):
    return lax.dot_general(a, b, (((1,), (1,)), ((), ())), preferred_element_type=F32)


def _silu(x):
    return x * jax.nn.sigmoid(x)


def _rope_tables(d):
    half = d // 2
    quarter = half // 2
    lane = np.arange(LANES)
    q = lane % d
    use_col = (q >= half)
    i = (q % half) % quarter
    first = (q % half) < quarter
    inv = (ROPE_BASE ** (-np.arange(quarter, dtype=np.float32) / np.float32(quarter))).astype(np.float32)
    t = np.arange(DEC_SEQ)
    pos = np.where(use_col[None, :], (t % GRID_W)[:, None], (t // GRID_W)[:, None]).astype(np.float32)
    ang = (pos * inv[i][None, :]).astype(np.float32)
    cos, sin = np.cos(ang), np.sin(ang)
    c = np.concatenate([np.ones((TM, LANES), np.float32), cos], axis=0)
    s1 = np.concatenate([np.zeros((TM, LANES), np.float32), np.where(first[None, :], -sin, 0.0)], axis=0)
    s2 = np.concatenate([np.zeros((TM, LANES), np.float32), np.where(first[None, :], 0.0, sin)], axis=0)
    return np.stack([c, s1, s2]).astype(np.float32)


def _rope_lanes(x, tab_ref, shift):
    return (x * tab_ref[0] + pltpu.roll(x, LANES - shift, 1) * tab_ref[1]
            + pltpu.roll(x, shift, 1) * tab_ref[2])


def _rope(x, tab_ref, shift):
    w = x.shape[1]
    return jnp.concatenate([_rope_lanes(x[:, c:c + LANES], tab_ref, shift) for c in range(0, w, LANES)], axis=1)


def _mod_kernel(c_ref, w_ref, b_ref, o_ref):
    c = c_ref[...]
    o_ref[0] = _dot(_silu(c).astype(BF16), w_ref[0].astype(BF16)) + b_ref[0]


def _modulation(cvecs, w_ada, b_ada):
    nb = 3 * D_MODEL // MOD_COLS
    return pl.pallas_call(
        _mod_kernel,
        grid=(DEPTH, nb),
        in_specs=[pl.BlockSpec((MOD_ROWS, D_MODEL), lambda l, j: (0, 0)),
                  pl.BlockSpec((1, D_MODEL, MOD_COLS), lambda l, j: (l, 0, j)),
                  pl.BlockSpec((1, 1, MOD_COLS), lambda l, j: (l, 0, j))],
        out_specs=pl.BlockSpec((1, MOD_ROWS, MOD_COLS), lambda l, j: (l, 0, j)),
        out_shape=jax.ShapeDtypeStruct((DEPTH, MOD_ROWS, 3 * D_MODEL), F32),
        compiler_params=_cparams(("arbitrary", "arbitrary")),
        name="modulation",
    )(cvecs, w_ada, b_ada.reshape(DEPTH, 1, 3 * D_MODEL))


W_STEPS = 4
W_ROWS = D_MODEL // W_STEPS


def _tile(i):
    return jnp.maximum(i - W_STEPS, 0)


def _wchunk(i):
    return jnp.minimum(i, W_STEPS - 1)


def _mod_row(t):
    return jnp.where(t < N_CTX_TILES, 0, 1 + (t - N_CTX_TILES) // LAT_TILES_PER_SEQ)


def _rope_blk(t):
    return jnp.where(t < N_CTX_TILES, 0, 1 + (t - N_CTX_TILES) % LAT_TILES_PER_SEQ)


def _cast_rows(i, src_ref, dst_ref):
    r0 = pl.multiple_of(i * W_ROWS, W_ROWS)
    dst_ref[pl.ds(r0, W_ROWS), :] = src_ref[0].astype(BF16)


WI_G = 0
WI_AQ = WI_G + N_BRANCH * BR_W
WI_AK, WI_AV, WI_BX, WI_CQ, WI_CKV = (WI_AQ + k * BR_W for k in range(1, 6))
WI_DQ = WI_CKV + 2 * BR_W
WI_DKV = WI_DQ + BR_W
SEQ_PER_TILE = TM // SEQ
IN_ROWS = 256


def _store_heads(ref, val, n_heads, width):
    vt = val.T
    for s in range(SEQ_PER_TILE):
        for h in range(n_heads):
            ref[s, 0, h] = vt[h * width:(h + 1) * width, s * SEQ:(s + 1) * SEQ]


def _per_query_head(x):
    return jnp.concatenate([x[:, kv * HD:(kv + 1) * HD] for kv in range(D_KV) for _ in range(D_GROUP)], axis=1)


def _inproj_kernel(n_alias, layer, xa_ref, xb_ref, mod_ref, g_ref, w_hbm, ta_ref, td_ref, *refs):
    h_ref, proj_ref, bx_ref, dk_ref, dv_ref, nk_ref, nv_ref, sk_ref, sv_ref, w_s, stage, sem = refs[n_alias:]
    i = pl.program_id(0)

    def w_copy(chunk):
        return pltpu.make_async_copy(w_hbm.at[layer, pl.ds(chunk * W_ROWS, W_ROWS), :],
                                     stage.at[chunk], sem.at[chunk])

    @pl.when(i == 0)
    def _():
        for c in range(W_STEPS):
            w_copy(c).start()

    @pl.when(i < W_STEPS)
    def _():
        w_copy(i).wait()
        r0 = pl.multiple_of(i * W_ROWS, W_ROWS)
        w_s[pl.ds(r0, W_ROWS), :] = stage[i].astype(BF16)

    @pl.when(i >= W_STEPS)
    def _():
        t = i - W_STEPS
        is_ctx = t < N_CTX_TILES
        mod = mod_ref[0]
        shift, scale = mod[:, :D_MODEL], mod[:, D_MODEL:2 * D_MODEL]
        w2 = 2 * BR_W
        kept = []
        for r in range(0, TM, IN_ROWS):
            rs = pl.ds(r, IN_ROWS)
            ta, td = ta_ref.at[:, rs, :], td_ref.at[:, rs, :]
            x = jnp.where(is_ctx, xa_ref[rs, :], xb_ref[rs, :])
            xn = x * lax.rsqrt(jnp.mean(x * x, axis=-1, keepdims=True) + EPS) * g_ref[0]
            hb = (xn * (1.0 + scale) + shift).astype(BF16)
            h_ref[rs, :] = hb

            def mm(c0, width):
                return _dot(hb, w_s[:, c0:c0 + width])

            gs = [mm(WI_G + c, w2) for c in range(0, N_BRANCH * BR_W, w2)]
            aq, ak, av = mm(WI_AQ, BR_W), mm(WI_AK, BR_W), mm(WI_AV, BR_W)
            cq, ckv = mm(WI_CQ, BR_W), mm(WI_CKV, w2)
            dq, dkv = mm(WI_DQ, BR_W), mm(WI_DKV, 2 * D_KV * HD)
            dk, dv = _rope(dkv[:, :D_KV * HD], td, HD // 4), dkv[:, D_KV * HD:]
            bx_ref[rs, :] = mm(WI_BX, BR_W)

            for n, g in enumerate(gs):
                proj_ref[rs, COL_G + n * w2:COL_G + (n + 1) * w2] = _silu(g).astype(BF16)
            proj_ref[rs, COL_AQ:COL_AQ + BR_W] = _rope(aq * QA_SCALE, ta, A_QK // 4).astype(BF16)
            proj_ref[rs, COL_AK:COL_AK + BR_W] = _rope(ak, ta, A_QK // 4).astype(BF16)
            proj_ref[rs, COL_AV:COL_AV + BR_W] = av.astype(BF16)
            proj_ref[rs, COL_CQ:COL_CQ + BR_W] = (cq * QH_SCALE).astype(BF16)
            proj_ref[rs, COL_CK:COL_CK + w2] = ckv.astype(BF16)
            proj_ref[rs, COL_DQ:COL_DQ + BR_W] = _rope(dq * QH_SCALE, td, HD // 4).astype(BF16)
            proj_ref[rs, COL_DK:COL_DK + BR_W] = _per_query_head(dk.astype(BF16))
            proj_ref[rs, COL_DV:COL_DV + BR_W] = _per_query_head(dv.astype(BF16))
            kept.append((ak, av, ckv, dk, dv))

        @pl.when(is_ctx)
        def _():
            ak, av, ckv, dk, dv = (jnp.concatenate(v, axis=0) for v in zip(*kept))
            akt = ak.T
            for s in range(SEQ_PER_TILE):
                for h in range(A_HEADS):
                    for m in range(2):
                        c0 = h * HD + m * A_QK
                        dk_ref[s, 0, h, m] = akt[c0:c0 + A_QK, s * SEQ:(s + 1) * SEQ]
            _store_heads(dv_ref, av, A_HEADS, HD)
            _store_heads(nk_ref, ckv[:, :BR_W], C_HEADS, HD)
            _store_heads(nv_ref, ckv[:, BR_W:], C_HEADS, HD)
            _store_heads(sk_ref, dk, D_KV, HD)
            _store_heads(sv_ref, dv, D_KV, HD)


def _cache_shapes():
    hs = lambda n: (BATCH, DEPTH, n, HD, SEQ)
    return [(BATCH, DEPTH, A_HEADS, 2, A_QK, SEQ), hs(A_HEADS), hs(C_HEADS), hs(C_HEADS), hs(D_KV), hs(D_KV)]


def _inproj(xa, xb, xb_off, mod3, l, norm_g, w_in, tab_a, tab_d, caches):
    row = lambda i: (_tile(i), 0)
    ctx_blk = lambda i: jnp.minimum(_tile(i), N_CTX_TILES - 1)

    def cache_spec(shape):
        blk = (SEQ_PER_TILE, 1) + shape[2:]
        nz = len(shape) - 2
        return pl.BlockSpec(blk, lambda i: (ctx_blk(i), l) + (0,) * nz)

    cshapes = _cache_shapes()
    aliases = {} if caches is None else {7 + k: 3 + k for k in range(6)}
    extra_specs = [] if caches is None else [pl.BlockSpec(memory_space=pl.ANY)] * 6
    extra_args = [] if caches is None else list(caches)
    outs = pl.pallas_call(
        functools.partial(_inproj_kernel, len(extra_args), l),
        grid=(W_STEPS + N_TILES,),
        in_specs=[
            pl.BlockSpec((TM, D_MODEL), lambda i: (ctx_blk(i), 0)),
            pl.BlockSpec((TM, D_MODEL), lambda i: (jnp.maximum(_tile(i) - N_CTX_TILES, 0) + xb_off, 0)),
            pl.BlockSpec((1, 1, 3 * D_MODEL), lambda i: (l * MOD_ROWS + _mod_row(_tile(i)), 0, 0)),
            pl.BlockSpec((1, 1, D_MODEL), lambda i: (l, 0, 0)),
            pl.BlockSpec(memory_space=pl.ANY),
            pl.BlockSpec((3, TM, LANES), lambda i: (0, _rope_blk(_tile(i)), 0)),
            pl.BlockSpec((3, TM, LANES), lambda i: (0, _rope_blk(_tile(i)), 0)),
        ] + extra_specs,
        out_specs=[
            pl.BlockSpec((TM, D_MODEL), row),
            pl.BlockSpec((TM, PROJ_W), row),
            pl.BlockSpec((TM, BR_W), row),
        ] + [cache_spec(s) for s in cshapes],
        out_shape=[
            jax.ShapeDtypeStruct((N_TOK, D_MODEL), BF16),
            jax.ShapeDtypeStruct((N_TOK, PROJ_W), BF16),
            jax.ShapeDtypeStruct((N_TOK, BR_W), F32),
        ] + [jax.ShapeDtypeStruct(s, F32) for s in cshapes],
        scratch_shapes=[pltpu.VMEM((D_MODEL, IN_W), BF16),
                        pltpu.VMEM((W_STEPS, W_ROWS, IN_W), F32),
                        pltpu.SemaphoreType.DMA((W_STEPS,))],
        input_output_aliases=aliases,
        compiler_params=_cparams(("arbitrary",)),
        name=f"inproj_l{l}",
    )(xa, xb, mod3, norm_g, w_in, tab_a, tab_d, *extra_args)
    return outs[0], outs[1], outs[2], tuple(outs[3:])


def _diff_lambda(lam_ref, lam_init):
    lv = lam_ref[0]
    s1 = jnp.sum(lv[0:1] * lv[1:2], axis=-1, keepdims=True)
    s2 = jnp.sum(lv[2:3] * lv[3:4], axis=-1, keepdims=True)
    return jnp.exp(s1) - jnp.exp(s2) + lam_init


def _diff_norm(o, dg_ref, lam_init):
    y = o * lax.rsqrt(jnp.mean(o * o, axis=-1, keepdims=True) + EPS) * dg_ref[0]
    return y * (1.0 - lam_init)


def _exp2_parts(parts, extra=None):
    m = functools.reduce(jnp.maximum, [jnp.max(s, axis=-1, keepdims=True) for s in parts])
    if extra is not None:
        m = jnp.maximum(m, extra)
    return [jnp.exp2(s - m).astype(BF16) for s in parts], m


def _pipelined(n, score_fn, pv_fn):
    outs = []
    nxt = score_fn(0)
    for c in range(n):
        cur, nxt = nxt, (score_fn(c + 1) if c + 1 < n else None)
        outs.append(pv_fn(c, cur))
    return outs


def _with_ones(v):
    return jnp.concatenate([v, jnp.ones_like(v)], axis=1)


def _with_ones_t(vt):
    return jnp.concatenate([vt, jnp.ones((2 * SUBLANES, vt.shape[1]), vt.dtype)], axis=0)


def _sink_col(sink2, kv, rows):
    r = lax.broadcasted_iota(jnp.int32, (D_GROUP * rows, 1), 0)
    return jnp.where(r < rows, sink2[:, kv * D_GROUP:kv * D_GROUP + 1], sink2[:, kv * D_GROUP + 1:kv * D_GROUP + 2])


CTX_PER_STEP = 2


def _lane_groups(n, rows):
    grp = lax.broadcasted_iota(jnp.int32, (rows, BR_W), 1) // (BR_W // n)
    return [grp == p for p in range(n)]


def _only(x, m):
    return jnp.where(m, x, jnp.zeros_like(x))


def _ones_outside(v, m):
    return jnp.where(m, v, jnp.ones_like(v))


def _swap_halves(p):
    return jnp.concatenate([p[:, BR_W // 2:], p[:, :BR_W // 2]], axis=1)


def _pick_heads(os):
    grp = lax.broadcasted_iota(jnp.int32, os[0].shape, 1) // HD
    acc = os[0]
    for h in range(1, len(os)):
        acc = jnp.where(grp == h, os[h], acc)
    return acc


def _head_rms(o, dg4_ref, lam_init):
    r = lax.broadcasted_iota(jnp.int32, (BR_W, BR_W), 0) // HD
    c = lax.broadcasted_iota(jnp.int32, (BR_W, BR_W), 1) // HD
    ss = _dot((o * o).astype(BF16), jnp.where(r == c, 1.0, 0.0).astype(BF16))
    return o * lax.rsqrt(ss * (1.0 / HD) + EPS) * dg4_ref[0] * (1.0 - lam_init)


def _ctx_attn_kernel(n_alias, lam_init, aq_ref, ak_ref, av_ref, cq_ref, ck_ref, cv_ref, dq_ref, dk_ref, dv_ref,
                     ga_ref, gc_ref, gd_ref, lam_ref, dg4_ref, sink_ref,
                     bx_ref, gr_ref, cw_ref, cb_ref, wa_ref, ba_ref, wx_ref, bxg_ref, llam_ref, *refs):
    z_ref, fin_ref = refs[n_alias:]
    seqs = [slice(s * SEQ, (s + 1) * SEQ) for s in range(CTX_PER_STEP)]

    xcs = [_lru_conv(bx_ref[rs, :], cw_ref[0], cb_ref[0]) for rs in seqs]
    xc_all = jnp.concatenate(xcs, axis=0)
    pres = [_lru_gates(d, xc_all, wa_ref, ba_ref, wx_ref, bxg_ref) for d in range(2)]
    for s, rs in enumerate(seqs):
        h0 = jnp.zeros((1, BR_W), F32)
        yf, hf = _lru_scan(0, xcs[s], [p[rs] for p in pres[0]], llam_ref[0, 0:1, :], h0)
        yb, hb = _lru_scan(1, xcs[s], [p[rs] for p in pres[1]], llam_ref[0, 1:2, :], h0)
        z_ref[rs, BR_W:2 * BR_W] = ((yf + yb) * gr_ref[rs, :].astype(F32)).astype(BF16)
        fin_ref[s, 0, 0:1, :] = hf
        fin_ref[s, 0, 1:2, :] = hb

    lam = _diff_lambda(lam_ref, lam_init)
    sink2 = sink_ref[0] * LOG2E
    m_qk = _lane_groups(2 * A_HEADS, SEQ)
    m_hd = _lane_groups(C_HEADS, SEQ)

    sa, sc, sd = [], [], []
    for rs in seqs:
        aq, cq, dq = aq_ref[rs, :], cq_ref[rs, :], dq_ref[rs, :]
        ak, ck, dk = ak_ref[rs, :], ck_ref[rs, :], dk_ref[rs, :]
        sa.append([_dot_nt(_only(aq, m), ak) for m in m_qk])
        sc.append([_dot_nt(_only(cq, m), ck) for m in m_hd])
        sd.append([_dot_nt(_only(dq, m), dk) for m in m_hd])

    ea = [[_exp2_parts([s])[0][0] for s in ss] for ss in sa]
    ec = [[_exp2_parts([s])[0][0] for s in ss] for ss in sc]
    ed = [[_exp2_parts([s], extra=sink2[:, j:j + 1]) for j, s in enumerate(ss)] for ss in sd]

    for i, rs in enumerate(seqs):
        av, cv, dv = av_ref[rs, :], cv_ref[rs, :], dv_ref[rs, :]
        os = []
        for h in range(A_HEADS):
            w = _ones_outside(av, m_hd[h])
            p0, p1 = _dot(ea[i][2 * h], w), _dot(ea[i][2 * h + 1], w)
            os.append(p0 * (1.0 / _swap_halves(p0)) - p1 * (lam / _swap_halves(p1)))
        z_ref[rs, 0:BR_W] = (_head_rms(_pick_heads(os), dg4_ref, lam_init) * ga_ref[rs, :].astype(F32)).astype(BF16)

        os = []
        for h in range(C_HEADS):
            p = _dot(ec[i][h], _ones_outside(cv, m_hd[h]))
            os.append(p * (1.0 / _swap_halves(p)))
        z_ref[rs, 2 * BR_W:3 * BR_W] = (_pick_heads(os) * gc_ref[rs, :].astype(F32)).astype(BF16)

        os = []
        for j in range(D_HEADS):
            (e,), m = ed[i][j]
            p = _dot(e, _ones_outside(dv, m_hd[j]))
            os.append(p * (1.0 / (_swap_halves(p) + jnp.exp2(sink2[:, j:j + 1] - m))))
        z_ref[rs, 3 * BR_W:4 * BR_W] = (_pick_heads(os) * gd_ref[rs, :].astype(F32)).astype(BF16)


def _ctx_attn(proj, bx, l, diff_lambda, dg4, sink3, lru_params, states):
    lam_init = 0.8 - 0.6 * math.exp(-0.3 * l)
    colblk = lambda cb: pl.BlockSpec((CTX_PER_STEP * SEQ, BR_W), lambda b: (b, cb))
    extra = [] if states is None else [states]
    return pl.pallas_call(
        functools.partial(_ctx_attn_kernel, len(extra), lam_init),
        grid=(BATCH // CTX_PER_STEP,),
        in_specs=[colblk(COL_AQ // BR_W), colblk(COL_AK // BR_W), colblk(COL_AV // BR_W),
                  colblk(COL_CQ // BR_W), colblk(COL_CK // BR_W), colblk(COL_CV // BR_W),
                  colblk(COL_DQ // BR_W), colblk(COL_DK // BR_W), colblk(COL_DV // BR_W),
                  colblk(GATE_A), colblk(GATE_C), colblk(GATE_D),
                  pl.BlockSpec((1, 4, A_QK), lambda b: (l, 0, 0)),
                  pl.BlockSpec((1, 1, BR_W), lambda b: (l, 0, 0)),
                  pl.BlockSpec((1, 1, D_HEADS), lambda b: (l, 0, 0)),
                  pl.BlockSpec((CTX_PER_STEP * SEQ, BR_W), lambda b: (b, 0)), colblk(GATE_R)]
                 + _lru_specs(l, lambda b: (l, 0, 0)) + [pl.BlockSpec(memory_space=pl.ANY)] * len(extra),
        out_specs=[pl.BlockSpec((CTX_PER_STEP * SEQ, N_BRANCH * BR_W), lambda b: (b, 0)),
                   pl.BlockSpec((CTX_PER_STEP, 1, 2, BR_W), lambda b: (b, l, 0, 0))],
        out_shape=[jax.ShapeDtypeStruct((N_CTX, N_BRANCH * BR_W), BF16),
                   jax.ShapeDtypeStruct((BATCH, DEPTH, 2, BR_W), F32)],
        input_output_aliases={} if states is None else {24: 1},
        compiler_params=_cparams(("arbitrary",)),
        name=f"ctx_attn_l{l}",
    )(*([proj] * 12), diff_lambda, dg4, sink3, bx, proj, *lru_params, *extra)


LA_QB = 256


def _lat_diff_kernel(lam_init, aq_ref, ak_ref, av_ref, ck_ref, cv_ref, ga_ref, lam_ref, dg_ref, za_ref):
    lam = _diff_lambda(lam_ref, lam_init)
    aq = aq_ref[...]
    va_ctx = [_with_ones_t(cv_ref[0, 0, h].astype(BF16)) for h in range(A_HEADS)]
    va_loc = [_with_ones(av_ref[:, h * HD:(h + 1) * HD]) for h in range(A_HEADS)]

    def scores(c):
        h, m = divmod(c, 2)
        lo = h * HD + m * A_QK
        q = aq[:, lo:lo + A_QK]
        return [_dot(q, ck_ref[0, 0, h, m].astype(BF16)),
                _dot_nt(q, ak_ref[:, lo:lo + A_QK])]

    def pv(h, s0, s1):
        (c0, l0), _ = _exp2_parts(s0)
        (c1, l1), _ = _exp2_parts(s1)
        p = (_dot_nt(jnp.concatenate([c0, c1], axis=0), va_ctx[h])[:, :HD + 1]
             + _dot(jnp.concatenate([l0, l1], axis=0), va_loc[h])[:, :HD + 1])
        return p[:LA_QB], p[LA_QB:]

    ss = [scores(c) for c in range(2 * A_HEADS)]
    outs = []
    for h in range(A_HEADS):
        p0, p1 = pv(h, ss[2 * h], ss[2 * h + 1])
        o = p0[:, :HD] * (1.0 / p0[:, HD:HD + 1]) - p1[:, :HD] * (lam / p1[:, HD:HD + 1])
        outs.append(_diff_norm(o, dg_ref, lam_init))
    za_ref[...] = (jnp.concatenate(outs, axis=1) * ga_ref[...].astype(F32)).astype(BF16)


def _lat_diff(proj, cache_k, cache_v, l, diff_lambda, dg3):
    lam_init = 0.8 - 0.6 * math.exp(-0.3 * l)
    nq = DEC_SEQ // LA_QB
    qrow = lambda b, j: N_CTX // LA_QB + b * nq + j
    srow = lambda b, j: N_CTX // DEC_SEQ + b
    return pl.pallas_call(
        functools.partial(_lat_diff_kernel, lam_init),
        grid=(DEC_BATCH, nq),
        in_specs=[pl.BlockSpec((LA_QB, BR_W), lambda b, j: (qrow(b, j), COL_AQ // BR_W)),
                  pl.BlockSpec((DEC_SEQ, BR_W), lambda b, j: (srow(b, j), COL_AK // BR_W)),
                  pl.BlockSpec((DEC_SEQ, BR_W), lambda b, j: (srow(b, j), COL_AV // BR_W)),
                  pl.BlockSpec((1, 1, A_HEADS, 2, A_QK, PAST_LEN), lambda b, j: (b, l, 0, 0, 0, 0)),
                  pl.BlockSpec((1, 1, A_HEADS, HD, PAST_LEN), lambda b, j: (b, l, 0, 0, 0)),
                  pl.BlockSpec((LA_QB, BR_W), lambda b, j: (qrow(b, j), 0)),
                  pl.BlockSpec((1, 4, A_QK), lambda b, j: (l, 0, 0)),
                  pl.BlockSpec((1, 1, HD), lambda b, j: (l, 0, 0))],
        out_specs=pl.BlockSpec((LA_QB, BR_W), lambda b, j: (b * nq + j, 0)),
        out_shape=jax.ShapeDtypeStruct((N_LAT, BR_W), BF16),
        compiler_params=_cparams(("arbitrary", "arbitrary")),
        name=f"lat_diff_l{l}",
    )(proj, proj, proj, cache_k, cache_v, proj, diff_lambda, dg3)


def _na_bias_kernel(rpb_ref, o_ref):
    l, h = pl.program_id(0), pl.program_id(1)
    base = (l * C_HEADS + h) * ((2 * NA_ROWS - 1) * (2 * NA_COLS - 1))
    qc = lax.broadcasted_iota(jnp.int32, (GRID_W, GRID_W), 0)
    kc = lax.broadcasted_iota(jnp.int32, (GRID_W, GRID_W), 1)
    cs = jnp.clip(qc - NA_COLS // 2, 0, GRID_W - NA_COLS)
    ok = (kc >= cs) & (kc < cs + NA_COLS)
    dcol = kc - qc + (NA_COLS - 1)
    neg = jnp.full((GRID_W, GRID_W), NEG, F32)
    tabs = []
    for d in range(2 * NA_ROWS - 1):
        t = neg
        for dc in range(2 * NA_COLS - 1):
            t = jnp.where(ok & (dcol == dc), rpb_ref[base + d * (2 * NA_COLS - 1) + dc] * LOG2E, t)
        tabs.append(t)
    for half in range(2):
        for rq in range(NA_ROWS):
            r = half * NA_ROWS + rq
            rs = min(max(r - NA_ROWS // 2, 0), GRID_ROWS - NA_ROWS)
            blks = []
            for kr in range(NA_KROWS):
                kabs = half * (NA_HALF_KSTART // GRID_W) + kr
                blks.append(tabs[kabs - r + NA_ROWS - 1] if rs <= kabs < rs + NA_ROWS else neg)
            o_ref[0, 0, half, rq * GRID_W:(rq + 1) * GRID_W, :] = jnp.concatenate(blks, axis=1)


def _na_bias(na_rpb):
    return pl.pallas_call(
        _na_bias_kernel,
        grid=(DEPTH, C_HEADS),
        in_specs=[pl.BlockSpec(memory_space=pltpu.SMEM)],
        out_specs=pl.BlockSpec((1, 1, 2, NA_HALF_Q, NA_KSPAN), lambda l, h: (l, h, 0, 0, 0)),
        out_shape=jax.ShapeDtypeStruct((DEPTH, C_HEADS, 2, NA_HALF_Q, NA_KSPAN), F32),
        compiler_params=_cparams(("arbitrary", "arbitrary")),
        name="na_bias",
    )(na_rpb.reshape(-1))


def _lat_na_kernel(half, cq_ref, ck_ref, cv_ref, kc_ref, vc_ref, bias_ref, gc_ref,
                   bx_ref, gr_ref, cw_ref, cb_ref, wa_ref, ba_ref, wx_ref, bxg_ref, llam_ref, st_ref, zc_ref, y_ref):
    k0 = half * NA_HALF_KSTART
    cq = cq_ref[...]
    kl_all = ck_ref[k0:k0 + NA_KSPAN, :]
    vl_all = cv_ref[k0:k0 + NA_KSPAN, :]

    def scores(h):
        q = cq[:, h * HD:(h + 1) * HD]
        return [_dot(q, kc_ref[0, 0, h].astype(BF16)),
                _dot_nt(q, kl_all[:, h * HD:(h + 1) * HD]) + bias_ref[0, h, 0]]

    def pv(h, s):
        (e_ctx, e_loc), _ = _exp2_parts(s)
        p_ctx = _dot_nt(e_ctx, _with_ones_t(vc_ref[0, 0, h].astype(BF16)))
        p_loc = _dot(e_loc, _with_ones(vl_all[:, h * HD:(h + 1) * HD]))
        return (p_ctx[:, :HD] + p_loc[:, :HD]) * (1.0 / (p_ctx[:, HD:HD + 1] + p_loc[:, HD:HD + 1]))

    xc = _lru_conv(bx_ref[...], cw_ref[0], cb_ref[0])
    pre = _lru_gates(half, xc, wa_ref, ba_ref, wx_ref, bxg_ref)
    y, _ = _lru_scan(half, xc, pre, llam_ref[0, half:half + 1, :], st_ref[0, 0, half:half + 1, :])
    y_ref[...] = y * gr_ref[...].astype(F32)

    ss = [scores(h) for h in range(C_HEADS)]
    outs = [pv(h, s) for h, s in enumerate(ss)]
    zc_ref[...] = (jnp.concatenate(outs, axis=1) * gc_ref[...].astype(F32)).astype(BF16)


def _lat_na(half, proj, bx, cache_k, cache_v, bias, l, lru_params, state_lru):
    qrow = lambda b: N_CTX // NA_HALF_Q + b * 2 + half
    srow = lambda b: N_CTX // DEC_SEQ + b
    return pl.pallas_call(
        functools.partial(_lat_na_kernel, half),
        grid=(DEC_BATCH,),
        in_specs=[pl.BlockSpec((NA_HALF_Q, BR_W), lambda b: (qrow(b), COL_CQ // BR_W)),
                  pl.BlockSpec((DEC_SEQ, BR_W), lambda b: (srow(b), COL_CK // BR_W)),
                  pl.BlockSpec((DEC_SEQ, BR_W), lambda b: (srow(b), COL_CV // BR_W)),
                  pl.BlockSpec((1, 1, C_HEADS, HD, PAST_LEN), lambda b: (b, l, 0, 0, 0)),
                  pl.BlockSpec((1, 1, C_HEADS, HD, PAST_LEN), lambda b: (b, l, 0, 0, 0)),
                  pl.BlockSpec((1, C_HEADS, 1, NA_HALF_Q, NA_KSPAN), lambda b: (l, 0, half, 0, 0)),
                  pl.BlockSpec((NA_HALF_Q, BR_W), lambda b: (qrow(b), GATE_C)),
                  pl.BlockSpec((DEC_SEQ, BR_W), lambda b: (srow(b), 0)),
                  pl.BlockSpec((DEC_SEQ, BR_W), lambda b: (srow(b), GATE_R))]
                 + _lru_specs(l, lambda b: (l, 0, 0))
                 + [pl.BlockSpec((1, 1, 2, BR_W), lambda b: (b, l, 0, 0))],
        out_specs=[pl.BlockSpec((NA_HALF_Q, BR_W), lambda b: (b, 0)),
                   pl.BlockSpec((DEC_SEQ, BR_W), lambda b: (b, 0))],
        out_shape=[jax.ShapeDtypeStruct((N_LAT // 2, BR_W), BF16), jax.ShapeDtypeStruct((N_LAT, BR_W), F32)],
        compiler_params=_cparams(("arbitrary",)),
        name=f"lat_na{half}_l{l}",
    )(proj, proj, proj, cache_k, cache_v, bias, proj, bx, proj, *lru_params, state_lru)


def _lat_swa_kernel(dq_ref, dk_ref, dv_ref, kc_ref, vc_ref, sink_ref, gd_ref, zd_ref):
    nq = DEC_SEQ // SWA_QB
    rows = D_GROUP * SWA_QB
    dq, dk, dv = dq_ref[...], dk_ref[...], dv_ref[...]
    sink2 = sink_ref[0] * LOG2E
    m_hd = _lane_groups(D_HEADS, DEC_SEQ)
    starts = [min(max(j * SWA_QB - WIN, 0), DEC_SEQ - SWA_SPAN) for j in range(nq)]
    qi = lax.broadcasted_iota(jnp.int32, (rows, SWA_SPAN), 0) % SWA_QB
    ki = lax.broadcasted_iota(jnp.int32, (rows, SWA_SPAN), 1)
    masks = {off: jnp.abs(qi + off - ki) <= WIN for off in sorted({j * SWA_QB - s for j, s in enumerate(starts)})}
    kc_t = jnp.concatenate([kc_ref[0, 0, kv] for kv in range(D_KV) for _ in range(D_GROUP)], axis=0).astype(BF16)
    vc_t = jnp.concatenate([vc_ref[0, 0, kv] for kv in range(D_KV) for _ in range(D_GROUP)], axis=0).astype(BF16)
    feat = lax.broadcasted_iota(jnp.int32, (BR_W, 1), 0) // (D_GROUP * HD)

    qm = [_only(dq, m) for m in m_hd]
    sks = [_sink_col(sink2, kv, SWA_QB) for kv in range(D_KV)]
    w_loc, w_ctx = [], []
    for kv in range(D_KV):
        w_loc.append(_ones_outside(dv, m_hd[kv * D_GROUP] | m_hd[kv * D_GROUP + 1]))
        w_ctx.append(jnp.where(feat == kv, vc_t, jnp.ones_like(vc_t)))

    def scores(c):
        kv, j = divmod(c, nq)
        q2 = jnp.concatenate([qm[kv * D_GROUP + g][j * SWA_QB:(j + 1) * SWA_QB] for g in range(D_GROUP)], axis=0)
        return [_dot(q2, kc_t),
                jnp.where(masks[j * SWA_QB - starts[j]], _dot_nt(q2, dk[starts[j]:starts[j] + SWA_SPAN]), NEG)]

    def pv(c, s):
        kv, j = divmod(c, nq)
        (e_ctx, e_loc), m = _exp2_parts(s, extra=sks[kv])
        p = _dot_nt(e_ctx, w_ctx[kv]) + _dot(e_loc, w_loc[kv][starts[j]:starts[j] + SWA_SPAN])
        return p * (1.0 / (_swap_halves(p) + jnp.exp2(sks[kv] - m)))

    o = _pipelined(D_KV * nq, scores, pv)
    os = [jnp.concatenate([o[kv * nq + j][g * SWA_QB:(g + 1) * SWA_QB] for j in range(nq)], axis=0)
          for kv in range(D_KV) for g in range(D_GROUP)]
    zd_ref[...] = (_pick_heads(os) * gd_ref[...].astype(F32)).astype(BF16)


def _lat_swa(proj, cache_k, cache_v, sink3, l):
    srow = lambda b: (N_CTX // DEC_SEQ + b)
    return pl.pallas_call(
        _lat_swa_kernel,
        grid=(DEC_BATCH,),
        in_specs=[pl.BlockSpec((DEC_SEQ, BR_W), lambda b: (srow(b), COL_DQ // BR_W)),
                  pl.BlockSpec((DEC_SEQ, BR_W), lambda b: (srow(b), COL_DK // BR_W)),
                  pl.BlockSpec((DEC_SEQ, BR_W), lambda b: (srow(b), COL_DV // BR_W)),
                  pl.BlockSpec((1, 1, D_KV, HD, PAST_LEN), lambda b: (b, l, 0, 0, 0)),
                  pl.BlockSpec((1, 1, D_KV, HD, PAST_LEN), lambda b: (b, l, 0, 0, 0)),
                  pl.BlockSpec((1, 1, D_HEADS), lambda b: (l, 0, 0)),
                  pl.BlockSpec((DEC_SEQ, BR_W), lambda b: (srow(b), GATE_D))],
        out_specs=pl.BlockSpec((DEC_SEQ, BR_W), lambda b: (b, 0)),
        out_shape=jax.ShapeDtypeStruct((N_LAT, BR_W), BF16),
        compiler_params=_cparams(("arbitrary",)),
        name=f"lat_swa_l{l}",
    )(proj, proj, proj, cache_k, cache_v, sink3, proj)


def _lru_conv(x, cw, cb):
    seq = x.shape[0]
    x3 = x.reshape(seq // SUBLANES, SUBLANES, BR_W)
    tin = lax.broadcasted_iota(jnp.int32, (1, SUBLANES, 1), 1)
    zero = jnp.zeros((1, SUBLANES, BR_W), F32)

    def at(shift):
        r = pltpu.roll(x3, (-shift) % SUBLANES, 1)
        if shift < 0:
            return jnp.where(tin < -shift, jnp.concatenate([zero, r[:-1]], axis=0), r)
        return jnp.where(tin >= SUBLANES - shift, jnp.concatenate([r[1:], zero], axis=0), r)

    xc = cb + at(-1) * cw[0:1] + x3 * cw[1:2] + at(1) * cw[2:3] + at(2) * cw[3:4]
    return xc.reshape(seq, BR_W)


def _gate_dense(w_ref, d):
    rows = []
    for n in range(B_BLOCKS):
        pieces = [jnp.zeros((B_BLK, n * B_BLK), BF16)] if n else []
        pieces.append(w_ref[0, d, n].astype(BF16))
        if n < B_BLOCKS - 1:
            pieces.append(jnp.zeros((B_BLK, (B_BLOCKS - 1 - n) * B_BLK), BF16))
        rows.append(jnp.concatenate(pieces, axis=1))
    return jnp.concatenate(rows, axis=0)


def _lru_gates(d, xc, wa_ref, ba_ref, wx_ref, bxg_ref):
    xb = xc.astype(BF16)
    return (_dot(xb, _gate_dense(wa_ref, d)) + ba_ref[0, d:d + 1, :],
            _dot(xb, _gate_dense(wx_ref, d)) + bxg_ref[0, d:d + 1, :])


def _lru_scan(d, xc, pre, lam_row, h0):
    seq = xc.shape[0]
    nl = -lam_row
    softplus = jnp.maximum(nl, 0.0) + jnp.log1p(jnp.exp(-jnp.abs(nl)))
    r = jax.nn.sigmoid(pre[0])
    ig = jax.nn.sigmoid(pre[1])
    log_a = -LRU_C * r * softplus
    a = jnp.exp(log_a)
    u = jnp.exp(0.5 * jnp.log(-jnp.tanh(log_a) * (a * a + 1.0))) * (ig * xc)
    nt = seq // SUBLANES
    tin = lax.broadcasted_iota(jnp.int32, (1, SUBLANES, 1), 1)
    a = a.reshape(nt, SUBLANES, BR_W)
    u = u.reshape(nt, SUBLANES, BR_W)
    s = 1
    while s < SUBLANES:
        if d == 0:
            keep = tin >= s
            a_n = jnp.where(keep, pltpu.roll(a, s, 1), 1.0)
            u_n = jnp.where(keep, pltpu.roll(u, s, 1), 0.0)
        else:
            keep = tin < SUBLANES - s
            a_n = jnp.where(keep, pltpu.roll(a, SUBLANES - s, 1), 1.0)
            u_n = jnp.where(keep, pltpu.roll(u, SUBLANES - s, 1), 0.0)
        u = u + a * u_n
        a = a * a_n
        s *= 2
    ys = [None] * nt
    h = h0
    for k in (range(nt) if d == 0 else range(nt - 1, -1, -1)):
        y = u[k] + a[k] * h
        ys[k] = y
        h = y[SUBLANES - 1:, :] if d == 0 else y[:1, :]
    return jnp.concatenate(ys, axis=0), h


def _lru_specs(l, imap):
    wspec = pl.BlockSpec((1, 2, B_BLOCKS, B_BLK, B_BLK), lambda *g: imap(*g) + (0, 0))
    vspec = pl.BlockSpec((1, 2, BR_W), imap)
    return [pl.BlockSpec((1, CONV_W, BR_W), imap), pl.BlockSpec((1, 1, BR_W), imap),
            wspec, vspec, wspec, vspec, vspec]


MERGE_COLS = 512

def _merge_kernel(layer, final, xa_ref, xb_ref, mod_ref, h_ref, zctx_ref, zal_ref, yf_ref, yb_ref,
                  zc0_ref, zc1_ref, zdl_ref, wmg_hbm, bmg_ref, wbo_ref, wo_ref, nf_ref, *refs):
    out_refs, (wmg_s, wbo_s, wo_s, stage, sem) = refs[:-5], refs[-5:]
    i = pl.program_id(0)

    def wmg_copy(chunk, n):
        slot = chunk % 2
        return pltpu.make_async_copy(wmg_hbm.at[layer, pl.ds(chunk * W_ROWS, W_ROWS), n, :],
                                     stage.at[slot, n], sem.at[slot, n])

    @pl.when(i == 0)
    def _():
        for n in range(N_BRANCH):
            wmg_copy(0, n).start()

    @pl.when(i < W_STEPS - 1)
    def _():
        for n in range(N_BRANCH):
            wmg_copy(i + 1, n).start()

    @pl.when(i < W_STEPS)
    def _():
        r0 = pl.multiple_of(i * W_ROWS, W_ROWS)
        for n in range(N_BRANCH):
            wmg_copy(i, n).wait()
            wmg_s[pl.ds(r0, W_ROWS), n * D_MODEL:(n + 1) * D_MODEL] = stage[i % 2, n].astype(BF16)
        _cast_rows(i, wbo_ref, wbo_s)
        _cast_rows(i, wo_ref, wo_s)

    @pl.when(i >= W_STEPS)
    def _():
        t = i - W_STEPS
        is_ctx = t < N_CTX_TILES
        x = jnp.where(is_ctx, xa_ref[...], xb_ref[...])
        gate = mod_ref[0][:, 2 * D_MODEL:]
        h = h_ref[...]
        second_half = (t - N_CTX_TILES) % LAT_TILES_PER_SEQ == 1
        z_lat = [zal_ref[...],
                 (yf_ref[...] + yb_ref[...]).astype(BF16),
                 jnp.where(second_half, zc1_ref[...], zc0_ref[...]),
                 zdl_ref[...]]
        zs = [jnp.where(is_ctx, zctx_ref[:, n * BR_W:(n + 1) * BR_W], z_lat[n]) for n in range(N_BRANCH)]
        bmg = bmg_ref[0]
        cols = []
        for c in range(0, D_MODEL, MERGE_COLS):
            acc = None
            for n in range(N_BRANCH):
                g = jax.nn.sigmoid(_dot(h, wmg_s[:, n * D_MODEL + c:n * D_MODEL + c + MERGE_COLS])
                                   + bmg[n:n + 1, c:c + MERGE_COLS])
                term = g * _dot(zs[n], wbo_s[n * BR_W:(n + 1) * BR_W, c:c + MERGE_COLS])
                acc = term if acc is None else acc + term
            cols.append(acc.astype(BF16))
        merged = jnp.concatenate(cols, axis=1)
        xn = x + gate * _dot(merged, wo_s[...])
        if not final:
            out_refs[0][...] = xn
        else:
            y = xn * lax.rsqrt(jnp.mean(xn * xn, axis=-1, keepdims=True) + EPS) * nf_ref[...]

            @pl.when(t < N_CTX_TILES)
            def _():
                out_refs[0][...] = y

            @pl.when(t >= N_CTX_TILES)
            def _():
                out_refs[1][...] = y


def _merge(final, xa, xb, xb_off, mod3, l, h, z_ctx, za_lat, y_f, y_b, zc_lat, zd_lat,
           w_mg, b_mg, w_bo, w_o, norm_f):
    assert TM == NA_HALF_Q
    row = lambda i: (_tile(i), 0)
    ctx_blk = lambda i: jnp.minimum(_tile(i), N_CTX_TILES - 1)
    lat_blk = lambda i: jnp.maximum(_tile(i) - N_CTX_TILES, 0)
    lat_spec = pl.BlockSpec((TM, BR_W), lambda i: (lat_blk(i), 0))
    half_spec = pl.BlockSpec((TM, BR_W), lambda i: (lat_blk(i) // LAT_TILES_PER_SEQ, 0))
    if final:
        out_specs = [pl.BlockSpec((TM, D_MODEL), lambda i: (ctx_blk(i), 0)),
                     pl.BlockSpec((TM, D_MODEL), lambda i: (lat_blk(i), 0))]
        out_shape = [jax.ShapeDtypeStruct((N_CTX, D_MODEL), F32), jax.ShapeDtypeStruct((N_LAT, D_MODEL), F32)]
    else:
        out_specs = pl.BlockSpec((TM, D_MODEL), row)
        out_shape = jax.ShapeDtypeStruct((N_TOK, D_MODEL), F32)
    wchunk = lambda width: pl.BlockSpec((1, W_ROWS, width), lambda i: (l, _wchunk(i), 0))
    return pl.pallas_call(
        functools.partial(_merge_kernel, l, final),
        grid=(W_STEPS + N_TILES,),
        in_specs=[
            pl.BlockSpec((TM, D_MODEL), lambda i: (ctx_blk(i), 0)),
            pl.BlockSpec((TM, D_MODEL), lambda i: (lat_blk(i) + xb_off, 0)),
            pl.BlockSpec((1, 1, 3 * D_MODEL), lambda i: (l * MOD_ROWS + _mod_row(_tile(i)), 0, 0)),
            pl.BlockSpec((TM, D_MODEL), row),
            pl.BlockSpec((TM, N_BRANCH * BR_W), lambda i: (ctx_blk(i), 0)),
            lat_spec,
            lat_spec, lat_spec,
            half_spec, half_spec,
            lat_spec,
            pl.BlockSpec(memory_space=pl.ANY),
            pl.BlockSpec((1, N_BRANCH, D_MODEL), lambda i: (l, 0, 0)),
            wchunk(D_MODEL),
            wchunk(D_MODEL),
            pl.BlockSpec((1, D_MODEL), lambda i: (0, 0)),
        ],
        out_specs=out_specs,
        out_shape=out_shape,
        scratch_shapes=[pltpu.VMEM((D_MODEL, N_BRANCH * D_MODEL), BF16),
                        pltpu.VMEM((N_BRANCH * BR_W, D_MODEL), BF16),
                        pltpu.VMEM((D_MODEL, D_MODEL), BF16),
                        pltpu.VMEM((2, N_BRANCH, W_ROWS, D_MODEL), F32),
                        pltpu.SemaphoreType.DMA((2, N_BRANCH))],
        compiler_params=_cparams(("arbitrary",)),
        name=f"merge_l{l}",
    )(xa, xb, mod3, h, z_ctx, za_lat, y_f, y_b, zc_lat[0], zc_lat[1], zd_lat,
      w_mg, b_mg,
      w_bo.reshape(DEPTH, N_BRANCH * BR_W, D_MODEL), w_o, norm_f.reshape(1, D_MODEL))


def kernel(x_prompt, x_sample, cache_diff_k, cache_diff_v, cache_na_k, cache_na_v, cache_swa_k, cache_swa_v,
           state_lru, c, c_ctx, norm_g, w_ada, b_ada, w_in, diff_lambda, diff_norm_g, conv_w, conv_b,
           lru_wa, lru_ba, lru_wx, lru_bx, lru_lam, na_rpb, swa_sink, w_mg, b_mg, w_bo, w_o, norm_f):
    tab_a, tab_d = jnp.asarray(_rope_tables(A_QK)), jnp.asarray(_rope_tables(HD))

    cvecs = jnp.concatenate([c_ctx[None, :], c, jnp.zeros((MOD_ROWS - 1 - DEC_BATCH, D_MODEL), F32)], axis=0)
    mod3 = _modulation(cvecs, w_ada, b_ada).reshape(DEPTH * MOD_ROWS, 1, 3 * D_MODEL)
    bias = _na_bias(na_rpb)
    norm_g3 = norm_g.reshape(DEPTH, 1, D_MODEL)
    dg3 = diff_norm_g.reshape(DEPTH, 1, HD)
    dg4 = jnp.tile(diff_norm_g, (1, A_HEADS)).reshape(DEPTH, 1, BR_W)
    sink3 = swa_sink.reshape(DEPTH, 1, D_HEADS)
    conv_b3 = conv_b.reshape(DEPTH, 1, BR_W)
    past = [jnp.swapaxes(t, -1, -2) for t in
            (cache_diff_k, cache_diff_v, cache_na_k, cache_na_v, cache_swa_k, cache_swa_v)]

    xa = x_prompt.reshape(N_CTX, D_MODEL)
    xb = x_sample.reshape(N_LAT, D_MODEL)
    xb_off = 0
    caches = states = None
    y_p = y_s = None
    for l in range(DEPTH):
        h, proj, bx, caches = _inproj(xa, xb, xb_off, mod3, l, norm_g3, w_in, tab_a, tab_d, caches)

        lru_params = (conv_w, conv_b3, lru_wa, lru_ba, lru_wx, lru_bx, lru_lam)
        z_ctx, states = _ctx_attn(proj, bx, l, diff_lambda, dg4, sink3, lru_params, states)
        za_lat = _lat_diff(proj, past[0], past[1], l, diff_lambda, dg3)
        zc0, y_f = _lat_na(0, proj, bx, past[2], past[3], bias, l, lru_params, state_lru)
        zc1, y_b = _lat_na(1, proj, bx, past[2], past[3], bias, l, lru_params, state_lru)
        zd_lat = _lat_swa(proj, past[4], past[5], sink3, l)

        final = l == DEPTH - 1
        out = _merge(final, xa, xb, xb_off, mod3, l, h, z_ctx, za_lat, y_f, y_b, (zc0, zc1), zd_lat,
                     w_mg, b_mg, w_bo, w_o, norm_f)
        if final:
            y_p, y_s = out
        else:
            xa = xb = out
            xb_off = N_CTX_TILES

    new_caches = [jnp.swapaxes(t, -1, -2) for t in caches]
    return (y_p.reshape(BATCH, SEQ, D_MODEL), y_s.reshape(DEC_BATCH, DEC_SEQ, D_MODEL), *new_caches, states)
```
